```python
import jax, jax.numpy as jnp
from jax import lax
import numpy as np

D_MODEL = 1024
BATCH = 8
SEQ = 8192
DEPTH = 1

GDN_H = 4
GDN_DK = 128
GDN_DV = 128
GDN_W = GDN_H * GDN_DV
CONV_W = 4
CHUNK = 64
SB_H = 4
SB_D = 128
SB_W = SB_H * SB_D
Q_BLOCK = 128
MEM_H = 4
MEM_D = 128
MEM_W = MEM_H * MEM_D
N_MEM = 256
N_BRANCH = 3
D_FF = 4 * D_MODEL
EPS = 1e-6
IN_SIZES = (3 * GDN_W, GDN_W, GDN_H, GDN_H, 3 * SB_W, MEM_W, N_BRANCH * D_MODEL)
D_IN = sum(IN_SIZES)

kernel_name = "hybrid_gdn_stickbreak_memxattn_gated_merge"


def rms_norm(x, g):
    xf = x.astype(jnp.float32)
    y = xf * lax.rsqrt(jnp.mean(xf * xf, axis=-1, keepdims=True) + EPS)
    return (y * g.astype(jnp.float32)).astype(x.dtype)


def l2norm(x):
    return x * lax.rsqrt(jnp.sum(x * x, axis=-1, keepdims=True) + EPS)


def causal_dwconv(x, w):
    c = x.shape[-1]
    return lax.conv_general_dilated(
        x, w[:, None, :].astype(x.dtype), window_strides=(1,),
        padding=[(w.shape[0] - 1, 0)], dimension_numbers=("NWC", "WIO", "NWC"),
        feature_group_count=c)


def gated_delta_rule_chunked(q, k, v, g, beta):
    b, s, h, dk = q.shape
    dv = v.shape[-1]
    n = s // CHUNK

    def to_chunks(t):
        return jnp.moveaxis(t.reshape(b, n, CHUNK, h, *t.shape[3:]), 3, 1)

    qc, kc, vc, gc, bc = (to_chunks(t) for t in (q, k, v, g, beta))
    gc = jnp.cumsum(gc, axis=-1)
    idx = jnp.arange(CHUNK)
    causal = idx[:, None] >= idx[None, :]
    strict = idx[:, None] > idx[None, :]
    decay = jnp.exp(jnp.where(causal, gc[..., :, None] - gc[..., None, :], -jnp.inf))
    kb = kc * bc[..., None]
    vb = vc * bc[..., None]
    lower = jnp.where(strict, jnp.einsum("bhncd,bhnsd->bhncs", kb, kc) * decay, 0.0)
    eye = jnp.eye(CHUNK, dtype=jnp.float32)
    t_inv = lax.linalg.triangular_solve(eye + lower, jnp.broadcast_to(eye, lower.shape),
                                        left_side=True, lower=True, unit_diagonal=True)
    u = jnp.einsum("bhncs,bhnsv->bhncv", t_inv, vb)
    w = jnp.einsum("bhncs,bhnsk->bhnck", t_inv, kb * jnp.exp(gc)[..., None])
    a_qk = jnp.where(causal, jnp.einsum("bhncd,bhnsd->bhncs", qc, kc) * decay, 0.0)
    q_dec = qc * jnp.exp(gc)[..., None]
    k_dec = kc * jnp.exp(gc[..., -1:] - gc)[..., None]
    g_last = jnp.exp(gc[..., -1])
    xs = tuple(jnp.moveaxis(t, 2, 0) for t in (u, w, a_qk, q_dec, k_dec, g_last))

    def step(state, inp):
        u_i, w_i, a_i, qd_i, kd_i, gl_i = inp
        v_new = u_i - jnp.einsum("bhck,bhkv->bhcv", w_i, state)
        o_i = (jnp.einsum("bhck,bhkv->bhcv", qd_i, state)
               + jnp.einsum("bhcs,bhsv->bhcv", a_i, v_new))
        state = state * gl_i[..., None, None] + jnp.einsum("bhck,bhcv->bhkv", kd_i, v_new)
        return state, o_i

    s0 = jnp.zeros((b, h, dk, dv), jnp.float32)
    _, o = lax.scan(step, s0, xs)
    o = jnp.moveaxis(jnp.moveaxis(o, 0, 2), 1, 3)
    return o.reshape(b, s, h, dv)


def stick_breaking_attention(q, k, v):
    s = q.shape[2]
    scale = q.shape[-1] ** -0.5
    outs = []
    for blk in range(s // Q_BLOCK):
        q0 = blk * Q_BLOCK
        kl = q0 + Q_BLOCK
        z = jnp.einsum("bhqd,bhkd->bhqk", q[:, :, q0:kl], k[:, :, :kl]) * scale
        mask = jnp.arange(kl)[None, :] < (q0 + jnp.arange(Q_BLOCK))[:, None]
        log_1mb = jnp.where(mask, jax.nn.log_sigmoid(-z), 0.0)
        between = lax.cumsum(log_1mb, axis=3, reverse=True) - log_1mb
        att = jnp.where(mask, jnp.exp(jax.nn.log_sigmoid(z) + between), 0.0)
        outs.append(jnp.einsum("bhqk,bhkd->bhqd", att, v[:, :, :kl]))
    return jnp.concatenate(outs, axis=2)


def _fwd_setup_inputs(seed: int = 0) -> dict:
    key = jax.random.key(seed)
    ks = jax.random.split(key, 24)
    f32 = jnp.float32
    nrm = lambda k, shape, scale: jax.random.normal(k, shape, f32) * scale
    gain = lambda k, n: 1.0 + 0.02 * jax.random.normal(k, (DEPTH, n), f32)
    dt = jnp.exp(jax.random.uniform(ks[5], (DEPTH, GDN_H), f32, np.log(1e-3), np.log(1e-1)))
    return {
        "x": nrm(ks[0], (BATCH, SEQ, D_MODEL), 1.0),
        "mem": nrm(ks[1], (BATCH, N_MEM, D_MODEL), 1.0),
        "norm1_g": gain(ks[2], D_MODEL),
        "w_in": nrm(ks[3], (DEPTH, D_MODEL, D_IN), D_MODEL ** -0.5),
        "conv_w": nrm(ks[4], (DEPTH, CONV_W, 3 * GDN_W), CONV_W ** -0.5),
        "a_log": jnp.log(jax.random.uniform(ks[6], (DEPTH, GDN_H), f32, 1.0, 16.0)),
        "dt_bias": jnp.log(jnp.expm1(dt)),
        "gdn_norm_g": gain(ks[7], GDN_DV),
        "sb_q_norm_g": gain(ks[8], SB_D),
        "sb_k_norm_g": gain(ks[9], SB_D),
        "mem_norm_g": gain(ks[10], D_MODEL),
        "w_mem_kv": nrm(ks[11], (DEPTH, D_MODEL, 2 * MEM_W), D_MODEL ** -0.5),
        "mem_q_norm_g": gain(ks[12], MEM_D),
        "mem_k_norm_g": gain(ks[13], MEM_D),
        "w_br_gdn": nrm(ks[14], (DEPTH, GDN_W, D_MODEL), GDN_W ** -0.5),
        "w_br_sb": nrm(ks[15], (DEPTH, SB_W, D_MODEL), SB_W ** -0.5),
        "w_br_mem": nrm(ks[16], (DEPTH, MEM_W, D_MODEL), MEM_W ** -0.5),
        "w_o": nrm(ks[17], (DEPTH, D_MODEL, D_MODEL), D_MODEL ** -0.5),
        "norm2_g": gain(ks[18], D_MODEL),
        "w_up": nrm(ks[19], (DEPTH, D_MODEL, D_FF), D_MODEL ** -0.5),
        "w_down": nrm(ks[20], (DEPTH, D_FF, D_MODEL), D_FF ** -0.5),
    }


def _fwd_reference(x, mem, norm1_g, w_in, conv_w, a_log, dt_bias, gdn_norm_g, sb_q_norm_g, sb_k_norm_g,
              mem_norm_g, w_mem_kv, mem_q_norm_g, mem_k_norm_g, w_br_gdn, w_br_sb, w_br_mem, w_o,
              norm2_g, w_up, w_down):
    b, s, _ = x.shape
    f32 = jnp.float32
    splits = np.cumsum(IN_SIZES)[:-1].tolist()
    for l in range(DEPTH):
        h = rms_norm(x, norm1_g[l])
        proj = h @ w_in[l]
        gdn_qkv, gdn_z, gdn_a, gdn_b, sb_qkv, mem_q, gate_logits = jnp.split(proj, splits, axis=-1)

        gdn_qkv = jax.nn.silu(causal_dwconv(gdn_qkv, conv_w[l])).astype(f32)
        gq, gk, gv = jnp.split(gdn_qkv, 3, axis=-1)
        gq = l2norm(gq.reshape(b, s, GDN_H, GDN_DK)) * (GDN_DK ** -0.5)
        gk = l2norm(gk.reshape(b, s, GDN_H, GDN_DK))
        gv = gv.reshape(b, s, GDN_H, GDN_DV)
        beta = jax.nn.sigmoid(gdn_b.astype(f32))
        g = -jnp.exp(a_log[l].astype(f32)) * jax.nn.softplus(gdn_a.astype(f32) + dt_bias[l].astype(f32))
        o_gdn = gated_delta_rule_chunked(gq, gk, gv, g, beta)
        o_gdn = rms_norm(o_gdn, gdn_norm_g[l]) * jax.nn.silu(gdn_z.astype(f32).reshape(b, s, GDN_H, GDN_DV))
        y_gdn = o_gdn.reshape(b, s, GDN_W).astype(x.dtype) @ w_br_gdn[l]

        sq, sk, sv = jnp.split(sb_qkv, 3, axis=-1)
        sq = rms_norm(sq.reshape(b, s, SB_H, SB_D), sb_q_norm_g[l])
        sk = rms_norm(sk.reshape(b, s, SB_H, SB_D), sb_k_norm_g[l])
        sv = sv.reshape(b, s, SB_H, SB_D)
        to_bhsd = lambda t: jnp.transpose(t, (0, 2, 1, 3)).astype(f32)
        o_sb = stick_breaking_attention(to_bhsd(sq), to_bhsd(sk), to_bhsd(sv))
        y_sb = jnp.transpose(o_sb, (0, 2, 1, 3)).reshape(b, s, SB_W).astype(x.dtype) @ w_br_sb[l]

        kv = rms_norm(mem, mem_norm_g[l]) @ w_mem_kv[l]
        km, vm = jnp.split(kv, 2, axis=-1)
        km = rms_norm(km.reshape(b, N_MEM, MEM_H, MEM_D), mem_k_norm_g[l]).astype(f32)
        vm = vm.reshape(b, N_MEM, MEM_H, MEM_D).astype(f32)
        qm = rms_norm(mem_q.reshape(b, s, MEM_H, MEM_D), mem_q_norm_g[l]).astype(f32)
        p = jax.nn.softmax(jnp.einsum("bshd,bmhd->bhsm", qm, km) * (MEM_D ** -0.5), axis=-1)
        o_mem = jnp.einsum("bhsm,bmhd->bshd", p, vm).reshape(b, s, MEM_W).astype(x.dtype)
        y_mem = o_mem @ w_br_mem[l]

        g_gdn, g_sb, g_mem = jnp.split(jax.nn.sigmoid(gate_logits), N_BRANCH, axis=-1)
        mix = g_gdn * y_gdn + g_sb * y_sb + g_mem * y_mem
        x = x + mix @ w_o[l]

        h2 = rms_norm(x, norm2_g[l])
        x = x + jnp.square(jax.nn.relu(h2 @ w_up[l])) @ w_down[l]
    return x


import jax as _jax
import jax.numpy as _jnp

TWIN_FORMAT = 'train_step'
FWD_PARAMS = ['x', 'mem', 'norm1_g', 'w_in', 'conv_w', 'a_log', 'dt_bias', 'gdn_norm_g', 'sb_q_norm_g', 'sb_k_norm_g', 'mem_norm_g', 'w_mem_kv', 'mem_q_norm_g', 'mem_k_norm_g', 'w_br_gdn', 'w_br_sb', 'w_br_mem', 'w_o', 'norm2_g', 'w_up', 'w_down']
TWIN_WEIGHTS = ['norm1_g', 'w_in', 'conv_w', 'a_log', 'dt_bias', 'gdn_norm_g', 'sb_q_norm_g', 'sb_k_norm_g', 'mem_norm_g', 'w_mem_kv', 'mem_q_norm_g', 'mem_k_norm_g', 'w_br_gdn', 'w_br_sb', 'w_br_mem', 'w_o', 'norm2_g', 'w_up', 'w_down']
TWIN_DIFF_INPUT = 'x'
TWIN_INPUTS = ['x', 'mem', 'norm1_g', 'w_in', 'conv_w', 'a_log', 'dt_bias', 'gdn_norm_g', 'sb_q_norm_g', 'sb_k_norm_g', 'mem_norm_g', 'w_mem_kv', 'mem_q_norm_g', 'mem_k_norm_g', 'w_br_gdn', 'w_br_sb', 'w_br_mem', 'w_o', 'norm2_g', 'w_up', 'w_down', 'loss_target', 'm_norm1_g', 'm_w_in', 'm_conv_w', 'm_a_log', 'm_dt_bias', 'm_gdn_norm_g', 'm_sb_q_norm_g', 'm_sb_k_norm_g', 'm_mem_norm_g', 'm_w_mem_kv', 'm_mem_q_norm_g', 'm_mem_k_norm_g', 'm_w_br_gdn', 'm_w_br_sb', 'm_w_br_mem', 'm_w_o', 'm_norm2_g', 'm_w_up', 'm_w_down', 'v_norm1_g', 'v_w_in', 'v_conv_w', 'v_a_log', 'v_dt_bias', 'v_gdn_norm_g', 'v_sb_q_norm_g', 'v_sb_k_norm_g', 'v_mem_norm_g', 'v_w_mem_kv', 'v_mem_q_norm_g', 'v_mem_k_norm_g', 'v_w_br_gdn', 'v_w_br_sb', 'v_w_br_mem', 'v_w_o', 'v_norm2_g', 'v_w_up', 'v_w_down']
TWIN_OUTPUTS = ['loss', 'grad_x', 'grad_norm1_g', 'grad_w_in', 'grad_conv_w', 'grad_a_log', 'grad_dt_bias', 'grad_gdn_norm_g', 'grad_sb_q_norm_g', 'grad_sb_k_norm_g', 'grad_mem_norm_g', 'grad_w_mem_kv', 'grad_mem_q_norm_g', 'grad_mem_k_norm_g', 'grad_w_br_gdn', 'grad_w_br_sb', 'grad_w_br_mem', 'grad_w_o', 'grad_norm2_g', 'grad_w_up', 'grad_w_down', 'delta_norm1_g', 'delta_w_in', 'delta_conv_w', 'delta_a_log', 'delta_dt_bias', 'delta_gdn_norm_g', 'delta_sb_q_norm_g', 'delta_sb_k_norm_g', 'delta_mem_norm_g', 'delta_w_mem_kv', 'delta_mem_q_norm_g', 'delta_mem_k_norm_g', 'delta_w_br_gdn', 'delta_w_br_sb', 'delta_w_br_mem', 'delta_w_o', 'delta_norm2_g', 'delta_w_up', 'delta_w_down', 'new_m_norm1_g', 'new_m_w_in', 'new_m_conv_w', 'new_m_a_log', 'new_m_dt_bias', 'new_m_gdn_norm_g', 'new_m_sb_q_norm_g', 'new_m_sb_k_norm_g', 'new_m_mem_norm_g', 'new_m_w_mem_kv', 'new_m_mem_q_norm_g', 'new_m_mem_k_norm_g', 'new_m_w_br_gdn', 'new_m_w_br_sb', 'new_m_w_br_mem', 'new_m_w_o', 'new_m_norm2_g', 'new_m_w_up', 'new_m_w_down', 'new_v_norm1_g', 'new_v_w_in', 'new_v_conv_w', 'new_v_a_log', 'new_v_dt_bias', 'new_v_gdn_norm_g', 'new_v_sb_q_norm_g', 'new_v_sb_k_norm_g', 'new_v_mem_norm_g', 'new_v_w_mem_kv', 'new_v_mem_q_norm_g', 'new_v_mem_k_norm_g', 'new_v_w_br_gdn', 'new_v_w_br_sb', 'new_v_w_br_mem', 'new_v_w_o', 'new_v_norm2_g', 'new_v_w_up', 'new_v_w_down']
TWIN_LEAF_KINDS = {'loss': 'loss', 'grad_x': 'grad_x', 'grad_norm1_g': 'grad_w', 'grad_w_in': 'grad_w', 'grad_conv_w': 'grad_w', 'grad_a_log': 'grad_w', 'grad_dt_bias': 'grad_w', 'grad_gdn_norm_g': 'grad_w', 'grad_sb_q_norm_g': 'grad_w', 'grad_sb_k_norm_g': 'grad_w', 'grad_mem_norm_g': 'grad_w', 'grad_w_mem_kv': 'grad_w', 'grad_mem_q_norm_g': 'grad_w', 'grad_mem_k_norm_g': 'grad_w', 'grad_w_br_gdn': 'grad_w', 'grad_w_br_sb': 'grad_w', 'grad_w_br_mem': 'grad_w', 'grad_w_o': 'grad_w', 'grad_norm2_g': 'grad_w', 'grad_w_up': 'grad_w', 'grad_w_down': 'grad_w', 'delta_norm1_g': 'delta_w', 'delta_w_in': 'delta_w', 'delta_conv_w': 'delta_w', 'delta_a_log': 'delta_w', 'delta_dt_bias': 'delta_w', 'delta_gdn_norm_g': 'delta_w', 'delta_sb_q_norm_g': 'delta_w', 'delta_sb_k_norm_g': 'delta_w', 'delta_mem_norm_g': 'delta_w', 'delta_w_mem_kv': 'delta_w', 'delta_mem_q_norm_g': 'delta_w', 'delta_mem_k_norm_g': 'delta_w', 'delta_w_br_gdn': 'delta_w', 'delta_w_br_sb': 'delta_w', 'delta_w_br_mem': 'delta_w', 'delta_w_o': 'delta_w', 'delta_norm2_g': 'delta_w', 'delta_w_up': 'delta_w', 'delta_w_down': 'delta_w', 'new_m_norm1_g': 'new_m', 'new_m_w_in': 'new_m', 'new_m_conv_w': 'new_m', 'new_m_a_log': 'new_m', 'new_m_dt_bias': 'new_m', 'new_m_gdn_norm_g': 'new_m', 'new_m_sb_q_norm_g': 'new_m', 'new_m_sb_k_norm_g': 'new_m', 'new_m_mem_norm_g': 'new_m', 'new_m_w_mem_kv': 'new_m', 'new_m_mem_q_norm_g': 'new_m', 'new_m_mem_k_norm_g': 'new_m', 'new_m_w_br_gdn': 'new_m', 'new_m_w_br_sb': 'new_m', 'new_m_w_br_mem': 'new_m', 'new_m_w_o': 'new_m', 'new_m_norm2_g': 'new_m', 'new_m_w_up': 'new_m', 'new_m_w_down': 'new_m', 'new_v_norm1_g': 'new_v', 'new_v_w_in': 'new_v', 'new_v_conv_w': 'new_v', 'new_v_a_log': 'new_v', 'new_v_dt_bias': 'new_v', 'new_v_gdn_norm_g': 'new_v', 'new_v_sb_q_norm_g': 'new_v', 'new_v_sb_k_norm_g': 'new_v', 'new_v_mem_norm_g': 'new_v', 'new_v_w_mem_kv': 'new_v', 'new_v_mem_q_norm_g': 'new_v', 'new_v_mem_k_norm_g': 'new_v', 'new_v_w_br_gdn': 'new_v', 'new_v_w_br_sb': 'new_v', 'new_v_w_br_mem': 'new_v', 'new_v_w_o': 'new_v', 'new_v_norm2_g': 'new_v', 'new_v_w_up': 'new_v', 'new_v_w_down': 'new_v'}


def _forward(args):
    return _fwd_reference(*[args[k] for k in FWD_PARAMS])


def _output_shape():
    def fwd():
        inp = _fwd_setup_inputs(0)
        return _fwd_reference(*[inp[k] for k in FWD_PARAMS])
    out = _jax.eval_shape(fwd)
    return out.shape, out.dtype

N_MICROBATCH = 1
ADAM_LR = 0.001
ADAM_B1 = 0.9
ADAM_B2 = 0.999
ADAM_EPS = 1e-08
ADAM_WD = 0.01
ADAM_STEP = 10
PER_EXAMPLE_BATCH_AXIS = {'x': 0, 'mem': 0, 'loss_target': 0}
SHARED_INPUTS = []
_WEIGHT_DTYPES = {'norm1_g': _jnp.float32, 'w_in': _jnp.float32, 'conv_w': _jnp.float32, 'a_log': _jnp.float32, 'dt_bias': _jnp.float32, 'gdn_norm_g': _jnp.float32, 'sb_q_norm_g': _jnp.float32, 'sb_k_norm_g': _jnp.float32, 'mem_norm_g': _jnp.float32, 'w_mem_kv': _jnp.float32, 'mem_q_norm_g': _jnp.float32, 'mem_k_norm_g': _jnp.float32, 'w_br_gdn': _jnp.float32, 'w_br_sb': _jnp.float32, 'w_br_mem': _jnp.float32, 'w_o': _jnp.float32, 'norm2_g': _jnp.float32, 'w_up': _jnp.float32, 'w_down': _jnp.float32}
MOMENT_SCALE = {'norm1_g': 1.639488e+01, 'w_in': 3.535434e-01, 'conv_w': 1.516063e+00, 'a_log': 6.015956e+01, 'dt_bias': 5.780118e+01, 'gdn_norm_g': 5.196291e+01, 'sb_q_norm_g': 9.485307e+00, 'sb_k_norm_g': 9.479232e+00, 'mem_norm_g': 5.897166e-01, 'w_mem_kv': 5.726209e-01, 'mem_q_norm_g': 1.415108e+00, 'mem_k_norm_g': 1.423678e+00, 'w_br_gdn': 2.840447e+00, 'w_br_sb': 5.984441e-01, 'w_br_mem': 6.167417e-01, 'w_o': 2.932370e+00, 'norm2_g': 1.907518e+02, 'w_up': 1.868082e+00, 'w_down': 1.595201e+01}


def _to_microbatches(a, axis):
    t = _jnp.moveaxis(a, axis, 0)
    t = t.reshape((N_MICROBATCH, t.shape[0] // N_MICROBATCH) + t.shape[1:])
    return _jnp.moveaxis(t, 1, axis + 1)


def setup_inputs(seed: int = 0) -> dict:
    inp = _fwd_setup_inputs(seed)
    key = _jax.random.fold_in(_jax.random.key(seed), 7919)
    shape, _ = _output_shape()
    out = dict(inp)
    out["loss_target"] = _jax.random.normal(_jax.random.fold_in(key, 0), shape, _jnp.float32)
    for i, name in enumerate(TWIN_WEIGHTS):
        w = inp[name].astype(_jnp.float32)
        if MOMENT_SCALE is None:
            s = _jnp.sqrt(_jnp.mean(_jnp.square(w)) + 1e-30)
        else:
            s = MOMENT_SCALE[name]
        km, kv = _jax.random.split(_jax.random.fold_in(key, i + 1))
        out[name] = w
        out["m_" + name] = s * _jax.random.normal(km, w.shape, _jnp.float32)
        out["v_" + name] = (s * s) * _jax.random.uniform(kv, w.shape, _jnp.float32, 0.5, 1.5)
    if N_MICROBATCH > 1:
        for name, axis in PER_EXAMPLE_BATCH_AXIS.items():
            out[name] = _to_microbatches(out[name], axis)
    return {'x': out['x'], 'mem': out['mem'], 'norm1_g': out['norm1_g'], 'w_in': out['w_in'], 'conv_w': out['conv_w'], 'a_log': out['a_log'], 'dt_bias': out['dt_bias'], 'gdn_norm_g': out['gdn_norm_g'], 'sb_q_norm_g': out['sb_q_norm_g'], 'sb_k_norm_g': out['sb_k_norm_g'], 'mem_norm_g': out['mem_norm_g'], 'w_mem_kv': out['w_mem_kv'], 'mem_q_norm_g': out['mem_q_norm_g'], 'mem_k_norm_g': out['mem_k_norm_g'], 'w_br_gdn': out['w_br_gdn'], 'w_br_sb': out['w_br_sb'], 'w_br_mem': out['w_br_mem'], 'w_o': out['w_o'], 'norm2_g': out['norm2_g'], 'w_up': out['w_up'], 'w_down': out['w_down'], 'loss_target': out['loss_target'], 'm_norm1_g': out['m_norm1_g'], 'm_w_in': out['m_w_in'], 'm_conv_w': out['m_conv_w'], 'm_a_log': out['m_a_log'], 'm_dt_bias': out['m_dt_bias'], 'm_gdn_norm_g': out['m_gdn_norm_g'], 'm_sb_q_norm_g': out['m_sb_q_norm_g'], 'm_sb_k_norm_g': out['m_sb_k_norm_g'], 'm_mem_norm_g': out['m_mem_norm_g'], 'm_w_mem_kv': out['m_w_mem_kv'], 'm_mem_q_norm_g': out['m_mem_q_norm_g'], 'm_mem_k_norm_g': out['m_mem_k_norm_g'], 'm_w_br_gdn': out['m_w_br_gdn'], 'm_w_br_sb': out['m_w_br_sb'], 'm_w_br_mem': out['m_w_br_mem'], 'm_w_o': out['m_w_o'], 'm_norm2_g': out['m_norm2_g'], 'm_w_up': out['m_w_up'], 'm_w_down': out['m_w_down'], 'v_norm1_g': out['v_norm1_g'], 'v_w_in': out['v_w_in'], 'v_conv_w': out['v_conv_w'], 'v_a_log': out['v_a_log'], 'v_dt_bias': out['v_dt_bias'], 'v_gdn_norm_g': out['v_gdn_norm_g'], 'v_sb_q_norm_g': out['v_sb_q_norm_g'], 'v_sb_k_norm_g': out['v_sb_k_norm_g'], 'v_mem_norm_g': out['v_mem_norm_g'], 'v_w_mem_kv': out['v_w_mem_kv'], 'v_mem_q_norm_g': out['v_mem_q_norm_g'], 'v_mem_k_norm_g': out['v_mem_k_norm_g'], 'v_w_br_gdn': out['v_w_br_gdn'], 'v_w_br_sb': out['v_w_br_sb'], 'v_w_br_mem': out['v_w_br_mem'], 'v_w_o': out['v_w_o'], 'v_norm2_g': out['v_norm2_g'], 'v_w_up': out['v_w_up'], 'v_w_down': out['v_w_down']}


def _loss(weights, diff, rest, loss_target):
    with _jax.named_scope("forward"):
        args = {**rest, TWIN_DIFF_INPUT: diff, **{k: w.astype(_WEIGHT_DTYPES[k]) for k, w in weights.items()}}
        y = _forward(args)
    with _jax.named_scope("loss_head"):
        err = _jnp.square(y.astype(_jnp.float32) - loss_target)
        return 0.5 * _jnp.sum(_jnp.mean(err, axis=-1)) if err.ndim else 0.5 * err


def _adamw(w, g, m, v):
    m = ADAM_B1 * m + (1.0 - ADAM_B1) * g
    v = ADAM_B2 * v + (1.0 - ADAM_B2) * _jnp.square(g)
    m_hat = m / (1.0 - ADAM_B1 ** ADAM_STEP)
    v_hat = v / (1.0 - ADAM_B2 ** ADAM_STEP)
    delta = -ADAM_LR * (m_hat / (_jnp.sqrt(v_hat) + ADAM_EPS) + ADAM_WD * w)
    return delta, m, v


def reference(x, mem, norm1_g, w_in, conv_w, a_log, dt_bias, gdn_norm_g, sb_q_norm_g, sb_k_norm_g, mem_norm_g, w_mem_kv, mem_q_norm_g, mem_k_norm_g, w_br_gdn, w_br_sb, w_br_mem, w_o, norm2_g, w_up, w_down, loss_target, m_norm1_g, m_w_in, m_conv_w, m_a_log, m_dt_bias, m_gdn_norm_g, m_sb_q_norm_g, m_sb_k_norm_g, m_mem_norm_g, m_w_mem_kv, m_mem_q_norm_g, m_mem_k_norm_g, m_w_br_gdn, m_w_br_sb, m_w_br_mem, m_w_o, m_norm2_g, m_w_up, m_w_down, v_norm1_g, v_w_in, v_conv_w, v_a_log, v_dt_bias, v_gdn_norm_g, v_sb_q_norm_g, v_sb_k_norm_g, v_mem_norm_g, v_w_mem_kv, v_mem_q_norm_g, v_mem_k_norm_g, v_w_br_gdn, v_w_br_sb, v_w_br_mem, v_w_o, v_norm2_g, v_w_up, v_w_down):
    given = dict(x=x, mem=mem, norm1_g=norm1_g, w_in=w_in, conv_w=conv_w, a_log=a_log, dt_bias=dt_bias, gdn_norm_g=gdn_norm_g, sb_q_norm_g=sb_q_norm_g, sb_k_norm_g=sb_k_norm_g, mem_norm_g=mem_norm_g, w_mem_kv=w_mem_kv, mem_q_norm_g=mem_q_norm_g, mem_k_norm_g=mem_k_norm_g, w_br_gdn=w_br_gdn, w_br_sb=w_br_sb, w_br_mem=w_br_mem, w_o=w_o, norm2_g=norm2_g, w_up=w_up, w_down=w_down, loss_target=loss_target, m_norm1_g=m_norm1_g, m_w_in=m_w_in, m_conv_w=m_conv_w, m_a_log=m_a_log, m_dt_bias=m_dt_bias, m_gdn_norm_g=m_gdn_norm_g, m_sb_q_norm_g=m_sb_q_norm_g, m_sb_k_norm_g=m_sb_k_norm_g, m_mem_norm_g=m_mem_norm_g, m_w_mem_kv=m_w_mem_kv, m_mem_q_norm_g=m_mem_q_norm_g, m_mem_k_norm_g=m_mem_k_norm_g, m_w_br_gdn=m_w_br_gdn, m_w_br_sb=m_w_br_sb, m_w_br_mem=m_w_br_mem, m_w_o=m_w_o, m_norm2_g=m_norm2_g, m_w_up=m_w_up, m_w_down=m_w_down, v_norm1_g=v_norm1_g, v_w_in=v_w_in, v_conv_w=v_conv_w, v_a_log=v_a_log, v_dt_bias=v_dt_bias, v_gdn_norm_g=v_gdn_norm_g, v_sb_q_norm_g=v_sb_q_norm_g, v_sb_k_norm_g=v_sb_k_norm_g, v_mem_norm_g=v_mem_norm_g, v_w_mem_kv=v_w_mem_kv, v_mem_q_norm_g=v_mem_q_norm_g, v_mem_k_norm_g=v_mem_k_norm_g, v_w_br_gdn=v_w_br_gdn, v_w_br_sb=v_w_br_sb, v_w_br_mem=v_w_br_mem, v_w_o=v_w_o, v_norm2_g=v_norm2_g, v_w_up=v_w_up, v_w_down=v_w_down)
    weights = {n: given[n] for n in TWIN_WEIGHTS}
    shared = {n: given[n] for n in SHARED_INPUTS}
    per_example = {n: given[n] for n in ['x', 'mem']}
    grad_fn = _jax.value_and_grad(_loss, argnums=(0, 1))

    def one_microbatch(ex, loss_target):
        ex = dict(ex)
        diff = ex.pop(TWIN_DIFF_INPUT)
        return grad_fn(weights, diff, {**shared, **ex}, loss_target)

    if N_MICROBATCH == 1:
        loss, (grad_w, grad_x) = one_microbatch(per_example, given["loss_target"])
    else:
        def body(carry, xs):
            loss_sum, grad_sum = carry
            l_k, (gw_k, gx_k) = one_microbatch(xs[0], xs[1])
            with _jax.named_scope("update"):
                return (loss_sum + l_k, _jax.tree.map(_jnp.add, grad_sum, gw_k)), gx_k

        init = (_jnp.zeros((), _jnp.float32), _jax.tree.map(_jnp.zeros_like, weights))
        (loss, grad_w), grad_x = _jax.lax.scan(body, init, (per_example, given["loss_target"]))
    with _jax.named_scope("update"):
        delta_w, new_m, new_v = {}, {}, {}
        for n in TWIN_WEIGHTS:
            delta_w[n], new_m[n], new_v[n] = _adamw(weights[n], grad_w[n], given["m_" + n], given["v_" + n])
    return (loss, grad_x, *[grad_w[n] for n in TWIN_WEIGHTS], *[delta_w[n] for n in TWIN_WEIGHTS],
            *[new_m[n] for n in TWIN_WEIGHTS], *[new_v[n] for n in TWIN_WEIGHTS])
```

```python
import jax
import jax.numpy as jnp
from jax import lax
from jax.experimental import pallas as pl
from jax.experimental.pallas import tpu as pltpu

F32 = jnp.float32
BF16 = jnp.bfloat16

D = 1024
NH = 4
DH = 128
HW = NH * DH
DFF = 4 * D
NMEM = 256
EPS = 1e-6
NDEV = 8
LANES = 128
PAIR = 128
CHUNK = 64
D_IN = 7176
D_INP = 7680
VMEM_LIMIT = 56 * 1024 * 1024

ADAM_LR, ADAM_B1, ADAM_B2, ADAM_EPS, ADAM_WD, ADAM_STEP = 0.001, 0.9, 0.999, 1e-08, 0.01, 10

CB_Z, CB_SQ, CB_SK, CB_SV, CB_MQ, CB_AB = 3, 4, 5, 6, 7, 14

NN = (((1,), (0,)), ((), ()))
NT = (((1,), (1,)), ((), ()))
TN = (((0,), (0,)), ((), ()))

BIG = ("w_in", "w_mem_kv", "w_br_gdn", "w_br_sb", "w_br_mem", "w_o", "w_up", "w_down", "conv_w")
BIG_ROWS = (7176, 1024, 512, 512, 512, 1024, 4096, 4096, 6)
R_BIG = 19456
SMALL = ("norm1_g", "a_log", "dt_bias", "gdn_norm_g", "sb_q_norm_g", "sb_k_norm_g", "mem_norm_g",
         "mem_q_norm_g", "mem_k_norm_g", "norm2_g")
SMALL_ROWS = (8, 1, 1, 1, 1, 1, 8, 1, 1, 8)
R_SMALL = 32
WEIGHTS = ("norm1_g", "w_in", "conv_w", "a_log", "dt_bias", "gdn_norm_g", "sb_q_norm_g", "sb_k_norm_g",
           "mem_norm_g", "w_mem_kv", "mem_q_norm_g", "mem_k_norm_g", "w_br_gdn", "w_br_sb", "w_br_mem",
           "w_o", "norm2_g", "w_up", "w_down")


def _cp(sem=None):
    return pltpu.CompilerParams(dimension_semantics=sem, vmem_limit_bytes=VMEM_LIMIT)


def _dot(a, b, dims=NN):
    return lax.dot_general(a, b, dims, preferred_element_type=F32)


def _dbf(a, b, dims=NN):
    return _dot(a.astype(BF16), b.astype(BF16), dims)


def _split(a, n):
    parts = []
    for _ in range(n):
        h = a.astype(BF16)
        parts.append(h)
        a = a - h.astype(F32)
    return parts


def _d3(a, b, dims=NN):
    ah, al = _split(a, 2)
    bh, bl = _split(b, 2)
    return _dot(ah, bh, dims) + (_dot(ah, bl, dims) + _dot(al, bh, dims))


def _dxr(a, e, dims=NN):
    eb = e.astype(BF16)
    a1, a2, a3 = _split(a, 3)
    return _dot(a1, eb, dims) + (_dot(a2, eb, dims) + _dot(a3, eb, dims))


def _dxl(e, a, dims=NN):
    eb = e.astype(BF16)
    a1, a2, a3 = _split(a, 3)
    return _dot(eb, a1, dims) + (_dot(eb, a2, dims) + _dot(eb, a3, dims))


def _sigmoid(x):
    return 1.0 / (1.0 + jnp.exp(-x))


def _softplus(x):
    return jnp.maximum(x, 0.0) + jnp.log(1.0 + jnp.exp(-jnp.abs(x)))


def _rms(x, g):
    r = lax.rsqrt(jnp.mean(x * x, axis=-1, keepdims=True) + EPS)
    return x * r * g, r


def _rms_bwd(dy, x, g, r):
    dyg = dy * g
    dx = r * (dyg - x * (r * r) * jnp.mean(dyg * x, axis=-1, keepdims=True))
    dg = jnp.sum(dy * (x * r), axis=0, keepdims=True)
    return dx, dg


def _hs(h):
    return slice(h * DH, (h + 1) * DH)


def _row_tile(s):
    return 512 if s >= 2048 else 256


def _narrow_tile(s):
    return min(256, s)


def _mm(name, a, b, mode, tm, tn, tk, pro=None, pro_g=None, epi=None, epi_x=None):
    if mode == "tn":
        K, M = a.shape
    else:
        M, K = a.shape
    N = b.shape[0] if mode == "nt" else b.shape[1]
    tm, tn, tk = min(tm, M), min(tn, N), min(tk, K)
    nk = K // tk
    assert M % tm == 0 and N % tn == 0 and K % tk == 0, (name, M, N, K, tm, tn, tk)
    dims = {"nn": NN, "nt": NT, "tn": TN}[mode]

    def body(*refs):
        a_ref, b_ref = refs[0], refs[1]
        pos = 2
        g_ref = e_ref = None
        if pro == "rms":
            g_ref = refs[pos]
            pos += 1
        if epi is not None:
            e_ref = refs[pos]
            pos += 1
        o_ref = refs[pos]
        av = a_ref[...]
        if pro == "rms":
            av, _ = _rms(av.astype(F32), g_ref[...])
        elif pro == "relu2":
            av = jnp.square(jnp.maximum(av, 0.0))
        part = _dbf(av, b_ref[...], dims)

        def finish(acc):
            if epi == "add":
                acc = acc + e_ref[...]
            elif epi == "drelu2":
                acc = acc * (2.0 * jnp.maximum(e_ref[...], 0.0))
            o_ref[...] = acc

        if nk == 1:
            finish(part)
        else:
            acc_ref = refs[pos + 1]
            k = pl.program_id(2)

            @pl.when(k == 0)
            def _():
                acc_ref[...] = part

            @pl.when(k > 0)
            def _():
                acc_ref[...] += part

            @pl.when(k == nk - 1)
            def _():
                finish(acc_ref[...])

    if mode == "tn":
        a_spec = pl.BlockSpec((tk, tm), lambda i, j, k: (k, i))
    else:
        a_spec = pl.BlockSpec((tm, tk), lambda i, j, k: (i, k))
    if mode == "nt":
        b_spec = pl.BlockSpec((tn, tk), lambda i, j, k: (j, k))
    else:
        b_spec = pl.BlockSpec((tk, tn), lambda i, j, k: (k, j))
    in_specs, ops = [a_spec, b_spec], [a, b]
    if pro == "rms":
        w = pro_g.shape[1]
        assert (tm if mode == "tn" else tk) == w, name
        in_specs.append(pl.BlockSpec((1, w), lambda i, j, k: (0, 0)))
        ops.append(pro_g)
    if epi is not None:
        in_specs.append(pl.BlockSpec((tm, tn), lambda i, j, k: (i, j)))
        ops.append(epi_x)
    return pl.pallas_call(
        body, name=name, grid=(M // tm, N // tn, nk),
        in_specs=in_specs, out_specs=pl.BlockSpec((tm, tn), lambda i, j, k: (i, j)),
        out_shape=jax.ShapeDtypeStruct((M, N), F32),
        scratch_shapes=[pltpu.VMEM((tm, tn), F32)] if nk > 1 else [],
        compiler_params=_cp(("parallel", "parallel", "arbitrary")),
    )(*ops)


def _head_select(first_lane):
    l = lax.broadcasted_iota(jnp.int32, (LANES, HW), 0)
    c = lax.broadcasted_iota(jnp.int32, (LANES, HW), 1)
    return (l == first_lane + c // DH).astype(F32)


def _conv_taps(buf, cw, ts):
    c = cw[3:4, :] * buf[8:8 + ts, :]
    for j in range(3):
        k = 3 - j
        c = c + cw[j:j + 1, :] * buf[8 - k:8 - k + ts, :]
    return c


def _pre_fwd(proj, conv_w, alog_f, dtb_f, gsq, gsk, gmq, S):
    ts = _narrow_tile(S)
    hb = ts // 8

    def body(qkv_ref, halo_ref, ab_ref, sq_ref, sk_ref, sv_ref, mq_ref, cw_ref, al_ref, dt_ref, gsq_ref, gsk_ref,
             gmq_ref, gq_o, gk_o, gv_o, gf_o, bf_o, sqn_o, skn_o, svb_o, qmn_o, buf):
        i = pl.program_id(0)
        buf[0:8, :] = jnp.where(i == 0, 0.0, halo_ref[...])
        buf[8:8 + ts, :] = qkv_ref[...]
        c = _conv_taps(buf, cw_ref[...], ts)
        a = c * _sigmoid(c)
        for h in range(NH):
            q = a[:, h * DH:(h + 1) * DH]
            k = a[:, HW + h * DH:HW + (h + 1) * DH]
            gq_o[:, _hs(h)] = q * (lax.rsqrt(jnp.sum(q * q, axis=-1, keepdims=True) + EPS) * DH ** -0.5)
            gk_o[:, _hs(h)] = k * lax.rsqrt(jnp.sum(k * k, axis=-1, keepdims=True) + EPS)
            sqn_o[:, _hs(h)] = _rms(sq_ref[:, _hs(h)], gsq_ref[...])[0].astype(BF16)
            skn_o[:, _hs(h)] = _rms(sk_ref[:, _hs(h)], gsk_ref[...])[0].astype(BF16)
            qmn_o[:, _hs(h)] = _rms(mq_ref[:, _hs(h)], gmq_ref[...])[0].astype(BF16)
        gv_o[...] = a[:, 2 * HW:3 * HW]
        svb_o[...] = sv_ref[...].astype(BF16)
        ab = ab_ref[:, 0:LANES]
        a_bc = _dxr(ab, _head_select(0))
        b_bc = _dxr(ab, _head_select(NH))
        gf_o[...] = -jnp.exp(al_ref[...]) * _softplus(a_bc + dt_ref[...])
        bf_o[...] = _sigmoid(b_bc)

    row = lambda cb: pl.BlockSpec((ts, HW), lambda i: (i, cb))
    full = lambda r, c: pl.BlockSpec((r, c), lambda i: (0, 0))
    f32o = jax.ShapeDtypeStruct((S, HW), F32)
    bfo = jax.ShapeDtypeStruct((S, HW), BF16)
    return pl.pallas_call(
        body, name="pre_fwd", grid=(S // ts,),
        in_specs=[pl.BlockSpec((ts, 3 * HW), lambda i: (i, 0)),
                  pl.BlockSpec((8, 3 * HW), lambda i: (jnp.maximum(i * hb - 1, 0), 0)),
                  row(CB_AB), row(CB_SQ), row(CB_SK), row(CB_SV), row(CB_MQ),
                  full(4, 3 * HW), full(1, HW), full(1, HW), full(1, DH), full(1, DH), full(1, DH)],
        out_specs=[pl.BlockSpec((ts, HW), lambda i: (i, 0))] * 9,
        out_shape=[f32o, f32o, f32o, f32o, f32o, bfo, bfo, bfo, bfo],
        scratch_shapes=[pltpu.VMEM((ts + 8, 3 * HW), F32)],
        compiler_params=_cp(("parallel",)),
    )(proj, proj, proj, proj, proj, proj, proj, conv_w, alog_f, dtb_f, gsq, gsk, gmq)


def _pre_bwd(proj, conv_w, alog_f, dtb_f, gsq, gsk, dgq, dgk, dgv, dgf, dbf, dsqn, dskn, S):
    ts = _narrow_tile(S)
    hb = ts // 8

    def body(qkv_ref, halo_ref, ab_ref, sq_ref, sk_ref, cw_ref, al_ref, dt_ref, gsq_ref, gsk_ref,
             dgq_ref, dgk_ref, dgv_ref, dgf_ref, dbf_ref, dsqn_ref, dskn_ref,
             dc_o, dab_o, dsq_o, dsk_o, dcw_o, dal_o, ddt_o, dgsq_o, dgsk_o, buf):
        i = pl.program_id(0)

        @pl.when(i == 0)
        def _():
            dcw_o[...] = jnp.zeros_like(dcw_o)
            dal_o[...] = jnp.zeros_like(dal_o)
            ddt_o[...] = jnp.zeros_like(ddt_o)
            dgsq_o[...] = jnp.zeros_like(dgsq_o)
            dgsk_o[...] = jnp.zeros_like(dgsk_o)

        buf[0:8, :] = jnp.where(i == 0, 0.0, halo_ref[...])
        buf[8:8 + ts, :] = qkv_ref[...]
        c = _conv_taps(buf, cw_ref[...], ts)
        sg = _sigmoid(c)
        a = c * sg
        dsilu = sg * (1.0 + c * (1.0 - sg))
        dgsq = jnp.zeros((1, DH), F32)
        dgsk = jnp.zeros((1, DH), F32)
        for h in range(NH):
            q = a[:, h * DH:(h + 1) * DH]
            k = a[:, HW + h * DH:HW + (h + 1) * DH]
            nq = lax.rsqrt(jnp.sum(q * q, axis=-1, keepdims=True) + EPS)
            nk = lax.rsqrt(jnp.sum(k * k, axis=-1, keepdims=True) + EPS)
            dyq = dgq_ref[:, _hs(h)]
            dyk = dgk_ref[:, _hs(h)]
            dq = (nq * dyq - q * (nq * nq * nq) * jnp.sum(dyq * q, axis=-1, keepdims=True)) * DH ** -0.5
            dk = nk * dyk - k * (nk * nk * nk) * jnp.sum(dyk * k, axis=-1, keepdims=True)
            dc_o[:, h * DH:(h + 1) * DH] = dq * dsilu[:, h * DH:(h + 1) * DH]
            dc_o[:, HW + h * DH:HW + (h + 1) * DH] = dk * dsilu[:, HW + h * DH:HW + (h + 1) * DH]
            x = sq_ref[:, _hs(h)]
            _, r = _rms(x, gsq_ref[...])
            dx, dg = _rms_bwd(dsqn_ref[:, _hs(h)], x, gsq_ref[...], r)
            dsq_o[:, _hs(h)] = dx
            dgsq = dgsq + dg
            x = sk_ref[:, _hs(h)]
            _, r = _rms(x, gsk_ref[...])
            dx, dg = _rms_bwd(dskn_ref[:, _hs(h)], x, gsk_ref[...], r)
            dsk_o[:, _hs(h)] = dx
            dgsk = dgsk + dg
        dc_o[:, 2 * HW:3 * HW] = dgv_ref[...] * dsilu[:, 2 * HW:3 * HW]
        dgsq_o[...] += dgsq
        dgsk_o[...] += dgsk
        dc = dc_o[...]
        for j in range(4):
            k = 3 - j
            dcw_o[j:j + 1, :] += jnp.sum(dc * buf[8 - k:8 - k + ts, :], axis=0, keepdims=True)
        ab = ab_ref[:, 0:LANES]
        a_bc = _dxr(ab, _head_select(0))
        b_bc = _dxr(ab, _head_select(NH))
        pre = a_bc + dt_ref[...]
        ea = jnp.exp(al_ref[...])
        dgf = dgf_ref[...]
        dal_o[...] += jnp.sum(dgf * (-ea * _softplus(pre)), axis=0, keepdims=True)
        da = dgf * (-ea * _sigmoid(pre))
        ddt_o[...] += jnp.sum(da, axis=0, keepdims=True)
        beta = _sigmoid(b_bc)
        db = dbf_ref[...] * beta * (1.0 - beta)
        lane = lax.broadcasted_iota(jnp.int32, (ts, LANES), 1)
        dab = jnp.zeros((ts, LANES), F32)
        for h in range(NH):
            dab = dab + jnp.where(lane == h, da[:, _hs(h)], 0.0) + jnp.where(lane == NH + h, db[:, _hs(h)], 0.0)
        dab_o[:, 0:LANES] = dab
        dab_o[:, LANES:HW] = jnp.zeros((ts, HW - LANES), F32)

    row = lambda cb: pl.BlockSpec((ts, HW), lambda i: (i, cb))
    full = lambda r, c: pl.BlockSpec((r, c), lambda i: (0, 0))
    t512 = pl.BlockSpec((ts, HW), lambda i: (i, 0))
    return pl.pallas_call(
        body, name="pre_bwd", grid=(S // ts,),
        in_specs=[pl.BlockSpec((ts, 3 * HW), lambda i: (i, 0)),
                  pl.BlockSpec((8, 3 * HW), lambda i: (jnp.maximum(i * hb - 1, 0), 0)),
                  row(CB_AB), row(CB_SQ), row(CB_SK),
                  full(4, 3 * HW), full(1, HW), full(1, HW), full(1, DH), full(1, DH)] + [t512] * 7,
        out_specs=[pl.BlockSpec((ts, 3 * HW), lambda i: (i, 0)), t512, t512, t512,
                   full(4, 3 * HW), full(1, HW), full(1, HW), full(1, DH), full(1, DH)],
        out_shape=[jax.ShapeDtypeStruct((S, 3 * HW), F32)] + [jax.ShapeDtypeStruct((S, HW), F32)] * 3
        + [jax.ShapeDtypeStruct((4, 3 * HW), F32), jax.ShapeDtypeStruct((1, HW), F32),
           jax.ShapeDtypeStruct((1, HW), F32), jax.ShapeDtypeStruct((1, DH), F32),
           jax.ShapeDtypeStruct((1, DH), F32)],
        scratch_shapes=[pltpu.VMEM((ts + 8, 3 * HW), F32)],
        compiler_params=_cp(("arbitrary",)),
    )(proj, proj, proj, proj, proj, conv_w, alog_f, dtb_f, gsq, gsk, dgq, dgk, dgv, dgf, dbf, dsqn, dskn)


def _conv_bwd(dc, conv_w, S):
    ts = _row_tile(S)
    hb = ts // 8
    n = S // ts

    def body(dc_ref, halo_ref, cw_ref, o_ref, buf):
        i = pl.program_id(0)
        buf[0:ts, :] = dc_ref[...]
        buf[ts:ts + 8, :] = jnp.where(i == n - 1, 0.0, halo_ref[...])
        cw = cw_ref[...]
        acc = cw[3:4, :] * buf[0:ts, :]
        for k in range(1, 4):
            acc = acc + cw[3 - k:4 - k, :] * buf[k:k + ts, :]
        o_ref[...] = acc

    return pl.pallas_call(
        body, name="conv_bwd", grid=(n,),
        in_specs=[pl.BlockSpec((ts, 3 * HW), lambda i: (i, 0)),
                  pl.BlockSpec((8, 3 * HW), lambda i: (jnp.minimum((i + 1) * hb, S // 8 - 1), 0)),
                  pl.BlockSpec((4, 3 * HW), lambda i: (0, 0))],
        out_specs=pl.BlockSpec((ts, 3 * HW), lambda i: (i, 0)),
        out_shape=jax.ShapeDtypeStruct((S, 3 * HW), F32),
        scratch_shapes=[pltpu.VMEM((ts + 8, 3 * HW), F32)],
        compiler_params=_cp(("parallel",)),
    )(dc, dc, conv_w)


def _gdn_masks():
    r = lax.broadcasted_iota(jnp.int32, (PAIR, PAIR), 0)
    c = lax.broadcasted_iota(jnp.int32, (PAIR, PAIR), 1)
    same = ((r >= CHUNK) & (c >= CHUNK)) | ((r < CHUNK) & (c < CHUNK))
    return dict(r=r, same=same, tril=same & (r >= c), strict=same & (r > c), triu=same & (c >= r), eye=r == c,
                in_a=r < CHUNK, last_a=r == CHUNK - 1, last_b=r == PAIR - 1)


def _tri_inv(lm, eye):
    p = eye.astype(F32) - lm
    lp = _d3(lm, lm)
    for it in range(5):
        p = p + _d3(p, lp)
        if it < 4:
            lp = _d3(lp, lp)
    return p


def _gdn_block(m, q, k, v, g, beta):
    f = {}
    gc = _dxl(m["tril"].astype(F32), g)
    gcr = jnp.sum(jnp.where(m["eye"], gc, 0.0), axis=0, keepdims=True)
    gam = jnp.where(m["tril"], jnp.exp(jnp.minimum(gc - gcr, 0.0)), 0.0)
    kb = k * beta
    vb = v * beta
    lm = jnp.where(m["strict"], _d3(kb, k, NT) * gam, 0.0)
    t = _tri_inv(lm, m["eye"])
    eg = jnp.exp(gc)
    kbe = kb * eg
    f["u"] = _d3(t, vb)
    f["w"] = _d3(t, kbe)
    f["aqk"] = jnp.where(m["tril"], _d3(q, k, NT) * gam, 0.0)
    f["qd"] = q * eg
    ga = jnp.sum(jnp.where(m["last_a"], gc, 0.0), axis=0, keepdims=True)
    gb = jnp.sum(jnp.where(m["last_b"], gc, 0.0), axis=0, keepdims=True)
    e2 = jnp.exp(jnp.where(m["in_a"], ga, gb) - gc)
    f["kd"] = k * e2
    f.update(gam=gam, kb=kb, vb=vb, lm=lm, t=t, eg=eg, kbe=kbe, e2=e2, gla=jnp.exp(ga), glb=jnp.exp(gb))
    return f


def _gdn_fwd(gq, gk, gv, gf, bf, S):
    nb = S // PAIR

    def body(q_ref, k_ref, v_ref, g_ref, b_ref, o_ref, st_ref, s_scr):
        @pl.when(pl.program_id(0) == 0)
        def _():
            s_scr[...] = jnp.zeros_like(s_scr)

        m = _gdn_masks()
        for h in range(NH):
            f = _gdn_block(m, q_ref[:, _hs(h)], k_ref[:, _hs(h)], v_ref[:, _hs(h)], g_ref[:, _hs(h)],
                           b_ref[:, _hs(h)])
            s0 = s_scr[h]
            st_ref[0, 0, h] = s0
            vna = f["u"][:CHUNK] - _d3(f["w"][:CHUNK], s0)
            oa = _d3(f["qd"][:CHUNK], s0)
            s1 = s0 * f["gla"] + _d3(f["kd"][:CHUNK], vna, TN)
            st_ref[0, 1, h] = s1
            vnb = f["u"][CHUNK:] - _d3(f["w"][CHUNK:], s1)
            ob = _d3(f["qd"][CHUNK:], s1)
            s_scr[h] = s1 * f["glb"] + _d3(f["kd"][CHUNK:], vnb, TN)
            vn = jnp.concatenate([vna, vnb], axis=0)
            o_ref[:, _hs(h)] = jnp.concatenate([oa, ob], axis=0) + _d3(f["aqk"], vn)

    blk = pl.BlockSpec((PAIR, HW), lambda i: (i, 0))
    return pl.pallas_call(
        body, name="gdn_fwd", grid=(nb,),
        in_specs=[blk] * 5,
        out_specs=[blk, pl.BlockSpec((1, 2, NH, DH, DH), lambda i: (i, 0, 0, 0, 0))],
        out_shape=[jax.ShapeDtypeStruct((S, HW), F32), jax.ShapeDtypeStruct((nb, 2, NH, DH, DH), F32)],
        scratch_shapes=[pltpu.VMEM((NH, DH, DH), F32)],
        compiler_params=_cp(("arbitrary",)),
    )(gq, gk, gv, gf, bf)


def _gdn_bwd(gq, gk, gv, gf, bf, states, do, S):
    nb = S // PAIR

    def body(q_ref, k_ref, v_ref, g_ref, b_ref, st_ref, do_ref, dq_o, dk_o, dv_o, dg_o, db_o, ds_scr):
        @pl.when(pl.program_id(0) == 0)
        def _():
            ds_scr[...] = jnp.zeros_like(ds_scr)

        m = _gdn_masks()
        ones = jnp.ones((PAIR, PAIR), F32)
        for h in range(NH):
            q, k, v, beta = q_ref[:, _hs(h)], k_ref[:, _hs(h)], v_ref[:, _hs(h)], b_ref[:, _hs(h)]
            f = _gdn_block(m, q, k, v, g_ref[:, _hs(h)], beta)
            u, w, aqk, qd, kd = f["u"], f["w"], f["aqk"], f["qd"], f["kd"]
            s0 = st_ref[0, 0, h]
            s1 = st_ref[0, 1, h]
            vna = u[:CHUNK] - _d3(w[:CHUNK], s0)
            vnb = u[CHUNK:] - _d3(w[CHUNK:], s1)
            vn = jnp.concatenate([vna, vnb], axis=0)
            do = do_ref[:, _hs(h)]
            ds = ds_scr[h]
            dvn_i = _d3(aqk, do, TN)
            dvnb = dvn_i[CHUNK:] + _d3(kd[CHUNK:], ds)
            dqdb = _d3(do[CHUNK:], s1, NT)
            dkdb = _d3(vnb, ds, NT)
            dglb = jnp.sum(jnp.sum(ds * s1, axis=1, keepdims=True), axis=0, keepdims=True)
            dwb = -_d3(dvnb, s1, NT)
            ds = ds * f["glb"] + _d3(qd[CHUNK:], do[CHUNK:], TN) - _d3(w[CHUNK:], dvnb, TN)
            dvna = dvn_i[:CHUNK] + _d3(kd[:CHUNK], ds)
            dqda = _d3(do[:CHUNK], s0, NT)
            dkda = _d3(vna, ds, NT)
            dgla = jnp.sum(jnp.sum(ds * s0, axis=1, keepdims=True), axis=0, keepdims=True)
            dwa = -_d3(dvna, s0, NT)
            ds_scr[h] = ds * f["gla"] + _d3(qd[:CHUNK], do[:CHUNK], TN) - _d3(w[:CHUNK], dvna, TN)
            dvn = jnp.concatenate([dvna, dvnb], axis=0)
            dqd = jnp.concatenate([dqda, dqdb], axis=0)
            dkd = jnp.concatenate([dkda, dkdb], axis=0)
            dw = jnp.concatenate([dwa, dwb], axis=0)
            daqk = jnp.where(m["tril"], _d3(do, vn, NT), 0.0)
            t = f["t"]
            dt = _d3(dvn, f["vb"], NT) + _d3(dw, f["kbe"], NT)
            dvb = _d3(t, dvn, TN)
            dkbe = _d3(t, dw, TN)
            dl = -jnp.where(m["strict"], _d3(t, _d3(dt, t, NT), TN), 0.0)
            dm = dl * f["gam"]
            dn = daqk * f["gam"]
            dkb = _d3(dm, k) + dkbe * f["eg"]
            dk_o[:, _hs(h)] = _d3(dm, f["kb"], TN) + _d3(dn, q, TN) + dkd * f["e2"] + beta * dkb
            dq_o[:, _hs(h)] = _d3(dn, k) + dqd * f["eg"]
            gm = dl * f["lm"] + daqk * aqk
            dkdkd = dkd * kd
            dgc = _dxr(gm + dqd * qd + dkbe * f["kbe"] - dkdkd, ones) - _dxr(gm, ones, TN)
            chunk_tot = _dxl(m["same"].astype(F32), _dxr(dkdkd, ones))
            dgl = jnp.where(m["in_a"], dgla * f["gla"], dglb * f["glb"])
            dgc = dgc + jnp.where(m["last_a"] | m["last_b"], chunk_tot + dgl, 0.0)
            db_o[:, _hs(h)] = _dxr(dkb * k + dvb * v, ones)
            dv_o[:, _hs(h)] = beta * dvb
            dg_o[:, _hs(h)] = _dxl(m["triu"].astype(F32), dgc)

    blk = pl.BlockSpec((PAIR, HW), lambda i: (nb - 1 - i, 0))
    o = jax.ShapeDtypeStruct((S, HW), F32)
    return pl.pallas_call(
        body, name="gdn_bwd", grid=(nb,),
        in_specs=[blk] * 5 + [pl.BlockSpec((1, 2, NH, DH, DH), lambda i: (nb - 1 - i, 0, 0, 0, 0)), blk],
        out_specs=[blk] * 5, out_shape=[o] * 5,
        scratch_shapes=[pltpu.VMEM((NH, DH, DH), F32)],
        compiler_params=_cp(("arbitrary",)),
    )(gq, gk, gv, gf, bf, states, do)


SB_TQ = 256
SB_TK = 128


def _sb_tile(q, k, row, col, k0):
    z = _dot(q, k, NT) * DH ** -0.5
    mask = (col + k0) < row
    ls = jnp.minimum(z, 0.0) - jnp.log(1.0 + jnp.exp(-jnp.abs(z)))
    lneg = jnp.where(mask, ls - z, 0.0)
    return mask, ls, lneg


def _prefix(x, u):
    xh, xl = _split(x, 2)
    return _dot(xh, u) + _dot(xl, u)


def _sb_fwd(sqn, skn, svb, S):
    tq, tk = min(SB_TQ, S), SB_TK

    def body(q_ref, k_ref, v_ref, o_ref, t_ref):
        qb = pl.program_id(1)
        q = q_ref[...]
        row = lax.broadcasted_iota(jnp.int32, (tq, tk), 0) + qb * tq
        col = lax.broadcasted_iota(jnp.int32, (tq, tk), 1)
        r2 = lax.broadcasted_iota(jnp.int32, (tk, tk), 0)
        c2 = lax.broadcasted_iota(jnp.int32, (tk, tk), 1)
        u_after = (r2 > c2).astype(BF16)
        nkb = (qb + 1) * (tq // tk)

        def step(i, carry):
            acc, run = carry
            k0 = pl.multiple_of((nkb - 1 - i) * tk, tk)
            mask, ls, lneg = _sb_tile(q, k_ref[pl.ds(k0, tk), :], row, col, k0)
            between = run + _prefix(lneg, u_after)
            att = jnp.where(mask, jnp.exp(ls + between), 0.0)
            acc = acc + _dot(att.astype(BF16), v_ref[pl.ds(k0, tk), :])
            return acc, run + jnp.sum(lneg, axis=1, keepdims=True)

        acc, run = lax.fori_loop(0, nkb, step, (jnp.zeros((tq, DH), F32), jnp.zeros((tq, 1), F32)))
        o_ref[...] = acc
        t_ref[...] = jnp.broadcast_to(run, (tq, DH))

    qspec = pl.BlockSpec((tq, DH), lambda h, i: (i, h))
    kspec = pl.BlockSpec((S, DH), lambda h, i: (0, h))
    return pl.pallas_call(
        body, name="sb_fwd", grid=(NH, S // tq),
        in_specs=[qspec, kspec, kspec], out_specs=[qspec, qspec],
        out_shape=[jax.ShapeDtypeStruct((S, HW), F32)] * 2,
        compiler_params=_cp(("parallel", "arbitrary")),
    )(sqn, skn, svb)


def _sb_bwd(sqn, skn, svb, do, tot, S):
    tq, tk = min(SB_TQ, S), SB_TK

    def body(q_ref, k_ref, v_ref, do_ref, t_ref, dq_o, dk_o, dv_o):
        qb = pl.program_id(1)

        @pl.when(qb == 0)
        def _():
            dk_o[...] = jnp.zeros_like(dk_o)
            dv_o[...] = jnp.zeros_like(dv_o)

        q = q_ref[...]
        do = do_ref[...].astype(BF16)
        tot_l = t_ref[...]
        row = lax.broadcasted_iota(jnp.int32, (tq, tk), 0) + qb * tq
        col = lax.broadcasted_iota(jnp.int32, (tq, tk), 1)
        r2 = lax.broadcasted_iota(jnp.int32, (tk, tk), 0)
        c2 = lax.broadcasted_iota(jnp.int32, (tk, tk), 1)
        u_upto = (r2 <= c2).astype(BF16)
        u_before = (r2 < c2).astype(BF16)
        nkb = (qb + 1) * (tq // tk)

        def step(kb, carry):
            dq, run_l, run_e = carry
            k0 = pl.multiple_of(kb * tk, tk)
            k = k_ref[pl.ds(k0, tk), :]
            v = v_ref[pl.ds(k0, tk), :]
            mask, ls, lneg = _sb_tile(q, k, row, col, k0)
            between = tot_l - (run_l + _prefix(lneg, u_upto))
            att = jnp.where(mask, jnp.exp(ls + between), 0.0)
            e = _dot(do, v, NT) * att
            f = run_e + _prefix(e, u_before)
            sg = jnp.exp(ls)
            dz = (jnp.where(mask, e * (1.0 - sg) - f * sg, 0.0) * DH ** -0.5).astype(BF16)
            dv_o[pl.ds(k0, tk), :] += _dot(att.astype(BF16), do, TN)
            dk_o[pl.ds(k0, tk), :] += _dot(dz, q, TN)
            return (dq + _dot(dz, k), run_l + jnp.sum(lneg, axis=1, keepdims=True),
                    run_e + jnp.sum(e, axis=1, keepdims=True))

        z1 = jnp.zeros((tq, 1), F32)
        dq, _, _ = lax.fori_loop(0, nkb, step, (jnp.zeros((tq, DH), F32), z1, z1))
        dq_o[...] = dq

    qspec = pl.BlockSpec((tq, DH), lambda h, i: (i, h))
    kspec = pl.BlockSpec((S, DH), lambda h, i: (0, h))
    o = jax.ShapeDtypeStruct((S, HW), F32)
    return pl.pallas_call(
        body, name="sb_bwd", grid=(NH, S // tq),
        in_specs=[qspec, kspec, kspec, qspec, qspec], out_specs=[qspec, kspec, kspec],
        out_shape=[o, o, o],
        compiler_params=_cp(("parallel", "arbitrary")),
    )(sqn, skn, svb, do, tot)


def _mem_probs(qn, kn):
    s = _dot(qn, kn.astype(BF16), NT) * DH ** -0.5
    p = jnp.exp(s - jnp.max(s, axis=-1, keepdims=True))
    return p / jnp.sum(p, axis=-1, keepdims=True)


def _mem_fwd(qmn, kv, gmk, S):
    ts = _row_tile(S)

    def body(q_ref, kv_ref, gk_ref, o_ref):
        for h in range(NH):
            kn, _ = _rms(kv_ref[:, _hs(h)], gk_ref[...])
            p = _mem_probs(q_ref[:, _hs(h)], kn)
            o_ref[:, _hs(h)] = _dbf(p, kv_ref[:, HW + h * DH:HW + (h + 1) * DH])

    return pl.pallas_call(
        body, name="mem_fwd", grid=(S // ts,),
        in_specs=[pl.BlockSpec((ts, HW), lambda i: (i, 0)), pl.BlockSpec((NMEM, 2 * HW), lambda i: (0, 0)),
                  pl.BlockSpec((1, DH), lambda i: (0, 0))],
        out_specs=pl.BlockSpec((ts, HW), lambda i: (i, 0)),
        out_shape=jax.ShapeDtypeStruct((S, HW), F32),
        compiler_params=_cp(("parallel",)),
    )(qmn, kv, gmk)


def _mem_bwd(proj, qmn, kv, gmq, gmk, do, S):
    ts = _row_tile(S)
    n = S // ts

    def body(mq_ref, q_ref, kv_ref, gq_ref, gk_ref, do_ref, dmq_o, dkv_o, dgq_o, dgk_o, dkn_scr):
        i = pl.program_id(0)

        @pl.when(i == 0)
        def _():
            dkv_o[...] = jnp.zeros_like(dkv_o)
            dgq_o[...] = jnp.zeros_like(dgq_o)
            dkn_scr[...] = jnp.zeros_like(dkn_scr)

        dgq = jnp.zeros((1, DH), F32)
        for h in range(NH):
            km = kv_ref[:, _hs(h)]
            vm = kv_ref[:, HW + h * DH:HW + (h + 1) * DH].astype(BF16)
            kn, _ = _rms(km, gk_ref[...])
            qn = q_ref[:, _hs(h)]
            p = _mem_probs(qn, kn)
            dob = do_ref[:, _hs(h)].astype(BF16)
            dkv_o[:, HW + h * DH:HW + (h + 1) * DH] += _dot(p.astype(BF16), dob, TN)
            dp = _dot(dob, vm, NT)
            dsc = (p * (dp - jnp.sum(dp * p, axis=-1, keepdims=True)) * DH ** -0.5).astype(BF16)
            dkn_scr[:, _hs(h)] += _dot(dsc, qn, TN)
            x = mq_ref[:, _hs(h)]
            _, r = _rms(x, gq_ref[...])
            dx, dg = _rms_bwd(_dot(dsc, kn.astype(BF16)), x, gq_ref[...], r)
            dmq_o[:, _hs(h)] = dx
            dgq = dgq + dg
        dgq_o[...] += dgq

        @pl.when(i == n - 1)
        def _():
            dgk = jnp.zeros((1, DH), F32)
            for h in range(NH):
                km = kv_ref[:, _hs(h)]
                _, r = _rms(km, gk_ref[...])
                dx, dg = _rms_bwd(dkn_scr[:, _hs(h)], km, gk_ref[...], r)
                dkv_o[:, _hs(h)] = dx
                dgk = dgk + dg
            dgk_o[...] = dgk

    full = lambda r, c: pl.BlockSpec((r, c), lambda i: (0, 0))
    t512 = pl.BlockSpec((ts, HW), lambda i: (i, 0))
    return pl.pallas_call(
        body, name="mem_bwd", grid=(n,),
        in_specs=[pl.BlockSpec((ts, HW), lambda i: (i, CB_MQ)), t512, full(NMEM, 2 * HW), full(1, DH), full(1, DH),
                  t512],
        out_specs=[t512, full(NMEM, 2 * HW), full(1, DH), full(1, DH)],
        out_shape=[jax.ShapeDtypeStruct((S, HW), F32), jax.ShapeDtypeStruct((NMEM, 2 * HW), F32),
                   jax.ShapeDtypeStruct((1, DH), F32), jax.ShapeDtypeStruct((1, DH), F32)],
        scratch_shapes=[pltpu.VMEM((NMEM, HW), F32)],
        compiler_params=_cp(("arbitrary",)),
    )(proj, qmn, kv, gmq, gmk, do)


def _gated_gdn(o, z, g):
    sg = _sigmoid(z)
    outs, rs = [], []
    for h in range(NH):
        y, r = _rms(o[:, _hs(h)], g)
        outs.append(y * (z[:, _hs(h)] * sg[:, _hs(h)]))
        rs.append(r)
    return jnp.concatenate(outs, axis=1), rs, sg


def _merge_fwd(x, proj, ogdn, osb, omem, ggdn, wbg, wbs, wbm, wo, S):
    ts = _narrow_tile(S)

    def body(x_ref, z_ref, g0_ref, g1_ref, g2_ref, og_ref, os_ref, om_ref, gg_ref, wbg_ref, wbs_ref, wbm_ref,
             wo_ref, x1_o, mix_o):
        on, _, _ = _gated_gdn(og_ref[...], z_ref[...], gg_ref[...])
        mix = (_sigmoid(g0_ref[...]) * _dbf(on, wbg_ref[...]) + _sigmoid(g1_ref[...]) * _dbf(os_ref[...], wbs_ref[...])
               + _sigmoid(g2_ref[...]) * _dbf(om_ref[...], wbm_ref[...]))
        mix_o[...] = mix.astype(BF16)
        x1_o[...] = x_ref[...] + _dbf(mix, wo_ref[...])

    t512 = pl.BlockSpec((ts, HW), lambda i: (i, 0))
    t1k = pl.BlockSpec((ts, D), lambda i: (i, 0))
    gate = lambda j: pl.BlockSpec((ts, D), lambda i: (i, 4 + j))
    full = lambda r, c: pl.BlockSpec((r, c), lambda i: (0, 0))
    return pl.pallas_call(
        body, name="merge_fwd", grid=(S // ts,),
        in_specs=[t1k, pl.BlockSpec((ts, HW), lambda i: (i, CB_Z)), gate(0), gate(1), gate(2), t512, t512, t512,
                  full(1, DH), full(HW, D), full(HW, D), full(HW, D), full(D, D)],
        out_specs=[t1k, t1k],
        out_shape=[jax.ShapeDtypeStruct((S, D), F32), jax.ShapeDtypeStruct((S, D), BF16)],
        compiler_params=_cp(("parallel",)),
    )(x, proj, proj, proj, proj, ogdn, osb, omem, ggdn, wbg, wbs, wbm, wo)


def _merge_bwd(dmix, proj, ogdn, osb, omem, ggdn, wbg, wbs, wbm, S):
    ts = _narrow_tile(S)

    def body(dm_ref, z_ref, g0_ref, g1_ref, g2_ref, og_ref, os_ref, om_ref, gg_ref, wbg_ref, wbs_ref, wbm_ref,
             dgl0_o, dgl1_o, dgl2_o, dog_o, dz_o, dos_o, dom_o, dwbg_o, dwbs_o, dwbm_o, dgg_o):
        @pl.when(pl.program_id(0) == 0)
        def _():
            for ref in (dwbg_o, dwbs_o, dwbm_o, dgg_o):
                ref[...] = jnp.zeros_like(ref)

        dm = dm_ref[...]
        og = og_ref[...]
        z = z_ref[...]
        on, rs, sg = _gated_gdn(og, z, gg_ref[...])
        branch = ((on, g0_ref, wbg_ref, dgl0_o, dwbg_o), (os_ref[...], g1_ref, wbs_ref, dgl1_o, dwbs_o),
                  (om_ref[...], g2_ref, wbm_ref, dgl2_o, dwbm_o))
        dos = []
        for o, g_ref, w_ref, dgl_o, dw_o in branch:
            ob = o.astype(BF16)
            gate = _sigmoid(g_ref[...])
            dgl_o[...] = dm * _dot(ob, w_ref[...]) * gate * (1.0 - gate)
            dy = (dm * gate).astype(BF16)
            dw_o[...] += _dot(ob, dy, TN)
            dos.append(_dot(dy, w_ref[...], NT))
        dos_o[...] = dos[1]
        dom_o[...] = dos[2]
        don = dos[0]
        dgg = jnp.zeros((1, DH), F32)
        for h in range(NH):
            oh, zh, sh = og[:, _hs(h)], z[:, _hs(h)], sg[:, _hs(h)]
            y = oh * rs[h] * gg_ref[...]
            dz_o[:, _hs(h)] = don[:, _hs(h)] * y * (sh * (1.0 + zh * (1.0 - sh)))
            dx, dg = _rms_bwd(don[:, _hs(h)] * (zh * sh), oh, gg_ref[...], rs[h])
            dog_o[:, _hs(h)] = dx
            dgg = dgg + dg
        dgg_o[...] += dgg

    t512 = pl.BlockSpec((ts, HW), lambda i: (i, 0))
    t1k = pl.BlockSpec((ts, D), lambda i: (i, 0))
    gate = lambda j: pl.BlockSpec((ts, D), lambda i: (i, 4 + j))
    full = lambda r, c: pl.BlockSpec((r, c), lambda i: (0, 0))
    s1k = jax.ShapeDtypeStruct((S, D), F32)
    s512 = jax.ShapeDtypeStruct((S, HW), F32)
    wsh = jax.ShapeDtypeStruct((HW, D), F32)
    return pl.pallas_call(
        body, name="merge_bwd", grid=(S // ts,),
        in_specs=[t1k, pl.BlockSpec((ts, HW), lambda i: (i, CB_Z)), gate(0), gate(1), gate(2), t512, t512, t512,
                  full(1, DH), full(HW, D), full(HW, D), full(HW, D)],
        out_specs=[t1k, t1k, t1k, t512, t512, t512, t512, full(HW, D), full(HW, D), full(HW, D), full(1, DH)],
        out_shape=[s1k, s1k, s1k, s512, s512, s512, s512, wsh, wsh, wsh, jax.ShapeDtypeStruct((1, DH), F32)],
        compiler_params=_cp(("arbitrary",)),
    )(dmix, proj, proj, proj, proj, ogdn, osb, omem, ggdn, wbg, wbs, wbm)


def _loss_grad(y, target, S):
    ts = _row_tile(S)

    def body(y_ref, t_ref, dy_o, loss_o):
        @pl.when(pl.program_id(0) == 0)
        def _():
            loss_o[...] = jnp.zeros_like(loss_o)

        err = y_ref[...] - t_ref[...]
        dy_o[...] = err * (1.0 / D)
        per_tok = jnp.sum(err * err, axis=1, keepdims=True) * (1.0 / D)
        loss_o[...] += 0.5 * jnp.sum(per_tok, axis=0, keepdims=True)

    t1k = pl.BlockSpec((ts, D), lambda i: (i, 0))
    return pl.pallas_call(
        body, name="loss_grad", grid=(S // ts,), in_specs=[t1k, t1k],
        out_specs=[t1k, pl.BlockSpec((1, 1), lambda i: (0, 0))],
        out_shape=[jax.ShapeDtypeStruct((S, D), F32), jax.ShapeDtypeStruct((1, 1), F32)],
        compiler_params=_cp(("arbitrary",)),
    )(y, target)


def _norm_bwd(name, dh, x, g, res):
    rows = x.shape[0]
    ts = min(_row_tile(rows), rows)

    def body(*refs):
        dh_ref, x_ref, g_ref = refs[:3]
        dx_o, dg_o = refs[-2:]

        @pl.when(pl.program_id(0) == 0)
        def _():
            dg_o[...] = jnp.zeros_like(dg_o)

        xv = x_ref[...]
        _, r = _rms(xv, g_ref[...])
        dx, dg = _rms_bwd(dh_ref[...], xv, g_ref[...], r)
        dx_o[...] = dx if res is None else dx + refs[3][...]
        dg_o[...] += dg

    t1k = pl.BlockSpec((ts, D), lambda i: (i, 0))
    gsp = pl.BlockSpec((1, D), lambda i: (0, 0))
    ops = [dh, x, g] + ([] if res is None else [res])
    return pl.pallas_call(
        body, name=name, grid=(rows // ts,), in_specs=[t1k, t1k, gsp] + ([] if res is None else [t1k]),
        out_specs=[t1k, gsp],
        out_shape=[jax.ShapeDtypeStruct((rows, D), F32), jax.ShapeDtypeStruct((1, D), F32)],
        compiler_params=_cp(("arbitrary",)),
    )(*ops)


def _adamw(name, gall, w, m, v):
    rows = w.shape[0]
    tr = min(1216, rows)
    assert rows % tr == 0

    def body(g_ref, w_ref, m_ref, v_ref, g_o, d_o, m_o, v_o):
        g = g_ref[0]
        for j in range(1, NDEV):
            g = g + g_ref[j]
        m_new = ADAM_B1 * m_ref[...] + (1.0 - ADAM_B1) * g
        v_new = ADAM_B2 * v_ref[...] + (1.0 - ADAM_B2) * jnp.square(g)
        m_hat = m_new / (1.0 - ADAM_B1 ** ADAM_STEP)
        v_hat = v_new / (1.0 - ADAM_B2 ** ADAM_STEP)
        g_o[...] = g
        d_o[...] = -ADAM_LR * (m_hat / (jnp.sqrt(v_hat) + ADAM_EPS) + ADAM_WD * w_ref[...])
        m_o[...] = m_new
        v_o[...] = v_new

    t = pl.BlockSpec((tr, LANES), lambda i: (i, 0))
    o = jax.ShapeDtypeStruct((rows, LANES), F32)
    return pl.pallas_call(
        body, name=name, grid=(rows // tr,),
        in_specs=[pl.BlockSpec((NDEV, tr, LANES), lambda i: (0, i, 0)), t, t, t],
        out_specs=[t, t, t, t], out_shape=[o, o, o, o],
        compiler_params=_cp(("parallel",)),
    )(gall, w, m, v)


def _exchange(name, x, gather):
    rows, cols = x.shape[-2:]

    def body(x_ref, o_ref, send_sems, recv_sems, local_sem):
        ix, iy, ic = lax.axis_index("x"), lax.axis_index("y"), lax.axis_index("c")
        me = 4 * ix + 2 * iy + ic
        own = pltpu.make_async_copy(x_ref if gather else x_ref.at[me], o_ref.at[me], local_sem)
        own.start()
        copies = []
        for k in range(1, NDEV):
            px, py, pc = ix ^ ((k >> 2) & 1), iy ^ ((k >> 1) & 1), ic ^ (k & 1)
            src = x_ref if gather else x_ref.at[4 * px + 2 * py + pc]
            cp = pltpu.make_async_remote_copy(
                src_ref=src, dst_ref=o_ref.at[me], send_sem=send_sems.at[k - 1], recv_sem=recv_sems.at[k - 1],
                device_id=(px, py, pc), device_id_type=pl.DeviceIdType.MESH)
            cp.start()
            copies.append(cp)
        for cp in copies:
            cp.wait()
        own.wait()

    hbm = pl.BlockSpec(memory_space=pltpu.HBM)
    return pl.pallas_call(
        body, name=name, in_specs=[hbm], out_specs=hbm,
        out_shape=jax.ShapeDtypeStruct((NDEV, rows, cols), x.dtype),
        scratch_shapes=[pltpu.SemaphoreType.DMA((NDEV - 1,)), pltpu.SemaphoreType.DMA((NDEV - 1,)),
                        pltpu.SemaphoreType.DMA],
    )(x)


COL_SHARDED = {"w_in": (D, D_IN), "w_br_gdn": (HW, D), "w_br_sb": (HW, D), "w_br_mem": (HW, D), "w_up": (D, DFF),
               "conv_w": (4, 3 * HW)}
ROW_SHARDED = {"w_mem_kv": (D, 2 * HW), "w_o": (D, D), "w_down": (DFF, D)}


def _pack_rows(parts, total):
    flat = jnp.concatenate([p.reshape(-1, LANES) for p in parts], axis=0)
    return jnp.pad(flat, ((0, total - flat.shape[0]), (0, 0)))


def _pack_full_grads(grads):
    parts = []
    for name in BIG:
        g = grads[name]
        if name in COL_SHARDED:
            r, c = COL_SHARDED[name]
            g = g.reshape(r, NDEV, c // NDEV).transpose(1, 0, 2)
        parts.append(g.reshape(NDEV, -1, LANES))
    flat = jnp.concatenate(parts, axis=1)
    return jnp.pad(flat, ((0, 0), (0, R_BIG - flat.shape[1]), (0, 0)))


def _unpack_gathered(slabs):
    out, pos = {}, 0
    for name, rows in zip(BIG, BIG_ROWS):
        g = slabs[:, pos:pos + rows]
        pos += rows
        if name in COL_SHARDED:
            r, c = COL_SHARDED[name]
            out[name] = g.reshape(NDEV, r, c // NDEV).transpose(1, 0, 2).reshape(r, c)
        else:
            r, c = ROW_SHARDED[name]
            out[name] = g.reshape(r, c)
    return out


def _unpack_shard(flat, shapes):
    out, pos = {}, 0
    for name, rows in zip(BIG, BIG_ROWS):
        out[name] = flat[pos:pos + rows].reshape(shapes[name])
        pos += rows
    return out


def _pack_small(vals):
    rows = []
    for name, n in zip(SMALL, SMALL_ROWS):
        v = vals[name].reshape(-1)
        rows.append(jnp.pad(v, (0, n * LANES - v.shape[0])).reshape(n, LANES))
    return _pack_rows(rows, R_SMALL)


def _unpack_small(flat, shapes):
    out, pos = {}, 0
    for name, n in zip(SMALL, SMALL_ROWS):
        size = shapes[name][-1]
        out[name] = flat[pos:pos + n].reshape(-1)[:size].reshape(shapes[name])
        pos += n
    return out


def _pad_w_in(w):
    return jnp.concatenate([w[:, :2048], w[:, 2056:], w[:, 2048:2056], jnp.zeros((D, D_INP - D_IN), w.dtype)], axis=1)


def _unpad_w_in(w):
    return jnp.concatenate([w[:, :2048], w[:, 7168:7176], w[:, 2048:7168]], axis=1)


def _per_head(v):
    return jnp.repeat(v.reshape(NH), DH).reshape(1, HW)


def _local_step(x, mem, target, w, sm):
    S = x.shape[0]
    ts = _row_tile(S)
    alog_f, dtb_f = _per_head(sm["a_log"]), _per_head(sm["dt_bias"])

    proj = _mm("in_proj", x, w["w_in"], "nn", ts, 1536, D, pro="rms", pro_g=sm["norm1_g"])
    gq, gk, gv, gf, bf, sqn, skn, svb, qmn = _pre_fwd(proj, w["conv_w"], alog_f, dtb_f, sm["sb_q_norm_g"],
                                                      sm["sb_k_norm_g"], sm["mem_q_norm_g"], S)
    ogdn, states = _gdn_fwd(gq, gk, gv, gf, bf, S)
    osb, sb_tot = _sb_fwd(sqn, skn, svb, S)
    kv = _mm("mem_kv", mem, w["w_mem_kv"], "nn", NMEM, D, D, pro="rms", pro_g=sm["mem_norm_g"])
    omem = _mem_fwd(qmn, kv, sm["mem_k_norm_g"], S)
    x1, mix = _merge_fwd(x, proj, ogdn, osb, omem, sm["gdn_norm_g"], w["w_br_gdn"], w["w_br_sb"], w["w_br_mem"],
                         w["w_o"], S)
    up = _mm("mlp_up", x1, w["w_up"], "nn", ts, 2048, D, pro="rms", pro_g=sm["norm2_g"])
    x2 = _mm("mlp_down", up, w["w_down"], "nn", ts, D, 1024, pro="relu2", epi="add", epi_x=x1)
    dy, loss = _loss_grad(x2, target, S)

    g = {}
    dup = _mm("d_up", dy, w["w_down"], "nt", ts, 1024, D, epi="drelu2", epi_x=up)
    g["w_down"] = _mm("dw_down", up, dy, "tn", 1024, D, 512, pro="relu2")
    g["w_up"] = _mm("dw_up", x1, dup, "tn", D, 1024, 512, pro="rms", pro_g=sm["norm2_g"])
    dh2 = _mm("d_h2", dup, w["w_up"], "nt", ts, D, 1024)
    dx1, g["norm2_g"] = _norm_bwd("norm2_bwd", dh2, x1, sm["norm2_g"], dy)

    dmix = _mm("d_mix", dx1, w["w_o"], "nt", ts, D, D)
    g["w_o"] = _mm("dw_o", mix, dx1, "tn", D, D, 512)
    (dgl0, dgl1, dgl2, dogdn, dz, dosb, domem, g["w_br_gdn"], g["w_br_sb"], g["w_br_mem"],
     g["gdn_norm_g"]) = _merge_bwd(dmix, proj, ogdn, osb, omem, sm["gdn_norm_g"], w["w_br_gdn"], w["w_br_sb"],
                                   w["w_br_mem"], S)
    dmq, dkv, g["mem_q_norm_g"], g["mem_k_norm_g"] = _mem_bwd(proj, qmn, kv, sm["mem_q_norm_g"], sm["mem_k_norm_g"],
                                                             domem, S)
    g["w_mem_kv"] = _mm("dw_mem_kv", mem, dkv, "tn", D, D, NMEM, pro="rms", pro_g=sm["mem_norm_g"])
    dmn = _mm("d_mem_n", dkv, w["w_mem_kv"], "nt", NMEM, D, D)
    _, g["mem_norm_g"] = _norm_bwd("mem_norm_bwd", dmn, mem, sm["mem_norm_g"], None)
    dsqn, dskn, dsv = _sb_bwd(sqn, skn, svb, dosb, sb_tot, S)
    dgq, dgk, dgv, dgf, dbf = _gdn_bwd(gq, gk, gv, gf, bf, states, dogdn, S)
    dc, dab, dsq, dsk, g["conv_w"], dal_f, ddt_f, g["sb_q_norm_g"], g["sb_k_norm_g"] = _pre_bwd(
        proj, w["conv_w"], alog_f, dtb_f, sm["sb_q_norm_g"], sm["sb_k_norm_g"], dgq, dgk, dgv, dgf, dbf, dsqn, dskn, S)
    g["a_log"] = dal_f.reshape(NH, DH)[:, 0].reshape(1, NH)
    g["dt_bias"] = ddt_f.reshape(NH, DH)[:, 0].reshape(1, NH)
    dqkv = _conv_bwd(dc, w["conv_w"], S)

    dproj = jnp.concatenate([dqkv, dz, dsq, dsk, dsv, dmq, dgl0, dgl1, dgl2, dab], axis=1)
    g["w_in"] = _mm("dw_in", x, dproj, "tn", D, 1536, 512, pro="rms", pro_g=sm["norm1_g"])
    dh = _mm("d_h", dproj, w["w_in"], "nt", ts, D, 1536)
    dx, g["norm1_g"] = _norm_bwd("norm1_bwd", dh, x, sm["norm1_g"], dx1)
    return loss[0, 0], dx, g


def kernel(x, mem, norm1_g, w_in, conv_w, a_log, dt_bias, gdn_norm_g, sb_q_norm_g, sb_k_norm_g, mem_norm_g, w_mem_kv, mem_q_norm_g, mem_k_norm_g, w_br_gdn, w_br_sb, w_br_mem, w_o, norm2_g, w_up, w_down, loss_target, m_norm1_g, m_w_in, m_conv_w, m_a_log, m_dt_bias, m_gdn_norm_g, m_sb_q_norm_g, m_sb_k_norm_g, m_mem_norm_g, m_w_mem_kv, m_mem_q_norm_g, m_mem_k_norm_g, m_w_br_gdn, m_w_br_sb, m_w_br_mem, m_w_o, m_norm2_g, m_w_up, m_w_down, v_norm1_g, v_w_in, v_conv_w, v_a_log, v_dt_bias, v_gdn_norm_g, v_sb_q_norm_g, v_sb_k_norm_g, v_mem_norm_g, v_w_mem_kv, v_mem_q_norm_g, v_mem_k_norm_g, v_w_br_gdn, v_w_br_sb, v_w_br_mem, v_w_o, v_norm2_g, v_w_up, v_w_down):
    given = dict(norm1_g=norm1_g, w_in=w_in, conv_w=conv_w, a_log=a_log, dt_bias=dt_bias, gdn_norm_g=gdn_norm_g,
                 sb_q_norm_g=sb_q_norm_g, sb_k_norm_g=sb_k_norm_g, mem_norm_g=mem_norm_g, w_mem_kv=w_mem_kv,
                 mem_q_norm_g=mem_q_norm_g, mem_k_norm_g=mem_k_norm_g, w_br_gdn=w_br_gdn, w_br_sb=w_br_sb,
                 w_br_mem=w_br_mem, w_o=w_o, norm2_g=norm2_g, w_up=w_up, w_down=w_down)
    mom1 = dict(norm1_g=m_norm1_g, w_in=m_w_in, conv_w=m_conv_w, a_log=m_a_log, dt_bias=m_dt_bias,
                gdn_norm_g=m_gdn_norm_g, sb_q_norm_g=m_sb_q_norm_g, sb_k_norm_g=m_sb_k_norm_g,
                mem_norm_g=m_mem_norm_g, w_mem_kv=m_w_mem_kv, mem_q_norm_g=m_mem_q_norm_g,
                mem_k_norm_g=m_mem_k_norm_g, w_br_gdn=m_w_br_gdn, w_br_sb=m_w_br_sb, w_br_mem=m_w_br_mem, w_o=m_w_o,
                norm2_g=m_norm2_g, w_up=m_w_up, w_down=m_w_down)
    mom2 = dict(norm1_g=v_norm1_g, w_in=v_w_in, conv_w=v_conv_w, a_log=v_a_log, dt_bias=v_dt_bias,
                gdn_norm_g=v_gdn_norm_g, sb_q_norm_g=v_sb_q_norm_g, sb_k_norm_g=v_sb_k_norm_g,
                mem_norm_g=v_mem_norm_g, w_mem_kv=v_w_mem_kv, mem_q_norm_g=v_mem_q_norm_g,
                mem_k_norm_g=v_mem_k_norm_g, w_br_gdn=v_w_br_gdn, w_br_sb=v_w_br_sb, w_br_mem=v_w_br_mem, w_o=v_w_o,
                norm2_g=v_norm2_g, w_up=v_w_up, w_down=v_w_down)
    shapes = {n: given[n].shape for n in WEIGHTS}

    w_loc = _pack_rows([given[n][0] for n in BIG], R_BIG)
    gathered = _exchange("gather_weights", w_loc.astype(BF16), True)
    w = _unpack_gathered(gathered[:, :sum(BIG_ROWS)])
    w["w_in"] = _pad_w_in(w["w_in"])
    conv_loc = jnp.pad(given["conv_w"][0].reshape(-1, LANES), ((0, 2), (0, 0)))
    conv_all = _exchange("gather_conv", conv_loc, True)
    w["conv_w"] = conv_all[:, :6].reshape(NDEV, 4, 3 * HW // NDEV).transpose(1, 0, 2).reshape(4, 3 * HW)
    sm = {n: given[n] for n in SMALL}

    loss, dx, g = _local_step(x[0], mem[0], loss_target[0], w, sm)
    g["w_in"] = _unpad_w_in(g["w_in"])

    g_all = _exchange("scatter_grads", _pack_full_grads(g), False)
    gb, db, mb, vb = _adamw("adamw_sharded", g_all, w_loc, _pack_rows([mom1[n][0] for n in BIG], R_BIG),
                            _pack_rows([mom2[n][0] for n in BIG], R_BIG))
    gs_all = _exchange("gather_small_grads", _pack_small(g), True)
    gs, dsm, ms, vs = _adamw("adamw_replicated", gs_all, _pack_small(given), _pack_small(mom1), _pack_small(mom2))

    outs = {}
    for prefix, big, small in (("grad_", gb, gs), ("delta_", db, dsm), ("new_m_", mb, ms), ("new_v_", vb, vs)):
        vals = _unpack_shard(big, shapes)
        vals.update(_unpack_small(small, shapes))
        for n in WEIGHTS:
            outs[prefix + n] = vals[n]
    loss = lax.psum(loss, ("x", "y", "c"))
    return (loss, dx[None], *[outs[p + n] for p in ("grad_", "delta_", "new_m_", "new_v_") for n in WEIGHTS])
```

```python
import jax
import jax.numpy as jnp
from jax import lax
from jax.experimental import pallas as pl
from jax.experimental.pallas import tpu as pltpu

F32 = jnp.float32
BF16 = jnp.bfloat16

D = 1024
NH = 4
DH = 128
HW = NH * DH
DFF = 4 * D
NMEM = 256
EPS = 1e-6
NDEV = 8
LANES = 128
PAIR = 128
CHUNK = 64
D_IN = 7176
D_INP = 7680
VMEM_LIMIT = 56 * 1024 * 1024

ADAM_LR, ADAM_B1, ADAM_B2, ADAM_EPS, ADAM_WD, ADAM_STEP = 0.001, 0.9, 0.999, 1e-08, 0.01, 10

CB_Z, CB_SQ, CB_SK, CB_SV, CB_MQ, CB_AB = 3, 4, 5, 6, 7, 14

NN = (((1,), (0,)), ((), ()))
NT = (((1,), (1,)), ((), ()))
TN = (((0,), (0,)), ((), ()))

BIG = ("w_in", "w_mem_kv", "w_br_gdn", "w_br_sb", "w_br_mem", "w_o", "w_up", "w_down", "conv_w")
BIG_ROWS = (7176, 1024, 512, 512, 512, 1024, 4096, 4096, 6)
R_BIG = 19456
SMALL = ("norm1_g", "a_log", "dt_bias", "gdn_norm_g", "sb_q_norm_g", "sb_k_norm_g", "mem_norm_g",
         "mem_q_norm_g", "mem_k_norm_g", "norm2_g")
SMALL_ROWS = (8, 1, 1, 1, 1, 1, 8, 1, 1, 8)
R_SMALL = 32
WEIGHTS = ("norm1_g", "w_in", "conv_w", "a_log", "dt_bias", "gdn_norm_g", "sb_q_norm_g", "sb_k_norm_g",
           "mem_norm_g", "w_mem_kv", "mem_q_norm_g", "mem_k_norm_g", "w_br_gdn", "w_br_sb", "w_br_mem",
           "w_o", "norm2_g", "w_up", "w_down")


def _cp(sem=None):
    return pltpu.CompilerParams(dimension_semantics=sem, vmem_limit_bytes=VMEM_LIMIT)


def _dot(a, b, dims=NN):
    return lax.dot_general(a, b, dims, preferred_element_type=F32)


def _dbf(a, b, dims=NN):
    return _dot(a.astype(BF16), b.astype(BF16), dims)


def _split(a, n):
    parts = []
    for _ in range(n):
        h = a.astype(BF16)
        parts.append(h)
        a = a - h.astype(F32)
    return parts


def _d3(a, b, dims=NN):
    ah, al = _split(a, 2)
    bh, bl = _split(b, 2)
    return _dot(ah, bh, dims) + (_dot(ah, bl, dims) + _dot(al, bh, dims))


def _dxr(a, e, dims=NN):
    eb = e.astype(BF16)
    a1, a2, a3 = _split(a, 3)
    return _dot(a1, eb, dims) + (_dot(a2, eb, dims) + _dot(a3, eb, dims))


def _dxl(e, a, dims=NN):
    eb = e.astype(BF16)
    a1, a2, a3 = _split(a, 3)
    return _dot(eb, a1, dims) + (_dot(eb, a2, dims) + _dot(eb, a3, dims))


def _sigmoid(x):
    return 1.0 / (1.0 + jnp.exp(-x))


def _softplus(x):
    return jnp.maximum(x, 0.0) + jnp.log(1.0 + jnp.exp(-jnp.abs(x)))


def _rms(x, g):
    r = lax.rsqrt(jnp.mean(x * x, axis=-1, keepdims=True) + EPS)
    return x * r * g, r


def _rms_bwd(dy, x, g, r):
    dyg = dy * g
    dx = r * (dyg - x * (r * r) * jnp.mean(dyg * x, axis=-1, keepdims=True))
    dg = jnp.sum(dy * (x * r), axis=0, keepdims=True)
    return dx, dg


def _hs(h):
    return slice(h * DH, (h + 1) * DH)


def _row_tile(s):
    return 512 if s >= 2048 else 256


def _narrow_tile(s):
    return min(256, s)


def _mm(name, a, b, mode, tm, tn, tk, pro=None, pro_g=None, epi=None, epi_x=None):
    if mode == "tn":
        K, M = a.shape
    else:
        M, K = a.shape
    N = b.shape[0] if mode == "nt" else b.shape[1]
    tm, tn, tk = min(tm, M), min(tn, N), min(tk, K)
    nk = K // tk
    assert M % tm == 0 and N % tn == 0 and K % tk == 0, (name, M, N, K, tm, tn, tk)
    dims = {"nn": NN, "nt": NT, "tn": TN}[mode]

    def body(*refs):
        a_ref, b_ref = refs[0], refs[1]
        pos = 2
        g_ref = e_ref = None
        if pro == "rms":
            g_ref = refs[pos]
            pos += 1
        if epi is not None:
            e_ref = refs[pos]
            pos += 1
        o_ref = refs[pos]
        av = a_ref[...]
        if pro == "rms":
            av, _ = _rms(av.astype(F32), g_ref[...])
        elif pro == "relu2":
            av = jnp.square(jnp.maximum(av, 0.0))
        part = _dbf(av, b_ref[...], dims)

        def finish(acc):
            if epi == "add":
                acc = acc + e_ref[...]
            elif epi == "drelu2":
                acc = acc * (2.0 * jnp.maximum(e_ref[...], 0.0))
            o_ref[...] = acc

        if nk == 1:
            finish(part)
        else:
            acc_ref = refs[pos + 1]
            k = pl.program_id(2)

            @pl.when(k == 0)
            def _():
                acc_ref[...] = part

            @pl.when(k > 0)
            def _():
                acc_ref[...] += part

            @pl.when(k == nk - 1)
            def _():
                finish(acc_ref[...])

    if mode == "tn":
        a_spec = pl.BlockSpec((tk, tm), lambda i, j, k: (k, i))
    else:
        a_spec = pl.BlockSpec((tm, tk), lambda i, j, k: (i, k))
    if mode == "nt":
        b_spec = pl.BlockSpec((tn, tk), lambda i, j, k: (j, k))
    else:
        b_spec = pl.BlockSpec((tk, tn), lambda i, j, k: (k, j))
    in_specs, ops = [a_spec, b_spec], [a, b]
    if pro == "rms":
        w = pro_g.shape[1]
        assert (tm if mode == "tn" else tk) == w, name
        in_specs.append(pl.BlockSpec((1, w), lambda i, j, k: (0, 0)))
        ops.append(pro_g)
    if epi is not None:
        in_specs.append(pl.BlockSpec((tm, tn), lambda i, j, k: (i, j)))
        ops.append(epi_x)
    return pl.pallas_call(
        body, name=name, grid=(M // tm, N // tn, nk),
        in_specs=in_specs, out_specs=pl.BlockSpec((tm, tn), lambda i, j, k: (i, j)),
        out_shape=jax.ShapeDtypeStruct((M, N), F32),
        scratch_shapes=[pltpu.VMEM((tm, tn), F32)] if nk > 1 else [],
        compiler_params=_cp(("parallel", "parallel", "arbitrary")),
    )(*ops)


def _head_select(first_lane):
    l = lax.broadcasted_iota(jnp.int32, (LANES, HW), 0)
    c = lax.broadcasted_iota(jnp.int32, (LANES, HW), 1)
    return (l == first_lane + c // DH).astype(F32)


def _conv_taps(buf, cw, ts):
    c = cw[3:4, :] * buf[8:8 + ts, :]
    for j in range(3):
        k = 3 - j
        c = c + cw[j:j + 1, :] * buf[8 - k:8 - k + ts, :]
    return c


def _pre_fwd(proj, conv_w, alog_f, dtb_f, gsq, gsk, gmq, S):
    ts = _narrow_tile(S)
    hb = ts // 8

    def body(qkv_ref, halo_ref, ab_ref, sq_ref, sk_ref, sv_ref, mq_ref, cw_ref, al_ref, dt_ref, gsq_ref, gsk_ref,
             gmq_ref, gq_o, gk_o, gv_o, gf_o, bf_o, sqn_o, skn_o, svb_o, qmn_o, buf):
        i = pl.program_id(0)
        buf[0:8, :] = jnp.where(i == 0, 0.0, halo_ref[...])
        buf[8:8 + ts, :] = qkv_ref[...]
        c = _conv_taps(buf, cw_ref[...], ts)
        a = c * _sigmoid(c)
        for h in range(NH):
            q = a[:, h * DH:(h + 1) * DH]
            k = a[:, HW + h * DH:HW + (h + 1) * DH]
            gq_o[:, _hs(h)] = q * (lax.rsqrt(jnp.sum(q * q, axis=-1, keepdims=True) + EPS) * DH ** -0.5)
            gk_o[:, _hs(h)] = k * lax.rsqrt(jnp.sum(k * k, axis=-1, keepdims=True) + EPS)
            sqn_o[:, _hs(h)] = _rms(sq_ref[:, _hs(h)], gsq_ref[...])[0].astype(BF16)
            skn_o[:, _hs(h)] = _rms(sk_ref[:, _hs(h)], gsk_ref[...])[0].astype(BF16)
            qmn_o[:, _hs(h)] = _rms(mq_ref[:, _hs(h)], gmq_ref[...])[0].astype(BF16)
        gv_o[...] = a[:, 2 * HW:3 * HW]
        svb_o[...] = sv_ref[...].astype(BF16)
        ab = ab_ref[:, 0:LANES]
        a_bc = _dxr(ab, _head_select(0))
        b_bc = _dxr(ab, _head_select(NH))
        gf_o[...] = -jnp.exp(al_ref[...]) * _softplus(a_bc + dt_ref[...])
        bf_o[...] = _sigmoid(b_bc)

    row = lambda cb: pl.BlockSpec((ts, HW), lambda i: (i, cb))
    full = lambda r, c: pl.BlockSpec((r, c), lambda i: (0, 0))
    f32o = jax.ShapeDtypeStruct((S, HW), F32)
    bfo = jax.ShapeDtypeStruct((S, HW), BF16)
    return pl.pallas_call(
        body, name="pre_fwd", grid=(S // ts,),
        in_specs=[pl.BlockSpec((ts, 3 * HW), lambda i: (i, 0)),
                  pl.BlockSpec((8, 3 * HW), lambda i: (jnp.maximum(i * hb - 1, 0), 0)),
                  row(CB_AB), row(CB_SQ), row(CB_SK), row(CB_SV), row(CB_MQ),
                  full(4, 3 * HW), full(1, HW), full(1, HW), full(1, DH), full(1, DH), full(1, DH)],
        out_specs=[pl.BlockSpec((ts, HW), lambda i: (i, 0))] * 9,
        out_shape=[f32o, f32o, f32o, f32o, f32o, bfo, bfo, bfo, bfo],
        scratch_shapes=[pltpu.VMEM((ts + 8, 3 * HW), F32)],
        compiler_params=_cp(("parallel",)),
    )(proj, proj, proj, proj, proj, proj, proj, conv_w, alog_f, dtb_f, gsq, gsk, gmq)


def _pre_bwd(proj, conv_w, alog_f, dtb_f, gsq, gsk, dgq, dgk, dgv, dgf, dbf, dsqn, dskn, S):
    ts = _narrow_tile(S)
    hb = ts // 8

    def body(qkv_ref, halo_ref, ab_ref, sq_ref, sk_ref, cw_ref, al_ref, dt_ref, gsq_ref, gsk_ref,
             dgq_ref, dgk_ref, dgv_ref, dgf_ref, dbf_ref, dsqn_ref, dskn_ref,
             dc_o, dab_o, dsq_o, dsk_o, dcw_o, dal_o, ddt_o, dgsq_o, dgsk_o, buf):
        i = pl.program_id(0)

        @pl.when(i == 0)
        def _():
            dcw_o[...] = jnp.zeros_like(dcw_o)
            dal_o[...] = jnp.zeros_like(dal_o)
            ddt_o[...] = jnp.zeros_like(ddt_o)
            dgsq_o[...] = jnp.zeros_like(dgsq_o)
            dgsk_o[...] = jnp.zeros_like(dgsk_o)

        buf[0:8, :] = jnp.where(i == 0, 0.0, halo_ref[...])
        buf[8:8 + ts, :] = qkv_ref[...]
        c = _conv_taps(buf, cw_ref[...], ts)
        sg = _sigmoid(c)
        a = c * sg
        dsilu = sg * (1.0 + c * (1.0 - sg))
        dgsq = jnp.zeros((1, DH), F32)
        dgsk = jnp.zeros((1, DH), F32)
        for h in range(NH):
            q = a[:, h * DH:(h + 1) * DH]
            k = a[:, HW + h * DH:HW + (h + 1) * DH]
            nq = lax.rsqrt(jnp.sum(q * q, axis=-1, keepdims=True) + EPS)
            nk = lax.rsqrt(jnp.sum(k * k, axis=-1, keepdims=True) + EPS)
            dyq = dgq_ref[:, _hs(h)]
            dyk = dgk_ref[:, _hs(h)]
            dq = (nq * dyq - q * (nq * nq * nq) * jnp.sum(dyq * q, axis=-1, keepdims=True)) * DH ** -0.5
            dk = nk * dyk - k * (nk * nk * nk) * jnp.sum(dyk * k, axis=-1, keepdims=True)
            dc_o[:, h * DH:(h + 1) * DH] = dq * dsilu[:, h * DH:(h + 1) * DH]
            dc_o[:, HW + h * DH:HW + (h + 1) * DH] = dk * dsilu[:, HW + h * DH:HW + (h + 1) * DH]
            x = sq_ref[:, _hs(h)]
            _, r = _rms(x, gsq_ref[...])
            dx, dg = _rms_bwd(dsqn_ref[:, _hs(h)], x, gsq_ref[...], r)
            dsq_o[:, _hs(h)] = dx
            dgsq = dgsq + dg
            x = sk_ref[:, _hs(h)]
            _, r = _rms(x, gsk_ref[...])
            dx, dg = _rms_bwd(dskn_ref[:, _hs(h)], x, gsk_ref[...], r)
            dsk_o[:, _hs(h)] = dx
            dgsk = dgsk + dg
        dc_o[:, 2 * HW:3 * HW] = dgv_ref[...] * dsilu[:, 2 * HW:3 * HW]
        dgsq_o[...] += dgsq
        dgsk_o[...] += dgsk
        dc = dc_o[...]
        for j in range(4):
            k = 3 - j
            dcw_o[j:j + 1, :] += jnp.sum(dc * buf[8 - k:8 - k + ts, :], axis=0, keepdims=True)
        ab = ab_ref[:, 0:LANES]
        a_bc = _dxr(ab, _head_select(0))
        b_bc = _dxr(ab, _head_select(NH))
        pre = a_bc + dt_ref[...]
        ea = jnp.exp(al_ref[...])
        dgf = dgf_ref[...]
        dal_o[...] += jnp.sum(dgf * (-ea * _softplus(pre)), axis=0, keepdims=True)
        da = dgf * (-ea * _sigmoid(pre))
        ddt_o[...] += jnp.sum(da, axis=0, keepdims=True)
        beta = _sigmoid(b_bc)
        db = dbf_ref[...] * beta * (1.0 - beta)
        lane = lax.broadcasted_iota(jnp.int32, (ts, LANES), 1)
        dab = jnp.zeros((ts, LANES), F32)
        for h in range(NH):
            dab = dab + jnp.where(lane == h, da[:, _hs(h)], 0.0) + jnp.where(lane == NH + h, db[:, _hs(h)], 0.0)
        dab_o[:, 0:LANES] = dab
        dab_o[:, LANES:HW] = jnp.zeros((ts, HW - LANES), F32)

    row = lambda cb: pl.BlockSpec((ts, HW), lambda i: (i, cb))
    full = lambda r, c: pl.BlockSpec((r, c), lambda i: (0, 0))
    t512 = pl.BlockSpec((ts, HW), lambda i: (i, 0))
    return pl.pallas_call(
        body, name="pre_bwd", grid=(S // ts,),
        in_specs=[pl.BlockSpec((ts, 3 * HW), lambda i: (i, 0)),
                  pl.BlockSpec((8, 3 * HW), lambda i: (jnp.maximum(i * hb - 1, 0), 0)),
                  row(CB_AB), row(CB_SQ), row(CB_SK),
                  full(4, 3 * HW), full(1, HW), full(1, HW), full(1, DH), full(1, DH)] + [t512] * 7,
        out_specs=[pl.BlockSpec((ts, 3 * HW), lambda i: (i, 0)), t512, t512, t512,
                   full(4, 3 * HW), full(1, HW), full(1, HW), full(1, DH), full(1, DH)],
        out_shape=[jax.ShapeDtypeStruct((S, 3 * HW), F32)] + [jax.ShapeDtypeStruct((S, HW), F32)] * 3
        + [jax.ShapeDtypeStruct((4, 3 * HW), F32), jax.ShapeDtypeStruct((1, HW), F32),
           jax.ShapeDtypeStruct((1, HW), F32), jax.ShapeDtypeStruct((1, DH), F32),
           jax.ShapeDtypeStruct((1, DH), F32)],
        scratch_shapes=[pltpu.VMEM((ts + 8, 3 * HW), F32)],
        compiler_params=_cp(("arbitrary",)),
    )(proj, proj, proj, proj, proj, conv_w, alog_f, dtb_f, gsq, gsk, dgq, dgk, dgv, dgf, dbf, dsqn, dskn)


def _conv_bwd(dc, conv_w, S):
    ts = _row_tile(S)
    hb = ts // 8
    n = S // ts

    def body(dc_ref, halo_ref, cw_ref, o_ref, buf):
        i = pl.program_id(0)
        buf[0:ts, :] = dc_ref[...]
        buf[ts:ts + 8, :] = jnp.where(i == n - 1, 0.0, halo_ref[...])
        cw = cw_ref[...]
        acc = cw[3:4, :] * buf[0:ts, :]
        for k in range(1, 4):
            acc = acc + cw[3 - k:4 - k, :] * buf[k:k + ts, :]
        o_ref[...] = acc

    return pl.pallas_call(
        body, name="conv_bwd", grid=(n,),
        in_specs=[pl.BlockSpec((ts, 3 * HW), lambda i: (i, 0)),
                  pl.BlockSpec((8, 3 * HW), lambda i: (jnp.minimum((i + 1) * hb, S // 8 - 1), 0)),
                  pl.BlockSpec((4, 3 * HW), lambda i: (0, 0))],
        out_specs=pl.BlockSpec((ts, 3 * HW), lambda i: (i, 0)),
        out_shape=jax.ShapeDtypeStruct((S, 3 * HW), F32),
        scratch_shapes=[pltpu.VMEM((ts + 8, 3 * HW), F32)],
        compiler_params=_cp(("parallel",)),
    )(dc, dc, conv_w)


def _gdn_masks():
    r = lax.broadcasted_iota(jnp.int32, (PAIR, PAIR), 0)
    c = lax.broadcasted_iota(jnp.int32, (PAIR, PAIR), 1)
    same = ((r >= CHUNK) & (c >= CHUNK)) | ((r < CHUNK) & (c < CHUNK))
    return dict(r=r, same=same, tril=same & (r >= c), strict=same & (r > c), triu=same & (c >= r), eye=r == c,
                in_a=r < CHUNK, last_a=r == CHUNK - 1, last_b=r == PAIR - 1)


def _each(fn, *cols):
    return [fn(*xs) for xs in zip(*cols)]


def _mul(a, b):
    return a * b


def _top(x):
    return x[:CHUNK]


def _bot(x):
    return x[CHUNK:]


def _rows(a, b):
    return jnp.concatenate([a, b], axis=0)


def _tri_inv(lm, eye):
    eye_f = eye.astype(F32)
    p = _each(lambda l: eye_f - l, lm)
    lp = _each(lambda l: _d3(l, l), lm)
    for it in range(5):
        p = _each(lambda a, b: a + _d3(a, b), p, lp)
        if it < 4:
            lp = _each(lambda b: _d3(b, b), lp)
    return p


def _gdn_block(m, q, k, v, g, beta):
    tril_f = m["tril"].astype(F32)
    col_sum = lambda mask: (lambda x: jnp.sum(jnp.where(mask, x, 0.0), axis=0, keepdims=True))
    gc = _each(lambda x: _dxl(tril_f, x), g)
    gcr = _each(col_sum(m["eye"]), gc)
    gam = _each(lambda a, b: jnp.where(m["tril"], jnp.exp(jnp.minimum(a - b, 0.0)), 0.0), gc, gcr)
    kb = _each(_mul, k, beta)
    vb = _each(_mul, v, beta)
    lm = _each(lambda a, b, c: jnp.where(m["strict"], _d3(a, b, NT) * c, 0.0), kb, k, gam)
    t = _tri_inv(lm, m["eye"])
    eg = _each(jnp.exp, gc)
    kbe = _each(_mul, kb, eg)
    u = _each(_d3, t, vb)
    w = _each(_d3, t, kbe)
    aqk = _each(lambda a, b, c: jnp.where(m["tril"], _d3(a, b, NT) * c, 0.0), q, k, gam)
    qd = _each(_mul, q, eg)
    ga = _each(col_sum(m["last_a"]), gc)
    gb = _each(col_sum(m["last_b"]), gc)
    e2 = _each(lambda a, b, c: jnp.exp(jnp.where(m["in_a"], a, b) - c), ga, gb, gc)
    kd = _each(_mul, k, e2)
    return dict(u=u, w=w, aqk=aqk, qd=qd, kd=kd, gam=gam, kb=kb, vb=vb, lm=lm, t=t, eg=eg, kbe=kbe, e2=e2,
                gla=_each(jnp.exp, ga), glb=_each(jnp.exp, gb))


def _gdn_fwd(gq, gk, gv, gf, bf, S):
    nb = S // PAIR

    def body(q_ref, k_ref, v_ref, g_ref, b_ref, o_ref, st_ref, s_scr):
        @pl.when(pl.program_id(0) == 0)
        def _():
            s_scr[...] = jnp.zeros_like(s_scr)

        m = _gdn_masks()
        heads = lambda ref: [ref[:, _hs(h)] for h in range(NH)]
        f = _gdn_block(m, heads(q_ref), heads(k_ref), heads(v_ref), heads(g_ref), heads(b_ref))
        u, w, qd, kd = f["u"], f["w"], f["qd"], f["kd"]
        s0 = [s_scr[h * DH:(h + 1) * DH, :] for h in range(NH)]
        vna = _each(lambda a, b, s: _top(a) - _d3(_top(b), s), u, w, s0)
        oa = _each(lambda a, s: _d3(_top(a), s), qd, s0)
        s1 = _each(lambda s, gl, a, vn: s * gl + _d3(_top(a), vn, TN), s0, f["gla"], kd, vna)
        vnb = _each(lambda a, b, s: _bot(a) - _d3(_bot(b), s), u, w, s1)
        ob = _each(lambda a, s: _d3(_bot(a), s), qd, s1)
        s2 = _each(lambda s, gl, a, vn: s * gl + _d3(_bot(a), vn, TN), s1, f["glb"], kd, vnb)
        outs = _each(lambda a, b, c, va, vb: _rows(a, b) + _d3(c, _rows(va, vb)), oa, ob, f["aqk"], vna, vnb)
        o_ref[...] = jnp.concatenate(outs, axis=1)
        st_ref[...] = jnp.concatenate(s0 + s1, axis=0)
        s_scr[...] = jnp.concatenate(s2, axis=0)

    blk = pl.BlockSpec((PAIR, HW), lambda i: (i, 0))
    return pl.pallas_call(
        body, name="gdn_fwd", grid=(nb,),
        in_specs=[blk] * 5,
        out_specs=[blk, pl.BlockSpec((2 * NH * DH, DH), lambda i: (i, 0))],
        out_shape=[jax.ShapeDtypeStruct((S, HW), F32), jax.ShapeDtypeStruct((nb * 2 * NH * DH, DH), F32)],
        scratch_shapes=[pltpu.VMEM((NH * DH, DH), F32)],
        compiler_params=_cp(("arbitrary",)),
    )(gq, gk, gv, gf, bf)


def _gdn_bwd(gq, gk, gv, gf, bf, states, do, S):
    nb = S // PAIR

    def body(q_ref, k_ref, v_ref, g_ref, b_ref, st_ref, do_ref, dq_o, dk_o, dv_o, dg_o, db_o, ds_scr):
        @pl.when(pl.program_id(0) == 0)
        def _():
            ds_scr[...] = jnp.zeros_like(ds_scr)

        m = _gdn_masks()
        ones = jnp.ones((PAIR, PAIR), F32)
        heads = lambda ref: [ref[:, _hs(h)] for h in range(NH)]
        q, k, v, beta, do = heads(q_ref), heads(k_ref), heads(v_ref), heads(b_ref), heads(do_ref)
        f = _gdn_block(m, q, k, v, heads(g_ref), beta)
        u, w, aqk, qd, kd, t = f["u"], f["w"], f["aqk"], f["qd"], f["kd"], f["t"]
        s0 = [st_ref[h * DH:(h + 1) * DH, :] for h in range(NH)]
        s1 = [st_ref[(NH + h) * DH:(NH + h + 1) * DH, :] for h in range(NH)]
        ds2 = [ds_scr[h * DH:(h + 1) * DH, :] for h in range(NH)]
        total = lambda a, b: jnp.sum(jnp.sum(a * b, axis=1, keepdims=True), axis=0, keepdims=True)
        vna = _each(lambda a, b, s: _top(a) - _d3(_top(b), s), u, w, s0)
        vnb = _each(lambda a, b, s: _bot(a) - _d3(_bot(b), s), u, w, s1)
        dvn_i = _each(lambda a, b: _d3(a, b, TN), aqk, do)
        dvnb = _each(lambda a, b, s: _bot(a) + _d3(_bot(b), s), dvn_i, kd, ds2)
        dqdb = _each(lambda a, s: _d3(_bot(a), s, NT), do, s1)
        dkdb = _each(lambda a, s: _d3(a, s, NT), vnb, ds2)
        dglb = _each(total, ds2, s1)
        dwb = _each(lambda a, s: -_d3(a, s, NT), dvnb, s1)
        ds1 = _each(lambda s, gl, a, b, c, d: s * gl + _d3(_bot(a), _bot(b), TN) - _d3(_bot(c), d, TN),
                    ds2, f["glb"], qd, do, w, dvnb)
        dvna = _each(lambda a, b, s: _top(a) + _d3(_top(b), s), dvn_i, kd, ds1)
        dqda = _each(lambda a, s: _d3(_top(a), s, NT), do, s0)
        dkda = _each(lambda a, s: _d3(a, s, NT), vna, ds1)
        dgla = _each(total, ds1, s0)
        dwa = _each(lambda a, s: -_d3(a, s, NT), dvna, s0)
        ds0 = _each(lambda s, gl, a, b, c, d: s * gl + _d3(_top(a), _top(b), TN) - _d3(_top(c), d, TN),
                    ds1, f["gla"], qd, do, w, dvna)
        dvn, dqd, dkd, dw = (_each(_rows, a, b) for a, b in ((dvna, dvnb), (dqda, dqdb), (dkda, dkdb), (dwa, dwb)))
        daqk = _each(lambda a, va, vb: jnp.where(m["tril"], _d3(a, _rows(va, vb), NT), 0.0), do, vna, vnb)
        dt = _each(lambda a, b, c, d: _d3(a, b, NT) + _d3(c, d, NT), dvn, f["vb"], dw, f["kbe"])
        dvb = _each(lambda a, b: _d3(a, b, TN), t, dvn)
        dkbe = _each(lambda a, b: _d3(a, b, TN), t, dw)
        dtt = _each(lambda a, b: _d3(a, b, NT), dt, t)
        dl = _each(lambda a, b: -jnp.where(m["strict"], _d3(a, b, TN), 0.0), t, dtt)
        dm = _each(_mul, dl, f["gam"])
        dn = _each(_mul, daqk, f["gam"])
        dkb = _each(lambda a, b, c, d: _d3(a, b) + c * d, dm, k, dkbe, f["eg"])
        dks = _each(lambda a, b, c, d, e, g, h, i: _d3(a, b, TN) + _d3(c, d, TN) + e * g + h * i,
                    dm, f["kb"], dn, q, dkd, f["e2"], beta, dkb)
        dqs = _each(lambda a, b, c, d: _d3(a, b) + c * d, dn, k, dqd, f["eg"])
        gm = _each(lambda a, b, c, d: a * b + c * d, dl, f["lm"], daqk, aqk)
        dkdkd = _each(_mul, dkd, kd)
        dgc = _each(lambda a, b, c, d, e, g: _dxr(a + b * c + d * e - g, ones) - _dxr(a, ones, TN),
                    gm, dqd, qd, dkbe, f["kbe"], dkdkd)
        same_f = m["same"].astype(F32)
        chunk_tot = _each(lambda a: _dxl(same_f, _dxr(a, ones)), dkdkd)
        last = m["last_a"] | m["last_b"]
        dgc = _each(lambda a, b, ga, gla, gb, glb: a + jnp.where(last, b + jnp.where(m["in_a"], ga * gla, gb * glb), 0.0),
                    dgc, chunk_tot, dgla, f["gla"], dglb, f["glb"])
        dbs = _each(lambda a, b, c, d: _dxr(a * b + c * d, ones), dkb, k, dvb, v)
        dvs = _each(_mul, beta, dvb)
        triu_f = m["triu"].astype(F32)
        dgs = _each(lambda a: _dxl(triu_f, a), dgc)
        for ref, parts in ((dq_o, dqs), (dk_o, dks), (dv_o, dvs), (dg_o, dgs), (db_o, dbs)):
            ref[...] = jnp.concatenate(parts, axis=1)
        ds_scr[...] = jnp.concatenate(ds0, axis=0)

    blk = pl.BlockSpec((PAIR, HW), lambda i: (nb - 1 - i, 0))
    o = jax.ShapeDtypeStruct((S, HW), F32)
    return pl.pallas_call(
        body, name="gdn_bwd", grid=(nb,),
        in_specs=[blk] * 5 + [pl.BlockSpec((2 * NH * DH, DH), lambda i: (nb - 1 - i, 0)), blk],
        out_specs=[blk] * 5, out_shape=[o] * 5,
        scratch_shapes=[pltpu.VMEM((NH * DH, DH), F32)],
        compiler_params=_cp(("arbitrary",)),
    )(gq, gk, gv, gf, bf, states, do)


SB_T = 256


def _sb_iotas(t):
    return lax.broadcasted_iota(jnp.int32, (t, t), 0), lax.broadcasted_iota(jnp.int32, (t, t), 1)


def _sb_scores(q, k, mask):
    z = _dot(q, k, NT) * DH ** -0.5
    ls = jnp.minimum(z, 0.0) - jnp.log(1.0 + jnp.exp(-jnp.abs(z)))
    lneg = ls - z
    if mask is not None:
        lneg = jnp.where(mask, lneg, 0.0)
    return ls, lneg


def _prefix(x, u):
    xh, xl = _split(x, 2)
    return _dot(xh, u) + _dot(xl, u)


def _sb_fwd(sqn, skn, svb, S):
    t = min(SB_T, S)

    def body(q_ref, k_ref, v_ref, o_ref, t_ref):
        qb = pl.program_id(1)
        q = q_ref[...]
        r, c = _sb_iotas(t)
        diag = c < r
        u_after = (r > c).astype(BF16)

        def tiles(k0s, run, mask):
            sc = _each(lambda k0: _sb_scores(q, k_ref[pl.ds(k0, t), :], mask), k0s)
            ls, lneg = [s[0] for s in sc], [s[1] for s in sc]
            sums = _each(lambda x: jnp.sum(x, axis=1, keepdims=True), lneg)
            pre = _each(lambda x: _prefix(x, u_after), lneg)
            runs = [run]
            for s in sums:
                runs.append(runs[-1] + s)
            att = _each(lambda a, b, rn: jnp.exp(a + (rn + b)), ls, pre, runs[:-1])
            if mask is not None:
                att = _each(lambda a: jnp.where(mask, a, 0.0), att)
            parts = _each(lambda a, k0: _dot(a.astype(BF16), v_ref[pl.ds(k0, t), :]), att, k0s)
            return sum(parts[1:], parts[0]), runs[-1]

        acc, run = tiles([pl.multiple_of(qb * t, t)], jnp.zeros((t, 1), F32), diag)

        def pair(i, carry):
            acc, run = carry
            part, run = tiles([pl.multiple_of((qb - 1 - 2 * i) * t, t), pl.multiple_of((qb - 2 - 2 * i) * t, t)],
                              run, None)
            return acc + part, run

        def single(i, carry):
            acc, run = carry
            part, run = tiles([0], run, None)
            return acc + part, run

        acc, run = lax.fori_loop(0, qb // 2, pair, (acc, run))
        acc, run = lax.fori_loop(0, qb % 2, single, (acc, run))
        o_ref[...] = acc
        t_ref[...] = jnp.broadcast_to(run, (t, DH))

    qspec = pl.BlockSpec((t, DH), lambda h, i: (i, h))
    kspec = pl.BlockSpec((S, DH), lambda h, i: (0, h))
    return pl.pallas_call(
        body, name="sb_fwd", grid=(NH, S // t),
        in_specs=[qspec, kspec, kspec], out_specs=[qspec, qspec],
        out_shape=[jax.ShapeDtypeStruct((S, HW), F32)] * 2,
        compiler_params=_cp(("parallel", "arbitrary")),
    )(sqn, skn, svb)


def _sb_bwd(sqn, skn, svb, do, tot, S):
    t = min(SB_T, S)

    def body(q_ref, k_ref, v_ref, do_ref, t_ref, dq_o, dk_o, dv_o):
        qb = pl.program_id(1)

        @pl.when(qb == 0)
        def _():
            dk_o[...] = jnp.zeros_like(dk_o)
            dv_o[...] = jnp.zeros_like(dv_o)

        q = q_ref[...]
        do = do_ref[...].astype(BF16)
        tot_l = jnp.concatenate([t_ref[...]] * (t // DH), axis=1)
        r, c = _sb_iotas(t)
        diag = c < r
        u_upto = (r <= c).astype(BF16)
        u_before = (r < c).astype(BF16)

        def tiles(k0s, run_l, run_e, mask):
            rowsum = lambda x: jnp.sum(x, axis=1, keepdims=True)
            ks = [k_ref[pl.ds(k0, t), :] for k0 in k0s]
            vs = [v_ref[pl.ds(k0, t), :] for k0 in k0s]
            sc = _each(lambda k: _sb_scores(q, k, mask), ks)
            ls, lneg = [s[0] for s in sc], [s[1] for s in sc]
            sums_l = _each(rowsum, lneg)
            pre_l = _each(lambda x: _prefix(x, u_upto), lneg)
            runs_l = [run_l]
            for s in sums_l:
                runs_l.append(runs_l[-1] + s)
            att = _each(lambda a, b, rn: jnp.exp(a + (tot_l - (rn + b))), ls, pre_l, runs_l[:-1])
            if mask is not None:
                att = _each(lambda a: jnp.where(mask, a, 0.0), att)
            e = _each(lambda v, a: _dot(do, v, NT) * a, vs, att)
            sums_e = _each(rowsum, e)
            pre_e = _each(lambda x: _prefix(x, u_before), e)
            runs_e = [run_e]
            for s in sums_e:
                runs_e.append(runs_e[-1] + s)
            sg = _each(jnp.exp, ls)
            dz = _each(lambda a, b, rn, s: a * (1.0 - s) - (rn + b) * s, e, pre_e, runs_e[:-1], sg)
            if mask is not None:
                dz = _each(lambda a: jnp.where(mask, a, 0.0), dz)
            dz = _each(lambda a: (a * DH ** -0.5).astype(BF16), dz)
            dvs = _each(lambda a: _dot(a.astype(BF16), do, TN), att)
            dks = _each(lambda a: _dot(a, q, TN), dz)
            dqs = _each(_dot, dz, ks)
            for k0, dv, dk in zip(k0s, dvs, dks):
                dv_o[pl.ds(k0, t), :] += dv
                dk_o[pl.ds(k0, t), :] += dk
            return sum(dqs[1:], dqs[0]), runs_l[-1], runs_e[-1]

        def pair(i, carry):
            dq, run_l, run_e = carry
            part, run_l, run_e = tiles([pl.multiple_of(2 * i * t, t), pl.multiple_of((2 * i + 1) * t, t)],
                                       run_l, run_e, None)
            return dq + part, run_l, run_e

        def single(i, carry):
            dq, run_l, run_e = carry
            part, run_l, run_e = tiles([pl.multiple_of((qb - 1) * t, t)], run_l, run_e, None)
            return dq + part, run_l, run_e

        z1 = jnp.zeros((t, 1), F32)
        carry = lax.fori_loop(0, qb // 2, pair, (jnp.zeros((t, DH), F32), z1, z1))
        dq, run_l, run_e = lax.fori_loop(0, qb % 2, single, carry)
        part, _, _ = tiles([pl.multiple_of(qb * t, t)], run_l, run_e, diag)
        dq_o[...] = dq + part

    qspec = pl.BlockSpec((t, DH), lambda h, i: (i, h))
    kspec = pl.BlockSpec((S, DH), lambda h, i: (0, h))
    o = jax.ShapeDtypeStruct((S, HW), F32)
    return pl.pallas_call(
        body, name="sb_bwd", grid=(NH, S // t),
        in_specs=[qspec, kspec, kspec, qspec, qspec], out_specs=[qspec, kspec, kspec],
        out_shape=[o, o, o],
        compiler_params=_cp(("parallel", "arbitrary")),
    )(sqn, skn, svb, do, tot)


def _mem_probs(qn, kn):
    s = _dot(qn, kn.astype(BF16), NT) * DH ** -0.5
    p = jnp.exp(s - jnp.max(s, axis=-1, keepdims=True))
    return p / jnp.sum(p, axis=-1, keepdims=True)


def _mem_fwd(qmn, kv, gmk, S):
    ts = _row_tile(S)

    def body(q_ref, kv_ref, gk_ref, o_ref):
        for h in range(NH):
            kn, _ = _rms(kv_ref[:, _hs(h)], gk_ref[...])
            p = _mem_probs(q_ref[:, _hs(h)], kn)
            o_ref[:, _hs(h)] = _dbf(p, kv_ref[:, HW + h * DH:HW + (h + 1) * DH])

    return pl.pallas_call(
        body, name="mem_fwd", grid=(S // ts,),
        in_specs=[pl.BlockSpec((ts, HW), lambda i: (i, 0)), pl.BlockSpec((NMEM, 2 * HW), lambda i: (0, 0)),
                  pl.BlockSpec((1, DH), lambda i: (0, 0))],
        out_specs=pl.BlockSpec((ts, HW), lambda i: (i, 0)),
        out_shape=jax.ShapeDtypeStruct((S, HW), F32),
        compiler_params=_cp(("parallel",)),
    )(qmn, kv, gmk)


def _mem_bwd(proj, qmn, kv, gmq, gmk, do, S):
    ts = _row_tile(S)
    n = S // ts

    def body(mq_ref, q_ref, kv_ref, gq_ref, gk_ref, do_ref, dmq_o, dkv_o, dgq_o, dgk_o, dkn_scr):
        i = pl.program_id(0)

        @pl.when(i == 0)
        def _():
            dkv_o[...] = jnp.zeros_like(dkv_o)
            dgq_o[...] = jnp.zeros_like(dgq_o)
            dkn_scr[...] = jnp.zeros_like(dkn_scr)

        dgq = jnp.zeros((1, DH), F32)
        for h in range(NH):
            km = kv_ref[:, _hs(h)]
            vm = kv_ref[:, HW + h * DH:HW + (h + 1) * DH].astype(BF16)
            kn, _ = _rms(km, gk_ref[...])
            qn = q_ref[:, _hs(h)]
            p = _mem_probs(qn, kn)
            dob = do_ref[:, _hs(h)].astype(BF16)
            dkv_o[:, HW + h * DH:HW + (h + 1) * DH] += _dot(p.astype(BF16), dob, TN)
            dp = _dot(dob, vm, NT)
            dsc = (p * (dp - jnp.sum(dp * p, axis=-1, keepdims=True)) * DH ** -0.5).astype(BF16)
            dkn_scr[:, _hs(h)] += _dot(dsc, qn, TN)
            x = mq_ref[:, _hs(h)]
            _, r = _rms(x, gq_ref[...])
            dx, dg = _rms_bwd(_dot(dsc, kn.astype(BF16)), x, gq_ref[...], r)
            dmq_o[:, _hs(h)] = dx
            dgq = dgq + dg
        dgq_o[...] += dgq

        @pl.when(i == n - 1)
        def _():
            dgk = jnp.zeros((1, DH), F32)
            for h in range(NH):
                km = kv_ref[:, _hs(h)]
                _, r = _rms(km, gk_ref[...])
                dx, dg = _rms_bwd(dkn_scr[:, _hs(h)], km, gk_ref[...], r)
                dkv_o[:, _hs(h)] = dx
                dgk = dgk + dg
            dgk_o[...] = dgk

    full = lambda r, c: pl.BlockSpec((r, c), lambda i: (0, 0))
    t512 = pl.BlockSpec((ts, HW), lambda i: (i, 0))
    return pl.pallas_call(
        body, name="mem_bwd", grid=(n,),
        in_specs=[pl.BlockSpec((ts, HW), lambda i: (i, CB_MQ)), t512, full(NMEM, 2 * HW), full(1, DH), full(1, DH),
                  t512],
        out_specs=[t512, full(NMEM, 2 * HW), full(1, DH), full(1, DH)],
        out_shape=[jax.ShapeDtypeStruct((S, HW), F32), jax.ShapeDtypeStruct((NMEM, 2 * HW), F32),
                   jax.ShapeDtypeStruct((1, DH), F32), jax.ShapeDtypeStruct((1, DH), F32)],
        scratch_shapes=[pltpu.VMEM((NMEM, HW), F32)],
        compiler_params=_cp(("arbitrary",)),
    )(proj, qmn, kv, gmq, gmk, do)


def _gated_gdn(o, z, g):
    sg = _sigmoid(z)
    outs, rs = [], []
    for h in range(NH):
        y, r = _rms(o[:, _hs(h)], g)
        outs.append(y * (z[:, _hs(h)] * sg[:, _hs(h)]))
        rs.append(r)
    return jnp.concatenate(outs, axis=1), rs, sg


def _merge_fwd(x, proj, ogdn, osb, omem, ggdn, wbg, wbs, wbm, wo, S):
    ts = _narrow_tile(S)

    def body(x_ref, z_ref, g0_ref, g1_ref, g2_ref, og_ref, os_ref, om_ref, gg_ref, wbg_ref, wbs_ref, wbm_ref,
             wo_ref, x1_o, mix_o):
        on, _, _ = _gated_gdn(og_ref[...], z_ref[...], gg_ref[...])
        mix = (_sigmoid(g0_ref[...]) * _dbf(on, wbg_ref[...]) + _sigmoid(g1_ref[...]) * _dbf(os_ref[...], wbs_ref[...])
               + _sigmoid(g2_ref[...]) * _dbf(om_ref[...], wbm_ref[...]))
        mix_o[...] = mix.astype(BF16)
        x1_o[...] = x_ref[...] + _dbf(mix, wo_ref[...])

    t512 = pl.BlockSpec((ts, HW), lambda i: (i, 0))
    t1k = pl.BlockSpec((ts, D), lambda i: (i, 0))
    gate = lambda j: pl.BlockSpec((ts, D), lambda i: (i, 4 + j))
    full = lambda r, c: pl.BlockSpec((r, c), lambda i: (0, 0))
    return pl.pallas_call(
        body, name="merge_fwd", grid=(S // ts,),
        in_specs=[t1k, pl.BlockSpec((ts, HW), lambda i: (i, CB_Z)), gate(0), gate(1), gate(2), t512, t512, t512,
                  full(1, DH), full(HW, D), full(HW, D), full(HW, D), full(D, D)],
        out_specs=[t1k, t1k],
        out_shape=[jax.ShapeDtypeStruct((S, D), F32), jax.ShapeDtypeStruct((S, D), BF16)],
        compiler_params=_cp(("parallel",)),
    )(x, proj, proj, proj, proj, ogdn, osb, omem, ggdn, wbg, wbs, wbm, wo)


def _merge_bwd(dmix, proj, ogdn, osb, omem, ggdn, wbg, wbs, wbm, S):
    ts = _narrow_tile(S)

    def body(dm_ref, z_ref, g0_ref, g1_ref, g2_ref, og_ref, os_ref, om_ref, gg_ref, wbg_ref, wbs_ref, wbm_ref,
             dgl0_o, dgl1_o, dgl2_o, dog_o, dz_o, dos_o, dom_o, dwbg_o, dwbs_o, dwbm_o, dgg_o):
        @pl.when(pl.program_id(0) == 0)
        def _():
            for ref in (dwbg_o, dwbs_o, dwbm_o, dgg_o):
                ref[...] = jnp.zeros_like(ref)

        dm = dm_ref[...]
        og = og_ref[...]
        z = z_ref[...]
        on, rs, sg = _gated_gdn(og, z, gg_ref[...])
        branch = ((on, g0_ref, wbg_ref, dgl0_o, dwbg_o), (os_ref[...], g1_ref, wbs_ref, dgl1_o, dwbs_o),
                  (om_ref[...], g2_ref, wbm_ref, dgl2_o, dwbm_o))
        dos = []
        for o, g_ref, w_ref, dgl_o, dw_o in branch:
            ob = o.astype(BF16)
            gate = _sigmoid(g_ref[...])
            dgl_o[...] = dm * _dot(ob, w_ref[...]) * gate * (1.0 - gate)
            dy = (dm * gate).astype(BF16)
            dw_o[...] += _dot(ob, dy, TN)
            dos.append(_dot(dy, w_ref[...], NT))
        dos_o[...] = dos[1]
        dom_o[...] = dos[2]
        don = dos[0]
        dgg = jnp.zeros((1, DH), F32)
        for h in range(NH):
            oh, zh, sh = og[:, _hs(h)], z[:, _hs(h)], sg[:, _hs(h)]
            y = oh * rs[h] * gg_ref[...]
            dz_o[:, _hs(h)] = don[:, _hs(h)] * y * (sh * (1.0 + zh * (1.0 - sh)))
            dx, dg = _rms_bwd(don[:, _hs(h)] * (zh * sh), oh, gg_ref[...], rs[h])
            dog_o[:, _hs(h)] = dx
            dgg = dgg + dg
        dgg_o[...] += dgg

    t512 = pl.BlockSpec((ts, HW), lambda i: (i, 0))
    t1k = pl.BlockSpec((ts, D), lambda i: (i, 0))
    gate = lambda j: pl.BlockSpec((ts, D), lambda i: (i, 4 + j))
    full = lambda r, c: pl.BlockSpec((r, c), lambda i: (0, 0))
    s1k = jax.ShapeDtypeStruct((S, D), F32)
    s512 = jax.ShapeDtypeStruct((S, HW), F32)
    wsh = jax.ShapeDtypeStruct((HW, D), F32)
    return pl.pallas_call(
        body, name="merge_bwd", grid=(S // ts,),
        in_specs=[t1k, pl.BlockSpec((ts, HW), lambda i: (i, CB_Z)), gate(0), gate(1), gate(2), t512, t512, t512,
                  full(1, DH), full(HW, D), full(HW, D), full(HW, D)],
        out_specs=[t1k, t1k, t1k, t512, t512, t512, t512, full(HW, D), full(HW, D), full(HW, D), full(1, DH)],
        out_shape=[s1k, s1k, s1k, s512, s512, s512, s512, wsh, wsh, wsh, jax.ShapeDtypeStruct((1, DH), F32)],
        compiler_params=_cp(("arbitrary",)),
    )(dmix, proj, proj, proj, proj, ogdn, osb, omem, ggdn, wbg, wbs, wbm)


def _loss_grad(y, target, S):
    ts = _row_tile(S)

    def body(y_ref, t_ref, dy_o, loss_o):
        @pl.when(pl.program_id(0) == 0)
        def _():
            loss_o[...] = jnp.zeros_like(loss_o)

        err = y_ref[...] - t_ref[...]
        dy_o[...] = err * (1.0 / D)
        per_tok = jnp.sum(err * err, axis=1, keepdims=True) * (1.0 / D)
        loss_o[...] += 0.5 * jnp.sum(per_tok, axis=0, keepdims=True)

    t1k = pl.BlockSpec((ts, D), lambda i: (i, 0))
    return pl.pallas_call(
        body, name="loss_grad", grid=(S // ts,), in_specs=[t1k, t1k],
        out_specs=[t1k, pl.BlockSpec((1, 1), lambda i: (0, 0))],
        out_shape=[jax.ShapeDtypeStruct((S, D), F32), jax.ShapeDtypeStruct((1, 1), F32)],
        compiler_params=_cp(("arbitrary",)),
    )(y, target)


def _norm_bwd(name, dh, x, g, res):
    rows = x.shape[0]
    ts = min(_row_tile(rows), rows)

    def body(*refs):
        dh_ref, x_ref, g_ref = refs[:3]
        dx_o, dg_o = refs[-2:]

        @pl.when(pl.program_id(0) == 0)
        def _():
            dg_o[...] = jnp.zeros_like(dg_o)

        xv = x_ref[...]
        _, r = _rms(xv, g_ref[...])
        dx, dg = _rms_bwd(dh_ref[...], xv, g_ref[...], r)
        dx_o[...] = dx if res is None else dx + refs[3][...]
        dg_o[...] += dg

    t1k = pl.BlockSpec((ts, D), lambda i: (i, 0))
    gsp = pl.BlockSpec((1, D), lambda i: (0, 0))
    ops = [dh, x, g] + ([] if res is None else [res])
    return pl.pallas_call(
        body, name=name, grid=(rows // ts,), in_specs=[t1k, t1k, gsp] + ([] if res is None else [t1k]),
        out_specs=[t1k, gsp],
        out_shape=[jax.ShapeDtypeStruct((rows, D), F32), jax.ShapeDtypeStruct((1, D), F32)],
        compiler_params=_cp(("arbitrary",)),
    )(*ops)


def _adamw(name, gall, w, m, v):
    rows = w.shape[0]
    tr = min(1216, rows)
    assert rows % tr == 0

    def body(g_ref, w_ref, m_ref, v_ref, g_o, d_o, m_o, v_o):
        g = g_ref[0].astype(F32)
        for j in range(1, NDEV):
            g = g + g_ref[j].astype(F32)
        m_new = ADAM_B1 * m_ref[...] + (1.0 - ADAM_B1) * g
        v_new = ADAM_B2 * v_ref[...] + (1.0 - ADAM_B2) * jnp.square(g)
        m_hat = m_new / (1.0 - ADAM_B1 ** ADAM_STEP)
        v_hat = v_new / (1.0 - ADAM_B2 ** ADAM_STEP)
        g_o[...] = g
        d_o[...] = -ADAM_LR * (m_hat / (jnp.sqrt(v_hat) + ADAM_EPS) + ADAM_WD * w_ref[...])
        m_o[...] = m_new
        v_o[...] = v_new

    t = pl.BlockSpec((tr, LANES), lambda i: (i, 0))
    o = jax.ShapeDtypeStruct((rows, LANES), F32)
    return pl.pallas_call(
        body, name=name, grid=(rows // tr,),
        in_specs=[pl.BlockSpec((NDEV, tr, LANES), lambda i: (0, i, 0)), t, t, t],
        out_specs=[t, t, t, t], out_shape=[o, o, o, o],
        compiler_params=_cp(("parallel",)),
    )(gall, w, m, v)


def _exchange(name, x, gather):
    rows, cols = x.shape[-2:]

    def body(x_ref, o_ref, send_sems, recv_sems, local_sem):
        ix, iy, ic = lax.axis_index("x"), lax.axis_index("y"), lax.axis_index("c")
        me = 4 * ix + 2 * iy + ic
        own = pltpu.make_async_copy(x_ref if gather else x_ref.at[me], o_ref.at[me], local_sem)
        own.start()
        copies = []
        for k in range(1, NDEV):
            px, py, pc = ix ^ ((k >> 2) & 1), iy ^ ((k >> 1) & 1), ic ^ (k & 1)
            src = x_ref if gather else x_ref.at[4 * px + 2 * py + pc]
            cp = pltpu.make_async_remote_copy(
                src_ref=src, dst_ref=o_ref.at[me], send_sem=send_sems.at[k - 1], recv_sem=recv_sems.at[k - 1],
                device_id=(px, py, pc), device_id_type=pl.DeviceIdType.MESH)
            cp.start()
            copies.append(cp)
        for cp in copies:
            cp.wait()
        own.wait()

    hbm = pl.BlockSpec(memory_space=pltpu.HBM)
    return pl.pallas_call(
        body, name=name, in_specs=[hbm], out_specs=hbm,
        out_shape=jax.ShapeDtypeStruct((NDEV, rows, cols), x.dtype),
        scratch_shapes=[pltpu.SemaphoreType.DMA((NDEV - 1,)), pltpu.SemaphoreType.DMA((NDEV - 1,)),
                        pltpu.SemaphoreType.DMA],
    )(x)


COL_SHARDED = {"w_in": (D, D_IN), "w_br_gdn": (HW, D), "w_br_sb": (HW, D), "w_br_mem": (HW, D), "w_up": (D, DFF),
               "conv_w": (4, 3 * HW)}
ROW_SHARDED = {"w_mem_kv": (D, 2 * HW), "w_o": (D, D), "w_down": (DFF, D)}


def _pack_rows(parts, total):
    flat = jnp.concatenate([p.reshape(-1, LANES) for p in parts], axis=0)
    return jnp.pad(flat, ((0, total - flat.shape[0]), (0, 0)))


def _pack_full_grads(grads):
    parts = []
    for name in BIG:
        g = grads[name]
        if name in COL_SHARDED:
            r, c = COL_SHARDED[name]
            g = g.reshape(r, NDEV, c // NDEV).transpose(1, 0, 2)
        parts.append(g.reshape(NDEV, -1, LANES))
    flat = jnp.concatenate(parts, axis=1)
    return jnp.pad(flat, ((0, 0), (0, R_BIG - flat.shape[1]), (0, 0)))


def _unpack_gathered(slabs):
    out, pos = {}, 0
    for name, rows in zip(BIG, BIG_ROWS):
        g = slabs[:, pos:pos + rows]
        pos += rows
        if name in COL_SHARDED:
            r, c = COL_SHARDED[name]
            out[name] = g.reshape(NDEV, r, c // NDEV).transpose(1, 0, 2).reshape(r, c)
        else:
            r, c = ROW_SHARDED[name]
            out[name] = g.reshape(r, c)
    return out


def _unpack_shard(flat, shapes):
    out, pos = {}, 0
    for name, rows in zip(BIG, BIG_ROWS):
        out[name] = flat[pos:pos + rows].reshape(shapes[name])
        pos += rows
    return out


def _pack_small(vals):
    rows = []
    for name, n in zip(SMALL, SMALL_ROWS):
        v = vals[name].reshape(-1)
        rows.append(jnp.pad(v, (0, n * LANES - v.shape[0])).reshape(n, LANES))
    return _pack_rows(rows, R_SMALL)


def _unpack_small(flat, shapes):
    out, pos = {}, 0
    for name, n in zip(SMALL, SMALL_ROWS):
        size = shapes[name][-1]
        out[name] = flat[pos:pos + n].reshape(-1)[:size].reshape(shapes[name])
        pos += n
    return out


def _pad_w_in(w):
    return jnp.concatenate([w[:, :2048], w[:, 2056:], w[:, 2048:2056], jnp.zeros((D, D_INP - D_IN), w.dtype)], axis=1)


def _unpad_w_in(w):
    return jnp.concatenate([w[:, :2048], w[:, 7168:7176], w[:, 2048:7168]], axis=1)


def _per_head(v):
    return jnp.repeat(v.reshape(NH), DH).reshape(1, HW)


def _local_step(x, mem, target, w, sm):
    S = x.shape[0]
    ts = _row_tile(S)
    alog_f, dtb_f = _per_head(sm["a_log"]), _per_head(sm["dt_bias"])

    proj = _mm("in_proj", x, w["w_in"], "nn", ts, 1536, D, pro="rms", pro_g=sm["norm1_g"])
    gq, gk, gv, gf, bf, sqn, skn, svb, qmn = _pre_fwd(proj, w["conv_w"], alog_f, dtb_f, sm["sb_q_norm_g"],
                                                      sm["sb_k_norm_g"], sm["mem_q_norm_g"], S)
    ogdn, states = _gdn_fwd(gq, gk, gv, gf, bf, S)
    osb, sb_tot = _sb_fwd(sqn, skn, svb, S)
    kv = _mm("mem_kv", mem, w["w_mem_kv"], "nn", NMEM, D, D, pro="rms", pro_g=sm["mem_norm_g"])
    omem = _mem_fwd(qmn, kv, sm["mem_k_norm_g"], S)
    x1, mix = _merge_fwd(x, proj, ogdn, osb, omem, sm["gdn_norm_g"], w["w_br_gdn"], w["w_br_sb"], w["w_br_mem"],
                         w["w_o"], S)
    up = _mm("mlp_up", x1, w["w_up"], "nn", ts, 2048, D, pro="rms", pro_g=sm["norm2_g"])
    x2 = _mm("mlp_down", up, w["w_down"], "nn", ts, D, 1024, pro="relu2", epi="add", epi_x=x1)
    dy, loss = _loss_grad(x2, target, S)

    g = {}
    dup = _mm("d_up", dy, w["w_down"], "nt", ts, 1024, D, epi="drelu2", epi_x=up)
    g["w_down"] = _mm("dw_down", up, dy, "tn", 1024, D, 512, pro="relu2")
    g["w_up"] = _mm("dw_up", x1, dup, "tn", D, 1024, 512, pro="rms", pro_g=sm["norm2_g"])
    dh2 = _mm("d_h2", dup, w["w_up"], "nt", ts, D, 1024)
    dx1, g["norm2_g"] = _norm_bwd("norm2_bwd", dh2, x1, sm["norm2_g"], dy)

    dmix = _mm("d_mix", dx1, w["w_o"], "nt", ts, D, D)
    g["w_o"] = _mm("dw_o", mix, dx1, "tn", D, D, 512)
    (dgl0, dgl1, dgl2, dogdn, dz, dosb, domem, g["w_br_gdn"], g["w_br_sb"], g["w_br_mem"],
     g["gdn_norm_g"]) = _merge_bwd(dmix, proj, ogdn, osb, omem, sm["gdn_norm_g"], w["w_br_gdn"], w["w_br_sb"],
                                   w["w_br_mem"], S)
    dmq, dkv, g["mem_q_norm_g"], g["mem_k_norm_g"] = _mem_bwd(proj, qmn, kv, sm["mem_q_norm_g"], sm["mem_k_norm_g"],
                                                             domem, S)
    g["w_mem_kv"] = _mm("dw_mem_kv", mem, dkv, "tn", D, D, NMEM, pro="rms", pro_g=sm["mem_norm_g"])
    dmn = _mm("d_mem_n", dkv, w["w_mem_kv"], "nt", NMEM, D, D)
    _, g["mem_norm_g"] = _norm_bwd("mem_norm_bwd", dmn, mem, sm["mem_norm_g"], None)
    dsqn, dskn, dsv = _sb_bwd(sqn, skn, svb, dosb, sb_tot, S)
    dgq, dgk, dgv, dgf, dbf = _gdn_bwd(gq, gk, gv, gf, bf, states, dogdn, S)
    dc, dab, dsq, dsk, g["conv_w"], dal_f, ddt_f, g["sb_q_norm_g"], g["sb_k_norm_g"] = _pre_bwd(
        proj, w["conv_w"], alog_f, dtb_f, sm["sb_q_norm_g"], sm["sb_k_norm_g"], dgq, dgk, dgv, dgf, dbf, dsqn, dskn, S)
    g["a_log"] = dal_f.reshape(NH, DH)[:, 0].reshape(1, NH)
    g["dt_bias"] = ddt_f.reshape(NH, DH)[:, 0].reshape(1, NH)
    dqkv = _conv_bwd(dc, w["conv_w"], S)

    dproj = jnp.concatenate([dqkv, dz, dsq, dsk, dsv, dmq, dgl0, dgl1, dgl2, dab], axis=1)
    g["w_in"] = _mm("dw_in", x, dproj, "tn", D, 1536, 512, pro="rms", pro_g=sm["norm1_g"])
    dh = _mm("d_h", dproj, w["w_in"], "nt", ts, D, 1536)
    dx, g["norm1_g"] = _norm_bwd("norm1_bwd", dh, x, sm["norm1_g"], dx1)
    return loss[0, 0], dx, g


def kernel(x, mem, norm1_g, w_in, conv_w, a_log, dt_bias, gdn_norm_g, sb_q_norm_g, sb_k_norm_g, mem_norm_g, w_mem_kv, mem_q_norm_g, mem_k_norm_g, w_br_gdn, w_br_sb, w_br_mem, w_o, norm2_g, w_up, w_down, loss_target, m_norm1_g, m_w_in, m_conv_w, m_a_log, m_dt_bias, m_gdn_norm_g, m_sb_q_norm_g, m_sb_k_norm_g, m_mem_norm_g, m_w_mem_kv, m_mem_q_norm_g, m_mem_k_norm_g, m_w_br_gdn, m_w_br_sb, m_w_br_mem, m_w_o, m_norm2_g, m_w_up, m_w_down, v_norm1_g, v_w_in, v_conv_w, v_a_log, v_dt_bias, v_gdn_norm_g, v_sb_q_norm_g, v_sb_k_norm_g, v_mem_norm_g, v_w_mem_kv, v_mem_q_norm_g, v_mem_k_norm_g, v_w_br_gdn, v_w_br_sb, v_w_br_mem, v_w_o, v_norm2_g, v_w_up, v_w_down):
    given = dict(norm1_g=norm1_g, w_in=w_in, conv_w=conv_w, a_log=a_log, dt_bias=dt_bias, gdn_norm_g=gdn_norm_g,
                 sb_q_norm_g=sb_q_norm_g, sb_k_norm_g=sb_k_norm_g, mem_norm_g=mem_norm_g, w_mem_kv=w_mem_kv,
                 mem_q_norm_g=mem_q_norm_g, mem_k_norm_g=mem_k_norm_g, w_br_gdn=w_br_gdn, w_br_sb=w_br_sb,
                 w_br_mem=w_br_mem, w_o=w_o, norm2_g=norm2_g, w_up=w_up, w_down=w_down)
    mom1 = dict(norm1_g=m_norm1_g, w_in=m_w_in, conv_w=m_conv_w, a_log=m_a_log, dt_bias=m_dt_bias,
                gdn_norm_g=m_gdn_norm_g, sb_q_norm_g=m_sb_q_norm_g, sb_k_norm_g=m_sb_k_norm_g,
                mem_norm_g=m_mem_norm_g, w_mem_kv=m_w_mem_kv, mem_q_norm_g=m_mem_q_norm_g,
                mem_k_norm_g=m_mem_k_norm_g, w_br_gdn=m_w_br_gdn, w_br_sb=m_w_br_sb, w_br_mem=m_w_br_mem, w_o=m_w_o,
                norm2_g=m_norm2_g, w_up=m_w_up, w_down=m_w_down)
    mom2 = dict(norm1_g=v_norm1_g, w_in=v_w_in, conv_w=v_conv_w, a_log=v_a_log, dt_bias=v_dt_bias,
                gdn_norm_g=v_gdn_norm_g, sb_q_norm_g=v_sb_q_norm_g, sb_k_norm_g=v_sb_k_norm_g,
                mem_norm_g=v_mem_norm_g, w_mem_kv=v_w_mem_kv, mem_q_norm_g=v_mem_q_norm_g,
                mem_k_norm_g=v_mem_k_norm_g, w_br_gdn=v_w_br_gdn, w_br_sb=v_w_br_sb, w_br_mem=v_w_br_mem, w_o=v_w_o,
                norm2_g=v_norm2_g, w_up=v_w_up, w_down=v_w_down)
    shapes = {n: given[n].shape for n in WEIGHTS}

    w_loc = _pack_rows([given[n][0] for n in BIG], R_BIG)
    gathered = _exchange("gather_weights", w_loc.astype(BF16), True)
    w = _unpack_gathered(gathered[:, :sum(BIG_ROWS)])
    w["w_in"] = _pad_w_in(w["w_in"])
    conv_loc = jnp.pad(given["conv_w"][0].reshape(-1, LANES), ((0, 2), (0, 0)))
    conv_all = _exchange("gather_conv", conv_loc, True)
    w["conv_w"] = conv_all[:, :6].reshape(NDEV, 4, 3 * HW // NDEV).transpose(1, 0, 2).reshape(4, 3 * HW)
    sm = {n: given[n] for n in SMALL}

    loss, dx, g = _local_step(x[0], mem[0], loss_target[0], w, sm)
    g["w_in"] = _unpad_w_in(g["w_in"])

    g_all = _exchange("scatter_grads", _pack_full_grads(g).astype(BF16), False)
    gb, db, mb, vb = _adamw("adamw_sharded", g_all, w_loc, _pack_rows([mom1[n][0] for n in BIG], R_BIG),
                            _pack_rows([mom2[n][0] for n in BIG], R_BIG))
    gs_all = _exchange("gather_small_grads", _pack_small(g), True)
    gs, dsm, ms, vs = _adamw("adamw_replicated", gs_all, _pack_small(given), _pack_small(mom1), _pack_small(mom2))

    outs = {}
    for prefix, big, small in (("grad_", gb, gs), ("delta_", db, dsm), ("new_m_", mb, ms), ("new_v_", vb, vs)):
        vals = _unpack_shard(big, shapes)
        vals.update(_unpack_small(small, shapes))
        for n in WEIGHTS:
            outs[prefix + n] = vals[n]
    loss = lax.psum(loss, ("x", "y", "c"))
    return (loss, dx[None], *[outs[p + n] for p in ("grad_", "delta_", "new_m_", "new_v_") for n in WEIGHTS])
```

```python
import jax
import jax.numpy as jnp
from jax import lax
from jax.experimental import pallas as pl
from jax.experimental.pallas import tpu as pltpu

F32 = jnp.float32
BF16 = jnp.bfloat16

D = 1024
NH = 4
DH = 128
HW = NH * DH
DFF = 4 * D
NMEM = 256
EPS = 1e-6
NDEV = 8
LANES = 128
PAIR = 128
CHUNK = 64
D_IN = 7176
D_INP = 7680
VMEM_LIMIT = 56 * 1024 * 1024

ADAM_LR, ADAM_B1, ADAM_B2, ADAM_EPS, ADAM_WD, ADAM_STEP = 0.001, 0.9, 0.999, 1e-08, 0.01, 10

CB_Z, CB_SQ, CB_SK, CB_SV, CB_MQ, CB_AB = 3, 4, 5, 6, 7, 14

NN = (((1,), (0,)), ((), ()))
NT = (((1,), (1,)), ((), ()))
TN = (((0,), (0,)), ((), ()))

BIG = ("w_in", "w_mem_kv", "w_br_gdn", "w_br_sb", "w_br_mem", "w_o", "w_up", "w_down", "conv_w")
BIG_ROWS = (7176, 1024, 512, 512, 512, 1024, 4096, 4096, 6)
R_BIG = 19456
SMALL = ("norm1_g", "a_log", "dt_bias", "gdn_norm_g", "sb_q_norm_g", "sb_k_norm_g", "mem_norm_g",
         "mem_q_norm_g", "mem_k_norm_g", "norm2_g")
SMALL_ROWS = (8, 1, 1, 1, 1, 1, 8, 1, 1, 8)
R_SMALL = 32
WEIGHTS = ("norm1_g", "w_in", "conv_w", "a_log", "dt_bias", "gdn_norm_g", "sb_q_norm_g", "sb_k_norm_g",
           "mem_norm_g", "w_mem_kv", "mem_q_norm_g", "mem_k_norm_g", "w_br_gdn", "w_br_sb", "w_br_mem",
           "w_o", "norm2_g", "w_up", "w_down")


def _cp(sem=None):
    return pltpu.CompilerParams(dimension_semantics=sem, vmem_limit_bytes=VMEM_LIMIT)


def _dot(a, b, dims=NN):
    return lax.dot_general(a, b, dims, preferred_element_type=F32)


def _dbf(a, b, dims=NN):
    return _dot(a.astype(BF16), b.astype(BF16), dims)


def _split(a, n):
    parts = []
    for _ in range(n):
        h = a.astype(BF16)
        parts.append(h)
        a = a - h.astype(F32)
    return parts


def _d3(a, b, dims=NN):
    ah, al = _split(a, 2)
    bh, bl = _split(b, 2)
    return _dot(ah, bh, dims) + (_dot(ah, bl, dims) + _dot(al, bh, dims))


def _dxr(a, e, dims=NN):
    eb = e.astype(BF16)
    a1, a2, a3 = _split(a, 3)
    return _dot(a1, eb, dims) + (_dot(a2, eb, dims) + _dot(a3, eb, dims))


def _dxl(e, a, dims=NN):
    eb = e.astype(BF16)
    a1, a2, a3 = _split(a, 3)
    return _dot(eb, a1, dims) + (_dot(eb, a2, dims) + _dot(eb, a3, dims))


def _sigmoid(x):
    return 1.0 / (1.0 + jnp.exp(-x))


def _softplus(x):
    return jnp.maximum(x, 0.0) + jnp.log(1.0 + jnp.exp(-jnp.abs(x)))


def _rms(x, g):
    r = lax.rsqrt(jnp.mean(x * x, axis=-1, keepdims=True) + EPS)
    return x * r * g, r


def _rms_bwd(dy, x, g, r):
    dyg = dy * g
    dx = r * (dyg - x * (r * r) * jnp.mean(dyg * x, axis=-1, keepdims=True))
    dg = jnp.sum(dy * (x * r), axis=0, keepdims=True)
    return dx, dg


def _hs(h):
    return slice(h * DH, (h + 1) * DH)


def _row_tile(s):
    return 512 if s >= 2048 else 256


def _narrow_tile(s):
    return min(256, s)


def _mm(name, a, b, mode, tm, tn, tk, pro=None, pro_g=None, epi=None, epi_x=None):
    if mode == "tn":
        K, M = a.shape
    else:
        M, K = a.shape
    N = b.shape[0] if mode == "nt" else b.shape[1]
    tm, tn, tk = min(tm, M), min(tn, N), min(tk, K)
    nk = K // tk
    assert M % tm == 0 and N % tn == 0 and K % tk == 0, (name, M, N, K, tm, tn, tk)
    dims = {"nn": NN, "nt": NT, "tn": TN}[mode]

    def body(*refs):
        a_ref, b_ref = refs[0], refs[1]
        pos = 2
        g_ref = e_ref = None
        if pro == "rms":
            g_ref = refs[pos]
            pos += 1
        if epi is not None:
            e_ref = refs[pos]
            pos += 1
        o_ref = refs[pos]
        av = a_ref[...]
        if pro == "rms":
            av, _ = _rms(av.astype(F32), g_ref[...])
        elif pro == "relu2":
            av = jnp.square(jnp.maximum(av, 0.0))
        part = _dbf(av, b_ref[...], dims)

        def finish(acc):
            if epi == "add":
                acc = acc + e_ref[...]
            elif epi == "drelu2":
                acc = acc * (2.0 * jnp.maximum(e_ref[...], 0.0))
            o_ref[...] = acc

        if nk == 1:
            finish(part)
        else:
            acc_ref = refs[pos + 1]
            k = pl.program_id(2)

            @pl.when(k == 0)
            def _():
                acc_ref[...] = part

            @pl.when(k > 0)
            def _():
                acc_ref[...] += part

            @pl.when(k == nk - 1)
            def _():
                finish(acc_ref[...])

    if mode == "tn":
        a_spec = pl.BlockSpec((tk, tm), lambda i, j, k: (k, i))
    else:
        a_spec = pl.BlockSpec((tm, tk), lambda i, j, k: (i, k))
    if mode == "nt":
        b_spec = pl.BlockSpec((tn, tk), lambda i, j, k: (j, k))
    else:
        b_spec = pl.BlockSpec((tk, tn), lambda i, j, k: (k, j))
    in_specs, ops = [a_spec, b_spec], [a, b]
    if pro == "rms":
        w = pro_g.shape[1]
        assert (tm if mode == "tn" else tk) == w, name
        in_specs.append(pl.BlockSpec((1, w), lambda i, j, k: (0, 0)))
        ops.append(pro_g)
    if epi is not None:
        in_specs.append(pl.BlockSpec((tm, tn), lambda i, j, k: (i, j)))
        ops.append(epi_x)
    return pl.pallas_call(
        body, name=name, grid=(M // tm, N // tn, nk),
        in_specs=in_specs, out_specs=pl.BlockSpec((tm, tn), lambda i, j, k: (i, j)),
        out_shape=jax.ShapeDtypeStruct((M, N), F32),
        scratch_shapes=[pltpu.VMEM((tm, tn), F32)] if nk > 1 else [],
        compiler_params=_cp(("parallel", "parallel", "arbitrary")),
    )(*ops)


def _head_select(first_lane):
    l = lax.broadcasted_iota(jnp.int32, (LANES, HW), 0)
    c = lax.broadcasted_iota(jnp.int32, (LANES, HW), 1)
    return (l == first_lane + c // DH).astype(F32)


def _conv_taps(buf, cw, ts):
    c = cw[3:4, :] * buf[8:8 + ts, :]
    for j in range(3):
        k = 3 - j
        c = c + cw[j:j + 1, :] * buf[8 - k:8 - k + ts, :]
    return c


def _pre_fwd(proj, conv_w, alog_f, dtb_f, gsq, gsk, gmq, S):
    ts = _narrow_tile(S)
    hb = ts // 8

    def body(qkv_ref, halo_ref, ab_ref, sq_ref, sk_ref, sv_ref, mq_ref, cw_ref, al_ref, dt_ref, gsq_ref, gsk_ref,
             gmq_ref, gq_o, gk_o, gv_o, gf_o, bf_o, sqn_o, skn_o, svb_o, qmn_o, buf):
        i = pl.program_id(0)
        buf[0:8, :] = jnp.where(i == 0, 0.0, halo_ref[...])
        buf[8:8 + ts, :] = qkv_ref[...]
        c = _conv_taps(buf, cw_ref[...], ts)
        a = c * _sigmoid(c)
        for h in range(NH):
            q = a[:, h * DH:(h + 1) * DH]
            k = a[:, HW + h * DH:HW + (h + 1) * DH]
            gq_o[:, _hs(h)] = q * (lax.rsqrt(jnp.sum(q * q, axis=-1, keepdims=True) + EPS) * DH ** -0.5)
            gk_o[:, _hs(h)] = k * lax.rsqrt(jnp.sum(k * k, axis=-1, keepdims=True) + EPS)
            sqn_o[:, _hs(h)] = _rms(sq_ref[:, _hs(h)], gsq_ref[...])[0].astype(BF16)
            skn_o[:, _hs(h)] = _rms(sk_ref[:, _hs(h)], gsk_ref[...])[0].astype(BF16)
            qmn_o[:, _hs(h)] = _rms(mq_ref[:, _hs(h)], gmq_ref[...])[0].astype(BF16)
        gv_o[...] = a[:, 2 * HW:3 * HW]
        svb_o[...] = sv_ref[...].astype(BF16)
        ab = ab_ref[:, 0:LANES]
        a_bc = _dxr(ab, _head_select(0))
        b_bc = _dxr(ab, _head_select(NH))
        gf_o[...] = -jnp.exp(al_ref[...]) * _softplus(a_bc + dt_ref[...])
        bf_o[...] = _sigmoid(b_bc)

    row = lambda cb: pl.BlockSpec((ts, HW), lambda i: (i, cb))
    full = lambda r, c: pl.BlockSpec((r, c), lambda i: (0, 0))
    f32o = jax.ShapeDtypeStruct((S, HW), F32)
    bfo = jax.ShapeDtypeStruct((S, HW), BF16)
    return pl.pallas_call(
        body, name="pre_fwd", grid=(S // ts,),
        in_specs=[pl.BlockSpec((ts, 3 * HW), lambda i: (i, 0)),
                  pl.BlockSpec((8, 3 * HW), lambda i: (jnp.maximum(i * hb - 1, 0), 0)),
                  row(CB_AB), row(CB_SQ), row(CB_SK), row(CB_SV), row(CB_MQ),
                  full(4, 3 * HW), full(1, HW), full(1, HW), full(1, DH), full(1, DH), full(1, DH)],
        out_specs=[pl.BlockSpec((ts, HW), lambda i: (i, 0))] * 9,
        out_shape=[f32o, f32o, f32o, f32o, f32o, bfo, bfo, bfo, bfo],
        scratch_shapes=[pltpu.VMEM((ts + 8, 3 * HW), F32)],
        compiler_params=_cp(("parallel",)),
    )(proj, proj, proj, proj, proj, proj, proj, conv_w, alog_f, dtb_f, gsq, gsk, gmq)


def _pre_bwd(proj, conv_w, alog_f, dtb_f, gsq, gsk, dgq, dgk, dgv, dgf, dbf, dsqn, dskn, S):
    ts = _narrow_tile(S)
    hb = ts // 8

    def body(qkv_ref, halo_ref, ab_ref, sq_ref, sk_ref, cw_ref, al_ref, dt_ref, gsq_ref, gsk_ref,
             dgq_ref, dgk_ref, dgv_ref, dgf_ref, dbf_ref, dsqn_ref, dskn_ref,
             dc_o, dab_o, dsq_o, dsk_o, dcw_o, dal_o, ddt_o, dgsq_o, dgsk_o, buf):
        i = pl.program_id(0)

        @pl.when(i == 0)
        def _():
            dcw_o[...] = jnp.zeros_like(dcw_o)
            dal_o[...] = jnp.zeros_like(dal_o)
            ddt_o[...] = jnp.zeros_like(ddt_o)
            dgsq_o[...] = jnp.zeros_like(dgsq_o)
            dgsk_o[...] = jnp.zeros_like(dgsk_o)

        buf[0:8, :] = jnp.where(i == 0, 0.0, halo_ref[...])
        buf[8:8 + ts, :] = qkv_ref[...]
        c = _conv_taps(buf, cw_ref[...], ts)
        sg = _sigmoid(c)
        a = c * sg
        dsilu = sg * (1.0 + c * (1.0 - sg))
        dgsq = jnp.zeros((1, DH), F32)
        dgsk = jnp.zeros((1, DH), F32)
        for h in range(NH):
            q = a[:, h * DH:(h + 1) * DH]
            k = a[:, HW + h * DH:HW + (h + 1) * DH]
            nq = lax.rsqrt(jnp.sum(q * q, axis=-1, keepdims=True) + EPS)
            nk = lax.rsqrt(jnp.sum(k * k, axis=-1, keepdims=True) + EPS)
            dyq = dgq_ref[:, _hs(h)]
            dyk = dgk_ref[:, _hs(h)]
            dq = (nq * dyq - q * (nq * nq * nq) * jnp.sum(dyq * q, axis=-1, keepdims=True)) * DH ** -0.5
            dk = nk * dyk - k * (nk * nk * nk) * jnp.sum(dyk * k, axis=-1, keepdims=True)
            dc_o[:, h * DH:(h + 1) * DH] = dq * dsilu[:, h * DH:(h + 1) * DH]
            dc_o[:, HW + h * DH:HW + (h + 1) * DH] = dk * dsilu[:, HW + h * DH:HW + (h + 1) * DH]
            x = sq_ref[:, _hs(h)]
            _, r = _rms(x, gsq_ref[...])
            dx, dg = _rms_bwd(dsqn_ref[:, _hs(h)], x, gsq_ref[...], r)
            dsq_o[:, _hs(h)] = dx
            dgsq = dgsq + dg
            x = sk_ref[:, _hs(h)]
            _, r = _rms(x, gsk_ref[...])
            dx, dg = _rms_bwd(dskn_ref[:, _hs(h)], x, gsk_ref[...], r)
            dsk_o[:, _hs(h)] = dx
            dgsk = dgsk + dg
        dc_o[:, 2 * HW:3 * HW] = dgv_ref[...] * dsilu[:, 2 * HW:3 * HW]
        dgsq_o[...] += dgsq
        dgsk_o[...] += dgsk
        dc = dc_o[...]
        for j in range(4):
            k = 3 - j
            dcw_o[j:j + 1, :] += jnp.sum(dc * buf[8 - k:8 - k + ts, :], axis=0, keepdims=True)
        ab = ab_ref[:, 0:LANES]
        a_bc = _dxr(ab, _head_select(0))
        b_bc = _dxr(ab, _head_select(NH))
        pre = a_bc + dt_ref[...]
        ea = jnp.exp(al_ref[...])
        dgf = dgf_ref[...]
        dal_o[...] += jnp.sum(dgf * (-ea * _softplus(pre)), axis=0, keepdims=True)
        da = dgf * (-ea * _sigmoid(pre))
        ddt_o[...] += jnp.sum(da, axis=0, keepdims=True)
        beta = _sigmoid(b_bc)
        db = dbf_ref[...] * beta * (1.0 - beta)
        lane = lax.broadcasted_iota(jnp.int32, (ts, LANES), 1)
        dab = jnp.zeros((ts, LANES), F32)
        for h in range(NH):
            dab = dab + jnp.where(lane == h, da[:, _hs(h)], 0.0) + jnp.where(lane == NH + h, db[:, _hs(h)], 0.0)
        dab_o[:, 0:LANES] = dab
        dab_o[:, LANES:HW] = jnp.zeros((ts, HW - LANES), F32)

    row = lambda cb: pl.BlockSpec((ts, HW), lambda i: (i, cb))
    full = lambda r, c: pl.BlockSpec((r, c), lambda i: (0, 0))
    t512 = pl.BlockSpec((ts, HW), lambda i: (i, 0))
    return pl.pallas_call(
        body, name="pre_bwd", grid=(S // ts,),
        in_specs=[pl.BlockSpec((ts, 3 * HW), lambda i: (i, 0)),
                  pl.BlockSpec((8, 3 * HW), lambda i: (jnp.maximum(i * hb - 1, 0), 0)),
                  row(CB_AB), row(CB_SQ), row(CB_SK),
                  full(4, 3 * HW), full(1, HW), full(1, HW), full(1, DH), full(1, DH)] + [t512] * 7,
        out_specs=[pl.BlockSpec((ts, 3 * HW), lambda i: (i, 0)), t512, t512, t512,
                   full(4, 3 * HW), full(1, HW), full(1, HW), full(1, DH), full(1, DH)],
        out_shape=[jax.ShapeDtypeStruct((S, 3 * HW), F32)] + [jax.ShapeDtypeStruct((S, HW), F32)] * 3
        + [jax.ShapeDtypeStruct((4, 3 * HW), F32), jax.ShapeDtypeStruct((1, HW), F32),
           jax.ShapeDtypeStruct((1, HW), F32), jax.ShapeDtypeStruct((1, DH), F32),
           jax.ShapeDtypeStruct((1, DH), F32)],
        scratch_shapes=[pltpu.VMEM((ts + 8, 3 * HW), F32)],
        compiler_params=_cp(("arbitrary",)),
    )(proj, proj, proj, proj, proj, conv_w, alog_f, dtb_f, gsq, gsk, dgq, dgk, dgv, dgf, dbf, dsqn, dskn)


def _conv_bwd(dc, conv_w, S):
    ts = _row_tile(S)
    hb = ts // 8
    n = S // ts

    def body(dc_ref, halo_ref, cw_ref, o_ref, buf):
        i = pl.program_id(0)
        buf[0:ts, :] = dc_ref[...]
        buf[ts:ts + 8, :] = jnp.where(i == n - 1, 0.0, halo_ref[...])
        cw = cw_ref[...]
        acc = cw[3:4, :] * buf[0:ts, :]
        for k in range(1, 4):
            acc = acc + cw[3 - k:4 - k, :] * buf[k:k + ts, :]
        o_ref[...] = acc

    return pl.pallas_call(
        body, name="conv_bwd", grid=(n,),
        in_specs=[pl.BlockSpec((ts, 3 * HW), lambda i: (i, 0)),
                  pl.BlockSpec((8, 3 * HW), lambda i: (jnp.minimum((i + 1) * hb, S // 8 - 1), 0)),
                  pl.BlockSpec((4, 3 * HW), lambda i: (0, 0))],
        out_specs=pl.BlockSpec((ts, 3 * HW), lambda i: (i, 0)),
        out_shape=jax.ShapeDtypeStruct((S, 3 * HW), F32),
        scratch_shapes=[pltpu.VMEM((ts + 8, 3 * HW), F32)],
        compiler_params=_cp(("parallel",)),
    )(dc, dc, conv_w)


def _gdn_masks():
    r = lax.broadcasted_iota(jnp.int32, (PAIR, PAIR), 0)
    c = lax.broadcasted_iota(jnp.int32, (PAIR, PAIR), 1)
    same = ((r >= CHUNK) & (c >= CHUNK)) | ((r < CHUNK) & (c < CHUNK))
    return dict(r=r, same=same, tril=same & (r >= c), strict=same & (r > c), triu=same & (c >= r), eye=r == c,
                in_a=r < CHUNK, last_a=r == CHUNK - 1, last_b=r == PAIR - 1)


def _each(fn, *cols):
    return [fn(*xs) for xs in zip(*cols)]


def _mul(a, b):
    return a * b


def _top(x):
    return x[:CHUNK]


def _bot(x):
    return x[CHUNK:]


def _rows(a, b):
    return jnp.concatenate([a, b], axis=0)


def _tri_inv(lm, eye):
    eye_f = eye.astype(F32)
    p = _each(lambda l: eye_f - l, lm)
    lp = _each(lambda l: _d3(l, l), lm)
    for it in range(5):
        p = _each(lambda a, b: a + _d3(a, b), p, lp)
        if it < 4:
            lp = _each(lambda b: _d3(b, b), lp)
    return p


def _gdn_block(m, q, k, v, g, beta):
    tril_f = m["tril"].astype(F32)
    col_sum = lambda mask: (lambda x: jnp.sum(jnp.where(mask, x, 0.0), axis=0, keepdims=True))
    gc = _each(lambda x: _dxl(tril_f, x), g)
    gcr = _each(col_sum(m["eye"]), gc)
    gam = _each(lambda a, b: jnp.where(m["tril"], jnp.exp(jnp.minimum(a - b, 0.0)), 0.0), gc, gcr)
    kb = _each(_mul, k, beta)
    vb = _each(_mul, v, beta)
    lm = _each(lambda a, b, c: jnp.where(m["strict"], _d3(a, b, NT) * c, 0.0), kb, k, gam)
    t = _tri_inv(lm, m["eye"])
    eg = _each(jnp.exp, gc)
    kbe = _each(_mul, kb, eg)
    u = _each(_d3, t, vb)
    w = _each(_d3, t, kbe)
    aqk = _each(lambda a, b, c: jnp.where(m["tril"], _d3(a, b, NT) * c, 0.0), q, k, gam)
    qd = _each(_mul, q, eg)
    ga = _each(col_sum(m["last_a"]), gc)
    gb = _each(col_sum(m["last_b"]), gc)
    e2 = _each(lambda a, b, c: jnp.exp(jnp.where(m["in_a"], a, b) - c), ga, gb, gc)
    kd = _each(_mul, k, e2)
    return dict(u=u, w=w, aqk=aqk, qd=qd, kd=kd, gam=gam, kb=kb, vb=vb, lm=lm, t=t, eg=eg, kbe=kbe, e2=e2,
                gla=_each(jnp.exp, ga), glb=_each(jnp.exp, gb))


def _gdn_fwd(gq, gk, gv, gf, bf, S):
    nb = S // PAIR

    def body(q_ref, k_ref, v_ref, g_ref, b_ref, o_ref, st_ref, s_scr):
        @pl.when(pl.program_id(0) == 0)
        def _():
            s_scr[...] = jnp.zeros_like(s_scr)

        m = _gdn_masks()
        heads = lambda ref: [ref[:, _hs(h)] for h in range(NH)]
        f = _gdn_block(m, heads(q_ref), heads(k_ref), heads(v_ref), heads(g_ref), heads(b_ref))
        u, w, qd, kd = f["u"], f["w"], f["qd"], f["kd"]
        s0 = [s_scr[h * DH:(h + 1) * DH, :] for h in range(NH)]
        vna = _each(lambda a, b, s: _top(a) - _d3(_top(b), s), u, w, s0)
        oa = _each(lambda a, s: _d3(_top(a), s), qd, s0)
        s1 = _each(lambda s, gl, a, vn: s * gl + _d3(_top(a), vn, TN), s0, f["gla"], kd, vna)
        vnb = _each(lambda a, b, s: _bot(a) - _d3(_bot(b), s), u, w, s1)
        ob = _each(lambda a, s: _d3(_bot(a), s), qd, s1)
        s2 = _each(lambda s, gl, a, vn: s * gl + _d3(_bot(a), vn, TN), s1, f["glb"], kd, vnb)
        outs = _each(lambda a, b, c, va, vb: _rows(a, b) + _d3(c, _rows(va, vb)), oa, ob, f["aqk"], vna, vnb)
        o_ref[...] = jnp.concatenate(outs, axis=1)
        st_ref[...] = jnp.concatenate(s0 + s1, axis=0)
        s_scr[...] = jnp.concatenate(s2, axis=0)

    blk = pl.BlockSpec((PAIR, HW), lambda i: (i, 0))
    return pl.pallas_call(
        body, name="gdn_fwd", grid=(nb,),
        in_specs=[blk] * 5,
        out_specs=[blk, pl.BlockSpec((2 * NH * DH, DH), lambda i: (i, 0))],
        out_shape=[jax.ShapeDtypeStruct((S, HW), F32), jax.ShapeDtypeStruct((nb * 2 * NH * DH, DH), F32)],
        scratch_shapes=[pltpu.VMEM((NH * DH, DH), F32)],
        compiler_params=_cp(("arbitrary",)),
    )(gq, gk, gv, gf, bf)


def _gdn_bwd(gq, gk, gv, gf, bf, states, do, S):
    nb = S // PAIR

    def body(q_ref, k_ref, v_ref, g_ref, b_ref, st_ref, do_ref, dq_o, dk_o, dv_o, dg_o, db_o, ds_scr):
        @pl.when(pl.program_id(0) == 0)
        def _():
            ds_scr[...] = jnp.zeros_like(ds_scr)

        m = _gdn_masks()
        ones = jnp.ones((PAIR, PAIR), F32)
        heads = lambda ref: [ref[:, _hs(h)] for h in range(NH)]
        q, k, v, beta, do = heads(q_ref), heads(k_ref), heads(v_ref), heads(b_ref), heads(do_ref)
        f = _gdn_block(m, q, k, v, heads(g_ref), beta)
        u, w, aqk, qd, kd, t = f["u"], f["w"], f["aqk"], f["qd"], f["kd"], f["t"]
        s0 = [st_ref[h * DH:(h + 1) * DH, :] for h in range(NH)]
        s1 = [st_ref[(NH + h) * DH:(NH + h + 1) * DH, :] for h in range(NH)]
        ds2 = [ds_scr[h * DH:(h + 1) * DH, :] for h in range(NH)]
        total = lambda a, b: jnp.sum(jnp.sum(a * b, axis=1, keepdims=True), axis=0, keepdims=True)
        vna = _each(lambda a, b, s: _top(a) - _d3(_top(b), s), u, w, s0)
        vnb = _each(lambda a, b, s: _bot(a) - _d3(_bot(b), s), u, w, s1)
        dvn_i = _each(lambda a, b: _d3(a, b, TN), aqk, do)
        dvnb = _each(lambda a, b, s: _bot(a) + _d3(_bot(b), s), dvn_i, kd, ds2)
        dqdb = _each(lambda a, s: _d3(_bot(a), s, NT), do, s1)
        dkdb = _each(lambda a, s: _d3(a, s, NT), vnb, ds2)
        dglb = _each(total, ds2, s1)
        dwb = _each(lambda a, s: -_d3(a, s, NT), dvnb, s1)
        ds1 = _each(lambda s, gl, a, b, c, d: s * gl + _d3(_bot(a), _bot(b), TN) - _d3(_bot(c), d, TN),
                    ds2, f["glb"], qd, do, w, dvnb)
        dvna = _each(lambda a, b, s: _top(a) + _d3(_top(b), s), dvn_i, kd, ds1)
        dqda = _each(lambda a, s: _d3(_top(a), s, NT), do, s0)
        dkda = _each(lambda a, s: _d3(a, s, NT), vna, ds1)
        dgla = _each(total, ds1, s0)
        dwa = _each(lambda a, s: -_d3(a, s, NT), dvna, s0)
        ds0 = _each(lambda s, gl, a, b, c, d: s * gl + _d3(_top(a), _top(b), TN) - _d3(_top(c), d, TN),
                    ds1, f["gla"], qd, do, w, dvna)
        dvn, dqd, dkd, dw = (_each(_rows, a, b) for a, b in ((dvna, dvnb), (dqda, dqdb), (dkda, dkdb), (dwa, dwb)))
        daqk = _each(lambda a, va, vb: jnp.where(m["tril"], _d3(a, _rows(va, vb), NT), 0.0), do, vna, vnb)
        dt = _each(lambda a, b, c, d: _d3(a, b, NT) + _d3(c, d, NT), dvn, f["vb"], dw, f["kbe"])
        dvb = _each(lambda a, b: _d3(a, b, TN), t, dvn)
        dkbe = _each(lambda a, b: _d3(a, b, TN), t, dw)
        dtt = _each(lambda a, b: _d3(a, b, NT), dt, t)
        dl = _each(lambda a, b: -jnp.where(m["strict"], _d3(a, b, TN), 0.0), t, dtt)
        dm = _each(_mul, dl, f["gam"])
        dn = _each(_mul, daqk, f["gam"])
        dkb = _each(lambda a, b, c, d: _d3(a, b) + c * d, dm, k, dkbe, f["eg"])
        dks = _each(lambda a, b, c, d, e, g, h, i: _d3(a, b, TN) + _d3(c, d, TN) + e * g + h * i,
                    dm, f["kb"], dn, q, dkd, f["e2"], beta, dkb)
        dqs = _each(lambda a, b, c, d: _d3(a, b) + c * d, dn, k, dqd, f["eg"])
        gm = _each(lambda a, b, c, d: a * b + c * d, dl, f["lm"], daqk, aqk)
        dkdkd = _each(_mul, dkd, kd)
        dgc = _each(lambda a, b, c, d, e, g: _dxr(a + b * c + d * e - g, ones) - _dxr(a, ones, TN),
                    gm, dqd, qd, dkbe, f["kbe"], dkdkd)
        same_f = m["same"].astype(F32)
        chunk_tot = _each(lambda a: _dxl(same_f, _dxr(a, ones)), dkdkd)
        last = m["last_a"] | m["last_b"]
        dgc = _each(lambda a, b, ga, gla, gb, glb: a + jnp.where(last, b + jnp.where(m["in_a"], ga * gla, gb * glb), 0.0),
                    dgc, chunk_tot, dgla, f["gla"], dglb, f["glb"])
        dbs = _each(lambda a, b, c, d: _dxr(a * b + c * d, ones), dkb, k, dvb, v)
        dvs = _each(_mul, beta, dvb)
        triu_f = m["triu"].astype(F32)
        dgs = _each(lambda a: _dxl(triu_f, a), dgc)
        for ref, parts in ((dq_o, dqs), (dk_o, dks), (dv_o, dvs), (dg_o, dgs), (db_o, dbs)):
            ref[...] = jnp.concatenate(parts, axis=1)
        ds_scr[...] = jnp.concatenate(ds0, axis=0)

    blk = pl.BlockSpec((PAIR, HW), lambda i: (nb - 1 - i, 0))
    o = jax.ShapeDtypeStruct((S, HW), F32)
    return pl.pallas_call(
        body, name="gdn_bwd", grid=(nb,),
        in_specs=[blk] * 5 + [pl.BlockSpec((2 * NH * DH, DH), lambda i: (nb - 1 - i, 0)), blk],
        out_specs=[blk] * 5, out_shape=[o] * 5,
        scratch_shapes=[pltpu.VMEM((NH * DH, DH), F32)],
        compiler_params=_cp(("arbitrary",)),
    )(gq, gk, gv, gf, bf, states, do)


SB_T = 256
SB_GROUP = 4
SB_GROUP_BWD = 4


def _group_sizes(g):
    sizes = []
    while g >= 1:
        sizes.append(g)
        g //= 2
    return sizes


def _sb_iotas(t):
    return lax.broadcasted_iota(jnp.int32, (t, t), 0), lax.broadcasted_iota(jnp.int32, (t, t), 1)


def _sb_scores(q, k, mask):
    z = _dot(q, k, NT) * DH ** -0.5
    ls = jnp.minimum(z, 0.0) - jnp.log(1.0 + jnp.exp(-jnp.abs(z)))
    lneg = ls - z
    if mask is not None:
        lneg = jnp.where(mask, lneg, 0.0)
    return ls, lneg


def _prefix(x, u):
    xh, xl = _split(x, 2)
    return _dot(xh, u) + _dot(xl, u)


def _sb_fwd(sqn, skn, svb, S):
    t = min(SB_T, S)

    def body(q_ref, k_ref, v_ref, o_ref, t_ref):
        qb = pl.program_id(1)
        q = q_ref[...]
        r, c = _sb_iotas(t)
        diag = c < r
        u_after = (r > c).astype(BF16)

        def tiles(k0s, run, mask):
            sc = _each(lambda k0: _sb_scores(q, k_ref[pl.ds(k0, t), :], mask), k0s)
            ls, lneg = [s[0] for s in sc], [s[1] for s in sc]
            sums = _each(lambda x: jnp.sum(x, axis=1, keepdims=True), lneg)
            pre = _each(lambda x: _prefix(x, u_after), lneg)
            runs = [run]
            for s in sums:
                runs.append(runs[-1] + s)
            att = _each(lambda a, b, rn: jnp.exp(a + (rn + b)), ls, pre, runs[:-1])
            if mask is not None:
                att = _each(lambda a: jnp.where(mask, a, 0.0), att)
            parts = _each(lambda a, k0: _dot(a.astype(BF16), v_ref[pl.ds(k0, t), :]), att, k0s)
            return sum(parts[1:], parts[0]), runs[-1]

        acc, run = tiles([pl.multiple_of(qb * t, t)], jnp.zeros((t, 1), F32), diag)

        carry, done = (acc, run), 0
        for size in _group_sizes(SB_GROUP):
            n = (qb - done) // size

            def group(i, carry, size=size, done=done):
                acc, run = carry
                first = qb - 1 - done - size * i
                part, run = tiles([pl.multiple_of((first - j) * t, t) for j in range(size)], run, None)
                return acc + part, run

            carry = lax.fori_loop(0, n, group, carry)
            done = done + n * size
        acc, run = carry
        o_ref[...] = acc
        t_ref[...] = jnp.broadcast_to(run, (t, DH))

    qspec = pl.BlockSpec((t, DH), lambda h, i: (i, h))
    kspec = pl.BlockSpec((S, DH), lambda h, i: (0, h))
    return pl.pallas_call(
        body, name="sb_fwd", grid=(NH, S // t),
        in_specs=[qspec, kspec, kspec], out_specs=[qspec, qspec],
        out_shape=[jax.ShapeDtypeStruct((S, HW), F32)] * 2,
        compiler_params=_cp(("parallel", "arbitrary")),
    )(sqn, skn, svb)


def _sb_bwd(sqn, skn, svb, do, tot, S):
    t = min(SB_T, S)

    def body(q_ref, k_ref, v_ref, do_ref, t_ref, dq_o, dk_o, dv_o):
        qb = pl.program_id(1)

        @pl.when(qb == 0)
        def _():
            dk_o[...] = jnp.zeros_like(dk_o)
            dv_o[...] = jnp.zeros_like(dv_o)

        q = q_ref[...]
        do = do_ref[...].astype(BF16)
        tot_l = jnp.concatenate([t_ref[...]] * (t // DH), axis=1)
        r, c = _sb_iotas(t)
        diag = c < r
        u_upto = (r <= c).astype(BF16)
        u_before = (r < c).astype(BF16)

        def tiles(k0s, run_l, run_e, mask):
            rowsum = lambda x: jnp.sum(x, axis=1, keepdims=True)
            ks = [k_ref[pl.ds(k0, t), :] for k0 in k0s]
            vs = [v_ref[pl.ds(k0, t), :] for k0 in k0s]
            sc = _each(lambda k: _sb_scores(q, k, mask), ks)
            ls, lneg = [s[0] for s in sc], [s[1] for s in sc]
            sums_l = _each(rowsum, lneg)
            pre_l = _each(lambda x: _prefix(x, u_upto), lneg)
            runs_l = [run_l]
            for s in sums_l:
                runs_l.append(runs_l[-1] + s)
            att = _each(lambda a, b, rn: jnp.exp(a + (tot_l - (rn + b))), ls, pre_l, runs_l[:-1])
            if mask is not None:
                att = _each(lambda a: jnp.where(mask, a, 0.0), att)
            e = _each(lambda v, a: _dot(do, v, NT) * a, vs, att)
            sums_e = _each(rowsum, e)
            pre_e = _each(lambda x: _prefix(x, u_before), e)
            runs_e = [run_e]
            for s in sums_e:
                runs_e.append(runs_e[-1] + s)
            sg = _each(jnp.exp, ls)
            dz = _each(lambda a, b, rn, s: a * (1.0 - s) - (rn + b) * s, e, pre_e, runs_e[:-1], sg)
            if mask is not None:
                dz = _each(lambda a: jnp.where(mask, a, 0.0), dz)
            dz = _each(lambda a: (a * DH ** -0.5).astype(BF16), dz)
            dvs = _each(lambda a: _dot(a.astype(BF16), do, TN), att)
            dks = _each(lambda a: _dot(a, q, TN), dz)
            dqs = _each(_dot, dz, ks)
            for k0, dv, dk in zip(k0s, dvs, dks):
                dv_o[pl.ds(k0, t), :] += dv
                dk_o[pl.ds(k0, t), :] += dk
            return sum(dqs[1:], dqs[0]), runs_l[-1], runs_e[-1]

        z1 = jnp.zeros((t, 1), F32)
        carry, done = (jnp.zeros((t, DH), F32), z1, z1), 0
        for size in _group_sizes(SB_GROUP_BWD):
            n = (qb - done) // size

            def group(i, carry, size=size, done=done):
                dq, run_l, run_e = carry
                first = done + size * i
                part, run_l, run_e = tiles([pl.multiple_of((first + j) * t, t) for j in range(size)], run_l, run_e,
                                           None)
                return dq + part, run_l, run_e

            carry = lax.fori_loop(0, n, group, carry)
            done = done + n * size
        dq, run_l, run_e = carry
        part, _, _ = tiles([pl.multiple_of(qb * t, t)], run_l, run_e, diag)
        dq_o[...] = dq + part

    qspec = pl.BlockSpec((t, DH), lambda h, i: (i, h))
    kspec = pl.BlockSpec((S, DH), lambda h, i: (0, h))
    o = jax.ShapeDtypeStruct((S, HW), F32)
    return pl.pallas_call(
        body, name="sb_bwd", grid=(NH, S // t),
        in_specs=[qspec, kspec, kspec, qspec, qspec], out_specs=[qspec, kspec, kspec],
        out_shape=[o, o, o],
        compiler_params=_cp(("parallel", "arbitrary")),
    )(sqn, skn, svb, do, tot)


def _mem_probs(qn, kn):
    s = _dot(qn, kn.astype(BF16), NT) * DH ** -0.5
    p = jnp.exp(s - jnp.max(s, axis=-1, keepdims=True))
    return p / jnp.sum(p, axis=-1, keepdims=True)


def _mem_fwd(qmn, kv, gmk, S):
    ts = _row_tile(S)

    def body(q_ref, kv_ref, gk_ref, o_ref):
        for h in range(NH):
            kn, _ = _rms(kv_ref[:, _hs(h)], gk_ref[...])
            p = _mem_probs(q_ref[:, _hs(h)], kn)
            o_ref[:, _hs(h)] = _dbf(p, kv_ref[:, HW + h * DH:HW + (h + 1) * DH])

    return pl.pallas_call(
        body, name="mem_fwd", grid=(S // ts,),
        in_specs=[pl.BlockSpec((ts, HW), lambda i: (i, 0)), pl.BlockSpec((NMEM, 2 * HW), lambda i: (0, 0)),
                  pl.BlockSpec((1, DH), lambda i: (0, 0))],
        out_specs=pl.BlockSpec((ts, HW), lambda i: (i, 0)),
        out_shape=jax.ShapeDtypeStruct((S, HW), F32),
        compiler_params=_cp(("parallel",)),
    )(qmn, kv, gmk)


def _mem_bwd(proj, qmn, kv, gmq, gmk, do, S):
    ts = _row_tile(S)
    n = S // ts

    def body(mq_ref, q_ref, kv_ref, gq_ref, gk_ref, do_ref, dmq_o, dkv_o, dgq_o, dgk_o, dkn_scr):
        i = pl.program_id(0)

        @pl.when(i == 0)
        def _():
            dkv_o[...] = jnp.zeros_like(dkv_o)
            dgq_o[...] = jnp.zeros_like(dgq_o)
            dkn_scr[...] = jnp.zeros_like(dkn_scr)

        dgq = jnp.zeros((1, DH), F32)
        for h in range(NH):
            km = kv_ref[:, _hs(h)]
            vm = kv_ref[:, HW + h * DH:HW + (h + 1) * DH].astype(BF16)
            kn, _ = _rms(km, gk_ref[...])
            qn = q_ref[:, _hs(h)]
            p = _mem_probs(qn, kn)
            dob = do_ref[:, _hs(h)].astype(BF16)
            dkv_o[:, HW + h * DH:HW + (h + 1) * DH] += _dot(p.astype(BF16), dob, TN)
            dp = _dot(dob, vm, NT)
            dsc = (p * (dp - jnp.sum(dp * p, axis=-1, keepdims=True)) * DH ** -0.5).astype(BF16)
            dkn_scr[:, _hs(h)] += _dot(dsc, qn, TN)
            x = mq_ref[:, _hs(h)]
            _, r = _rms(x, gq_ref[...])
            dx, dg = _rms_bwd(_dot(dsc, kn.astype(BF16)), x, gq_ref[...], r)
            dmq_o[:, _hs(h)] = dx
            dgq = dgq + dg
        dgq_o[...] += dgq

        @pl.when(i == n - 1)
        def _():
            dgk = jnp.zeros((1, DH), F32)
            for h in range(NH):
                km = kv_ref[:, _hs(h)]
                _, r = _rms(km, gk_ref[...])
                dx, dg = _rms_bwd(dkn_scr[:, _hs(h)], km, gk_ref[...], r)
                dkv_o[:, _hs(h)] = dx
                dgk = dgk + dg
            dgk_o[...] = dgk

    full = lambda r, c: pl.BlockSpec((r, c), lambda i: (0, 0))
    t512 = pl.BlockSpec((ts, HW), lambda i: (i, 0))
    return pl.pallas_call(
        body, name="mem_bwd", grid=(n,),
        in_specs=[pl.BlockSpec((ts, HW), lambda i: (i, CB_MQ)), t512, full(NMEM, 2 * HW), full(1, DH), full(1, DH),
                  t512],
        out_specs=[t512, full(NMEM, 2 * HW), full(1, DH), full(1, DH)],
        out_shape=[jax.ShapeDtypeStruct((S, HW), F32), jax.ShapeDtypeStruct((NMEM, 2 * HW), F32),
                   jax.ShapeDtypeStruct((1, DH), F32), jax.ShapeDtypeStruct((1, DH), F32)],
        scratch_shapes=[pltpu.VMEM((NMEM, HW), F32)],
        compiler_params=_cp(("arbitrary",)),
    )(proj, qmn, kv, gmq, gmk, do)


def _gated_gdn(o, z, g):
    sg = _sigmoid(z)
    outs, rs = [], []
    for h in range(NH):
        y, r = _rms(o[:, _hs(h)], g)
        outs.append(y * (z[:, _hs(h)] * sg[:, _hs(h)]))
        rs.append(r)
    return jnp.concatenate(outs, axis=1), rs, sg


def _merge_fwd(x, proj, ogdn, osb, omem, ggdn, wbg, wbs, wbm, wo, S):
    ts = _narrow_tile(S)

    def body(x_ref, z_ref, g0_ref, g1_ref, g2_ref, og_ref, os_ref, om_ref, gg_ref, wbg_ref, wbs_ref, wbm_ref,
             wo_ref, x1_o, mix_o):
        on, _, _ = _gated_gdn(og_ref[...], z_ref[...], gg_ref[...])
        mix = (_sigmoid(g0_ref[...]) * _dbf(on, wbg_ref[...]) + _sigmoid(g1_ref[...]) * _dbf(os_ref[...], wbs_ref[...])
               + _sigmoid(g2_ref[...]) * _dbf(om_ref[...], wbm_ref[...]))
        mix_o[...] = mix.astype(BF16)
        x1_o[...] = x_ref[...] + _dbf(mix, wo_ref[...])

    t512 = pl.BlockSpec((ts, HW), lambda i: (i, 0))
    t1k = pl.BlockSpec((ts, D), lambda i: (i, 0))
    gate = lambda j: pl.BlockSpec((ts, D), lambda i: (i, 4 + j))
    full = lambda r, c: pl.BlockSpec((r, c), lambda i: (0, 0))
    return pl.pallas_call(
        body, name="merge_fwd", grid=(S // ts,),
        in_specs=[t1k, pl.BlockSpec((ts, HW), lambda i: (i, CB_Z)), gate(0), gate(1), gate(2), t512, t512, t512,
                  full(1, DH), full(HW, D), full(HW, D), full(HW, D), full(D, D)],
        out_specs=[t1k, t1k],
        out_shape=[jax.ShapeDtypeStruct((S, D), F32), jax.ShapeDtypeStruct((S, D), BF16)],
        compiler_params=_cp(("parallel",)),
    )(x, proj, proj, proj, proj, ogdn, osb, omem, ggdn, wbg, wbs, wbm, wo)


def _merge_bwd(dmix, proj, ogdn, osb, omem, ggdn, wbg, wbs, wbm, S):
    ts = _narrow_tile(S)

    def body(dm_ref, z_ref, g0_ref, g1_ref, g2_ref, og_ref, os_ref, om_ref, gg_ref, wbg_ref, wbs_ref, wbm_ref,
             dgl0_o, dgl1_o, dgl2_o, dog_o, dz_o, dos_o, dom_o, dwbg_o, dwbs_o, dwbm_o, dgg_o):
        @pl.when(pl.program_id(0) == 0)
        def _():
            for ref in (dwbg_o, dwbs_o, dwbm_o, dgg_o):
                ref[...] = jnp.zeros_like(ref)

        dm = dm_ref[...]
        og = og_ref[...]
        z = z_ref[...]
        on, rs, sg = _gated_gdn(og, z, gg_ref[...])
        branch = ((on, g0_ref, wbg_ref, dgl0_o, dwbg_o), (os_ref[...], g1_ref, wbs_ref, dgl1_o, dwbs_o),
                  (om_ref[...], g2_ref, wbm_ref, dgl2_o, dwbm_o))
        dos = []
        for o, g_ref, w_ref, dgl_o, dw_o in branch:
            ob = o.astype(BF16)
            gate = _sigmoid(g_ref[...])
            dgl_o[...] = dm * _dot(ob, w_ref[...]) * gate * (1.0 - gate)
            dy = (dm * gate).astype(BF16)
            dw_o[...] += _dot(ob, dy, TN)
            dos.append(_dot(dy, w_ref[...], NT))
        dos_o[...] = dos[1]
        dom_o[...] = dos[2]
        don = dos[0]
        dgg = jnp.zeros((1, DH), F32)
        for h in range(NH):
            oh, zh, sh = og[:, _hs(h)], z[:, _hs(h)], sg[:, _hs(h)]
            y = oh * rs[h] * gg_ref[...]
            dz_o[:, _hs(h)] = don[:, _hs(h)] * y * (sh * (1.0 + zh * (1.0 - sh)))
            dx, dg = _rms_bwd(don[:, _hs(h)] * (zh * sh), oh, gg_ref[...], rs[h])
            dog_o[:, _hs(h)] = dx
            dgg = dgg + dg
        dgg_o[...] += dgg

    t512 = pl.BlockSpec((ts, HW), lambda i: (i, 0))
    t1k = pl.BlockSpec((ts, D), lambda i: (i, 0))
    gate = lambda j: pl.BlockSpec((ts, D), lambda i: (i, 4 + j))
    full = lambda r, c: pl.BlockSpec((r, c), lambda i: (0, 0))
    s1k = jax.ShapeDtypeStruct((S, D), F32)
    s512 = jax.ShapeDtypeStruct((S, HW), F32)
    wsh = jax.ShapeDtypeStruct((HW, D), F32)
    return pl.pallas_call(
        body, name="merge_bwd", grid=(S // ts,),
        in_specs=[t1k, pl.BlockSpec((ts, HW), lambda i: (i, CB_Z)), gate(0), gate(1), gate(2), t512, t512, t512,
                  full(1, DH), full(HW, D), full(HW, D), full(HW, D)],
        out_specs=[t1k, t1k, t1k, t512, t512, t512, t512, full(HW, D), full(HW, D), full(HW, D), full(1, DH)],
        out_shape=[s1k, s1k, s1k, s512, s512, s512, s512, wsh, wsh, wsh, jax.ShapeDtypeStruct((1, DH), F32)],
        compiler_params=_cp(("arbitrary",)),
    )(dmix, proj, proj, proj, proj, ogdn, osb, omem, ggdn, wbg, wbs, wbm)


def _loss_grad(y, target, S):
    ts = _row_tile(S)

    def body(y_ref, t_ref, dy_o, loss_o):
        @pl.when(pl.program_id(0) == 0)
        def _():
            loss_o[...] = jnp.zeros_like(loss_o)

        err = y_ref[...] - t_ref[...]
        dy_o[...] = err * (1.0 / D)
        per_tok = jnp.sum(err * err, axis=1, keepdims=True) * (1.0 / D)
        loss_o[...] += 0.5 * jnp.sum(per_tok, axis=0, keepdims=True)

    t1k = pl.BlockSpec((ts, D), lambda i: (i, 0))
    return pl.pallas_call(
        body, name="loss_grad", grid=(S // ts,), in_specs=[t1k, t1k],
        out_specs=[t1k, pl.BlockSpec((1, 1), lambda i: (0, 0))],
        out_shape=[jax.ShapeDtypeStruct((S, D), F32), jax.ShapeDtypeStruct((1, 1), F32)],
        compiler_params=_cp(("arbitrary",)),
    )(y, target)


def _norm_bwd(name, dh, x, g, res):
    rows = x.shape[0]
    ts = min(_row_tile(rows), rows)

    def body(*refs):
        dh_ref, x_ref, g_ref = refs[:3]
        dx_o, dg_o = refs[-2:]

        @pl.when(pl.program_id(0) == 0)
        def _():
            dg_o[...] = jnp.zeros_like(dg_o)

        xv = x_ref[...]
        _, r = _rms(xv, g_ref[...])
        dx, dg = _rms_bwd(dh_ref[...], xv, g_ref[...], r)
        dx_o[...] = dx if res is None else dx + refs[3][...]
        dg_o[...] += dg

    t1k = pl.BlockSpec((ts, D), lambda i: (i, 0))
    gsp = pl.BlockSpec((1, D), lambda i: (0, 0))
    ops = [dh, x, g] + ([] if res is None else [res])
    return pl.pallas_call(
        body, name=name, grid=(rows // ts,), in_specs=[t1k, t1k, gsp] + ([] if res is None else [t1k]),
        out_specs=[t1k, gsp],
        out_shape=[jax.ShapeDtypeStruct((rows, D), F32), jax.ShapeDtypeStruct((1, D), F32)],
        compiler_params=_cp(("arbitrary",)),
    )(*ops)


def _adamw(name, gall, w, m, v):
    rows = w.shape[0]
    nsrc = gall.shape[0]
    tr = min(1216, rows)
    assert rows % tr == 0

    def body(g_ref, w_ref, m_ref, v_ref, g_o, d_o, m_o, v_o):
        g = g_ref[0].astype(F32)
        for j in range(1, nsrc):
            g = g + g_ref[j].astype(F32)
        m_new = ADAM_B1 * m_ref[...] + (1.0 - ADAM_B1) * g
        v_new = ADAM_B2 * v_ref[...] + (1.0 - ADAM_B2) * jnp.square(g)
        m_hat = m_new / (1.0 - ADAM_B1 ** ADAM_STEP)
        v_hat = v_new / (1.0 - ADAM_B2 ** ADAM_STEP)
        g_o[...] = g
        d_o[...] = -ADAM_LR * (m_hat / (jnp.sqrt(v_hat) + ADAM_EPS) + ADAM_WD * w_ref[...])
        m_o[...] = m_new
        v_o[...] = v_new

    t = pl.BlockSpec((tr, LANES), lambda i: (i, 0))
    o = jax.ShapeDtypeStruct((rows, LANES), F32)
    return pl.pallas_call(
        body, name=name, grid=(rows // tr,),
        in_specs=[pl.BlockSpec((nsrc, tr, LANES), lambda i: (0, i, 0)), t, t, t],
        out_specs=[t, t, t, t], out_shape=[o, o, o, o],
        compiler_params=_cp(("parallel",)),
    )(gall, w, m, v)


def _pair_sum(mine, theirs):
    rows = mine.shape[1]
    tr = min(1216, rows)
    assert rows % tr == 0
    core = lax.axis_index("c").astype(jnp.int32).reshape(1)

    def body(c_ref, a_ref, b_ref, o_ref):
        o_ref[...] = (a_ref[...].astype(F32) + b_ref[...].astype(F32)).astype(o_ref.dtype)

    blk = pl.BlockSpec((1, tr, LANES), lambda j, i, c_ref: (j, i, 0))
    return pl.pallas_call(
        body, name="pair_sum",
        grid_spec=pltpu.PrefetchScalarGridSpec(
            num_scalar_prefetch=1, grid=(NDEV // 2, rows // tr),
            in_specs=[pl.BlockSpec((1, tr, LANES), lambda j, i, c_ref: (2 * j + c_ref[0], i, 0)), blk],
            out_specs=blk),
        out_shape=jax.ShapeDtypeStruct((NDEV // 2, rows, LANES), mine.dtype),
        compiler_params=_cp(("parallel", "parallel")),
    )(core, mine, theirs)


HBM_SPEC = pl.BlockSpec(memory_space=pltpu.HBM)


def _remote(src, dst, send_sems, recv_sems, k, to):
    return pltpu.make_async_remote_copy(src_ref=src, dst_ref=dst, send_sem=send_sems.at[k], recv_sem=recv_sems.at[k],
                                        device_id=to, device_id_type=pl.DeviceIdType.MESH)


def _gather(name, x):
    rows, cols = x.shape

    def body(x_ref, o_ref, send_sems, recv_sems, local_sem):
        ix, iy, ic = lax.axis_index("x"), lax.axis_index("y"), lax.axis_index("c")
        me, sibling = (ix, iy, ic), (ix, iy, 1 - ic)
        chips = [(1 - ix, iy), (ix, 1 - iy), (1 - ix, 1 - iy)]

        def slab(px, py, pc):
            return o_ref.at[4 * px + 2 * py + pc]

        def copy(k, block, to, src=None):
            return _remote(slab(*block) if src is None else src, slab(*block), send_sems, recv_sems, k, to)

        mine = pltpu.make_async_copy(x_ref, slab(*me), local_sem)
        mine.start()
        first = [copy(0, me, sibling, src=x_ref)]
        first += [copy(1 + j, me, (*chip, ic), src=x_ref) for j, chip in enumerate(chips)]
        for cp in first:
            cp.start()
        passed = [copy(4 + j, (*chip, ic), sibling) for j, chip in enumerate(chips)]
        for j, chip in enumerate(chips):
            copy(1 + j, (*chip, ic), me).wait_recv()
            passed[j].start()
        copy(0, sibling, me).wait_recv()
        for j, chip in enumerate(chips):
            copy(4 + j, (*chip, 1 - ic), me).wait_recv()
        for cp in first + passed:
            cp.wait_send()
        mine.wait()

    return pl.pallas_call(
        body, name=name, in_specs=[HBM_SPEC], out_specs=HBM_SPEC,
        out_shape=jax.ShapeDtypeStruct((NDEV, rows, cols), x.dtype),
        scratch_shapes=[pltpu.SemaphoreType.DMA((NDEV - 1,)), pltpu.SemaphoreType.DMA((NDEV - 1,)),
                        pltpu.SemaphoreType.DMA],
    )(x)


def _sibling_exchange(name, x):
    rows, cols = x.shape[-2:]
    nchip = NDEV // 2

    def body(x_ref, o_ref, send_sems, recv_sems):
        ix, iy, ic = lax.axis_index("x"), lax.axis_index("y"), lax.axis_index("c")
        copies = [_remote(x_ref.at[2 * j + (1 - ic)], o_ref.at[j], send_sems, recv_sems, j, (ix, iy, 1 - ic))
                  for j in range(nchip)]
        for cp in copies:
            cp.start()
        for cp in copies:
            cp.wait()

    return pl.pallas_call(
        body, name=name, in_specs=[HBM_SPEC], out_specs=HBM_SPEC,
        out_shape=jax.ShapeDtypeStruct((nchip, rows, cols), x.dtype),
        scratch_shapes=[pltpu.SemaphoreType.DMA((nchip,)), pltpu.SemaphoreType.DMA((nchip,))],
    )(x)


def _chip_exchange(name, x):
    rows, cols = x.shape[-2:]
    nchip = NDEV // 2

    def body(x_ref, o_ref, send_sems, recv_sems, local_sem):
        ix, iy, ic = lax.axis_index("x"), lax.axis_index("y"), lax.axis_index("c")
        my_chip = 2 * ix + iy
        own = pltpu.make_async_copy(x_ref.at[my_chip], o_ref.at[my_chip], local_sem)
        own.start()
        copies = []
        for k in range(1, nchip):
            px, py = ix ^ (k >> 1), iy ^ (k & 1)
            copies.append(_remote(x_ref.at[2 * px + py], o_ref.at[my_chip], send_sems, recv_sems, k - 1, (px, py, ic)))
        for cp in copies:
            cp.start()
        for cp in copies:
            cp.wait()
        own.wait()

    return pl.pallas_call(
        body, name=name, in_specs=[HBM_SPEC], out_specs=HBM_SPEC,
        out_shape=jax.ShapeDtypeStruct((nchip, rows, cols), x.dtype),
        scratch_shapes=[pltpu.SemaphoreType.DMA((nchip - 1,)), pltpu.SemaphoreType.DMA((nchip - 1,)),
                        pltpu.SemaphoreType.DMA],
    )(x)


COL_SHARDED = {"w_in": (D, D_IN), "w_br_gdn": (HW, D), "w_br_sb": (HW, D), "w_br_mem": (HW, D), "w_up": (D, DFF),
               "conv_w": (4, 3 * HW)}
ROW_SHARDED = {"w_mem_kv": (D, 2 * HW), "w_o": (D, D), "w_down": (DFF, D)}


def _pack_rows(parts, total):
    flat = jnp.concatenate([p.reshape(-1, LANES) for p in parts], axis=0)
    return jnp.pad(flat, ((0, total - flat.shape[0]), (0, 0)))


def _pack_full_grads(grads):
    parts = []
    for name in BIG:
        g = grads[name]
        if name in COL_SHARDED:
            r, c = COL_SHARDED[name]
            g = g.reshape(r, NDEV, c // NDEV).transpose(1, 0, 2)
        parts.append(g.reshape(NDEV, -1, LANES))
    flat = jnp.concatenate(parts, axis=1)
    return jnp.pad(flat, ((0, 0), (0, R_BIG - flat.shape[1]), (0, 0)))


def _unpack_gathered(slabs):
    out, pos = {}, 0
    for name, rows in zip(BIG, BIG_ROWS):
        g = slabs[:, pos:pos + rows]
        pos += rows
        if name in COL_SHARDED:
            r, c = COL_SHARDED[name]
            out[name] = g.reshape(NDEV, r, c // NDEV).transpose(1, 0, 2).reshape(r, c)
        else:
            r, c = ROW_SHARDED[name]
            out[name] = g.reshape(r, c)
    return out


def _unpack_shard(flat, shapes):
    out, pos = {}, 0
    for name, rows in zip(BIG, BIG_ROWS):
        out[name] = flat[pos:pos + rows].reshape(shapes[name])
        pos += rows
    return out


def _pack_small(vals):
    rows = []
    for name, n in zip(SMALL, SMALL_ROWS):
        v = vals[name].reshape(-1)
        rows.append(jnp.pad(v, (0, n * LANES - v.shape[0])).reshape(n, LANES))
    return _pack_rows(rows, R_SMALL)


def _unpack_small(flat, shapes):
    out, pos = {}, 0
    for name, n in zip(SMALL, SMALL_ROWS):
        size = shapes[name][-1]
        out[name] = flat[pos:pos + n].reshape(-1)[:size].reshape(shapes[name])
        pos += n
    return out


def _pad_w_in(w):
    return jnp.concatenate([w[:, :2048], w[:, 2056:], w[:, 2048:2056], jnp.zeros((D, D_INP - D_IN), w.dtype)], axis=1)


def _unpad_w_in(w):
    return jnp.concatenate([w[:, :2048], w[:, 7168:7176], w[:, 2048:7168]], axis=1)


def _per_head(v):
    return jnp.repeat(v.reshape(NH), DH).reshape(1, HW)


def _local_step(x, mem, target, w, sm):
    S = x.shape[0]
    ts = _row_tile(S)
    alog_f, dtb_f = _per_head(sm["a_log"]), _per_head(sm["dt_bias"])

    proj = _mm("in_proj", x, w["w_in"], "nn", ts, 1536, D, pro="rms", pro_g=sm["norm1_g"])
    gq, gk, gv, gf, bf, sqn, skn, svb, qmn = _pre_fwd(proj, w["conv_w"], alog_f, dtb_f, sm["sb_q_norm_g"],
                                                      sm["sb_k_norm_g"], sm["mem_q_norm_g"], S)
    ogdn, states = _gdn_fwd(gq, gk, gv, gf, bf, S)
    osb, sb_tot = _sb_fwd(sqn, skn, svb, S)
    kv = _mm("mem_kv", mem, w["w_mem_kv"], "nn", NMEM, D, D, pro="rms", pro_g=sm["mem_norm_g"])
    omem = _mem_fwd(qmn, kv, sm["mem_k_norm_g"], S)
    x1, mix = _merge_fwd(x, proj, ogdn, osb, omem, sm["gdn_norm_g"], w["w_br_gdn"], w["w_br_sb"], w["w_br_mem"],
                         w["w_o"], S)
    up = _mm("mlp_up", x1, w["w_up"], "nn", ts, 2048, D, pro="rms", pro_g=sm["norm2_g"])
    x2 = _mm("mlp_down", up, w["w_down"], "nn", ts, D, 1024, pro="relu2", epi="add", epi_x=x1)
    dy, loss = _loss_grad(x2, target, S)

    g = {}
    dup = _mm("d_up", dy, w["w_down"], "nt", ts, 1024, D, epi="drelu2", epi_x=up)
    g["w_down"] = _mm("dw_down", up, dy, "tn", 1024, D, 512, pro="relu2")
    g["w_up"] = _mm("dw_up", x1, dup, "tn", D, 1024, 512, pro="rms", pro_g=sm["norm2_g"])
    dh2 = _mm("d_h2", dup, w["w_up"], "nt", ts, D, 1024)
    dx1, g["norm2_g"] = _norm_bwd("norm2_bwd", dh2, x1, sm["norm2_g"], dy)

    dmix = _mm("d_mix", dx1, w["w_o"], "nt", ts, D, D)
    g["w_o"] = _mm("dw_o", mix, dx1, "tn", D, D, 512)
    (dgl0, dgl1, dgl2, dogdn, dz, dosb, domem, g["w_br_gdn"], g["w_br_sb"], g["w_br_mem"],
     g["gdn_norm_g"]) = _merge_bwd(dmix, proj, ogdn, osb, omem, sm["gdn_norm_g"], w["w_br_gdn"], w["w_br_sb"],
                                   w["w_br_mem"], S)
    dmq, dkv, g["mem_q_norm_g"], g["mem_k_norm_g"] = _mem_bwd(proj, qmn, kv, sm["mem_q_norm_g"], sm["mem_k_norm_g"],
                                                             domem, S)
    g["w_mem_kv"] = _mm("dw_mem_kv", mem, dkv, "tn", D, D, NMEM, pro="rms", pro_g=sm["mem_norm_g"])
    dmn = _mm("d_mem_n", dkv, w["w_mem_kv"], "nt", NMEM, D, D)
    _, g["mem_norm_g"] = _norm_bwd("mem_norm_bwd", dmn, mem, sm["mem_norm_g"], None)
    dsqn, dskn, dsv = _sb_bwd(sqn, skn, svb, dosb, sb_tot, S)
    dgq, dgk, dgv, dgf, dbf = _gdn_bwd(gq, gk, gv, gf, bf, states, dogdn, S)
    dc, dab, dsq, dsk, g["conv_w"], dal_f, ddt_f, g["sb_q_norm_g"], g["sb_k_norm_g"] = _pre_bwd(
        proj, w["conv_w"], alog_f, dtb_f, sm["sb_q_norm_g"], sm["sb_k_norm_g"], dgq, dgk, dgv, dgf, dbf, dsqn, dskn, S)
    g["a_log"] = dal_f.reshape(NH, DH)[:, 0].reshape(1, NH)
    g["dt_bias"] = ddt_f.reshape(NH, DH)[:, 0].reshape(1, NH)
    dqkv = _conv_bwd(dc, w["conv_w"], S)

    dproj = jnp.concatenate([dqkv, dz, dsq, dsk, dsv, dmq, dgl0, dgl1, dgl2, dab], axis=1)
    g["w_in"] = _mm("dw_in", x, dproj, "tn", D, 1536, 512, pro="rms", pro_g=sm["norm1_g"])
    dh = _mm("d_h", dproj, w["w_in"], "nt", ts, D, 1536)
    dx, g["norm1_g"] = _norm_bwd("norm1_bwd", dh, x, sm["norm1_g"], dx1)
    return loss[0, 0], dx, g


def kernel(x, mem, norm1_g, w_in, conv_w, a_log, dt_bias, gdn_norm_g, sb_q_norm_g, sb_k_norm_g, mem_norm_g, w_mem_kv, mem_q_norm_g, mem_k_norm_g, w_br_gdn, w_br_sb, w_br_mem, w_o, norm2_g, w_up, w_down, loss_target, m_norm1_g, m_w_in, m_conv_w, m_a_log, m_dt_bias, m_gdn_norm_g, m_sb_q_norm_g, m_sb_k_norm_g, m_mem_norm_g, m_w_mem_kv, m_mem_q_norm_g, m_mem_k_norm_g, m_w_br_gdn, m_w_br_sb, m_w_br_mem, m_w_o, m_norm2_g, m_w_up, m_w_down, v_norm1_g, v_w_in, v_conv_w, v_a_log, v_dt_bias, v_gdn_norm_g, v_sb_q_norm_g, v_sb_k_norm_g, v_mem_norm_g, v_w_mem_kv, v_mem_q_norm_g, v_mem_k_norm_g, v_w_br_gdn, v_w_br_sb, v_w_br_mem, v_w_o, v_norm2_g, v_w_up, v_w_down):
    given = dict(norm1_g=norm1_g, w_in=w_in, conv_w=conv_w, a_log=a_log, dt_bias=dt_bias, gdn_norm_g=gdn_norm_g,
                 sb_q_norm_g=sb_q_norm_g, sb_k_norm_g=sb_k_norm_g, mem_norm_g=mem_norm_g, w_mem_kv=w_mem_kv,
                 mem_q_norm_g=mem_q_norm_g, mem_k_norm_g=mem_k_norm_g, w_br_gdn=w_br_gdn, w_br_sb=w_br_sb,
                 w_br_mem=w_br_mem, w_o=w_o, norm2_g=norm2_g, w_up=w_up, w_down=w_down)
    mom1 = dict(norm1_g=m_norm1_g, w_in=m_w_in, conv_w=m_conv_w, a_log=m_a_log, dt_bias=m_dt_bias,
                gdn_norm_g=m_gdn_norm_g, sb_q_norm_g=m_sb_q_norm_g, sb_k_norm_g=m_sb_k_norm_g,
                mem_norm_g=m_mem_norm_g, w_mem_kv=m_w_mem_kv, mem_q_norm_g=m_mem_q_norm_g,
                mem_k_norm_g=m_mem_k_norm_g, w_br_gdn=m_w_br_gdn, w_br_sb=m_w_br_sb, w_br_mem=m_w_br_mem, w_o=m_w_o,
                norm2_g=m_norm2_g, w_up=m_w_up, w_down=m_w_down)
    mom2 = dict(norm1_g=v_norm1_g, w_in=v_w_in, conv_w=v_conv_w, a_log=v_a_log, dt_bias=v_dt_bias,
                gdn_norm_g=v_gdn_norm_g, sb_q_norm_g=v_sb_q_norm_g, sb_k_norm_g=v_sb_k_norm_g,
                mem_norm_g=v_mem_norm_g, w_mem_kv=v_w_mem_kv, mem_q_norm_g=v_mem_q_norm_g,
                mem_k_norm_g=v_mem_k_norm_g, w_br_gdn=v_w_br_gdn, w_br_sb=v_w_br_sb, w_br_mem=v_w_br_mem, w_o=v_w_o,
                norm2_g=v_norm2_g, w_up=v_w_up, w_down=v_w_down)
    shapes = {n: given[n].shape for n in WEIGHTS}

    w_loc = _pack_rows([given[n][0] for n in BIG], R_BIG)
    gathered = _gather("gather_weights", w_loc.astype(BF16))
    w = _unpack_gathered(gathered[:, :sum(BIG_ROWS)])
    w["w_in"] = _pad_w_in(w["w_in"])
    conv_loc = jnp.pad(given["conv_w"][0].reshape(-1, LANES), ((0, 2), (0, 0)))
    conv_all = _gather("gather_conv", conv_loc)
    w["conv_w"] = conv_all[:, :6].reshape(NDEV, 4, 3 * HW // NDEV).transpose(1, 0, 2).reshape(4, 3 * HW)
    sm = {n: given[n] for n in SMALL}

    loss, dx, g = _local_step(x[0], mem[0], loss_target[0], w, sm)
    g["w_in"] = _unpad_w_in(g["w_in"])

    g_mine = _pack_full_grads(g).astype(BF16)
    g_pair = _pair_sum(g_mine, _sibling_exchange("scatter_sibling", g_mine))
    g_all = _chip_exchange("scatter_chips", g_pair)
    gb, db, mb, vb = _adamw("adamw_sharded", g_all, w_loc, _pack_rows([mom1[n][0] for n in BIG], R_BIG),
                            _pack_rows([mom2[n][0] for n in BIG], R_BIG))
    gs_all = _gather("gather_small_grads", _pack_small(g))
    gs, dsm, ms, vs = _adamw("adamw_replicated", gs_all, _pack_small(given), _pack_small(mom1), _pack_small(mom2))

    outs = {}
    for prefix, big, small in (("grad_", gb, gs), ("delta_", db, dsm), ("new_m_", mb, ms), ("new_v_", vb, vs)):
        vals = _unpack_shard(big, shapes)
        vals.update(_unpack_small(small, shapes))
        for n in WEIGHTS:
            outs[prefix + n] = vals[n]
    loss = lax.psum(loss, ("x", "y", "c"))
    return (loss, dx[None], *[outs[p + n] for p in ("grad_", "delta_", "new_m_", "new_v_") for n in WEIGHTS])
```

```python
import jax
import jax.numpy as jnp
from jax import lax
from jax.experimental import pallas as pl
from jax.experimental.pallas import tpu as pltpu

F32 = jnp.float32
BF16 = jnp.bfloat16

D = 1024
NH = 4
DH = 128
HW = NH * DH
DFF = 4 * D
NMEM = 256
EPS = 1e-6
NDEV = 8
LANES = 128
PAIR = 128
CHUNK = 64
D_IN = 7176
D_INP = 7680
VMEM_LIMIT = 56 * 1024 * 1024

ADAM_LR, ADAM_B1, ADAM_B2, ADAM_EPS, ADAM_WD, ADAM_STEP = 0.001, 0.9, 0.999, 1e-08, 0.01, 10

CB_Z, CB_SQ, CB_SK, CB_SV, CB_MQ, CB_AB = 3, 4, 5, 6, 7, 14

NN = (((1,), (0,)), ((), ()))
NT = (((1,), (1,)), ((), ()))
TN = (((0,), (0,)), ((), ()))

BIG = ("w_in", "w_mem_kv", "w_br_gdn", "w_br_sb", "w_br_mem", "w_o", "w_up", "w_down", "conv_w")
BIG_ROWS = (7176, 1024, 512, 512, 512, 1024, 4096, 4096, 6)
R_BIG = 19456
SMALL = ("norm1_g", "a_log", "dt_bias", "gdn_norm_g", "sb_q_norm_g", "sb_k_norm_g", "mem_norm_g",
         "mem_q_norm_g", "mem_k_norm_g", "norm2_g")
SMALL_ROWS = (8, 1, 1, 1, 1, 1, 8, 1, 1, 8)
R_SMALL = 32
WEIGHTS = ("norm1_g", "w_in", "conv_w", "a_log", "dt_bias", "gdn_norm_g", "sb_q_norm_g", "sb_k_norm_g",
           "mem_norm_g", "w_mem_kv", "mem_q_norm_g", "mem_k_norm_g", "w_br_gdn", "w_br_sb", "w_br_mem",
           "w_o", "norm2_g", "w_up", "w_down")


def _cp(sem=None):
    return pltpu.CompilerParams(dimension_semantics=sem, vmem_limit_bytes=VMEM_LIMIT)


def _dot(a, b, dims=NN):
    return lax.dot_general(a, b, dims, preferred_element_type=F32)


def _dbf(a, b, dims=NN):
    return _dot(a.astype(BF16), b.astype(BF16), dims)


def _split(a, n):
    parts = []
    for _ in range(n):
        h = a.astype(BF16)
        parts.append(h)
        a = a - h.astype(F32)
    return parts


def _d3(a, b, dims=NN):
    ah, al = _split(a, 2)
    bh, bl = _split(b, 2)
    return _dot(ah, bh, dims) + (_dot(ah, bl, dims) + _dot(al, bh, dims))


def _dxr(a, e, dims=NN):
    eb = e.astype(BF16)
    a1, a2, a3 = _split(a, 3)
    return _dot(a1, eb, dims) + (_dot(a2, eb, dims) + _dot(a3, eb, dims))


def _dxl(e, a, dims=NN):
    eb = e.astype(BF16)
    a1, a2, a3 = _split(a, 3)
    return _dot(eb, a1, dims) + (_dot(eb, a2, dims) + _dot(eb, a3, dims))


def _sigmoid(x):
    return 1.0 / (1.0 + jnp.exp(-x))


def _softplus(x):
    return jnp.maximum(x, 0.0) + jnp.log(1.0 + jnp.exp(-jnp.abs(x)))


def _rms(x, g):
    r = lax.rsqrt(jnp.mean(x * x, axis=-1, keepdims=True) + EPS)
    return x * r * g, r


def _rms_bwd(dy, x, g, r):
    dyg = dy * g
    dx = r * (dyg - x * (r * r) * jnp.mean(dyg * x, axis=-1, keepdims=True))
    dg = jnp.sum(dy * (x * r), axis=0, keepdims=True)
    return dx, dg


def _hs(h):
    return slice(h * DH, (h + 1) * DH)


def _row_tile(s):
    return 512 if s >= 2048 else 256


def _narrow_tile(s):
    return min(256, s)


def _mm(name, a, b, mode, tm, tn, tk, pro=None, pro_g=None, epi=None, epi_x=None, out_dtype=F32, n_outer=False):
    if mode == "tn":
        K, M = a.shape
    else:
        M, K = a.shape
    N = b.shape[0] if mode == "nt" else b.shape[1]
    tm, tn, tk = min(tm, M), min(tn, N), min(tk, K)
    nk = K // tk
    assert M % tm == 0 and N % tn == 0 and K % tk == 0, (name, M, N, K, tm, tn, tk)
    dims = {"nn": NN, "nt": NT, "tn": TN}[mode]

    def body(*refs):
        a_ref, b_ref = refs[0], refs[1]
        pos = 2
        g_ref = e_ref = None
        if pro == "rms":
            g_ref = refs[pos]
            pos += 1
        if epi is not None:
            e_ref = refs[pos]
            pos += 1
        o_ref = refs[pos]
        av = a_ref[...]
        if pro == "rms":
            av, _ = _rms(av.astype(F32), g_ref[...])
        elif pro == "relu2":
            av = jnp.square(jnp.maximum(av, 0.0))
        part = _dbf(av, b_ref[...], dims)

        def finish(acc):
            if epi == "add":
                acc = acc + e_ref[...]
            elif epi == "drelu2":
                acc = acc * (2.0 * jnp.maximum(e_ref[...], 0.0))
            o_ref[...] = acc.astype(out_dtype)

        if nk == 1:
            finish(part)
        else:
            acc_ref = refs[pos + 1]
            k = pl.program_id(2)

            @pl.when(k == 0)
            def _():
                acc_ref[...] = part

            @pl.when(k > 0)
            def _():
                acc_ref[...] += part

            @pl.when(k == nk - 1)
            def _():
                finish(acc_ref[...])

    def spec(shape, index):
        if n_outer:
            return pl.BlockSpec(shape, lambda j, i, k: index(i, j, k))
        return pl.BlockSpec(shape, index)

    if mode == "tn":
        a_spec = spec((tk, tm), lambda i, j, k: (k, i))
    else:
        a_spec = spec((tm, tk), lambda i, j, k: (i, k))
    if mode == "nt":
        b_spec = spec((tn, tk), lambda i, j, k: (j, k))
    else:
        b_spec = spec((tk, tn), lambda i, j, k: (k, j))
    in_specs, ops = [a_spec, b_spec], [a, b]
    if pro == "rms":
        w = pro_g.shape[1]
        assert (tm if mode == "tn" else tk) == w, name
        in_specs.append(spec((1, w), lambda i, j, k: (0, 0)))
        ops.append(pro_g)
    if epi is not None:
        in_specs.append(spec((tm, tn), lambda i, j, k: (i, j)))
        ops.append(epi_x)
    grid = (N // tn, M // tm, nk) if n_outer else (M // tm, N // tn, nk)
    return pl.pallas_call(
        body, name=name, grid=grid,
        in_specs=in_specs, out_specs=spec((tm, tn), lambda i, j, k: (i, j)),
        out_shape=jax.ShapeDtypeStruct((M, N), out_dtype),
        scratch_shapes=[pltpu.VMEM((tm, tn), F32)] if nk > 1 else [],
        compiler_params=_cp(("parallel", "parallel", "arbitrary")),
    )(*ops)


def _head_select(first_lane):
    l = lax.broadcasted_iota(jnp.int32, (LANES, HW), 0)
    c = lax.broadcasted_iota(jnp.int32, (LANES, HW), 1)
    return (l == first_lane + c // DH).astype(F32)


def _conv_taps(buf, cw, ts):
    c = cw[3:4, :] * buf[8:8 + ts, :]
    for j in range(3):
        k = 3 - j
        c = c + cw[j:j + 1, :] * buf[8 - k:8 - k + ts, :]
    return c


def _pre_fwd(proj, conv_w, alog_f, dtb_f, gsq, gsk, gmq, S):
    ts = _narrow_tile(S)
    hb = ts // 8

    def body(qkv_ref, halo_ref, ab_ref, sq_ref, sk_ref, sv_ref, mq_ref, cw_ref, al_ref, dt_ref, gsq_ref, gsk_ref,
             gmq_ref, gq_o, gk_o, gv_o, gf_o, bf_o, sqn_o, skn_o, svb_o, qmn_o, buf):
        i = pl.program_id(0)
        buf[0:8, :] = jnp.where(i == 0, 0.0, halo_ref[...])
        buf[8:8 + ts, :] = qkv_ref[...]
        c = _conv_taps(buf, cw_ref[...], ts)
        a = c * _sigmoid(c)
        for h in range(NH):
            q = a[:, h * DH:(h + 1) * DH]
            k = a[:, HW + h * DH:HW + (h + 1) * DH]
            gq_o[:, _hs(h)] = q * (lax.rsqrt(jnp.sum(q * q, axis=-1, keepdims=True) + EPS) * DH ** -0.5)
            gk_o[:, _hs(h)] = k * lax.rsqrt(jnp.sum(k * k, axis=-1, keepdims=True) + EPS)
            sqn_o[:, _hs(h)] = _rms(sq_ref[:, _hs(h)], gsq_ref[...])[0].astype(BF16)
            skn_o[:, _hs(h)] = _rms(sk_ref[:, _hs(h)], gsk_ref[...])[0].astype(BF16)
            qmn_o[:, _hs(h)] = _rms(mq_ref[:, _hs(h)], gmq_ref[...])[0].astype(BF16)
        gv_o[...] = a[:, 2 * HW:3 * HW]
        svb_o[...] = sv_ref[...].astype(BF16)
        ab = ab_ref[:, 0:LANES]
        a_bc = _dxr(ab, _head_select(0))
        b_bc = _dxr(ab, _head_select(NH))
        gf_o[...] = -jnp.exp(al_ref[...]) * _softplus(a_bc + dt_ref[...])
        bf_o[...] = _sigmoid(b_bc)

    row = lambda cb: pl.BlockSpec((ts, HW), lambda i: (i, cb))
    full = lambda r, c: pl.BlockSpec((r, c), lambda i: (0, 0))
    f32o = jax.ShapeDtypeStruct((S, HW), F32)
    bfo = jax.ShapeDtypeStruct((S, HW), BF16)
    return pl.pallas_call(
        body, name="pre_fwd", grid=(S // ts,),
        in_specs=[pl.BlockSpec((ts, 3 * HW), lambda i: (i, 0)),
                  pl.BlockSpec((8, 3 * HW), lambda i: (jnp.maximum(i * hb - 1, 0), 0)),
                  row(CB_AB), row(CB_SQ), row(CB_SK), row(CB_SV), row(CB_MQ),
                  full(4, 3 * HW), full(1, HW), full(1, HW), full(1, DH), full(1, DH), full(1, DH)],
        out_specs=[pl.BlockSpec((ts, HW), lambda i: (i, 0))] * 9,
        out_shape=[f32o, f32o, f32o, f32o, f32o, bfo, bfo, bfo, bfo],
        scratch_shapes=[pltpu.VMEM((ts + 8, 3 * HW), F32)],
        compiler_params=_cp(("parallel",)),
    )(proj, proj, proj, proj, proj, proj, proj, conv_w, alog_f, dtb_f, gsq, gsk, gmq)


def _pre_bwd(proj, conv_w, alog_f, dtb_f, gsq, gsk, dgq, dgk, dgv, dgf, dbf, dsqn, dskn, S):
    ts = _narrow_tile(S)
    hb = ts // 8

    def body(qkv_ref, halo_ref, ab_ref, sq_ref, sk_ref, cw_ref, al_ref, dt_ref, gsq_ref, gsk_ref,
             dgq_ref, dgk_ref, dgv_ref, dgf_ref, dbf_ref, dsqn_ref, dskn_ref,
             dc_o, dab_o, dsq_o, dsk_o, dcw_o, dal_o, ddt_o, dgsq_o, dgsk_o, buf):
        i = pl.program_id(0)

        @pl.when(i == 0)
        def _():
            dcw_o[...] = jnp.zeros_like(dcw_o)
            dal_o[...] = jnp.zeros_like(dal_o)
            ddt_o[...] = jnp.zeros_like(ddt_o)
            dgsq_o[...] = jnp.zeros_like(dgsq_o)
            dgsk_o[...] = jnp.zeros_like(dgsk_o)

        buf[0:8, :] = jnp.where(i == 0, 0.0, halo_ref[...])
        buf[8:8 + ts, :] = qkv_ref[...]
        c = _conv_taps(buf, cw_ref[...], ts)
        sg = _sigmoid(c)
        a = c * sg
        dsilu = sg * (1.0 + c * (1.0 - sg))
        dgsq = jnp.zeros((1, DH), F32)
        dgsk = jnp.zeros((1, DH), F32)
        for h in range(NH):
            q = a[:, h * DH:(h + 1) * DH]
            k = a[:, HW + h * DH:HW + (h + 1) * DH]
            nq = lax.rsqrt(jnp.sum(q * q, axis=-1, keepdims=True) + EPS)
            nk = lax.rsqrt(jnp.sum(k * k, axis=-1, keepdims=True) + EPS)
            dyq = dgq_ref[:, _hs(h)]
            dyk = dgk_ref[:, _hs(h)]
            dq = (nq * dyq - q * (nq * nq * nq) * jnp.sum(dyq * q, axis=-1, keepdims=True)) * DH ** -0.5
            dk = nk * dyk - k * (nk * nk * nk) * jnp.sum(dyk * k, axis=-1, keepdims=True)
            dc_o[:, h * DH:(h + 1) * DH] = dq * dsilu[:, h * DH:(h + 1) * DH]
            dc_o[:, HW + h * DH:HW + (h + 1) * DH] = dk * dsilu[:, HW + h * DH:HW + (h + 1) * DH]
            x = sq_ref[:, _hs(h)]
            _, r = _rms(x, gsq_ref[...])
            dx, dg = _rms_bwd(dsqn_ref[:, _hs(h)], x, gsq_ref[...], r)
            dsq_o[:, _hs(h)] = dx.astype(BF16)
            dgsq = dgsq + dg
            x = sk_ref[:, _hs(h)]
            _, r = _rms(x, gsk_ref[...])
            dx, dg = _rms_bwd(dskn_ref[:, _hs(h)], x, gsk_ref[...], r)
            dsk_o[:, _hs(h)] = dx.astype(BF16)
            dgsk = dgsk + dg
        dc_o[:, 2 * HW:3 * HW] = dgv_ref[...] * dsilu[:, 2 * HW:3 * HW]
        dgsq_o[...] += dgsq
        dgsk_o[...] += dgsk
        dc = dc_o[...]
        for j in range(4):
            k = 3 - j
            dcw_o[j:j + 1, :] += jnp.sum(dc * buf[8 - k:8 - k + ts, :], axis=0, keepdims=True)
        ab = ab_ref[:, 0:LANES]
        a_bc = _dxr(ab, _head_select(0))
        b_bc = _dxr(ab, _head_select(NH))
        pre = a_bc + dt_ref[...]
        ea = jnp.exp(al_ref[...])
        dgf = dgf_ref[...]
        dal_o[...] += jnp.sum(dgf * (-ea * _softplus(pre)), axis=0, keepdims=True)
        da = dgf * (-ea * _sigmoid(pre))
        ddt_o[...] += jnp.sum(da, axis=0, keepdims=True)
        beta = _sigmoid(b_bc)
        db = dbf_ref[...] * beta * (1.0 - beta)
        lane = lax.broadcasted_iota(jnp.int32, (ts, LANES), 1)
        dab = jnp.zeros((ts, LANES), F32)
        for h in range(NH):
            dab = dab + jnp.where(lane == h, da[:, _hs(h)], 0.0) + jnp.where(lane == NH + h, db[:, _hs(h)], 0.0)
        dab_o[:, 0:LANES] = dab.astype(BF16)
        dab_o[:, LANES:HW] = jnp.zeros((ts, HW - LANES), BF16)

    row = lambda cb: pl.BlockSpec((ts, HW), lambda i: (i, cb))
    full = lambda r, c: pl.BlockSpec((r, c), lambda i: (0, 0))
    t512 = pl.BlockSpec((ts, HW), lambda i: (i, 0))
    return pl.pallas_call(
        body, name="pre_bwd", grid=(S // ts,),
        in_specs=[pl.BlockSpec((ts, 3 * HW), lambda i: (i, 0)),
                  pl.BlockSpec((8, 3 * HW), lambda i: (jnp.maximum(i * hb - 1, 0), 0)),
                  row(CB_AB), row(CB_SQ), row(CB_SK),
                  full(4, 3 * HW), full(1, HW), full(1, HW), full(1, DH), full(1, DH)] + [t512] * 7,
        out_specs=[pl.BlockSpec((ts, 3 * HW), lambda i: (i, 0)), t512, t512, t512,
                   full(4, 3 * HW), full(1, HW), full(1, HW), full(1, DH), full(1, DH)],
        out_shape=[jax.ShapeDtypeStruct((S, 3 * HW), F32)] + [jax.ShapeDtypeStruct((S, HW), BF16)] * 3
        + [jax.ShapeDtypeStruct((4, 3 * HW), F32), jax.ShapeDtypeStruct((1, HW), F32),
           jax.ShapeDtypeStruct((1, HW), F32), jax.ShapeDtypeStruct((1, DH), F32),
           jax.ShapeDtypeStruct((1, DH), F32)],
        scratch_shapes=[pltpu.VMEM((ts + 8, 3 * HW), F32)],
        compiler_params=_cp(("arbitrary",)),
    )(proj, proj, proj, proj, proj, conv_w, alog_f, dtb_f, gsq, gsk, dgq, dgk, dgv, dgf, dbf, dsqn, dskn)


def _conv_bwd(dc, conv_w, S):
    ts = _row_tile(S)
    hb = ts // 8
    n = S // ts

    def body(dc_ref, halo_ref, cw_ref, o_ref, buf):
        i = pl.program_id(0)
        buf[0:ts, :] = dc_ref[...]
        buf[ts:ts + 8, :] = jnp.where(i == n - 1, 0.0, halo_ref[...])
        cw = cw_ref[...]
        acc = cw[3:4, :] * buf[0:ts, :]
        for k in range(1, 4):
            acc = acc + cw[3 - k:4 - k, :] * buf[k:k + ts, :]
        o_ref[...] = acc.astype(BF16)

    return pl.pallas_call(
        body, name="conv_bwd", grid=(n,),
        in_specs=[pl.BlockSpec((ts, 3 * HW), lambda i: (i, 0)),
                  pl.BlockSpec((8, 3 * HW), lambda i: (jnp.minimum((i + 1) * hb, S // 8 - 1), 0)),
                  pl.BlockSpec((4, 3 * HW), lambda i: (0, 0))],
        out_specs=pl.BlockSpec((ts, 3 * HW), lambda i: (i, 0)),
        out_shape=jax.ShapeDtypeStruct((S, 3 * HW), BF16),
        scratch_shapes=[pltpu.VMEM((ts + 8, 3 * HW), F32)],
        compiler_params=_cp(("parallel",)),
    )(dc, dc, conv_w)


def _gdn_masks():
    r = lax.broadcasted_iota(jnp.int32, (PAIR, PAIR), 0)
    c = lax.broadcasted_iota(jnp.int32, (PAIR, PAIR), 1)
    same = ((r >= CHUNK) & (c >= CHUNK)) | ((r < CHUNK) & (c < CHUNK))
    return dict(r=r, same=same, tril=same & (r >= c), strict=same & (r > c), triu=same & (c >= r), eye=r == c,
                in_a=r < CHUNK, last_a=r == CHUNK - 1, last_b=r == PAIR - 1)


def _each(fn, *cols):
    return [fn(*xs) for xs in zip(*cols)]


def _mul(a, b):
    return a * b


def _top(x):
    return x[:CHUNK]


def _bot(x):
    return x[CHUNK:]


def _rows(a, b):
    return jnp.concatenate([a, b], axis=0)


def _tri_inv(lm, eye):
    eye_f = eye.astype(F32)
    p = _each(lambda l: eye_f - l, lm)
    lp = _each(lambda l: _d3(l, l), lm)
    for it in range(5):
        p = _each(lambda a, b: a + _d3(a, b), p, lp)
        if it < 4:
            lp = _each(lambda b: _d3(b, b), lp)
    return p


def _gdn_block(m, q, k, v, g, beta):
    tril_f = m["tril"].astype(F32)
    col_sum = lambda mask: (lambda x: jnp.sum(jnp.where(mask, x, 0.0), axis=0, keepdims=True))
    gc = _each(lambda x: _dxl(tril_f, x), g)
    gcr = _each(col_sum(m["eye"]), gc)
    gam = _each(lambda a, b: jnp.where(m["tril"], jnp.exp(jnp.minimum(a - b, 0.0)), 0.0), gc, gcr)
    kb = _each(_mul, k, beta)
    vb = _each(_mul, v, beta)
    lm = _each(lambda a, b, c: jnp.where(m["strict"], _d3(a, b, NT) * c, 0.0), kb, k, gam)
    t = _tri_inv(lm, m["eye"])
    eg = _each(jnp.exp, gc)
    kbe = _each(_mul, kb, eg)
    u = _each(_d3, t, vb)
    w = _each(_d3, t, kbe)
    aqk = _each(lambda a, b, c: jnp.where(m["tril"], _d3(a, b, NT) * c, 0.0), q, k, gam)
    qd = _each(_mul, q, eg)
    ga = _each(col_sum(m["last_a"]), gc)
    gb = _each(col_sum(m["last_b"]), gc)
    e2 = _each(lambda a, b, c: jnp.exp(jnp.where(m["in_a"], a, b) - c), ga, gb, gc)
    kd = _each(_mul, k, e2)
    return dict(u=u, w=w, aqk=aqk, qd=qd, kd=kd, gam=gam, kb=kb, vb=vb, lm=lm, t=t, eg=eg, kbe=kbe, e2=e2,
                gla=_each(jnp.exp, ga), glb=_each(jnp.exp, gb))


def _gdn_fwd(gq, gk, gv, gf, bf, S):
    nb = S // PAIR

    def body(q_ref, k_ref, v_ref, g_ref, b_ref, o_ref, st_ref, s_scr):
        @pl.when(pl.program_id(0) == 0)
        def _():
            s_scr[...] = jnp.zeros_like(s_scr)

        m = _gdn_masks()
        heads = lambda ref: [ref[:, _hs(h)] for h in range(NH)]
        f = _gdn_block(m, heads(q_ref), heads(k_ref), heads(v_ref), heads(g_ref), heads(b_ref))
        u, w, qd, kd = f["u"], f["w"], f["qd"], f["kd"]
        s0 = [s_scr[h * DH:(h + 1) * DH, :] for h in range(NH)]
        vna = _each(lambda a, b, s: _top(a) - _d3(_top(b), s), u, w, s0)
        oa = _each(lambda a, s: _d3(_top(a), s), qd, s0)
        s1 = _each(lambda s, gl, a, vn: s * gl + _d3(_top(a), vn, TN), s0, f["gla"], kd, vna)
        vnb = _each(lambda a, b, s: _bot(a) - _d3(_bot(b), s), u, w, s1)
        ob = _each(lambda a, s: _d3(_bot(a), s), qd, s1)
        s2 = _each(lambda s, gl, a, vn: s * gl + _d3(_bot(a), vn, TN), s1, f["glb"], kd, vnb)
        outs = _each(lambda a, b, c, va, vb: _rows(a, b) + _d3(c, _rows(va, vb)), oa, ob, f["aqk"], vna, vnb)
        o_ref[...] = jnp.concatenate(outs, axis=1)
        st_ref[...] = jnp.concatenate(s0 + s1, axis=0)
        s_scr[...] = jnp.concatenate(s2, axis=0)

    blk = pl.BlockSpec((PAIR, HW), lambda i: (i, 0))
    return pl.pallas_call(
        body, name="gdn_fwd", grid=(nb,),
        in_specs=[blk] * 5,
        out_specs=[blk, pl.BlockSpec((2 * NH * DH, DH), lambda i: (i, 0))],
        out_shape=[jax.ShapeDtypeStruct((S, HW), F32), jax.ShapeDtypeStruct((nb * 2 * NH * DH, DH), F32)],
        scratch_shapes=[pltpu.VMEM((NH * DH, DH), F32)],
        compiler_params=_cp(("arbitrary",)),
    )(gq, gk, gv, gf, bf)


def _gdn_bwd(gq, gk, gv, gf, bf, states, do, S):
    nb = S // PAIR

    def body(q_ref, k_ref, v_ref, g_ref, b_ref, st_ref, do_ref, dq_o, dk_o, dv_o, dg_o, db_o, ds_scr):
        @pl.when(pl.program_id(0) == 0)
        def _():
            ds_scr[...] = jnp.zeros_like(ds_scr)

        m = _gdn_masks()
        ones = jnp.ones((PAIR, PAIR), F32)
        heads = lambda ref: [ref[:, _hs(h)] for h in range(NH)]
        q, k, v, beta, do = heads(q_ref), heads(k_ref), heads(v_ref), heads(b_ref), heads(do_ref)
        f = _gdn_block(m, q, k, v, heads(g_ref), beta)
        u, w, aqk, qd, kd, t = f["u"], f["w"], f["aqk"], f["qd"], f["kd"], f["t"]
        s0 = [st_ref[h * DH:(h + 1) * DH, :] for h in range(NH)]
        s1 = [st_ref[(NH + h) * DH:(NH + h + 1) * DH, :] for h in range(NH)]
        ds2 = [ds_scr[h * DH:(h + 1) * DH, :] for h in range(NH)]
        total = lambda a, b: jnp.sum(jnp.sum(a * b, axis=1, keepdims=True), axis=0, keepdims=True)
        vna = _each(lambda a, b, s: _top(a) - _d3(_top(b), s), u, w, s0)
        vnb = _each(lambda a, b, s: _bot(a) - _d3(_bot(b), s), u, w, s1)
        dvn_i = _each(lambda a, b: _d3(a, b, TN), aqk, do)
        dvnb = _each(lambda a, b, s: _bot(a) + _d3(_bot(b), s), dvn_i, kd, ds2)
        dqdb = _each(lambda a, s: _d3(_bot(a), s, NT), do, s1)
        dkdb = _each(lambda a, s: _d3(a, s, NT), vnb, ds2)
        dglb = _each(total, ds2, s1)
        dwb = _each(lambda a, s: -_d3(a, s, NT), dvnb, s1)
        ds1 = _each(lambda s, gl, a, b, c, d: s * gl + _d3(_bot(a), _bot(b), TN) - _d3(_bot(c), d, TN),
                    ds2, f["glb"], qd, do, w, dvnb)
        dvna = _each(lambda a, b, s: _top(a) + _d3(_top(b), s), dvn_i, kd, ds1)
        dqda = _each(lambda a, s: _d3(_top(a), s, NT), do, s0)
        dkda = _each(lambda a, s: _d3(a, s, NT), vna, ds1)
        dgla = _each(total, ds1, s0)
        dwa = _each(lambda a, s: -_d3(a, s, NT), dvna, s0)
        ds0 = _each(lambda s, gl, a, b, c, d: s * gl + _d3(_top(a), _top(b), TN) - _d3(_top(c), d, TN),
                    ds1, f["gla"], qd, do, w, dvna)
        dvn, dqd, dkd, dw = (_each(_rows, a, b) for a, b in ((dvna, dvnb), (dqda, dqdb), (dkda, dkdb), (dwa, dwb)))
        daqk = _each(lambda a, va, vb: jnp.where(m["tril"], _d3(a, _rows(va, vb), NT), 0.0), do, vna, vnb)
        dt = _each(lambda a, b, c, d: _d3(a, b, NT) + _d3(c, d, NT), dvn, f["vb"], dw, f["kbe"])
        dvb = _each(lambda a, b: _d3(a, b, TN), t, dvn)
        dkbe = _each(lambda a, b: _d3(a, b, TN), t, dw)
        dtt = _each(lambda a, b: _d3(a, b, NT), dt, t)
        dl = _each(lambda a, b: -jnp.where(m["strict"], _d3(a, b, TN), 0.0), t, dtt)
        dm = _each(_mul, dl, f["gam"])
        dn = _each(_mul, daqk, f["gam"])
        dkb = _each(lambda a, b, c, d: _d3(a, b) + c * d, dm, k, dkbe, f["eg"])
        dks = _each(lambda a, b, c, d, e, g, h, i: _d3(a, b, TN) + _d3(c, d, TN) + e * g + h * i,
                    dm, f["kb"], dn, q, dkd, f["e2"], beta, dkb)
        dqs = _each(lambda a, b, c, d: _d3(a, b) + c * d, dn, k, dqd, f["eg"])
        gm = _each(lambda a, b, c, d: a * b + c * d, dl, f["lm"], daqk, aqk)
        dkdkd = _each(_mul, dkd, kd)
        dgc = _each(lambda a, b, c, d, e, g: _dxr(a + b * c + d * e - g, ones) - _dxr(a, ones, TN),
                    gm, dqd, qd, dkbe, f["kbe"], dkdkd)
        same_f = m["same"].astype(F32)
        chunk_tot = _each(lambda a: _dxl(same_f, _dxr(a, ones)), dkdkd)
        last = m["last_a"] | m["last_b"]
        dgc = _each(lambda a, b, ga, gla, gb, glb: a + jnp.where(last, b + jnp.where(m["in_a"], ga * gla, gb * glb), 0.0),
                    dgc, chunk_tot, dgla, f["gla"], dglb, f["glb"])
        dbs = _each(lambda a, b, c, d: _dxr(a * b + c * d, ones), dkb, k, dvb, v)
        dvs = _each(_mul, beta, dvb)
        triu_f = m["triu"].astype(F32)
        dgs = _each(lambda a: _dxl(triu_f, a), dgc)
        for ref, parts in ((dq_o, dqs), (dk_o, dks), (dv_o, dvs), (dg_o, dgs), (db_o, dbs)):
            ref[...] = jnp.concatenate(parts, axis=1)
        ds_scr[...] = jnp.concatenate(ds0, axis=0)

    blk = pl.BlockSpec((PAIR, HW), lambda i: (nb - 1 - i, 0))
    o = jax.ShapeDtypeStruct((S, HW), F32)
    return pl.pallas_call(
        body, name="gdn_bwd", grid=(nb,),
        in_specs=[blk] * 5 + [pl.BlockSpec((2 * NH * DH, DH), lambda i: (nb - 1 - i, 0)), blk],
        out_specs=[blk] * 5, out_shape=[o] * 5,
        scratch_shapes=[pltpu.VMEM((NH * DH, DH), F32)],
        compiler_params=_cp(("arbitrary",)),
    )(gq, gk, gv, gf, bf, states, do)


SB_T = 256
SB_GROUP = 4
SB_GROUP_BWD = 4


def _group_sizes(g):
    sizes = []
    while g >= 1:
        sizes.append(g)
        g //= 2
    return sizes


def _sb_iotas(t):
    return lax.broadcasted_iota(jnp.int32, (t, t), 0), lax.broadcasted_iota(jnp.int32, (t, t), 1)


def _sb_scores(q, k, mask):
    z = _dot(q, k, NT) * DH ** -0.5
    ls = jnp.minimum(z, 0.0) - jnp.log(1.0 + jnp.exp(-jnp.abs(z)))
    lneg = ls - z
    if mask is not None:
        lneg = jnp.where(mask, lneg, 0.0)
    return ls, lneg


def _prefix(x, u):
    xh, xl = _split(x, 2)
    return _dot(xh, u) + _dot(xl, u)


def _sb_fwd(sqn, skn, svb, S):
    t = min(SB_T, S)

    def body(q_ref, k_ref, v_ref, o_ref, t_ref):
        qb = pl.program_id(1)
        q = q_ref[...]
        r, c = _sb_iotas(t)
        diag = c < r
        u_after = (r > c).astype(BF16)

        def tiles(k0s, run, mask):
            sc = _each(lambda k0: _sb_scores(q, k_ref[pl.ds(k0, t), :], mask), k0s)
            ls, lneg = [s[0] for s in sc], [s[1] for s in sc]
            sums = _each(lambda x: jnp.sum(x, axis=1, keepdims=True), lneg)
            pre = _each(lambda x: _prefix(x, u_after), lneg)
            runs = [run]
            for s in sums:
                runs.append(runs[-1] + s)
            att = _each(lambda a, b, rn: jnp.exp(a + (rn + b)), ls, pre, runs[:-1])
            if mask is not None:
                att = _each(lambda a: jnp.where(mask, a, 0.0), att)
            parts = _each(lambda a, k0: _dot(a.astype(BF16), v_ref[pl.ds(k0, t), :]), att, k0s)
            return sum(parts[1:], parts[0]), runs[-1]

        acc, run = tiles([pl.multiple_of(qb * t, t)], jnp.zeros((t, 1), F32), diag)

        carry, done = (acc, run), 0
        for size in _group_sizes(SB_GROUP):
            n = (qb - done) // size

            def group(i, carry, size=size, done=done):
                acc, run = carry
                first = qb - 1 - done - size * i
                part, run = tiles([pl.multiple_of((first - j) * t, t) for j in range(size)], run, None)
                return acc + part, run

            carry = lax.fori_loop(0, n, group, carry)
            done = done + n * size
        acc, run = carry
        o_ref[...] = acc.astype(BF16)
        t_ref[...] = jnp.broadcast_to(run, (t, DH))

    qspec = pl.BlockSpec((t, DH), lambda h, i: (i, h))
    kspec = pl.BlockSpec((S, DH), lambda h, i: (0, h))
    return pl.pallas_call(
        body, name="sb_fwd", grid=(NH, S // t),
        in_specs=[qspec, kspec, kspec], out_specs=[qspec, qspec],
        out_shape=[jax.ShapeDtypeStruct((S, HW), BF16), jax.ShapeDtypeStruct((S, HW), F32)],
        compiler_params=_cp(("parallel", "arbitrary")),
    )(sqn, skn, svb)


def _sb_bwd(sqn, skn, svb, do, tot, S):
    t = min(SB_T, S)

    def body(q_ref, k_ref, v_ref, do_ref, t_ref, dq_o, dk_o, dv_o, dv_acc):
        qb = pl.program_id(1)

        @pl.when(qb == 0)
        def _():
            dk_o[...] = jnp.zeros_like(dk_o)
            dv_acc[...] = jnp.zeros_like(dv_acc)

        q = q_ref[...]
        do = do_ref[...].astype(BF16)
        tot_l = jnp.concatenate([t_ref[...]] * (t // DH), axis=1)
        r, c = _sb_iotas(t)
        diag = c < r
        u_upto = (r <= c).astype(BF16)
        u_before = (r < c).astype(BF16)

        def tiles(k0s, run_l, run_e, mask):
            rowsum = lambda x: jnp.sum(x, axis=1, keepdims=True)
            ks = [k_ref[pl.ds(k0, t), :] for k0 in k0s]
            vs = [v_ref[pl.ds(k0, t), :] for k0 in k0s]
            sc = _each(lambda k: _sb_scores(q, k, mask), ks)
            ls, lneg = [s[0] for s in sc], [s[1] for s in sc]
            sums_l = _each(rowsum, lneg)
            pre_l = _each(lambda x: _prefix(x, u_upto), lneg)
            runs_l = [run_l]
            for s in sums_l:
                runs_l.append(runs_l[-1] + s)
            att = _each(lambda a, b, rn: jnp.exp(a + (tot_l - (rn + b))), ls, pre_l, runs_l[:-1])
            if mask is not None:
                att = _each(lambda a: jnp.where(mask, a, 0.0), att)
            e = _each(lambda v, a: _dot(do, v, NT) * a, vs, att)
            sums_e = _each(rowsum, e)
            pre_e = _each(lambda x: _prefix(x, u_before), e)
            runs_e = [run_e]
            for s in sums_e:
                runs_e.append(runs_e[-1] + s)
            sg = _each(jnp.exp, ls)
            dz = _each(lambda a, b, rn, s: a * (1.0 - s) - (rn + b) * s, e, pre_e, runs_e[:-1], sg)
            if mask is not None:
                dz = _each(lambda a: jnp.where(mask, a, 0.0), dz)
            dz = _each(lambda a: (a * DH ** -0.5).astype(BF16), dz)
            dvs = _each(lambda a: _dot(a.astype(BF16), do, TN), att)
            dks = _each(lambda a: _dot(a, q, TN), dz)
            dqs = _each(_dot, dz, ks)
            for k0, dv, dk in zip(k0s, dvs, dks):
                dv_acc[pl.ds(k0, t), :] += dv
                dk_o[pl.ds(k0, t), :] += dk
            return sum(dqs[1:], dqs[0]), runs_l[-1], runs_e[-1]

        z1 = jnp.zeros((t, 1), F32)
        carry, done = (jnp.zeros((t, DH), F32), z1, z1), 0
        for size in _group_sizes(SB_GROUP_BWD):
            n = (qb - done) // size

            def group(i, carry, size=size, done=done):
                dq, run_l, run_e = carry
                first = done + size * i
                part, run_l, run_e = tiles([pl.multiple_of((first + j) * t, t) for j in range(size)], run_l, run_e,
                                           None)
                return dq + part, run_l, run_e

            carry = lax.fori_loop(0, n, group, carry)
            done = done + n * size
        dq, run_l, run_e = carry
        part, _, _ = tiles([pl.multiple_of(qb * t, t)], run_l, run_e, diag)
        dq_o[...] = dq + part

        @pl.when(qb == S // t - 1)
        def _():
            dv_o[...] = dv_acc[...].astype(BF16)

    qspec = pl.BlockSpec((t, DH), lambda h, i: (i, h))
    kspec = pl.BlockSpec((S, DH), lambda h, i: (0, h))
    o = jax.ShapeDtypeStruct((S, HW), F32)
    return pl.pallas_call(
        body, name="sb_bwd", grid=(NH, S // t),
        in_specs=[qspec, kspec, kspec, qspec, qspec], out_specs=[qspec, kspec, kspec],
        out_shape=[o, o, jax.ShapeDtypeStruct((S, HW), BF16)],
        scratch_shapes=[pltpu.VMEM((S, DH), F32)],
        compiler_params=_cp(("parallel", "arbitrary")),
    )(sqn, skn, svb, do, tot)


def _mem_probs(qn, kn):
    s = _dot(qn, kn.astype(BF16), NT) * DH ** -0.5
    p = jnp.exp(s - jnp.max(s, axis=-1, keepdims=True))
    return p / jnp.sum(p, axis=-1, keepdims=True)


def _mem_fwd(qmn, kv, gmk, S):
    ts = _row_tile(S)

    def body(q_ref, kv_ref, gk_ref, o_ref):
        for h in range(NH):
            kn, _ = _rms(kv_ref[:, _hs(h)], gk_ref[...])
            p = _mem_probs(q_ref[:, _hs(h)], kn)
            o_ref[:, _hs(h)] = _dbf(p, kv_ref[:, HW + h * DH:HW + (h + 1) * DH]).astype(BF16)

    return pl.pallas_call(
        body, name="mem_fwd", grid=(S // ts,),
        in_specs=[pl.BlockSpec((ts, HW), lambda i: (i, 0)), pl.BlockSpec((NMEM, 2 * HW), lambda i: (0, 0)),
                  pl.BlockSpec((1, DH), lambda i: (0, 0))],
        out_specs=pl.BlockSpec((ts, HW), lambda i: (i, 0)),
        out_shape=jax.ShapeDtypeStruct((S, HW), BF16),
        compiler_params=_cp(("parallel",)),
    )(qmn, kv, gmk)


def _mem_bwd(proj, qmn, kv, gmq, gmk, do, S):
    ts = _row_tile(S)
    n = S // ts

    def body(mq_ref, q_ref, kv_ref, gq_ref, gk_ref, do_ref, dmq_o, dkv_o, dgq_o, dgk_o, dkn_scr):
        i = pl.program_id(0)

        @pl.when(i == 0)
        def _():
            dkv_o[...] = jnp.zeros_like(dkv_o)
            dgq_o[...] = jnp.zeros_like(dgq_o)
            dkn_scr[...] = jnp.zeros_like(dkn_scr)

        dgq = jnp.zeros((1, DH), F32)
        for h in range(NH):
            km = kv_ref[:, _hs(h)]
            vm = kv_ref[:, HW + h * DH:HW + (h + 1) * DH].astype(BF16)
            kn, _ = _rms(km, gk_ref[...])
            qn = q_ref[:, _hs(h)]
            p = _mem_probs(qn, kn)
            dob = do_ref[:, _hs(h)].astype(BF16)
            dkv_o[:, HW + h * DH:HW + (h + 1) * DH] += _dot(p.astype(BF16), dob, TN)
            dp = _dot(dob, vm, NT)
            dsc = (p * (dp - jnp.sum(dp * p, axis=-1, keepdims=True)) * DH ** -0.5).astype(BF16)
            dkn_scr[:, _hs(h)] += _dot(dsc, qn, TN)
            x = mq_ref[:, _hs(h)]
            _, r = _rms(x, gq_ref[...])
            dx, dg = _rms_bwd(_dot(dsc, kn.astype(BF16)), x, gq_ref[...], r)
            dmq_o[:, _hs(h)] = dx.astype(BF16)
            dgq = dgq + dg
        dgq_o[...] += dgq

        @pl.when(i == n - 1)
        def _():
            dgk = jnp.zeros((1, DH), F32)
            for h in range(NH):
                km = kv_ref[:, _hs(h)]
                _, r = _rms(km, gk_ref[...])
                dx, dg = _rms_bwd(dkn_scr[:, _hs(h)], km, gk_ref[...], r)
                dkv_o[:, _hs(h)] = dx
                dgk = dgk + dg
            dgk_o[...] = dgk

    full = lambda r, c: pl.BlockSpec((r, c), lambda i: (0, 0))
    t512 = pl.BlockSpec((ts, HW), lambda i: (i, 0))
    return pl.pallas_call(
        body, name="mem_bwd", grid=(n,),
        in_specs=[pl.BlockSpec((ts, HW), lambda i: (i, CB_MQ)), t512, full(NMEM, 2 * HW), full(1, DH), full(1, DH),
                  t512],
        out_specs=[t512, full(NMEM, 2 * HW), full(1, DH), full(1, DH)],
        out_shape=[jax.ShapeDtypeStruct((S, HW), BF16), jax.ShapeDtypeStruct((NMEM, 2 * HW), F32),
                   jax.ShapeDtypeStruct((1, DH), F32), jax.ShapeDtypeStruct((1, DH), F32)],
        scratch_shapes=[pltpu.VMEM((NMEM, HW), F32)],
        compiler_params=_cp(("arbitrary",)),
    )(proj, qmn, kv, gmq, gmk, do)


def _gated_gdn(o, z, g):
    sg = _sigmoid(z)
    outs, rs = [], []
    for h in range(NH):
        y, r = _rms(o[:, _hs(h)], g)
        outs.append(y * (z[:, _hs(h)] * sg[:, _hs(h)]))
        rs.append(r)
    return jnp.concatenate(outs, axis=1), rs, sg


def _merge_fwd(x, proj, ogdn, osb, omem, ggdn, wbg, wbs, wbm, wo, S):
    ts = _narrow_tile(S)

    def body(x_ref, z_ref, g0_ref, g1_ref, g2_ref, og_ref, os_ref, om_ref, gg_ref, wbg_ref, wbs_ref, wbm_ref,
             wo_ref, x1_o, mix_o):
        on, _, _ = _gated_gdn(og_ref[...], z_ref[...], gg_ref[...])
        mix = (_sigmoid(g0_ref[...]) * _dbf(on, wbg_ref[...]) + _sigmoid(g1_ref[...]) * _dbf(os_ref[...], wbs_ref[...])
               + _sigmoid(g2_ref[...]) * _dbf(om_ref[...], wbm_ref[...]))
        mix_o[...] = mix.astype(BF16)
        x1_o[...] = x_ref[...] + _dbf(mix, wo_ref[...])

    t512 = pl.BlockSpec((ts, HW), lambda i: (i, 0))
    t1k = pl.BlockSpec((ts, D), lambda i: (i, 0))
    gate = lambda j: pl.BlockSpec((ts, D), lambda i: (i, 4 + j))
    full = lambda r, c: pl.BlockSpec((r, c), lambda i: (0, 0))
    return pl.pallas_call(
        body, name="merge_fwd", grid=(S // ts,),
        in_specs=[t1k, pl.BlockSpec((ts, HW), lambda i: (i, CB_Z)), gate(0), gate(1), gate(2), t512, t512, t512,
                  full(1, DH), full(HW, D), full(HW, D), full(HW, D), full(D, D)],
        out_specs=[t1k, t1k],
        out_shape=[jax.ShapeDtypeStruct((S, D), F32), jax.ShapeDtypeStruct((S, D), BF16)],
        compiler_params=_cp(("parallel",)),
    )(x, proj, proj, proj, proj, ogdn, osb, omem, ggdn, wbg, wbs, wbm, wo)


def _merge_bwd(dmix, proj, ogdn, osb, omem, ggdn, wbg, wbs, wbm, S):
    ts = _narrow_tile(S)

    def body(dm_ref, z_ref, g0_ref, g1_ref, g2_ref, og_ref, os_ref, om_ref, gg_ref, wbg_ref, wbs_ref, wbm_ref,
             dgl0_o, dgl1_o, dgl2_o, dog_o, dz_o, dos_o, dom_o, dwbg_o, dwbs_o, dwbm_o, dgg_o):
        @pl.when(pl.program_id(0) == 0)
        def _():
            for ref in (dwbg_o, dwbs_o, dwbm_o, dgg_o):
                ref[...] = jnp.zeros_like(ref)

        dm = dm_ref[...]
        og = og_ref[...]
        z = z_ref[...]
        on, rs, sg = _gated_gdn(og, z, gg_ref[...])
        branch = ((on, g0_ref, wbg_ref, dgl0_o, dwbg_o), (os_ref[...], g1_ref, wbs_ref, dgl1_o, dwbs_o),
                  (om_ref[...], g2_ref, wbm_ref, dgl2_o, dwbm_o))
        dos = []
        for o, g_ref, w_ref, dgl_o, dw_o in branch:
            ob = o.astype(BF16)
            gate = _sigmoid(g_ref[...])
            dgl_o[...] = (dm * _dot(ob, w_ref[...]) * gate * (1.0 - gate)).astype(BF16)
            dy = (dm * gate).astype(BF16)
            dw_o[...] += _dot(ob, dy, TN)
            dos.append(_dot(dy, w_ref[...], NT))
        dos_o[...] = dos[1].astype(BF16)
        dom_o[...] = dos[2].astype(BF16)
        don = dos[0]
        dgg = jnp.zeros((1, DH), F32)
        for h in range(NH):
            oh, zh, sh = og[:, _hs(h)], z[:, _hs(h)], sg[:, _hs(h)]
            y = oh * rs[h] * gg_ref[...]
            dz_o[:, _hs(h)] = (don[:, _hs(h)] * y * (sh * (1.0 + zh * (1.0 - sh)))).astype(BF16)
            dx, dg = _rms_bwd(don[:, _hs(h)] * (zh * sh), oh, gg_ref[...], rs[h])
            dog_o[:, _hs(h)] = dx
            dgg = dgg + dg
        dgg_o[...] += dgg

    t512 = pl.BlockSpec((ts, HW), lambda i: (i, 0))
    t1k = pl.BlockSpec((ts, D), lambda i: (i, 0))
    gate = lambda j: pl.BlockSpec((ts, D), lambda i: (i, 4 + j))
    full = lambda r, c: pl.BlockSpec((r, c), lambda i: (0, 0))
    s1k = jax.ShapeDtypeStruct((S, D), BF16)
    s512 = jax.ShapeDtypeStruct((S, HW), BF16)
    wsh = jax.ShapeDtypeStruct((HW, D), F32)
    return pl.pallas_call(
        body, name="merge_bwd", grid=(S // ts,),
        in_specs=[t1k, pl.BlockSpec((ts, HW), lambda i: (i, CB_Z)), gate(0), gate(1), gate(2), t512, t512, t512,
                  full(1, DH), full(HW, D), full(HW, D), full(HW, D)],
        out_specs=[t1k, t1k, t1k, t512, t512, t512, t512, full(HW, D), full(HW, D), full(HW, D), full(1, DH)],
        out_shape=[s1k, s1k, s1k, jax.ShapeDtypeStruct((S, HW), F32), s512, s512, s512, wsh, wsh, wsh,
                   jax.ShapeDtypeStruct((1, DH), F32)],
        compiler_params=_cp(("arbitrary",)),
    )(dmix, proj, proj, proj, proj, ogdn, osb, omem, ggdn, wbg, wbs, wbm)


def _loss_grad(y, target, S):
    ts = _row_tile(S)

    def body(y_ref, t_ref, dy_o, loss_o):
        @pl.when(pl.program_id(0) == 0)
        def _():
            loss_o[...] = jnp.zeros_like(loss_o)

        err = y_ref[...] - t_ref[...]
        dy_o[...] = err * (1.0 / D)
        per_tok = jnp.sum(err * err, axis=1, keepdims=True) * (1.0 / D)
        loss_o[...] += 0.5 * jnp.sum(per_tok, axis=0, keepdims=True)

    t1k = pl.BlockSpec((ts, D), lambda i: (i, 0))
    return pl.pallas_call(
        body, name="loss_grad", grid=(S // ts,), in_specs=[t1k, t1k],
        out_specs=[t1k, pl.BlockSpec((1, 1), lambda i: (0, 0))],
        out_shape=[jax.ShapeDtypeStruct((S, D), F32), jax.ShapeDtypeStruct((1, 1), F32)],
        compiler_params=_cp(("arbitrary",)),
    )(y, target)


def _norm_bwd(name, dh, x, g, res):
    rows = x.shape[0]
    ts = min(_row_tile(rows), rows)

    def body(*refs):
        dh_ref, x_ref, g_ref = refs[:3]
        dx_o, dg_o = refs[-2:]

        @pl.when(pl.program_id(0) == 0)
        def _():
            dg_o[...] = jnp.zeros_like(dg_o)

        xv = x_ref[...]
        _, r = _rms(xv, g_ref[...])
        dx, dg = _rms_bwd(dh_ref[...], xv, g_ref[...], r)
        dx_o[...] = dx if res is None else dx + refs[3][...]
        dg_o[...] += dg

    t1k = pl.BlockSpec((ts, D), lambda i: (i, 0))
    gsp = pl.BlockSpec((1, D), lambda i: (0, 0))
    ops = [dh, x, g] + ([] if res is None else [res])
    return pl.pallas_call(
        body, name=name, grid=(rows // ts,), in_specs=[t1k, t1k, gsp] + ([] if res is None else [t1k]),
        out_specs=[t1k, gsp],
        out_shape=[jax.ShapeDtypeStruct((rows, D), F32), jax.ShapeDtypeStruct((1, D), F32)],
        compiler_params=_cp(("arbitrary",)),
    )(*ops)


def _adamw(name, gall, w, m, v):
    rows = w.shape[0]
    nsrc = gall.shape[0]
    tr = min(1216, rows)
    assert rows % tr == 0

    def body(g_ref, w_ref, m_ref, v_ref, g_o, d_o, m_o, v_o):
        g = g_ref[0].astype(F32)
        for j in range(1, nsrc):
            g = g + g_ref[j].astype(F32)
        m_new = ADAM_B1 * m_ref[...] + (1.0 - ADAM_B1) * g
        v_new = ADAM_B2 * v_ref[...] + (1.0 - ADAM_B2) * jnp.square(g)
        m_hat = m_new / (1.0 - ADAM_B1 ** ADAM_STEP)
        v_hat = v_new / (1.0 - ADAM_B2 ** ADAM_STEP)
        g_o[...] = g
        d_o[...] = -ADAM_LR * (m_hat / (jnp.sqrt(v_hat) + ADAM_EPS) + ADAM_WD * w_ref[...])
        m_o[...] = m_new
        v_o[...] = v_new

    t = pl.BlockSpec((tr, LANES), lambda i: (i, 0))
    o = jax.ShapeDtypeStruct((rows, LANES), F32)
    return pl.pallas_call(
        body, name=name, grid=(rows // tr,),
        in_specs=[pl.BlockSpec((nsrc, tr, LANES), lambda i: (0, i, 0)), t, t, t],
        out_specs=[t, t, t, t], out_shape=[o, o, o, o],
        compiler_params=_cp(("parallel",)),
    )(gall, w, m, v)


def _pair_sum(mine, theirs):
    rows = mine.shape[1]
    tr = min(1216, rows)
    assert rows % tr == 0
    core = lax.axis_index("c").astype(jnp.int32).reshape(1)

    def body(c_ref, a_ref, b_ref, o_ref):
        o_ref[...] = (a_ref[...].astype(F32) + b_ref[...].astype(F32)).astype(o_ref.dtype)

    blk = pl.BlockSpec((1, tr, LANES), lambda j, i, c_ref: (j, i, 0))
    return pl.pallas_call(
        body, name="pair_sum",
        grid_spec=pltpu.PrefetchScalarGridSpec(
            num_scalar_prefetch=1, grid=(NDEV // 2, rows // tr),
            in_specs=[pl.BlockSpec((1, tr, LANES), lambda j, i, c_ref: (2 * j + c_ref[0], i, 0)), blk],
            out_specs=blk),
        out_shape=jax.ShapeDtypeStruct((NDEV // 2, rows, LANES), mine.dtype),
        compiler_params=_cp(("parallel", "parallel")),
    )(core, mine, theirs)


HBM_SPEC = pl.BlockSpec(memory_space=pltpu.HBM)


def _remote(src, dst, send_sems, recv_sems, k, to):
    return pltpu.make_async_remote_copy(src_ref=src, dst_ref=dst, send_sem=send_sems.at[k], recv_sem=recv_sems.at[k],
                                        device_id=to, device_id_type=pl.DeviceIdType.MESH)


def _gather(name, x):
    rows, cols = x.shape

    def body(x_ref, o_ref, send_sems, recv_sems, local_sem):
        ix, iy, ic = lax.axis_index("x"), lax.axis_index("y"), lax.axis_index("c")
        me, sibling = (ix, iy, ic), (ix, iy, 1 - ic)
        chips = [(1 - ix, iy), (ix, 1 - iy), (1 - ix, 1 - iy)]

        def slab(px, py, pc):
            return o_ref.at[4 * px + 2 * py + pc]

        def copy(k, block, to, src=None):
            return _remote(slab(*block) if src is None else src, slab(*block), send_sems, recv_sems, k, to)

        mine = pltpu.make_async_copy(x_ref, slab(*me), local_sem)
        mine.start()
        first = [copy(0, me, sibling, src=x_ref)]
        first += [copy(1 + j, me, (*chip, ic), src=x_ref) for j, chip in enumerate(chips)]
        for cp in first:
            cp.start()
        passed = [copy(4 + j, (*chip, ic), sibling) for j, chip in enumerate(chips)]
        for j, chip in enumerate(chips):
            copy(1 + j, (*chip, ic), me).wait_recv()
            passed[j].start()
        copy(0, sibling, me).wait_recv()
        for j, chip in enumerate(chips):
            copy(4 + j, (*chip, 1 - ic), me).wait_recv()
        for cp in first + passed:
            cp.wait_send()
        mine.wait()

    return pl.pallas_call(
        body, name=name, in_specs=[HBM_SPEC], out_specs=HBM_SPEC,
        out_shape=jax.ShapeDtypeStruct((NDEV, rows, cols), x.dtype),
        scratch_shapes=[pltpu.SemaphoreType.DMA((NDEV - 1,)), pltpu.SemaphoreType.DMA((NDEV - 1,)),
                        pltpu.SemaphoreType.DMA],
    )(x)


def _sibling_exchange(name, x):
    rows, cols = x.shape[-2:]
    nchip = NDEV // 2

    def body(x_ref, o_ref, send_sems, recv_sems):
        ix, iy, ic = lax.axis_index("x"), lax.axis_index("y"), lax.axis_index("c")
        copies = [_remote(x_ref.at[2 * j + (1 - ic)], o_ref.at[j], send_sems, recv_sems, j, (ix, iy, 1 - ic))
                  for j in range(nchip)]
        for cp in copies:
            cp.start()
        for cp in copies:
            cp.wait()

    return pl.pallas_call(
        body, name=name, in_specs=[HBM_SPEC], out_specs=HBM_SPEC,
        out_shape=jax.ShapeDtypeStruct((nchip, rows, cols), x.dtype),
        scratch_shapes=[pltpu.SemaphoreType.DMA((nchip,)), pltpu.SemaphoreType.DMA((nchip,))],
    )(x)


def _chip_exchange(name, x):
    rows, cols = x.shape[-2:]
    nchip = NDEV // 2

    def body(x_ref, o_ref, send_sems, recv_sems, local_sem):
        ix, iy, ic = lax.axis_index("x"), lax.axis_index("y"), lax.axis_index("c")
        my_chip = 2 * ix + iy
        own = pltpu.make_async_copy(x_ref.at[my_chip], o_ref.at[my_chip], local_sem)
        own.start()
        copies = []
        for k in range(1, nchip):
            px, py = ix ^ (k >> 1), iy ^ (k & 1)
            copies.append(_remote(x_ref.at[2 * px + py], o_ref.at[my_chip], send_sems, recv_sems, k - 1, (px, py, ic)))
        for cp in copies:
            cp.start()
        for cp in copies:
            cp.wait()
        own.wait()

    return pl.pallas_call(
        body, name=name, in_specs=[HBM_SPEC], out_specs=HBM_SPEC,
        out_shape=jax.ShapeDtypeStruct((nchip, rows, cols), x.dtype),
        scratch_shapes=[pltpu.SemaphoreType.DMA((nchip - 1,)), pltpu.SemaphoreType.DMA((nchip - 1,)),
                        pltpu.SemaphoreType.DMA],
    )(x)


COL_SHARDED = {"w_in": (D, D_IN), "w_br_gdn": (HW, D), "w_br_sb": (HW, D), "w_br_mem": (HW, D), "w_up": (D, DFF),
               "conv_w": (4, 3 * HW)}
ROW_SHARDED = {"w_mem_kv": (D, 2 * HW), "w_o": (D, D), "w_down": (DFF, D)}


def _pack_rows(parts, total):
    flat = jnp.concatenate([p.reshape(-1, LANES) for p in parts], axis=0)
    return jnp.pad(flat, ((0, total - flat.shape[0]), (0, 0)))


def _pack_full_grads(grads):
    parts = []
    for name in BIG:
        g = grads[name]
        if name in COL_SHARDED:
            r, c = COL_SHARDED[name]
            g = g.reshape(r, NDEV, c // NDEV).transpose(1, 0, 2)
        parts.append(g.reshape(NDEV, -1, LANES))
    flat = jnp.concatenate(parts, axis=1)
    return jnp.pad(flat, ((0, 0), (0, R_BIG - flat.shape[1]), (0, 0)))


def _unpack_gathered(slabs):
    out, pos = {}, 0
    for name, rows in zip(BIG, BIG_ROWS):
        g = slabs[:, pos:pos + rows]
        pos += rows
        if name in COL_SHARDED:
            r, c = COL_SHARDED[name]
            out[name] = g.reshape(NDEV, r, c // NDEV).transpose(1, 0, 2).reshape(r, c)
        else:
            r, c = ROW_SHARDED[name]
            out[name] = g.reshape(r, c)
    return out


def _unpack_shard(flat, shapes):
    out, pos = {}, 0
    for name, rows in zip(BIG, BIG_ROWS):
        out[name] = flat[pos:pos + rows].reshape(shapes[name])
        pos += rows
    return out


def _pack_small(vals):
    rows = []
    for name, n in zip(SMALL, SMALL_ROWS):
        v = vals[name].reshape(-1)
        rows.append(jnp.pad(v, (0, n * LANES - v.shape[0])).reshape(n, LANES))
    return _pack_rows(rows, R_SMALL)


def _unpack_small(flat, shapes):
    out, pos = {}, 0
    for name, n in zip(SMALL, SMALL_ROWS):
        size = shapes[name][-1]
        out[name] = flat[pos:pos + n].reshape(-1)[:size].reshape(shapes[name])
        pos += n
    return out


def _pad_w_in(w):
    return jnp.concatenate([w[:, :2048], w[:, 2056:], w[:, 2048:2056], jnp.zeros((D, D_INP - D_IN), w.dtype)], axis=1)


def _unpad_w_in(w):
    return jnp.concatenate([w[:, :2048], w[:, 7168:7176], w[:, 2048:7168]], axis=1)


def _per_head(v):
    return jnp.repeat(v.reshape(NH), DH).reshape(1, HW)


def _local_step(x, mem, target, w, sm):
    S = x.shape[0]
    ts = _row_tile(S)
    alog_f, dtb_f = _per_head(sm["a_log"]), _per_head(sm["dt_bias"])

    proj = _mm("in_proj", x, w["w_in"], "nn", ts, 1536, D, pro="rms", pro_g=sm["norm1_g"], n_outer=True)
    gq, gk, gv, gf, bf, sqn, skn, svb, qmn = _pre_fwd(proj, w["conv_w"], alog_f, dtb_f, sm["sb_q_norm_g"],
                                                      sm["sb_k_norm_g"], sm["mem_q_norm_g"], S)
    ogdn, states = _gdn_fwd(gq, gk, gv, gf, bf, S)
    osb, sb_tot = _sb_fwd(sqn, skn, svb, S)
    kv = _mm("mem_kv", mem, w["w_mem_kv"], "nn", NMEM, D, D, pro="rms", pro_g=sm["mem_norm_g"])
    omem = _mem_fwd(qmn, kv, sm["mem_k_norm_g"], S)
    x1, mix = _merge_fwd(x, proj, ogdn, osb, omem, sm["gdn_norm_g"], w["w_br_gdn"], w["w_br_sb"], w["w_br_mem"],
                         w["w_o"], S)
    up = _mm("mlp_up", x1, w["w_up"], "nn", ts, 2048, D, pro="rms", pro_g=sm["norm2_g"], n_outer=True)
    x2 = _mm("mlp_down", up, w["w_down"], "nn", ts, D, 1024, pro="relu2", epi="add", epi_x=x1)
    dy, loss = _loss_grad(x2, target, S)

    g = {}
    dup = _mm("d_up", dy, w["w_down"], "nt", ts, 1024, D, epi="drelu2", epi_x=up, out_dtype=BF16)
    g["w_down"] = _mm("dw_down", up, dy, "tn", 1024, D, 512, pro="relu2")
    g["w_up"] = _mm("dw_up", x1, dup, "tn", D, 1024, 512, pro="rms", pro_g=sm["norm2_g"])
    dh2 = _mm("d_h2", dup, w["w_up"], "nt", ts, D, 1024)
    dx1, g["norm2_g"] = _norm_bwd("norm2_bwd", dh2, x1, sm["norm2_g"], dy)

    dmix = _mm("d_mix", dx1, w["w_o"], "nt", ts, D, D)
    g["w_o"] = _mm("dw_o", mix, dx1, "tn", D, D, 512)
    (dgl0, dgl1, dgl2, dogdn, dz, dosb, domem, g["w_br_gdn"], g["w_br_sb"], g["w_br_mem"],
     g["gdn_norm_g"]) = _merge_bwd(dmix, proj, ogdn, osb, omem, sm["gdn_norm_g"], w["w_br_gdn"], w["w_br_sb"],
                                   w["w_br_mem"], S)
    dmq, dkv, g["mem_q_norm_g"], g["mem_k_norm_g"] = _mem_bwd(proj, qmn, kv, sm["mem_q_norm_g"], sm["mem_k_norm_g"],
                                                             domem, S)
    g["w_mem_kv"] = _mm("dw_mem_kv", mem, dkv, "tn", D, D, NMEM, pro="rms", pro_g=sm["mem_norm_g"])
    dmn = _mm("d_mem_n", dkv, w["w_mem_kv"], "nt", NMEM, D, D)
    _, g["mem_norm_g"] = _norm_bwd("mem_norm_bwd", dmn, mem, sm["mem_norm_g"], None)
    dsqn, dskn, dsv = _sb_bwd(sqn, skn, svb, dosb, sb_tot, S)
    dgq, dgk, dgv, dgf, dbf = _gdn_bwd(gq, gk, gv, gf, bf, states, dogdn, S)
    dc, dab, dsq, dsk, g["conv_w"], dal_f, ddt_f, g["sb_q_norm_g"], g["sb_k_norm_g"] = _pre_bwd(
        proj, w["conv_w"], alog_f, dtb_f, sm["sb_q_norm_g"], sm["sb_k_norm_g"], dgq, dgk, dgv, dgf, dbf, dsqn, dskn, S)
    g["a_log"] = dal_f.reshape(NH, DH)[:, 0].reshape(1, NH)
    g["dt_bias"] = ddt_f.reshape(NH, DH)[:, 0].reshape(1, NH)
    dqkv = _conv_bwd(dc, w["conv_w"], S)

    dproj = jnp.concatenate([dqkv, dz, dsq, dsk, dsv, dmq, dgl0, dgl1, dgl2, dab], axis=1)
    g["w_in"] = _mm("dw_in", x, dproj, "tn", D, 1536, 512, pro="rms", pro_g=sm["norm1_g"])
    dh = _mm("d_h", dproj, w["w_in"], "nt", ts, D, 1536)
    dx, g["norm1_g"] = _norm_bwd("norm1_bwd", dh, x, sm["norm1_g"], dx1)
    return loss[0, 0], dx, g


def kernel(x, mem, norm1_g, w_in, conv_w, a_log, dt_bias, gdn_norm_g, sb_q_norm_g, sb_k_norm_g, mem_norm_g, w_mem_kv, mem_q_norm_g, mem_k_norm_g, w_br_gdn, w_br_sb, w_br_mem, w_o, norm2_g, w_up, w_down, loss_target, m_norm1_g, m_w_in, m_conv_w, m_a_log, m_dt_bias, m_gdn_norm_g, m_sb_q_norm_g, m_sb_k_norm_g, m_mem_norm_g, m_w_mem_kv, m_mem_q_norm_g, m_mem_k_norm_g, m_w_br_gdn, m_w_br_sb, m_w_br_mem, m_w_o, m_norm2_g, m_w_up, m_w_down, v_norm1_g, v_w_in, v_conv_w, v_a_log, v_dt_bias, v_gdn_norm_g, v_sb_q_norm_g, v_sb_k_norm_g, v_mem_norm_g, v_w_mem_kv, v_mem_q_norm_g, v_mem_k_norm_g, v_w_br_gdn, v_w_br_sb, v_w_br_mem, v_w_o, v_norm2_g, v_w_up, v_w_down):
    given = dict(norm1_g=norm1_g, w_in=w_in, conv_w=conv_w, a_log=a_log, dt_bias=dt_bias, gdn_norm_g=gdn_norm_g,
                 sb_q_norm_g=sb_q_norm_g, sb_k_norm_g=sb_k_norm_g, mem_norm_g=mem_norm_g, w_mem_kv=w_mem_kv,
                 mem_q_norm_g=mem_q_norm_g, mem_k_norm_g=mem_k_norm_g, w_br_gdn=w_br_gdn, w_br_sb=w_br_sb,
                 w_br_mem=w_br_mem, w_o=w_o, norm2_g=norm2_g, w_up=w_up, w_down=w_down)
    mom1 = dict(norm1_g=m_norm1_g, w_in=m_w_in, conv_w=m_conv_w, a_log=m_a_log, dt_bias=m_dt_bias,
                gdn_norm_g=m_gdn_norm_g, sb_q_norm_g=m_sb_q_norm_g, sb_k_norm_g=m_sb_k_norm_g,
                mem_norm_g=m_mem_norm_g, w_mem_kv=m_w_mem_kv, mem_q_norm_g=m_mem_q_norm_g,
                mem_k_norm_g=m_mem_k_norm_g, w_br_gdn=m_w_br_gdn, w_br_sb=m_w_br_sb, w_br_mem=m_w_br_mem, w_o=m_w_o,
                norm2_g=m_norm2_g, w_up=m_w_up, w_down=m_w_down)
    mom2 = dict(norm1_g=v_norm1_g, w_in=v_w_in, conv_w=v_conv_w, a_log=v_a_log, dt_bias=v_dt_bias,
                gdn_norm_g=v_gdn_norm_g, sb_q_norm_g=v_sb_q_norm_g, sb_k_norm_g=v_sb_k_norm_g,
                mem_norm_g=v_mem_norm_g, w_mem_kv=v_w_mem_kv, mem_q_norm_g=v_mem_q_norm_g,
                mem_k_norm_g=v_mem_k_norm_g, w_br_gdn=v_w_br_gdn, w_br_sb=v_w_br_sb, w_br_mem=v_w_br_mem, w_o=v_w_o,
                norm2_g=v_norm2_g, w_up=v_w_up, w_down=v_w_down)
    shapes = {n: given[n].shape for n in WEIGHTS}

    w_loc = _pack_rows([given[n][0] for n in BIG], R_BIG)
    gathered = _gather("gather_weights", w_loc.astype(BF16))
    w = _unpack_gathered(gathered[:, :sum(BIG_ROWS)])
    w["w_in"] = _pad_w_in(w["w_in"])
    conv_loc = jnp.pad(given["conv_w"][0].reshape(-1, LANES), ((0, 2), (0, 0)))
    conv_all = _gather("gather_conv", conv_loc)
    w["conv_w"] = conv_all[:, :6].reshape(NDEV, 4, 3 * HW // NDEV).transpose(1, 0, 2).reshape(4, 3 * HW)
    sm = {n: given[n] for n in SMALL}

    loss, dx, g = _local_step(x[0], mem[0], loss_target[0], w, sm)
    g["w_in"] = _unpad_w_in(g["w_in"])

    g_mine = _pack_full_grads(g).astype(BF16)
    g_pair = _pair_sum(g_mine, _sibling_exchange("scatter_sibling", g_mine))
    g_all = _chip_exchange("scatter_chips", g_pair)
    gb, db, mb, vb = _adamw("adamw_sharded", g_all, w_loc, _pack_rows([mom1[n][0] for n in BIG], R_BIG),
                            _pack_rows([mom2[n][0] for n in BIG], R_BIG))
    gs_all = _gather("gather_small_grads", _pack_small(g))
    gs, dsm, ms, vs = _adamw("adamw_replicated", gs_all, _pack_small(given), _pack_small(mom1), _pack_small(mom2))

    outs = {}
    for prefix, big, small in (("grad_", gb, gs), ("delta_", db, dsm), ("new_m_", mb, ms), ("new_v_", vb, vs)):
        vals = _unpack_shard(big, shapes)
        vals.update(_unpack_small(small, shapes))
        for n in WEIGHTS:
            outs[prefix + n] = vals[n]
    loss = lax.psum(loss, ("x", "y", "c"))
    return (loss, dx[None], *[outs[p + n] for p in ("grad_", "delta_", "new_m_", "new_v_") for n in WEIGHTS])
```

```python
import jax
import jax.numpy as jnp
from jax import lax
from jax.experimental import pallas as pl
from jax.experimental.pallas import tpu as pltpu

F32 = jnp.float32
BF16 = jnp.bfloat16

D = 1024
NH = 4
DH = 128
HW = NH * DH
DFF = 4 * D
NMEM = 256
EPS = 1e-6
NDEV = 8
LANES = 128
PAIR = 128
CHUNK = 64
D_IN = 7176
D_INP = 7680
VMEM_LIMIT = 56 * 1024 * 1024

ADAM_LR, ADAM_B1, ADAM_B2, ADAM_EPS, ADAM_WD, ADAM_STEP = 0.001, 0.9, 0.999, 1e-08, 0.01, 10

CB_Z, CB_SQ, CB_SK, CB_SV, CB_MQ, CB_AB = 3, 4, 5, 6, 7, 14

NN = (((1,), (0,)), ((), ()))
NT = (((1,), (1,)), ((), ()))
TN = (((0,), (0,)), ((), ()))

BIG = ("w_in", "w_mem_kv", "w_br_gdn", "w_br_sb", "w_br_mem", "w_o", "w_up", "w_down", "conv_w")
BIG_ROWS = (7176, 1024, 512, 512, 512, 1024, 4096, 4096, 6)
R_BIG = 19456
SMALL = ("norm1_g", "a_log", "dt_bias", "gdn_norm_g", "sb_q_norm_g", "sb_k_norm_g", "mem_norm_g",
         "mem_q_norm_g", "mem_k_norm_g", "norm2_g")
SMALL_ROWS = (8, 1, 1, 1, 1, 1, 8, 1, 1, 8)
R_SMALL = 32
WEIGHTS = ("norm1_g", "w_in", "conv_w", "a_log", "dt_bias", "gdn_norm_g", "sb_q_norm_g", "sb_k_norm_g",
           "mem_norm_g", "w_mem_kv", "mem_q_norm_g", "mem_k_norm_g", "w_br_gdn", "w_br_sb", "w_br_mem",
           "w_o", "norm2_g", "w_up", "w_down")


def _cp(sem=None):
    return pltpu.CompilerParams(dimension_semantics=sem, vmem_limit_bytes=VMEM_LIMIT)


def _dot(a, b, dims=NN):
    return lax.dot_general(a, b, dims, preferred_element_type=F32)


def _dbf(a, b, dims=NN):
    return _dot(a.astype(BF16), b.astype(BF16), dims)


def _split(a, n):
    parts = []
    for _ in range(n):
        h = a.astype(BF16)
        parts.append(h)
        a = a - h.astype(F32)
    return parts


def _d3(a, b, dims=NN):
    ah, al = _split(a, 2)
    bh, bl = _split(b, 2)
    return _dot(ah, bh, dims) + (_dot(ah, bl, dims) + _dot(al, bh, dims))


def _dxr(a, e, dims=NN):
    eb = e.astype(BF16)
    a1, a2, a3 = _split(a, 3)
    return _dot(a1, eb, dims) + (_dot(a2, eb, dims) + _dot(a3, eb, dims))


def _dxl(e, a, dims=NN):
    eb = e.astype(BF16)
    a1, a2, a3 = _split(a, 3)
    return _dot(eb, a1, dims) + (_dot(eb, a2, dims) + _dot(eb, a3, dims))


def _sigmoid(x):
    return 1.0 / (1.0 + jnp.exp(-x))


def _softplus(x):
    return jnp.maximum(x, 0.0) + jnp.log(1.0 + jnp.exp(-jnp.abs(x)))


def _rms(x, g):
    r = lax.rsqrt(jnp.mean(x * x, axis=-1, keepdims=True) + EPS)
    return x * r * g, r


def _rms_bwd(dy, x, g, r):
    dyg = dy * g
    dx = r * (dyg - x * (r * r) * jnp.mean(dyg * x, axis=-1, keepdims=True))
    dg = jnp.sum(dy * (x * r), axis=0, keepdims=True)
    return dx, dg


def _hs(h):
    return slice(h * DH, (h + 1) * DH)


def _row_tile(s):
    return 512 if s >= 2048 else 256


def _narrow_tile(s):
    return min(256, s)


def _mm(name, a, b, mode, tm, tn, tk, pro=None, pro_g=None, epi=None, epi_x=None, out_dtype=F32, n_outer=False):
    if mode == "tn":
        K, M = a.shape
    else:
        M, K = a.shape
    N = b.shape[0] if mode == "nt" else b.shape[1]
    tm, tn, tk = min(tm, M), min(tn, N), min(tk, K)
    nk = K // tk
    assert M % tm == 0 and N % tn == 0 and K % tk == 0, (name, M, N, K, tm, tn, tk)
    dims = {"nn": NN, "nt": NT, "tn": TN}[mode]

    def body(*refs):
        a_ref, b_ref = refs[0], refs[1]
        pos = 2
        g_ref = e_ref = None
        if pro == "rms":
            g_ref = refs[pos]
            pos += 1
        if epi is not None:
            e_ref = refs[pos]
            pos += 1
        o_ref = refs[pos]
        av = a_ref[...]
        if pro == "rms":
            av, _ = _rms(av.astype(F32), g_ref[...])
        elif pro == "relu2":
            av = jnp.square(jnp.maximum(av, 0.0))
        part = _dbf(av, b_ref[...], dims)

        def finish(acc):
            if epi == "add":
                acc = acc + e_ref[...]
            elif epi == "drelu2":
                acc = acc * (2.0 * jnp.maximum(e_ref[...], 0.0))
            o_ref[...] = acc.astype(out_dtype)

        if nk == 1:
            finish(part)
        else:
            acc_ref = refs[pos + 1]
            k = pl.program_id(2)

            @pl.when(k == 0)
            def _():
                acc_ref[...] = part

            @pl.when(k > 0)
            def _():
                acc_ref[...] += part

            @pl.when(k == nk - 1)
            def _():
                finish(acc_ref[...])

    def spec(shape, index):
        if n_outer:
            return pl.BlockSpec(shape, lambda j, i, k: index(i, j, k))
        return pl.BlockSpec(shape, index)

    if mode == "tn":
        a_spec = spec((tk, tm), lambda i, j, k: (k, i))
    else:
        a_spec = spec((tm, tk), lambda i, j, k: (i, k))
    if mode == "nt":
        b_spec = spec((tn, tk), lambda i, j, k: (j, k))
    else:
        b_spec = spec((tk, tn), lambda i, j, k: (k, j))
    in_specs, ops = [a_spec, b_spec], [a, b]
    if pro == "rms":
        w = pro_g.shape[1]
        assert (tm if mode == "tn" else tk) == w, name
        in_specs.append(spec((1, w), lambda i, j, k: (0, 0)))
        ops.append(pro_g)
    if epi is not None:
        in_specs.append(spec((tm, tn), lambda i, j, k: (i, j)))
        ops.append(epi_x)
    grid = (N // tn, M // tm, nk) if n_outer else (M // tm, N // tn, nk)
    return pl.pallas_call(
        body, name=name, grid=grid,
        in_specs=in_specs, out_specs=spec((tm, tn), lambda i, j, k: (i, j)),
        out_shape=jax.ShapeDtypeStruct((M, N), out_dtype),
        scratch_shapes=[pltpu.VMEM((tm, tn), F32)] if nk > 1 else [],
        compiler_params=_cp(("parallel", "parallel", "arbitrary")),
    )(*ops)


def _head_select(first_lane):
    l = lax.broadcasted_iota(jnp.int32, (LANES, HW), 0)
    c = lax.broadcasted_iota(jnp.int32, (LANES, HW), 1)
    return (l == first_lane + c // DH).astype(F32)


def _conv_taps(buf, cw, ts):
    c = cw[3:4, :] * buf[8:8 + ts, :]
    for j in range(3):
        k = 3 - j
        c = c + cw[j:j + 1, :] * buf[8 - k:8 - k + ts, :]
    return c


def _pre_fwd(proj, conv_w, alog_f, dtb_f, gsq, gsk, gmq, S):
    ts = _narrow_tile(S)
    hb = ts // 8

    def body(qkv_ref, halo_ref, ab_ref, sq_ref, sk_ref, sv_ref, mq_ref, cw_ref, al_ref, dt_ref, gsq_ref, gsk_ref,
             gmq_ref, gq_o, gk_o, gv_o, gf_o, bf_o, sqn_o, skn_o, svb_o, qmn_o, buf):
        i = pl.program_id(0)
        buf[0:8, :] = jnp.where(i == 0, 0.0, halo_ref[...])
        buf[8:8 + ts, :] = qkv_ref[...]
        c = _conv_taps(buf, cw_ref[...], ts)
        a = c * _sigmoid(c)
        for h in range(NH):
            q = a[:, h * DH:(h + 1) * DH]
            k = a[:, HW + h * DH:HW + (h + 1) * DH]
            gq_o[:, _hs(h)] = q * (lax.rsqrt(jnp.sum(q * q, axis=-1, keepdims=True) + EPS) * DH ** -0.5)
            gk_o[:, _hs(h)] = k * lax.rsqrt(jnp.sum(k * k, axis=-1, keepdims=True) + EPS)
            sqn_o[:, _hs(h)] = _rms(sq_ref[:, _hs(h)], gsq_ref[...])[0].astype(BF16)
            skn_o[:, _hs(h)] = _rms(sk_ref[:, _hs(h)], gsk_ref[...])[0].astype(BF16)
            qmn_o[:, _hs(h)] = _rms(mq_ref[:, _hs(h)], gmq_ref[...])[0].astype(BF16)
        gv_o[...] = a[:, 2 * HW:3 * HW]
        svb_o[...] = sv_ref[...].astype(BF16)
        ab = ab_ref[:, 0:LANES]
        a_bc = _dxr(ab, _head_select(0))
        b_bc = _dxr(ab, _head_select(NH))
        gf_o[...] = -jnp.exp(al_ref[...]) * _softplus(a_bc + dt_ref[...])
        bf_o[...] = _sigmoid(b_bc)

    row = lambda cb: pl.BlockSpec((ts, HW), lambda i: (i, cb))
    full = lambda r, c: pl.BlockSpec((r, c), lambda i: (0, 0))
    f32o = jax.ShapeDtypeStruct((S, HW), F32)
    bfo = jax.ShapeDtypeStruct((S, HW), BF16)
    return pl.pallas_call(
        body, name="pre_fwd", grid=(S // ts,),
        in_specs=[pl.BlockSpec((ts, 3 * HW), lambda i: (i, 0)),
                  pl.BlockSpec((8, 3 * HW), lambda i: (jnp.maximum(i * hb - 1, 0), 0)),
                  row(CB_AB), row(CB_SQ), row(CB_SK), row(CB_SV), row(CB_MQ),
                  full(4, 3 * HW), full(1, HW), full(1, HW), full(1, DH), full(1, DH), full(1, DH)],
        out_specs=[pl.BlockSpec((ts, HW), lambda i: (i, 0))] * 9,
        out_shape=[f32o, f32o, f32o, f32o, f32o, bfo, bfo, bfo, bfo],
        scratch_shapes=[pltpu.VMEM((ts + 8, 3 * HW), F32)],
        compiler_params=_cp(("parallel",)),
    )(proj, proj, proj, proj, proj, proj, proj, conv_w, alog_f, dtb_f, gsq, gsk, gmq)


def _pre_bwd(proj, conv_w, alog_f, dtb_f, gsq, gsk, dgq, dgk, dgv, dgf, dbf, dsqn, dskn, S):
    ts = _narrow_tile(S)
    hb = ts // 8

    def body(qkv_ref, halo_ref, ab_ref, sq_ref, sk_ref, cw_ref, al_ref, dt_ref, gsq_ref, gsk_ref,
             dgq_ref, dgk_ref, dgv_ref, dgf_ref, dbf_ref, dsqn_ref, dskn_ref,
             dc_o, dab_o, dsq_o, dsk_o, dcw_o, dal_o, ddt_o, dgsq_o, dgsk_o, buf):
        i = pl.program_id(0)

        @pl.when(i == 0)
        def _():
            dcw_o[...] = jnp.zeros_like(dcw_o)
            dal_o[...] = jnp.zeros_like(dal_o)
            ddt_o[...] = jnp.zeros_like(ddt_o)
            dgsq_o[...] = jnp.zeros_like(dgsq_o)
            dgsk_o[...] = jnp.zeros_like(dgsk_o)

        buf[0:8, :] = jnp.where(i == 0, 0.0, halo_ref[...])
        buf[8:8 + ts, :] = qkv_ref[...]
        c = _conv_taps(buf, cw_ref[...], ts)
        sg = _sigmoid(c)
        a = c * sg
        dsilu = sg * (1.0 + c * (1.0 - sg))
        dgsq = jnp.zeros((1, DH), F32)
        dgsk = jnp.zeros((1, DH), F32)
        for h in range(NH):
            q = a[:, h * DH:(h + 1) * DH]
            k = a[:, HW + h * DH:HW + (h + 1) * DH]
            nq = lax.rsqrt(jnp.sum(q * q, axis=-1, keepdims=True) + EPS)
            nk = lax.rsqrt(jnp.sum(k * k, axis=-1, keepdims=True) + EPS)
            dyq = dgq_ref[:, _hs(h)]
            dyk = dgk_ref[:, _hs(h)]
            dq = (nq * dyq - q * (nq * nq * nq) * jnp.sum(dyq * q, axis=-1, keepdims=True)) * DH ** -0.5
            dk = nk * dyk - k * (nk * nk * nk) * jnp.sum(dyk * k, axis=-1, keepdims=True)
            dc_o[:, h * DH:(h + 1) * DH] = dq * dsilu[:, h * DH:(h + 1) * DH]
            dc_o[:, HW + h * DH:HW + (h + 1) * DH] = dk * dsilu[:, HW + h * DH:HW + (h + 1) * DH]
            x = sq_ref[:, _hs(h)]
            _, r = _rms(x, gsq_ref[...])
            dx, dg = _rms_bwd(dsqn_ref[:, _hs(h)], x, gsq_ref[...], r)
            dsq_o[:, _hs(h)] = dx.astype(BF16)
            dgsq = dgsq + dg
            x = sk_ref[:, _hs(h)]
            _, r = _rms(x, gsk_ref[...])
            dx, dg = _rms_bwd(dskn_ref[:, _hs(h)], x, gsk_ref[...], r)
            dsk_o[:, _hs(h)] = dx.astype(BF16)
            dgsk = dgsk + dg
        dc_o[:, 2 * HW:3 * HW] = dgv_ref[...] * dsilu[:, 2 * HW:3 * HW]
        dgsq_o[...] += dgsq
        dgsk_o[...] += dgsk
        dc = dc_o[...]
        for j in range(4):
            k = 3 - j
            dcw_o[j:j + 1, :] += jnp.sum(dc * buf[8 - k:8 - k + ts, :], axis=0, keepdims=True)
        ab = ab_ref[:, 0:LANES]
        a_bc = _dxr(ab, _head_select(0))
        b_bc = _dxr(ab, _head_select(NH))
        pre = a_bc + dt_ref[...]
        ea = jnp.exp(al_ref[...])
        dgf = dgf_ref[...]
        dal_o[...] += jnp.sum(dgf * (-ea * _softplus(pre)), axis=0, keepdims=True)
        da = dgf * (-ea * _sigmoid(pre))
        ddt_o[...] += jnp.sum(da, axis=0, keepdims=True)
        beta = _sigmoid(b_bc)
        db = dbf_ref[...] * beta * (1.0 - beta)
        lane = lax.broadcasted_iota(jnp.int32, (ts, LANES), 1)
        dab = jnp.zeros((ts, LANES), F32)
        for h in range(NH):
            dab = dab + jnp.where(lane == h, da[:, _hs(h)], 0.0) + jnp.where(lane == NH + h, db[:, _hs(h)], 0.0)
        dab_o[:, 0:LANES] = dab.astype(BF16)
        dab_o[:, LANES:HW] = jnp.zeros((ts, HW - LANES), BF16)

    row = lambda cb: pl.BlockSpec((ts, HW), lambda i: (i, cb))
    full = lambda r, c: pl.BlockSpec((r, c), lambda i: (0, 0))
    t512 = pl.BlockSpec((ts, HW), lambda i: (i, 0))
    return pl.pallas_call(
        body, name="pre_bwd", grid=(S // ts,),
        in_specs=[pl.BlockSpec((ts, 3 * HW), lambda i: (i, 0)),
                  pl.BlockSpec((8, 3 * HW), lambda i: (jnp.maximum(i * hb - 1, 0), 0)),
                  row(CB_AB), row(CB_SQ), row(CB_SK),
                  full(4, 3 * HW), full(1, HW), full(1, HW), full(1, DH), full(1, DH)] + [t512] * 7,
        out_specs=[pl.BlockSpec((ts, 3 * HW), lambda i: (i, 0)), t512, t512, t512,
                   full(4, 3 * HW), full(1, HW), full(1, HW), full(1, DH), full(1, DH)],
        out_shape=[jax.ShapeDtypeStruct((S, 3 * HW), F32)] + [jax.ShapeDtypeStruct((S, HW), BF16)] * 3
        + [jax.ShapeDtypeStruct((4, 3 * HW), F32), jax.ShapeDtypeStruct((1, HW), F32),
           jax.ShapeDtypeStruct((1, HW), F32), jax.ShapeDtypeStruct((1, DH), F32),
           jax.ShapeDtypeStruct((1, DH), F32)],
        scratch_shapes=[pltpu.VMEM((ts + 8, 3 * HW), F32)],
        compiler_params=_cp(("arbitrary",)),
    )(proj, proj, proj, proj, proj, conv_w, alog_f, dtb_f, gsq, gsk, dgq, dgk, dgv, dgf, dbf, dsqn, dskn)


def _conv_bwd(dc, conv_w, S):
    ts = _row_tile(S)
    hb = ts // 8
    n = S // ts

    def body(dc_ref, halo_ref, cw_ref, o_ref, buf):
        i = pl.program_id(0)
        buf[0:ts, :] = dc_ref[...]
        buf[ts:ts + 8, :] = jnp.where(i == n - 1, 0.0, halo_ref[...])
        cw = cw_ref[...]
        acc = cw[3:4, :] * buf[0:ts, :]
        for k in range(1, 4):
            acc = acc + cw[3 - k:4 - k, :] * buf[k:k + ts, :]
        o_ref[...] = acc.astype(BF16)

    return pl.pallas_call(
        body, name="conv_bwd", grid=(n,),
        in_specs=[pl.BlockSpec((ts, 3 * HW), lambda i: (i, 0)),
                  pl.BlockSpec((8, 3 * HW), lambda i: (jnp.minimum((i + 1) * hb, S // 8 - 1), 0)),
                  pl.BlockSpec((4, 3 * HW), lambda i: (0, 0))],
        out_specs=pl.BlockSpec((ts, 3 * HW), lambda i: (i, 0)),
        out_shape=jax.ShapeDtypeStruct((S, 3 * HW), BF16),
        scratch_shapes=[pltpu.VMEM((ts + 8, 3 * HW), F32)],
        compiler_params=_cp(("parallel",)),
    )(dc, dc, conv_w)


def _gdn_masks():
    r = lax.broadcasted_iota(jnp.int32, (PAIR, PAIR), 0)
    c = lax.broadcasted_iota(jnp.int32, (PAIR, PAIR), 1)
    same = ((r >= CHUNK) & (c >= CHUNK)) | ((r < CHUNK) & (c < CHUNK))
    return dict(r=r, same=same, tril=same & (r >= c), strict=same & (r > c), triu=same & (c >= r), eye=r == c,
                in_a=r < CHUNK, last_a=r == CHUNK - 1, last_b=r == PAIR - 1)


def _each(fn, *cols):
    return [fn(*xs) for xs in zip(*cols)]


def _mul(a, b):
    return a * b


def _top(x):
    return x[:CHUNK]


def _bot(x):
    return x[CHUNK:]


def _rows(a, b):
    return jnp.concatenate([a, b], axis=0)


def _tri_inv(lm, eye):
    eye_f = eye.astype(F32)
    p = _each(lambda l: eye_f - l, lm)
    lp = _each(lambda l: _d3(l, l), lm)
    for it in range(5):
        p = _each(lambda a, b: a + _d3(a, b), p, lp)
        if it < 4:
            lp = _each(lambda b: _d3(b, b), lp)
    return p


def _gdn_block(m, q, k, v, g, beta):
    tril_f = m["tril"].astype(F32)
    col_sum = lambda mask: (lambda x: jnp.sum(jnp.where(mask, x, 0.0), axis=0, keepdims=True))
    gc = _each(lambda x: _dxl(tril_f, x), g)
    gcr = _each(col_sum(m["eye"]), gc)
    gam = _each(lambda a, b: jnp.where(m["tril"], jnp.exp(jnp.minimum(a - b, 0.0)), 0.0), gc, gcr)
    kb = _each(_mul, k, beta)
    vb = _each(_mul, v, beta)
    lm = _each(lambda a, b, c: jnp.where(m["strict"], _d3(a, b, NT) * c, 0.0), kb, k, gam)
    t = _tri_inv(lm, m["eye"])
    eg = _each(jnp.exp, gc)
    kbe = _each(_mul, kb, eg)
    u = _each(_d3, t, vb)
    w = _each(_d3, t, kbe)
    aqk = _each(lambda a, b, c: jnp.where(m["tril"], _d3(a, b, NT) * c, 0.0), q, k, gam)
    qd = _each(_mul, q, eg)
    ga = _each(col_sum(m["last_a"]), gc)
    gb = _each(col_sum(m["last_b"]), gc)
    e2 = _each(lambda a, b, c: jnp.exp(jnp.where(m["in_a"], a, b) - c), ga, gb, gc)
    kd = _each(_mul, k, e2)
    return dict(u=u, w=w, aqk=aqk, qd=qd, kd=kd, gam=gam, kb=kb, vb=vb, lm=lm, t=t, eg=eg, kbe=kbe, e2=e2,
                gla=_each(jnp.exp, ga), glb=_each(jnp.exp, gb))


def _gdn_fwd(gq, gk, gv, gf, bf, S):
    nb = S // PAIR

    def body(q_ref, k_ref, v_ref, g_ref, b_ref, o_ref, st_ref, s_scr):
        @pl.when(pl.program_id(0) == 0)
        def _():
            s_scr[...] = jnp.zeros_like(s_scr)

        m = _gdn_masks()
        heads = lambda ref: [ref[:, _hs(h)] for h in range(NH)]
        f = _gdn_block(m, heads(q_ref), heads(k_ref), heads(v_ref), heads(g_ref), heads(b_ref))
        u, w, qd, kd = f["u"], f["w"], f["qd"], f["kd"]
        s0 = [s_scr[h * DH:(h + 1) * DH, :] for h in range(NH)]
        vna = _each(lambda a, b, s: _top(a) - _d3(_top(b), s), u, w, s0)
        oa = _each(lambda a, s: _d3(_top(a), s), qd, s0)
        s1 = _each(lambda s, gl, a, vn: s * gl + _d3(_top(a), vn, TN), s0, f["gla"], kd, vna)
        vnb = _each(lambda a, b, s: _bot(a) - _d3(_bot(b), s), u, w, s1)
        ob = _each(lambda a, s: _d3(_bot(a), s), qd, s1)
        s2 = _each(lambda s, gl, a, vn: s * gl + _d3(_bot(a), vn, TN), s1, f["glb"], kd, vnb)
        outs = _each(lambda a, b, c, va, vb: _rows(a, b) + _d3(c, _rows(va, vb)), oa, ob, f["aqk"], vna, vnb)
        o_ref[...] = jnp.concatenate(outs, axis=1)
        st_ref[...] = jnp.concatenate(s0 + s1, axis=0)
        s_scr[...] = jnp.concatenate(s2, axis=0)

    blk = pl.BlockSpec((PAIR, HW), lambda i: (i, 0))
    return pl.pallas_call(
        body, name="gdn_fwd", grid=(nb,),
        in_specs=[blk] * 5,
        out_specs=[blk, pl.BlockSpec((2 * NH * DH, DH), lambda i: (i, 0))],
        out_shape=[jax.ShapeDtypeStruct((S, HW), F32), jax.ShapeDtypeStruct((nb * 2 * NH * DH, DH), F32)],
        scratch_shapes=[pltpu.VMEM((NH * DH, DH), F32)],
        compiler_params=_cp(("arbitrary",)),
    )(gq, gk, gv, gf, bf)


def _gdn_bwd(gq, gk, gv, gf, bf, states, do, S):
    nb = S // PAIR

    def body(q_ref, k_ref, v_ref, g_ref, b_ref, st_ref, do_ref, dq_o, dk_o, dv_o, dg_o, db_o, ds_scr):
        @pl.when(pl.program_id(0) == 0)
        def _():
            ds_scr[...] = jnp.zeros_like(ds_scr)

        m = _gdn_masks()
        ones = jnp.ones((PAIR, PAIR), F32)
        heads = lambda ref: [ref[:, _hs(h)] for h in range(NH)]
        q, k, v, beta, do = heads(q_ref), heads(k_ref), heads(v_ref), heads(b_ref), heads(do_ref)
        f = _gdn_block(m, q, k, v, heads(g_ref), beta)
        u, w, aqk, qd, kd, t = f["u"], f["w"], f["aqk"], f["qd"], f["kd"], f["t"]
        s0 = [st_ref[h * DH:(h + 1) * DH, :] for h in range(NH)]
        s1 = [st_ref[(NH + h) * DH:(NH + h + 1) * DH, :] for h in range(NH)]
        ds2 = [ds_scr[h * DH:(h + 1) * DH, :] for h in range(NH)]
        total = lambda a, b: jnp.sum(jnp.sum(a * b, axis=1, keepdims=True), axis=0, keepdims=True)
        vna = _each(lambda a, b, s: _top(a) - _d3(_top(b), s), u, w, s0)
        vnb = _each(lambda a, b, s: _bot(a) - _d3(_bot(b), s), u, w, s1)
        dvn_i = _each(lambda a, b: _d3(a, b, TN), aqk, do)
        dvnb = _each(lambda a, b, s: _bot(a) + _d3(_bot(b), s), dvn_i, kd, ds2)
        dqdb = _each(lambda a, s: _d3(_bot(a), s, NT), do, s1)
        dkdb = _each(lambda a, s: _d3(a, s, NT), vnb, ds2)
        dglb = _each(total, ds2, s1)
        dwb = _each(lambda a, s: -_d3(a, s, NT), dvnb, s1)
        ds1 = _each(lambda s, gl, a, b, c, d: s * gl + _d3(_bot(a), _bot(b), TN) - _d3(_bot(c), d, TN),
                    ds2, f["glb"], qd, do, w, dvnb)
        dvna = _each(lambda a, b, s: _top(a) + _d3(_top(b), s), dvn_i, kd, ds1)
        dqda = _each(lambda a, s: _d3(_top(a), s, NT), do, s0)
        dkda = _each(lambda a, s: _d3(a, s, NT), vna, ds1)
        dgla = _each(total, ds1, s0)
        dwa = _each(lambda a, s: -_d3(a, s, NT), dvna, s0)
        ds0 = _each(lambda s, gl, a, b, c, d: s * gl + _d3(_top(a), _top(b), TN) - _d3(_top(c), d, TN),
                    ds1, f["gla"], qd, do, w, dvna)
        dvn, dqd, dkd, dw = (_each(_rows, a, b) for a, b in ((dvna, dvnb), (dqda, dqdb), (dkda, dkdb), (dwa, dwb)))
        daqk = _each(lambda a, va, vb: jnp.where(m["tril"], _d3(a, _rows(va, vb), NT), 0.0), do, vna, vnb)
        dt = _each(lambda a, b, c, d: _d3(a, b, NT) + _d3(c, d, NT), dvn, f["vb"], dw, f["kbe"])
        dvb = _each(lambda a, b: _d3(a, b, TN), t, dvn)
        dkbe = _each(lambda a, b: _d3(a, b, TN), t, dw)
        dtt = _each(lambda a, b: _d3(a, b, NT), dt, t)
        dl = _each(lambda a, b: -jnp.where(m["strict"], _d3(a, b, TN), 0.0), t, dtt)
        dm = _each(_mul, dl, f["gam"])
        dn = _each(_mul, daqk, f["gam"])
        dkb = _each(lambda a, b, c, d: _d3(a, b) + c * d, dm, k, dkbe, f["eg"])
        dks = _each(lambda a, b, c, d, e, g, h, i: _d3(a, b, TN) + _d3(c, d, TN) + e * g + h * i,
                    dm, f["kb"], dn, q, dkd, f["e2"], beta, dkb)
        dqs = _each(lambda a, b, c, d: _d3(a, b) + c * d, dn, k, dqd, f["eg"])
        gm = _each(lambda a, b, c, d: a * b + c * d, dl, f["lm"], daqk, aqk)
        dkdkd = _each(_mul, dkd, kd)
        dgc = _each(lambda a, b, c, d, e, g: _dxr(a + b * c + d * e - g, ones) - _dxr(a, ones, TN),
                    gm, dqd, qd, dkbe, f["kbe"], dkdkd)
        same_f = m["same"].astype(F32)
        chunk_tot = _each(lambda a: _dxl(same_f, _dxr(a, ones)), dkdkd)
        last = m["last_a"] | m["last_b"]
        dgc = _each(lambda a, b, ga, gla, gb, glb: a + jnp.where(last, b + jnp.where(m["in_a"], ga * gla, gb * glb), 0.0),
                    dgc, chunk_tot, dgla, f["gla"], dglb, f["glb"])
        dbs = _each(lambda a, b, c, d: _dxr(a * b + c * d, ones), dkb, k, dvb, v)
        dvs = _each(_mul, beta, dvb)
        triu_f = m["triu"].astype(F32)
        dgs = _each(lambda a: _dxl(triu_f, a), dgc)
        for ref, parts in ((dq_o, dqs), (dk_o, dks), (dv_o, dvs), (dg_o, dgs), (db_o, dbs)):
            ref[...] = jnp.concatenate(parts, axis=1)
        ds_scr[...] = jnp.concatenate(ds0, axis=0)

    blk = pl.BlockSpec((PAIR, HW), lambda i: (nb - 1 - i, 0))
    o = jax.ShapeDtypeStruct((S, HW), F32)
    return pl.pallas_call(
        body, name="gdn_bwd", grid=(nb,),
        in_specs=[blk] * 5 + [pl.BlockSpec((2 * NH * DH, DH), lambda i: (nb - 1 - i, 0)), blk],
        out_specs=[blk] * 5, out_shape=[o] * 5,
        scratch_shapes=[pltpu.VMEM((NH * DH, DH), F32)],
        compiler_params=_cp(("arbitrary",)),
    )(gq, gk, gv, gf, bf, states, do)


SB_T = 256
SB_GROUP = 4
SB_GROUP_BWD = 4
SB_SINGLES = 2
SB_DEAD = -110.0


def _group_sizes(g):
    sizes = []
    while g >= 1:
        sizes.append(g)
        g //= 2
    return sizes


def _sb_iotas(t):
    return lax.broadcasted_iota(jnp.int32, (t, t), 0), lax.broadcasted_iota(jnp.int32, (t, t), 1)


def _sb_scores(q, k, mask):
    z = _dot(q, k, NT) * DH ** -0.5
    ls = jnp.minimum(z, 0.0) - jnp.log(1.0 + jnp.exp(-jnp.abs(z)))
    lneg = ls - z
    if mask is not None:
        lneg = jnp.where(mask, lneg, 0.0)
    return ls, lneg


def _prefix(x, u):
    xh, xl = _split(x, 2)
    return _dot(xh, u) + _dot(xl, u)


def _sb_fwd(sqn, skn, svb, S):
    t = min(SB_T, S)

    def body(q_ref, k_ref, v_ref, o_ref, t_ref, cnt_ref):
        qb = pl.program_id(1)
        q = q_ref[...]
        r, c = _sb_iotas(t)
        diag = c < r
        u_after = (r > c).astype(BF16)

        def tiles(k0s, run, mask):
            sc = _each(lambda k0: _sb_scores(q, k_ref[pl.ds(k0, t), :], mask), k0s)
            ls, lneg = [s[0] for s in sc], [s[1] for s in sc]
            sums = _each(lambda x: jnp.sum(x, axis=1, keepdims=True), lneg)
            pre = _each(lambda x: _prefix(x, u_after), lneg)
            runs = [run]
            for s in sums:
                runs.append(runs[-1] + s)
            att = _each(lambda a, b, rn: jnp.exp(a + (rn + b)), ls, pre, runs[:-1])
            if mask is not None:
                att = _each(lambda a: jnp.where(mask, a, 0.0), att)
            parts = _each(lambda a, k0: _dot(a.astype(BF16), v_ref[pl.ds(k0, t), :]), att, k0s)
            return sum(parts[1:], parts[0]), runs[-1]

        acc, run = tiles([pl.multiple_of(qb * t, t)], jnp.zeros((t, 1), F32), diag)

        def alive(run):
            return jnp.max(run) >= SB_DEAD

        carry, done = (0, acc, run, alive(run)), 0
        for size, limit in [(1, SB_SINGLES)] + [(s, None) for s in _group_sizes(SB_GROUP)]:

            def more(c, size=size, done=done, limit=limit):
                i, _, _, go = c
                fits = done + (i + 1) * size <= qb
                return (fits if limit is None else fits & (i < limit)) & go

            def group(c, size=size, done=done):
                i, acc, run, _ = c
                first = qb - 1 - done - size * i
                part, run = tiles([pl.multiple_of((first - j) * t, t) for j in range(size)], run, None)
                return i + 1, acc + part, run, alive(run)

            n, acc, run, go = lax.while_loop(more, group, (0,) + carry[1:])
            carry, done = (0, acc, run, go), done + n * size
        o_ref[...] = acc.astype(BF16)
        t_ref[...] = jnp.broadcast_to(run, (t, DH))
        cnt_ref[pl.program_id(0), qb] = done

    qspec = pl.BlockSpec((t, DH), lambda h, i: (i, h))
    kspec = pl.BlockSpec((S, DH), lambda h, i: (0, h))
    return pl.pallas_call(
        body, name="sb_fwd", grid=(NH, S // t),
        in_specs=[qspec, kspec, kspec],
        out_specs=[qspec, qspec, pl.BlockSpec(memory_space=pltpu.SMEM)],
        out_shape=[jax.ShapeDtypeStruct((S, HW), BF16), jax.ShapeDtypeStruct((S, HW), F32),
                   jax.ShapeDtypeStruct((NH, S // t), jnp.int32)],
        compiler_params=_cp(("arbitrary", "arbitrary")),
    )(sqn, skn, svb)


def _sb_bwd(sqn, skn, svb, do, tot, walked, S):
    t = min(SB_T, S)

    def body(cnt_ref, q_ref, k_ref, v_ref, do_ref, t_ref, dq_o, dk_o, dv_o, dv_acc):
        qb = pl.program_id(1)

        @pl.when(qb == 0)
        def _():
            dk_o[...] = jnp.zeros_like(dk_o)
            dv_acc[...] = jnp.zeros_like(dv_acc)

        q = q_ref[...]
        do = do_ref[...].astype(BF16)
        tot_l = jnp.concatenate([t_ref[...]] * (t // DH), axis=1)
        r, c = _sb_iotas(t)
        diag = c < r
        u_upto = (r <= c).astype(BF16)
        u_before = (r < c).astype(BF16)

        def tiles(k0s, run_l, run_e, mask):
            rowsum = lambda x: jnp.sum(x, axis=1, keepdims=True)
            ks = [k_ref[pl.ds(k0, t), :] for k0 in k0s]
            vs = [v_ref[pl.ds(k0, t), :] for k0 in k0s]
            sc = _each(lambda k: _sb_scores(q, k, mask), ks)
            ls, lneg = [s[0] for s in sc], [s[1] for s in sc]
            sums_l = _each(rowsum, lneg)
            pre_l = _each(lambda x: _prefix(x, u_upto), lneg)
            runs_l = [run_l]
            for s in sums_l:
                runs_l.append(runs_l[-1] + s)
            att = _each(lambda a, b, rn: jnp.exp(a + (tot_l - (rn + b))), ls, pre_l, runs_l[:-1])
            if mask is not None:
                att = _each(lambda a: jnp.where(mask, a, 0.0), att)
            e = _each(lambda v, a: _dot(do, v, NT) * a, vs, att)
            sums_e = _each(rowsum, e)
            pre_e = _each(lambda x: _prefix(x, u_before), e)
            runs_e = [run_e]
            for s in sums_e:
                runs_e.append(runs_e[-1] + s)
            sg = _each(jnp.exp, ls)
            dz = _each(lambda a, b, rn, s: a * (1.0 - s) - (rn + b) * s, e, pre_e, runs_e[:-1], sg)
            if mask is not None:
                dz = _each(lambda a: jnp.where(mask, a, 0.0), dz)
            dz = _each(lambda a: (a * DH ** -0.5).astype(BF16), dz)
            dvs = _each(lambda a: _dot(a.astype(BF16), do, TN), att)
            dks = _each(lambda a: _dot(a, q, TN), dz)
            dqs = _each(_dot, dz, ks)
            for k0, dv, dk in zip(k0s, dvs, dks):
                dv_acc[pl.ds(k0, t), :] += dv
                dk_o[pl.ds(k0, t), :] += dk
            return sum(dqs[1:], dqs[0]), runs_l[-1], runs_e[-1]

        walked = cnt_ref[pl.program_id(0), qb]
        z1 = jnp.zeros((t, 1), F32)
        carry, done = (jnp.zeros((t, DH), F32), z1, z1), 0
        for size in _group_sizes(SB_GROUP_BWD):
            n = (walked - done) // size

            def group(i, carry, size=size, done=done):
                dq, run_l, run_e = carry
                first = qb - walked + done + size * i
                part, run_l, run_e = tiles([pl.multiple_of((first + j) * t, t) for j in range(size)], run_l, run_e,
                                           None)
                return dq + part, run_l, run_e

            carry = lax.fori_loop(0, n, group, carry)
            done = done + n * size
        dq, run_l, run_e = carry
        part, _, _ = tiles([pl.multiple_of(qb * t, t)], run_l, run_e, diag)
        dq_o[...] = dq + part

        @pl.when(qb == S // t - 1)
        def _():
            dv_o[...] = dv_acc[...].astype(BF16)

    qspec = pl.BlockSpec((t, DH), lambda h, i, cnt: (i, h))
    kspec = pl.BlockSpec((S, DH), lambda h, i, cnt: (0, h))
    o = jax.ShapeDtypeStruct((S, HW), F32)
    return pl.pallas_call(
        body, name="sb_bwd",
        grid_spec=pltpu.PrefetchScalarGridSpec(
            num_scalar_prefetch=1, grid=(NH, S // t),
            in_specs=[qspec, kspec, kspec, qspec, qspec], out_specs=[qspec, kspec, kspec],
            scratch_shapes=[pltpu.VMEM((S, DH), F32)]),
        out_shape=[o, o, jax.ShapeDtypeStruct((S, HW), BF16)],
        compiler_params=_cp(("parallel", "arbitrary")),
    )(walked, sqn, skn, svb, do, tot)


def _mem_probs(qn, kn):
    s = _dot(qn, kn.astype(BF16), NT) * DH ** -0.5
    p = jnp.exp(s - jnp.max(s, axis=-1, keepdims=True))
    return p / jnp.sum(p, axis=-1, keepdims=True)


def _mem_fwd(qmn, kv, gmk, S):
    ts = _row_tile(S)

    def body(q_ref, kv_ref, gk_ref, o_ref):
        for h in range(NH):
            kn, _ = _rms(kv_ref[:, _hs(h)], gk_ref[...])
            p = _mem_probs(q_ref[:, _hs(h)], kn)
            o_ref[:, _hs(h)] = _dbf(p, kv_ref[:, HW + h * DH:HW + (h + 1) * DH]).astype(BF16)

    return pl.pallas_call(
        body, name="mem_fwd", grid=(S // ts,),
        in_specs=[pl.BlockSpec((ts, HW), lambda i: (i, 0)), pl.BlockSpec((NMEM, 2 * HW), lambda i: (0, 0)),
                  pl.BlockSpec((1, DH), lambda i: (0, 0))],
        out_specs=pl.BlockSpec((ts, HW), lambda i: (i, 0)),
        out_shape=jax.ShapeDtypeStruct((S, HW), BF16),
        compiler_params=_cp(("parallel",)),
    )(qmn, kv, gmk)


def _mem_bwd(proj, qmn, kv, gmq, gmk, do, S):
    ts = _row_tile(S)
    n = S // ts

    def body(mq_ref, q_ref, kv_ref, gq_ref, gk_ref, do_ref, dmq_o, dkv_o, dgq_o, dgk_o, dkn_scr):
        i = pl.program_id(0)

        @pl.when(i == 0)
        def _():
            dkv_o[...] = jnp.zeros_like(dkv_o)
            dgq_o[...] = jnp.zeros_like(dgq_o)
            dkn_scr[...] = jnp.zeros_like(dkn_scr)

        dgq = jnp.zeros((1, DH), F32)
        for h in range(NH):
            km = kv_ref[:, _hs(h)]
            vm = kv_ref[:, HW + h * DH:HW + (h + 1) * DH].astype(BF16)
            kn, _ = _rms(km, gk_ref[...])
            qn = q_ref[:, _hs(h)]
            p = _mem_probs(qn, kn)
            dob = do_ref[:, _hs(h)].astype(BF16)
            dkv_o[:, HW + h * DH:HW + (h + 1) * DH] += _dot(p.astype(BF16), dob, TN)
            dp = _dot(dob, vm, NT)
            dsc = (p * (dp - jnp.sum(dp * p, axis=-1, keepdims=True)) * DH ** -0.5).astype(BF16)
            dkn_scr[:, _hs(h)] += _dot(dsc, qn, TN)
            x = mq_ref[:, _hs(h)]
            _, r = _rms(x, gq_ref[...])
            dx, dg = _rms_bwd(_dot(dsc, kn.astype(BF16)), x, gq_ref[...], r)
            dmq_o[:, _hs(h)] = dx.astype(BF16)
            dgq = dgq + dg
        dgq_o[...] += dgq

        @pl.when(i == n - 1)
        def _():
            dgk = jnp.zeros((1, DH), F32)
            for h in range(NH):
                km = kv_ref[:, _hs(h)]
                _, r = _rms(km, gk_ref[...])
                dx, dg = _rms_bwd(dkn_scr[:, _hs(h)], km, gk_ref[...], r)
                dkv_o[:, _hs(h)] = dx
                dgk = dgk + dg
            dgk_o[...] = dgk

    full = lambda r, c: pl.BlockSpec((r, c), lambda i: (0, 0))
    t512 = pl.BlockSpec((ts, HW), lambda i: (i, 0))
    return pl.pallas_call(
        body, name="mem_bwd", grid=(n,),
        in_specs=[pl.BlockSpec((ts, HW), lambda i: (i, CB_MQ)), t512, full(NMEM, 2 * HW), full(1, DH), full(1, DH),
                  t512],
        out_specs=[t512, full(NMEM, 2 * HW), full(1, DH), full(1, DH)],
        out_shape=[jax.ShapeDtypeStruct((S, HW), BF16), jax.ShapeDtypeStruct((NMEM, 2 * HW), F32),
                   jax.ShapeDtypeStruct((1, DH), F32), jax.ShapeDtypeStruct((1, DH), F32)],
        scratch_shapes=[pltpu.VMEM((NMEM, HW), F32)],
        compiler_params=_cp(("arbitrary",)),
    )(proj, qmn, kv, gmq, gmk, do)


def _gated_gdn(o, z, g):
    sg = _sigmoid(z)
    outs, rs = [], []
    for h in range(NH):
        y, r = _rms(o[:, _hs(h)], g)
        outs.append(y * (z[:, _hs(h)] * sg[:, _hs(h)]))
        rs.append(r)
    return jnp.concatenate(outs, axis=1), rs, sg


def _merge_fwd(x, proj, ogdn, osb, omem, ggdn, wbg, wbs, wbm, wo, S):
    ts = _narrow_tile(S)

    def body(x_ref, z_ref, g0_ref, g1_ref, g2_ref, og_ref, os_ref, om_ref, gg_ref, wbg_ref, wbs_ref, wbm_ref,
             wo_ref, x1_o, mix_o):
        on, _, _ = _gated_gdn(og_ref[...], z_ref[...], gg_ref[...])
        mix = (_sigmoid(g0_ref[...]) * _dbf(on, wbg_ref[...]) + _sigmoid(g1_ref[...]) * _dbf(os_ref[...], wbs_ref[...])
               + _sigmoid(g2_ref[...]) * _dbf(om_ref[...], wbm_ref[...]))
        mix_o[...] = mix.astype(BF16)
        x1_o[...] = x_ref[...] + _dbf(mix, wo_ref[...])

    t512 = pl.BlockSpec((ts, HW), lambda i: (i, 0))
    t1k = pl.BlockSpec((ts, D), lambda i: (i, 0))
    gate = lambda j: pl.BlockSpec((ts, D), lambda i: (i, 4 + j))
    full = lambda r, c: pl.BlockSpec((r, c), lambda i: (0, 0))
    return pl.pallas_call(
        body, name="merge_fwd", grid=(S // ts,),
        in_specs=[t1k, pl.BlockSpec((ts, HW), lambda i: (i, CB_Z)), gate(0), gate(1), gate(2), t512, t512, t512,
                  full(1, DH), full(HW, D), full(HW, D), full(HW, D), full(D, D)],
        out_specs=[t1k, t1k],
        out_shape=[jax.ShapeDtypeStruct((S, D), F32), jax.ShapeDtypeStruct((S, D), BF16)],
        compiler_params=_cp(("parallel",)),
    )(x, proj, proj, proj, proj, ogdn, osb, omem, ggdn, wbg, wbs, wbm, wo)


def _merge_bwd(dmix, proj, ogdn, osb, omem, ggdn, wbg, wbs, wbm, S):
    ts = _narrow_tile(S)

    def body(dm_ref, z_ref, g0_ref, g1_ref, g2_ref, og_ref, os_ref, om_ref, gg_ref, wbg_ref, wbs_ref, wbm_ref,
             dgl0_o, dgl1_o, dgl2_o, dog_o, dz_o, dos_o, dom_o, dwbg_o, dwbs_o, dwbm_o, dgg_o):
        @pl.when(pl.program_id(0) == 0)
        def _():
            for ref in (dwbg_o, dwbs_o, dwbm_o, dgg_o):
                ref[...] = jnp.zeros_like(ref)

        dm = dm_ref[...]
        og = og_ref[...]
        z = z_ref[...]
        on, rs, sg = _gated_gdn(og, z, gg_ref[...])
        branch = ((on, g0_ref, wbg_ref, dgl0_o, dwbg_o), (os_ref[...], g1_ref, wbs_ref, dgl1_o, dwbs_o),
                  (om_ref[...], g2_ref, wbm_ref, dgl2_o, dwbm_o))
        dos = []
        for o, g_ref, w_ref, dgl_o, dw_o in branch:
            ob = o.astype(BF16)
            gate = _sigmoid(g_ref[...])
            dgl_o[...] = (dm * _dot(ob, w_ref[...]) * gate * (1.0 - gate)).astype(BF16)
            dy = (dm * gate).astype(BF16)
            dw_o[...] += _dot(ob, dy, TN)
            dos.append(_dot(dy, w_ref[...], NT))
        dos_o[...] = dos[1].astype(BF16)
        dom_o[...] = dos[2].astype(BF16)
        don = dos[0]
        dgg = jnp.zeros((1, DH), F32)
        for h in range(NH):
            oh, zh, sh = og[:, _hs(h)], z[:, _hs(h)], sg[:, _hs(h)]
            y = oh * rs[h] * gg_ref[...]
            dz_o[:, _hs(h)] = (don[:, _hs(h)] * y * (sh * (1.0 + zh * (1.0 - sh)))).astype(BF16)
            dx, dg = _rms_bwd(don[:, _hs(h)] * (zh * sh), oh, gg_ref[...], rs[h])
            dog_o[:, _hs(h)] = dx
            dgg = dgg + dg
        dgg_o[...] += dgg

    t512 = pl.BlockSpec((ts, HW), lambda i: (i, 0))
    t1k = pl.BlockSpec((ts, D), lambda i: (i, 0))
    gate = lambda j: pl.BlockSpec((ts, D), lambda i: (i, 4 + j))
    full = lambda r, c: pl.BlockSpec((r, c), lambda i: (0, 0))
    s1k = jax.ShapeDtypeStruct((S, D), BF16)
    s512 = jax.ShapeDtypeStruct((S, HW), BF16)
    wsh = jax.ShapeDtypeStruct((HW, D), F32)
    return pl.pallas_call(
        body, name="merge_bwd", grid=(S // ts,),
        in_specs=[t1k, pl.BlockSpec((ts, HW), lambda i: (i, CB_Z)), gate(0), gate(1), gate(2), t512, t512, t512,
                  full(1, DH), full(HW, D), full(HW, D), full(HW, D)],
        out_specs=[t1k, t1k, t1k, t512, t512, t512, t512, full(HW, D), full(HW, D), full(HW, D), full(1, DH)],
        out_shape=[s1k, s1k, s1k, jax.ShapeDtypeStruct((S, HW), F32), s512, s512, s512, wsh, wsh, wsh,
                   jax.ShapeDtypeStruct((1, DH), F32)],
        compiler_params=_cp(("arbitrary",)),
    )(dmix, proj, proj, proj, proj, ogdn, osb, omem, ggdn, wbg, wbs, wbm)


def _loss_grad(y, target, S):
    ts = _row_tile(S)

    def body(y_ref, t_ref, dy_o, loss_o):
        @pl.when(pl.program_id(0) == 0)
        def _():
            loss_o[...] = jnp.zeros_like(loss_o)

        err = y_ref[...] - t_ref[...]
        dy_o[...] = err * (1.0 / D)
        per_tok = jnp.sum(err * err, axis=1, keepdims=True) * (1.0 / D)
        loss_o[...] += 0.5 * jnp.sum(per_tok, axis=0, keepdims=True)

    t1k = pl.BlockSpec((ts, D), lambda i: (i, 0))
    return pl.pallas_call(
        body, name="loss_grad", grid=(S // ts,), in_specs=[t1k, t1k],
        out_specs=[t1k, pl.BlockSpec((1, 1), lambda i: (0, 0))],
        out_shape=[jax.ShapeDtypeStruct((S, D), F32), jax.ShapeDtypeStruct((1, 1), F32)],
        compiler_params=_cp(("arbitrary",)),
    )(y, target)


def _norm_bwd(name, dh, x, g, res):
    rows = x.shape[0]
    ts = min(_row_tile(rows), rows)

    def body(*refs):
        dh_ref, x_ref, g_ref = refs[:3]
        dx_o, dg_o = refs[-2:]

        @pl.when(pl.program_id(0) == 0)
        def _():
            dg_o[...] = jnp.zeros_like(dg_o)

        xv = x_ref[...]
        _, r = _rms(xv, g_ref[...])
        dx, dg = _rms_bwd(dh_ref[...], xv, g_ref[...], r)
        dx_o[...] = dx if res is None else dx + refs[3][...]
        dg_o[...] += dg

    t1k = pl.BlockSpec((ts, D), lambda i: (i, 0))
    gsp = pl.BlockSpec((1, D), lambda i: (0, 0))
    ops = [dh, x, g] + ([] if res is None else [res])
    return pl.pallas_call(
        body, name=name, grid=(rows // ts,), in_specs=[t1k, t1k, gsp] + ([] if res is None else [t1k]),
        out_specs=[t1k, gsp],
        out_shape=[jax.ShapeDtypeStruct((rows, D), F32), jax.ShapeDtypeStruct((1, D), F32)],
        compiler_params=_cp(("arbitrary",)),
    )(*ops)


def _adamw(name, gall, w, m, v):
    rows = w.shape[0]
    nsrc = gall.shape[0]
    tr = min(1216, rows)
    assert rows % tr == 0

    def body(g_ref, w_ref, m_ref, v_ref, g_o, d_o, m_o, v_o):
        g = g_ref[0].astype(F32)
        for j in range(1, nsrc):
            g = g + g_ref[j].astype(F32)
        m_new = ADAM_B1 * m_ref[...] + (1.0 - ADAM_B1) * g
        v_new = ADAM_B2 * v_ref[...] + (1.0 - ADAM_B2) * jnp.square(g)
        m_hat = m_new / (1.0 - ADAM_B1 ** ADAM_STEP)
        v_hat = v_new / (1.0 - ADAM_B2 ** ADAM_STEP)
        g_o[...] = g
        d_o[...] = -ADAM_LR * (m_hat / (jnp.sqrt(v_hat) + ADAM_EPS) + ADAM_WD * w_ref[...])
        m_o[...] = m_new
        v_o[...] = v_new

    t = pl.BlockSpec((tr, LANES), lambda i: (i, 0))
    o = jax.ShapeDtypeStruct((rows, LANES), F32)
    return pl.pallas_call(
        body, name=name, grid=(rows // tr,),
        in_specs=[pl.BlockSpec((nsrc, tr, LANES), lambda i: (0, i, 0)), t, t, t],
        out_specs=[t, t, t, t], out_shape=[o, o, o, o],
        compiler_params=_cp(("parallel",)),
    )(gall, w, m, v)


def _pair_sum(mine, theirs):
    rows = mine.shape[1]
    tr = min(1216, rows)
    assert rows % tr == 0
    core = lax.axis_index("c").astype(jnp.int32).reshape(1)

    def body(c_ref, a_ref, b_ref, o_ref):
        o_ref[...] = (a_ref[...].astype(F32) + b_ref[...].astype(F32)).astype(o_ref.dtype)

    blk = pl.BlockSpec((1, tr, LANES), lambda j, i, c_ref: (j, i, 0))
    return pl.pallas_call(
        body, name="pair_sum",
        grid_spec=pltpu.PrefetchScalarGridSpec(
            num_scalar_prefetch=1, grid=(NDEV // 2, rows // tr),
            in_specs=[pl.BlockSpec((1, tr, LANES), lambda j, i, c_ref: (2 * j + c_ref[0], i, 0)), blk],
            out_specs=blk),
        out_shape=jax.ShapeDtypeStruct((NDEV // 2, rows, LANES), mine.dtype),
        compiler_params=_cp(("parallel", "parallel")),
    )(core, mine, theirs)


HBM_SPEC = pl.BlockSpec(memory_space=pltpu.HBM)


def _remote(src, dst, send_sems, recv_sems, k, to):
    return pltpu.make_async_remote_copy(src_ref=src, dst_ref=dst, send_sem=send_sems.at[k], recv_sem=recv_sems.at[k],
                                        device_id=to, device_id_type=pl.DeviceIdType.MESH)


def _gather(name, x):
    rows, cols = x.shape

    def body(x_ref, o_ref, send_sems, recv_sems, local_sem):
        ix, iy, ic = lax.axis_index("x"), lax.axis_index("y"), lax.axis_index("c")
        me, sibling = (ix, iy, ic), (ix, iy, 1 - ic)
        chips = [(1 - ix, iy), (ix, 1 - iy), (1 - ix, 1 - iy)]

        def slab(px, py, pc):
            return o_ref.at[4 * px + 2 * py + pc]

        def copy(k, block, to, src=None):
            return _remote(slab(*block) if src is None else src, slab(*block), send_sems, recv_sems, k, to)

        mine = pltpu.make_async_copy(x_ref, slab(*me), local_sem)
        mine.start()
        first = [copy(0, me, sibling, src=x_ref)]
        first += [copy(1 + j, me, (*chip, ic), src=x_ref) for j, chip in enumerate(chips)]
        for cp in first:
            cp.start()
        passed = [copy(4 + j, (*chip, ic), sibling) for j, chip in enumerate(chips)]
        for j, chip in enumerate(chips):
            copy(1 + j, (*chip, ic), me).wait_recv()
            passed[j].start()
        copy(0, sibling, me).wait_recv()
        for j, chip in enumerate(chips):
            copy(4 + j, (*chip, 1 - ic), me).wait_recv()
        for cp in first + passed:
            cp.wait_send()
        mine.wait()

    return pl.pallas_call(
        body, name=name, in_specs=[HBM_SPEC], out_specs=HBM_SPEC,
        out_shape=jax.ShapeDtypeStruct((NDEV, rows, cols), x.dtype),
        scratch_shapes=[pltpu.SemaphoreType.DMA((NDEV - 1,)), pltpu.SemaphoreType.DMA((NDEV - 1,)),
                        pltpu.SemaphoreType.DMA],
    )(x)


def _sibling_exchange(name, x):
    rows, cols = x.shape[-2:]
    nchip = NDEV // 2

    def body(x_ref, o_ref, send_sems, recv_sems):
        ix, iy, ic = lax.axis_index("x"), lax.axis_index("y"), lax.axis_index("c")
        copies = [_remote(x_ref.at[2 * j + (1 - ic)], o_ref.at[j], send_sems, recv_sems, j, (ix, iy, 1 - ic))
                  for j in range(nchip)]
        for cp in copies:
            cp.start()
        for cp in copies:
            cp.wait()

    return pl.pallas_call(
        body, name=name, in_specs=[HBM_SPEC], out_specs=HBM_SPEC,
        out_shape=jax.ShapeDtypeStruct((nchip, rows, cols), x.dtype),
        scratch_shapes=[pltpu.SemaphoreType.DMA((nchip,)), pltpu.SemaphoreType.DMA((nchip,))],
    )(x)


def _chip_exchange(name, x):
    rows, cols = x.shape[-2:]
    nchip = NDEV // 2

    def body(x_ref, o_ref, send_sems, recv_sems, local_sem):
        ix, iy, ic = lax.axis_index("x"), lax.axis_index("y"), lax.axis_index("c")
        my_chip = 2 * ix + iy
        own = pltpu.make_async_copy(x_ref.at[my_chip], o_ref.at[my_chip], local_sem)
        own.start()
        copies = []
        for k in range(1, nchip):
            px, py = ix ^ (k >> 1), iy ^ (k & 1)
            copies.append(_remote(x_ref.at[2 * px + py], o_ref.at[my_chip], send_sems, recv_sems, k - 1, (px, py, ic)))
        for cp in copies:
            cp.start()
        for cp in copies:
            cp.wait()
        own.wait()

    return pl.pallas_call(
        body, name=name, in_specs=[HBM_SPEC], out_specs=HBM_SPEC,
        out_shape=jax.ShapeDtypeStruct((nchip, rows, cols), x.dtype),
        scratch_shapes=[pltpu.SemaphoreType.DMA((nchip - 1,)), pltpu.SemaphoreType.DMA((nchip - 1,)),
                        pltpu.SemaphoreType.DMA],
    )(x)


COL_SHARDED = {"w_in": (D, D_IN), "w_br_gdn": (HW, D), "w_br_sb": (HW, D), "w_br_mem": (HW, D), "w_up": (D, DFF),
               "conv_w": (4, 3 * HW)}
ROW_SHARDED = {"w_mem_kv": (D, 2 * HW), "w_o": (D, D), "w_down": (DFF, D)}


def _pack_rows(parts, total):
    flat = jnp.concatenate([p.reshape(-1, LANES) for p in parts], axis=0)
    return jnp.pad(flat, ((0, total - flat.shape[0]), (0, 0)))


def _pack_full_grads(grads):
    parts = []
    for name in BIG:
        g = grads[name]
        if name in COL_SHARDED:
            r, c = COL_SHARDED[name]
            g = g.reshape(r, NDEV, c // NDEV).transpose(1, 0, 2)
        parts.append(g.reshape(NDEV, -1, LANES))
    flat = jnp.concatenate(parts, axis=1)
    return jnp.pad(flat, ((0, 0), (0, R_BIG - flat.shape[1]), (0, 0)))


def _unpack_gathered(slabs):
    out, pos = {}, 0
    for name, rows in zip(BIG, BIG_ROWS):
        g = slabs[:, pos:pos + rows]
        pos += rows
        if name in COL_SHARDED:
            r, c = COL_SHARDED[name]
            out[name] = g.reshape(NDEV, r, c // NDEV).transpose(1, 0, 2).reshape(r, c)
        else:
            r, c = ROW_SHARDED[name]
            out[name] = g.reshape(r, c)
    return out


def _unpack_shard(flat, shapes):
    out, pos = {}, 0
    for name, rows in zip(BIG, BIG_ROWS):
        out[name] = flat[pos:pos + rows].reshape(shapes[name])
        pos += rows
    return out


def _pack_small(vals):
    rows = []
    for name, n in zip(SMALL, SMALL_ROWS):
        v = vals[name].reshape(-1)
        rows.append(jnp.pad(v, (0, n * LANES - v.shape[0])).reshape(n, LANES))
    return _pack_rows(rows, R_SMALL)


def _unpack_small(flat, shapes):
    out, pos = {}, 0
    for name, n in zip(SMALL, SMALL_ROWS):
        size = shapes[name][-1]
        out[name] = flat[pos:pos + n].reshape(-1)[:size].reshape(shapes[name])
        pos += n
    return out


def _pad_w_in(w):
    return jnp.concatenate([w[:, :2048], w[:, 2056:], w[:, 2048:2056], jnp.zeros((D, D_INP - D_IN), w.dtype)], axis=1)


def _unpad_w_in(w):
    return jnp.concatenate([w[:, :2048], w[:, 7168:7176], w[:, 2048:7168]], axis=1)


def _per_head(v):
    return jnp.repeat(v.reshape(NH), DH).reshape(1, HW)


def _local_step(x, mem, target, w, sm):
    S = x.shape[0]
    ts = _row_tile(S)
    alog_f, dtb_f = _per_head(sm["a_log"]), _per_head(sm["dt_bias"])

    proj = _mm("in_proj", x, w["w_in"], "nn", ts, 1536, D, pro="rms", pro_g=sm["norm1_g"], n_outer=True)
    gq, gk, gv, gf, bf, sqn, skn, svb, qmn = _pre_fwd(proj, w["conv_w"], alog_f, dtb_f, sm["sb_q_norm_g"],
                                                      sm["sb_k_norm_g"], sm["mem_q_norm_g"], S)
    ogdn, states = _gdn_fwd(gq, gk, gv, gf, bf, S)
    osb, sb_tot, sb_walked = _sb_fwd(sqn, skn, svb, S)
    kv = _mm("mem_kv", mem, w["w_mem_kv"], "nn", NMEM, D, D, pro="rms", pro_g=sm["mem_norm_g"])
    omem = _mem_fwd(qmn, kv, sm["mem_k_norm_g"], S)
    x1, mix = _merge_fwd(x, proj, ogdn, osb, omem, sm["gdn_norm_g"], w["w_br_gdn"], w["w_br_sb"], w["w_br_mem"],
                         w["w_o"], S)
    up = _mm("mlp_up", x1, w["w_up"], "nn", ts, 2048, D, pro="rms", pro_g=sm["norm2_g"], n_outer=True)
    x2 = _mm("mlp_down", up, w["w_down"], "nn", ts, D, 1024, pro="relu2", epi="add", epi_x=x1)
    dy, loss = _loss_grad(x2, target, S)

    g = {}
    dup = _mm("d_up", dy, w["w_down"], "nt", ts, 1024, D, epi="drelu2", epi_x=up, out_dtype=BF16)
    g["w_down"] = _mm("dw_down", up, dy, "tn", 1024, D, 512, pro="relu2")
    g["w_up"] = _mm("dw_up", x1, dup, "tn", D, 1024, 512, pro="rms", pro_g=sm["norm2_g"])
    dh2 = _mm("d_h2", dup, w["w_up"], "nt", ts, D, 1024)
    dx1, g["norm2_g"] = _norm_bwd("norm2_bwd", dh2, x1, sm["norm2_g"], dy)

    dmix = _mm("d_mix", dx1, w["w_o"], "nt", ts, D, D)
    g["w_o"] = _mm("dw_o", mix, dx1, "tn", D, D, 512)
    (dgl0, dgl1, dgl2, dogdn, dz, dosb, domem, g["w_br_gdn"], g["w_br_sb"], g["w_br_mem"],
     g["gdn_norm_g"]) = _merge_bwd(dmix, proj, ogdn, osb, omem, sm["gdn_norm_g"], w["w_br_gdn"], w["w_br_sb"],
                                   w["w_br_mem"], S)
    dmq, dkv, g["mem_q_norm_g"], g["mem_k_norm_g"] = _mem_bwd(proj, qmn, kv, sm["mem_q_norm_g"], sm["mem_k_norm_g"],
                                                             domem, S)
    g["w_mem_kv"] = _mm("dw_mem_kv", mem, dkv, "tn", D, D, NMEM, pro="rms", pro_g=sm["mem_norm_g"])
    dmn = _mm("d_mem_n", dkv, w["w_mem_kv"], "nt", NMEM, D, D)
    _, g["mem_norm_g"] = _norm_bwd("mem_norm_bwd", dmn, mem, sm["mem_norm_g"], None)
    dsqn, dskn, dsv = _sb_bwd(sqn, skn, svb, dosb, sb_tot, sb_walked, S)
    dgq, dgk, dgv, dgf, dbf = _gdn_bwd(gq, gk, gv, gf, bf, states, dogdn, S)
    dc, dab, dsq, dsk, g["conv_w"], dal_f, ddt_f, g["sb_q_norm_g"], g["sb_k_norm_g"] = _pre_bwd(
        proj, w["conv_w"], alog_f, dtb_f, sm["sb_q_norm_g"], sm["sb_k_norm_g"], dgq, dgk, dgv, dgf, dbf, dsqn, dskn, S)
    g["a_log"] = dal_f.reshape(NH, DH)[:, 0].reshape(1, NH)
    g["dt_bias"] = ddt_f.reshape(NH, DH)[:, 0].reshape(1, NH)
    dqkv = _conv_bwd(dc, w["conv_w"], S)

    dproj = jnp.concatenate([dqkv, dz, dsq, dsk, dsv, dmq, dgl0, dgl1, dgl2, dab], axis=1)
    g["w_in"] = _mm("dw_in", x, dproj, "tn", D, 1536, 512, pro="rms", pro_g=sm["norm1_g"])
    dh = _mm("d_h", dproj, w["w_in"], "nt", ts, D, 1536)
    dx, g["norm1_g"] = _norm_bwd("norm1_bwd", dh, x, sm["norm1_g"], dx1)
    return loss[0, 0], dx, g


def kernel(x, mem, norm1_g, w_in, conv_w, a_log, dt_bias, gdn_norm_g, sb_q_norm_g, sb_k_norm_g, mem_norm_g, w_mem_kv, mem_q_norm_g, mem_k_norm_g, w_br_gdn, w_br_sb, w_br_mem, w_o, norm2_g, w_up, w_down, loss_target, m_norm1_g, m_w_in, m_conv_w, m_a_log, m_dt_bias, m_gdn_norm_g, m_sb_q_norm_g, m_sb_k_norm_g, m_mem_norm_g, m_w_mem_kv, m_mem_q_norm_g, m_mem_k_norm_g, m_w_br_gdn, m_w_br_sb, m_w_br_mem, m_w_o, m_norm2_g, m_w_up, m_w_down, v_norm1_g, v_w_in, v_conv_w, v_a_log, v_dt_bias, v_gdn_norm_g, v_sb_q_norm_g, v_sb_k_norm_g, v_mem_norm_g, v_w_mem_kv, v_mem_q_norm_g, v_mem_k_norm_g, v_w_br_gdn, v_w_br_sb, v_w_br_mem, v_w_o, v_norm2_g, v_w_up, v_w_down):
    given = dict(norm1_g=norm1_g, w_in=w_in, conv_w=conv_w, a_log=a_log, dt_bias=dt_bias, gdn_norm_g=gdn_norm_g,
                 sb_q_norm_g=sb_q_norm_g, sb_k_norm_g=sb_k_norm_g, mem_norm_g=mem_norm_g, w_mem_kv=w_mem_kv,
                 mem_q_norm_g=mem_q_norm_g, mem_k_norm_g=mem_k_norm_g, w_br_gdn=w_br_gdn, w_br_sb=w_br_sb,
                 w_br_mem=w_br_mem, w_o=w_o, norm2_g=norm2_g, w_up=w_up, w_down=w_down)
    mom1 = dict(norm1_g=m_norm1_g, w_in=m_w_in, conv_w=m_conv_w, a_log=m_a_log, dt_bias=m_dt_bias,
                gdn_norm_g=m_gdn_norm_g, sb_q_norm_g=m_sb_q_norm_g, sb_k_norm_g=m_sb_k_norm_g,
                mem_norm_g=m_mem_norm_g, w_mem_kv=m_w_mem_kv, mem_q_norm_g=m_mem_q_norm_g,
                mem_k_norm_g=m_mem_k_norm_g, w_br_gdn=m_w_br_gdn, w_br_sb=m_w_br_sb, w_br_mem=m_w_br_mem, w_o=m_w_o,
                norm2_g=m_norm2_g, w_up=m_w_up, w_down=m_w_down)
    mom2 = dict(norm1_g=v_norm1_g, w_in=v_w_in, conv_w=v_conv_w, a_log=v_a_log, dt_bias=v_dt_bias,
                gdn_norm_g=v_gdn_norm_g, sb_q_norm_g=v_sb_q_norm_g, sb_k_norm_g=v_sb_k_norm_g,
                mem_norm_g=v_mem_norm_g, w_mem_kv=v_w_mem_kv, mem_q_norm_g=v_mem_q_norm_g,
                mem_k_norm_g=v_mem_k_norm_g, w_br_gdn=v_w_br_gdn, w_br_sb=v_w_br_sb, w_br_mem=v_w_br_mem, w_o=v_w_o,
                norm2_g=v_norm2_g, w_up=v_w_up, w_down=v_w_down)
    shapes = {n: given[n].shape for n in WEIGHTS}

    w_loc = _pack_rows([given[n][0] for n in BIG], R_BIG)
    gathered = _gather("gather_weights", w_loc.astype(BF16))
    w = _unpack_gathered(gathered[:, :sum(BIG_ROWS)])
    w["w_in"] = _pad_w_in(w["w_in"])
    conv_loc = jnp.pad(given["conv_w"][0].reshape(-1, LANES), ((0, 2), (0, 0)))
    conv_all = _gather("gather_conv", conv_loc)
    w["conv_w"] = conv_all[:, :6].reshape(NDEV, 4, 3 * HW // NDEV).transpose(1, 0, 2).reshape(4, 3 * HW)
    sm = {n: given[n] for n in SMALL}

    loss, dx, g = _local_step(x[0], mem[0], loss_target[0], w, sm)
    g["w_in"] = _unpad_w_in(g["w_in"])

    g_mine = _pack_full_grads(g).astype(BF16)
    g_pair = _pair_sum(g_mine, _sibling_exchange("scatter_sibling", g_mine))
    g_all = _chip_exchange("scatter_chips", g_pair)
    gb, db, mb, vb = _adamw("adamw_sharded", g_all, w_loc, _pack_rows([mom1[n][0] for n in BIG], R_BIG),
                            _pack_rows([mom2[n][0] for n in BIG], R_BIG))
    gs_all = _gather("gather_small_grads", _pack_small(g))
    gs, dsm, ms, vs = _adamw("adamw_replicated", gs_all, _pack_small(given), _pack_small(mom1), _pack_small(mom2))

    outs = {}
    for prefix, big, small in (("grad_", gb, gs), ("delta_", db, dsm), ("new_m_", mb, ms), ("new_v_", vb, vs)):
        vals = _unpack_shard(big, shapes)
        vals.update(_unpack_small(small, shapes))
        for n in WEIGHTS:
            outs[prefix + n] = vals[n]
    loss = lax.psum(loss, ("x", "y", "c"))
    return (loss, dx[None], *[outs[p + n] for p in ("grad_", "delta_", "new_m_", "new_v_") for n in WEIGHTS])
```

```python
import jax
import jax.numpy as jnp
from jax import lax
from jax.experimental import pallas as pl
from jax.experimental.pallas import tpu as pltpu

F32 = jnp.float32
BF16 = jnp.bfloat16

D = 1024
NH = 4
DH = 128
HW = NH * DH
DFF = 4 * D
NMEM = 256
EPS = 1e-6
NDEV = 8
LANES = 128
PAIR = 128
CHUNK = 64
D_IN = 7176
D_INP = 7680
VMEM_LIMIT = 56 * 1024 * 1024

ADAM_LR, ADAM_B1, ADAM_B2, ADAM_EPS, ADAM_WD, ADAM_STEP = 0.001, 0.9, 0.999, 1e-08, 0.01, 10

CB_Z, CB_SQ, CB_SK, CB_SV, CB_MQ, CB_AB = 3, 4, 5, 6, 7, 14

NN = (((1,), (0,)), ((), ()))
NT = (((1,), (1,)), ((), ()))
TN = (((0,), (0,)), ((), ()))

BIG = ("w_in", "w_mem_kv", "w_br_gdn", "w_br_sb", "w_br_mem", "w_o", "w_up", "w_down", "conv_w")
BIG_ROWS = (8192, 1024, 512, 512, 512, 1024, 4096, 4096, 8)
R_BIG = 20480
SLAB_TILE = 1024
SMALL = ("norm1_g", "a_log", "dt_bias", "gdn_norm_g", "sb_q_norm_g", "sb_k_norm_g", "mem_norm_g",
         "mem_q_norm_g", "mem_k_norm_g", "norm2_g")
SMALL_ROWS = (8, 1, 1, 1, 1, 1, 8, 1, 1, 8)
R_SMALL = 32
WEIGHTS = ("norm1_g", "w_in", "conv_w", "a_log", "dt_bias", "gdn_norm_g", "sb_q_norm_g", "sb_k_norm_g",
           "mem_norm_g", "w_mem_kv", "mem_q_norm_g", "mem_k_norm_g", "w_br_gdn", "w_br_sb", "w_br_mem",
           "w_o", "norm2_g", "w_up", "w_down")


def _cp(sem=None):
    return pltpu.CompilerParams(dimension_semantics=sem, vmem_limit_bytes=VMEM_LIMIT)


def _dot(a, b, dims=NN):
    return lax.dot_general(a, b, dims, preferred_element_type=F32)


def _dbf(a, b, dims=NN):
    return _dot(a.astype(BF16), b.astype(BF16), dims)


def _split(a, n):
    parts = []
    for _ in range(n):
        h = a.astype(BF16)
        parts.append(h)
        a = a - h.astype(F32)
    return parts


def _dg(a, b, dims=NN):
    return _dbf(a, b, dims)


def _d3(a, b, dims=NN):
    ah, al = _split(a, 2)
    bh, bl = _split(b, 2)
    return _dot(ah, bh, dims) + (_dot(ah, bl, dims) + _dot(al, bh, dims))


def _dxr(a, e, dims=NN):
    eb = e.astype(BF16)
    a1, a2, a3 = _split(a, 3)
    return _dot(a1, eb, dims) + (_dot(a2, eb, dims) + _dot(a3, eb, dims))


def _dxl(e, a, dims=NN):
    eb = e.astype(BF16)
    a1, a2, a3 = _split(a, 3)
    return _dot(eb, a1, dims) + (_dot(eb, a2, dims) + _dot(eb, a3, dims))


def _sigmoid(x):
    return 1.0 / (1.0 + jnp.exp(-x))


def _softplus(x):
    return jnp.maximum(x, 0.0) + jnp.log(1.0 + jnp.exp(-jnp.abs(x)))


def _rms(x, g):
    r = lax.rsqrt(jnp.mean(x * x, axis=-1, keepdims=True) + EPS)
    return x * r * g, r


def _rms_bwd(dy, x, g, r):
    dyg = dy * g
    dx = r * (dyg - x * (r * r) * jnp.mean(dyg * x, axis=-1, keepdims=True))
    dg = jnp.sum(dy * (x * r), axis=0, keepdims=True)
    return dx, dg


def _hs(h):
    return slice(h * DH, (h + 1) * DH)


def _row_tile(s):
    return 512 if s >= 2048 else 256


def _narrow_tile(s):
    return min(256, s)


def _mm(name, a, b, mode, tm, tn, tk, pro=None, pro_g=None, epi=None, epi_x=None, out_dtype=F32, n_outer=False):
    if mode == "tn":
        K, M = a.shape
    else:
        M, K = a.shape
    N = b.shape[0] if mode == "nt" else b.shape[1]
    tm, tn, tk = min(tm, M), min(tn, N), min(tk, K)
    nk = K // tk
    assert M % tm == 0 and N % tn == 0 and K % tk == 0, (name, M, N, K, tm, tn, tk)
    dims = {"nn": NN, "nt": NT, "tn": TN}[mode]

    def body(*refs):
        a_ref, b_ref = refs[0], refs[1]
        pos = 2
        g_ref = e_ref = None
        if pro == "rms":
            g_ref = refs[pos]
            pos += 1
        if epi is not None:
            e_ref = refs[pos]
            pos += 1
        o_ref = refs[pos]
        av = a_ref[...]
        if pro == "rms":
            av, _ = _rms(av.astype(F32), g_ref[...])
        elif pro == "relu2":
            av = jnp.square(jnp.maximum(av, 0.0))
        part = _dbf(av, b_ref[...], dims)

        def finish(acc):
            if epi == "add":
                acc = acc + e_ref[...]
            elif epi == "drelu2":
                acc = acc * (2.0 * jnp.maximum(e_ref[...], 0.0))
            o_ref[...] = acc.astype(out_dtype)

        if nk == 1:
            finish(part)
        else:
            acc_ref = refs[pos + 1]
            k = pl.program_id(2)

            @pl.when(k == 0)
            def _():
                acc_ref[...] = part

            @pl.when(k > 0)
            def _():
                acc_ref[...] += part

            @pl.when(k == nk - 1)
            def _():
                finish(acc_ref[...])

    def spec(shape, index):
        if n_outer:
            return pl.BlockSpec(shape, lambda j, i, k: index(i, j, k))
        return pl.BlockSpec(shape, index)

    if mode == "tn":
        a_spec = spec((tk, tm), lambda i, j, k: (k, i))
    else:
        a_spec = spec((tm, tk), lambda i, j, k: (i, k))
    if mode == "nt":
        b_spec = spec((tn, tk), lambda i, j, k: (j, k))
    else:
        b_spec = spec((tk, tn), lambda i, j, k: (k, j))
    in_specs, ops = [a_spec, b_spec], [a, b]
    if pro == "rms":
        w = pro_g.shape[1]
        assert (tm if mode == "tn" else tk) == w, name
        in_specs.append(spec((1, w), lambda i, j, k: (0, 0)))
        ops.append(pro_g)
    if epi is not None:
        in_specs.append(spec((tm, tn), lambda i, j, k: (i, j)))
        ops.append(epi_x)
    grid = (N // tn, M // tm, nk) if n_outer else (M // tm, N // tn, nk)
    return pl.pallas_call(
        body, name=name, grid=grid,
        in_specs=in_specs, out_specs=spec((tm, tn), lambda i, j, k: (i, j)),
        out_shape=jax.ShapeDtypeStruct((M, N), out_dtype),
        scratch_shapes=[pltpu.VMEM((tm, tn), F32)] if nk > 1 else [],
        compiler_params=_cp(("parallel", "parallel", "arbitrary")),
    )(*ops)


def _head_select(first_lane):
    l = lax.broadcasted_iota(jnp.int32, (LANES, HW), 0)
    c = lax.broadcasted_iota(jnp.int32, (LANES, HW), 1)
    return (l == first_lane + c // DH).astype(F32)


def _conv_taps(buf, cw, ts):
    c = cw[3:4, :] * buf[8:8 + ts, :]
    for j in range(3):
        k = 3 - j
        c = c + cw[j:j + 1, :] * buf[8 - k:8 - k + ts, :]
    return c


def _pre_fwd(proj, conv_w, alog_f, dtb_f, gsq, gsk, gmq, S):
    ts = _narrow_tile(S)
    hb = ts // 8

    def body(qkv_ref, halo_ref, ab_ref, sq_ref, sk_ref, sv_ref, mq_ref, cw_ref, al_ref, dt_ref, gsq_ref, gsk_ref,
             gmq_ref, gq_o, gk_o, gv_o, gf_o, bf_o, sqn_o, skn_o, svb_o, qmn_o, buf):
        i = pl.program_id(0)
        buf[0:8, :] = jnp.where(i == 0, 0.0, halo_ref[...])
        buf[8:8 + ts, :] = qkv_ref[...]
        c = _conv_taps(buf, cw_ref[...], ts)
        a = c * _sigmoid(c)
        for h in range(NH):
            q = a[:, h * DH:(h + 1) * DH]
            k = a[:, HW + h * DH:HW + (h + 1) * DH]
            gq_o[:, _hs(h)] = q * (lax.rsqrt(jnp.sum(q * q, axis=-1, keepdims=True) + EPS) * DH ** -0.5)
            gk_o[:, _hs(h)] = k * lax.rsqrt(jnp.sum(k * k, axis=-1, keepdims=True) + EPS)
            sqn_o[:, _hs(h)] = _rms(sq_ref[:, _hs(h)], gsq_ref[...])[0].astype(BF16)
            skn_o[:, _hs(h)] = _rms(sk_ref[:, _hs(h)], gsk_ref[...])[0].astype(BF16)
            qmn_o[:, _hs(h)] = _rms(mq_ref[:, _hs(h)], gmq_ref[...])[0].astype(BF16)
        gv_o[...] = a[:, 2 * HW:3 * HW]
        svb_o[...] = sv_ref[...].astype(BF16)
        ab = ab_ref[:, 0:LANES]
        a_bc = _dxr(ab, _head_select(0))
        b_bc = _dxr(ab, _head_select(NH))
        gf_o[...] = -jnp.exp(al_ref[...]) * _softplus(a_bc + dt_ref[...])
        bf_o[...] = _sigmoid(b_bc)

    row = lambda cb: pl.BlockSpec((ts, HW), lambda i: (i, cb))
    full = lambda r, c: pl.BlockSpec((r, c), lambda i: (0, 0))
    f32o = jax.ShapeDtypeStruct((S, HW), F32)
    bfo = jax.ShapeDtypeStruct((S, HW), BF16)
    return pl.pallas_call(
        body, name="pre_fwd", grid=(S // ts,),
        in_specs=[pl.BlockSpec((ts, 3 * HW), lambda i: (i, 0)),
                  pl.BlockSpec((8, 3 * HW), lambda i: (jnp.maximum(i * hb - 1, 0), 0)),
                  row(CB_AB), row(CB_SQ), row(CB_SK), row(CB_SV), row(CB_MQ),
                  full(4, 3 * HW), full(1, HW), full(1, HW), full(1, DH), full(1, DH), full(1, DH)],
        out_specs=[pl.BlockSpec((ts, HW), lambda i: (i, 0))] * 9,
        out_shape=[f32o, f32o, f32o, f32o, f32o, bfo, bfo, bfo, bfo],
        scratch_shapes=[pltpu.VMEM((ts + 8, 3 * HW), F32)],
        compiler_params=_cp(("parallel",)),
    )(proj, proj, proj, proj, proj, proj, proj, conv_w, alog_f, dtb_f, gsq, gsk, gmq)


def _pre_bwd(proj, conv_w, alog_f, dtb_f, gsq, gsk, dgq, dgk, dgv, dgf, dbf, dsqn, dskn, S):
    ts = _narrow_tile(S)
    hb = ts // 8

    def body(qkv_ref, halo_ref, ab_ref, sq_ref, sk_ref, cw_ref, al_ref, dt_ref, gsq_ref, gsk_ref,
             dgq_ref, dgk_ref, dgv_ref, dgf_ref, dbf_ref, dsqn_ref, dskn_ref,
             dc_o, dab_o, dsq_o, dsk_o, dcw_o, dal_o, ddt_o, dgsq_o, dgsk_o, buf):
        i = pl.program_id(0)

        @pl.when(i == 0)
        def _():
            dcw_o[...] = jnp.zeros_like(dcw_o)
            dal_o[...] = jnp.zeros_like(dal_o)
            ddt_o[...] = jnp.zeros_like(ddt_o)
            dgsq_o[...] = jnp.zeros_like(dgsq_o)
            dgsk_o[...] = jnp.zeros_like(dgsk_o)

        buf[0:8, :] = jnp.where(i == 0, 0.0, halo_ref[...])
        buf[8:8 + ts, :] = qkv_ref[...]
        c = _conv_taps(buf, cw_ref[...], ts)
        sg = _sigmoid(c)
        a = c * sg
        dsilu = sg * (1.0 + c * (1.0 - sg))
        dgsq = jnp.zeros((1, DH), F32)
        dgsk = jnp.zeros((1, DH), F32)
        for h in range(NH):
            q = a[:, h * DH:(h + 1) * DH]
            k = a[:, HW + h * DH:HW + (h + 1) * DH]
            nq = lax.rsqrt(jnp.sum(q * q, axis=-1, keepdims=True) + EPS)
            nk = lax.rsqrt(jnp.sum(k * k, axis=-1, keepdims=True) + EPS)
            dyq = dgq_ref[:, _hs(h)]
            dyk = dgk_ref[:, _hs(h)]
            dq = (nq * dyq - q * (nq * nq * nq) * jnp.sum(dyq * q, axis=-1, keepdims=True)) * DH ** -0.5
            dk = nk * dyk - k * (nk * nk * nk) * jnp.sum(dyk * k, axis=-1, keepdims=True)
            dc_o[:, h * DH:(h + 1) * DH] = dq * dsilu[:, h * DH:(h + 1) * DH]
            dc_o[:, HW + h * DH:HW + (h + 1) * DH] = dk * dsilu[:, HW + h * DH:HW + (h + 1) * DH]
            x = sq_ref[:, _hs(h)]
            _, r = _rms(x, gsq_ref[...])
            dx, dg = _rms_bwd(dsqn_ref[:, _hs(h)], x, gsq_ref[...], r)
            dsq_o[:, _hs(h)] = dx.astype(BF16)
            dgsq = dgsq + dg
            x = sk_ref[:, _hs(h)]
            _, r = _rms(x, gsk_ref[...])
            dx, dg = _rms_bwd(dskn_ref[:, _hs(h)], x, gsk_ref[...], r)
            dsk_o[:, _hs(h)] = dx.astype(BF16)
            dgsk = dgsk + dg
        dc_o[:, 2 * HW:3 * HW] = dgv_ref[...] * dsilu[:, 2 * HW:3 * HW]
        dgsq_o[...] += dgsq
        dgsk_o[...] += dgsk
        dc = dc_o[...]
        for j in range(4):
            k = 3 - j
            dcw_o[j:j + 1, :] += jnp.sum(dc * buf[8 - k:8 - k + ts, :], axis=0, keepdims=True)
        ab = ab_ref[:, 0:LANES]
        a_bc = _dxr(ab, _head_select(0))
        b_bc = _dxr(ab, _head_select(NH))
        pre = a_bc + dt_ref[...]
        ea = jnp.exp(al_ref[...])
        dgf = dgf_ref[...]
        dal_o[...] += jnp.sum(dgf * (-ea * _softplus(pre)), axis=0, keepdims=True)
        da = dgf * (-ea * _sigmoid(pre))
        ddt_o[...] += jnp.sum(da, axis=0, keepdims=True)
        beta = _sigmoid(b_bc)
        db = dbf_ref[...] * beta * (1.0 - beta)
        lane = lax.broadcasted_iota(jnp.int32, (ts, LANES), 1)
        dab = jnp.zeros((ts, LANES), F32)
        for h in range(NH):
            dab = dab + jnp.where(lane == h, da[:, _hs(h)], 0.0) + jnp.where(lane == NH + h, db[:, _hs(h)], 0.0)
        dab_o[:, 0:LANES] = dab.astype(BF16)
        dab_o[:, LANES:HW] = jnp.zeros((ts, HW - LANES), BF16)

    row = lambda cb: pl.BlockSpec((ts, HW), lambda i: (i, cb))
    full = lambda r, c: pl.BlockSpec((r, c), lambda i: (0, 0))
    t512 = pl.BlockSpec((ts, HW), lambda i: (i, 0))
    return pl.pallas_call(
        body, name="pre_bwd", grid=(S // ts,),
        in_specs=[pl.BlockSpec((ts, 3 * HW), lambda i: (i, 0)),
                  pl.BlockSpec((8, 3 * HW), lambda i: (jnp.maximum(i * hb - 1, 0), 0)),
                  row(CB_AB), row(CB_SQ), row(CB_SK),
                  full(4, 3 * HW), full(1, HW), full(1, HW), full(1, DH), full(1, DH)] + [t512] * 7,
        out_specs=[pl.BlockSpec((ts, 3 * HW), lambda i: (i, 0)), t512, t512, t512,
                   full(4, 3 * HW), full(1, HW), full(1, HW), full(1, DH), full(1, DH)],
        out_shape=[jax.ShapeDtypeStruct((S, 3 * HW), F32)] + [jax.ShapeDtypeStruct((S, HW), BF16)] * 3
        + [jax.ShapeDtypeStruct((4, 3 * HW), F32), jax.ShapeDtypeStruct((1, HW), F32),
           jax.ShapeDtypeStruct((1, HW), F32), jax.ShapeDtypeStruct((1, DH), F32),
           jax.ShapeDtypeStruct((1, DH), F32)],
        scratch_shapes=[pltpu.VMEM((ts + 8, 3 * HW), F32)],
        compiler_params=_cp(("arbitrary",)),
    )(proj, proj, proj, proj, proj, conv_w, alog_f, dtb_f, gsq, gsk, dgq, dgk, dgv, dgf, dbf, dsqn, dskn)


def _conv_bwd(dc, conv_w, S):
    ts = _row_tile(S)
    hb = ts // 8
    n = S // ts

    def body(dc_ref, halo_ref, cw_ref, o_ref, buf):
        i = pl.program_id(0)
        buf[0:ts, :] = dc_ref[...]
        buf[ts:ts + 8, :] = jnp.where(i == n - 1, 0.0, halo_ref[...])
        cw = cw_ref[...]
        acc = cw[3:4, :] * buf[0:ts, :]
        for k in range(1, 4):
            acc = acc + cw[3 - k:4 - k, :] * buf[k:k + ts, :]
        o_ref[...] = acc.astype(BF16)

    return pl.pallas_call(
        body, name="conv_bwd", grid=(n,),
        in_specs=[pl.BlockSpec((ts, 3 * HW), lambda i: (i, 0)),
                  pl.BlockSpec((8, 3 * HW), lambda i: (jnp.minimum((i + 1) * hb, S // 8 - 1), 0)),
                  pl.BlockSpec((4, 3 * HW), lambda i: (0, 0))],
        out_specs=pl.BlockSpec((ts, 3 * HW), lambda i: (i, 0)),
        out_shape=jax.ShapeDtypeStruct((S, 3 * HW), BF16),
        scratch_shapes=[pltpu.VMEM((ts + 8, 3 * HW), F32)],
        compiler_params=_cp(("parallel",)),
    )(dc, dc, conv_w)


def _gdn_masks():
    r = lax.broadcasted_iota(jnp.int32, (PAIR, PAIR), 0)
    c = lax.broadcasted_iota(jnp.int32, (PAIR, PAIR), 1)
    same = ((r >= CHUNK) & (c >= CHUNK)) | ((r < CHUNK) & (c < CHUNK))
    return dict(r=r, same=same, tril=same & (r >= c), strict=same & (r > c), triu=same & (c >= r), eye=r == c,
                in_a=r < CHUNK, last_a=r == CHUNK - 1, last_b=r == PAIR - 1)


def _each(fn, *cols):
    return [fn(*xs) for xs in zip(*cols)]


def _mul(a, b):
    return a * b


def _top(x):
    return x[:CHUNK]


def _bot(x):
    return x[CHUNK:]


def _rows(a, b):
    return jnp.concatenate([a, b], axis=0)


def _tri_inv(lm, eye):
    eye_f = eye.astype(F32)
    p = _each(lambda l: eye_f - l, lm)
    lp = _each(lambda l: _d3(l, l), lm)
    for it in range(5):
        p = _each(lambda a, b: a + _d3(a, b), p, lp)
        if it < 4:
            lp = _each(lambda b: _d3(b, b), lp)
    return p


def _gdn_block(m, q, k, v, g, beta):
    tril_f = m["tril"].astype(F32)
    col_sum = lambda mask: (lambda x: jnp.sum(jnp.where(mask, x, 0.0), axis=0, keepdims=True))
    gc = _each(lambda x: _dxl(tril_f, x), g)
    gcr = _each(col_sum(m["eye"]), gc)
    gam = _each(lambda a, b: jnp.where(m["tril"], jnp.exp(jnp.minimum(a - b, 0.0)), 0.0), gc, gcr)
    kb = _each(_mul, k, beta)
    vb = _each(_mul, v, beta)
    lm = _each(lambda a, b, c: jnp.where(m["strict"], _dg(a, b, NT) * c, 0.0), kb, k, gam)
    t = _tri_inv(lm, m["eye"])
    eg = _each(jnp.exp, gc)
    kbe = _each(_mul, kb, eg)
    u = _each(_d3, t, vb)
    w = _each(_d3, t, kbe)
    aqk = _each(lambda a, b, c: jnp.where(m["tril"], _dg(a, b, NT) * c, 0.0), q, k, gam)
    qd = _each(_mul, q, eg)
    ga = _each(col_sum(m["last_a"]), gc)
    gb = _each(col_sum(m["last_b"]), gc)
    e2 = _each(lambda a, b, c: jnp.exp(jnp.where(m["in_a"], a, b) - c), ga, gb, gc)
    kd = _each(_mul, k, e2)
    return dict(u=u, w=w, aqk=aqk, qd=qd, kd=kd, gam=gam, kb=kb, vb=vb, lm=lm, t=t, eg=eg, kbe=kbe, e2=e2,
                gla=_each(jnp.exp, ga), glb=_each(jnp.exp, gb))


def _gdn_fwd(gq, gk, gv, gf, bf, S):
    nb = S // PAIR

    def body(q_ref, k_ref, v_ref, g_ref, b_ref, o_ref, st_ref, s_scr):
        @pl.when(pl.program_id(0) == 0)
        def _():
            s_scr[...] = jnp.zeros_like(s_scr)

        m = _gdn_masks()
        heads = lambda ref: [ref[:, _hs(h)] for h in range(NH)]
        f = _gdn_block(m, heads(q_ref), heads(k_ref), heads(v_ref), heads(g_ref), heads(b_ref))
        u, w, qd, kd = f["u"], f["w"], f["qd"], f["kd"]
        s0 = [s_scr[h * DH:(h + 1) * DH, :] for h in range(NH)]
        vna = _each(lambda a, b, s: _top(a) - _dg(_top(b), s), u, w, s0)
        oa = _each(lambda a, s: _dg(_top(a), s), qd, s0)
        s1 = _each(lambda s, gl, a, vn: s * gl + _dg(_top(a), vn, TN), s0, f["gla"], kd, vna)
        vnb = _each(lambda a, b, s: _bot(a) - _dg(_bot(b), s), u, w, s1)
        ob = _each(lambda a, s: _dg(_bot(a), s), qd, s1)
        s2 = _each(lambda s, gl, a, vn: s * gl + _dg(_bot(a), vn, TN), s1, f["glb"], kd, vnb)
        outs = _each(lambda a, b, c, va, vb: _rows(a, b) + _dg(c, _rows(va, vb)), oa, ob, f["aqk"], vna, vnb)
        o_ref[...] = jnp.concatenate(outs, axis=1)
        st_ref[...] = jnp.concatenate(s0 + s1, axis=0)
        s_scr[...] = jnp.concatenate(s2, axis=0)

    blk = pl.BlockSpec((PAIR, HW), lambda i: (i, 0))
    return pl.pallas_call(
        body, name="gdn_fwd", grid=(nb,),
        in_specs=[blk] * 5,
        out_specs=[blk, pl.BlockSpec((2 * NH * DH, DH), lambda i: (i, 0))],
        out_shape=[jax.ShapeDtypeStruct((S, HW), F32), jax.ShapeDtypeStruct((nb * 2 * NH * DH, DH), F32)],
        scratch_shapes=[pltpu.VMEM((NH * DH, DH), F32)],
        compiler_params=_cp(("arbitrary",)),
    )(gq, gk, gv, gf, bf)


def _gdn_bwd(gq, gk, gv, gf, bf, states, do, S):
    nb = S // PAIR

    def body(q_ref, k_ref, v_ref, g_ref, b_ref, st_ref, do_ref, dq_o, dk_o, dv_o, dg_o, db_o, ds_scr):
        @pl.when(pl.program_id(0) == 0)
        def _():
            ds_scr[...] = jnp.zeros_like(ds_scr)

        m = _gdn_masks()
        ones = jnp.ones((PAIR, PAIR), F32)
        heads = lambda ref: [ref[:, _hs(h)] for h in range(NH)]
        q, k, v, beta, do = heads(q_ref), heads(k_ref), heads(v_ref), heads(b_ref), heads(do_ref)
        f = _gdn_block(m, q, k, v, heads(g_ref), beta)
        u, w, aqk, qd, kd, t = f["u"], f["w"], f["aqk"], f["qd"], f["kd"], f["t"]
        s0 = [st_ref[h * DH:(h + 1) * DH, :] for h in range(NH)]
        s1 = [st_ref[(NH + h) * DH:(NH + h + 1) * DH, :] for h in range(NH)]
        ds2 = [ds_scr[h * DH:(h + 1) * DH, :] for h in range(NH)]
        total = lambda a, b: jnp.sum(jnp.sum(a * b, axis=1, keepdims=True), axis=0, keepdims=True)
        vna = _each(lambda a, b, s: _top(a) - _dg(_top(b), s), u, w, s0)
        vnb = _each(lambda a, b, s: _bot(a) - _dg(_bot(b), s), u, w, s1)
        dvn_i = _each(lambda a, b: _dg(a, b, TN), aqk, do)
        dvnb = _each(lambda a, b, s: _bot(a) + _dg(_bot(b), s), dvn_i, kd, ds2)
        dqdb = _each(lambda a, s: _dg(_bot(a), s, NT), do, s1)
        dkdb = _each(lambda a, s: _dg(a, s, NT), vnb, ds2)
        dglb = _each(total, ds2, s1)
        dwb = _each(lambda a, s: -_dg(a, s, NT), dvnb, s1)
        ds1 = _each(lambda s, gl, a, b, c, d: s * gl + _dg(_bot(a), _bot(b), TN) - _dg(_bot(c), d, TN),
                    ds2, f["glb"], qd, do, w, dvnb)
        dvna = _each(lambda a, b, s: _top(a) + _dg(_top(b), s), dvn_i, kd, ds1)
        dqda = _each(lambda a, s: _dg(_top(a), s, NT), do, s0)
        dkda = _each(lambda a, s: _dg(a, s, NT), vna, ds1)
        dgla = _each(total, ds1, s0)
        dwa = _each(lambda a, s: -_dg(a, s, NT), dvna, s0)
        ds0 = _each(lambda s, gl, a, b, c, d: s * gl + _dg(_top(a), _top(b), TN) - _dg(_top(c), d, TN),
                    ds1, f["gla"], qd, do, w, dvna)
        dvn, dqd, dkd, dw = (_each(_rows, a, b) for a, b in ((dvna, dvnb), (dqda, dqdb), (dkda, dkdb), (dwa, dwb)))
        daqk = _each(lambda a, va, vb: jnp.where(m["tril"], _dg(a, _rows(va, vb), NT), 0.0), do, vna, vnb)
        dt = _each(lambda a, b, c, d: _dg(a, b, NT) + _dg(c, d, NT), dvn, f["vb"], dw, f["kbe"])
        dvb = _each(lambda a, b: _dg(a, b, TN), t, dvn)
        dkbe = _each(lambda a, b: _dg(a, b, TN), t, dw)
        dtt = _each(lambda a, b: _d3(a, b, NT), dt, t)
        dl = _each(lambda a, b: -jnp.where(m["strict"], _d3(a, b, TN), 0.0), t, dtt)
        dm = _each(_mul, dl, f["gam"])
        dn = _each(_mul, daqk, f["gam"])
        dkb = _each(lambda a, b, c, d: _dg(a, b) + c * d, dm, k, dkbe, f["eg"])
        dks = _each(lambda a, b, c, d, e, g, h, i: _dg(a, b, TN) + _dg(c, d, TN) + e * g + h * i,
                    dm, f["kb"], dn, q, dkd, f["e2"], beta, dkb)
        dqs = _each(lambda a, b, c, d: _dg(a, b) + c * d, dn, k, dqd, f["eg"])
        gm = _each(lambda a, b, c, d: a * b + c * d, dl, f["lm"], daqk, aqk)
        dkdkd = _each(_mul, dkd, kd)
        dgc = _each(lambda a, b, c, d, e, g: _dxr(a + b * c + d * e - g, ones) - _dxr(a, ones, TN),
                    gm, dqd, qd, dkbe, f["kbe"], dkdkd)
        same_f = m["same"].astype(F32)
        chunk_tot = _each(lambda a: _dxl(same_f, _dxr(a, ones)), dkdkd)
        last = m["last_a"] | m["last_b"]
        dgc = _each(lambda a, b, ga, gla, gb, glb: a + jnp.where(last, b + jnp.where(m["in_a"], ga * gla, gb * glb), 0.0),
                    dgc, chunk_tot, dgla, f["gla"], dglb, f["glb"])
        dbs = _each(lambda a, b, c, d: _dxr(a * b + c * d, ones), dkb, k, dvb, v)
        dvs = _each(_mul, beta, dvb)
        triu_f = m["triu"].astype(F32)
        dgs = _each(lambda a: _dxl(triu_f, a), dgc)
        for ref, parts in ((dq_o, dqs), (dk_o, dks), (dv_o, dvs), (dg_o, dgs), (db_o, dbs)):
            ref[...] = jnp.concatenate(parts, axis=1)
        ds_scr[...] = jnp.concatenate(ds0, axis=0)

    blk = pl.BlockSpec((PAIR, HW), lambda i: (nb - 1 - i, 0))
    o = jax.ShapeDtypeStruct((S, HW), F32)
    return pl.pallas_call(
        body, name="gdn_bwd", grid=(nb,),
        in_specs=[blk] * 5 + [pl.BlockSpec((2 * NH * DH, DH), lambda i: (nb - 1 - i, 0)), blk],
        out_specs=[blk] * 5, out_shape=[o] * 5,
        scratch_shapes=[pltpu.VMEM((NH * DH, DH), F32)],
        compiler_params=_cp(("arbitrary",)),
    )(gq, gk, gv, gf, bf, states, do)


SB_T = 256
SB_GROUP = 4
SB_GROUP_BWD = 4
SB_SINGLES = 2
SB_DEAD = -110.0


def _group_sizes(g):
    sizes = []
    while g >= 1:
        sizes.append(g)
        g //= 2
    return sizes


def _sb_iotas(t):
    return lax.broadcasted_iota(jnp.int32, (t, t), 0), lax.broadcasted_iota(jnp.int32, (t, t), 1)


def _sb_scores(q, k, mask):
    z = _dot(q, k, NT) * DH ** -0.5
    ls = jnp.minimum(z, 0.0) - jnp.log(1.0 + jnp.exp(-jnp.abs(z)))
    lneg = ls - z
    if mask is not None:
        lneg = jnp.where(mask, lneg, 0.0)
    return ls, lneg


def _prefix(x, u):
    xh, xl = _split(x, 2)
    return _dot(xh, u) + _dot(xl, u)


def _sb_fwd(sqn, skn, svb, S):
    t = min(SB_T, S)

    def body(q_ref, k_ref, v_ref, o_ref, t_ref, cnt_ref):
        qb = pl.program_id(1)
        q = q_ref[...]
        r, c = _sb_iotas(t)
        diag = c < r
        u_after = (r > c).astype(BF16)

        def tiles(k0s, run, mask):
            sc = _each(lambda k0: _sb_scores(q, k_ref[pl.ds(k0, t), :], mask), k0s)
            ls, lneg = [s[0] for s in sc], [s[1] for s in sc]
            sums = _each(lambda x: jnp.sum(x, axis=1, keepdims=True), lneg)
            pre = _each(lambda x: _prefix(x, u_after), lneg)
            runs = [run]
            for s in sums:
                runs.append(runs[-1] + s)
            att = _each(lambda a, b, rn: jnp.exp(a + (rn + b)), ls, pre, runs[:-1])
            if mask is not None:
                att = _each(lambda a: jnp.where(mask, a, 0.0), att)
            parts = _each(lambda a, k0: _dot(a.astype(BF16), v_ref[pl.ds(k0, t), :]), att, k0s)
            return sum(parts[1:], parts[0]), runs[-1]

        acc, run = tiles([pl.multiple_of(qb * t, t)], jnp.zeros((t, 1), F32), diag)

        def alive(run):
            return jnp.max(run) >= SB_DEAD

        carry, done = (0, acc, run, alive(run)), 0
        for size, limit in [(1, SB_SINGLES)] + [(s, None) for s in _group_sizes(SB_GROUP)]:

            def more(c, size=size, done=done, limit=limit):
                i, _, _, go = c
                fits = done + (i + 1) * size <= qb
                return (fits if limit is None else fits & (i < limit)) & go

            def group(c, size=size, done=done):
                i, acc, run, _ = c
                first = qb - 1 - done - size * i
                part, run = tiles([pl.multiple_of((first - j) * t, t) for j in range(size)], run, None)
                return i + 1, acc + part, run, alive(run)

            n, acc, run, go = lax.while_loop(more, group, (0,) + carry[1:])
            carry, done = (0, acc, run, go), done + n * size
        o_ref[...] = acc.astype(BF16)
        t_ref[...] = jnp.broadcast_to(run, (t, DH))
        cnt_ref[pl.program_id(0), qb] = done

    qspec = pl.BlockSpec((t, DH), lambda h, i: (i, h))
    kspec = pl.BlockSpec((S, DH), lambda h, i: (0, h))
    return pl.pallas_call(
        body, name="sb_fwd", grid=(NH, S // t),
        in_specs=[qspec, kspec, kspec],
        out_specs=[qspec, qspec, pl.BlockSpec(memory_space=pltpu.SMEM)],
        out_shape=[jax.ShapeDtypeStruct((S, HW), BF16), jax.ShapeDtypeStruct((S, HW), F32),
                   jax.ShapeDtypeStruct((NH, S // t), jnp.int32)],
        compiler_params=_cp(("arbitrary", "arbitrary")),
    )(sqn, skn, svb)


def _sb_bwd(sqn, skn, svb, do, tot, walked, S):
    t = min(SB_T, S)

    def body(cnt_ref, q_ref, k_ref, v_ref, do_ref, t_ref, dq_o, dk_o, dv_o, dv_acc):
        qb = pl.program_id(1)

        @pl.when(qb == 0)
        def _():
            dk_o[...] = jnp.zeros_like(dk_o)
            dv_acc[...] = jnp.zeros_like(dv_acc)

        q = q_ref[...]
        do = do_ref[...].astype(BF16)
        tot_l = jnp.concatenate([t_ref[...]] * (t // DH), axis=1)
        r, c = _sb_iotas(t)
        diag = c < r
        u_upto = (r <= c).astype(BF16)
        u_before = (r < c).astype(BF16)

        def tiles(k0s, run_l, run_e, mask):
            rowsum = lambda x: jnp.sum(x, axis=1, keepdims=True)
            ks = [k_ref[pl.ds(k0, t), :] for k0 in k0s]
            vs = [v_ref[pl.ds(k0, t), :] for k0 in k0s]
            sc = _each(lambda k: _sb_scores(q, k, mask), ks)
            ls, lneg = [s[0] for s in sc], [s[1] for s in sc]
            sums_l = _each(rowsum, lneg)
            pre_l = _each(lambda x: _prefix(x, u_upto), lneg)
            runs_l = [run_l]
            for s in sums_l:
                runs_l.append(runs_l[-1] + s)
            att = _each(lambda a, b, rn: jnp.exp(a + (tot_l - (rn + b))), ls, pre_l, runs_l[:-1])
            if mask is not None:
                att = _each(lambda a: jnp.where(mask, a, 0.0), att)
            e = _each(lambda v, a: _dot(do, v, NT) * a, vs, att)
            sums_e = _each(rowsum, e)
            pre_e = _each(lambda x: _prefix(x, u_before), e)
            runs_e = [run_e]
            for s in sums_e:
                runs_e.append(runs_e[-1] + s)
            sg = _each(jnp.exp, ls)
            dz = _each(lambda a, b, rn, s: a * (1.0 - s) - (rn + b) * s, e, pre_e, runs_e[:-1], sg)
            if mask is not None:
                dz = _each(lambda a: jnp.where(mask, a, 0.0), dz)
            dz = _each(lambda a: (a * DH ** -0.5).astype(BF16), dz)
            dvs = _each(lambda a: _dot(a.astype(BF16), do, TN), att)
            dks = _each(lambda a: _dot(a, q, TN), dz)
            dqs = _each(_dot, dz, ks)
            for k0, dv, dk in zip(k0s, dvs, dks):
                dv_acc[pl.ds(k0, t), :] += dv
                dk_o[pl.ds(k0, t), :] += dk
            return sum(dqs[1:], dqs[0]), runs_l[-1], runs_e[-1]

        walked = cnt_ref[pl.program_id(0), qb]
        z1 = jnp.zeros((t, 1), F32)
        carry, done = (jnp.zeros((t, DH), F32), z1, z1), 0
        for size in _group_sizes(SB_GROUP_BWD):
            n = (walked - done) // size

            def group(i, carry, size=size, done=done):
                dq, run_l, run_e = carry
                first = qb - walked + done + size * i
                part, run_l, run_e = tiles([pl.multiple_of((first + j) * t, t) for j in range(size)], run_l, run_e,
                                           None)
                return dq + part, run_l, run_e

            carry = lax.fori_loop(0, n, group, carry)
            done = done + n * size
        dq, run_l, run_e = carry
        part, _, _ = tiles([pl.multiple_of(qb * t, t)], run_l, run_e, diag)
        dq_o[...] = dq + part

        @pl.when(qb == S // t - 1)
        def _():
            dv_o[...] = dv_acc[...].astype(BF16)

    qspec = pl.BlockSpec((t, DH), lambda h, i, cnt: (i, h))
    kspec = pl.BlockSpec((S, DH), lambda h, i, cnt: (0, h))
    o = jax.ShapeDtypeStruct((S, HW), F32)
    return pl.pallas_call(
        body, name="sb_bwd",
        grid_spec=pltpu.PrefetchScalarGridSpec(
            num_scalar_prefetch=1, grid=(NH, S // t),
            in_specs=[qspec, kspec, kspec, qspec, qspec], out_specs=[qspec, kspec, kspec],
            scratch_shapes=[pltpu.VMEM((S, DH), F32)]),
        out_shape=[o, o, jax.ShapeDtypeStruct((S, HW), BF16)],
        compiler_params=_cp(("parallel", "arbitrary")),
    )(walked, sqn, skn, svb, do, tot)


def _mem_probs(qn, kn):
    s = _dot(qn, kn.astype(BF16), NT) * DH ** -0.5
    p = jnp.exp(s - jnp.max(s, axis=-1, keepdims=True))
    return p / jnp.sum(p, axis=-1, keepdims=True)


def _mem_fwd(qmn, kv, gmk, S):
    ts = _row_tile(S)

    def body(q_ref, kv_ref, gk_ref, o_ref):
        for h in range(NH):
            kn, _ = _rms(kv_ref[:, _hs(h)], gk_ref[...])
            p = _mem_probs(q_ref[:, _hs(h)], kn)
            o_ref[:, _hs(h)] = _dbf(p, kv_ref[:, HW + h * DH:HW + (h + 1) * DH]).astype(BF16)

    return pl.pallas_call(
        body, name="mem_fwd", grid=(S // ts,),
        in_specs=[pl.BlockSpec((ts, HW), lambda i: (i, 0)), pl.BlockSpec((NMEM, 2 * HW), lambda i: (0, 0)),
                  pl.BlockSpec((1, DH), lambda i: (0, 0))],
        out_specs=pl.BlockSpec((ts, HW), lambda i: (i, 0)),
        out_shape=jax.ShapeDtypeStruct((S, HW), BF16),
        compiler_params=_cp(("parallel",)),
    )(qmn, kv, gmk)


def _mem_bwd(proj, qmn, kv, gmq, gmk, do, S):
    ts = _row_tile(S)
    n = S // ts

    def body(mq_ref, q_ref, kv_ref, gq_ref, gk_ref, do_ref, dmq_o, dkv_o, dgq_o, dgk_o, dkn_scr):
        i = pl.program_id(0)

        @pl.when(i == 0)
        def _():
            dkv_o[...] = jnp.zeros_like(dkv_o)
            dgq_o[...] = jnp.zeros_like(dgq_o)
            dkn_scr[...] = jnp.zeros_like(dkn_scr)

        dgq = jnp.zeros((1, DH), F32)
        for h in range(NH):
            km = kv_ref[:, _hs(h)]
            vm = kv_ref[:, HW + h * DH:HW + (h + 1) * DH].astype(BF16)
            kn, _ = _rms(km, gk_ref[...])
            qn = q_ref[:, _hs(h)]
            p = _mem_probs(qn, kn)
            dob = do_ref[:, _hs(h)].astype(BF16)
            dkv_o[:, HW + h * DH:HW + (h + 1) * DH] += _dot(p.astype(BF16), dob, TN)
            dp = _dot(dob, vm, NT)
            dsc = (p * (dp - jnp.sum(dp * p, axis=-1, keepdims=True)) * DH ** -0.5).astype(BF16)
            dkn_scr[:, _hs(h)] += _dot(dsc, qn, TN)
            x = mq_ref[:, _hs(h)]
            _, r = _rms(x, gq_ref[...])
            dx, dg = _rms_bwd(_dot(dsc, kn.astype(BF16)), x, gq_ref[...], r)
            dmq_o[:, _hs(h)] = dx.astype(BF16)
            dgq = dgq + dg
        dgq_o[...] += dgq

        @pl.when(i == n - 1)
        def _():
            dgk = jnp.zeros((1, DH), F32)
            for h in range(NH):
                km = kv_ref[:, _hs(h)]
                _, r = _rms(km, gk_ref[...])
                dx, dg = _rms_bwd(dkn_scr[:, _hs(h)], km, gk_ref[...], r)
                dkv_o[:, _hs(h)] = dx
                dgk = dgk + dg
            dgk_o[...] = dgk

    full = lambda r, c: pl.BlockSpec((r, c), lambda i: (0, 0))
    t512 = pl.BlockSpec((ts, HW), lambda i: (i, 0))
    return pl.pallas_call(
        body, name="mem_bwd", grid=(n,),
        in_specs=[pl.BlockSpec((ts, HW), lambda i: (i, CB_MQ)), t512, full(NMEM, 2 * HW), full(1, DH), full(1, DH),
                  t512],
        out_specs=[t512, full(NMEM, 2 * HW), full(1, DH), full(1, DH)],
        out_shape=[jax.ShapeDtypeStruct((S, HW), BF16), jax.ShapeDtypeStruct((NMEM, 2 * HW), F32),
                   jax.ShapeDtypeStruct((1, DH), F32), jax.ShapeDtypeStruct((1, DH), F32)],
        scratch_shapes=[pltpu.VMEM((NMEM, HW), F32)],
        compiler_params=_cp(("arbitrary",)),
    )(proj, qmn, kv, gmq, gmk, do)


def _gated_gdn(o, z, g):
    sg = _sigmoid(z)
    outs, rs = [], []
    for h in range(NH):
        y, r = _rms(o[:, _hs(h)], g)
        outs.append(y * (z[:, _hs(h)] * sg[:, _hs(h)]))
        rs.append(r)
    return jnp.concatenate(outs, axis=1), rs, sg


def _merge_fwd(x, proj, ogdn, osb, omem, ggdn, wbg, wbs, wbm, wo, S):
    ts = _narrow_tile(S)

    def body(x_ref, z_ref, g0_ref, g1_ref, g2_ref, og_ref, os_ref, om_ref, gg_ref, wbg_ref, wbs_ref, wbm_ref,
             wo_ref, x1_o, mix_o):
        on, _, _ = _gated_gdn(og_ref[...], z_ref[...], gg_ref[...])
        mix = (_sigmoid(g0_ref[...]) * _dbf(on, wbg_ref[...]) + _sigmoid(g1_ref[...]) * _dbf(os_ref[...], wbs_ref[...])
               + _sigmoid(g2_ref[...]) * _dbf(om_ref[...], wbm_ref[...]))
        mix_o[...] = mix.astype(BF16)
        x1_o[...] = x_ref[...] + _dbf(mix, wo_ref[...])

    t512 = pl.BlockSpec((ts, HW), lambda i: (i, 0))
    t1k = pl.BlockSpec((ts, D), lambda i: (i, 0))
    gate = lambda j: pl.BlockSpec((ts, D), lambda i: (i, 4 + j))
    full = lambda r, c: pl.BlockSpec((r, c), lambda i: (0, 0))
    return pl.pallas_call(
        body, name="merge_fwd", grid=(S // ts,),
        in_specs=[t1k, pl.BlockSpec((ts, HW), lambda i: (i, CB_Z)), gate(0), gate(1), gate(2), t512, t512, t512,
                  full(1, DH), full(HW, D), full(HW, D), full(HW, D), full(D, D)],
        out_specs=[t1k, t1k],
        out_shape=[jax.ShapeDtypeStruct((S, D), F32), jax.ShapeDtypeStruct((S, D), BF16)],
        compiler_params=_cp(("parallel",)),
    )(x, proj, proj, proj, proj, ogdn, osb, omem, ggdn, wbg, wbs, wbm, wo)


def _merge_bwd(dmix, proj, ogdn, osb, omem, ggdn, wbg, wbs, wbm, S):
    ts = _narrow_tile(S)

    def body(dm_ref, z_ref, g0_ref, g1_ref, g2_ref, og_ref, os_ref, om_ref, gg_ref, wbg_ref, wbs_ref, wbm_ref,
             dgl0_o, dgl1_o, dgl2_o, dog_o, dz_o, dos_o, dom_o, dwbg_o, dwbs_o, dwbm_o, dgg_o):
        @pl.when(pl.program_id(0) == 0)
        def _():
            for ref in (dwbg_o, dwbs_o, dwbm_o, dgg_o):
                ref[...] = jnp.zeros_like(ref)

        dm = dm_ref[...]
        og = og_ref[...]
        z = z_ref[...]
        on, rs, sg = _gated_gdn(og, z, gg_ref[...])
        branch = ((on, g0_ref, wbg_ref, dgl0_o, dwbg_o), (os_ref[...], g1_ref, wbs_ref, dgl1_o, dwbs_o),
                  (om_ref[...], g2_ref, wbm_ref, dgl2_o, dwbm_o))
        dos = []
        for o, g_ref, w_ref, dgl_o, dw_o in branch:
            ob = o.astype(BF16)
            gate = _sigmoid(g_ref[...])
            dgl_o[...] = (dm * _dot(ob, w_ref[...]) * gate * (1.0 - gate)).astype(BF16)
            dy = (dm * gate).astype(BF16)
            dw_o[...] += _dot(ob, dy, TN)
            dos.append(_dot(dy, w_ref[...], NT))
        dos_o[...] = dos[1].astype(BF16)
        dom_o[...] = dos[2].astype(BF16)
        don = dos[0]
        dgg = jnp.zeros((1, DH), F32)
        for h in range(NH):
            oh, zh, sh = og[:, _hs(h)], z[:, _hs(h)], sg[:, _hs(h)]
            y = oh * rs[h] * gg_ref[...]
            dz_o[:, _hs(h)] = (don[:, _hs(h)] * y * (sh * (1.0 + zh * (1.0 - sh)))).astype(BF16)
            dx, dg = _rms_bwd(don[:, _hs(h)] * (zh * sh), oh, gg_ref[...], rs[h])
            dog_o[:, _hs(h)] = dx
            dgg = dgg + dg
        dgg_o[...] += dgg

    t512 = pl.BlockSpec((ts, HW), lambda i: (i, 0))
    t1k = pl.BlockSpec((ts, D), lambda i: (i, 0))
    gate = lambda j: pl.BlockSpec((ts, D), lambda i: (i, 4 + j))
    full = lambda r, c: pl.BlockSpec((r, c), lambda i: (0, 0))
    s1k = jax.ShapeDtypeStruct((S, D), BF16)
    s512 = jax.ShapeDtypeStruct((S, HW), BF16)
    wsh = jax.ShapeDtypeStruct((HW, D), F32)
    return pl.pallas_call(
        body, name="merge_bwd", grid=(S // ts,),
        in_specs=[t1k, pl.BlockSpec((ts, HW), lambda i: (i, CB_Z)), gate(0), gate(1), gate(2), t512, t512, t512,
                  full(1, DH), full(HW, D), full(HW, D), full(HW, D)],
        out_specs=[t1k, t1k, t1k, t512, t512, t512, t512, full(HW, D), full(HW, D), full(HW, D), full(1, DH)],
        out_shape=[s1k, s1k, s1k, jax.ShapeDtypeStruct((S, HW), F32), s512, s512, s512, wsh, wsh, wsh,
                   jax.ShapeDtypeStruct((1, DH), F32)],
        compiler_params=_cp(("arbitrary",)),
    )(dmix, proj, proj, proj, proj, ogdn, osb, omem, ggdn, wbg, wbs, wbm)


def _loss_grad(y, target, S):
    ts = _row_tile(S)

    def body(y_ref, t_ref, dy_o, loss_o):
        @pl.when(pl.program_id(0) == 0)
        def _():
            loss_o[...] = jnp.zeros_like(loss_o)

        err = y_ref[...] - t_ref[...]
        dy_o[...] = err * (1.0 / D)
        per_tok = jnp.sum(err * err, axis=1, keepdims=True) * (1.0 / D)
        loss_o[...] += 0.5 * jnp.sum(per_tok, axis=0, keepdims=True)

    t1k = pl.BlockSpec((ts, D), lambda i: (i, 0))
    return pl.pallas_call(
        body, name="loss_grad", grid=(S // ts,), in_specs=[t1k, t1k],
        out_specs=[t1k, pl.BlockSpec((1, 1), lambda i: (0, 0))],
        out_shape=[jax.ShapeDtypeStruct((S, D), F32), jax.ShapeDtypeStruct((1, 1), F32)],
        compiler_params=_cp(("arbitrary",)),
    )(y, target)


def _norm_bwd(name, dh, x, g, res):
    rows = x.shape[0]
    ts = min(_row_tile(rows), rows)

    def body(*refs):
        dh_ref, x_ref, g_ref = refs[:3]
        dx_o, dg_o = refs[-2:]

        @pl.when(pl.program_id(0) == 0)
        def _():
            dg_o[...] = jnp.zeros_like(dg_o)

        xv = x_ref[...]
        _, r = _rms(xv, g_ref[...])
        dx, dg = _rms_bwd(dh_ref[...], xv, g_ref[...], r)
        dx_o[...] = dx if res is None else dx + refs[3][...]
        dg_o[...] += dg

    t1k = pl.BlockSpec((ts, D), lambda i: (i, 0))
    gsp = pl.BlockSpec((1, D), lambda i: (0, 0))
    ops = [dh, x, g] + ([] if res is None else [res])
    return pl.pallas_call(
        body, name=name, grid=(rows // ts,), in_specs=[t1k, t1k, gsp] + ([] if res is None else [t1k]),
        out_specs=[t1k, gsp],
        out_shape=[jax.ShapeDtypeStruct((rows, D), F32), jax.ShapeDtypeStruct((1, D), F32)],
        compiler_params=_cp(("arbitrary",)),
    )(*ops)


def _adamw(name, gall, w, m, v):
    rows = w.shape[0]
    nsrc = gall.shape[0]
    tr = min(SLAB_TILE, rows)
    assert rows % tr == 0

    def body(g_ref, w_ref, m_ref, v_ref, g_o, d_o, m_o, v_o):
        g = g_ref[0].astype(F32)
        for j in range(1, nsrc):
            g = g + g_ref[j].astype(F32)
        m_new = ADAM_B1 * m_ref[...] + (1.0 - ADAM_B1) * g
        v_new = ADAM_B2 * v_ref[...] + (1.0 - ADAM_B2) * jnp.square(g)
        m_hat = m_new / (1.0 - ADAM_B1 ** ADAM_STEP)
        v_hat = v_new / (1.0 - ADAM_B2 ** ADAM_STEP)
        g_o[...] = g
        d_o[...] = -ADAM_LR * (m_hat / (jnp.sqrt(v_hat) + ADAM_EPS) + ADAM_WD * w_ref[...])
        m_o[...] = m_new
        v_o[...] = v_new

    t = pl.BlockSpec((tr, LANES), lambda i: (i, 0))
    o = jax.ShapeDtypeStruct((rows, LANES), F32)
    return pl.pallas_call(
        body, name=name, grid=(rows // tr,),
        in_specs=[pl.BlockSpec((nsrc, tr, LANES), lambda i: (0, i, 0)), t, t, t],
        out_specs=[t, t, t, t], out_shape=[o, o, o, o],
        compiler_params=_cp(("parallel",)),
    )(gall, w, m, v)


def _pair_sum(mine, theirs):
    rows = mine.shape[1]
    tr = min(SLAB_TILE, rows)
    assert rows % tr == 0
    core = lax.axis_index("c").astype(jnp.int32).reshape(1)

    def body(c_ref, a_ref, b_ref, o_ref):
        o_ref[...] = (a_ref[...].astype(F32) + b_ref[...].astype(F32)).astype(o_ref.dtype)

    blk = pl.BlockSpec((1, tr, LANES), lambda j, i, c_ref: (j, i, 0))
    return pl.pallas_call(
        body, name="pair_sum",
        grid_spec=pltpu.PrefetchScalarGridSpec(
            num_scalar_prefetch=1, grid=(NDEV // 2, rows // tr),
            in_specs=[pl.BlockSpec((1, tr, LANES), lambda j, i, c_ref: (2 * j + c_ref[0], i, 0)), blk],
            out_specs=blk),
        out_shape=jax.ShapeDtypeStruct((NDEV // 2, rows, LANES), mine.dtype),
        compiler_params=_cp(("parallel", "parallel")),
    )(core, mine, theirs)


HBM_SPEC = pl.BlockSpec(memory_space=pltpu.HBM)


def _remote(src, dst, send_sems, recv_sems, k, to):
    return pltpu.make_async_remote_copy(src_ref=src, dst_ref=dst, send_sem=send_sems.at[k], recv_sem=recv_sems.at[k],
                                        device_id=to, device_id_type=pl.DeviceIdType.MESH)


def _gather(name, x):
    rows, cols = x.shape

    def body(x_ref, o_ref, send_sems, recv_sems, local_sem):
        ix, iy, ic = lax.axis_index("x"), lax.axis_index("y"), lax.axis_index("c")
        me, sibling = (ix, iy, ic), (ix, iy, 1 - ic)
        chips = [(1 - ix, iy), (ix, 1 - iy), (1 - ix, 1 - iy)]

        def slab(px, py, pc):
            return o_ref.at[4 * px + 2 * py + pc]

        def copy(k, block, to, src=None):
            return _remote(slab(*block) if src is None else src, slab(*block), send_sems, recv_sems, k, to)

        mine = pltpu.make_async_copy(x_ref, slab(*me), local_sem)
        mine.start()
        first = [copy(0, me, sibling, src=x_ref)]
        first += [copy(1 + j, me, (*chip, ic), src=x_ref) for j, chip in enumerate(chips)]
        for cp in first:
            cp.start()
        passed = [copy(4 + j, (*chip, ic), sibling) for j, chip in enumerate(chips)]
        for j, chip in enumerate(chips):
            copy(1 + j, (*chip, ic), me).wait_recv()
            passed[j].start()
        copy(0, sibling, me).wait_recv()
        for j, chip in enumerate(chips):
            copy(4 + j, (*chip, 1 - ic), me).wait_recv()
        for cp in first + passed:
            cp.wait_send()
        mine.wait()

    return pl.pallas_call(
        body, name=name, in_specs=[HBM_SPEC], out_specs=HBM_SPEC,
        out_shape=jax.ShapeDtypeStruct((NDEV, rows, cols), x.dtype),
        scratch_shapes=[pltpu.SemaphoreType.DMA((NDEV - 1,)), pltpu.SemaphoreType.DMA((NDEV - 1,)),
                        pltpu.SemaphoreType.DMA],
    )(x)


def _sibling_exchange(name, x):
    rows, cols = x.shape[-2:]
    nchip = NDEV // 2

    def body(x_ref, o_ref, send_sems, recv_sems):
        ix, iy, ic = lax.axis_index("x"), lax.axis_index("y"), lax.axis_index("c")
        copies = [_remote(x_ref.at[2 * j + (1 - ic)], o_ref.at[j], send_sems, recv_sems, j, (ix, iy, 1 - ic))
                  for j in range(nchip)]
        for cp in copies:
            cp.start()
        for cp in copies:
            cp.wait()

    return pl.pallas_call(
        body, name=name, in_specs=[HBM_SPEC], out_specs=HBM_SPEC,
        out_shape=jax.ShapeDtypeStruct((nchip, rows, cols), x.dtype),
        scratch_shapes=[pltpu.SemaphoreType.DMA((nchip,)), pltpu.SemaphoreType.DMA((nchip,))],
    )(x)


def _chip_exchange(name, x):
    rows, cols = x.shape[-2:]
    nchip = NDEV // 2

    def body(x_ref, o_ref, send_sems, recv_sems, local_sem):
        ix, iy, ic = lax.axis_index("x"), lax.axis_index("y"), lax.axis_index("c")
        my_chip = 2 * ix + iy
        own = pltpu.make_async_copy(x_ref.at[my_chip], o_ref.at[my_chip], local_sem)
        own.start()
        copies = []
        for k in range(1, nchip):
            px, py = ix ^ (k >> 1), iy ^ (k & 1)
            copies.append(_remote(x_ref.at[2 * px + py], o_ref.at[my_chip], send_sems, recv_sems, k - 1, (px, py, ic)))
        for cp in copies:
            cp.start()
        for cp in copies:
            cp.wait()
        own.wait()

    return pl.pallas_call(
        body, name=name, in_specs=[HBM_SPEC], out_specs=HBM_SPEC,
        out_shape=jax.ShapeDtypeStruct((nchip, rows, cols), x.dtype),
        scratch_shapes=[pltpu.SemaphoreType.DMA((nchip - 1,)), pltpu.SemaphoreType.DMA((nchip - 1,)),
                        pltpu.SemaphoreType.DMA],
    )(x)


COL_SHARDED = {"w_in": (D, D_IN), "w_br_gdn": (HW, D), "w_br_sb": (HW, D), "w_br_mem": (HW, D), "w_up": (D, DFF),
               "conv_w": (4, 3 * HW)}
ROW_SHARDED = {"w_mem_kv": (D, 2 * HW), "w_o": (D, D), "w_down": (DFF, D)}


def _lane_blocks(c):
    return -(-c // LANES)


def _to_slab(p):
    c = p.shape[-1]
    nb = _lane_blocks(c)
    p = jnp.pad(p, [(0, 0)] * (p.ndim - 1) + [(0, nb * LANES - c)])
    return jnp.concatenate([p[..., j * LANES:(j + 1) * LANES] for j in range(nb)], axis=-2)


def _from_slab(flat, r, c):
    nb = _lane_blocks(c)
    return jnp.concatenate([flat[..., j * r:(j + 1) * r, :] for j in range(nb)], axis=-1)[..., :c]


def _shard_dims(name):
    if name in COL_SHARDED:
        r, c = COL_SHARDED[name]
        return r, c // NDEV
    r, c = ROW_SHARDED[name]
    return r // NDEV, c


def _pack_rows(parts, total):
    flat = jnp.concatenate(parts, axis=-2)
    return jnp.pad(flat, [(0, 0)] * (flat.ndim - 2) + [(0, total - flat.shape[-2]), (0, 0)])


def _pack_shards(vals):
    return _pack_rows([_to_slab(vals[n][0]) for n in BIG], R_BIG)


def _pack_full_grads(grads):
    parts = []
    for name in BIG:
        g = grads[name]
        r, c = _shard_dims(name)
        if name in COL_SHARDED:
            g = g.reshape(r, NDEV, c).transpose(1, 0, 2)
        else:
            g = g.reshape(NDEV, r, c)
        parts.append(_to_slab(g))
    return _pack_rows(parts, R_BIG)


def _unpack_gathered(slabs):
    out, pos = {}, 0
    for name, rows in zip(BIG, BIG_ROWS):
        r, c = _shard_dims(name)
        g = _from_slab(slabs[:, pos:pos + rows], r, c)
        pos += rows
        if name in COL_SHARDED:
            out[name] = g.transpose(1, 0, 2).reshape(r, NDEV * c)
        else:
            out[name] = g.reshape(NDEV * r, c)
    return out


def _unpack_shard(flat, shapes):
    out, pos = {}, 0
    for name, rows in zip(BIG, BIG_ROWS):
        r, c = _shard_dims(name)
        out[name] = _from_slab(flat[pos:pos + rows], r, c).reshape(shapes[name])
        pos += rows
    return out


def _pack_small(vals):
    rows = []
    for name, n in zip(SMALL, SMALL_ROWS):
        v = vals[name].reshape(-1)
        rows.append(jnp.pad(v, (0, n * LANES - v.shape[0])).reshape(n, LANES))
    return _pack_rows(rows, R_SMALL)


def _unpack_small(flat, shapes):
    out, pos = {}, 0
    for name, n in zip(SMALL, SMALL_ROWS):
        size = shapes[name][-1]
        out[name] = flat[pos:pos + n].reshape(-1)[:size].reshape(shapes[name])
        pos += n
    return out


def _pad_w_in(w):
    return jnp.concatenate([w[:, :2048], w[:, 2056:], w[:, 2048:2056], jnp.zeros((D, D_INP - D_IN), w.dtype)], axis=1)


def _unpad_w_in(w):
    return jnp.concatenate([w[:, :2048], w[:, 7168:7176], w[:, 2048:7168]], axis=1)


def _per_head(v):
    return jnp.repeat(v.reshape(NH), DH).reshape(1, HW)


def _local_step(x, mem, target, w, sm):
    S = x.shape[0]
    ts = _row_tile(S)
    alog_f, dtb_f = _per_head(sm["a_log"]), _per_head(sm["dt_bias"])

    proj = _mm("in_proj", x, w["w_in"], "nn", ts, 1536, D, pro="rms", pro_g=sm["norm1_g"], n_outer=True)
    gq, gk, gv, gf, bf, sqn, skn, svb, qmn = _pre_fwd(proj, w["conv_w"], alog_f, dtb_f, sm["sb_q_norm_g"],
                                                      sm["sb_k_norm_g"], sm["mem_q_norm_g"], S)
    ogdn, states = _gdn_fwd(gq, gk, gv, gf, bf, S)
    osb, sb_tot, sb_walked = _sb_fwd(sqn, skn, svb, S)
    kv = _mm("mem_kv", mem, w["w_mem_kv"], "nn", NMEM, D, D, pro="rms", pro_g=sm["mem_norm_g"])
    omem = _mem_fwd(qmn, kv, sm["mem_k_norm_g"], S)
    x1, mix = _merge_fwd(x, proj, ogdn, osb, omem, sm["gdn_norm_g"], w["w_br_gdn"], w["w_br_sb"], w["w_br_mem"],
                         w["w_o"], S)
    up = _mm("mlp_up", x1, w["w_up"], "nn", ts, 2048, D, pro="rms", pro_g=sm["norm2_g"], n_outer=True)
    x2 = _mm("mlp_down", up, w["w_down"], "nn", ts, D, 1024, pro="relu2", epi="add", epi_x=x1)
    dy, loss = _loss_grad(x2, target, S)

    g = {}
    dup = _mm("d_up", dy, w["w_down"], "nt", ts, 1024, D, epi="drelu2", epi_x=up, out_dtype=BF16)
    g["w_down"] = _mm("dw_down", up, dy, "tn", 1024, D, 512, pro="relu2")
    g["w_up"] = _mm("dw_up", x1, dup, "tn", D, 1024, 512, pro="rms", pro_g=sm["norm2_g"])
    dh2 = _mm("d_h2", dup, w["w_up"], "nt", ts, D, 1024)
    dx1, g["norm2_g"] = _norm_bwd("norm2_bwd", dh2, x1, sm["norm2_g"], dy)

    dmix = _mm("d_mix", dx1, w["w_o"], "nt", ts, D, D)
    g["w_o"] = _mm("dw_o", mix, dx1, "tn", D, D, 512)
    (dgl0, dgl1, dgl2, dogdn, dz, dosb, domem, g["w_br_gdn"], g["w_br_sb"], g["w_br_mem"],
     g["gdn_norm_g"]) = _merge_bwd(dmix, proj, ogdn, osb, omem, sm["gdn_norm_g"], w["w_br_gdn"], w["w_br_sb"],
                                   w["w_br_mem"], S)
    dmq, dkv, g["mem_q_norm_g"], g["mem_k_norm_g"] = _mem_bwd(proj, qmn, kv, sm["mem_q_norm_g"], sm["mem_k_norm_g"],
                                                             domem, S)
    g["w_mem_kv"] = _mm("dw_mem_kv", mem, dkv, "tn", D, D, NMEM, pro="rms", pro_g=sm["mem_norm_g"])
    dmn = _mm("d_mem_n", dkv, w["w_mem_kv"], "nt", NMEM, D, D)
    _, g["mem_norm_g"] = _norm_bwd("mem_norm_bwd", dmn, mem, sm["mem_norm_g"], None)
    dsqn, dskn, dsv = _sb_bwd(sqn, skn, svb, dosb, sb_tot, sb_walked, S)
    dgq, dgk, dgv, dgf, dbf = _gdn_bwd(gq, gk, gv, gf, bf, states, dogdn, S)
    dc, dab, dsq, dsk, g["conv_w"], dal_f, ddt_f, g["sb_q_norm_g"], g["sb_k_norm_g"] = _pre_bwd(
        proj, w["conv_w"], alog_f, dtb_f, sm["sb_q_norm_g"], sm["sb_k_norm_g"], dgq, dgk, dgv, dgf, dbf, dsqn, dskn, S)
    g["a_log"] = dal_f.reshape(NH, DH)[:, 0].reshape(1, NH)
    g["dt_bias"] = ddt_f.reshape(NH, DH)[:, 0].reshape(1, NH)
    dqkv = _conv_bwd(dc, w["conv_w"], S)

    dproj = jnp.concatenate([dqkv, dz, dsq, dsk, dsv, dmq, dgl0, dgl1, dgl2, dab], axis=1)
    g["w_in"] = _mm("dw_in", x, dproj, "tn", D, 1536, 512, pro="rms", pro_g=sm["norm1_g"])
    dh = _mm("d_h", dproj, w["w_in"], "nt", ts, D, 1536)
    dx, g["norm1_g"] = _norm_bwd("norm1_bwd", dh, x, sm["norm1_g"], dx1)
    return loss[0, 0], dx, g


def kernel(x, mem, norm1_g, w_in, conv_w, a_log, dt_bias, gdn_norm_g, sb_q_norm_g, sb_k_norm_g, mem_norm_g, w_mem_kv, mem_q_norm_g, mem_k_norm_g, w_br_gdn, w_br_sb, w_br_mem, w_o, norm2_g, w_up, w_down, loss_target, m_norm1_g, m_w_in, m_conv_w, m_a_log, m_dt_bias, m_gdn_norm_g, m_sb_q_norm_g, m_sb_k_norm_g, m_mem_norm_g, m_w_mem_kv, m_mem_q_norm_g, m_mem_k_norm_g, m_w_br_gdn, m_w_br_sb, m_w_br_mem, m_w_o, m_norm2_g, m_w_up, m_w_down, v_norm1_g, v_w_in, v_conv_w, v_a_log, v_dt_bias, v_gdn_norm_g, v_sb_q_norm_g, v_sb_k_norm_g, v_mem_norm_g, v_w_mem_kv, v_mem_q_norm_g, v_mem_k_norm_g, v_w_br_gdn, v_w_br_sb, v_w_br_mem, v_w_o, v_norm2_g, v_w_up, v_w_down):
    given = dict(norm1_g=norm1_g, w_in=w_in, conv_w=conv_w, a_log=a_log, dt_bias=dt_bias, gdn_norm_g=gdn_norm_g,
                 sb_q_norm_g=sb_q_norm_g, sb_k_norm_g=sb_k_norm_g, mem_norm_g=mem_norm_g, w_mem_kv=w_mem_kv,
                 mem_q_norm_g=mem_q_norm_g, mem_k_norm_g=mem_k_norm_g, w_br_gdn=w_br_gdn, w_br_sb=w_br_sb,
                 w_br_mem=w_br_mem, w_o=w_o, norm2_g=norm2_g, w_up=w_up, w_down=w_down)
    mom1 = dict(norm1_g=m_norm1_g, w_in=m_w_in, conv_w=m_conv_w, a_log=m_a_log, dt_bias=m_dt_bias,
                gdn_norm_g=m_gdn_norm_g, sb_q_norm_g=m_sb_q_norm_g, sb_k_norm_g=m_sb_k_norm_g,
                mem_norm_g=m_mem_norm_g, w_mem_kv=m_w_mem_kv, mem_q_norm_g=m_mem_q_norm_g,
                mem_k_norm_g=m_mem_k_norm_g, w_br_gdn=m_w_br_gdn, w_br_sb=m_w_br_sb, w_br_mem=m_w_br_mem, w_o=m_w_o,
                norm2_g=m_norm2_g, w_up=m_w_up, w_down=m_w_down)
    mom2 = dict(norm1_g=v_norm1_g, w_in=v_w_in, conv_w=v_conv_w, a_log=v_a_log, dt_bias=v_dt_bias,
                gdn_norm_g=v_gdn_norm_g, sb_q_norm_g=v_sb_q_norm_g, sb_k_norm_g=v_sb_k_norm_g,
                mem_norm_g=v_mem_norm_g, w_mem_kv=v_w_mem_kv, mem_q_norm_g=v_mem_q_norm_g,
                mem_k_norm_g=v_mem_k_norm_g, w_br_gdn=v_w_br_gdn, w_br_sb=v_w_br_sb, w_br_mem=v_w_br_mem, w_o=v_w_o,
                norm2_g=v_norm2_g, w_up=v_w_up, w_down=v_w_down)
    shapes = {n: given[n].shape for n in WEIGHTS}

    w_loc = _pack_shards(given)
    gathered = _gather("gather_weights", w_loc.astype(BF16))
    w = _unpack_gathered(gathered[:, :sum(BIG_ROWS)])
    w["w_in"] = _pad_w_in(w["w_in"])
    conv_loc = jnp.pad(given["conv_w"][0].reshape(-1, LANES), ((0, 2), (0, 0)))
    conv_all = _gather("gather_conv", conv_loc)
    w["conv_w"] = conv_all[:, :6].reshape(NDEV, 4, 3 * HW // NDEV).transpose(1, 0, 2).reshape(4, 3 * HW)
    sm = {n: given[n] for n in SMALL}

    loss, dx, g = _local_step(x[0], mem[0], loss_target[0], w, sm)
    g["w_in"] = _unpad_w_in(g["w_in"])

    g_mine = _pack_full_grads(g).astype(BF16)
    g_pair = _pair_sum(g_mine, _sibling_exchange("scatter_sibling", g_mine))
    g_all = _chip_exchange("scatter_chips", g_pair)
    gb, db, mb, vb = _adamw("adamw_sharded", g_all, w_loc, _pack_shards(mom1), _pack_shards(mom2))
    gs_all = _gather("gather_small_grads", _pack_small(g))
    gs, dsm, ms, vs = _adamw("adamw_replicated", gs_all, _pack_small(given), _pack_small(mom1), _pack_small(mom2))

    outs = {}
    for prefix, big, small in (("grad_", gb, gs), ("delta_", db, dsm), ("new_m_", mb, ms), ("new_v_", vb, vs)):
        vals = _unpack_shard(big, shapes)
        vals.update(_unpack_small(small, shapes))
        for n in WEIGHTS:
            outs[prefix + n] = vals[n]
    loss = lax.psum(loss, ("x", "y", "c"))
    return (loss, dx[None], *[outs[p + n] for p in ("grad_", "delta_", "new_m_", "new_v_") for n in WEIGHTS])
```

```python
import jax
import jax.numpy as jnp
from jax import lax
from jax.experimental import pallas as pl
from jax.experimental.pallas import tpu as pltpu

F32 = jnp.float32
BF16 = jnp.bfloat16

D = 1024
NH = 4
DH = 128
HW = NH * DH
DFF = 4 * D
NMEM = 256
EPS = 1e-6
NDEV = 8
LANES = 128
PAIR = 128
CHUNK = 64
D_IN = 7176
D_INP = 7680
VMEM_LIMIT = 56 * 1024 * 1024

ADAM_LR, ADAM_B1, ADAM_B2, ADAM_EPS, ADAM_WD, ADAM_STEP = 0.001, 0.9, 0.999, 1e-08, 0.01, 10

CB_Z, CB_SQ, CB_SK, CB_SV, CB_MQ, CB_AB = 3, 4, 5, 6, 7, 14

NN = (((1,), (0,)), ((), ()))
NT = (((1,), (1,)), ((), ()))
TN = (((0,), (0,)), ((), ()))

BIG = ("w_in", "w_mem_kv", "w_br_gdn", "w_br_sb", "w_br_mem", "w_o", "w_up", "w_down", "conv_w")
BIG_ROWS = (7176, 1024, 512, 512, 512, 1024, 4096, 4096, 6)
R_BIG = 19456
SLAB_TILE = 1216
SMALL = ("norm1_g", "a_log", "dt_bias", "gdn_norm_g", "sb_q_norm_g", "sb_k_norm_g", "mem_norm_g",
         "mem_q_norm_g", "mem_k_norm_g", "norm2_g")
SMALL_ROWS = (8, 1, 1, 1, 1, 1, 8, 1, 1, 8)
R_SMALL = 32
WEIGHTS = ("norm1_g", "w_in", "conv_w", "a_log", "dt_bias", "gdn_norm_g", "sb_q_norm_g", "sb_k_norm_g",
           "mem_norm_g", "w_mem_kv", "mem_q_norm_g", "mem_k_norm_g", "w_br_gdn", "w_br_sb", "w_br_mem",
           "w_o", "norm2_g", "w_up", "w_down")


def _cp(sem=None):
    return pltpu.CompilerParams(dimension_semantics=sem, vmem_limit_bytes=VMEM_LIMIT)


def _dot(a, b, dims=NN):
    return lax.dot_general(a, b, dims, preferred_element_type=F32)


def _dbf(a, b, dims=NN):
    return _dot(a.astype(BF16), b.astype(BF16), dims)


def _split(a, n):
    parts = []
    for _ in range(n):
        h = a.astype(BF16)
        parts.append(h)
        a = a - h.astype(F32)
    return parts


def _dg(a, b, dims=NN):
    return _dbf(a, b, dims)


def _d3(a, b, dims=NN):
    ah, al = _split(a, 2)
    bh, bl = _split(b, 2)
    return _dot(ah, bh, dims) + (_dot(ah, bl, dims) + _dot(al, bh, dims))


def _dxr(a, e, dims=NN):
    eb = e.astype(BF16)
    a1, a2, a3 = _split(a, 3)
    return _dot(a1, eb, dims) + (_dot(a2, eb, dims) + _dot(a3, eb, dims))


def _dxl(e, a, dims=NN):
    eb = e.astype(BF16)
    a1, a2, a3 = _split(a, 3)
    return _dot(eb, a1, dims) + (_dot(eb, a2, dims) + _dot(eb, a3, dims))


def _sigmoid(x):
    return 1.0 / (1.0 + jnp.exp(-x))


def _softplus(x):
    return jnp.maximum(x, 0.0) + jnp.log(1.0 + jnp.exp(-jnp.abs(x)))


def _rms(x, g):
    r = lax.rsqrt(jnp.mean(x * x, axis=-1, keepdims=True) + EPS)
    return x * r * g, r


def _rms_bwd(dy, x, g, r):
    dyg = dy * g
    dx = r * (dyg - x * (r * r) * jnp.mean(dyg * x, axis=-1, keepdims=True))
    dg = jnp.sum(dy * (x * r), axis=0, keepdims=True)
    return dx, dg


def _hs(h):
    return slice(h * DH, (h + 1) * DH)


def _row_tile(s):
    return 512 if s >= 2048 else 256


def _narrow_tile(s):
    return min(256, s)


def _mm(name, a, b, mode, tm, tn, tk, pro=None, pro_g=None, epi=None, epi_x=None, out_dtype=F32, n_outer=False):
    if mode == "tn":
        K, M = a.shape
    else:
        M, K = a.shape
    N = b.shape[0] if mode == "nt" else b.shape[1]
    tm, tn, tk = min(tm, M), min(tn, N), min(tk, K)
    nk = K // tk
    assert M % tm == 0 and N % tn == 0 and K % tk == 0, (name, M, N, K, tm, tn, tk)
    dims = {"nn": NN, "nt": NT, "tn": TN}[mode]

    def body(*refs):
        a_ref, b_ref = refs[0], refs[1]
        pos = 2
        g_ref = e_ref = None
        if pro == "rms":
            g_ref = refs[pos]
            pos += 1
        if epi is not None:
            e_ref = refs[pos]
            pos += 1
        o_ref = refs[pos]
        av = a_ref[...]
        if pro == "rms":
            av, _ = _rms(av.astype(F32), g_ref[...])
        elif pro == "relu2":
            av = jnp.square(jnp.maximum(av, 0.0))
        part = _dbf(av, b_ref[...], dims)

        def finish(acc):
            if epi == "add":
                acc = acc + e_ref[...]
            elif epi == "drelu2":
                acc = acc * (2.0 * jnp.maximum(e_ref[...], 0.0))
            o_ref[...] = acc.astype(out_dtype)

        if nk == 1:
            finish(part)
        else:
            acc_ref = refs[pos + 1]
            k = pl.program_id(2)

            @pl.when(k == 0)
            def _():
                acc_ref[...] = part

            @pl.when(k > 0)
            def _():
                acc_ref[...] += part

            @pl.when(k == nk - 1)
            def _():
                finish(acc_ref[...])

    def spec(shape, index):
        if n_outer:
            return pl.BlockSpec(shape, lambda j, i, k: index(i, j, k))
        return pl.BlockSpec(shape, index)

    if mode == "tn":
        a_spec = spec((tk, tm), lambda i, j, k: (k, i))
    else:
        a_spec = spec((tm, tk), lambda i, j, k: (i, k))
    if mode == "nt":
        b_spec = spec((tn, tk), lambda i, j, k: (j, k))
    else:
        b_spec = spec((tk, tn), lambda i, j, k: (k, j))
    in_specs, ops = [a_spec, b_spec], [a, b]
    if pro == "rms":
        w = pro_g.shape[1]
        assert (tm if mode == "tn" else tk) == w, name
        in_specs.append(spec((1, w), lambda i, j, k: (0, 0)))
        ops.append(pro_g)
    if epi is not None:
        in_specs.append(spec((tm, tn), lambda i, j, k: (i, j)))
        ops.append(epi_x)
    grid = (N // tn, M // tm, nk) if n_outer else (M // tm, N // tn, nk)
    return pl.pallas_call(
        body, name=name, grid=grid,
        in_specs=in_specs, out_specs=spec((tm, tn), lambda i, j, k: (i, j)),
        out_shape=jax.ShapeDtypeStruct((M, N), out_dtype),
        scratch_shapes=[pltpu.VMEM((tm, tn), F32)] if nk > 1 else [],
        compiler_params=_cp(("parallel", "parallel", "arbitrary")),
    )(*ops)


def _head_select(first_lane):
    l = lax.broadcasted_iota(jnp.int32, (LANES, HW), 0)
    c = lax.broadcasted_iota(jnp.int32, (LANES, HW), 1)
    return (l == first_lane + c // DH).astype(F32)


def _conv_taps(buf, cw, ts):
    c = cw[3:4, :] * buf[8:8 + ts, :]
    for j in range(3):
        k = 3 - j
        c = c + cw[j:j + 1, :] * buf[8 - k:8 - k + ts, :]
    return c


def _pre_fwd(proj, conv_w, alog_f, dtb_f, gsq, gsk, gmq, S):
    ts = _narrow_tile(S)
    hb = ts // 8

    def body(qkv_ref, halo_ref, ab_ref, sq_ref, sk_ref, sv_ref, mq_ref, cw_ref, al_ref, dt_ref, gsq_ref, gsk_ref,
             gmq_ref, gq_o, gk_o, gv_o, gf_o, bf_o, sqn_o, skn_o, svb_o, qmn_o, buf):
        i = pl.program_id(0)
        buf[0:8, :] = jnp.where(i == 0, 0.0, halo_ref[...])
        buf[8:8 + ts, :] = qkv_ref[...]
        c = _conv_taps(buf, cw_ref[...], ts)
        a = c * _sigmoid(c)
        for h in range(NH):
            q = a[:, h * DH:(h + 1) * DH]
            k = a[:, HW + h * DH:HW + (h + 1) * DH]
            gq_o[:, _hs(h)] = q * (lax.rsqrt(jnp.sum(q * q, axis=-1, keepdims=True) + EPS) * DH ** -0.5)
            gk_o[:, _hs(h)] = k * lax.rsqrt(jnp.sum(k * k, axis=-1, keepdims=True) + EPS)
            sqn_o[:, _hs(h)] = _rms(sq_ref[:, _hs(h)], gsq_ref[...])[0].astype(BF16)
            skn_o[:, _hs(h)] = _rms(sk_ref[:, _hs(h)], gsk_ref[...])[0].astype(BF16)
            qmn_o[:, _hs(h)] = _rms(mq_ref[:, _hs(h)], gmq_ref[...])[0].astype(BF16)
        gv_o[...] = a[:, 2 * HW:3 * HW]
        svb_o[...] = sv_ref[...].astype(BF16)
        ab = ab_ref[:, 0:LANES]
        a_bc = _dxr(ab, _head_select(0))
        b_bc = _dxr(ab, _head_select(NH))
        gf_o[...] = -jnp.exp(al_ref[...]) * _softplus(a_bc + dt_ref[...])
        bf_o[...] = _sigmoid(b_bc)

    row = lambda cb: pl.BlockSpec((ts, HW), lambda i: (i, cb))
    full = lambda r, c: pl.BlockSpec((r, c), lambda i: (0, 0))
    f32o = jax.ShapeDtypeStruct((S, HW), F32)
    bfo = jax.ShapeDtypeStruct((S, HW), BF16)
    return pl.pallas_call(
        body, name="pre_fwd", grid=(S // ts,),
        in_specs=[pl.BlockSpec((ts, 3 * HW), lambda i: (i, 0)),
                  pl.BlockSpec((8, 3 * HW), lambda i: (jnp.maximum(i * hb - 1, 0), 0)),
                  row(CB_AB), row(CB_SQ), row(CB_SK), row(CB_SV), row(CB_MQ),
                  full(4, 3 * HW), full(1, HW), full(1, HW), full(1, DH), full(1, DH), full(1, DH)],
        out_specs=[pl.BlockSpec((ts, HW), lambda i: (i, 0))] * 9,
        out_shape=[f32o, f32o, f32o, f32o, f32o, bfo, bfo, bfo, bfo],
        scratch_shapes=[pltpu.VMEM((ts + 8, 3 * HW), F32)],
        compiler_params=_cp(("parallel",)),
    )(proj, proj, proj, proj, proj, proj, proj, conv_w, alog_f, dtb_f, gsq, gsk, gmq)


def _pre_bwd(proj, conv_w, alog_f, dtb_f, gsq, gsk, dgq, dgk, dgv, dgf, dbf, dsqn, dskn, S):
    ts = _narrow_tile(S)
    hb = ts // 8

    def body(qkv_ref, halo_ref, ab_ref, sq_ref, sk_ref, cw_ref, al_ref, dt_ref, gsq_ref, gsk_ref,
             dgq_ref, dgk_ref, dgv_ref, dgf_ref, dbf_ref, dsqn_ref, dskn_ref,
             dc_o, dab_o, dsq_o, dsk_o, dcw_o, dal_o, ddt_o, dgsq_o, dgsk_o, buf):
        i = pl.program_id(0)

        @pl.when(i == 0)
        def _():
            dcw_o[...] = jnp.zeros_like(dcw_o)
            dal_o[...] = jnp.zeros_like(dal_o)
            ddt_o[...] = jnp.zeros_like(ddt_o)
            dgsq_o[...] = jnp.zeros_like(dgsq_o)
            dgsk_o[...] = jnp.zeros_like(dgsk_o)

        buf[0:8, :] = jnp.where(i == 0, 0.0, halo_ref[...])
        buf[8:8 + ts, :] = qkv_ref[...]
        c = _conv_taps(buf, cw_ref[...], ts)
        sg = _sigmoid(c)
        a = c * sg
        dsilu = sg * (1.0 + c * (1.0 - sg))
        dgsq = jnp.zeros((1, DH), F32)
        dgsk = jnp.zeros((1, DH), F32)
        for h in range(NH):
            q = a[:, h * DH:(h + 1) * DH]
            k = a[:, HW + h * DH:HW + (h + 1) * DH]
            nq = lax.rsqrt(jnp.sum(q * q, axis=-1, keepdims=True) + EPS)
            nk = lax.rsqrt(jnp.sum(k * k, axis=-1, keepdims=True) + EPS)
            dyq = dgq_ref[:, _hs(h)]
            dyk = dgk_ref[:, _hs(h)]
            dq = (nq * dyq - q * (nq * nq * nq) * jnp.sum(dyq * q, axis=-1, keepdims=True)) * DH ** -0.5
            dk = nk * dyk - k * (nk * nk * nk) * jnp.sum(dyk * k, axis=-1, keepdims=True)
            dc_o[:, h * DH:(h + 1) * DH] = dq * dsilu[:, h * DH:(h + 1) * DH]
            dc_o[:, HW + h * DH:HW + (h + 1) * DH] = dk * dsilu[:, HW + h * DH:HW + (h + 1) * DH]
            x = sq_ref[:, _hs(h)]
            _, r = _rms(x, gsq_ref[...])
            dx, dg = _rms_bwd(dsqn_ref[:, _hs(h)], x, gsq_ref[...], r)
            dsq_o[:, _hs(h)] = dx.astype(BF16)
            dgsq = dgsq + dg
            x = sk_ref[:, _hs(h)]
            _, r = _rms(x, gsk_ref[...])
            dx, dg = _rms_bwd(dskn_ref[:, _hs(h)], x, gsk_ref[...], r)
            dsk_o[:, _hs(h)] = dx.astype(BF16)
            dgsk = dgsk + dg
        dc_o[:, 2 * HW:3 * HW] = dgv_ref[...] * dsilu[:, 2 * HW:3 * HW]
        dgsq_o[...] += dgsq
        dgsk_o[...] += dgsk
        dc = dc_o[...]
        for j in range(4):
            k = 3 - j
            dcw_o[j:j + 1, :] += jnp.sum(dc * buf[8 - k:8 - k + ts, :], axis=0, keepdims=True)
        ab = ab_ref[:, 0:LANES]
        a_bc = _dxr(ab, _head_select(0))
        b_bc = _dxr(ab, _head_select(NH))
        pre = a_bc + dt_ref[...]
        ea = jnp.exp(al_ref[...])
        dgf = dgf_ref[...]
        dal_o[...] += jnp.sum(dgf * (-ea * _softplus(pre)), axis=0, keepdims=True)
        da = dgf * (-ea * _sigmoid(pre))
        ddt_o[...] += jnp.sum(da, axis=0, keepdims=True)
        beta = _sigmoid(b_bc)
        db = dbf_ref[...] * beta * (1.0 - beta)
        lane = lax.broadcasted_iota(jnp.int32, (ts, LANES), 1)
        dab = jnp.zeros((ts, LANES), F32)
        for h in range(NH):
            dab = dab + jnp.where(lane == h, da[:, _hs(h)], 0.0) + jnp.where(lane == NH + h, db[:, _hs(h)], 0.0)
        dab_o[:, 0:LANES] = dab.astype(BF16)
        dab_o[:, LANES:HW] = jnp.zeros((ts, HW - LANES), BF16)

    row = lambda cb: pl.BlockSpec((ts, HW), lambda i: (i, cb))
    full = lambda r, c: pl.BlockSpec((r, c), lambda i: (0, 0))
    t512 = pl.BlockSpec((ts, HW), lambda i: (i, 0))
    return pl.pallas_call(
        body, name="pre_bwd", grid=(S // ts,),
        in_specs=[pl.BlockSpec((ts, 3 * HW), lambda i: (i, 0)),
                  pl.BlockSpec((8, 3 * HW), lambda i: (jnp.maximum(i * hb - 1, 0), 0)),
                  row(CB_AB), row(CB_SQ), row(CB_SK),
                  full(4, 3 * HW), full(1, HW), full(1, HW), full(1, DH), full(1, DH)] + [t512] * 7,
        out_specs=[pl.BlockSpec((ts, 3 * HW), lambda i: (i, 0)), t512, t512, t512,
                   full(4, 3 * HW), full(1, HW), full(1, HW), full(1, DH), full(1, DH)],
        out_shape=[jax.ShapeDtypeStruct((S, 3 * HW), F32)] + [jax.ShapeDtypeStruct((S, HW), BF16)] * 3
        + [jax.ShapeDtypeStruct((4, 3 * HW), F32), jax.ShapeDtypeStruct((1, HW), F32),
           jax.ShapeDtypeStruct((1, HW), F32), jax.ShapeDtypeStruct((1, DH), F32),
           jax.ShapeDtypeStruct((1, DH), F32)],
        scratch_shapes=[pltpu.VMEM((ts + 8, 3 * HW), F32)],
        compiler_params=_cp(("arbitrary",)),
    )(proj, proj, proj, proj, proj, conv_w, alog_f, dtb_f, gsq, gsk, dgq, dgk, dgv, dgf, dbf, dsqn, dskn)


def _conv_bwd(dc, conv_w, S):
    ts = _row_tile(S)
    hb = ts // 8
    n = S // ts

    def body(dc_ref, halo_ref, cw_ref, o_ref, buf):
        i = pl.program_id(0)
        buf[0:ts, :] = dc_ref[...]
        buf[ts:ts + 8, :] = jnp.where(i == n - 1, 0.0, halo_ref[...])
        cw = cw_ref[...]
        acc = cw[3:4, :] * buf[0:ts, :]
        for k in range(1, 4):
            acc = acc + cw[3 - k:4 - k, :] * buf[k:k + ts, :]
        o_ref[...] = acc.astype(BF16)

    return pl.pallas_call(
        body, name="conv_bwd", grid=(n,),
        in_specs=[pl.BlockSpec((ts, 3 * HW), lambda i: (i, 0)),
                  pl.BlockSpec((8, 3 * HW), lambda i: (jnp.minimum((i + 1) * hb, S // 8 - 1), 0)),
                  pl.BlockSpec((4, 3 * HW), lambda i: (0, 0))],
        out_specs=pl.BlockSpec((ts, 3 * HW), lambda i: (i, 0)),
        out_shape=jax.ShapeDtypeStruct((S, 3 * HW), BF16),
        scratch_shapes=[pltpu.VMEM((ts + 8, 3 * HW), F32)],
        compiler_params=_cp(("parallel",)),
    )(dc, dc, conv_w)


def _gdn_masks():
    r = lax.broadcasted_iota(jnp.int32, (PAIR, PAIR), 0)
    c = lax.broadcasted_iota(jnp.int32, (PAIR, PAIR), 1)
    same = ((r >= CHUNK) & (c >= CHUNK)) | ((r < CHUNK) & (c < CHUNK))
    return dict(r=r, same=same, tril=same & (r >= c), strict=same & (r > c), triu=same & (c >= r), eye=r == c,
                in_a=r < CHUNK, last_a=r == CHUNK - 1, last_b=r == PAIR - 1)


def _each(fn, *cols):
    return [fn(*xs) for xs in zip(*cols)]


def _mul(a, b):
    return a * b


def _top(x):
    return x[:CHUNK]


def _bot(x):
    return x[CHUNK:]


def _rows(a, b):
    return jnp.concatenate([a, b], axis=0)


def _tri_inv(lm, eye):
    eye_f = eye.astype(F32)
    p = _each(lambda l: eye_f - l, lm)
    lp = _each(lambda l: _d3(l, l), lm)
    for it in range(5):
        p = _each(lambda a, b: a + _d3(a, b), p, lp)
        if it < 4:
            lp = _each(lambda b: _d3(b, b), lp)
    return p


def _gdn_block(m, q, k, v, g, beta):
    tril_f = m["tril"].astype(F32)
    col_sum = lambda mask: (lambda x: jnp.sum(jnp.where(mask, x, 0.0), axis=0, keepdims=True))
    gc = _each(lambda x: _dxl(tril_f, x), g)
    gcr = _each(col_sum(m["eye"]), gc)
    gam = _each(lambda a, b: jnp.where(m["tril"], jnp.exp(jnp.minimum(a - b, 0.0)), 0.0), gc, gcr)
    kb = _each(_mul, k, beta)
    vb = _each(_mul, v, beta)
    lm = _each(lambda a, b, c: jnp.where(m["strict"], _dg(a, b, NT) * c, 0.0), kb, k, gam)
    t = _tri_inv(lm, m["eye"])
    eg = _each(jnp.exp, gc)
    kbe = _each(_mul, kb, eg)
    u = _each(_d3, t, vb)
    w = _each(_d3, t, kbe)
    aqk = _each(lambda a, b, c: jnp.where(m["tril"], _dg(a, b, NT) * c, 0.0), q, k, gam)
    qd = _each(_mul, q, eg)
    ga = _each(col_sum(m["last_a"]), gc)
    gb = _each(col_sum(m["last_b"]), gc)
    e2 = _each(lambda a, b, c: jnp.exp(jnp.where(m["in_a"], a, b) - c), ga, gb, gc)
    kd = _each(_mul, k, e2)
    return dict(u=u, w=w, aqk=aqk, qd=qd, kd=kd, gam=gam, kb=kb, vb=vb, lm=lm, t=t, eg=eg, kbe=kbe, e2=e2,
                gla=_each(jnp.exp, ga), glb=_each(jnp.exp, gb))


def _gdn_fwd(gq, gk, gv, gf, bf, S):
    nb = S // PAIR

    def body(q_ref, k_ref, v_ref, g_ref, b_ref, o_ref, st_ref, s_scr):
        @pl.when(pl.program_id(0) == 0)
        def _():
            s_scr[...] = jnp.zeros_like(s_scr)

        m = _gdn_masks()
        heads = lambda ref: [ref[:, _hs(h)] for h in range(NH)]
        f = _gdn_block(m, heads(q_ref), heads(k_ref), heads(v_ref), heads(g_ref), heads(b_ref))
        u, w, qd, kd = f["u"], f["w"], f["qd"], f["kd"]
        s0 = [s_scr[h * DH:(h + 1) * DH, :] for h in range(NH)]
        vna = _each(lambda a, b, s: _top(a) - _dg(_top(b), s), u, w, s0)
        oa = _each(lambda a, s: _dg(_top(a), s), qd, s0)
        s1 = _each(lambda s, gl, a, vn: s * gl + _dg(_top(a), vn, TN), s0, f["gla"], kd, vna)
        vnb = _each(lambda a, b, s: _bot(a) - _dg(_bot(b), s), u, w, s1)
        ob = _each(lambda a, s: _dg(_bot(a), s), qd, s1)
        s2 = _each(lambda s, gl, a, vn: s * gl + _dg(_bot(a), vn, TN), s1, f["glb"], kd, vnb)
        outs = _each(lambda a, b, c, va, vb: _rows(a, b) + _dg(c, _rows(va, vb)), oa, ob, f["aqk"], vna, vnb)
        o_ref[...] = jnp.concatenate(outs, axis=1)
        st_ref[...] = jnp.concatenate(s0 + s1, axis=0)
        s_scr[...] = jnp.concatenate(s2, axis=0)

    blk = pl.BlockSpec((PAIR, HW), lambda i: (i, 0))
    return pl.pallas_call(
        body, name="gdn_fwd", grid=(nb,),
        in_specs=[blk] * 5,
        out_specs=[blk, pl.BlockSpec((2 * NH * DH, DH), lambda i: (i, 0))],
        out_shape=[jax.ShapeDtypeStruct((S, HW), F32), jax.ShapeDtypeStruct((nb * 2 * NH * DH, DH), F32)],
        scratch_shapes=[pltpu.VMEM((NH * DH, DH), F32)],
        compiler_params=_cp(("arbitrary",)),
    )(gq, gk, gv, gf, bf)


def _gdn_bwd(gq, gk, gv, gf, bf, states, do, S):
    nb = S // PAIR

    def body(q_ref, k_ref, v_ref, g_ref, b_ref, st_ref, do_ref, dq_o, dk_o, dv_o, dg_o, db_o, ds_scr):
        @pl.when(pl.program_id(0) == 0)
        def _():
            ds_scr[...] = jnp.zeros_like(ds_scr)

        m = _gdn_masks()
        ones = jnp.ones((PAIR, PAIR), F32)
        heads = lambda ref: [ref[:, _hs(h)] for h in range(NH)]
        q, k, v, beta, do = heads(q_ref), heads(k_ref), heads(v_ref), heads(b_ref), heads(do_ref)
        f = _gdn_block(m, q, k, v, heads(g_ref), beta)
        u, w, aqk, qd, kd, t = f["u"], f["w"], f["aqk"], f["qd"], f["kd"], f["t"]
        s0 = [st_ref[h * DH:(h + 1) * DH, :] for h in range(NH)]
        s1 = [st_ref[(NH + h) * DH:(NH + h + 1) * DH, :] for h in range(NH)]
        ds2 = [ds_scr[h * DH:(h + 1) * DH, :] for h in range(NH)]
        total = lambda a, b: jnp.sum(jnp.sum(a * b, axis=1, keepdims=True), axis=0, keepdims=True)
        vna = _each(lambda a, b, s: _top(a) - _dg(_top(b), s), u, w, s0)
        vnb = _each(lambda a, b, s: _bot(a) - _dg(_bot(b), s), u, w, s1)
        dvn_i = _each(lambda a, b: _dg(a, b, TN), aqk, do)
        dvnb = _each(lambda a, b, s: _bot(a) + _dg(_bot(b), s), dvn_i, kd, ds2)
        dqdb = _each(lambda a, s: _dg(_bot(a), s, NT), do, s1)
        dkdb = _each(lambda a, s: _dg(a, s, NT), vnb, ds2)
        dglb = _each(total, ds2, s1)
        dwb = _each(lambda a, s: -_dg(a, s, NT), dvnb, s1)
        ds1 = _each(lambda s, gl, a, b, c, d: s * gl + _dg(_bot(a), _bot(b), TN) - _dg(_bot(c), d, TN),
                    ds2, f["glb"], qd, do, w, dvnb)
        dvna = _each(lambda a, b, s: _top(a) + _dg(_top(b), s), dvn_i, kd, ds1)
        dqda = _each(lambda a, s: _dg(_top(a), s, NT), do, s0)
        dkda = _each(lambda a, s: _dg(a, s, NT), vna, ds1)
        dgla = _each(total, ds1, s0)
        dwa = _each(lambda a, s: -_dg(a, s, NT), dvna, s0)
        ds0 = _each(lambda s, gl, a, b, c, d: s * gl + _dg(_top(a), _top(b), TN) - _dg(_top(c), d, TN),
                    ds1, f["gla"], qd, do, w, dvna)
        dvn, dqd, dkd, dw = (_each(_rows, a, b) for a, b in ((dvna, dvnb), (dqda, dqdb), (dkda, dkdb), (dwa, dwb)))
        daqk = _each(lambda a, va, vb: jnp.where(m["tril"], _dg(a, _rows(va, vb), NT), 0.0), do, vna, vnb)
        dt = _each(lambda a, b, c, d: _dg(a, b, NT) + _dg(c, d, NT), dvn, f["vb"], dw, f["kbe"])
        dvb = _each(lambda a, b: _dg(a, b, TN), t, dvn)
        dkbe = _each(lambda a, b: _dg(a, b, TN), t, dw)
        dtt = _each(lambda a, b: _d3(a, b, NT), dt, t)
        dl = _each(lambda a, b: -jnp.where(m["strict"], _d3(a, b, TN), 0.0), t, dtt)
        dm = _each(_mul, dl, f["gam"])
        dn = _each(_mul, daqk, f["gam"])
        dkb = _each(lambda a, b, c, d: _dg(a, b) + c * d, dm, k, dkbe, f["eg"])
        dks = _each(lambda a, b, c, d, e, g, h, i: _dg(a, b, TN) + _dg(c, d, TN) + e * g + h * i,
                    dm, f["kb"], dn, q, dkd, f["e2"], beta, dkb)
        dqs = _each(lambda a, b, c, d: _dg(a, b) + c * d, dn, k, dqd, f["eg"])
        gm = _each(lambda a, b, c, d: a * b + c * d, dl, f["lm"], daqk, aqk)
        dkdkd = _each(_mul, dkd, kd)
        dgc = _each(lambda a, b, c, d, e, g: _dxr(a + b * c + d * e - g, ones) - _dxr(a, ones, TN),
                    gm, dqd, qd, dkbe, f["kbe"], dkdkd)
        same_f = m["same"].astype(F32)
        chunk_tot = _each(lambda a: _dxl(same_f, _dxr(a, ones)), dkdkd)
        last = m["last_a"] | m["last_b"]
        dgc = _each(lambda a, b, ga, gla, gb, glb: a + jnp.where(last, b + jnp.where(m["in_a"], ga * gla, gb * glb), 0.0),
                    dgc, chunk_tot, dgla, f["gla"], dglb, f["glb"])
        dbs = _each(lambda a, b, c, d: _dxr(a * b + c * d, ones), dkb, k, dvb, v)
        dvs = _each(_mul, beta, dvb)
        triu_f = m["triu"].astype(F32)
        dgs = _each(lambda a: _dxl(triu_f, a), dgc)
        for ref, parts in ((dq_o, dqs), (dk_o, dks), (dv_o, dvs), (dg_o, dgs), (db_o, dbs)):
            ref[...] = jnp.concatenate(parts, axis=1)
        ds_scr[...] = jnp.concatenate(ds0, axis=0)

    blk = pl.BlockSpec((PAIR, HW), lambda i: (nb - 1 - i, 0))
    o = jax.ShapeDtypeStruct((S, HW), F32)
    return pl.pallas_call(
        body, name="gdn_bwd", grid=(nb,),
        in_specs=[blk] * 5 + [pl.BlockSpec((2 * NH * DH, DH), lambda i: (nb - 1 - i, 0)), blk],
        out_specs=[blk] * 5, out_shape=[o] * 5,
        scratch_shapes=[pltpu.VMEM((NH * DH, DH), F32)],
        compiler_params=_cp(("arbitrary",)),
    )(gq, gk, gv, gf, bf, states, do)


SB_T = 256
SB_GROUP = 4
SB_GROUP_BWD = 4
SB_SINGLES = 2
SB_DEAD = -110.0


def _group_sizes(g):
    sizes = []
    while g >= 1:
        sizes.append(g)
        g //= 2
    return sizes


def _sb_iotas(t):
    return lax.broadcasted_iota(jnp.int32, (t, t), 0), lax.broadcasted_iota(jnp.int32, (t, t), 1)


def _sb_scores(q, k, mask):
    z = _dot(q, k, NT) * DH ** -0.5
    ls = jnp.minimum(z, 0.0) - jnp.log(1.0 + jnp.exp(-jnp.abs(z)))
    lneg = ls - z
    if mask is not None:
        lneg = jnp.where(mask, lneg, 0.0)
    return ls, lneg


def _prefix(x, u):
    xh, xl = _split(x, 2)
    return _dot(xh, u) + _dot(xl, u)


def _sb_fwd(sqn, skn, svb, S):
    t = min(SB_T, S)

    def body(q_ref, k_ref, v_ref, o_ref, t_ref, cnt_ref):
        qb = pl.program_id(1)
        q = q_ref[...]
        r, c = _sb_iotas(t)
        diag = c < r
        u_after = (r > c).astype(BF16)

        def tiles(k0s, run, mask):
            sc = _each(lambda k0: _sb_scores(q, k_ref[pl.ds(k0, t), :], mask), k0s)
            ls, lneg = [s[0] for s in sc], [s[1] for s in sc]
            sums = _each(lambda x: jnp.sum(x, axis=1, keepdims=True), lneg)
            pre = _each(lambda x: _prefix(x, u_after), lneg)
            runs = [run]
            for s in sums:
                runs.append(runs[-1] + s)
            att = _each(lambda a, b, rn: jnp.exp(a + (rn + b)), ls, pre, runs[:-1])
            if mask is not None:
                att = _each(lambda a: jnp.where(mask, a, 0.0), att)
            parts = _each(lambda a, k0: _dot(a.astype(BF16), v_ref[pl.ds(k0, t), :]), att, k0s)
            return sum(parts[1:], parts[0]), runs[-1]

        acc, run = tiles([pl.multiple_of(qb * t, t)], jnp.zeros((t, 1), F32), diag)

        def alive(run):
            return jnp.max(run) >= SB_DEAD

        carry, done = (0, acc, run, alive(run)), 0
        for size, limit in [(1, SB_SINGLES)] + [(s, None) for s in _group_sizes(SB_GROUP)]:

            def more(c, size=size, done=done, limit=limit):
                i, _, _, go = c
                fits = done + (i + 1) * size <= qb
                return (fits if limit is None else fits & (i < limit)) & go

            def group(c, size=size, done=done):
                i, acc, run, _ = c
                first = qb - 1 - done - size * i
                part, run = tiles([pl.multiple_of((first - j) * t, t) for j in range(size)], run, None)
                return i + 1, acc + part, run, alive(run)

            n, acc, run, go = lax.while_loop(more, group, (0,) + carry[1:])
            carry, done = (0, acc, run, go), done + n * size
        o_ref[...] = acc.astype(BF16)
        t_ref[...] = jnp.broadcast_to(run, (t, DH))
        cnt_ref[pl.program_id(0), qb] = done

    qspec = pl.BlockSpec((t, DH), lambda h, i: (i, h))
    kspec = pl.BlockSpec((S, DH), lambda h, i: (0, h))
    return pl.pallas_call(
        body, name="sb_fwd", grid=(NH, S // t),
        in_specs=[qspec, kspec, kspec],
        out_specs=[qspec, qspec, pl.BlockSpec(memory_space=pltpu.SMEM)],
        out_shape=[jax.ShapeDtypeStruct((S, HW), BF16), jax.ShapeDtypeStruct((S, HW), F32),
                   jax.ShapeDtypeStruct((NH, S // t), jnp.int32)],
        compiler_params=_cp(("arbitrary", "arbitrary")),
    )(sqn, skn, svb)


def _sb_bwd(sqn, skn, svb, do, tot, walked, S):
    t = min(SB_T, S)

    def body(cnt_ref, q_ref, k_ref, v_ref, do_ref, t_ref, dq_o, dk_o, dv_o, dv_acc):
        qb = pl.program_id(1)

        @pl.when(qb == 0)
        def _():
            dk_o[...] = jnp.zeros_like(dk_o)
            dv_acc[...] = jnp.zeros_like(dv_acc)

        q = q_ref[...]
        do = do_ref[...].astype(BF16)
        tot_l = jnp.concatenate([t_ref[...]] * (t // DH), axis=1)
        r, c = _sb_iotas(t)
        diag = c < r
        u_upto = (r <= c).astype(BF16)
        u_before = (r < c).astype(BF16)

        def tiles(k0s, run_l, run_e, mask):
            rowsum = lambda x: jnp.sum(x, axis=1, keepdims=True)
            ks = [k_ref[pl.ds(k0, t), :] for k0 in k0s]
            vs = [v_ref[pl.ds(k0, t), :] for k0 in k0s]
            sc = _each(lambda k: _sb_scores(q, k, mask), ks)
            ls, lneg = [s[0] for s in sc], [s[1] for s in sc]
            sums_l = _each(rowsum, lneg)
            pre_l = _each(lambda x: _prefix(x, u_upto), lneg)
            runs_l = [run_l]
            for s in sums_l:
                runs_l.append(runs_l[-1] + s)
            att = _each(lambda a, b, rn: jnp.exp(a + (tot_l - (rn + b))), ls, pre_l, runs_l[:-1])
            if mask is not None:
                att = _each(lambda a: jnp.where(mask, a, 0.0), att)
            e = _each(lambda v, a: _dot(do, v, NT) * a, vs, att)
            sums_e = _each(rowsum, e)
            pre_e = _each(lambda x: _prefix(x, u_before), e)
            runs_e = [run_e]
            for s in sums_e:
                runs_e.append(runs_e[-1] + s)
            sg = _each(jnp.exp, ls)
            dz = _each(lambda a, b, rn, s: a * (1.0 - s) - (rn + b) * s, e, pre_e, runs_e[:-1], sg)
            if mask is not None:
                dz = _each(lambda a: jnp.where(mask, a, 0.0), dz)
            dz = _each(lambda a: (a * DH ** -0.5).astype(BF16), dz)
            dvs = _each(lambda a: _dot(a.astype(BF16), do, TN), att)
            dks = _each(lambda a: _dot(a, q, TN), dz)
            dqs = _each(_dot, dz, ks)
            for k0, dv, dk in zip(k0s, dvs, dks):
                dv_acc[pl.ds(k0, t), :] += dv
                dk_o[pl.ds(k0, t), :] += dk
            return sum(dqs[1:], dqs[0]), runs_l[-1], runs_e[-1]

        walked = cnt_ref[pl.program_id(0), qb]
        z1 = jnp.zeros((t, 1), F32)
        carry, done = (jnp.zeros((t, DH), F32), z1, z1), 0
        for size in _group_sizes(SB_GROUP_BWD):
            n = (walked - done) // size

            def group(i, carry, size=size, done=done):
                dq, run_l, run_e = carry
                first = qb - walked + done + size * i
                part, run_l, run_e = tiles([pl.multiple_of((first + j) * t, t) for j in range(size)], run_l, run_e,
                                           None)
                return dq + part, run_l, run_e

            carry = lax.fori_loop(0, n, group, carry)
            done = done + n * size
        dq, run_l, run_e = carry
        part, _, _ = tiles([pl.multiple_of(qb * t, t)], run_l, run_e, diag)
        dq_o[...] = dq + part

        @pl.when(qb == S // t - 1)
        def _():
            dv_o[...] = dv_acc[...].astype(BF16)

    qspec = pl.BlockSpec((t, DH), lambda h, i, cnt: (i, h))
    kspec = pl.BlockSpec((S, DH), lambda h, i, cnt: (0, h))
    o = jax.ShapeDtypeStruct((S, HW), F32)
    return pl.pallas_call(
        body, name="sb_bwd",
        grid_spec=pltpu.PrefetchScalarGridSpec(
            num_scalar_prefetch=1, grid=(NH, S // t),
            in_specs=[qspec, kspec, kspec, qspec, qspec], out_specs=[qspec, kspec, kspec],
            scratch_shapes=[pltpu.VMEM((S, DH), F32)]),
        out_shape=[o, o, jax.ShapeDtypeStruct((S, HW), BF16)],
        compiler_params=_cp(("parallel", "arbitrary")),
    )(walked, sqn, skn, svb, do, tot)


def _mem_probs(qn, kn):
    s = _dot(qn, kn.astype(BF16), NT) * DH ** -0.5
    p = jnp.exp(s - jnp.max(s, axis=-1, keepdims=True))
    return p / jnp.sum(p, axis=-1, keepdims=True)


def _mem_fwd(qmn, kv, gmk, S):
    ts = _row_tile(S)

    def body(q_ref, kv_ref, gk_ref, o_ref):
        for h in range(NH):
            kn, _ = _rms(kv_ref[:, _hs(h)], gk_ref[...])
            p = _mem_probs(q_ref[:, _hs(h)], kn)
            o_ref[:, _hs(h)] = _dbf(p, kv_ref[:, HW + h * DH:HW + (h + 1) * DH]).astype(BF16)

    return pl.pallas_call(
        body, name="mem_fwd", grid=(S // ts,),
        in_specs=[pl.BlockSpec((ts, HW), lambda i: (i, 0)), pl.BlockSpec((NMEM, 2 * HW), lambda i: (0, 0)),
                  pl.BlockSpec((1, DH), lambda i: (0, 0))],
        out_specs=pl.BlockSpec((ts, HW), lambda i: (i, 0)),
        out_shape=jax.ShapeDtypeStruct((S, HW), BF16),
        compiler_params=_cp(("parallel",)),
    )(qmn, kv, gmk)


def _mem_bwd(proj, qmn, kv, gmq, gmk, do, S):
    ts = _row_tile(S)
    n = S // ts

    def body(mq_ref, q_ref, kv_ref, gq_ref, gk_ref, do_ref, dmq_o, dkv_o, dgq_o, dgk_o, dkn_scr):
        i = pl.program_id(0)

        @pl.when(i == 0)
        def _():
            dkv_o[...] = jnp.zeros_like(dkv_o)
            dgq_o[...] = jnp.zeros_like(dgq_o)
            dkn_scr[...] = jnp.zeros_like(dkn_scr)

        dgq = jnp.zeros((1, DH), F32)
        for h in range(NH):
            km = kv_ref[:, _hs(h)]
            vm = kv_ref[:, HW + h * DH:HW + (h + 1) * DH].astype(BF16)
            kn, _ = _rms(km, gk_ref[...])
            qn = q_ref[:, _hs(h)]
            p = _mem_probs(qn, kn)
            dob = do_ref[:, _hs(h)].astype(BF16)
            dkv_o[:, HW + h * DH:HW + (h + 1) * DH] += _dot(p.astype(BF16), dob, TN)
            dp = _dot(dob, vm, NT)
            dsc = (p * (dp - jnp.sum(dp * p, axis=-1, keepdims=True)) * DH ** -0.5).astype(BF16)
            dkn_scr[:, _hs(h)] += _dot(dsc, qn, TN)
            x = mq_ref[:, _hs(h)]
            _, r = _rms(x, gq_ref[...])
            dx, dg = _rms_bwd(_dot(dsc, kn.astype(BF16)), x, gq_ref[...], r)
            dmq_o[:, _hs(h)] = dx.astype(BF16)
            dgq = dgq + dg
        dgq_o[...] += dgq

        @pl.when(i == n - 1)
        def _():
            dgk = jnp.zeros((1, DH), F32)
            for h in range(NH):
                km = kv_ref[:, _hs(h)]
                _, r = _rms(km, gk_ref[...])
                dx, dg = _rms_bwd(dkn_scr[:, _hs(h)], km, gk_ref[...], r)
                dkv_o[:, _hs(h)] = dx
                dgk = dgk + dg
            dgk_o[...] = dgk

    full = lambda r, c: pl.BlockSpec((r, c), lambda i: (0, 0))
    t512 = pl.BlockSpec((ts, HW), lambda i: (i, 0))
    return pl.pallas_call(
        body, name="mem_bwd", grid=(n,),
        in_specs=[pl.BlockSpec((ts, HW), lambda i: (i, CB_MQ)), t512, full(NMEM, 2 * HW), full(1, DH), full(1, DH),
                  t512],
        out_specs=[t512, full(NMEM, 2 * HW), full(1, DH), full(1, DH)],
        out_shape=[jax.ShapeDtypeStruct((S, HW), BF16), jax.ShapeDtypeStruct((NMEM, 2 * HW), F32),
                   jax.ShapeDtypeStruct((1, DH), F32), jax.ShapeDtypeStruct((1, DH), F32)],
        scratch_shapes=[pltpu.VMEM((NMEM, HW), F32)],
        compiler_params=_cp(("arbitrary",)),
    )(proj, qmn, kv, gmq, gmk, do)


def _gated_gdn(o, z, g):
    sg = _sigmoid(z)
    outs, rs = [], []
    for h in range(NH):
        y, r = _rms(o[:, _hs(h)], g)
        outs.append(y * (z[:, _hs(h)] * sg[:, _hs(h)]))
        rs.append(r)
    return jnp.concatenate(outs, axis=1), rs, sg


def _merge_fwd(x, proj, ogdn, osb, omem, ggdn, wbg, wbs, wbm, wo, S):
    ts = _narrow_tile(S)

    def body(x_ref, z_ref, g0_ref, g1_ref, g2_ref, og_ref, os_ref, om_ref, gg_ref, wbg_ref, wbs_ref, wbm_ref,
             wo_ref, x1_o, mix_o):
        on, _, _ = _gated_gdn(og_ref[...], z_ref[...], gg_ref[...])
        mix = (_sigmoid(g0_ref[...]) * _dbf(on, wbg_ref[...]) + _sigmoid(g1_ref[...]) * _dbf(os_ref[...], wbs_ref[...])
               + _sigmoid(g2_ref[...]) * _dbf(om_ref[...], wbm_ref[...]))
        mix_o[...] = mix.astype(BF16)
        x1_o[...] = x_ref[...] + _dbf(mix, wo_ref[...])

    t512 = pl.BlockSpec((ts, HW), lambda i: (i, 0))
    t1k = pl.BlockSpec((ts, D), lambda i: (i, 0))
    gate = lambda j: pl.BlockSpec((ts, D), lambda i: (i, 4 + j))
    full = lambda r, c: pl.BlockSpec((r, c), lambda i: (0, 0))
    return pl.pallas_call(
        body, name="merge_fwd", grid=(S // ts,),
        in_specs=[t1k, pl.BlockSpec((ts, HW), lambda i: (i, CB_Z)), gate(0), gate(1), gate(2), t512, t512, t512,
                  full(1, DH), full(HW, D), full(HW, D), full(HW, D), full(D, D)],
        out_specs=[t1k, t1k],
        out_shape=[jax.ShapeDtypeStruct((S, D), F32), jax.ShapeDtypeStruct((S, D), BF16)],
        compiler_params=_cp(("parallel",)),
    )(x, proj, proj, proj, proj, ogdn, osb, omem, ggdn, wbg, wbs, wbm, wo)


def _merge_bwd(dmix, proj, ogdn, osb, omem, ggdn, wbg, wbs, wbm, S):
    ts = _narrow_tile(S)

    def body(dm_ref, z_ref, g0_ref, g1_ref, g2_ref, og_ref, os_ref, om_ref, gg_ref, wbg_ref, wbs_ref, wbm_ref,
             dgl0_o, dgl1_o, dgl2_o, dog_o, dz_o, dos_o, dom_o, dwbg_o, dwbs_o, dwbm_o, dgg_o):
        @pl.when(pl.program_id(0) == 0)
        def _():
            for ref in (dwbg_o, dwbs_o, dwbm_o, dgg_o):
                ref[...] = jnp.zeros_like(ref)

        dm = dm_ref[...]
        og = og_ref[...]
        z = z_ref[...]
        on, rs, sg = _gated_gdn(og, z, gg_ref[...])
        branch = ((on, g0_ref, wbg_ref, dgl0_o, dwbg_o), (os_ref[...], g1_ref, wbs_ref, dgl1_o, dwbs_o),
                  (om_ref[...], g2_ref, wbm_ref, dgl2_o, dwbm_o))
        dos = []
        for o, g_ref, w_ref, dgl_o, dw_o in branch:
            ob = o.astype(BF16)
            gate = _sigmoid(g_ref[...])
            dgl_o[...] = (dm * _dot(ob, w_ref[...]) * gate * (1.0 - gate)).astype(BF16)
            dy = (dm * gate).astype(BF16)
            dw_o[...] += _dot(ob, dy, TN)
            dos.append(_dot(dy, w_ref[...], NT))
        dos_o[...] = dos[1].astype(BF16)
        dom_o[...] = dos[2].astype(BF16)
        don = dos[0]
        dgg = jnp.zeros((1, DH), F32)
        for h in range(NH):
            oh, zh, sh = og[:, _hs(h)], z[:, _hs(h)], sg[:, _hs(h)]
            y = oh * rs[h] * gg_ref[...]
            dz_o[:, _hs(h)] = (don[:, _hs(h)] * y * (sh * (1.0 + zh * (1.0 - sh)))).astype(BF16)
            dx, dg = _rms_bwd(don[:, _hs(h)] * (zh * sh), oh, gg_ref[...], rs[h])
            dog_o[:, _hs(h)] = dx
            dgg = dgg + dg
        dgg_o[...] += dgg

    t512 = pl.BlockSpec((ts, HW), lambda i: (i, 0))
    t1k = pl.BlockSpec((ts, D), lambda i: (i, 0))
    gate = lambda j: pl.BlockSpec((ts, D), lambda i: (i, 4 + j))
    full = lambda r, c: pl.BlockSpec((r, c), lambda i: (0, 0))
    s1k = jax.ShapeDtypeStruct((S, D), BF16)
    s512 = jax.ShapeDtypeStruct((S, HW), BF16)
    wsh = jax.ShapeDtypeStruct((HW, D), F32)
    return pl.pallas_call(
        body, name="merge_bwd", grid=(S // ts,),
        in_specs=[t1k, pl.BlockSpec((ts, HW), lambda i: (i, CB_Z)), gate(0), gate(1), gate(2), t512, t512, t512,
                  full(1, DH), full(HW, D), full(HW, D), full(HW, D)],
        out_specs=[t1k, t1k, t1k, t512, t512, t512, t512, full(HW, D), full(HW, D), full(HW, D), full(1, DH)],
        out_shape=[s1k, s1k, s1k, jax.ShapeDtypeStruct((S, HW), F32), s512, s512, s512, wsh, wsh, wsh,
                   jax.ShapeDtypeStruct((1, DH), F32)],
        compiler_params=_cp(("arbitrary",)),
    )(dmix, proj, proj, proj, proj, ogdn, osb, omem, ggdn, wbg, wbs, wbm)


def _loss_grad(y, target, S):
    ts = _row_tile(S)

    def body(y_ref, t_ref, dy_o, loss_o):
        @pl.when(pl.program_id(0) == 0)
        def _():
            loss_o[...] = jnp.zeros_like(loss_o)

        err = y_ref[...] - t_ref[...]
        dy_o[...] = err * (1.0 / D)
        per_tok = jnp.sum(err * err, axis=1, keepdims=True) * (1.0 / D)
        loss_o[...] += 0.5 * jnp.sum(per_tok, axis=0, keepdims=True)

    t1k = pl.BlockSpec((ts, D), lambda i: (i, 0))
    return pl.pallas_call(
        body, name="loss_grad", grid=(S // ts,), in_specs=[t1k, t1k],
        out_specs=[t1k, pl.BlockSpec((1, 1), lambda i: (0, 0))],
        out_shape=[jax.ShapeDtypeStruct((S, D), F32), jax.ShapeDtypeStruct((1, 1), F32)],
        compiler_params=_cp(("arbitrary",)),
    )(y, target)


def _norm_bwd(name, dh, x, g, res):
    rows = x.shape[0]
    ts = min(_row_tile(rows), rows)

    def body(*refs):
        dh_ref, x_ref, g_ref = refs[:3]
        dx_o, dg_o = refs[-2:]

        @pl.when(pl.program_id(0) == 0)
        def _():
            dg_o[...] = jnp.zeros_like(dg_o)

        xv = x_ref[...]
        _, r = _rms(xv, g_ref[...])
        dx, dg = _rms_bwd(dh_ref[...], xv, g_ref[...], r)
        dx_o[...] = dx if res is None else dx + refs[3][...]
        dg_o[...] += dg

    t1k = pl.BlockSpec((ts, D), lambda i: (i, 0))
    gsp = pl.BlockSpec((1, D), lambda i: (0, 0))
    ops = [dh, x, g] + ([] if res is None else [res])
    return pl.pallas_call(
        body, name=name, grid=(rows // ts,), in_specs=[t1k, t1k, gsp] + ([] if res is None else [t1k]),
        out_specs=[t1k, gsp],
        out_shape=[jax.ShapeDtypeStruct((rows, D), F32), jax.ShapeDtypeStruct((1, D), F32)],
        compiler_params=_cp(("arbitrary",)),
    )(*ops)


def _adamw(name, gall, w, m, v):
    rows = w.shape[0]
    nsrc = gall.shape[0]
    tr = min(SLAB_TILE, rows)
    assert rows % tr == 0

    def body(g_ref, w_ref, m_ref, v_ref, g_o, d_o, m_o, v_o):
        g = g_ref[0].astype(F32)
        for j in range(1, nsrc):
            g = g + g_ref[j].astype(F32)
        m_new = ADAM_B1 * m_ref[...] + (1.0 - ADAM_B1) * g
        v_new = ADAM_B2 * v_ref[...] + (1.0 - ADAM_B2) * jnp.square(g)
        m_hat = m_new / (1.0 - ADAM_B1 ** ADAM_STEP)
        v_hat = v_new / (1.0 - ADAM_B2 ** ADAM_STEP)
        g_o[...] = g
        d_o[...] = -ADAM_LR * (m_hat / (jnp.sqrt(v_hat) + ADAM_EPS) + ADAM_WD * w_ref[...])
        m_o[...] = m_new
        v_o[...] = v_new

    t = pl.BlockSpec((tr, LANES), lambda i: (i, 0))
    o = jax.ShapeDtypeStruct((rows, LANES), F32)
    return pl.pallas_call(
        body, name=name, grid=(rows // tr,),
        in_specs=[pl.BlockSpec((nsrc, tr, LANES), lambda i: (0, i, 0)), t, t, t],
        out_specs=[t, t, t, t], out_shape=[o, o, o, o],
        compiler_params=_cp(("parallel",)),
    )(gall, w, m, v)


def _pair_sum(mine, theirs):
    rows = mine.shape[1]
    tr = min(SLAB_TILE, rows)
    assert rows % tr == 0
    core = lax.axis_index("c").astype(jnp.int32).reshape(1)

    def body(c_ref, a_ref, b_ref, o_ref):
        o_ref[...] = (a_ref[...].astype(F32) + b_ref[...].astype(F32)).astype(o_ref.dtype)

    blk = pl.BlockSpec((1, tr, LANES), lambda j, i, c_ref: (j, i, 0))
    return pl.pallas_call(
        body, name="pair_sum",
        grid_spec=pltpu.PrefetchScalarGridSpec(
            num_scalar_prefetch=1, grid=(NDEV // 2, rows // tr),
            in_specs=[pl.BlockSpec((1, tr, LANES), lambda j, i, c_ref: (2 * j + c_ref[0], i, 0)), blk],
            out_specs=blk),
        out_shape=jax.ShapeDtypeStruct((NDEV // 2, rows, LANES), mine.dtype),
        compiler_params=_cp(("parallel", "parallel")),
    )(core, mine, theirs)


HBM_SPEC = pl.BlockSpec(memory_space=pltpu.HBM)


def _remote(src, dst, send_sems, recv_sems, k, to):
    return pltpu.make_async_remote_copy(src_ref=src, dst_ref=dst, send_sem=send_sems.at[k], recv_sem=recv_sems.at[k],
                                        device_id=to, device_id_type=pl.DeviceIdType.MESH)


def _gather(name, x):
    rows, cols = x.shape

    def body(x_ref, o_ref, send_sems, recv_sems, local_sem):
        ix, iy, ic = lax.axis_index("x"), lax.axis_index("y"), lax.axis_index("c")
        me, sibling = (ix, iy, ic), (ix, iy, 1 - ic)
        chips = [(1 - ix, iy), (ix, 1 - iy), (1 - ix, 1 - iy)]

        def slab(px, py, pc):
            return o_ref.at[4 * px + 2 * py + pc]

        def copy(k, block, to, src=None):
            return _remote(slab(*block) if src is None else src, slab(*block), send_sems, recv_sems, k, to)

        mine = pltpu.make_async_copy(x_ref, slab(*me), local_sem)
        mine.start()
        first = [copy(0, me, sibling, src=x_ref)]
        first += [copy(1 + j, me, (*chip, ic), src=x_ref) for j, chip in enumerate(chips)]
        for cp in first:
            cp.start()
        passed = [copy(4 + j, (*chip, ic), sibling) for j, chip in enumerate(chips)]
        for j, chip in enumerate(chips):
            copy(1 + j, (*chip, ic), me).wait_recv()
            passed[j].start()
        copy(0, sibling, me).wait_recv()
        for j, chip in enumerate(chips):
            copy(4 + j, (*chip, 1 - ic), me).wait_recv()
        for cp in first + passed:
            cp.wait_send()
        mine.wait()

    return pl.pallas_call(
        body, name=name, in_specs=[HBM_SPEC], out_specs=HBM_SPEC,
        out_shape=jax.ShapeDtypeStruct((NDEV, rows, cols), x.dtype),
        scratch_shapes=[pltpu.SemaphoreType.DMA((NDEV - 1,)), pltpu.SemaphoreType.DMA((NDEV - 1,)),
                        pltpu.SemaphoreType.DMA],
    )(x)


def _sibling_exchange(name, x):
    rows, cols = x.shape[-2:]
    nchip = NDEV // 2

    def body(x_ref, o_ref, send_sems, recv_sems):
        ix, iy, ic = lax.axis_index("x"), lax.axis_index("y"), lax.axis_index("c")
        copies = [_remote(x_ref.at[2 * j + (1 - ic)], o_ref.at[j], send_sems, recv_sems, j, (ix, iy, 1 - ic))
                  for j in range(nchip)]
        for cp in copies:
            cp.start()
        for cp in copies:
            cp.wait()

    return pl.pallas_call(
        body, name=name, in_specs=[HBM_SPEC], out_specs=HBM_SPEC,
        out_shape=jax.ShapeDtypeStruct((nchip, rows, cols), x.dtype),
        scratch_shapes=[pltpu.SemaphoreType.DMA((nchip,)), pltpu.SemaphoreType.DMA((nchip,))],
    )(x)


def _chip_exchange(name, x):
    rows, cols = x.shape[-2:]
    nchip = NDEV // 2

    def body(x_ref, o_ref, send_sems, recv_sems, local_sem):
        ix, iy, ic = lax.axis_index("x"), lax.axis_index("y"), lax.axis_index("c")
        my_chip = 2 * ix + iy
        own = pltpu.make_async_copy(x_ref.at[my_chip], o_ref.at[my_chip], local_sem)
        own.start()
        copies = []
        for k in range(1, nchip):
            px, py = ix ^ (k >> 1), iy ^ (k & 1)
            copies.append(_remote(x_ref.at[2 * px + py], o_ref.at[my_chip], send_sems, recv_sems, k - 1, (px, py, ic)))
        for cp in copies:
            cp.start()
        for cp in copies:
            cp.wait()
        own.wait()

    return pl.pallas_call(
        body, name=name, in_specs=[HBM_SPEC], out_specs=HBM_SPEC,
        out_shape=jax.ShapeDtypeStruct((nchip, rows, cols), x.dtype),
        scratch_shapes=[pltpu.SemaphoreType.DMA((nchip - 1,)), pltpu.SemaphoreType.DMA((nchip - 1,)),
                        pltpu.SemaphoreType.DMA],
    )(x)


COL_SHARDED = {"w_in": (D, D_IN), "w_br_gdn": (HW, D), "w_br_sb": (HW, D), "w_br_mem": (HW, D), "w_up": (D, DFF),
               "conv_w": (4, 3 * HW)}
ROW_SHARDED = {"w_mem_kv": (D, 2 * HW), "w_o": (D, D), "w_down": (DFF, D)}


def _to_slab(p):
    return p.reshape(p.shape[:-2] + (-1, LANES))


def _from_slab(flat, r, c):
    return flat.reshape(flat.shape[:-2] + (r, c))


def _shard_dims(name):
    if name in COL_SHARDED:
        r, c = COL_SHARDED[name]
        return r, c // NDEV
    r, c = ROW_SHARDED[name]
    return r // NDEV, c


def _pack_rows(parts, total):
    flat = jnp.concatenate(parts, axis=-2)
    return jnp.pad(flat, [(0, 0)] * (flat.ndim - 2) + [(0, total - flat.shape[-2]), (0, 0)])


def _pack_shards(vals):
    return _pack_rows([_to_slab(vals[n][0]) for n in BIG], R_BIG)


def _pack_full_grads(grads):
    parts = []
    for name in BIG:
        g = grads[name]
        r, c = _shard_dims(name)
        if name in COL_SHARDED:
            g = g.reshape(r, NDEV, c).transpose(1, 0, 2)
        else:
            g = g.reshape(NDEV, r, c)
        parts.append(_to_slab(g))
    return _pack_rows(parts, R_BIG)


def _unpack_gathered(slabs):
    out, pos = {}, 0
    for name, rows in zip(BIG, BIG_ROWS):
        r, c = _shard_dims(name)
        g = _from_slab(slabs[:, pos:pos + rows], r, c)
        pos += rows
        if name in COL_SHARDED:
            out[name] = g.transpose(1, 0, 2).reshape(r, NDEV * c)
        else:
            out[name] = g.reshape(NDEV * r, c)
    return out


def _unpack_shard(flat, shapes):
    out, pos = {}, 0
    for name, rows in zip(BIG, BIG_ROWS):
        r, c = _shard_dims(name)
        out[name] = _from_slab(flat[pos:pos + rows], r, c).reshape(shapes[name])
        pos += rows
    return out


def _pack_small(vals):
    rows = []
    for name, n in zip(SMALL, SMALL_ROWS):
        v = vals[name].reshape(-1)
        rows.append(jnp.pad(v, (0, n * LANES - v.shape[0])).reshape(n, LANES))
    return _pack_rows(rows, R_SMALL)


def _unpack_small(flat, shapes):
    out, pos = {}, 0
    for name, n in zip(SMALL, SMALL_ROWS):
        size = shapes[name][-1]
        out[name] = flat[pos:pos + n].reshape(-1)[:size].reshape(shapes[name])
        pos += n
    return out


def _pad_w_in(w):
    return jnp.concatenate([w[:, :2048], w[:, 2056:], w[:, 2048:2056], jnp.zeros((D, D_INP - D_IN), w.dtype)], axis=1)


def _unpad_w_in(w):
    return jnp.concatenate([w[:, :2048], w[:, 7168:7176], w[:, 2048:7168]], axis=1)


def _per_head(v):
    return jnp.repeat(v.reshape(NH), DH).reshape(1, HW)


def _local_step(x, mem, target, w, sm):
    S = x.shape[0]
    ts = _row_tile(S)
    alog_f, dtb_f = _per_head(sm["a_log"]), _per_head(sm["dt_bias"])

    proj = _mm("in_proj", x, w["w_in"], "nn", ts, 1536, D, pro="rms", pro_g=sm["norm1_g"], n_outer=True)
    gq, gk, gv, gf, bf, sqn, skn, svb, qmn = _pre_fwd(proj, w["conv_w"], alog_f, dtb_f, sm["sb_q_norm_g"],
                                                      sm["sb_k_norm_g"], sm["mem_q_norm_g"], S)
    ogdn, states = _gdn_fwd(gq, gk, gv, gf, bf, S)
    osb, sb_tot, sb_walked = _sb_fwd(sqn, skn, svb, S)
    kv = _mm("mem_kv", mem, w["w_mem_kv"], "nn", NMEM, D, D, pro="rms", pro_g=sm["mem_norm_g"])
    omem = _mem_fwd(qmn, kv, sm["mem_k_norm_g"], S)
    x1, mix = _merge_fwd(x, proj, ogdn, osb, omem, sm["gdn_norm_g"], w["w_br_gdn"], w["w_br_sb"], w["w_br_mem"],
                         w["w_o"], S)
    up = _mm("mlp_up", x1, w["w_up"], "nn", ts, 2048, D, pro="rms", pro_g=sm["norm2_g"], n_outer=True)
    x2 = _mm("mlp_down", up, w["w_down"], "nn", ts, D, 1024, pro="relu2", epi="add", epi_x=x1)
    dy, loss = _loss_grad(x2, target, S)

    g = {}
    dup = _mm("d_up", dy, w["w_down"], "nt", ts, 1024, D, epi="drelu2", epi_x=up, out_dtype=BF16)
    g["w_down"] = _mm("dw_down", up, dy, "tn", 1024, D, 512, pro="relu2")
    g["w_up"] = _mm("dw_up", x1, dup, "tn", D, 1024, 512, pro="rms", pro_g=sm["norm2_g"])
    dh2 = _mm("d_h2", dup, w["w_up"], "nt", ts, D, 1024)
    dx1, g["norm2_g"] = _norm_bwd("norm2_bwd", dh2, x1, sm["norm2_g"], dy)

    dmix = _mm("d_mix", dx1, w["w_o"], "nt", ts, D, D)
    g["w_o"] = _mm("dw_o", mix, dx1, "tn", D, D, 512)
    (dgl0, dgl1, dgl2, dogdn, dz, dosb, domem, g["w_br_gdn"], g["w_br_sb"], g["w_br_mem"],
     g["gdn_norm_g"]) = _merge_bwd(dmix, proj, ogdn, osb, omem, sm["gdn_norm_g"], w["w_br_gdn"], w["w_br_sb"],
                                   w["w_br_mem"], S)
    dmq, dkv, g["mem_q_norm_g"], g["mem_k_norm_g"] = _mem_bwd(proj, qmn, kv, sm["mem_q_norm_g"], sm["mem_k_norm_g"],
                                                             domem, S)
    g["w_mem_kv"] = _mm("dw_mem_kv", mem, dkv, "tn", D, D, NMEM, pro="rms", pro_g=sm["mem_norm_g"])
    dmn = _mm("d_mem_n", dkv, w["w_mem_kv"], "nt", NMEM, D, D)
    _, g["mem_norm_g"] = _norm_bwd("mem_norm_bwd", dmn, mem, sm["mem_norm_g"], None)
    dsqn, dskn, dsv = _sb_bwd(sqn, skn, svb, dosb, sb_tot, sb_walked, S)
    dgq, dgk, dgv, dgf, dbf = _gdn_bwd(gq, gk, gv, gf, bf, states, dogdn, S)
    dc, dab, dsq, dsk, g["conv_w"], dal_f, ddt_f, g["sb_q_norm_g"], g["sb_k_norm_g"] = _pre_bwd(
        proj, w["conv_w"], alog_f, dtb_f, sm["sb_q_norm_g"], sm["sb_k_norm_g"], dgq, dgk, dgv, dgf, dbf, dsqn, dskn, S)
    g["a_log"] = dal_f.reshape(NH, DH)[:, 0].reshape(1, NH)
    g["dt_bias"] = ddt_f.reshape(NH, DH)[:, 0].reshape(1, NH)
    dqkv = _conv_bwd(dc, w["conv_w"], S)

    dproj = jnp.concatenate([dqkv, dz, dsq, dsk, dsv, dmq, dgl0, dgl1, dgl2, dab], axis=1)
    g["w_in"] = _mm("dw_in", x, dproj, "tn", D, 1536, 512, pro="rms", pro_g=sm["norm1_g"])
    dh = _mm("d_h", dproj, w["w_in"], "nt", ts, D, 1536)
    dx, g["norm1_g"] = _norm_bwd("norm1_bwd", dh, x, sm["norm1_g"], dx1)
    return loss[0, 0], dx, g


def kernel(x, mem, norm1_g, w_in, conv_w, a_log, dt_bias, gdn_norm_g, sb_q_norm_g, sb_k_norm_g, mem_norm_g, w_mem_kv, mem_q_norm_g, mem_k_norm_g, w_br_gdn, w_br_sb, w_br_mem, w_o, norm2_g, w_up, w_down, loss_target, m_norm1_g, m_w_in, m_conv_w, m_a_log, m_dt_bias, m_gdn_norm_g, m_sb_q_norm_g, m_sb_k_norm_g, m_mem_norm_g, m_w_mem_kv, m_mem_q_norm_g, m_mem_k_norm_g, m_w_br_gdn, m_w_br_sb, m_w_br_mem, m_w_o, m_norm2_g, m_w_up, m_w_down, v_norm1_g, v_w_in, v_conv_w, v_a_log, v_dt_bias, v_gdn_norm_g, v_sb_q_norm_g, v_sb_k_norm_g, v_mem_norm_g, v_w_mem_kv, v_mem_q_norm_g, v_mem_k_norm_g, v_w_br_gdn, v_w_br_sb, v_w_br_mem, v_w_o, v_norm2_g, v_w_up, v_w_down):
    given = dict(norm1_g=norm1_g, w_in=w_in, conv_w=conv_w, a_log=a_log, dt_bias=dt_bias, gdn_norm_g=gdn_norm_g,
                 sb_q_norm_g=sb_q_norm_g, sb_k_norm_g=sb_k_norm_g, mem_norm_g=mem_norm_g, w_mem_kv=w_mem_kv,
                 mem_q_norm_g=mem_q_norm_g, mem_k_norm_g=mem_k_norm_g, w_br_gdn=w_br_gdn, w_br_sb=w_br_sb,
                 w_br_mem=w_br_mem, w_o=w_o, norm2_g=norm2_g, w_up=w_up, w_down=w_down)
    mom1 = dict(norm1_g=m_norm1_g, w_in=m_w_in, conv_w=m_conv_w, a_log=m_a_log, dt_bias=m_dt_bias,
                gdn_norm_g=m_gdn_norm_g, sb_q_norm_g=m_sb_q_norm_g, sb_k_norm_g=m_sb_k_norm_g,
                mem_norm_g=m_mem_norm_g, w_mem_kv=m_w_mem_kv, mem_q_norm_g=m_mem_q_norm_g,
                mem_k_norm_g=m_mem_k_norm_g, w_br_gdn=m_w_br_gdn, w_br_sb=m_w_br_sb, w_br_mem=m_w_br_mem, w_o=m_w_o,
                norm2_g=m_norm2_g, w_up=m_w_up, w_down=m_w_down)
    mom2 = dict(norm1_g=v_norm1_g, w_in=v_w_in, conv_w=v_conv_w, a_log=v_a_log, dt_bias=v_dt_bias,
                gdn_norm_g=v_gdn_norm_g, sb_q_norm_g=v_sb_q_norm_g, sb_k_norm_g=v_sb_k_norm_g,
                mem_norm_g=v_mem_norm_g, w_mem_kv=v_w_mem_kv, mem_q_norm_g=v_mem_q_norm_g,
                mem_k_norm_g=v_mem_k_norm_g, w_br_gdn=v_w_br_gdn, w_br_sb=v_w_br_sb, w_br_mem=v_w_br_mem, w_o=v_w_o,
                norm2_g=v_norm2_g, w_up=v_w_up, w_down=v_w_down)
    shapes = {n: given[n].shape for n in WEIGHTS}

    w_loc = _pack_shards(given)
    gathered = _gather("gather_weights", w_loc.astype(BF16))
    w = _unpack_gathered(gathered[:, :sum(BIG_ROWS)])
    w["w_in"] = _pad_w_in(w["w_in"])
    conv_loc = jnp.pad(given["conv_w"][0].reshape(-1, LANES), ((0, 2), (0, 0)))
    conv_all = _gather("gather_conv", conv_loc)
    w["conv_w"] = conv_all[:, :6].reshape(NDEV, 4, 3 * HW // NDEV).transpose(1, 0, 2).reshape(4, 3 * HW)
    sm = {n: given[n] for n in SMALL}

    loss, dx, g = _local_step(x[0], mem[0], loss_target[0], w, sm)
    g["w_in"] = _unpad_w_in(g["w_in"])

    g_mine = _pack_full_grads(g).astype(BF16)
    g_pair = _pair_sum(g_mine, _sibling_exchange("scatter_sibling", g_mine))
    g_all = _chip_exchange("scatter_chips", g_pair)
    gb, db, mb, vb = _adamw("adamw_sharded", g_all, w_loc, _pack_shards(mom1), _pack_shards(mom2))
    gs_all = _gather("gather_small_grads", _pack_small(g))
    gs, dsm, ms, vs = _adamw("adamw_replicated", gs_all, _pack_small(given), _pack_small(mom1), _pack_small(mom2))

    outs = {}
    for prefix, big, small in (("grad_", gb, gs), ("delta_", db, dsm), ("new_m_", mb, ms), ("new_v_", vb, vs)):
        vals = _unpack_shard(big, shapes)
        vals.update(_unpack_small(small, shapes))
        for n in WEIGHTS:
            outs[prefix + n] = vals[n]
    loss = lax.psum(loss, ("x", "y", "c"))
    return (loss, dx[None], *[outs[p + n] for p in ("grad_", "delta_", "new_m_", "new_v_") for n in WEIGHTS])
```

```python
import jax
import jax.numpy as jnp
from jax import lax
from jax.experimental import pallas as pl
from jax.experimental.pallas import tpu as pltpu

F32 = jnp.float32
BF16 = jnp.bfloat16

D = 1024
NH = 4
DH = 128
HW = NH * DH
DFF = 4 * D
NMEM = 256
EPS = 1e-6
NDEV = 8
LANES = 128
PAIR = 128
CHUNK = 64
D_IN = 7176
D_INP = 7680
VMEM_LIMIT = 56 * 1024 * 1024

ADAM_LR, ADAM_B1, ADAM_B2, ADAM_EPS, ADAM_WD, ADAM_STEP = 0.001, 0.9, 0.999, 1e-08, 0.01, 10

CB_Z, CB_SQ, CB_SK, CB_SV, CB_MQ, CB_AB = 3, 4, 5, 6, 7, 14

NN = (((1,), (0,)), ((), ()))
NT = (((1,), (1,)), ((), ()))
TN = (((0,), (0,)), ((), ()))

BIG = ("w_in", "w_mem_kv", "w_br_gdn", "w_br_sb", "w_br_mem", "w_o", "w_up", "w_down", "conv_w")
BIG_ROWS = (7176, 1024, 512, 512, 512, 1024, 4096, 4096, 6)
R_BIG = 19456
SLAB_TILE = 1216
SMALL = ("norm1_g", "a_log", "dt_bias", "gdn_norm_g", "sb_q_norm_g", "sb_k_norm_g", "mem_norm_g",
         "mem_q_norm_g", "mem_k_norm_g", "norm2_g")
SMALL_ROWS = (8, 1, 1, 1, 1, 1, 8, 1, 1, 8)
R_SMALL = 32
WEIGHTS = ("norm1_g", "w_in", "conv_w", "a_log", "dt_bias", "gdn_norm_g", "sb_q_norm_g", "sb_k_norm_g",
           "mem_norm_g", "w_mem_kv", "mem_q_norm_g", "mem_k_norm_g", "w_br_gdn", "w_br_sb", "w_br_mem",
           "w_o", "norm2_g", "w_up", "w_down")


def _cp(sem=None):
    return pltpu.CompilerParams(dimension_semantics=sem, vmem_limit_bytes=VMEM_LIMIT)


def _dot(a, b, dims=NN):
    return lax.dot_general(a, b, dims, preferred_element_type=F32)


def _dbf(a, b, dims=NN):
    return _dot(a.astype(BF16), b.astype(BF16), dims)


def _split(a, n):
    parts = []
    for _ in range(n):
        h = a.astype(BF16)
        parts.append(h)
        a = a - h.astype(F32)
    return parts


def _dg(a, b, dims=NN):
    return _dbf(a, b, dims)


def _d3(a, b, dims=NN):
    ah, al = _split(a, 2)
    bh, bl = _split(b, 2)
    return _dot(ah, bh, dims) + (_dot(ah, bl, dims) + _dot(al, bh, dims))


def _dxr(a, e, dims=NN):
    eb = e.astype(BF16)
    a1, a2, a3 = _split(a, 3)
    return _dot(a1, eb, dims) + (_dot(a2, eb, dims) + _dot(a3, eb, dims))


def _dxl(e, a, dims=NN):
    eb = e.astype(BF16)
    a1, a2, a3 = _split(a, 3)
    return _dot(eb, a1, dims) + (_dot(eb, a2, dims) + _dot(eb, a3, dims))


def _sigmoid(x):
    return 1.0 / (1.0 + jnp.exp(-x))


def _softplus(x):
    return jnp.maximum(x, 0.0) + jnp.log(1.0 + jnp.exp(-jnp.abs(x)))


def _rms(x, g):
    r = lax.rsqrt(jnp.mean(x * x, axis=-1, keepdims=True) + EPS)
    return x * r * g, r


def _rms_bwd(dy, x, g, r):
    dyg = dy * g
    dx = r * (dyg - x * (r * r) * jnp.mean(dyg * x, axis=-1, keepdims=True))
    dg = jnp.sum(dy * (x * r), axis=0, keepdims=True)
    return dx, dg


def _hs(h):
    return slice(h * DH, (h + 1) * DH)


def _row_tile(s):
    return 512 if s >= 2048 else 256


def _narrow_tile(s):
    return min(256, s)


def _mm(name, a, b, mode, tm, tn, tk, pro=None, pro_g=None, epi=None, epi_x=None, out_dtype=F32, n_outer=False):
    if mode == "tn":
        K, M = a.shape
    else:
        M, K = a.shape
    N = b.shape[0] if mode == "nt" else b.shape[1]
    tm, tn, tk = min(tm, M), min(tn, N), min(tk, K)
    nk = K // tk
    assert M % tm == 0 and N % tn == 0 and K % tk == 0, (name, M, N, K, tm, tn, tk)
    dims = {"nn": NN, "nt": NT, "tn": TN}[mode]
    reducing = epi in ("rms_bwd", "loss")
    assert not reducing or (tn == N and not n_outer), name
    epi_ops = () if epi is None else (epi_x if isinstance(epi_x, tuple) else (epi_x,))

    def body(*refs):
        a_ref, b_ref = refs[0], refs[1]
        pos = 2
        g_ref = None
        if pro == "rms":
            g_ref = refs[pos]
            pos += 1
        e_refs = refs[pos:pos + len(epi_ops)]
        pos += len(epi_ops)
        o_ref = refs[pos]
        pos += 1
        r_ref = None
        if reducing:
            r_ref = refs[pos]
            pos += 1
        av = a_ref[...]
        if pro == "rms":
            av, _ = _rms(av.astype(F32), g_ref[...])
        elif pro == "relu2":
            av = jnp.square(jnp.maximum(av, 0.0))
        part = _dbf(av, b_ref[...], dims)
        first = pl.program_id(0) == 0

        def finish(acc):
            red = None
            if epi == "add":
                acc = acc + e_refs[0][...]
            elif epi == "drelu2":
                acc = acc * (2.0 * jnp.maximum(e_refs[0][...], 0.0))
            elif epi == "rms_bwd":
                xv, gv = e_refs[0][...], e_refs[1][...]
                _, r = _rms(xv, gv)
                dx, red = _rms_bwd(acc, xv, gv, r)
                acc = dx + e_refs[2][...]
            elif epi == "loss":
                err = acc + e_refs[0][...] - e_refs[1][...]
                acc = err * (1.0 / N)
                per_tok = jnp.sum(err * err, axis=1, keepdims=True) * (1.0 / N)
                red = 0.5 * jnp.sum(per_tok, axis=0, keepdims=True)
            o_ref[...] = acc.astype(out_dtype)
            if reducing:

                @pl.when(first)
                def _():
                    r_ref[...] = red

                @pl.when(jnp.logical_not(first))
                def _():
                    r_ref[...] += red

        if nk == 1:
            finish(part)
        else:
            acc_ref = refs[pos]
            k = pl.program_id(2)

            @pl.when(k == 0)
            def _():
                acc_ref[...] = part

            @pl.when(k > 0)
            def _():
                acc_ref[...] += part

            @pl.when(k == nk - 1)
            def _():
                finish(acc_ref[...])

    def spec(shape, index):
        if n_outer:
            return pl.BlockSpec(shape, lambda j, i, k: index(i, j, k))
        return pl.BlockSpec(shape, index)

    if mode == "tn":
        a_spec = spec((tk, tm), lambda i, j, k: (k, i))
    else:
        a_spec = spec((tm, tk), lambda i, j, k: (i, k))
    if mode == "nt":
        b_spec = spec((tn, tk), lambda i, j, k: (j, k))
    else:
        b_spec = spec((tk, tn), lambda i, j, k: (k, j))
    in_specs, ops = [a_spec, b_spec], [a, b]
    if pro == "rms":
        w = pro_g.shape[1]
        assert (tm if mode == "tn" else tk) == w, name
        in_specs.append(spec((1, w), lambda i, j, k: (0, 0)))
        ops.append(pro_g)
    for op in epi_ops:
        if op.shape[0] == 1:
            in_specs.append(spec((1, tn), lambda i, j, k: (0, j)))
        else:
            in_specs.append(spec((tm, tn), lambda i, j, k: (i, j)))
        ops.append(op)
    out_specs = [spec((tm, tn), lambda i, j, k: (i, j))]
    out_shape = [jax.ShapeDtypeStruct((M, N), out_dtype)]
    if reducing:
        width = N if epi == "rms_bwd" else 1
        out_specs.append(spec((1, width), lambda i, j, k: (0, 0)))
        out_shape.append(jax.ShapeDtypeStruct((1, width), F32))
    grid = (N // tn, M // tm, nk) if n_outer else (M // tm, N // tn, nk)
    outs = pl.pallas_call(
        body, name=name, grid=grid,
        in_specs=in_specs, out_specs=out_specs, out_shape=out_shape,
        scratch_shapes=[pltpu.VMEM((tm, tn), F32)] if nk > 1 else [],
        compiler_params=_cp(("arbitrary" if reducing else "parallel", "parallel", "arbitrary")),
    )(*ops)
    return outs if reducing else outs[0]


def _head_select(first_lane):
    l = lax.broadcasted_iota(jnp.int32, (LANES, HW), 0)
    c = lax.broadcasted_iota(jnp.int32, (LANES, HW), 1)
    return (l == first_lane + c // DH).astype(F32)


def _conv_taps(buf, cw, ts):
    c = cw[3:4, :] * buf[8:8 + ts, :]
    for j in range(3):
        k = 3 - j
        c = c + cw[j:j + 1, :] * buf[8 - k:8 - k + ts, :]
    return c


def _pre_fwd(proj, conv_w, alog_f, dtb_f, gsq, gsk, gmq, S):
    ts = _narrow_tile(S)
    hb = ts // 8

    def body(qkv_ref, halo_ref, ab_ref, sq_ref, sk_ref, sv_ref, mq_ref, cw_ref, al_ref, dt_ref, gsq_ref, gsk_ref,
             gmq_ref, gq_o, gk_o, gv_o, gf_o, bf_o, sqn_o, skn_o, svb_o, qmn_o, buf):
        i = pl.program_id(0)
        buf[0:8, :] = jnp.where(i == 0, 0.0, halo_ref[...])
        buf[8:8 + ts, :] = qkv_ref[...]
        c = _conv_taps(buf, cw_ref[...], ts)
        a = c * _sigmoid(c)
        for h in range(NH):
            q = a[:, h * DH:(h + 1) * DH]
            k = a[:, HW + h * DH:HW + (h + 1) * DH]
            gq_o[:, _hs(h)] = q * (lax.rsqrt(jnp.sum(q * q, axis=-1, keepdims=True) + EPS) * DH ** -0.5)
            gk_o[:, _hs(h)] = k * lax.rsqrt(jnp.sum(k * k, axis=-1, keepdims=True) + EPS)
            sqn_o[:, _hs(h)] = _rms(sq_ref[:, _hs(h)], gsq_ref[...])[0].astype(BF16)
            skn_o[:, _hs(h)] = _rms(sk_ref[:, _hs(h)], gsk_ref[...])[0].astype(BF16)
            qmn_o[:, _hs(h)] = _rms(mq_ref[:, _hs(h)], gmq_ref[...])[0].astype(BF16)
        gv_o[...] = a[:, 2 * HW:3 * HW]
        svb_o[...] = sv_ref[...].astype(BF16)
        ab = ab_ref[:, 0:LANES]
        a_bc = _dxr(ab, _head_select(0))
        b_bc = _dxr(ab, _head_select(NH))
        gf_o[...] = -jnp.exp(al_ref[...]) * _softplus(a_bc + dt_ref[...])
        bf_o[...] = _sigmoid(b_bc)

    row = lambda cb: pl.BlockSpec((ts, HW), lambda i: (i, cb))
    full = lambda r, c: pl.BlockSpec((r, c), lambda i: (0, 0))
    f32o = jax.ShapeDtypeStruct((S, HW), F32)
    bfo = jax.ShapeDtypeStruct((S, HW), BF16)
    return pl.pallas_call(
        body, name="pre_fwd", grid=(S // ts,),
        in_specs=[pl.BlockSpec((ts, 3 * HW), lambda i: (i, 0)),
                  pl.BlockSpec((8, 3 * HW), lambda i: (jnp.maximum(i * hb - 1, 0), 0)),
                  row(CB_AB), row(CB_SQ), row(CB_SK), row(CB_SV), row(CB_MQ),
                  full(4, 3 * HW), full(1, HW), full(1, HW), full(1, DH), full(1, DH), full(1, DH)],
        out_specs=[pl.BlockSpec((ts, HW), lambda i: (i, 0))] * 9,
        out_shape=[f32o, f32o, f32o, f32o, f32o, bfo, bfo, bfo, bfo],
        scratch_shapes=[pltpu.VMEM((ts + 8, 3 * HW), F32)],
        compiler_params=_cp(("parallel",)),
    )(proj, proj, proj, proj, proj, proj, proj, conv_w, alog_f, dtb_f, gsq, gsk, gmq)


def _pre_bwd(proj, conv_w, alog_f, dtb_f, gsq, gsk, dgq, dgk, dgv, dgf, dbf, dsqn, dskn, S):
    ts = _narrow_tile(S)
    hb = ts // 8

    def body(qkv_ref, halo_ref, ab_ref, sq_ref, sk_ref, cw_ref, al_ref, dt_ref, gsq_ref, gsk_ref,
             dgq_ref, dgk_ref, dgv_ref, dgf_ref, dbf_ref, dsqn_ref, dskn_ref,
             dc_o, dab_o, dsq_o, dsk_o, dcw_o, dal_o, ddt_o, dgsq_o, dgsk_o, buf):
        i = pl.program_id(0)

        @pl.when(i == 0)
        def _():
            dcw_o[...] = jnp.zeros_like(dcw_o)
            dal_o[...] = jnp.zeros_like(dal_o)
            ddt_o[...] = jnp.zeros_like(ddt_o)
            dgsq_o[...] = jnp.zeros_like(dgsq_o)
            dgsk_o[...] = jnp.zeros_like(dgsk_o)

        buf[0:8, :] = jnp.where(i == 0, 0.0, halo_ref[...])
        buf[8:8 + ts, :] = qkv_ref[...]
        c = _conv_taps(buf, cw_ref[...], ts)
        sg = _sigmoid(c)
        a = c * sg
        dsilu = sg * (1.0 + c * (1.0 - sg))
        dgsq = jnp.zeros((1, DH), F32)
        dgsk = jnp.zeros((1, DH), F32)
        for h in range(NH):
            q = a[:, h * DH:(h + 1) * DH]
            k = a[:, HW + h * DH:HW + (h + 1) * DH]
            nq = lax.rsqrt(jnp.sum(q * q, axis=-1, keepdims=True) + EPS)
            nk = lax.rsqrt(jnp.sum(k * k, axis=-1, keepdims=True) + EPS)
            dyq = dgq_ref[:, _hs(h)]
            dyk = dgk_ref[:, _hs(h)]
            dq = (nq * dyq - q * (nq * nq * nq) * jnp.sum(dyq * q, axis=-1, keepdims=True)) * DH ** -0.5
            dk = nk * dyk - k * (nk * nk * nk) * jnp.sum(dyk * k, axis=-1, keepdims=True)
            dc_o[:, h * DH:(h + 1) * DH] = dq * dsilu[:, h * DH:(h + 1) * DH]
            dc_o[:, HW + h * DH:HW + (h + 1) * DH] = dk * dsilu[:, HW + h * DH:HW + (h + 1) * DH]
            x = sq_ref[:, _hs(h)]
            _, r = _rms(x, gsq_ref[...])
            dx, dg = _rms_bwd(dsqn_ref[:, _hs(h)], x, gsq_ref[...], r)
            dsq_o[:, _hs(h)] = dx.astype(BF16)
            dgsq = dgsq + dg
            x = sk_ref[:, _hs(h)]
            _, r = _rms(x, gsk_ref[...])
            dx, dg = _rms_bwd(dskn_ref[:, _hs(h)], x, gsk_ref[...], r)
            dsk_o[:, _hs(h)] = dx.astype(BF16)
            dgsk = dgsk + dg
        dc_o[:, 2 * HW:3 * HW] = dgv_ref[...] * dsilu[:, 2 * HW:3 * HW]
        dgsq_o[...] += dgsq
        dgsk_o[...] += dgsk
        dc = dc_o[...]
        for j in range(4):
            k = 3 - j
            dcw_o[j:j + 1, :] += jnp.sum(dc * buf[8 - k:8 - k + ts, :], axis=0, keepdims=True)
        ab = ab_ref[:, 0:LANES]
        a_bc = _dxr(ab, _head_select(0))
        b_bc = _dxr(ab, _head_select(NH))
        pre = a_bc + dt_ref[...]
        ea = jnp.exp(al_ref[...])
        dgf = dgf_ref[...]
        dal_o[...] += jnp.sum(dgf * (-ea * _softplus(pre)), axis=0, keepdims=True)
        da = dgf * (-ea * _sigmoid(pre))
        ddt_o[...] += jnp.sum(da, axis=0, keepdims=True)
        beta = _sigmoid(b_bc)
        db = dbf_ref[...] * beta * (1.0 - beta)
        lane = lax.broadcasted_iota(jnp.int32, (ts, LANES), 1)
        dab = jnp.zeros((ts, LANES), F32)
        for h in range(NH):
            dab = dab + jnp.where(lane == h, da[:, _hs(h)], 0.0) + jnp.where(lane == NH + h, db[:, _hs(h)], 0.0)
        dab_o[:, 0:LANES] = dab.astype(BF16)
        dab_o[:, LANES:HW] = jnp.zeros((ts, HW - LANES), BF16)

    row = lambda cb: pl.BlockSpec((ts, HW), lambda i: (i, cb))
    full = lambda r, c: pl.BlockSpec((r, c), lambda i: (0, 0))
    t512 = pl.BlockSpec((ts, HW), lambda i: (i, 0))
    return pl.pallas_call(
        body, name="pre_bwd", grid=(S // ts,),
        in_specs=[pl.BlockSpec((ts, 3 * HW), lambda i: (i, 0)),
                  pl.BlockSpec((8, 3 * HW), lambda i: (jnp.maximum(i * hb - 1, 0), 0)),
                  row(CB_AB), row(CB_SQ), row(CB_SK),
                  full(4, 3 * HW), full(1, HW), full(1, HW), full(1, DH), full(1, DH)] + [t512] * 7,
        out_specs=[pl.BlockSpec((ts, 3 * HW), lambda i: (i, 0)), t512, t512, t512,
                   full(4, 3 * HW), full(1, HW), full(1, HW), full(1, DH), full(1, DH)],
        out_shape=[jax.ShapeDtypeStruct((S, 3 * HW), F32)] + [jax.ShapeDtypeStruct((S, HW), BF16)] * 3
        + [jax.ShapeDtypeStruct((4, 3 * HW), F32), jax.ShapeDtypeStruct((1, HW), F32),
           jax.ShapeDtypeStruct((1, HW), F32), jax.ShapeDtypeStruct((1, DH), F32),
           jax.ShapeDtypeStruct((1, DH), F32)],
        scratch_shapes=[pltpu.VMEM((ts + 8, 3 * HW), F32)],
        compiler_params=_cp(("arbitrary",)),
    )(proj, proj, proj, proj, proj, conv_w, alog_f, dtb_f, gsq, gsk, dgq, dgk, dgv, dgf, dbf, dsqn, dskn)


def _conv_bwd(dc, conv_w, S):
    ts = _row_tile(S)
    hb = ts // 8
    n = S // ts

    def body(dc_ref, halo_ref, cw_ref, o_ref, buf):
        i = pl.program_id(0)
        buf[0:ts, :] = dc_ref[...]
        buf[ts:ts + 8, :] = jnp.where(i == n - 1, 0.0, halo_ref[...])
        cw = cw_ref[...]
        acc = cw[3:4, :] * buf[0:ts, :]
        for k in range(1, 4):
            acc = acc + cw[3 - k:4 - k, :] * buf[k:k + ts, :]
        o_ref[...] = acc.astype(BF16)

    return pl.pallas_call(
        body, name="conv_bwd", grid=(n,),
        in_specs=[pl.BlockSpec((ts, 3 * HW), lambda i: (i, 0)),
                  pl.BlockSpec((8, 3 * HW), lambda i: (jnp.minimum((i + 1) * hb, S // 8 - 1), 0)),
                  pl.BlockSpec((4, 3 * HW), lambda i: (0, 0))],
        out_specs=pl.BlockSpec((ts, 3 * HW), lambda i: (i, 0)),
        out_shape=jax.ShapeDtypeStruct((S, 3 * HW), BF16),
        scratch_shapes=[pltpu.VMEM((ts + 8, 3 * HW), F32)],
        compiler_params=_cp(("parallel",)),
    )(dc, dc, conv_w)


def _gdn_masks():
    r = lax.broadcasted_iota(jnp.int32, (PAIR, PAIR), 0)
    c = lax.broadcasted_iota(jnp.int32, (PAIR, PAIR), 1)
    same = ((r >= CHUNK) & (c >= CHUNK)) | ((r < CHUNK) & (c < CHUNK))
    return dict(r=r, same=same, tril=same & (r >= c), strict=same & (r > c), triu=same & (c >= r), eye=r == c,
                in_a=r < CHUNK, last_a=r == CHUNK - 1, last_b=r == PAIR - 1)


def _each(fn, *cols):
    return [fn(*xs) for xs in zip(*cols)]


def _mul(a, b):
    return a * b


def _top(x):
    return x[:CHUNK]


def _bot(x):
    return x[CHUNK:]


def _rows(a, b):
    return jnp.concatenate([a, b], axis=0)


def _tri_inv(lm, eye):
    eye_f = eye.astype(F32)
    p = _each(lambda l: eye_f - l, lm)
    lp = _each(lambda l: _d3(l, l), lm)
    for it in range(5):
        p = _each(lambda a, b: a + _d3(a, b), p, lp)
        if it < 4:
            lp = _each(lambda b: _d3(b, b), lp)
    return p


def _gdn_block(m, q, k, v, g, beta):
    tril_f = m["tril"].astype(F32)
    col_sum = lambda mask: (lambda x: jnp.sum(jnp.where(mask, x, 0.0), axis=0, keepdims=True))
    gc = _each(lambda x: _dxl(tril_f, x), g)
    gcr = _each(col_sum(m["eye"]), gc)
    gam = _each(lambda a, b: jnp.where(m["tril"], jnp.exp(jnp.minimum(a - b, 0.0)), 0.0), gc, gcr)
    kb = _each(_mul, k, beta)
    vb = _each(_mul, v, beta)
    lm = _each(lambda a, b, c: jnp.where(m["strict"], _dg(a, b, NT) * c, 0.0), kb, k, gam)
    t = _tri_inv(lm, m["eye"])
    eg = _each(jnp.exp, gc)
    kbe = _each(_mul, kb, eg)
    u = _each(_d3, t, vb)
    w = _each(_d3, t, kbe)
    aqk = _each(lambda a, b, c: jnp.where(m["tril"], _dg(a, b, NT) * c, 0.0), q, k, gam)
    qd = _each(_mul, q, eg)
    ga = _each(col_sum(m["last_a"]), gc)
    gb = _each(col_sum(m["last_b"]), gc)
    e2 = _each(lambda a, b, c: jnp.exp(jnp.where(m["in_a"], a, b) - c), ga, gb, gc)
    kd = _each(_mul, k, e2)
    return dict(u=u, w=w, aqk=aqk, qd=qd, kd=kd, gam=gam, kb=kb, vb=vb, lm=lm, t=t, eg=eg, kbe=kbe, e2=e2,
                gla=_each(jnp.exp, ga), glb=_each(jnp.exp, gb))


def _gdn_fwd(gq, gk, gv, gf, bf, S):
    nb = S // PAIR

    def body(q_ref, k_ref, v_ref, g_ref, b_ref, o_ref, st_ref, s_scr):
        @pl.when(pl.program_id(0) == 0)
        def _():
            s_scr[...] = jnp.zeros_like(s_scr)

        m = _gdn_masks()
        heads = lambda ref: [ref[:, _hs(h)] for h in range(NH)]
        f = _gdn_block(m, heads(q_ref), heads(k_ref), heads(v_ref), heads(g_ref), heads(b_ref))
        u, w, qd, kd = f["u"], f["w"], f["qd"], f["kd"]
        s0 = [s_scr[h * DH:(h + 1) * DH, :] for h in range(NH)]
        vna = _each(lambda a, b, s: _top(a) - _dg(_top(b), s), u, w, s0)
        oa = _each(lambda a, s: _dg(_top(a), s), qd, s0)
        s1 = _each(lambda s, gl, a, vn: s * gl + _dg(_top(a), vn, TN), s0, f["gla"], kd, vna)
        vnb = _each(lambda a, b, s: _bot(a) - _dg(_bot(b), s), u, w, s1)
        ob = _each(lambda a, s: _dg(_bot(a), s), qd, s1)
        s2 = _each(lambda s, gl, a, vn: s * gl + _dg(_bot(a), vn, TN), s1, f["glb"], kd, vnb)
        outs = _each(lambda a, b, c, va, vb: _rows(a, b) + _dg(c, _rows(va, vb)), oa, ob, f["aqk"], vna, vnb)
        o_ref[...] = jnp.concatenate(outs, axis=1)
        st_ref[...] = jnp.concatenate(s0 + s1, axis=0)
        s_scr[...] = jnp.concatenate(s2, axis=0)

    blk = pl.BlockSpec((PAIR, HW), lambda i: (i, 0))
    return pl.pallas_call(
        body, name="gdn_fwd", grid=(nb,),
        in_specs=[blk] * 5,
        out_specs=[blk, pl.BlockSpec((2 * NH * DH, DH), lambda i: (i, 0))],
        out_shape=[jax.ShapeDtypeStruct((S, HW), F32), jax.ShapeDtypeStruct((nb * 2 * NH * DH, DH), F32)],
        scratch_shapes=[pltpu.VMEM((NH * DH, DH), F32)],
        compiler_params=_cp(("arbitrary",)),
    )(gq, gk, gv, gf, bf)


def _gdn_bwd(gq, gk, gv, gf, bf, states, do, S):
    nb = S // PAIR

    def body(q_ref, k_ref, v_ref, g_ref, b_ref, st_ref, do_ref, dq_o, dk_o, dv_o, dg_o, db_o, ds_scr):
        @pl.when(pl.program_id(0) == 0)
        def _():
            ds_scr[...] = jnp.zeros_like(ds_scr)

        m = _gdn_masks()
        ones = jnp.ones((PAIR, PAIR), F32)
        heads = lambda ref: [ref[:, _hs(h)] for h in range(NH)]
        q, k, v, beta, do = heads(q_ref), heads(k_ref), heads(v_ref), heads(b_ref), heads(do_ref)
        f = _gdn_block(m, q, k, v, heads(g_ref), beta)
        u, w, aqk, qd, kd, t = f["u"], f["w"], f["aqk"], f["qd"], f["kd"], f["t"]
        s0 = [st_ref[h * DH:(h + 1) * DH, :] for h in range(NH)]
        s1 = [st_ref[(NH + h) * DH:(NH + h + 1) * DH, :] for h in range(NH)]
        ds2 = [ds_scr[h * DH:(h + 1) * DH, :] for h in range(NH)]
        total = lambda a, b: jnp.sum(jnp.sum(a * b, axis=1, keepdims=True), axis=0, keepdims=True)
        vna = _each(lambda a, b, s: _top(a) - _dg(_top(b), s), u, w, s0)
        vnb = _each(lambda a, b, s: _bot(a) - _dg(_bot(b), s), u, w, s1)
        dvn_i = _each(lambda a, b: _dg(a, b, TN), aqk, do)
        dvnb = _each(lambda a, b, s: _bot(a) + _dg(_bot(b), s), dvn_i, kd, ds2)
        dqdb = _each(lambda a, s: _dg(_bot(a), s, NT), do, s1)
        dkdb = _each(lambda a, s: _dg(a, s, NT), vnb, ds2)
        dglb = _each(total, ds2, s1)
        dwb = _each(lambda a, s: -_dg(a, s, NT), dvnb, s1)
        ds1 = _each(lambda s, gl, a, b, c, d: s * gl + _dg(_bot(a), _bot(b), TN) - _dg(_bot(c), d, TN),
                    ds2, f["glb"], qd, do, w, dvnb)
        dvna = _each(lambda a, b, s: _top(a) + _dg(_top(b), s), dvn_i, kd, ds1)
        dqda = _each(lambda a, s: _dg(_top(a), s, NT), do, s0)
        dkda = _each(lambda a, s: _dg(a, s, NT), vna, ds1)
        dgla = _each(total, ds1, s0)
        dwa = _each(lambda a, s: -_dg(a, s, NT), dvna, s0)
        ds0 = _each(lambda s, gl, a, b, c, d: s * gl + _dg(_top(a), _top(b), TN) - _dg(_top(c), d, TN),
                    ds1, f["gla"], qd, do, w, dvna)
        dvn, dqd, dkd, dw = (_each(_rows, a, b) for a, b in ((dvna, dvnb), (dqda, dqdb), (dkda, dkdb), (dwa, dwb)))
        daqk = _each(lambda a, va, vb: jnp.where(m["tril"], _dg(a, _rows(va, vb), NT), 0.0), do, vna, vnb)
        dt = _each(lambda a, b, c, d: _dg(a, b, NT) + _dg(c, d, NT), dvn, f["vb"], dw, f["kbe"])
        dvb = _each(lambda a, b: _dg(a, b, TN), t, dvn)
        dkbe = _each(lambda a, b: _dg(a, b, TN), t, dw)
        dtt = _each(lambda a, b: _d3(a, b, NT), dt, t)
        dl = _each(lambda a, b: -jnp.where(m["strict"], _d3(a, b, TN), 0.0), t, dtt)
        dm = _each(_mul, dl, f["gam"])
        dn = _each(_mul, daqk, f["gam"])
        dkb = _each(lambda a, b, c, d: _dg(a, b) + c * d, dm, k, dkbe, f["eg"])
        dks = _each(lambda a, b, c, d, e, g, h, i: _dg(a, b, TN) + _dg(c, d, TN) + e * g + h * i,
                    dm, f["kb"], dn, q, dkd, f["e2"], beta, dkb)
        dqs = _each(lambda a, b, c, d: _dg(a, b) + c * d, dn, k, dqd, f["eg"])
        gm = _each(lambda a, b, c, d: a * b + c * d, dl, f["lm"], daqk, aqk)
        dkdkd = _each(_mul, dkd, kd)
        dgc = _each(lambda a, b, c, d, e, g: _dxr(a + b * c + d * e - g, ones) - _dxr(a, ones, TN),
                    gm, dqd, qd, dkbe, f["kbe"], dkdkd)
        same_f = m["same"].astype(F32)
        chunk_tot = _each(lambda a: _dxl(same_f, _dxr(a, ones)), dkdkd)
        last = m["last_a"] | m["last_b"]
        dgc = _each(lambda a, b, ga, gla, gb, glb: a + jnp.where(last, b + jnp.where(m["in_a"], ga * gla, gb * glb), 0.0),
                    dgc, chunk_tot, dgla, f["gla"], dglb, f["glb"])
        dbs = _each(lambda a, b, c, d: _dxr(a * b + c * d, ones), dkb, k, dvb, v)
        dvs = _each(_mul, beta, dvb)
        triu_f = m["triu"].astype(F32)
        dgs = _each(lambda a: _dxl(triu_f, a), dgc)
        for ref, parts in ((dq_o, dqs), (dk_o, dks), (dv_o, dvs), (dg_o, dgs), (db_o, dbs)):
            ref[...] = jnp.concatenate(parts, axis=1)
        ds_scr[...] = jnp.concatenate(ds0, axis=0)

    blk = pl.BlockSpec((PAIR, HW), lambda i: (nb - 1 - i, 0))
    o = jax.ShapeDtypeStruct((S, HW), F32)
    return pl.pallas_call(
        body, name="gdn_bwd", grid=(nb,),
        in_specs=[blk] * 5 + [pl.BlockSpec((2 * NH * DH, DH), lambda i: (nb - 1 - i, 0)), blk],
        out_specs=[blk] * 5, out_shape=[o] * 5,
        scratch_shapes=[pltpu.VMEM((NH * DH, DH), F32)],
        compiler_params=_cp(("arbitrary",)),
    )(gq, gk, gv, gf, bf, states, do)


SB_T = 256
SB_GROUP = 4
SB_GROUP_BWD = 4
SB_SINGLES = 1
SB_DEAD = -110.0


def _group_sizes(g):
    sizes = []
    while g >= 1:
        sizes.append(g)
        g //= 2
    return sizes


def _sb_iotas(t):
    return lax.broadcasted_iota(jnp.int32, (t, t), 0), lax.broadcasted_iota(jnp.int32, (t, t), 1)


def _sb_scores(q, k, mask):
    z = _dot(q, k, NT) * DH ** -0.5
    ls = jnp.minimum(z, 0.0) - jnp.log(1.0 + jnp.exp(-jnp.abs(z)))
    lneg = ls - z
    if mask is not None:
        lneg = jnp.where(mask, lneg, 0.0)
    return ls, lneg


def _prefix(x, u):
    xh, xl = _split(x, 2)
    return _dot(xh, u) + _dot(xl, u)


def _sb_fwd(sqn, skn, svb, S):
    t = min(SB_T, S)

    def body(q_ref, k_ref, v_ref, o_ref, t_ref, cnt_ref):
        qb = pl.program_id(1)
        q = q_ref[...]
        r, c = _sb_iotas(t)
        diag = c < r
        u_after = (r > c).astype(BF16)

        def tiles(k0s, run, masks):
            sc = _each(lambda k0, m: _sb_scores(q, k_ref[pl.ds(k0, t), :], m), k0s, masks)
            ls, lneg = [s[0] for s in sc], [s[1] for s in sc]
            sums = _each(lambda x: jnp.sum(x, axis=1, keepdims=True), lneg)
            pre = _each(lambda x: _prefix(x, u_after), lneg)
            runs = [run]
            for s in sums:
                runs.append(runs[-1] + s)
            att = _each(lambda a, b, rn: jnp.exp(a + (rn + b)), ls, pre, runs[:-1])
            att = _each(lambda a, m: a if m is None else jnp.where(m, a, 0.0), att, masks)
            parts = _each(lambda a, k0: _dot(a.astype(BF16), v_ref[pl.ds(k0, t), :]), att, k0s)
            return sum(parts[1:], parts[0]), runs[-1]

        left = jnp.full((t, t), qb > 0)
        acc, run = tiles([pl.multiple_of(qb * t, t), pl.multiple_of(jnp.maximum(qb - 1, 0) * t, t)],
                         jnp.zeros((t, 1), F32), [diag, left])

        def alive(run):
            return jnp.max(run) >= SB_DEAD

        carry, done = (0, acc, run, alive(run)), jnp.minimum(qb, 1)
        for size, limit in [(1, SB_SINGLES)] + [(s, None) for s in _group_sizes(SB_GROUP)]:

            def more(c, size=size, done=done, limit=limit):
                i, _, _, go = c
                fits = done + (i + 1) * size <= qb
                return (fits if limit is None else fits & (i < limit)) & go

            def group(c, size=size, done=done):
                i, acc, run, _ = c
                first = qb - 1 - done - size * i
                part, run = tiles([pl.multiple_of((first - j) * t, t) for j in range(size)], run, [None] * size)
                return i + 1, acc + part, run, alive(run)

            n, acc, run, go = lax.while_loop(more, group, (0,) + carry[1:])
            carry, done = (0, acc, run, go), done + n * size
        o_ref[...] = acc.astype(BF16)
        t_ref[...] = jnp.broadcast_to(run, (t, DH))
        cnt_ref[pl.program_id(0), qb] = done

    qspec = pl.BlockSpec((t, DH), lambda h, i: (i, h))
    kspec = pl.BlockSpec((S, DH), lambda h, i: (0, h))
    return pl.pallas_call(
        body, name="sb_fwd", grid=(NH, S // t),
        in_specs=[qspec, kspec, kspec],
        out_specs=[qspec, qspec, pl.BlockSpec(memory_space=pltpu.SMEM)],
        out_shape=[jax.ShapeDtypeStruct((S, HW), BF16), jax.ShapeDtypeStruct((S, HW), F32),
                   jax.ShapeDtypeStruct((NH, S // t), jnp.int32)],
        compiler_params=_cp(("arbitrary", "arbitrary")),
    )(sqn, skn, svb)


def _sb_bwd(sqn, skn, svb, do, tot, walked, S):
    t = min(SB_T, S)

    def body(cnt_ref, q_ref, k_ref, v_ref, do_ref, t_ref, dq_o, dk_o, dv_o, dv_acc):
        qb = pl.program_id(1)

        @pl.when(qb == 0)
        def _():
            dk_o[...] = jnp.zeros_like(dk_o)
            dv_acc[...] = jnp.zeros_like(dv_acc)

        q = q_ref[...]
        do = do_ref[...].astype(BF16)
        tot_l = jnp.concatenate([t_ref[...]] * (t // DH), axis=1)
        r, c = _sb_iotas(t)
        diag = c < r
        u_upto = (r <= c).astype(BF16)
        u_before = (r < c).astype(BF16)

        def tiles(k0s, run_l, run_e, masks):
            rowsum = lambda x: jnp.sum(x, axis=1, keepdims=True)
            masked = lambda xs: _each(lambda a, m: a if m is None else jnp.where(m, a, 0.0), xs, masks)
            ks = [k_ref[pl.ds(k0, t), :] for k0 in k0s]
            vs = [v_ref[pl.ds(k0, t), :] for k0 in k0s]
            sc = _each(lambda k, m: _sb_scores(q, k, m), ks, masks)
            ls, lneg = [s[0] for s in sc], [s[1] for s in sc]
            sums_l = _each(rowsum, lneg)
            pre_l = _each(lambda x: _prefix(x, u_upto), lneg)
            runs_l = [run_l]
            for s in sums_l:
                runs_l.append(runs_l[-1] + s)
            att = masked(_each(lambda a, b, rn: jnp.exp(a + (tot_l - (rn + b))), ls, pre_l, runs_l[:-1]))
            e = _each(lambda v, a: _dot(do, v, NT) * a, vs, att)
            sums_e = _each(rowsum, e)
            pre_e = _each(lambda x: _prefix(x, u_before), e)
            runs_e = [run_e]
            for s in sums_e:
                runs_e.append(runs_e[-1] + s)
            sg = _each(jnp.exp, ls)
            dz = masked(_each(lambda a, b, rn, s: a * (1.0 - s) - (rn + b) * s, e, pre_e, runs_e[:-1], sg))
            dz = _each(lambda a: (a * DH ** -0.5).astype(BF16), dz)
            dvs = _each(lambda a: _dot(a.astype(BF16), do, TN), att)
            dks = _each(lambda a: _dot(a, q, TN), dz)
            dqs = _each(_dot, dz, ks)
            for k0, dv, dk in zip(k0s, dvs, dks):
                dv_acc[pl.ds(k0, t), :] += dv
                dk_o[pl.ds(k0, t), :] += dk
            return sum(dqs[1:], dqs[0]), runs_l[-1], runs_e[-1]

        walked = cnt_ref[pl.program_id(0), qb]
        early = jnp.maximum(walked - 1, 0)
        z1 = jnp.zeros((t, 1), F32)
        carry, done = (jnp.zeros((t, DH), F32), z1, z1), 0
        for size in _group_sizes(SB_GROUP_BWD):
            n = (early - done) // size

            def group(i, carry, size=size, done=done):
                dq, run_l, run_e = carry
                first = qb - walked + done + size * i
                part, run_l, run_e = tiles([pl.multiple_of((first + j) * t, t) for j in range(size)], run_l, run_e,
                                           [None] * size)
                return dq + part, run_l, run_e

            carry = lax.fori_loop(0, n, group, carry)
            done = done + n * size
        dq, run_l, run_e = carry
        left = jnp.full((t, t), qb > 0)
        part, _, _ = tiles([pl.multiple_of(jnp.maximum(qb - 1, 0) * t, t), pl.multiple_of(qb * t, t)], run_l, run_e,
                           [left, diag])
        dq_o[...] = dq + part

        @pl.when(qb == S // t - 1)
        def _():
            dv_o[...] = dv_acc[...].astype(BF16)

    qspec = pl.BlockSpec((t, DH), lambda h, i, cnt: (i, h))
    kspec = pl.BlockSpec((S, DH), lambda h, i, cnt: (0, h))
    o = jax.ShapeDtypeStruct((S, HW), F32)
    return pl.pallas_call(
        body, name="sb_bwd",
        grid_spec=pltpu.PrefetchScalarGridSpec(
            num_scalar_prefetch=1, grid=(NH, S // t),
            in_specs=[qspec, kspec, kspec, qspec, qspec], out_specs=[qspec, kspec, kspec],
            scratch_shapes=[pltpu.VMEM((S, DH), F32)]),
        out_shape=[o, o, jax.ShapeDtypeStruct((S, HW), BF16)],
        compiler_params=_cp(("parallel", "arbitrary")),
    )(walked, sqn, skn, svb, do, tot)


def _mem_probs(qn, kn):
    s = _dot(qn, kn.astype(BF16), NT) * DH ** -0.5
    p = jnp.exp(s - jnp.max(s, axis=-1, keepdims=True))
    return p / jnp.sum(p, axis=-1, keepdims=True)


def _mem_fwd(qmn, kv, gmk, S):
    ts = _row_tile(S)

    def body(q_ref, kv_ref, gk_ref, o_ref):
        for h in range(NH):
            kn, _ = _rms(kv_ref[:, _hs(h)], gk_ref[...])
            p = _mem_probs(q_ref[:, _hs(h)], kn)
            o_ref[:, _hs(h)] = _dbf(p, kv_ref[:, HW + h * DH:HW + (h + 1) * DH]).astype(BF16)

    return pl.pallas_call(
        body, name="mem_fwd", grid=(S // ts,),
        in_specs=[pl.BlockSpec((ts, HW), lambda i: (i, 0)), pl.BlockSpec((NMEM, 2 * HW), lambda i: (0, 0)),
                  pl.BlockSpec((1, DH), lambda i: (0, 0))],
        out_specs=pl.BlockSpec((ts, HW), lambda i: (i, 0)),
        out_shape=jax.ShapeDtypeStruct((S, HW), BF16),
        compiler_params=_cp(("parallel",)),
    )(qmn, kv, gmk)


def _mem_bwd(proj, qmn, kv, gmq, gmk, do, S):
    ts = _row_tile(S)
    n = S // ts

    def body(mq_ref, q_ref, kv_ref, gq_ref, gk_ref, do_ref, dmq_o, dkv_o, dgq_o, dgk_o, dkn_scr):
        i = pl.program_id(0)

        @pl.when(i == 0)
        def _():
            dkv_o[...] = jnp.zeros_like(dkv_o)
            dgq_o[...] = jnp.zeros_like(dgq_o)
            dkn_scr[...] = jnp.zeros_like(dkn_scr)

        dgq = jnp.zeros((1, DH), F32)
        for h in range(NH):
            km = kv_ref[:, _hs(h)]
            vm = kv_ref[:, HW + h * DH:HW + (h + 1) * DH].astype(BF16)
            kn, _ = _rms(km, gk_ref[...])
            qn = q_ref[:, _hs(h)]
            p = _mem_probs(qn, kn)
            dob = do_ref[:, _hs(h)].astype(BF16)
            dkv_o[:, HW + h * DH:HW + (h + 1) * DH] += _dot(p.astype(BF16), dob, TN)
            dp = _dot(dob, vm, NT)
            dsc = (p * (dp - jnp.sum(dp * p, axis=-1, keepdims=True)) * DH ** -0.5).astype(BF16)
            dkn_scr[:, _hs(h)] += _dot(dsc, qn, TN)
            x = mq_ref[:, _hs(h)]
            _, r = _rms(x, gq_ref[...])
            dx, dg = _rms_bwd(_dot(dsc, kn.astype(BF16)), x, gq_ref[...], r)
            dmq_o[:, _hs(h)] = dx.astype(BF16)
            dgq = dgq + dg
        dgq_o[...] += dgq

        @pl.when(i == n - 1)
        def _():
            dgk = jnp.zeros((1, DH), F32)
            for h in range(NH):
                km = kv_ref[:, _hs(h)]
                _, r = _rms(km, gk_ref[...])
                dx, dg = _rms_bwd(dkn_scr[:, _hs(h)], km, gk_ref[...], r)
                dkv_o[:, _hs(h)] = dx
                dgk = dgk + dg
            dgk_o[...] = dgk

    full = lambda r, c: pl.BlockSpec((r, c), lambda i: (0, 0))
    t512 = pl.BlockSpec((ts, HW), lambda i: (i, 0))
    return pl.pallas_call(
        body, name="mem_bwd", grid=(n,),
        in_specs=[pl.BlockSpec((ts, HW), lambda i: (i, CB_MQ)), t512, full(NMEM, 2 * HW), full(1, DH), full(1, DH),
                  t512],
        out_specs=[t512, full(NMEM, 2 * HW), full(1, DH), full(1, DH)],
        out_shape=[jax.ShapeDtypeStruct((S, HW), BF16), jax.ShapeDtypeStruct((NMEM, 2 * HW), F32),
                   jax.ShapeDtypeStruct((1, DH), F32), jax.ShapeDtypeStruct((1, DH), F32)],
        scratch_shapes=[pltpu.VMEM((NMEM, HW), F32)],
        compiler_params=_cp(("arbitrary",)),
    )(proj, qmn, kv, gmq, gmk, do)


def _gated_gdn(o, z, g):
    sg = _sigmoid(z)
    outs, rs = [], []
    for h in range(NH):
        y, r = _rms(o[:, _hs(h)], g)
        outs.append(y * (z[:, _hs(h)] * sg[:, _hs(h)]))
        rs.append(r)
    return jnp.concatenate(outs, axis=1), rs, sg


def _merge_fwd(x, proj, ogdn, osb, omem, ggdn, wbg, wbs, wbm, wo, S):
    ts = _narrow_tile(S)

    def body(x_ref, z_ref, g0_ref, g1_ref, g2_ref, og_ref, os_ref, om_ref, gg_ref, wbg_ref, wbs_ref, wbm_ref,
             wo_ref, x1_o, mix_o):
        on, _, _ = _gated_gdn(og_ref[...], z_ref[...], gg_ref[...])
        mix = (_sigmoid(g0_ref[...]) * _dbf(on, wbg_ref[...]) + _sigmoid(g1_ref[...]) * _dbf(os_ref[...], wbs_ref[...])
               + _sigmoid(g2_ref[...]) * _dbf(om_ref[...], wbm_ref[...]))
        mix_o[...] = mix.astype(BF16)
        x1_o[...] = x_ref[...] + _dbf(mix, wo_ref[...])

    t512 = pl.BlockSpec((ts, HW), lambda i: (i, 0))
    t1k = pl.BlockSpec((ts, D), lambda i: (i, 0))
    gate = lambda j: pl.BlockSpec((ts, D), lambda i: (i, 4 + j))
    full = lambda r, c: pl.BlockSpec((r, c), lambda i: (0, 0))
    return pl.pallas_call(
        body, name="merge_fwd", grid=(S // ts,),
        in_specs=[t1k, pl.BlockSpec((ts, HW), lambda i: (i, CB_Z)), gate(0), gate(1), gate(2), t512, t512, t512,
                  full(1, DH), full(HW, D), full(HW, D), full(HW, D), full(D, D)],
        out_specs=[t1k, t1k],
        out_shape=[jax.ShapeDtypeStruct((S, D), F32), jax.ShapeDtypeStruct((S, D), BF16)],
        compiler_params=_cp(("parallel",)),
    )(x, proj, proj, proj, proj, ogdn, osb, omem, ggdn, wbg, wbs, wbm, wo)


def _merge_bwd(dmix, proj, ogdn, osb, omem, ggdn, wbg, wbs, wbm, S):
    ts = _narrow_tile(S)

    def body(dm_ref, z_ref, g0_ref, g1_ref, g2_ref, og_ref, os_ref, om_ref, gg_ref, wbg_ref, wbs_ref, wbm_ref,
             dgl0_o, dgl1_o, dgl2_o, dog_o, dz_o, dos_o, dom_o, dwbg_o, dwbs_o, dwbm_o, dgg_o):
        @pl.when(pl.program_id(0) == 0)
        def _():
            for ref in (dwbg_o, dwbs_o, dwbm_o, dgg_o):
                ref[...] = jnp.zeros_like(ref)

        dm = dm_ref[...]
        og = og_ref[...]
        z = z_ref[...]
        on, rs, sg = _gated_gdn(og, z, gg_ref[...])
        branch = ((on, g0_ref, wbg_ref, dgl0_o, dwbg_o), (os_ref[...], g1_ref, wbs_ref, dgl1_o, dwbs_o),
                  (om_ref[...], g2_ref, wbm_ref, dgl2_o, dwbm_o))
        dos = []
        for o, g_ref, w_ref, dgl_o, dw_o in branch:
            ob = o.astype(BF16)
            gate = _sigmoid(g_ref[...])
            dgl_o[...] = (dm * _dot(ob, w_ref[...]) * gate * (1.0 - gate)).astype(BF16)
            dy = (dm * gate).astype(BF16)
            dw_o[...] += _dot(ob, dy, TN)
            dos.append(_dot(dy, w_ref[...], NT))
        dos_o[...] = dos[1].astype(BF16)
        dom_o[...] = dos[2].astype(BF16)
        don = dos[0]
        dgg = jnp.zeros((1, DH), F32)
        for h in range(NH):
            oh, zh, sh = og[:, _hs(h)], z[:, _hs(h)], sg[:, _hs(h)]
            y = oh * rs[h] * gg_ref[...]
            dz_o[:, _hs(h)] = (don[:, _hs(h)] * y * (sh * (1.0 + zh * (1.0 - sh)))).astype(BF16)
            dx, dg = _rms_bwd(don[:, _hs(h)] * (zh * sh), oh, gg_ref[...], rs[h])
            dog_o[:, _hs(h)] = dx
            dgg = dgg + dg
        dgg_o[...] += dgg

    t512 = pl.BlockSpec((ts, HW), lambda i: (i, 0))
    t1k = pl.BlockSpec((ts, D), lambda i: (i, 0))
    gate = lambda j: pl.BlockSpec((ts, D), lambda i: (i, 4 + j))
    full = lambda r, c: pl.BlockSpec((r, c), lambda i: (0, 0))
    s1k = jax.ShapeDtypeStruct((S, D), BF16)
    s512 = jax.ShapeDtypeStruct((S, HW), BF16)
    wsh = jax.ShapeDtypeStruct((HW, D), F32)
    return pl.pallas_call(
        body, name="merge_bwd", grid=(S // ts,),
        in_specs=[t1k, pl.BlockSpec((ts, HW), lambda i: (i, CB_Z)), gate(0), gate(1), gate(2), t512, t512, t512,
                  full(1, DH), full(HW, D), full(HW, D), full(HW, D)],
        out_specs=[t1k, t1k, t1k, t512, t512, t512, t512, full(HW, D), full(HW, D), full(HW, D), full(1, DH)],
        out_shape=[s1k, s1k, s1k, jax.ShapeDtypeStruct((S, HW), F32), s512, s512, s512, wsh, wsh, wsh,
                   jax.ShapeDtypeStruct((1, DH), F32)],
        compiler_params=_cp(("arbitrary",)),
    )(dmix, proj, proj, proj, proj, ogdn, osb, omem, ggdn, wbg, wbs, wbm)


def _norm_bwd(name, dh, x, g, res):
    rows = x.shape[0]
    ts = min(_row_tile(rows), rows)

    def body(*refs):
        dh_ref, x_ref, g_ref = refs[:3]
        dx_o, dg_o = refs[-2:]

        @pl.when(pl.program_id(0) == 0)
        def _():
            dg_o[...] = jnp.zeros_like(dg_o)

        xv = x_ref[...]
        _, r = _rms(xv, g_ref[...])
        dx, dg = _rms_bwd(dh_ref[...], xv, g_ref[...], r)
        dx_o[...] = dx if res is None else dx + refs[3][...]
        dg_o[...] += dg

    t1k = pl.BlockSpec((ts, D), lambda i: (i, 0))
    gsp = pl.BlockSpec((1, D), lambda i: (0, 0))
    ops = [dh, x, g] + ([] if res is None else [res])
    return pl.pallas_call(
        body, name=name, grid=(rows // ts,), in_specs=[t1k, t1k, gsp] + ([] if res is None else [t1k]),
        out_specs=[t1k, gsp],
        out_shape=[jax.ShapeDtypeStruct((rows, D), F32), jax.ShapeDtypeStruct((1, D), F32)],
        compiler_params=_cp(("arbitrary",)),
    )(*ops)


def _adamw(name, gall, w, m, v):
    rows = w.shape[0]
    nsrc = gall.shape[0]
    tr = min(SLAB_TILE, rows)
    assert rows % tr == 0

    def body(g_ref, w_ref, m_ref, v_ref, g_o, d_o, m_o, v_o):
        g = g_ref[0].astype(F32)
        for j in range(1, nsrc):
            g = g + g_ref[j].astype(F32)
        m_new = ADAM_B1 * m_ref[...] + (1.0 - ADAM_B1) * g
        v_new = ADAM_B2 * v_ref[...] + (1.0 - ADAM_B2) * jnp.square(g)
        m_hat = m_new / (1.0 - ADAM_B1 ** ADAM_STEP)
        v_hat = v_new / (1.0 - ADAM_B2 ** ADAM_STEP)
        g_o[...] = g
        d_o[...] = -ADAM_LR * (m_hat / (jnp.sqrt(v_hat) + ADAM_EPS) + ADAM_WD * w_ref[...])
        m_o[...] = m_new
        v_o[...] = v_new

    t = pl.BlockSpec((tr, LANES), lambda i: (i, 0))
    o = jax.ShapeDtypeStruct((rows, LANES), F32)
    return pl.pallas_call(
        body, name=name, grid=(rows // tr,),
        in_specs=[pl.BlockSpec((nsrc, tr, LANES), lambda i: (0, i, 0)), t, t, t],
        out_specs=[t, t, t, t], out_shape=[o, o, o, o],
        compiler_params=_cp(("parallel",)),
    )(gall, w, m, v)


def _pair_sum(mine, theirs):
    rows = mine.shape[1]
    tr = min(SLAB_TILE, rows)
    assert rows % tr == 0
    core = lax.axis_index("c").astype(jnp.int32).reshape(1)

    def body(c_ref, a_ref, b_ref, o_ref):
        o_ref[...] = (a_ref[...].astype(F32) + b_ref[...].astype(F32)).astype(o_ref.dtype)

    blk = pl.BlockSpec((1, tr, LANES), lambda j, i, c_ref: (j, i, 0))
    return pl.pallas_call(
        body, name="pair_sum",
        grid_spec=pltpu.PrefetchScalarGridSpec(
            num_scalar_prefetch=1, grid=(NDEV // 2, rows // tr),
            in_specs=[pl.BlockSpec((1, tr, LANES), lambda j, i, c_ref: (2 * j + c_ref[0], i, 0)), blk],
            out_specs=blk),
        out_shape=jax.ShapeDtypeStruct((NDEV // 2, rows, LANES), mine.dtype),
        compiler_params=_cp(("parallel", "parallel")),
    )(core, mine, theirs)


HBM_SPEC = pl.BlockSpec(memory_space=pltpu.HBM)


def _remote(src, dst, send_sems, recv_sems, k, to):
    return pltpu.make_async_remote_copy(src_ref=src, dst_ref=dst, send_sem=send_sems.at[k], recv_sem=recv_sems.at[k],
                                        device_id=to, device_id_type=pl.DeviceIdType.MESH)


def _gather(name, x):
    rows, cols = x.shape

    def body(x_ref, o_ref, send_sems, recv_sems, local_sem):
        ix, iy, ic = lax.axis_index("x"), lax.axis_index("y"), lax.axis_index("c")
        me, sibling = (ix, iy, ic), (ix, iy, 1 - ic)
        chips = [(1 - ix, iy), (ix, 1 - iy), (1 - ix, 1 - iy)]

        def slab(px, py, pc):
            return o_ref.at[4 * px + 2 * py + pc]

        def copy(k, block, to, src=None):
            return _remote(slab(*block) if src is None else src, slab(*block), send_sems, recv_sems, k, to)

        mine = pltpu.make_async_copy(x_ref, slab(*me), local_sem)
        mine.start()
        first = [copy(0, me, sibling, src=x_ref)]
        first += [copy(1 + j, me, (*chip, ic), src=x_ref) for j, chip in enumerate(chips)]
        for cp in first:
            cp.start()
        passed = [copy(4 + j, (*chip, ic), sibling) for j, chip in enumerate(chips)]
        for j, chip in enumerate(chips):
            copy(1 + j, (*chip, ic), me).wait_recv()
            passed[j].start()
        copy(0, sibling, me).wait_recv()
        for j, chip in enumerate(chips):
            copy(4 + j, (*chip, 1 - ic), me).wait_recv()
        for cp in first + passed:
            cp.wait_send()
        mine.wait()

    return pl.pallas_call(
        body, name=name, in_specs=[HBM_SPEC], out_specs=HBM_SPEC,
        out_shape=jax.ShapeDtypeStruct((NDEV, rows, cols), x.dtype),
        scratch_shapes=[pltpu.SemaphoreType.DMA((NDEV - 1,)), pltpu.SemaphoreType.DMA((NDEV - 1,)),
                        pltpu.SemaphoreType.DMA],
    )(x)


def _sibling_exchange(name, x):
    rows, cols = x.shape[-2:]
    nchip = NDEV // 2

    def body(x_ref, o_ref, send_sems, recv_sems):
        ix, iy, ic = lax.axis_index("x"), lax.axis_index("y"), lax.axis_index("c")
        copies = [_remote(x_ref.at[2 * j + (1 - ic)], o_ref.at[j], send_sems, recv_sems, j, (ix, iy, 1 - ic))
                  for j in range(nchip)]
        for cp in copies:
            cp.start()
        for cp in copies:
            cp.wait()

    return pl.pallas_call(
        body, name=name, in_specs=[HBM_SPEC], out_specs=HBM_SPEC,
        out_shape=jax.ShapeDtypeStruct((nchip, rows, cols), x.dtype),
        scratch_shapes=[pltpu.SemaphoreType.DMA((nchip,)), pltpu.SemaphoreType.DMA((nchip,))],
    )(x)


def _chip_exchange(name, x):
    rows, cols = x.shape[-2:]
    nchip = NDEV // 2

    def body(x_ref, o_ref, send_sems, recv_sems, local_sem):
        ix, iy, ic = lax.axis_index("x"), lax.axis_index("y"), lax.axis_index("c")
        my_chip = 2 * ix + iy
        own = pltpu.make_async_copy(x_ref.at[my_chip], o_ref.at[my_chip], local_sem)
        own.start()
        copies = []
        for k in range(1, nchip):
            px, py = ix ^ (k >> 1), iy ^ (k & 1)
            copies.append(_remote(x_ref.at[2 * px + py], o_ref.at[my_chip], send_sems, recv_sems, k - 1, (px, py, ic)))
        for cp in copies:
            cp.start()
        for cp in copies:
            cp.wait()
        own.wait()

    return pl.pallas_call(
        body, name=name, in_specs=[HBM_SPEC], out_specs=HBM_SPEC,
        out_shape=jax.ShapeDtypeStruct((nchip, rows, cols), x.dtype),
        scratch_shapes=[pltpu.SemaphoreType.DMA((nchip - 1,)), pltpu.SemaphoreType.DMA((nchip - 1,)),
                        pltpu.SemaphoreType.DMA],
    )(x)


COL_SHARDED = {"w_in": (D, D_IN), "w_br_gdn": (HW, D), "w_br_sb": (HW, D), "w_br_mem": (HW, D), "w_up": (D, DFF),
               "conv_w": (4, 3 * HW)}
ROW_SHARDED = {"w_mem_kv": (D, 2 * HW), "w_o": (D, D), "w_down": (DFF, D)}


def _to_slab(p):
    return p.reshape(p.shape[:-2] + (-1, LANES))


def _from_slab(flat, r, c):
    return flat.reshape(flat.shape[:-2] + (r, c))


def _shard_dims(name):
    if name in COL_SHARDED:
        r, c = COL_SHARDED[name]
        return r, c // NDEV
    r, c = ROW_SHARDED[name]
    return r // NDEV, c


def _pack_rows(parts, total):
    flat = jnp.concatenate(parts, axis=-2)
    return jnp.pad(flat, [(0, 0)] * (flat.ndim - 2) + [(0, total - flat.shape[-2]), (0, 0)])


def _pack_shards(vals):
    return _pack_rows([_to_slab(vals[n][0]) for n in BIG], R_BIG)


def _pack_full_grads(grads):
    parts = []
    for name in BIG:
        g = grads[name]
        r, c = _shard_dims(name)
        if name in COL_SHARDED:
            g = g.reshape(r, NDEV, c).transpose(1, 0, 2)
        else:
            g = g.reshape(NDEV, r, c)
        parts.append(_to_slab(g))
    return _pack_rows(parts, R_BIG)


def _unpack_gathered(slabs):
    out, pos = {}, 0
    for name, rows in zip(BIG, BIG_ROWS):
        r, c = _shard_dims(name)
        g = _from_slab(slabs[:, pos:pos + rows], r, c)
        pos += rows
        if name in COL_SHARDED:
            out[name] = g.transpose(1, 0, 2).reshape(r, NDEV * c)
        else:
            out[name] = g.reshape(NDEV * r, c)
    return out


def _unpack_shard(flat, shapes):
    out, pos = {}, 0
    for name, rows in zip(BIG, BIG_ROWS):
        r, c = _shard_dims(name)
        out[name] = _from_slab(flat[pos:pos + rows], r, c).reshape(shapes[name])
        pos += rows
    return out


def _pack_small(vals):
    rows = []
    for name, n in zip(SMALL, SMALL_ROWS):
        v = vals[name].reshape(-1)
        rows.append(jnp.pad(v, (0, n * LANES - v.shape[0])).reshape(n, LANES))
    return _pack_rows(rows, R_SMALL)


def _unpack_small(flat, shapes):
    out, pos = {}, 0
    for name, n in zip(SMALL, SMALL_ROWS):
        size = shapes[name][-1]
        out[name] = flat[pos:pos + n].reshape(-1)[:size].reshape(shapes[name])
        pos += n
    return out


def _pad_w_in(w):
    return jnp.concatenate([w[:, :2048], w[:, 2056:], w[:, 2048:2056], jnp.zeros((D, D_INP - D_IN), w.dtype)], axis=1)


def _unpad_w_in(w):
    return jnp.concatenate([w[:, :2048], w[:, 7168:7176], w[:, 2048:7168]], axis=1)


def _per_head(v):
    return jnp.repeat(v.reshape(NH), DH).reshape(1, HW)


def _local_step(x, mem, target, w, sm):
    S = x.shape[0]
    ts = _row_tile(S)
    alog_f, dtb_f = _per_head(sm["a_log"]), _per_head(sm["dt_bias"])

    proj = _mm("in_proj", x, w["w_in"], "nn", ts, 1536, D, pro="rms", pro_g=sm["norm1_g"], n_outer=True)
    gq, gk, gv, gf, bf, sqn, skn, svb, qmn = _pre_fwd(proj, w["conv_w"], alog_f, dtb_f, sm["sb_q_norm_g"],
                                                      sm["sb_k_norm_g"], sm["mem_q_norm_g"], S)
    ogdn, states = _gdn_fwd(gq, gk, gv, gf, bf, S)
    osb, sb_tot, sb_walked = _sb_fwd(sqn, skn, svb, S)
    kv = _mm("mem_kv", mem, w["w_mem_kv"], "nn", NMEM, D, D, pro="rms", pro_g=sm["mem_norm_g"])
    omem = _mem_fwd(qmn, kv, sm["mem_k_norm_g"], S)
    x1, mix = _merge_fwd(x, proj, ogdn, osb, omem, sm["gdn_norm_g"], w["w_br_gdn"], w["w_br_sb"], w["w_br_mem"],
                         w["w_o"], S)
    up = _mm("mlp_up", x1, w["w_up"], "nn", ts, 2048, D, pro="rms", pro_g=sm["norm2_g"], n_outer=True)
    dy, loss = _mm("mlp_down", up, w["w_down"], "nn", ts, D, 1024, pro="relu2", epi="loss", epi_x=(x1, target))

    g = {}
    dup = _mm("d_up", dy, w["w_down"], "nt", ts, 1024, D, epi="drelu2", epi_x=up, out_dtype=BF16)
    g["w_down"] = _mm("dw_down", up, dy, "tn", 1024, D, 512, pro="relu2")
    g["w_up"] = _mm("dw_up", x1, dup, "tn", D, 1024, 512, pro="rms", pro_g=sm["norm2_g"])
    dx1, g["norm2_g"] = _mm("d_h2", dup, w["w_up"], "nt", ts, D, 1024, epi="rms_bwd", epi_x=(x1, sm["norm2_g"], dy))

    dmix = _mm("d_mix", dx1, w["w_o"], "nt", ts, D, D)
    g["w_o"] = _mm("dw_o", mix, dx1, "tn", D, D, 512)
    (dgl0, dgl1, dgl2, dogdn, dz, dosb, domem, g["w_br_gdn"], g["w_br_sb"], g["w_br_mem"],
     g["gdn_norm_g"]) = _merge_bwd(dmix, proj, ogdn, osb, omem, sm["gdn_norm_g"], w["w_br_gdn"], w["w_br_sb"],
                                   w["w_br_mem"], S)
    dmq, dkv, g["mem_q_norm_g"], g["mem_k_norm_g"] = _mem_bwd(proj, qmn, kv, sm["mem_q_norm_g"], sm["mem_k_norm_g"],
                                                             domem, S)
    g["w_mem_kv"] = _mm("dw_mem_kv", mem, dkv, "tn", D, D, NMEM, pro="rms", pro_g=sm["mem_norm_g"])
    dmn = _mm("d_mem_n", dkv, w["w_mem_kv"], "nt", NMEM, D, D)
    _, g["mem_norm_g"] = _norm_bwd("mem_norm_bwd", dmn, mem, sm["mem_norm_g"], None)
    dsqn, dskn, dsv = _sb_bwd(sqn, skn, svb, dosb, sb_tot, sb_walked, S)
    dgq, dgk, dgv, dgf, dbf = _gdn_bwd(gq, gk, gv, gf, bf, states, dogdn, S)
    dc, dab, dsq, dsk, g["conv_w"], dal_f, ddt_f, g["sb_q_norm_g"], g["sb_k_norm_g"] = _pre_bwd(
        proj, w["conv_w"], alog_f, dtb_f, sm["sb_q_norm_g"], sm["sb_k_norm_g"], dgq, dgk, dgv, dgf, dbf, dsqn, dskn, S)
    g["a_log"] = dal_f.reshape(NH, DH)[:, 0].reshape(1, NH)
    g["dt_bias"] = ddt_f.reshape(NH, DH)[:, 0].reshape(1, NH)
    dqkv = _conv_bwd(dc, w["conv_w"], S)

    dproj = jnp.concatenate([dqkv, dz, dsq, dsk, dsv, dmq, dgl0, dgl1, dgl2, dab], axis=1)
    g["w_in"] = _mm("dw_in", x, dproj, "tn", D, 1536, 512, pro="rms", pro_g=sm["norm1_g"])
    dx, g["norm1_g"] = _mm("d_h", dproj, w["w_in"], "nt", ts, D, 1536, epi="rms_bwd", epi_x=(x, sm["norm1_g"], dx1))
    return loss[0, 0], dx, g


def kernel(x, mem, norm1_g, w_in, conv_w, a_log, dt_bias, gdn_norm_g, sb_q_norm_g, sb_k_norm_g, mem_norm_g, w_mem_kv, mem_q_norm_g, mem_k_norm_g, w_br_gdn, w_br_sb, w_br_mem, w_o, norm2_g, w_up, w_down, loss_target, m_norm1_g, m_w_in, m_conv_w, m_a_log, m_dt_bias, m_gdn_norm_g, m_sb_q_norm_g, m_sb_k_norm_g, m_mem_norm_g, m_w_mem_kv, m_mem_q_norm_g, m_mem_k_norm_g, m_w_br_gdn, m_w_br_sb, m_w_br_mem, m_w_o, m_norm2_g, m_w_up, m_w_down, v_norm1_g, v_w_in, v_conv_w, v_a_log, v_dt_bias, v_gdn_norm_g, v_sb_q_norm_g, v_sb_k_norm_g, v_mem_norm_g, v_w_mem_kv, v_mem_q_norm_g, v_mem_k_norm_g, v_w_br_gdn, v_w_br_sb, v_w_br_mem, v_w_o, v_norm2_g, v_w_up, v_w_down):
    given = dict(norm1_g=norm1_g, w_in=w_in, conv_w=conv_w, a_log=a_log, dt_bias=dt_bias, gdn_norm_g=gdn_norm_g,
                 sb_q_norm_g=sb_q_norm_g, sb_k_norm_g=sb_k_norm_g, mem_norm_g=mem_norm_g, w_mem_kv=w_mem_kv,
                 mem_q_norm_g=mem_q_norm_g, mem_k_norm_g=mem_k_norm_g, w_br_gdn=w_br_gdn, w_br_sb=w_br_sb,
                 w_br_mem=w_br_mem, w_o=w_o, norm2_g=norm2_g, w_up=w_up, w_down=w_down)
    mom1 = dict(norm1_g=m_norm1_g, w_in=m_w_in, conv_w=m_conv_w, a_log=m_a_log, dt_bias=m_dt_bias,
                gdn_norm_g=m_gdn_norm_g, sb_q_norm_g=m_sb_q_norm_g, sb_k_norm_g=m_sb_k_norm_g,
                mem_norm_g=m_mem_norm_g, w_mem_kv=m_w_mem_kv, mem_q_norm_g=m_mem_q_norm_g,
                mem_k_norm_g=m_mem_k_norm_g, w_br_gdn=m_w_br_gdn, w_br_sb=m_w_br_sb, w_br_mem=m_w_br_mem, w_o=m_w_o,
                norm2_g=m_norm2_g, w_up=m_w_up, w_down=m_w_down)
    mom2 = dict(norm1_g=v_norm1_g, w_in=v_w_in, conv_w=v_conv_w, a_log=v_a_log, dt_bias=v_dt_bias,
                gdn_norm_g=v_gdn_norm_g, sb_q_norm_g=v_sb_q_norm_g, sb_k_norm_g=v_sb_k_norm_g,
                mem_norm_g=v_mem_norm_g, w_mem_kv=v_w_mem_kv, mem_q_norm_g=v_mem_q_norm_g,
                mem_k_norm_g=v_mem_k_norm_g, w_br_gdn=v_w_br_gdn, w_br_sb=v_w_br_sb, w_br_mem=v_w_br_mem, w_o=v_w_o,
                norm2_g=v_norm2_g, w_up=v_w_up, w_down=v_w_down)
    shapes = {n: given[n].shape for n in WEIGHTS}

    w_loc = _pack_shards(given)
    gathered = _gather("gather_weights", w_loc.astype(BF16))
    w = _unpack_gathered(gathered[:, :sum(BIG_ROWS)])
    w["w_in"] = _pad_w_in(w["w_in"])
    conv_loc = jnp.pad(given["conv_w"][0].reshape(-1, LANES), ((0, 2), (0, 0)))
    conv_all = _gather("gather_conv", conv_loc)
    w["conv_w"] = conv_all[:, :6].reshape(NDEV, 4, 3 * HW // NDEV).transpose(1, 0, 2).reshape(4, 3 * HW)
    sm = {n: given[n] for n in SMALL}

    loss, dx, g = _local_step(x[0], mem[0], loss_target[0], w, sm)
    g["w_in"] = _unpad_w_in(g["w_in"])

    g_mine = _pack_full_grads(g).astype(BF16)
    g_pair = _pair_sum(g_mine, _sibling_exchange("scatter_sibling", g_mine))
    g_all = _chip_exchange("scatter_chips", g_pair)
    gb, db, mb, vb = _adamw("adamw_sharded", g_all, w_loc, _pack_shards(mom1), _pack_shards(mom2))
    gs_all = _gather("gather_small_grads", _pack_small(g))
    gs, dsm, ms, vs = _adamw("adamw_replicated", gs_all, _pack_small(given), _pack_small(mom1), _pack_small(mom2))

    outs = {}
    for prefix, big, small in (("grad_", gb, gs), ("delta_", db, dsm), ("new_m_", mb, ms), ("new_v_", vb, vs)):
        vals = _unpack_shard(big, shapes)
        vals.update(_unpack_small(small, shapes))
        for n in WEIGHTS:
            outs[prefix + n] = vals[n]
    loss = lax.psum(loss, ("x", "y", "c"))
    return (loss, dx[None], *[outs[p + n] for p in ("grad_", "delta_", "new_m_", "new_v_") for n in WEIGHTS])
```

```python
import jax
import jax.numpy as jnp
from jax import lax
from jax.experimental import pallas as pl
from jax.experimental.pallas import tpu as pltpu

F32 = jnp.float32
BF16 = jnp.bfloat16

D = 1024
NH = 4
DH = 128
HW = NH * DH
DFF = 4 * D
NMEM = 256
EPS = 1e-6
NDEV = 8
LANES = 128
PAIR = 128
CHUNK = 64
D_IN = 7176
D_INP = 7680
VMEM_LIMIT = 56 * 1024 * 1024

ADAM_LR, ADAM_B1, ADAM_B2, ADAM_EPS, ADAM_WD, ADAM_STEP = 0.001, 0.9, 0.999, 1e-08, 0.01, 10

CB_Z, CB_SQ, CB_SK, CB_SV, CB_MQ, CB_AB = 3, 4, 5, 6, 7, 14

NN = (((1,), (0,)), ((), ()))
NT = (((1,), (1,)), ((), ()))
TN = (((0,), (0,)), ((), ()))

BIG = ("w_in", "w_mem_kv", "w_br_gdn", "w_br_sb", "w_br_mem", "w_o", "w_up", "w_down", "conv_w")
BIG_ROWS = (7176, 1024, 512, 512, 512, 1024, 4096, 4096, 6)
R_BIG = 19456
SLAB_TILE = 1216
SMALL = ("norm1_g", "a_log", "dt_bias", "gdn_norm_g", "sb_q_norm_g", "sb_k_norm_g", "mem_norm_g",
         "mem_q_norm_g", "mem_k_norm_g", "norm2_g")
SMALL_ROWS = (8, 1, 1, 1, 1, 1, 8, 1, 1, 8)
R_SMALL = 32
WEIGHTS = ("norm1_g", "w_in", "conv_w", "a_log", "dt_bias", "gdn_norm_g", "sb_q_norm_g", "sb_k_norm_g",
           "mem_norm_g", "w_mem_kv", "mem_q_norm_g", "mem_k_norm_g", "w_br_gdn", "w_br_sb", "w_br_mem",
           "w_o", "norm2_g", "w_up", "w_down")


def _cp(sem=None):
    return pltpu.CompilerParams(dimension_semantics=sem, vmem_limit_bytes=VMEM_LIMIT)


def _dot(a, b, dims=NN):
    return lax.dot_general(a, b, dims, preferred_element_type=F32)


def _dbf(a, b, dims=NN):
    return _dot(a.astype(BF16), b.astype(BF16), dims)


def _split(a, n):
    parts = []
    for _ in range(n):
        h = a.astype(BF16)
        parts.append(h)
        a = a - h.astype(F32)
    return parts


def _dg(a, b, dims=NN):
    return _dbf(a, b, dims)


def _dxr(a, e, dims=NN):
    eb = e.astype(BF16)
    a1, a2, a3 = _split(a, 3)
    return _dot(a1, eb, dims) + (_dot(a2, eb, dims) + _dot(a3, eb, dims))


def _dxl(e, a, dims=NN):
    eb = e.astype(BF16)
    a1, a2, a3 = _split(a, 3)
    return _dot(eb, a1, dims) + (_dot(eb, a2, dims) + _dot(eb, a3, dims))


def _sigmoid(x):
    return 1.0 / (1.0 + jnp.exp(-x))


def _softplus(x):
    return jnp.maximum(x, 0.0) + jnp.log(1.0 + jnp.exp(-jnp.abs(x)))


def _rms(x, g):
    r = lax.rsqrt(jnp.mean(x * x, axis=-1, keepdims=True) + EPS)
    return x * r * g, r


def _rms_bwd(dy, x, g, r):
    dyg = dy * g
    dx = r * (dyg - x * (r * r) * jnp.mean(dyg * x, axis=-1, keepdims=True))
    dg = jnp.sum(dy * (x * r), axis=0, keepdims=True)
    return dx, dg


def _hs(h):
    return slice(h * DH, (h + 1) * DH)


def _row_tile(s):
    return 512 if s >= 2048 else 256


def _narrow_tile(s):
    return min(256, s)


def _mm(name, a, b, mode, tm, tn, tk, pro=None, pro_g=None, epi=None, epi_x=None, out_dtype=F32, n_outer=False):
    if mode == "tn":
        K, M = a.shape
    else:
        M, K = a.shape
    N = b.shape[0] if mode == "nt" else b.shape[1]
    tm, tn, tk = min(tm, M), min(tn, N), min(tk, K)
    nk = K // tk
    assert M % tm == 0 and N % tn == 0 and K % tk == 0, (name, M, N, K, tm, tn, tk)
    dims = {"nn": NN, "nt": NT, "tn": TN}[mode]
    reducing = epi in ("rms_bwd", "loss")
    assert not reducing or (tn == N and not n_outer), name
    epi_ops = () if epi is None else (epi_x if isinstance(epi_x, tuple) else (epi_x,))

    def body(*refs):
        a_ref, b_ref = refs[0], refs[1]
        pos = 2
        g_ref = None
        if pro == "rms":
            g_ref = refs[pos]
            pos += 1
        e_refs = refs[pos:pos + len(epi_ops)]
        pos += len(epi_ops)
        o_ref = refs[pos]
        pos += 1
        r_ref = None
        if reducing:
            r_ref = refs[pos]
            pos += 1
        av = a_ref[...]
        if pro == "rms":
            av, _ = _rms(av.astype(F32), g_ref[...])
        elif pro == "relu2":
            av = jnp.square(jnp.maximum(av, 0.0))
        part = _dbf(av, b_ref[...], dims)
        first = pl.program_id(0) == 0

        def finish(acc):
            red = None
            if epi == "add":
                acc = acc + e_refs[0][...]
            elif epi == "drelu2":
                acc = acc * (2.0 * jnp.maximum(e_refs[0][...], 0.0))
            elif epi == "rms_bwd":
                xv, gv = e_refs[0][...], e_refs[1][...]
                _, r = _rms(xv, gv)
                dx, red = _rms_bwd(acc, xv, gv, r)
                acc = dx + e_refs[2][...]
            elif epi == "loss":
                err = acc + e_refs[0][...] - e_refs[1][...]
                acc = err * (1.0 / N)
                per_tok = jnp.sum(err * err, axis=1, keepdims=True) * (1.0 / N)
                red = 0.5 * jnp.sum(per_tok, axis=0, keepdims=True)
            o_ref[...] = acc.astype(out_dtype)
            if reducing:

                @pl.when(first)
                def _():
                    r_ref[...] = red

                @pl.when(jnp.logical_not(first))
                def _():
                    r_ref[...] += red

        if nk == 1:
            finish(part)
        else:
            acc_ref = refs[pos]
            k = pl.program_id(2)

            @pl.when(k == 0)
            def _():
                acc_ref[...] = part

            @pl.when(k > 0)
            def _():
                acc_ref[...] += part

            @pl.when(k == nk - 1)
            def _():
                finish(acc_ref[...])

    def spec(shape, index):
        if n_outer:
            return pl.BlockSpec(shape, lambda j, i, k: index(i, j, k))
        return pl.BlockSpec(shape, index)

    if mode == "tn":
        a_spec = spec((tk, tm), lambda i, j, k: (k, i))
    else:
        a_spec = spec((tm, tk), lambda i, j, k: (i, k))
    if mode == "nt":
        b_spec = spec((tn, tk), lambda i, j, k: (j, k))
    else:
        b_spec = spec((tk, tn), lambda i, j, k: (k, j))
    in_specs, ops = [a_spec, b_spec], [a, b]
    if pro == "rms":
        w = pro_g.shape[1]
        assert (tm if mode == "tn" else tk) == w, name
        in_specs.append(spec((1, w), lambda i, j, k: (0, 0)))
        ops.append(pro_g)
    for op in epi_ops:
        if op.shape[0] == 1:
            in_specs.append(spec((1, tn), lambda i, j, k: (0, j)))
        else:
            in_specs.append(spec((tm, tn), lambda i, j, k: (i, j)))
        ops.append(op)
    out_specs = [spec((tm, tn), lambda i, j, k: (i, j))]
    out_shape = [jax.ShapeDtypeStruct((M, N), out_dtype)]
    if reducing:
        width = N if epi == "rms_bwd" else 1
        out_specs.append(spec((1, width), lambda i, j, k: (0, 0)))
        out_shape.append(jax.ShapeDtypeStruct((1, width), F32))
    grid = (N // tn, M // tm, nk) if n_outer else (M // tm, N // tn, nk)
    outs = pl.pallas_call(
        body, name=name, grid=grid,
        in_specs=in_specs, out_specs=out_specs, out_shape=out_shape,
        scratch_shapes=[pltpu.VMEM((tm, tn), F32)] if nk > 1 else [],
        compiler_params=_cp(("arbitrary" if reducing else "parallel", "parallel", "arbitrary")),
    )(*ops)
    return outs if reducing else outs[0]


def _head_select(first_lane):
    l = lax.broadcasted_iota(jnp.int32, (LANES, HW), 0)
    c = lax.broadcasted_iota(jnp.int32, (LANES, HW), 1)
    return (l == first_lane + c // DH).astype(F32)


def _conv_taps(buf, cw, ts):
    c = cw[3:4, :] * buf[8:8 + ts, :]
    for j in range(3):
        k = 3 - j
        c = c + cw[j:j + 1, :] * buf[8 - k:8 - k + ts, :]
    return c


def _pre_fwd(proj, conv_w, alog_f, dtb_f, gsq, gsk, gmq, S):
    ts = _narrow_tile(S)
    hb = ts // 8

    def body(qkv_ref, halo_ref, ab_ref, sq_ref, sk_ref, sv_ref, mq_ref, cw_ref, al_ref, dt_ref, gsq_ref, gsk_ref,
             gmq_ref, gq_o, gk_o, gv_o, gf_o, bf_o, sqn_o, skn_o, svb_o, qmn_o, buf):
        i = pl.program_id(0)
        buf[0:8, :] = jnp.where(i == 0, 0.0, halo_ref[...])
        buf[8:8 + ts, :] = qkv_ref[...]
        c = _conv_taps(buf, cw_ref[...], ts)
        a = c * _sigmoid(c)
        for h in range(NH):
            q = a[:, h * DH:(h + 1) * DH]
            k = a[:, HW + h * DH:HW + (h + 1) * DH]
            gq_o[:, _hs(h)] = q * (lax.rsqrt(jnp.sum(q * q, axis=-1, keepdims=True) + EPS) * DH ** -0.5)
            gk_o[:, _hs(h)] = k * lax.rsqrt(jnp.sum(k * k, axis=-1, keepdims=True) + EPS)
            sqn_o[:, _hs(h)] = _rms(sq_ref[:, _hs(h)], gsq_ref[...])[0].astype(BF16)
            skn_o[:, _hs(h)] = _rms(sk_ref[:, _hs(h)], gsk_ref[...])[0].astype(BF16)
            qmn_o[:, _hs(h)] = _rms(mq_ref[:, _hs(h)], gmq_ref[...])[0].astype(BF16)
        gv_o[...] = a[:, 2 * HW:3 * HW]
        svb_o[...] = sv_ref[...].astype(BF16)
        ab = ab_ref[:, 0:LANES]
        a_bc = _dxr(ab, _head_select(0))
        b_bc = _dxr(ab, _head_select(NH))
        gf_o[...] = -jnp.exp(al_ref[...]) * _softplus(a_bc + dt_ref[...])
        bf_o[...] = _sigmoid(b_bc)

    row = lambda cb: pl.BlockSpec((ts, HW), lambda i: (i, cb))
    full = lambda r, c: pl.BlockSpec((r, c), lambda i: (0, 0))
    f32o = jax.ShapeDtypeStruct((S, HW), F32)
    bfo = jax.ShapeDtypeStruct((S, HW), BF16)
    return pl.pallas_call(
        body, name="pre_fwd", grid=(S // ts,),
        in_specs=[pl.BlockSpec((ts, 3 * HW), lambda i: (i, 0)),
                  pl.BlockSpec((8, 3 * HW), lambda i: (jnp.maximum(i * hb - 1, 0), 0)),
                  row(CB_AB), row(CB_SQ), row(CB_SK), row(CB_SV), row(CB_MQ),
                  full(4, 3 * HW), full(1, HW), full(1, HW), full(1, DH), full(1, DH), full(1, DH)],
        out_specs=[pl.BlockSpec((ts, HW), lambda i: (i, 0))] * 9,
        out_shape=[f32o, f32o, f32o, f32o, f32o, bfo, bfo, bfo, bfo],
        scratch_shapes=[pltpu.VMEM((ts + 8, 3 * HW), F32)],
        compiler_params=_cp(("parallel",)),
    )(proj, proj, proj, proj, proj, proj, proj, conv_w, alog_f, dtb_f, gsq, gsk, gmq)


def _pre_bwd(proj, conv_w, alog_f, dtb_f, gsq, gsk, dgq, dgk, dgv, dgf, dbf, dsqn, dskn, S):
    ts = _narrow_tile(S)
    hb = ts // 8

    def body(qkv_ref, halo_ref, ab_ref, sq_ref, sk_ref, cw_ref, al_ref, dt_ref, gsq_ref, gsk_ref,
             dgq_ref, dgk_ref, dgv_ref, dgf_ref, dbf_ref, dsqn_ref, dskn_ref,
             dc_o, dab_o, dsq_o, dsk_o, dcw_o, dal_o, ddt_o, dgsq_o, dgsk_o, buf):
        i = pl.program_id(0)

        @pl.when(i == 0)
        def _():
            dcw_o[...] = jnp.zeros_like(dcw_o)
            dal_o[...] = jnp.zeros_like(dal_o)
            ddt_o[...] = jnp.zeros_like(ddt_o)
            dgsq_o[...] = jnp.zeros_like(dgsq_o)
            dgsk_o[...] = jnp.zeros_like(dgsk_o)

        buf[0:8, :] = jnp.where(i == 0, 0.0, halo_ref[...])
        buf[8:8 + ts, :] = qkv_ref[...]
        c = _conv_taps(buf, cw_ref[...], ts)
        sg = _sigmoid(c)
        a = c * sg
        dsilu = sg * (1.0 + c * (1.0 - sg))
        dgsq = jnp.zeros((1, DH), F32)
        dgsk = jnp.zeros((1, DH), F32)
        for h in range(NH):
            q = a[:, h * DH:(h + 1) * DH]
            k = a[:, HW + h * DH:HW + (h + 1) * DH]
            nq = lax.rsqrt(jnp.sum(q * q, axis=-1, keepdims=True) + EPS)
            nk = lax.rsqrt(jnp.sum(k * k, axis=-1, keepdims=True) + EPS)
            dyq = dgq_ref[:, _hs(h)]
            dyk = dgk_ref[:, _hs(h)]
            dq = (nq * dyq - q * (nq * nq * nq) * jnp.sum(dyq * q, axis=-1, keepdims=True)) * DH ** -0.5
            dk = nk * dyk - k * (nk * nk * nk) * jnp.sum(dyk * k, axis=-1, keepdims=True)
            dc_o[:, h * DH:(h + 1) * DH] = dq * dsilu[:, h * DH:(h + 1) * DH]
            dc_o[:, HW + h * DH:HW + (h + 1) * DH] = dk * dsilu[:, HW + h * DH:HW + (h + 1) * DH]
            x = sq_ref[:, _hs(h)]
            _, r = _rms(x, gsq_ref[...])
            dx, dg = _rms_bwd(dsqn_ref[:, _hs(h)], x, gsq_ref[...], r)
            dsq_o[:, _hs(h)] = dx.astype(BF16)
            dgsq = dgsq + dg
            x = sk_ref[:, _hs(h)]
            _, r = _rms(x, gsk_ref[...])
            dx, dg = _rms_bwd(dskn_ref[:, _hs(h)], x, gsk_ref[...], r)
            dsk_o[:, _hs(h)] = dx.astype(BF16)
            dgsk = dgsk + dg
        dc_o[:, 2 * HW:3 * HW] = dgv_ref[...] * dsilu[:, 2 * HW:3 * HW]
        dgsq_o[...] += dgsq
        dgsk_o[...] += dgsk
        dc = dc_o[...]
        for j in range(4):
            k = 3 - j
            dcw_o[j:j + 1, :] += jnp.sum(dc * buf[8 - k:8 - k + ts, :], axis=0, keepdims=True)
        ab = ab_ref[:, 0:LANES]
        a_bc = _dxr(ab, _head_select(0))
        b_bc = _dxr(ab, _head_select(NH))
        pre = a_bc + dt_ref[...]
        ea = jnp.exp(al_ref[...])
        dgf = dgf_ref[...]
        dal_o[...] += jnp.sum(dgf * (-ea * _softplus(pre)), axis=0, keepdims=True)
        da = dgf * (-ea * _sigmoid(pre))
        ddt_o[...] += jnp.sum(da, axis=0, keepdims=True)
        beta = _sigmoid(b_bc)
        db = dbf_ref[...] * beta * (1.0 - beta)
        lane = lax.broadcasted_iota(jnp.int32, (ts, LANES), 1)
        dab = jnp.zeros((ts, LANES), F32)
        for h in range(NH):
            dab = dab + jnp.where(lane == h, da[:, _hs(h)], 0.0) + jnp.where(lane == NH + h, db[:, _hs(h)], 0.0)
        dab_o[:, 0:LANES] = dab.astype(BF16)
        dab_o[:, LANES:HW] = jnp.zeros((ts, HW - LANES), BF16)

    row = lambda cb: pl.BlockSpec((ts, HW), lambda i: (i, cb))
    full = lambda r, c: pl.BlockSpec((r, c), lambda i: (0, 0))
    t512 = pl.BlockSpec((ts, HW), lambda i: (i, 0))
    return pl.pallas_call(
        body, name="pre_bwd", grid=(S // ts,),
        in_specs=[pl.BlockSpec((ts, 3 * HW), lambda i: (i, 0)),
                  pl.BlockSpec((8, 3 * HW), lambda i: (jnp.maximum(i * hb - 1, 0), 0)),
                  row(CB_AB), row(CB_SQ), row(CB_SK),
                  full(4, 3 * HW), full(1, HW), full(1, HW), full(1, DH), full(1, DH)] + [t512] * 7,
        out_specs=[pl.BlockSpec((ts, 3 * HW), lambda i: (i, 0)), t512, t512, t512,
                   full(4, 3 * HW), full(1, HW), full(1, HW), full(1, DH), full(1, DH)],
        out_shape=[jax.ShapeDtypeStruct((S, 3 * HW), F32)] + [jax.ShapeDtypeStruct((S, HW), BF16)] * 3
        + [jax.ShapeDtypeStruct((4, 3 * HW), F32), jax.ShapeDtypeStruct((1, HW), F32),
           jax.ShapeDtypeStruct((1, HW), F32), jax.ShapeDtypeStruct((1, DH), F32),
           jax.ShapeDtypeStruct((1, DH), F32)],
        scratch_shapes=[pltpu.VMEM((ts + 8, 3 * HW), F32)],
        compiler_params=_cp(("arbitrary",)),
    )(proj, proj, proj, proj, proj, conv_w, alog_f, dtb_f, gsq, gsk, dgq, dgk, dgv, dgf, dbf, dsqn, dskn)


def _conv_bwd(dc, conv_w, S):
    ts = _row_tile(S)
    hb = ts // 8
    n = S // ts

    def body(dc_ref, halo_ref, cw_ref, o_ref, buf):
        i = pl.program_id(0)
        buf[0:ts, :] = dc_ref[...]
        buf[ts:ts + 8, :] = jnp.where(i == n - 1, 0.0, halo_ref[...])
        cw = cw_ref[...]
        acc = cw[3:4, :] * buf[0:ts, :]
        for k in range(1, 4):
            acc = acc + cw[3 - k:4 - k, :] * buf[k:k + ts, :]
        o_ref[...] = acc.astype(BF16)

    return pl.pallas_call(
        body, name="conv_bwd", grid=(n,),
        in_specs=[pl.BlockSpec((ts, 3 * HW), lambda i: (i, 0)),
                  pl.BlockSpec((8, 3 * HW), lambda i: (jnp.minimum((i + 1) * hb, S // 8 - 1), 0)),
                  pl.BlockSpec((4, 3 * HW), lambda i: (0, 0))],
        out_specs=pl.BlockSpec((ts, 3 * HW), lambda i: (i, 0)),
        out_shape=jax.ShapeDtypeStruct((S, 3 * HW), BF16),
        scratch_shapes=[pltpu.VMEM((ts + 8, 3 * HW), F32)],
        compiler_params=_cp(("parallel",)),
    )(dc, dc, conv_w)


def _gdn_masks():
    r = lax.broadcasted_iota(jnp.int32, (PAIR, PAIR), 0)
    c = lax.broadcasted_iota(jnp.int32, (PAIR, PAIR), 1)
    same = ((r >= CHUNK) & (c >= CHUNK)) | ((r < CHUNK) & (c < CHUNK))
    return dict(r=r, same=same, tril=same & (r >= c), strict=same & (r > c), triu=same & (c >= r), eye=r == c,
                in_a=r < CHUNK, last_a=r == CHUNK - 1, last_b=r == PAIR - 1)


def _each(fn, *cols):
    return [fn(*xs) for xs in zip(*cols)]


def _mul(a, b):
    return a * b


def _top(x):
    return x[:CHUNK]


def _bot(x):
    return x[CHUNK:]


def _rows(a, b):
    return jnp.concatenate([a, b], axis=0)


def _tri_inv(lm, eye):
    eye_f = eye.astype(F32)
    p = _each(lambda l: eye_f - l, lm)
    lp = _each(lambda l: _dg(l, l), lm)
    for it in range(5):
        p = _each(lambda a, b: a + _dg(a, b), p, lp)
        if it < 4:
            lp = _each(lambda b: _dg(b, b), lp)
    return p


def _gdn_block(m, q, k, v, g, beta):
    tril_f = m["tril"].astype(F32)
    col_sum = lambda mask: (lambda x: jnp.sum(jnp.where(mask, x, 0.0), axis=0, keepdims=True))
    gc = _each(lambda x: _dxl(tril_f, x), g)
    gcr = _each(col_sum(m["eye"]), gc)
    gam = _each(lambda a, b: jnp.where(m["tril"], jnp.exp(jnp.minimum(a - b, 0.0)), 0.0), gc, gcr)
    kb = _each(_mul, k, beta)
    vb = _each(_mul, v, beta)
    lm = _each(lambda a, b, c: jnp.where(m["strict"], _dg(a, b, NT) * c, 0.0), kb, k, gam)
    t = _tri_inv(lm, m["eye"])
    eg = _each(jnp.exp, gc)
    kbe = _each(_mul, kb, eg)
    u = _each(_dg, t, vb)
    w = _each(_dg, t, kbe)
    aqk = _each(lambda a, b, c: jnp.where(m["tril"], _dg(a, b, NT) * c, 0.0), q, k, gam)
    qd = _each(_mul, q, eg)
    ga = _each(col_sum(m["last_a"]), gc)
    gb = _each(col_sum(m["last_b"]), gc)
    e2 = _each(lambda a, b, c: jnp.exp(jnp.where(m["in_a"], a, b) - c), ga, gb, gc)
    kd = _each(_mul, k, e2)
    return dict(u=u, w=w, aqk=aqk, qd=qd, kd=kd, gam=gam, kb=kb, vb=vb, lm=lm, t=t, eg=eg, kbe=kbe, e2=e2,
                gla=_each(jnp.exp, ga), glb=_each(jnp.exp, gb))


def _gdn_fwd(gq, gk, gv, gf, bf, S):
    nb = S // PAIR

    def body(q_ref, k_ref, v_ref, g_ref, b_ref, o_ref, st_ref, s_scr):
        @pl.when(pl.program_id(0) == 0)
        def _():
            s_scr[...] = jnp.zeros_like(s_scr)

        m = _gdn_masks()
        heads = lambda ref: [ref[:, _hs(h)] for h in range(NH)]
        f = _gdn_block(m, heads(q_ref), heads(k_ref), heads(v_ref), heads(g_ref), heads(b_ref))
        u, w, qd, kd = f["u"], f["w"], f["qd"], f["kd"]
        s0 = [s_scr[h * DH:(h + 1) * DH, :] for h in range(NH)]
        vna = _each(lambda a, b, s: _top(a) - _dg(_top(b), s), u, w, s0)
        oa = _each(lambda a, s: _dg(_top(a), s), qd, s0)
        s1 = _each(lambda s, gl, a, vn: s * gl + _dg(_top(a), vn, TN), s0, f["gla"], kd, vna)
        vnb = _each(lambda a, b, s: _bot(a) - _dg(_bot(b), s), u, w, s1)
        ob = _each(lambda a, s: _dg(_bot(a), s), qd, s1)
        s2 = _each(lambda s, gl, a, vn: s * gl + _dg(_bot(a), vn, TN), s1, f["glb"], kd, vnb)
        outs = _each(lambda a, b, c, va, vb: _rows(a, b) + _dg(c, _rows(va, vb)), oa, ob, f["aqk"], vna, vnb)
        o_ref[...] = jnp.concatenate(outs, axis=1)
        st_ref[...] = jnp.concatenate(s0 + s1, axis=0)
        s_scr[...] = jnp.concatenate(s2, axis=0)

    blk = pl.BlockSpec((PAIR, HW), lambda i: (i, 0))
    return pl.pallas_call(
        body, name="gdn_fwd", grid=(nb,),
        in_specs=[blk] * 5,
        out_specs=[blk, pl.BlockSpec((2 * NH * DH, DH), lambda i: (i, 0))],
        out_shape=[jax.ShapeDtypeStruct((S, HW), F32), jax.ShapeDtypeStruct((nb * 2 * NH * DH, DH), F32)],
        scratch_shapes=[pltpu.VMEM((NH * DH, DH), F32)],
        compiler_params=_cp(("arbitrary",)),
    )(gq, gk, gv, gf, bf)


def _gdn_bwd(gq, gk, gv, gf, bf, states, do, S):
    nb = S // PAIR

    def body(q_ref, k_ref, v_ref, g_ref, b_ref, st_ref, do_ref, dq_o, dk_o, dv_o, dg_o, db_o, ds_scr):
        @pl.when(pl.program_id(0) == 0)
        def _():
            ds_scr[...] = jnp.zeros_like(ds_scr)

        m = _gdn_masks()
        ones = jnp.ones((PAIR, PAIR), F32)
        heads = lambda ref: [ref[:, _hs(h)] for h in range(NH)]
        q, k, v, beta, do = heads(q_ref), heads(k_ref), heads(v_ref), heads(b_ref), heads(do_ref)
        f = _gdn_block(m, q, k, v, heads(g_ref), beta)
        u, w, aqk, qd, kd, t = f["u"], f["w"], f["aqk"], f["qd"], f["kd"], f["t"]
        s0 = [st_ref[h * DH:(h + 1) * DH, :] for h in range(NH)]
        s1 = [st_ref[(NH + h) * DH:(NH + h + 1) * DH, :] for h in range(NH)]
        ds2 = [ds_scr[h * DH:(h + 1) * DH, :] for h in range(NH)]
        total = lambda a, b: jnp.sum(jnp.sum(a * b, axis=1, keepdims=True), axis=0, keepdims=True)
        vna = _each(lambda a, b, s: _top(a) - _dg(_top(b), s), u, w, s0)
        vnb = _each(lambda a, b, s: _bot(a) - _dg(_bot(b), s), u, w, s1)
        dvn_i = _each(lambda a, b: _dg(a, b, TN), aqk, do)
        dvnb = _each(lambda a, b, s: _bot(a) + _dg(_bot(b), s), dvn_i, kd, ds2)
        dqdb = _each(lambda a, s: _dg(_bot(a), s, NT), do, s1)
        dkdb = _each(lambda a, s: _dg(a, s, NT), vnb, ds2)
        dglb = _each(total, ds2, s1)
        dwb = _each(lambda a, s: -_dg(a, s, NT), dvnb, s1)
        ds1 = _each(lambda s, gl, a, b, c, d: s * gl + _dg(_bot(a), _bot(b), TN) - _dg(_bot(c), d, TN),
                    ds2, f["glb"], qd, do, w, dvnb)
        dvna = _each(lambda a, b, s: _top(a) + _dg(_top(b), s), dvn_i, kd, ds1)
        dqda = _each(lambda a, s: _dg(_top(a), s, NT), do, s0)
        dkda = _each(lambda a, s: _dg(a, s, NT), vna, ds1)
        dgla = _each(total, ds1, s0)
        dwa = _each(lambda a, s: -_dg(a, s, NT), dvna, s0)
        ds0 = _each(lambda s, gl, a, b, c, d: s * gl + _dg(_top(a), _top(b), TN) - _dg(_top(c), d, TN),
                    ds1, f["gla"], qd, do, w, dvna)
        dvn, dqd, dkd, dw = (_each(_rows, a, b) for a, b in ((dvna, dvnb), (dqda, dqdb), (dkda, dkdb), (dwa, dwb)))
        daqk = _each(lambda a, va, vb: jnp.where(m["tril"], _dg(a, _rows(va, vb), NT), 0.0), do, vna, vnb)
        dt = _each(lambda a, b, c, d: _dg(a, b, NT) + _dg(c, d, NT), dvn, f["vb"], dw, f["kbe"])
        dvb = _each(lambda a, b: _dg(a, b, TN), t, dvn)
        dkbe = _each(lambda a, b: _dg(a, b, TN), t, dw)
        dtt = _each(lambda a, b: _dg(a, b, NT), dt, t)
        dl = _each(lambda a, b: -jnp.where(m["strict"], _dg(a, b, TN), 0.0), t, dtt)
        dm = _each(_mul, dl, f["gam"])
        dn = _each(_mul, daqk, f["gam"])
        dkb = _each(lambda a, b, c, d: _dg(a, b) + c * d, dm, k, dkbe, f["eg"])
        dks = _each(lambda a, b, c, d, e, g, h, i: _dg(a, b, TN) + _dg(c, d, TN) + e * g + h * i,
                    dm, f["kb"], dn, q, dkd, f["e2"], beta, dkb)
        dqs = _each(lambda a, b, c, d: _dg(a, b) + c * d, dn, k, dqd, f["eg"])
        gm = _each(lambda a, b, c, d: a * b + c * d, dl, f["lm"], daqk, aqk)
        dkdkd = _each(_mul, dkd, kd)
        dgc = _each(lambda a, b, c, d, e, g: _dxr(a + b * c + d * e - g, ones) - _dxr(a, ones, TN),
                    gm, dqd, qd, dkbe, f["kbe"], dkdkd)
        same_f = m["same"].astype(F32)
        chunk_tot = _each(lambda a: _dxl(same_f, _dxr(a, ones)), dkdkd)
        last = m["last_a"] | m["last_b"]
        dgc = _each(lambda a, b, ga, gla, gb, glb: a + jnp.where(last, b + jnp.where(m["in_a"], ga * gla, gb * glb), 0.0),
                    dgc, chunk_tot, dgla, f["gla"], dglb, f["glb"])
        dbs = _each(lambda a, b, c, d: _dxr(a * b + c * d, ones), dkb, k, dvb, v)
        dvs = _each(_mul, beta, dvb)
        triu_f = m["triu"].astype(F32)
        dgs = _each(lambda a: _dxl(triu_f, a), dgc)
        for ref, parts in ((dq_o, dqs), (dk_o, dks), (dv_o, dvs), (dg_o, dgs), (db_o, dbs)):
            ref[...] = jnp.concatenate(parts, axis=1)
        ds_scr[...] = jnp.concatenate(ds0, axis=0)

    blk = pl.BlockSpec((PAIR, HW), lambda i: (nb - 1 - i, 0))
    o = jax.ShapeDtypeStruct((S, HW), F32)
    return pl.pallas_call(
        body, name="gdn_bwd", grid=(nb,),
        in_specs=[blk] * 5 + [pl.BlockSpec((2 * NH * DH, DH), lambda i: (nb - 1 - i, 0)), blk],
        out_specs=[blk] * 5, out_shape=[o] * 5,
        scratch_shapes=[pltpu.VMEM((NH * DH, DH), F32)],
        compiler_params=_cp(("arbitrary",)),
    )(gq, gk, gv, gf, bf, states, do)


SB_T = 256
SB_GROUP = 4
SB_GROUP_BWD = 4
SB_SINGLES = 1
SB_DEAD = -110.0


def _group_sizes(g):
    sizes = []
    while g >= 1:
        sizes.append(g)
        g //= 2
    return sizes


def _sb_iotas(t):
    return lax.broadcasted_iota(jnp.int32, (t, t), 0), lax.broadcasted_iota(jnp.int32, (t, t), 1)


def _sb_scores(q, k, mask):
    z = _dot(q, k, NT) * DH ** -0.5
    ls = jnp.minimum(z, 0.0) - jnp.log(1.0 + jnp.exp(-jnp.abs(z)))
    lneg = ls - z
    if mask is not None:
        lneg = jnp.where(mask, lneg, 0.0)
    return ls, lneg


def _prefix(x, u):
    xh, xl = _split(x, 2)
    return _dot(xh, u) + _dot(xl, u)


def _sb_fwd(sqn, skn, svb, S):
    t = min(SB_T, S)

    def body(q_ref, k_ref, v_ref, o_ref, t_ref, cnt_ref):
        qb = pl.program_id(1)
        q = q_ref[...]
        r, c = _sb_iotas(t)
        diag = c < r
        u_after = (r > c).astype(BF16)

        def tiles(k0s, run, masks):
            sc = _each(lambda k0, m: _sb_scores(q, k_ref[pl.ds(k0, t), :], m), k0s, masks)
            ls, lneg = [s[0] for s in sc], [s[1] for s in sc]
            sums = _each(lambda x: jnp.sum(x, axis=1, keepdims=True), lneg)
            pre = _each(lambda x: _prefix(x, u_after), lneg)
            runs = [run]
            for s in sums:
                runs.append(runs[-1] + s)
            att = _each(lambda a, b, rn: jnp.exp(a + (rn + b)), ls, pre, runs[:-1])
            att = _each(lambda a, m: a if m is None else jnp.where(m, a, 0.0), att, masks)
            parts = _each(lambda a, k0: _dot(a.astype(BF16), v_ref[pl.ds(k0, t), :]), att, k0s)
            return sum(parts[1:], parts[0]), runs[-1]

        left = jnp.full((t, t), qb > 0)
        acc, run = tiles([pl.multiple_of(qb * t, t), pl.multiple_of(jnp.maximum(qb - 1, 0) * t, t)],
                         jnp.zeros((t, 1), F32), [diag, left])

        def alive(run):
            return jnp.max(run) >= SB_DEAD

        carry, done = (0, acc, run, alive(run)), jnp.minimum(qb, 1)
        for size, limit in [(1, SB_SINGLES)] + [(s, None) for s in _group_sizes(SB_GROUP)]:

            def more(c, size=size, done=done, limit=limit):
                i, _, _, go = c
                fits = done + (i + 1) * size <= qb
                return (fits if limit is None else fits & (i < limit)) & go

            def group(c, size=size, done=done):
                i, acc, run, _ = c
                first = qb - 1 - done - size * i
                part, run = tiles([pl.multiple_of((first - j) * t, t) for j in range(size)], run, [None] * size)
                return i + 1, acc + part, run, alive(run)

            n, acc, run, go = lax.while_loop(more, group, (0,) + carry[1:])
            carry, done = (0, acc, run, go), done + n * size
        o_ref[...] = acc.astype(BF16)
        t_ref[...] = jnp.broadcast_to(run, (t, DH))
        cnt_ref[pl.program_id(0), qb] = done

    qspec = pl.BlockSpec((t, DH), lambda h, i: (i, h))
    kspec = pl.BlockSpec((S, DH), lambda h, i: (0, h))
    return pl.pallas_call(
        body, name="sb_fwd", grid=(NH, S // t),
        in_specs=[qspec, kspec, kspec],
        out_specs=[qspec, qspec, pl.BlockSpec(memory_space=pltpu.SMEM)],
        out_shape=[jax.ShapeDtypeStruct((S, HW), BF16), jax.ShapeDtypeStruct((S, HW), F32),
                   jax.ShapeDtypeStruct((NH, S // t), jnp.int32)],
        compiler_params=_cp(("arbitrary", "arbitrary")),
    )(sqn, skn, svb)


def _sb_bwd(sqn, skn, svb, do, tot, walked, S):
    t = min(SB_T, S)

    def body(cnt_ref, q_ref, k_ref, v_ref, do_ref, t_ref, dq_o, dk_o, dv_o, dv_acc):
        qb = pl.program_id(1)

        @pl.when(qb == 0)
        def _():
            dk_o[...] = jnp.zeros_like(dk_o)
            dv_acc[...] = jnp.zeros_like(dv_acc)

        q = q_ref[...]
        do = do_ref[...].astype(BF16)
        tot_l = jnp.concatenate([t_ref[...]] * (t // DH), axis=1)
        r, c = _sb_iotas(t)
        diag = c < r
        u_upto = (r <= c).astype(BF16)
        u_before = (r < c).astype(BF16)

        def tiles(k0s, run_l, run_e, masks):
            rowsum = lambda x: jnp.sum(x, axis=1, keepdims=True)
            masked = lambda xs: _each(lambda a, m: a if m is None else jnp.where(m, a, 0.0), xs, masks)
            ks = [k_ref[pl.ds(k0, t), :] for k0 in k0s]
            vs = [v_ref[pl.ds(k0, t), :] for k0 in k0s]
            sc = _each(lambda k, m: _sb_scores(q, k, m), ks, masks)
            ls, lneg = [s[0] for s in sc], [s[1] for s in sc]
            sums_l = _each(rowsum, lneg)
            pre_l = _each(lambda x: _prefix(x, u_upto), lneg)
            runs_l = [run_l]
            for s in sums_l:
                runs_l.append(runs_l[-1] + s)
            att = masked(_each(lambda a, b, rn: jnp.exp(a + (tot_l - (rn + b))), ls, pre_l, runs_l[:-1]))
            e = _each(lambda v, a: _dot(do, v, NT) * a, vs, att)
            sums_e = _each(rowsum, e)
            pre_e = _each(lambda x: _prefix(x, u_before), e)
            runs_e = [run_e]
            for s in sums_e:
                runs_e.append(runs_e[-1] + s)
            sg = _each(jnp.exp, ls)
            dz = masked(_each(lambda a, b, rn, s: a * (1.0 - s) - (rn + b) * s, e, pre_e, runs_e[:-1], sg))
            dz = _each(lambda a: (a * DH ** -0.5).astype(BF16), dz)
            dvs = _each(lambda a: _dot(a.astype(BF16), do, TN), att)
            dks = _each(lambda a: _dot(a, q, TN), dz)
            dqs = _each(_dot, dz, ks)
            for k0, dv, dk in zip(k0s, dvs, dks):
                dv_acc[pl.ds(k0, t), :] += dv
                dk_o[pl.ds(k0, t), :] += dk
            return sum(dqs[1:], dqs[0]), runs_l[-1], runs_e[-1]

        walked = cnt_ref[pl.program_id(0), qb]
        early = jnp.maximum(walked - 1, 0)
        z1 = jnp.zeros((t, 1), F32)
        carry, done = (jnp.zeros((t, DH), F32), z1, z1), 0
        for size in _group_sizes(SB_GROUP_BWD):
            n = (early - done) // size

            def group(i, carry, size=size, done=done):
                dq, run_l, run_e = carry
                first = qb - walked + done + size * i
                part, run_l, run_e = tiles([pl.multiple_of((first + j) * t, t) for j in range(size)], run_l, run_e,
                                           [None] * size)
                return dq + part, run_l, run_e

            carry = lax.fori_loop(0, n, group, carry)
            done = done + n * size
        dq, run_l, run_e = carry
        left = jnp.full((t, t), qb > 0)
        part, _, _ = tiles([pl.multiple_of(jnp.maximum(qb - 1, 0) * t, t), pl.multiple_of(qb * t, t)], run_l, run_e,
                           [left, diag])
        dq_o[...] = dq + part

        @pl.when(qb == S // t - 1)
        def _():
            dv_o[...] = dv_acc[...].astype(BF16)

    qspec = pl.BlockSpec((t, DH), lambda h, i, cnt: (i, h))
    kspec = pl.BlockSpec((S, DH), lambda h, i, cnt: (0, h))
    o = jax.ShapeDtypeStruct((S, HW), F32)
    return pl.pallas_call(
        body, name="sb_bwd",
        grid_spec=pltpu.PrefetchScalarGridSpec(
            num_scalar_prefetch=1, grid=(NH, S // t),
            in_specs=[qspec, kspec, kspec, qspec, qspec], out_specs=[qspec, kspec, kspec],
            scratch_shapes=[pltpu.VMEM((S, DH), F32)]),
        out_shape=[o, o, jax.ShapeDtypeStruct((S, HW), BF16)],
        compiler_params=_cp(("parallel", "arbitrary")),
    )(walked, sqn, skn, svb, do, tot)


def _mem_probs(qn, kn):
    s = _dot(qn, kn.astype(BF16), NT) * DH ** -0.5
    p = jnp.exp(s - jnp.max(s, axis=-1, keepdims=True))
    return p / jnp.sum(p, axis=-1, keepdims=True)


def _mem_fwd(qmn, kv, gmk, S):
    ts = _row_tile(S)

    def body(q_ref, kv_ref, gk_ref, o_ref):
        for h in range(NH):
            kn, _ = _rms(kv_ref[:, _hs(h)], gk_ref[...])
            p = _mem_probs(q_ref[:, _hs(h)], kn)
            o_ref[:, _hs(h)] = _dbf(p, kv_ref[:, HW + h * DH:HW + (h + 1) * DH]).astype(BF16)

    return pl.pallas_call(
        body, name="mem_fwd", grid=(S // ts,),
        in_specs=[pl.BlockSpec((ts, HW), lambda i: (i, 0)), pl.BlockSpec((NMEM, 2 * HW), lambda i: (0, 0)),
                  pl.BlockSpec((1, DH), lambda i: (0, 0))],
        out_specs=pl.BlockSpec((ts, HW), lambda i: (i, 0)),
        out_shape=jax.ShapeDtypeStruct((S, HW), BF16),
        compiler_params=_cp(("parallel",)),
    )(qmn, kv, gmk)


def _mem_bwd(proj, qmn, kv, gmq, gmk, do, S):
    ts = _row_tile(S)
    n = S // ts

    def body(mq_ref, q_ref, kv_ref, gq_ref, gk_ref, do_ref, dmq_o, dkv_o, dgq_o, dgk_o, dkn_scr):
        i = pl.program_id(0)

        @pl.when(i == 0)
        def _():
            dkv_o[...] = jnp.zeros_like(dkv_o)
            dgq_o[...] = jnp.zeros_like(dgq_o)
            dkn_scr[...] = jnp.zeros_like(dkn_scr)

        dgq = jnp.zeros((1, DH), F32)
        for h in range(NH):
            km = kv_ref[:, _hs(h)]
            vm = kv_ref[:, HW + h * DH:HW + (h + 1) * DH].astype(BF16)
            kn, _ = _rms(km, gk_ref[...])
            qn = q_ref[:, _hs(h)]
            p = _mem_probs(qn, kn)
            dob = do_ref[:, _hs(h)].astype(BF16)
            dkv_o[:, HW + h * DH:HW + (h + 1) * DH] += _dot(p.astype(BF16), dob, TN)
            dp = _dot(dob, vm, NT)
            dsc = (p * (dp - jnp.sum(dp * p, axis=-1, keepdims=True)) * DH ** -0.5).astype(BF16)
            dkn_scr[:, _hs(h)] += _dot(dsc, qn, TN)
            x = mq_ref[:, _hs(h)]
            _, r = _rms(x, gq_ref[...])
            dx, dg = _rms_bwd(_dot(dsc, kn.astype(BF16)), x, gq_ref[...], r)
            dmq_o[:, _hs(h)] = dx.astype(BF16)
            dgq = dgq + dg
        dgq_o[...] += dgq

        @pl.when(i == n - 1)
        def _():
            dgk = jnp.zeros((1, DH), F32)
            for h in range(NH):
                km = kv_ref[:, _hs(h)]
                _, r = _rms(km, gk_ref[...])
                dx, dg = _rms_bwd(dkn_scr[:, _hs(h)], km, gk_ref[...], r)
                dkv_o[:, _hs(h)] = dx
                dgk = dgk + dg
            dgk_o[...] = dgk

    full = lambda r, c: pl.BlockSpec((r, c), lambda i: (0, 0))
    t512 = pl.BlockSpec((ts, HW), lambda i: (i, 0))
    return pl.pallas_call(
        body, name="mem_bwd", grid=(n,),
        in_specs=[pl.BlockSpec((ts, HW), lambda i: (i, CB_MQ)), t512, full(NMEM, 2 * HW), full(1, DH), full(1, DH),
                  t512],
        out_specs=[t512, full(NMEM, 2 * HW), full(1, DH), full(1, DH)],
        out_shape=[jax.ShapeDtypeStruct((S, HW), BF16), jax.ShapeDtypeStruct((NMEM, 2 * HW), F32),
                   jax.ShapeDtypeStruct((1, DH), F32), jax.ShapeDtypeStruct((1, DH), F32)],
        scratch_shapes=[pltpu.VMEM((NMEM, HW), F32)],
        compiler_params=_cp(("arbitrary",)),
    )(proj, qmn, kv, gmq, gmk, do)


def _gated_gdn(o, z, g):
    sg = _sigmoid(z)
    outs, rs = [], []
    for h in range(NH):
        y, r = _rms(o[:, _hs(h)], g)
        outs.append(y * (z[:, _hs(h)] * sg[:, _hs(h)]))
        rs.append(r)
    return jnp.concatenate(outs, axis=1), rs, sg


def _merge_fwd(x, proj, ogdn, osb, omem, ggdn, wbg, wbs, wbm, wo, S):
    ts = _narrow_tile(S)

    def body(x_ref, z_ref, g0_ref, g1_ref, g2_ref, og_ref, os_ref, om_ref, gg_ref, wbg_ref, wbs_ref, wbm_ref,
             wo_ref, x1_o, mix_o):
        on, _, _ = _gated_gdn(og_ref[...], z_ref[...], gg_ref[...])
        mix = (_sigmoid(g0_ref[...]) * _dbf(on, wbg_ref[...]) + _sigmoid(g1_ref[...]) * _dbf(os_ref[...], wbs_ref[...])
               + _sigmoid(g2_ref[...]) * _dbf(om_ref[...], wbm_ref[...]))
        mix_o[...] = mix.astype(BF16)
        x1_o[...] = x_ref[...] + _dbf(mix, wo_ref[...])

    t512 = pl.BlockSpec((ts, HW), lambda i: (i, 0))
    t1k = pl.BlockSpec((ts, D), lambda i: (i, 0))
    gate = lambda j: pl.BlockSpec((ts, D), lambda i: (i, 4 + j))
    full = lambda r, c: pl.BlockSpec((r, c), lambda i: (0, 0))
    return pl.pallas_call(
        body, name="merge_fwd", grid=(S // ts,),
        in_specs=[t1k, pl.BlockSpec((ts, HW), lambda i: (i, CB_Z)), gate(0), gate(1), gate(2), t512, t512, t512,
                  full(1, DH), full(HW, D), full(HW, D), full(HW, D), full(D, D)],
        out_specs=[t1k, t1k],
        out_shape=[jax.ShapeDtypeStruct((S, D), F32), jax.ShapeDtypeStruct((S, D), BF16)],
        compiler_params=_cp(("parallel",)),
    )(x, proj, proj, proj, proj, ogdn, osb, omem, ggdn, wbg, wbs, wbm, wo)


def _merge_bwd(dmix, proj, ogdn, osb, omem, ggdn, wbg, wbs, wbm, S):
    ts = _narrow_tile(S)

    def body(dm_ref, z_ref, g0_ref, g1_ref, g2_ref, og_ref, os_ref, om_ref, gg_ref, wbg_ref, wbs_ref, wbm_ref,
             dgl0_o, dgl1_o, dgl2_o, dog_o, dz_o, dos_o, dom_o, dwbg_o, dwbs_o, dwbm_o, dgg_o):
        @pl.when(pl.program_id(0) == 0)
        def _():
            for ref in (dwbg_o, dwbs_o, dwbm_o, dgg_o):
                ref[...] = jnp.zeros_like(ref)

        dm = dm_ref[...]
        og = og_ref[...]
        z = z_ref[...]
        on, rs, sg = _gated_gdn(og, z, gg_ref[...])
        branch = ((on, g0_ref, wbg_ref, dgl0_o, dwbg_o), (os_ref[...], g1_ref, wbs_ref, dgl1_o, dwbs_o),
                  (om_ref[...], g2_ref, wbm_ref, dgl2_o, dwbm_o))
        dos = []
        for o, g_ref, w_ref, dgl_o, dw_o in branch:
            ob = o.astype(BF16)
            gate = _sigmoid(g_ref[...])
            dgl_o[...] = (dm * _dot(ob, w_ref[...]) * gate * (1.0 - gate)).astype(BF16)
            dy = (dm * gate).astype(BF16)
            dw_o[...] += _dot(ob, dy, TN)
            dos.append(_dot(dy, w_ref[...], NT))
        dos_o[...] = dos[1].astype(BF16)
        dom_o[...] = dos[2].astype(BF16)
        don = dos[0]
        dgg = jnp.zeros((1, DH), F32)
        for h in range(NH):
            oh, zh, sh = og[:, _hs(h)], z[:, _hs(h)], sg[:, _hs(h)]
            y = oh * rs[h] * gg_ref[...]
            dz_o[:, _hs(h)] = (don[:, _hs(h)] * y * (sh * (1.0 + zh * (1.0 - sh)))).astype(BF16)
            dx, dg = _rms_bwd(don[:, _hs(h)] * (zh * sh), oh, gg_ref[...], rs[h])
            dog_o[:, _hs(h)] = dx
            dgg = dgg + dg
        dgg_o[...] += dgg

    t512 = pl.BlockSpec((ts, HW), lambda i: (i, 0))
    t1k = pl.BlockSpec((ts, D), lambda i: (i, 0))
    gate = lambda j: pl.BlockSpec((ts, D), lambda i: (i, 4 + j))
    full = lambda r, c: pl.BlockSpec((r, c), lambda i: (0, 0))
    s1k = jax.ShapeDtypeStruct((S, D), BF16)
    s512 = jax.ShapeDtypeStruct((S, HW), BF16)
    wsh = jax.ShapeDtypeStruct((HW, D), F32)
    return pl.pallas_call(
        body, name="merge_bwd", grid=(S // ts,),
        in_specs=[t1k, pl.BlockSpec((ts, HW), lambda i: (i, CB_Z)), gate(0), gate(1), gate(2), t512, t512, t512,
                  full(1, DH), full(HW, D), full(HW, D), full(HW, D)],
        out_specs=[t1k, t1k, t1k, t512, t512, t512, t512, full(HW, D), full(HW, D), full(HW, D), full(1, DH)],
        out_shape=[s1k, s1k, s1k, jax.ShapeDtypeStruct((S, HW), F32), s512, s512, s512, wsh, wsh, wsh,
                   jax.ShapeDtypeStruct((1, DH), F32)],
        compiler_params=_cp(("arbitrary",)),
    )(dmix, proj, proj, proj, proj, ogdn, osb, omem, ggdn, wbg, wbs, wbm)


def _norm_cast(name, x, g):
    rows = x.shape[0]
    ts = min(_row_tile(rows), rows)

    def body(x_ref, g_ref, o_ref):
        o_ref[...] = _rms(x_ref[...], g_ref[...])[0].astype(BF16)

    t1k = pl.BlockSpec((ts, D), lambda i: (i, 0))
    return pl.pallas_call(
        body, name=name, grid=(rows // ts,), in_specs=[t1k, pl.BlockSpec((1, D), lambda i: (0, 0))], out_specs=t1k,
        out_shape=jax.ShapeDtypeStruct((rows, D), BF16), compiler_params=_cp(("parallel",)),
    )(x, g)


def _norm_bwd(name, dh, x, g, res):
    rows = x.shape[0]
    ts = min(_row_tile(rows), rows)

    def body(*refs):
        dh_ref, x_ref, g_ref = refs[:3]
        dx_o, dg_o = refs[-2:]

        @pl.when(pl.program_id(0) == 0)
        def _():
            dg_o[...] = jnp.zeros_like(dg_o)

        xv = x_ref[...]
        _, r = _rms(xv, g_ref[...])
        dx, dg = _rms_bwd(dh_ref[...], xv, g_ref[...], r)
        dx_o[...] = dx if res is None else dx + refs[3][...]
        dg_o[...] += dg

    t1k = pl.BlockSpec((ts, D), lambda i: (i, 0))
    gsp = pl.BlockSpec((1, D), lambda i: (0, 0))
    ops = [dh, x, g] + ([] if res is None else [res])
    return pl.pallas_call(
        body, name=name, grid=(rows // ts,), in_specs=[t1k, t1k, gsp] + ([] if res is None else [t1k]),
        out_specs=[t1k, gsp],
        out_shape=[jax.ShapeDtypeStruct((rows, D), F32), jax.ShapeDtypeStruct((1, D), F32)],
        compiler_params=_cp(("arbitrary",)),
    )(*ops)


def _adamw(name, gall, w, m, v):
    rows = w.shape[0]
    nsrc = gall.shape[0]
    tr = min(SLAB_TILE, rows)
    assert rows % tr == 0

    def body(g_ref, w_ref, m_ref, v_ref, g_o, d_o, m_o, v_o):
        g = g_ref[0].astype(F32)
        for j in range(1, nsrc):
            g = g + g_ref[j].astype(F32)
        m_new = ADAM_B1 * m_ref[...] + (1.0 - ADAM_B1) * g
        v_new = ADAM_B2 * v_ref[...] + (1.0 - ADAM_B2) * jnp.square(g)
        m_hat = m_new / (1.0 - ADAM_B1 ** ADAM_STEP)
        v_hat = v_new / (1.0 - ADAM_B2 ** ADAM_STEP)
        g_o[...] = g
        d_o[...] = -ADAM_LR * (m_hat / (jnp.sqrt(v_hat) + ADAM_EPS) + ADAM_WD * w_ref[...])
        m_o[...] = m_new
        v_o[...] = v_new

    t = pl.BlockSpec((tr, LANES), lambda i: (i, 0))
    o = jax.ShapeDtypeStruct((rows, LANES), F32)
    return pl.pallas_call(
        body, name=name, grid=(rows // tr,),
        in_specs=[pl.BlockSpec((nsrc, tr, LANES), lambda i: (0, i, 0)), t, t, t],
        out_specs=[t, t, t, t], out_shape=[o, o, o, o],
        compiler_params=_cp(("parallel",)),
    )(gall, w, m, v)


def _pair_sum(mine, theirs):
    rows = mine.shape[1]
    tr = min(SLAB_TILE, rows)
    assert rows % tr == 0
    core = lax.axis_index("c").astype(jnp.int32).reshape(1)

    def body(c_ref, a_ref, b_ref, o_ref):
        o_ref[...] = (a_ref[...].astype(F32) + b_ref[...].astype(F32)).astype(o_ref.dtype)

    blk = pl.BlockSpec((1, tr, LANES), lambda j, i, c_ref: (j, i, 0))
    return pl.pallas_call(
        body, name="pair_sum",
        grid_spec=pltpu.PrefetchScalarGridSpec(
            num_scalar_prefetch=1, grid=(NDEV // 2, rows // tr),
            in_specs=[pl.BlockSpec((1, tr, LANES), lambda j, i, c_ref: (2 * j + c_ref[0], i, 0)), blk],
            out_specs=blk),
        out_shape=jax.ShapeDtypeStruct((NDEV // 2, rows, LANES), mine.dtype),
        compiler_params=_cp(("parallel", "parallel")),
    )(core, mine, theirs)


HBM_SPEC = pl.BlockSpec(memory_space=pltpu.HBM)


def _remote(src, dst, send_sems, recv_sems, k, to):
    return pltpu.make_async_remote_copy(src_ref=src, dst_ref=dst, send_sem=send_sems.at[k], recv_sem=recv_sems.at[k],
                                        device_id=to, device_id_type=pl.DeviceIdType.MESH)


def _gather(name, x):
    rows, cols = x.shape

    def body(x_ref, o_ref, send_sems, recv_sems, local_sem):
        ix, iy, ic = lax.axis_index("x"), lax.axis_index("y"), lax.axis_index("c")
        me, sibling = (ix, iy, ic), (ix, iy, 1 - ic)
        chips = [(1 - ix, iy), (ix, 1 - iy), (1 - ix, 1 - iy)]

        def slab(px, py, pc):
            return o_ref.at[4 * px + 2 * py + pc]

        def copy(k, block, to, src=None):
            return _remote(slab(*block) if src is None else src, slab(*block), send_sems, recv_sems, k, to)

        mine = pltpu.make_async_copy(x_ref, slab(*me), local_sem)
        mine.start()
        first = [copy(0, me, sibling, src=x_ref)]
        first += [copy(1 + j, me, (*chip, ic), src=x_ref) for j, chip in enumerate(chips)]
        for cp in first:
            cp.start()
        passed = [copy(4 + j, (*chip, ic), sibling) for j, chip in enumerate(chips)]
        for j, chip in enumerate(chips):
            copy(1 + j, (*chip, ic), me).wait_recv()
            passed[j].start()
        copy(0, sibling, me).wait_recv()
        for j, chip in enumerate(chips):
            copy(4 + j, (*chip, 1 - ic), me).wait_recv()
        for cp in first + passed:
            cp.wait_send()
        mine.wait()

    return pl.pallas_call(
        body, name=name, in_specs=[HBM_SPEC], out_specs=HBM_SPEC,
        out_shape=jax.ShapeDtypeStruct((NDEV, rows, cols), x.dtype),
        scratch_shapes=[pltpu.SemaphoreType.DMA((NDEV - 1,)), pltpu.SemaphoreType.DMA((NDEV - 1,)),
                        pltpu.SemaphoreType.DMA],
    )(x)


def _sibling_exchange(name, x):
    rows, cols = x.shape[-2:]
    nchip = NDEV // 2

    def body(x_ref, o_ref, send_sems, recv_sems):
        ix, iy, ic = lax.axis_index("x"), lax.axis_index("y"), lax.axis_index("c")
        copies = [_remote(x_ref.at[2 * j + (1 - ic)], o_ref.at[j], send_sems, recv_sems, j, (ix, iy, 1 - ic))
                  for j in range(nchip)]
        for cp in copies:
            cp.start()
        for cp in copies:
            cp.wait()

    return pl.pallas_call(
        body, name=name, in_specs=[HBM_SPEC], out_specs=HBM_SPEC,
        out_shape=jax.ShapeDtypeStruct((nchip, rows, cols), x.dtype),
        scratch_shapes=[pltpu.SemaphoreType.DMA((nchip,)), pltpu.SemaphoreType.DMA((nchip,))],
    )(x)


def _chip_exchange(name, x):
    rows, cols = x.shape[-2:]
    nchip = NDEV // 2

    def body(x_ref, o_ref, send_sems, recv_sems, local_sem):
        ix, iy, ic = lax.axis_index("x"), lax.axis_index("y"), lax.axis_index("c")
        my_chip = 2 * ix + iy
        own = pltpu.make_async_copy(x_ref.at[my_chip], o_ref.at[my_chip], local_sem)
        own.start()
        copies = []
        for k in range(1, nchip):
            px, py = ix ^ (k >> 1), iy ^ (k & 1)
            copies.append(_remote(x_ref.at[2 * px + py], o_ref.at[my_chip], send_sems, recv_sems, k - 1, (px, py, ic)))
        for cp in copies:
            cp.start()
        for cp in copies:
            cp.wait()
        own.wait()

    return pl.pallas_call(
        body, name=name, in_specs=[HBM_SPEC], out_specs=HBM_SPEC,
        out_shape=jax.ShapeDtypeStruct((nchip, rows, cols), x.dtype),
        scratch_shapes=[pltpu.SemaphoreType.DMA((nchip - 1,)), pltpu.SemaphoreType.DMA((nchip - 1,)),
                        pltpu.SemaphoreType.DMA],
    )(x)


COL_SHARDED = {"w_in": (D, D_IN), "w_br_gdn": (HW, D), "w_br_sb": (HW, D), "w_br_mem": (HW, D), "w_up": (D, DFF),
               "conv_w": (4, 3 * HW)}
ROW_SHARDED = {"w_mem_kv": (D, 2 * HW), "w_o": (D, D), "w_down": (DFF, D)}


def _to_slab(p):
    return p.reshape(p.shape[:-2] + (-1, LANES))


def _from_slab(flat, r, c):
    return flat.reshape(flat.shape[:-2] + (r, c))


def _shard_dims(name):
    if name in COL_SHARDED:
        r, c = COL_SHARDED[name]
        return r, c // NDEV
    r, c = ROW_SHARDED[name]
    return r // NDEV, c


def _pack_rows(parts, total):
    flat = jnp.concatenate(parts, axis=-2)
    return jnp.pad(flat, [(0, 0)] * (flat.ndim - 2) + [(0, total - flat.shape[-2]), (0, 0)])


def _pack_shards(vals):
    return _pack_rows([_to_slab(vals[n][0]) for n in BIG], R_BIG)


def _pack_full_grads(grads):
    parts = []
    for name in BIG:
        g = grads[name]
        r, c = _shard_dims(name)
        if name in COL_SHARDED:
            g = g.reshape(r, NDEV, c).transpose(1, 0, 2)
        else:
            g = g.reshape(NDEV, r, c)
        parts.append(_to_slab(g))
    return _pack_rows(parts, R_BIG)


def _unpack_gathered(slabs):
    out, pos = {}, 0
    for name, rows in zip(BIG, BIG_ROWS):
        r, c = _shard_dims(name)
        g = _from_slab(slabs[:, pos:pos + rows], r, c)
        pos += rows
        if name in COL_SHARDED:
            out[name] = g.transpose(1, 0, 2).reshape(r, NDEV * c)
        else:
            out[name] = g.reshape(NDEV * r, c)
    return out


def _unpack_shard(flat, shapes):
    out, pos = {}, 0
    for name, rows in zip(BIG, BIG_ROWS):
        r, c = _shard_dims(name)
        out[name] = _from_slab(flat[pos:pos + rows], r, c).reshape(shapes[name])
        pos += rows
    return out


def _pack_small(vals):
    rows = []
    for name, n in zip(SMALL, SMALL_ROWS):
        v = vals[name].reshape(-1)
        rows.append(jnp.pad(v, (0, n * LANES - v.shape[0])).reshape(n, LANES))
    return _pack_rows(rows, R_SMALL)


def _unpack_small(flat, shapes):
    out, pos = {}, 0
    for name, n in zip(SMALL, SMALL_ROWS):
        size = shapes[name][-1]
        out[name] = flat[pos:pos + n].reshape(-1)[:size].reshape(shapes[name])
        pos += n
    return out


def _pad_w_in(w):
    return jnp.concatenate([w[:, :2048], w[:, 2056:], w[:, 2048:2056], jnp.zeros((D, D_INP - D_IN), w.dtype)], axis=1)


def _unpad_w_in(w):
    return jnp.concatenate([w[:, :2048], w[:, 7168:7176], w[:, 2048:7168]], axis=1)


def _per_head(v):
    return jnp.repeat(v.reshape(NH), DH).reshape(1, HW)


def _local_step(x, mem, target, w, sm):
    S = x.shape[0]
    ts = _row_tile(S)
    alog_f, dtb_f = _per_head(sm["a_log"]), _per_head(sm["dt_bias"])

    h1 = _norm_cast("norm1", x, sm["norm1_g"])
    proj = _mm("in_proj", h1, w["w_in"], "nn", ts, 1536, D, n_outer=True)
    gq, gk, gv, gf, bf, sqn, skn, svb, qmn = _pre_fwd(proj, w["conv_w"], alog_f, dtb_f, sm["sb_q_norm_g"],
                                                      sm["sb_k_norm_g"], sm["mem_q_norm_g"], S)
    ogdn, states = _gdn_fwd(gq, gk, gv, gf, bf, S)
    osb, sb_tot, sb_walked = _sb_fwd(sqn, skn, svb, S)
    kv = _mm("mem_kv", mem, w["w_mem_kv"], "nn", NMEM, D, D, pro="rms", pro_g=sm["mem_norm_g"])
    omem = _mem_fwd(qmn, kv, sm["mem_k_norm_g"], S)
    x1, mix = _merge_fwd(x, proj, ogdn, osb, omem, sm["gdn_norm_g"], w["w_br_gdn"], w["w_br_sb"], w["w_br_mem"],
                         w["w_o"], S)
    h2 = _norm_cast("norm2", x1, sm["norm2_g"])
    up = _mm("mlp_up", h2, w["w_up"], "nn", ts, 2048, D, n_outer=True)
    dy, loss = _mm("mlp_down", up, w["w_down"], "nn", ts, D, 1024, pro="relu2", epi="loss", epi_x=(x1, target))

    g = {}
    dup = _mm("d_up", dy, w["w_down"], "nt", ts, 1024, D, epi="drelu2", epi_x=up, out_dtype=BF16)
    g["w_down"] = _mm("dw_down", up, dy, "tn", 1024, D, 512, pro="relu2")
    g["w_up"] = _mm("dw_up", h2, dup, "tn", D, 1024, 512)
    dx1, g["norm2_g"] = _mm("d_h2", dup, w["w_up"], "nt", ts, D, 1024, epi="rms_bwd", epi_x=(x1, sm["norm2_g"], dy))

    dmix = _mm("d_mix", dx1, w["w_o"], "nt", ts, D, D)
    g["w_o"] = _mm("dw_o", mix, dx1, "tn", D, D, 512)
    (dgl0, dgl1, dgl2, dogdn, dz, dosb, domem, g["w_br_gdn"], g["w_br_sb"], g["w_br_mem"],
     g["gdn_norm_g"]) = _merge_bwd(dmix, proj, ogdn, osb, omem, sm["gdn_norm_g"], w["w_br_gdn"], w["w_br_sb"],
                                   w["w_br_mem"], S)
    dmq, dkv, g["mem_q_norm_g"], g["mem_k_norm_g"] = _mem_bwd(proj, qmn, kv, sm["mem_q_norm_g"], sm["mem_k_norm_g"],
                                                             domem, S)
    g["w_mem_kv"] = _mm("dw_mem_kv", mem, dkv, "tn", D, D, NMEM, pro="rms", pro_g=sm["mem_norm_g"])
    dmn = _mm("d_mem_n", dkv, w["w_mem_kv"], "nt", NMEM, D, D)
    _, g["mem_norm_g"] = _norm_bwd("mem_norm_bwd", dmn, mem, sm["mem_norm_g"], None)
    dsqn, dskn, dsv = _sb_bwd(sqn, skn, svb, dosb, sb_tot, sb_walked, S)
    dgq, dgk, dgv, dgf, dbf = _gdn_bwd(gq, gk, gv, gf, bf, states, dogdn, S)
    dc, dab, dsq, dsk, g["conv_w"], dal_f, ddt_f, g["sb_q_norm_g"], g["sb_k_norm_g"] = _pre_bwd(
        proj, w["conv_w"], alog_f, dtb_f, sm["sb_q_norm_g"], sm["sb_k_norm_g"], dgq, dgk, dgv, dgf, dbf, dsqn, dskn, S)
    g["a_log"] = dal_f.reshape(NH, DH)[:, 0].reshape(1, NH)
    g["dt_bias"] = ddt_f.reshape(NH, DH)[:, 0].reshape(1, NH)
    dqkv = _conv_bwd(dc, w["conv_w"], S)

    dproj = jnp.concatenate([dqkv, dz, dsq, dsk, dsv, dmq, dgl0, dgl1, dgl2, dab], axis=1)
    g["w_in"] = _mm("dw_in", h1, dproj, "tn", D, 1536, 512)
    dx, g["norm1_g"] = _mm("d_h", dproj, w["w_in"], "nt", ts, D, 1536, epi="rms_bwd", epi_x=(x, sm["norm1_g"], dx1))
    return loss[0, 0], dx, g


def kernel(x, mem, norm1_g, w_in, conv_w, a_log, dt_bias, gdn_norm_g, sb_q_norm_g, sb_k_norm_g, mem_norm_g, w_mem_kv, mem_q_norm_g, mem_k_norm_g, w_br_gdn, w_br_sb, w_br_mem, w_o, norm2_g, w_up, w_down, loss_target, m_norm1_g, m_w_in, m_conv_w, m_a_log, m_dt_bias, m_gdn_norm_g, m_sb_q_norm_g, m_sb_k_norm_g, m_mem_norm_g, m_w_mem_kv, m_mem_q_norm_g, m_mem_k_norm_g, m_w_br_gdn, m_w_br_sb, m_w_br_mem, m_w_o, m_norm2_g, m_w_up, m_w_down, v_norm1_g, v_w_in, v_conv_w, v_a_log, v_dt_bias, v_gdn_norm_g, v_sb_q_norm_g, v_sb_k_norm_g, v_mem_norm_g, v_w_mem_kv, v_mem_q_norm_g, v_mem_k_norm_g, v_w_br_gdn, v_w_br_sb, v_w_br_mem, v_w_o, v_norm2_g, v_w_up, v_w_down):
    given = dict(norm1_g=norm1_g, w_in=w_in, conv_w=conv_w, a_log=a_log, dt_bias=dt_bias, gdn_norm_g=gdn_norm_g,
                 sb_q_norm_g=sb_q_norm_g, sb_k_norm_g=sb_k_norm_g, mem_norm_g=mem_norm_g, w_mem_kv=w_mem_kv,
                 mem_q_norm_g=mem_q_norm_g, mem_k_norm_g=mem_k_norm_g, w_br_gdn=w_br_gdn, w_br_sb=w_br_sb,
                 w_br_mem=w_br_mem, w_o=w_o, norm2_g=norm2_g, w_up=w_up, w_down=w_down)
    mom1 = dict(norm1_g=m_norm1_g, w_in=m_w_in, conv_w=m_conv_w, a_log=m_a_log, dt_bias=m_dt_bias,
                gdn_norm_g=m_gdn_norm_g, sb_q_norm_g=m_sb_q_norm_g, sb_k_norm_g=m_sb_k_norm_g,
                mem_norm_g=m_mem_norm_g, w_mem_kv=m_w_mem_kv, mem_q_norm_g=m_mem_q_norm_g,
                mem_k_norm_g=m_mem_k_norm_g, w_br_gdn=m_w_br_gdn, w_br_sb=m_w_br_sb, w_br_mem=m_w_br_mem, w_o=m_w_o,
                norm2_g=m_norm2_g, w_up=m_w_up, w_down=m_w_down)
    mom2 = dict(norm1_g=v_norm1_g, w_in=v_w_in, conv_w=v_conv_w, a_log=v_a_log, dt_bias=v_dt_bias,
                gdn_norm_g=v_gdn_norm_g, sb_q_norm_g=v_sb_q_norm_g, sb_k_norm_g=v_sb_k_norm_g,
                mem_norm_g=v_mem_norm_g, w_mem_kv=v_w_mem_kv, mem_q_norm_g=v_mem_q_norm_g,
                mem_k_norm_g=v_mem_k_norm_g, w_br_gdn=v_w_br_gdn, w_br_sb=v_w_br_sb, w_br_mem=v_w_br_mem, w_o=v_w_o,
                norm2_g=v_norm2_g, w_up=v_w_up, w_down=v_w_down)
    shapes = {n: given[n].shape for n in WEIGHTS}

    w_loc = _pack_shards(given)
    gathered = _gather("gather_weights", w_loc.astype(BF16))
    w = _unpack_gathered(gathered[:, :sum(BIG_ROWS)])
    w["w_in"] = _pad_w_in(w["w_in"])
    conv_loc = jnp.pad(given["conv_w"][0].reshape(-1, LANES), ((0, 2), (0, 0)))
    conv_all = _gather("gather_conv", conv_loc)
    w["conv_w"] = conv_all[:, :6].reshape(NDEV, 4, 3 * HW // NDEV).transpose(1, 0, 2).reshape(4, 3 * HW)
    sm = {n: given[n] for n in SMALL}

    loss, dx, g = _local_step(x[0], mem[0], loss_target[0], w, sm)
    g["w_in"] = _unpad_w_in(g["w_in"])

    g_mine = _pack_full_grads(g).astype(BF16)
    g_pair = _pair_sum(g_mine, _sibling_exchange("scatter_sibling", g_mine))
    g_all = _chip_exchange("scatter_chips", g_pair)
    gb, db, mb, vb = _adamw("adamw_sharded", g_all, w_loc, _pack_shards(mom1), _pack_shards(mom2))
    gs_all = _gather("gather_small_grads", _pack_small(g))
    gs, dsm, ms, vs = _adamw("adamw_replicated", gs_all, _pack_small(given), _pack_small(mom1), _pack_small(mom2))

    outs = {}
    for prefix, big, small in (("grad_", gb, gs), ("delta_", db, dsm), ("new_m_", mb, ms), ("new_v_", vb, vs)):
        vals = _unpack_shard(big, shapes)
        vals.update(_unpack_small(small, shapes))
        for n in WEIGHTS:
            outs[prefix + n] = vals[n]
    loss = lax.psum(loss, ("x", "y", "c"))
    return (loss, dx[None], *[outs[p + n] for p in ("grad_", "delta_", "new_m_", "new_v_") for n in WEIGHTS])
```

```python
import jax
import jax.numpy as jnp
from jax import lax
from jax.experimental import pallas as pl
from jax.experimental.pallas import tpu as pltpu

F32 = jnp.float32
BF16 = jnp.bfloat16

D = 1024
NH = 4
DH = 128
HW = NH * DH
DFF = 4 * D
NMEM = 256
EPS = 1e-6
NDEV = 8
LANES = 128
PAIR = 128
CHUNK = 64
D_IN = 7176
D_INP = 7680
VMEM_LIMIT = 56 * 1024 * 1024

ADAM_LR, ADAM_B1, ADAM_B2, ADAM_EPS, ADAM_WD, ADAM_STEP = 0.001, 0.9, 0.999, 1e-08, 0.01, 10

CB_Z, CB_SQ, CB_SK, CB_SV, CB_MQ, CB_AB = 3, 4, 5, 6, 7, 14

NN = (((1,), (0,)), ((), ()))
NT = (((1,), (1,)), ((), ()))
TN = (((0,), (0,)), ((), ()))

BIG = ("w_in", "w_mem_kv", "w_br_gdn", "w_br_sb", "w_br_mem", "w_o", "w_up", "w_down", "conv_w")
BIG_ROWS = (7176, 1024, 512, 512, 512, 1024, 4096, 4096, 6)
R_BIG = 19456
SLAB_TILE = 1216
SMALL = ("norm1_g", "a_log", "dt_bias", "gdn_norm_g", "sb_q_norm_g", "sb_k_norm_g", "mem_norm_g",
         "mem_q_norm_g", "mem_k_norm_g", "norm2_g")
VEC = ("norm1_g", "mem_norm_g", "norm2_g", "gdn_norm_g", "sb_q_norm_g", "sb_k_norm_g", "mem_q_norm_g", "mem_k_norm_g",
       "a_log", "dt_bias")
VEC_SIZES = (1024, 1024, 1024, 128, 128, 128, 128, 128, 4, 4)
VEC_OFFSETS = (0, 1024, 2048, 3072, 3200, 3328, 3456, 3584, 3712, 3716)
VEC_WIDTH = 3840
WEIGHTS = ("norm1_g", "w_in", "conv_w", "a_log", "dt_bias", "gdn_norm_g", "sb_q_norm_g", "sb_k_norm_g",
           "mem_norm_g", "w_mem_kv", "mem_q_norm_g", "mem_k_norm_g", "w_br_gdn", "w_br_sb", "w_br_mem",
           "w_o", "norm2_g", "w_up", "w_down")


def _cp(sem=None):
    return pltpu.CompilerParams(dimension_semantics=sem, vmem_limit_bytes=VMEM_LIMIT)


def _dot(a, b, dims=NN):
    return lax.dot_general(a, b, dims, preferred_element_type=F32)


def _dbf(a, b, dims=NN):
    return _dot(a.astype(BF16), b.astype(BF16), dims)


def _split(a, n):
    parts = []
    for _ in range(n):
        h = a.astype(BF16)
        parts.append(h)
        a = a - h.astype(F32)
    return parts


def _dg(a, b, dims=NN):
    return _dbf(a, b, dims)


def _dxr(a, e, dims=NN):
    eb = e.astype(BF16)
    a1, a2, a3 = _split(a, 3)
    return _dot(a1, eb, dims) + (_dot(a2, eb, dims) + _dot(a3, eb, dims))


def _dxl(e, a, dims=NN):
    eb = e.astype(BF16)
    a1, a2, a3 = _split(a, 3)
    return _dot(eb, a1, dims) + (_dot(eb, a2, dims) + _dot(eb, a3, dims))


def _sigmoid(x):
    return 1.0 / (1.0 + jnp.exp(-x))


def _softplus(x):
    return jnp.maximum(x, 0.0) + jnp.log(1.0 + jnp.exp(-jnp.abs(x)))


def _rms(x, g):
    r = lax.rsqrt(jnp.mean(x * x, axis=-1, keepdims=True) + EPS)
    return x * r * g, r


def _rms_bwd(dy, x, g, r):
    dyg = dy * g
    dx = r * (dyg - x * (r * r) * jnp.mean(dyg * x, axis=-1, keepdims=True))
    dg = jnp.sum(dy * (x * r), axis=0, keepdims=True)
    return dx, dg


def _hs(h):
    return slice(h * DH, (h + 1) * DH)


def _row_tile(s):
    return 512 if s >= 2048 else 256


def _narrow_tile(s):
    return min(256, s)


def _mm(name, a, b, mode, tm, tn, tk, pro=None, pro_g=None, epi=None, epi_x=None, out_dtype=F32, n_outer=False):
    if mode == "tn":
        K, M = a.shape
    else:
        M, K = a.shape
    N = b.shape[0] if mode == "nt" else b.shape[1]
    tm, tn, tk = min(tm, M), min(tn, N), min(tk, K)
    nk = K // tk
    assert M % tm == 0 and N % tn == 0 and K % tk == 0, (name, M, N, K, tm, tn, tk)
    dims = {"nn": NN, "nt": NT, "tn": TN}[mode]
    reducing = epi in ("rms_bwd", "loss")
    assert not reducing or (tn == N and not n_outer), name
    epi_ops = () if epi is None else (epi_x if isinstance(epi_x, tuple) else (epi_x,))

    def body(*refs):
        a_ref, b_ref = refs[0], refs[1]
        pos = 2
        g_ref = None
        if pro == "rms":
            g_ref = refs[pos]
            pos += 1
        e_refs = refs[pos:pos + len(epi_ops)]
        pos += len(epi_ops)
        o_ref = refs[pos]
        pos += 1
        r_ref = None
        if reducing:
            r_ref = refs[pos]
            pos += 1
        av = a_ref[...]
        if pro == "rms":
            av, _ = _rms(av.astype(F32), g_ref[...])
        elif pro == "relu2":
            av = jnp.square(jnp.maximum(av, 0.0))
        part = _dbf(av, b_ref[...], dims)
        first = pl.program_id(0) == 0

        def finish(acc):
            red = None
            if epi == "add":
                acc = acc + e_refs[0][...]
            elif epi == "drelu2":
                acc = acc * (2.0 * jnp.maximum(e_refs[0][...], 0.0))
            elif epi == "rms_bwd":
                xv, gv = e_refs[0][...], e_refs[1][...]
                _, r = _rms(xv, gv)
                dx, red = _rms_bwd(acc, xv, gv, r)
                acc = dx + e_refs[2][...]
            elif epi == "loss":
                err = acc + e_refs[0][...] - e_refs[1][...]
                acc = err * (1.0 / N)
                per_tok = jnp.sum(err * err, axis=1, keepdims=True) * (1.0 / N)
                red = 0.5 * jnp.sum(per_tok, axis=0, keepdims=True)
            o_ref[...] = acc.astype(out_dtype)
            if reducing:

                @pl.when(first)
                def _():
                    r_ref[...] = red

                @pl.when(jnp.logical_not(first))
                def _():
                    r_ref[...] += red

        if nk == 1:
            finish(part)
        else:
            acc_ref = refs[pos]
            k = pl.program_id(2)

            @pl.when(k == 0)
            def _():
                acc_ref[...] = part

            @pl.when(k > 0)
            def _():
                acc_ref[...] += part

            @pl.when(k == nk - 1)
            def _():
                finish(acc_ref[...])

    def spec(shape, index):
        if n_outer:
            return pl.BlockSpec(shape, lambda j, i, k: index(i, j, k))
        return pl.BlockSpec(shape, index)

    if mode == "tn":
        a_spec = spec((tk, tm), lambda i, j, k: (k, i))
    else:
        a_spec = spec((tm, tk), lambda i, j, k: (i, k))
    if mode == "nt":
        b_spec = spec((tn, tk), lambda i, j, k: (j, k))
    else:
        b_spec = spec((tk, tn), lambda i, j, k: (k, j))
    in_specs, ops = [a_spec, b_spec], [a, b]
    if pro == "rms":
        w = pro_g.shape[1]
        assert (tm if mode == "tn" else tk) == w, name
        in_specs.append(spec((1, w), lambda i, j, k: (0, 0)))
        ops.append(pro_g)
    for op in epi_ops:
        if op.shape[0] == 1:
            in_specs.append(spec((1, tn), lambda i, j, k: (0, j)))
        else:
            in_specs.append(spec((tm, tn), lambda i, j, k: (i, j)))
        ops.append(op)
    out_specs = [spec((tm, tn), lambda i, j, k: (i, j))]
    out_shape = [jax.ShapeDtypeStruct((M, N), out_dtype)]
    if reducing:
        width = N if epi == "rms_bwd" else 1
        out_specs.append(spec((1, width), lambda i, j, k: (0, 0)))
        out_shape.append(jax.ShapeDtypeStruct((1, width), F32))
    grid = (N // tn, M // tm, nk) if n_outer else (M // tm, N // tn, nk)
    outs = pl.pallas_call(
        body, name=name, grid=grid,
        in_specs=in_specs, out_specs=out_specs, out_shape=out_shape,
        scratch_shapes=[pltpu.VMEM((tm, tn), F32)] if nk > 1 else [],
        compiler_params=_cp(("arbitrary" if reducing else "parallel", "parallel", "arbitrary")),
    )(*ops)
    return outs if reducing else outs[0]


def _head_select(first_lane):
    l = lax.broadcasted_iota(jnp.int32, (LANES, HW), 0)
    c = lax.broadcasted_iota(jnp.int32, (LANES, HW), 1)
    return (l == first_lane + c // DH).astype(F32)


def _conv_taps(buf, cw, ts):
    c = cw[3:4, :] * buf[8:8 + ts, :]
    for j in range(3):
        k = 3 - j
        c = c + cw[j:j + 1, :] * buf[8 - k:8 - k + ts, :]
    return c


def _pre_fwd(proj, conv_w, alog_f, dtb_f, gsq, gsk, gmq, S):
    ts = _narrow_tile(S)
    hb = ts // 8

    def body(qkv_ref, halo_ref, ab_ref, sq_ref, sk_ref, sv_ref, mq_ref, cw_ref, al_ref, dt_ref, gsq_ref, gsk_ref,
             gmq_ref, gq_o, gk_o, gv_o, gf_o, bf_o, sqn_o, skn_o, svb_o, qmn_o, buf):
        i = pl.program_id(0)
        buf[0:8, :] = jnp.where(i == 0, 0.0, halo_ref[...])
        buf[8:8 + ts, :] = qkv_ref[...]
        c = _conv_taps(buf, cw_ref[...], ts)
        a = c * _sigmoid(c)
        for h in range(NH):
            q = a[:, h * DH:(h + 1) * DH]
            k = a[:, HW + h * DH:HW + (h + 1) * DH]
            gq_o[:, _hs(h)] = q * (lax.rsqrt(jnp.sum(q * q, axis=-1, keepdims=True) + EPS) * DH ** -0.5)
            gk_o[:, _hs(h)] = k * lax.rsqrt(jnp.sum(k * k, axis=-1, keepdims=True) + EPS)
            sqn_o[:, _hs(h)] = _rms(sq_ref[:, _hs(h)], gsq_ref[...])[0].astype(BF16)
            skn_o[:, _hs(h)] = _rms(sk_ref[:, _hs(h)], gsk_ref[...])[0].astype(BF16)
            qmn_o[:, _hs(h)] = _rms(mq_ref[:, _hs(h)], gmq_ref[...])[0].astype(BF16)
        gv_o[...] = a[:, 2 * HW:3 * HW]
        svb_o[...] = sv_ref[...].astype(BF16)
        ab = ab_ref[:, 0:LANES]
        a_bc = _dxr(ab, _head_select(0))
        b_bc = _dxr(ab, _head_select(NH))
        gf_o[...] = -jnp.exp(al_ref[...]) * _softplus(a_bc + dt_ref[...])
        bf_o[...] = _sigmoid(b_bc)

    row = lambda cb: pl.BlockSpec((ts, HW), lambda i: (i, cb))
    full = lambda r, c: pl.BlockSpec((r, c), lambda i: (0, 0))
    f32o = jax.ShapeDtypeStruct((S, HW), F32)
    bfo = jax.ShapeDtypeStruct((S, HW), BF16)
    return pl.pallas_call(
        body, name="pre_fwd", grid=(S // ts,),
        in_specs=[pl.BlockSpec((ts, 3 * HW), lambda i: (i, 0)),
                  pl.BlockSpec((8, 3 * HW), lambda i: (jnp.maximum(i * hb - 1, 0), 0)),
                  row(CB_AB), row(CB_SQ), row(CB_SK), row(CB_SV), row(CB_MQ),
                  full(4, 3 * HW), full(1, HW), full(1, HW), full(1, DH), full(1, DH), full(1, DH)],
        out_specs=[pl.BlockSpec((ts, HW), lambda i: (i, 0))] * 9,
        out_shape=[f32o, f32o, f32o, f32o, f32o, bfo, bfo, bfo, bfo],
        scratch_shapes=[pltpu.VMEM((ts + 8, 3 * HW), F32)],
        compiler_params=_cp(("parallel",)),
    )(proj, proj, proj, proj, proj, proj, proj, conv_w, alog_f, dtb_f, gsq, gsk, gmq)


def _pre_bwd(proj, conv_w, alog_f, dtb_f, gsq, gsk, dgq, dgk, dgv, dgf, dbf, dsqn, dskn, S):
    ts = _narrow_tile(S)
    hb = ts // 8

    def body(qkv_ref, halo_ref, ab_ref, sq_ref, sk_ref, cw_ref, al_ref, dt_ref, gsq_ref, gsk_ref,
             dgq_ref, dgk_ref, dgv_ref, dgf_ref, dbf_ref, dsqn_ref, dskn_ref,
             dc_o, dab_o, dsq_o, dsk_o, dcw_o, dal_o, ddt_o, dgsq_o, dgsk_o, buf):
        i = pl.program_id(0)

        @pl.when(i == 0)
        def _():
            dcw_o[...] = jnp.zeros_like(dcw_o)
            dal_o[...] = jnp.zeros_like(dal_o)
            ddt_o[...] = jnp.zeros_like(ddt_o)
            dgsq_o[...] = jnp.zeros_like(dgsq_o)
            dgsk_o[...] = jnp.zeros_like(dgsk_o)

        buf[0:8, :] = jnp.where(i == 0, 0.0, halo_ref[...])
        buf[8:8 + ts, :] = qkv_ref[...]
        c = _conv_taps(buf, cw_ref[...], ts)
        sg = _sigmoid(c)
        a = c * sg
        dsilu = sg * (1.0 + c * (1.0 - sg))
        dgsq = jnp.zeros((1, DH), F32)
        dgsk = jnp.zeros((1, DH), F32)
        for h in range(NH):
            q = a[:, h * DH:(h + 1) * DH]
            k = a[:, HW + h * DH:HW + (h + 1) * DH]
            nq = lax.rsqrt(jnp.sum(q * q, axis=-1, keepdims=True) + EPS)
            nk = lax.rsqrt(jnp.sum(k * k, axis=-1, keepdims=True) + EPS)
            dyq = dgq_ref[:, _hs(h)]
            dyk = dgk_ref[:, _hs(h)]
            dq = (nq * dyq - q * (nq * nq * nq) * jnp.sum(dyq * q, axis=-1, keepdims=True)) * DH ** -0.5
            dk = nk * dyk - k * (nk * nk * nk) * jnp.sum(dyk * k, axis=-1, keepdims=True)
            dc_o[:, h * DH:(h + 1) * DH] = dq * dsilu[:, h * DH:(h + 1) * DH]
            dc_o[:, HW + h * DH:HW + (h + 1) * DH] = dk * dsilu[:, HW + h * DH:HW + (h + 1) * DH]
            x = sq_ref[:, _hs(h)]
            _, r = _rms(x, gsq_ref[...])
            dx, dg = _rms_bwd(dsqn_ref[:, _hs(h)], x, gsq_ref[...], r)
            dsq_o[:, _hs(h)] = dx.astype(BF16)
            dgsq = dgsq + dg
            x = sk_ref[:, _hs(h)]
            _, r = _rms(x, gsk_ref[...])
            dx, dg = _rms_bwd(dskn_ref[:, _hs(h)], x, gsk_ref[...], r)
            dsk_o[:, _hs(h)] = dx.astype(BF16)
            dgsk = dgsk + dg
        dc_o[:, 2 * HW:3 * HW] = dgv_ref[...] * dsilu[:, 2 * HW:3 * HW]
        dgsq_o[...] += dgsq
        dgsk_o[...] += dgsk
        dc = dc_o[...]
        for j in range(4):
            k = 3 - j
            dcw_o[j:j + 1, :] += jnp.sum(dc * buf[8 - k:8 - k + ts, :], axis=0, keepdims=True)
        ab = ab_ref[:, 0:LANES]
        a_bc = _dxr(ab, _head_select(0))
        b_bc = _dxr(ab, _head_select(NH))
        pre = a_bc + dt_ref[...]
        ea = jnp.exp(al_ref[...])
        dgf = dgf_ref[...]
        dal_o[...] += jnp.sum(dgf * (-ea * _softplus(pre)), axis=0, keepdims=True)
        da = dgf * (-ea * _sigmoid(pre))
        ddt_o[...] += jnp.sum(da, axis=0, keepdims=True)
        beta = _sigmoid(b_bc)
        db = dbf_ref[...] * beta * (1.0 - beta)
        lane = lax.broadcasted_iota(jnp.int32, (ts, LANES), 1)
        dab = jnp.zeros((ts, LANES), F32)
        for h in range(NH):
            dab = dab + jnp.where(lane == h, da[:, _hs(h)], 0.0) + jnp.where(lane == NH + h, db[:, _hs(h)], 0.0)
        dab_o[:, 0:LANES] = dab.astype(BF16)
        dab_o[:, LANES:HW] = jnp.zeros((ts, HW - LANES), BF16)

    row = lambda cb: pl.BlockSpec((ts, HW), lambda i: (i, cb))
    full = lambda r, c: pl.BlockSpec((r, c), lambda i: (0, 0))
    t512 = pl.BlockSpec((ts, HW), lambda i: (i, 0))
    return pl.pallas_call(
        body, name="pre_bwd", grid=(S // ts,),
        in_specs=[pl.BlockSpec((ts, 3 * HW), lambda i: (i, 0)),
                  pl.BlockSpec((8, 3 * HW), lambda i: (jnp.maximum(i * hb - 1, 0), 0)),
                  row(CB_AB), row(CB_SQ), row(CB_SK),
                  full(4, 3 * HW), full(1, HW), full(1, HW), full(1, DH), full(1, DH)] + [t512] * 7,
        out_specs=[pl.BlockSpec((ts, 3 * HW), lambda i: (i, 0)), t512, t512, t512,
                   full(4, 3 * HW), full(1, HW), full(1, HW), full(1, DH), full(1, DH)],
        out_shape=[jax.ShapeDtypeStruct((S, 3 * HW), F32)] + [jax.ShapeDtypeStruct((S, HW), BF16)] * 3
        + [jax.ShapeDtypeStruct((4, 3 * HW), F32), jax.ShapeDtypeStruct((1, HW), F32),
           jax.ShapeDtypeStruct((1, HW), F32), jax.ShapeDtypeStruct((1, DH), F32),
           jax.ShapeDtypeStruct((1, DH), F32)],
        scratch_shapes=[pltpu.VMEM((ts + 8, 3 * HW), F32)],
        compiler_params=_cp(("arbitrary",)),
    )(proj, proj, proj, proj, proj, conv_w, alog_f, dtb_f, gsq, gsk, dgq, dgk, dgv, dgf, dbf, dsqn, dskn)


def _conv_bwd(dc, conv_w, S):
    ts = _row_tile(S)
    hb = ts // 8
    n = S // ts

    def body(dc_ref, halo_ref, cw_ref, o_ref, buf):
        i = pl.program_id(0)
        buf[0:ts, :] = dc_ref[...]
        buf[ts:ts + 8, :] = jnp.where(i == n - 1, 0.0, halo_ref[...])
        cw = cw_ref[...]
        acc = cw[3:4, :] * buf[0:ts, :]
        for k in range(1, 4):
            acc = acc + cw[3 - k:4 - k, :] * buf[k:k + ts, :]
        o_ref[...] = acc.astype(BF16)

    return pl.pallas_call(
        body, name="conv_bwd", grid=(n,),
        in_specs=[pl.BlockSpec((ts, 3 * HW), lambda i: (i, 0)),
                  pl.BlockSpec((8, 3 * HW), lambda i: (jnp.minimum((i + 1) * hb, S // 8 - 1), 0)),
                  pl.BlockSpec((4, 3 * HW), lambda i: (0, 0))],
        out_specs=pl.BlockSpec((ts, 3 * HW), lambda i: (i, 0)),
        out_shape=jax.ShapeDtypeStruct((S, 3 * HW), BF16),
        scratch_shapes=[pltpu.VMEM((ts + 8, 3 * HW), F32)],
        compiler_params=_cp(("parallel",)),
    )(dc, dc, conv_w)


def _gdn_masks():
    r = lax.broadcasted_iota(jnp.int32, (PAIR, PAIR), 0)
    c = lax.broadcasted_iota(jnp.int32, (PAIR, PAIR), 1)
    same = ((r >= CHUNK) & (c >= CHUNK)) | ((r < CHUNK) & (c < CHUNK))
    return dict(r=r, same=same, tril=same & (r >= c), strict=same & (r > c), triu=same & (c >= r), eye=r == c,
                in_a=r < CHUNK, last_a=r == CHUNK - 1, last_b=r == PAIR - 1)


def _each(fn, *cols):
    return [fn(*xs) for xs in zip(*cols)]


def _mul(a, b):
    return a * b


def _top(x):
    return x[:CHUNK]


def _bot(x):
    return x[CHUNK:]


def _rows(a, b):
    return jnp.concatenate([a, b], axis=0)


def _tri_inv(lm, eye):
    eye_f = eye.astype(F32)
    p = _each(lambda l: eye_f - l, lm)
    lp = _each(lambda l: _dg(l, l), lm)
    for it in range(5):
        p = _each(lambda a, b: a + _dg(a, b), p, lp)
        if it < 4:
            lp = _each(lambda b: _dg(b, b), lp)
    return p


def _gdn_block(m, q, k, v, g, beta):
    tril_f = m["tril"].astype(F32)
    col_sum = lambda mask: (lambda x: jnp.sum(jnp.where(mask, x, 0.0), axis=0, keepdims=True))
    gc = _each(lambda x: _dxl(tril_f, x), g)
    gcr = _each(col_sum(m["eye"]), gc)
    gam = _each(lambda a, b: jnp.where(m["tril"], jnp.exp(jnp.minimum(a - b, 0.0)), 0.0), gc, gcr)
    kb = _each(_mul, k, beta)
    vb = _each(_mul, v, beta)
    lm = _each(lambda a, b, c: jnp.where(m["strict"], _dg(a, b, NT) * c, 0.0), kb, k, gam)
    t = _tri_inv(lm, m["eye"])
    eg = _each(jnp.exp, gc)
    kbe = _each(_mul, kb, eg)
    u = _each(_dg, t, vb)
    w = _each(_dg, t, kbe)
    aqk = _each(lambda a, b, c: jnp.where(m["tril"], _dg(a, b, NT) * c, 0.0), q, k, gam)
    qd = _each(_mul, q, eg)
    ga = _each(col_sum(m["last_a"]), gc)
    gb = _each(col_sum(m["last_b"]), gc)
    e2 = _each(lambda a, b, c: jnp.exp(jnp.where(m["in_a"], a, b) - c), ga, gb, gc)
    kd = _each(_mul, k, e2)
    return dict(u=u, w=w, aqk=aqk, qd=qd, kd=kd, gam=gam, kb=kb, vb=vb, lm=lm, t=t, eg=eg, kbe=kbe, e2=e2,
                gla=_each(jnp.exp, ga), glb=_each(jnp.exp, gb))


def _gdn_fwd(gq, gk, gv, gf, bf, S):
    nb = S // PAIR

    def body(q_ref, k_ref, v_ref, g_ref, b_ref, o_ref, st_ref, s_scr):
        @pl.when(pl.program_id(0) == 0)
        def _():
            s_scr[...] = jnp.zeros_like(s_scr)

        m = _gdn_masks()
        heads = lambda ref: [ref[:, _hs(h)] for h in range(NH)]
        f = _gdn_block(m, heads(q_ref), heads(k_ref), heads(v_ref), heads(g_ref), heads(b_ref))
        u, w, qd, kd = f["u"], f["w"], f["qd"], f["kd"]
        s0 = [s_scr[h * DH:(h + 1) * DH, :] for h in range(NH)]
        vna = _each(lambda a, b, s: _top(a) - _dg(_top(b), s), u, w, s0)
        oa = _each(lambda a, s: _dg(_top(a), s), qd, s0)
        s1 = _each(lambda s, gl, a, vn: s * gl + _dg(_top(a), vn, TN), s0, f["gla"], kd, vna)
        vnb = _each(lambda a, b, s: _bot(a) - _dg(_bot(b), s), u, w, s1)
        ob = _each(lambda a, s: _dg(_bot(a), s), qd, s1)
        s2 = _each(lambda s, gl, a, vn: s * gl + _dg(_bot(a), vn, TN), s1, f["glb"], kd, vnb)
        outs = _each(lambda a, b, c, va, vb: _rows(a, b) + _dg(c, _rows(va, vb)), oa, ob, f["aqk"], vna, vnb)
        o_ref[...] = jnp.concatenate(outs, axis=1)
        st_ref[...] = jnp.concatenate(s0 + s1, axis=0)
        s_scr[...] = jnp.concatenate(s2, axis=0)

    blk = pl.BlockSpec((PAIR, HW), lambda i: (i, 0))
    return pl.pallas_call(
        body, name="gdn_fwd", grid=(nb,),
        in_specs=[blk] * 5,
        out_specs=[blk, pl.BlockSpec((2 * NH * DH, DH), lambda i: (i, 0))],
        out_shape=[jax.ShapeDtypeStruct((S, HW), F32), jax.ShapeDtypeStruct((nb * 2 * NH * DH, DH), F32)],
        scratch_shapes=[pltpu.VMEM((NH * DH, DH), F32)],
        compiler_params=_cp(("arbitrary",)),
    )(gq, gk, gv, gf, bf)


def _gdn_bwd(gq, gk, gv, gf, bf, states, do, S):
    nb = S // PAIR

    def body(q_ref, k_ref, v_ref, g_ref, b_ref, st_ref, do_ref, dq_o, dk_o, dv_o, dg_o, db_o, ds_scr):
        @pl.when(pl.program_id(0) == 0)
        def _():
            ds_scr[...] = jnp.zeros_like(ds_scr)

        m = _gdn_masks()
        ones = jnp.ones((PAIR, PAIR), F32)
        heads = lambda ref: [ref[:, _hs(h)] for h in range(NH)]
        q, k, v, beta, do = heads(q_ref), heads(k_ref), heads(v_ref), heads(b_ref), heads(do_ref)
        f = _gdn_block(m, q, k, v, heads(g_ref), beta)
        u, w, aqk, qd, kd, t = f["u"], f["w"], f["aqk"], f["qd"], f["kd"], f["t"]
        s0 = [st_ref[h * DH:(h + 1) * DH, :] for h in range(NH)]
        s1 = [st_ref[(NH + h) * DH:(NH + h + 1) * DH, :] for h in range(NH)]
        ds2 = [ds_scr[h * DH:(h + 1) * DH, :] for h in range(NH)]
        total = lambda a, b: jnp.sum(jnp.sum(a * b, axis=1, keepdims=True), axis=0, keepdims=True)
        vna = _each(lambda a, b, s: _top(a) - _dg(_top(b), s), u, w, s0)
        vnb = _each(lambda a, b, s: _bot(a) - _dg(_bot(b), s), u, w, s1)
        dvn_i = _each(lambda a, b: _dg(a, b, TN), aqk, do)
        dvnb = _each(lambda a, b, s: _bot(a) + _dg(_bot(b), s), dvn_i, kd, ds2)
        dqdb = _each(lambda a, s: _dg(_bot(a), s, NT), do, s1)
        dkdb = _each(lambda a, s: _dg(a, s, NT), vnb, ds2)
        dglb = _each(total, ds2, s1)
        dwb = _each(lambda a, s: -_dg(a, s, NT), dvnb, s1)
        ds1 = _each(lambda s, gl, a, b, c, d: s * gl + _dg(_bot(a), _bot(b), TN) - _dg(_bot(c), d, TN),
                    ds2, f["glb"], qd, do, w, dvnb)
        dvna = _each(lambda a, b, s: _top(a) + _dg(_top(b), s), dvn_i, kd, ds1)
        dqda = _each(lambda a, s: _dg(_top(a), s, NT), do, s0)
        dkda = _each(lambda a, s: _dg(a, s, NT), vna, ds1)
        dgla = _each(total, ds1, s0)
        dwa = _each(lambda a, s: -_dg(a, s, NT), dvna, s0)
        ds0 = _each(lambda s, gl, a, b, c, d: s * gl + _dg(_top(a), _top(b), TN) - _dg(_top(c), d, TN),
                    ds1, f["gla"], qd, do, w, dvna)
        dvn, dqd, dkd, dw = (_each(_rows, a, b) for a, b in ((dvna, dvnb), (dqda, dqdb), (dkda, dkdb), (dwa, dwb)))
        daqk = _each(lambda a, va, vb: jnp.where(m["tril"], _dg(a, _rows(va, vb), NT), 0.0), do, vna, vnb)
        dt = _each(lambda a, b, c, d: _dg(a, b, NT) + _dg(c, d, NT), dvn, f["vb"], dw, f["kbe"])
        dvb = _each(lambda a, b: _dg(a, b, TN), t, dvn)
        dkbe = _each(lambda a, b: _dg(a, b, TN), t, dw)
        dtt = _each(lambda a, b: _dg(a, b, NT), dt, t)
        dl = _each(lambda a, b: -jnp.where(m["strict"], _dg(a, b, TN), 0.0), t, dtt)
        dm = _each(_mul, dl, f["gam"])
        dn = _each(_mul, daqk, f["gam"])
        dkb = _each(lambda a, b, c, d: _dg(a, b) + c * d, dm, k, dkbe, f["eg"])
        dks = _each(lambda a, b, c, d, e, g, h, i: _dg(a, b, TN) + _dg(c, d, TN) + e * g + h * i,
                    dm, f["kb"], dn, q, dkd, f["e2"], beta, dkb)
        dqs = _each(lambda a, b, c, d: _dg(a, b) + c * d, dn, k, dqd, f["eg"])
        gm = _each(lambda a, b, c, d: a * b + c * d, dl, f["lm"], daqk, aqk)
        dkdkd = _each(_mul, dkd, kd)
        dgc = _each(lambda a, b, c, d, e, g: _dxr(a + b * c + d * e - g, ones) - _dxr(a, ones, TN),
                    gm, dqd, qd, dkbe, f["kbe"], dkdkd)
        same_f = m["same"].astype(F32)
        chunk_tot = _each(lambda a: _dxl(same_f, _dxr(a, ones)), dkdkd)
        last = m["last_a"] | m["last_b"]
        dgc = _each(lambda a, b, ga, gla, gb, glb: a + jnp.where(last, b + jnp.where(m["in_a"], ga * gla, gb * glb), 0.0),
                    dgc, chunk_tot, dgla, f["gla"], dglb, f["glb"])
        dbs = _each(lambda a, b, c, d: _dxr(a * b + c * d, ones), dkb, k, dvb, v)
        dvs = _each(_mul, beta, dvb)
        triu_f = m["triu"].astype(F32)
        dgs = _each(lambda a: _dxl(triu_f, a), dgc)
        for ref, parts in ((dq_o, dqs), (dk_o, dks), (dv_o, dvs), (dg_o, dgs), (db_o, dbs)):
            ref[...] = jnp.concatenate(parts, axis=1)
        ds_scr[...] = jnp.concatenate(ds0, axis=0)

    blk = pl.BlockSpec((PAIR, HW), lambda i: (nb - 1 - i, 0))
    o = jax.ShapeDtypeStruct((S, HW), F32)
    return pl.pallas_call(
        body, name="gdn_bwd", grid=(nb,),
        in_specs=[blk] * 5 + [pl.BlockSpec((2 * NH * DH, DH), lambda i: (nb - 1 - i, 0)), blk],
        out_specs=[blk] * 5, out_shape=[o] * 5,
        scratch_shapes=[pltpu.VMEM((NH * DH, DH), F32)],
        compiler_params=_cp(("arbitrary",)),
    )(gq, gk, gv, gf, bf, states, do)


SB_T = 256
SB_GROUP = 4
SB_GROUP_BWD = 4
SB_SINGLES = 1
SB_DEAD = -110.0


def _group_sizes(g):
    sizes = []
    while g >= 1:
        sizes.append(g)
        g //= 2
    return sizes


def _sb_iotas(t):
    return lax.broadcasted_iota(jnp.int32, (t, t), 0), lax.broadcasted_iota(jnp.int32, (t, t), 1)


def _sb_scores(q, k, mask):
    z = _dot(q, k, NT) * DH ** -0.5
    ls = jnp.minimum(z, 0.0) - jnp.log(1.0 + jnp.exp(-jnp.abs(z)))
    lneg = ls - z
    if mask is not None:
        lneg = jnp.where(mask, lneg, 0.0)
    return ls, lneg


def _prefix(x, u):
    xh, xl = _split(x, 2)
    return _dot(xh, u) + _dot(xl, u)


def _sb_fwd(sqn, skn, svb, S):
    t = min(SB_T, S)

    def body(q_ref, k_ref, v_ref, o_ref, t_ref, cnt_ref):
        qb = pl.program_id(1)
        q = q_ref[...]
        r, c = _sb_iotas(t)
        diag = c < r
        u_after = (r > c).astype(BF16)

        def tiles(k0s, run, masks):
            sc = _each(lambda k0, m: _sb_scores(q, k_ref[pl.ds(k0, t), :], m), k0s, masks)
            ls, lneg = [s[0] for s in sc], [s[1] for s in sc]
            sums = _each(lambda x: jnp.sum(x, axis=1, keepdims=True), lneg)
            pre = _each(lambda x: _prefix(x, u_after), lneg)
            runs = [run]
            for s in sums:
                runs.append(runs[-1] + s)
            att = _each(lambda a, b, rn: jnp.exp(a + (rn + b)), ls, pre, runs[:-1])
            att = _each(lambda a, m: a if m is None else jnp.where(m, a, 0.0), att, masks)
            parts = _each(lambda a, k0: _dot(a.astype(BF16), v_ref[pl.ds(k0, t), :]), att, k0s)
            return sum(parts[1:], parts[0]), runs[-1]

        left = jnp.full((t, t), qb > 0)
        acc, run = tiles([pl.multiple_of(qb * t, t), pl.multiple_of(jnp.maximum(qb - 1, 0) * t, t)],
                         jnp.zeros((t, 1), F32), [diag, left])

        def alive(run):
            return jnp.max(run) >= SB_DEAD

        carry, done = (0, acc, run, alive(run)), jnp.minimum(qb, 1)
        for size, limit in [(1, SB_SINGLES)] + [(s, None) for s in _group_sizes(SB_GROUP)]:

            def more(c, size=size, done=done, limit=limit):
                i, _, _, go = c
                fits = done + (i + 1) * size <= qb
                return (fits if limit is None else fits & (i < limit)) & go

            def group(c, size=size, done=done):
                i, acc, run, _ = c
                first = qb - 1 - done - size * i
                part, run = tiles([pl.multiple_of((first - j) * t, t) for j in range(size)], run, [None] * size)
                return i + 1, acc + part, run, alive(run)

            n, acc, run, go = lax.while_loop(more, group, (0,) + carry[1:])
            carry, done = (0, acc, run, go), done + n * size
        o_ref[...] = acc.astype(BF16)
        t_ref[...] = jnp.broadcast_to(run, (t, DH))
        cnt_ref[pl.program_id(0), qb] = done

    qspec = pl.BlockSpec((t, DH), lambda h, i: (i, h))
    kspec = pl.BlockSpec((S, DH), lambda h, i: (0, h))
    return pl.pallas_call(
        body, name="sb_fwd", grid=(NH, S // t),
        in_specs=[qspec, kspec, kspec],
        out_specs=[qspec, qspec, pl.BlockSpec(memory_space=pltpu.SMEM)],
        out_shape=[jax.ShapeDtypeStruct((S, HW), BF16), jax.ShapeDtypeStruct((S, HW), F32),
                   jax.ShapeDtypeStruct((NH, S // t), jnp.int32)],
        compiler_params=_cp(("arbitrary", "arbitrary")),
    )(sqn, skn, svb)


def _sb_bwd(sqn, skn, svb, do, tot, walked, S):
    t = min(SB_T, S)

    def body(cnt_ref, q_ref, k_ref, v_ref, do_ref, t_ref, dq_o, dk_o, dv_o, dv_acc):
        qb = pl.program_id(1)

        @pl.when(qb == 0)
        def _():
            dk_o[...] = jnp.zeros_like(dk_o)
            dv_acc[...] = jnp.zeros_like(dv_acc)

        q = q_ref[...]
        do = do_ref[...].astype(BF16)
        tot_l = jnp.concatenate([t_ref[...]] * (t // DH), axis=1)
        r, c = _sb_iotas(t)
        diag = c < r
        u_upto = (r <= c).astype(BF16)
        u_before = (r < c).astype(BF16)

        def tiles(k0s, run_l, run_e, masks):
            rowsum = lambda x: jnp.sum(x, axis=1, keepdims=True)
            masked = lambda xs: _each(lambda a, m: a if m is None else jnp.where(m, a, 0.0), xs, masks)
            ks = [k_ref[pl.ds(k0, t), :] for k0 in k0s]
            vs = [v_ref[pl.ds(k0, t), :] for k0 in k0s]
            sc = _each(lambda k, m: _sb_scores(q, k, m), ks, masks)
            ls, lneg = [s[0] for s in sc], [s[1] for s in sc]
            sums_l = _each(rowsum, lneg)
            pre_l = _each(lambda x: _prefix(x, u_upto), lneg)
            runs_l = [run_l]
            for s in sums_l:
                runs_l.append(runs_l[-1] + s)
            att = masked(_each(lambda a, b, rn: jnp.exp(a + (tot_l - (rn + b))), ls, pre_l, runs_l[:-1]))
            e = _each(lambda v, a: _dot(do, v, NT) * a, vs, att)
            sums_e = _each(rowsum, e)
            pre_e = _each(lambda x: _prefix(x, u_before), e)
            runs_e = [run_e]
            for s in sums_e:
                runs_e.append(runs_e[-1] + s)
            sg = _each(jnp.exp, ls)
            dz = masked(_each(lambda a, b, rn, s: a * (1.0 - s) - (rn + b) * s, e, pre_e, runs_e[:-1], sg))
            dz = _each(lambda a: (a * DH ** -0.5).astype(BF16), dz)
            dvs = _each(lambda a: _dot(a.astype(BF16), do, TN), att)
            dks = _each(lambda a: _dot(a, q, TN), dz)
            dqs = _each(_dot, dz, ks)
            for k0, dv, dk in zip(k0s, dvs, dks):
                dv_acc[pl.ds(k0, t), :] += dv
                dk_o[pl.ds(k0, t), :] += dk
            return sum(dqs[1:], dqs[0]), runs_l[-1], runs_e[-1]

        walked = cnt_ref[pl.program_id(0), qb]
        early = jnp.maximum(walked - 1, 0)
        z1 = jnp.zeros((t, 1), F32)
        carry, done = (jnp.zeros((t, DH), F32), z1, z1), 0
        for size in _group_sizes(SB_GROUP_BWD):
            n = (early - done) // size

            def group(i, carry, size=size, done=done):
                dq, run_l, run_e = carry
                first = qb - walked + done + size * i
                part, run_l, run_e = tiles([pl.multiple_of((first + j) * t, t) for j in range(size)], run_l, run_e,
                                           [None] * size)
                return dq + part, run_l, run_e

            carry = lax.fori_loop(0, n, group, carry)
            done = done + n * size
        dq, run_l, run_e = carry
        left = jnp.full((t, t), qb > 0)
        part, _, _ = tiles([pl.multiple_of(jnp.maximum(qb - 1, 0) * t, t), pl.multiple_of(qb * t, t)], run_l, run_e,
                           [left, diag])
        dq_o[...] = dq + part

        @pl.when(qb == S // t - 1)
        def _():
            dv_o[...] = dv_acc[...].astype(BF16)

    qspec = pl.BlockSpec((t, DH), lambda h, i, cnt: (i, h))
    kspec = pl.BlockSpec((S, DH), lambda h, i, cnt: (0, h))
    o = jax.ShapeDtypeStruct((S, HW), F32)
    return pl.pallas_call(
        body, name="sb_bwd",
        grid_spec=pltpu.PrefetchScalarGridSpec(
            num_scalar_prefetch=1, grid=(NH, S // t),
            in_specs=[qspec, kspec, kspec, qspec, qspec], out_specs=[qspec, kspec, kspec],
            scratch_shapes=[pltpu.VMEM((S, DH), F32)]),
        out_shape=[o, o, jax.ShapeDtypeStruct((S, HW), BF16)],
        compiler_params=_cp(("parallel", "arbitrary")),
    )(walked, sqn, skn, svb, do, tot)


def _mem_probs(qn, kn):
    s = _dot(qn, kn.astype(BF16), NT) * DH ** -0.5
    p = jnp.exp(s - jnp.max(s, axis=-1, keepdims=True))
    return p / jnp.sum(p, axis=-1, keepdims=True)


def _mem_fwd(qmn, kv, gmk, S):
    ts = _row_tile(S)

    def body(q_ref, kv_ref, gk_ref, o_ref):
        for h in range(NH):
            kn, _ = _rms(kv_ref[:, _hs(h)], gk_ref[...])
            p = _mem_probs(q_ref[:, _hs(h)], kn)
            o_ref[:, _hs(h)] = _dbf(p, kv_ref[:, HW + h * DH:HW + (h + 1) * DH]).astype(BF16)

    return pl.pallas_call(
        body, name="mem_fwd", grid=(S // ts,),
        in_specs=[pl.BlockSpec((ts, HW), lambda i: (i, 0)), pl.BlockSpec((NMEM, 2 * HW), lambda i: (0, 0)),
                  pl.BlockSpec((1, DH), lambda i: (0, 0))],
        out_specs=pl.BlockSpec((ts, HW), lambda i: (i, 0)),
        out_shape=jax.ShapeDtypeStruct((S, HW), BF16),
        compiler_params=_cp(("parallel",)),
    )(qmn, kv, gmk)


def _mem_bwd(proj, qmn, kv, gmq, gmk, do, S):
    ts = _row_tile(S)
    n = S // ts

    def body(mq_ref, q_ref, kv_ref, gq_ref, gk_ref, do_ref, dmq_o, dkv_o, dgq_o, dgk_o, dkn_scr):
        i = pl.program_id(0)

        @pl.when(i == 0)
        def _():
            dkv_o[...] = jnp.zeros_like(dkv_o)
            dgq_o[...] = jnp.zeros_like(dgq_o)
            dkn_scr[...] = jnp.zeros_like(dkn_scr)

        dgq = jnp.zeros((1, DH), F32)
        for h in range(NH):
            km = kv_ref[:, _hs(h)]
            vm = kv_ref[:, HW + h * DH:HW + (h + 1) * DH].astype(BF16)
            kn, _ = _rms(km, gk_ref[...])
            qn = q_ref[:, _hs(h)]
            p = _mem_probs(qn, kn)
            dob = do_ref[:, _hs(h)].astype(BF16)
            dkv_o[:, HW + h * DH:HW + (h + 1) * DH] += _dot(p.astype(BF16), dob, TN)
            dp = _dot(dob, vm, NT)
            dsc = (p * (dp - jnp.sum(dp * p, axis=-1, keepdims=True)) * DH ** -0.5).astype(BF16)
            dkn_scr[:, _hs(h)] += _dot(dsc, qn, TN)
            x = mq_ref[:, _hs(h)]
            _, r = _rms(x, gq_ref[...])
            dx, dg = _rms_bwd(_dot(dsc, kn.astype(BF16)), x, gq_ref[...], r)
            dmq_o[:, _hs(h)] = dx.astype(BF16)
            dgq = dgq + dg
        dgq_o[...] += dgq

        @pl.when(i == n - 1)
        def _():
            dgk = jnp.zeros((1, DH), F32)
            for h in range(NH):
                km = kv_ref[:, _hs(h)]
                _, r = _rms(km, gk_ref[...])
                dx, dg = _rms_bwd(dkn_scr[:, _hs(h)], km, gk_ref[...], r)
                dkv_o[:, _hs(h)] = dx
                dgk = dgk + dg
            dgk_o[...] = dgk

    full = lambda r, c: pl.BlockSpec((r, c), lambda i: (0, 0))
    t512 = pl.BlockSpec((ts, HW), lambda i: (i, 0))
    return pl.pallas_call(
        body, name="mem_bwd", grid=(n,),
        in_specs=[pl.BlockSpec((ts, HW), lambda i: (i, CB_MQ)), t512, full(NMEM, 2 * HW), full(1, DH), full(1, DH),
                  t512],
        out_specs=[t512, full(NMEM, 2 * HW), full(1, DH), full(1, DH)],
        out_shape=[jax.ShapeDtypeStruct((S, HW), BF16), jax.ShapeDtypeStruct((NMEM, 2 * HW), F32),
                   jax.ShapeDtypeStruct((1, DH), F32), jax.ShapeDtypeStruct((1, DH), F32)],
        scratch_shapes=[pltpu.VMEM((NMEM, HW), F32)],
        compiler_params=_cp(("arbitrary",)),
    )(proj, qmn, kv, gmq, gmk, do)


def _gated_gdn(o, z, g):
    sg = _sigmoid(z)
    outs, rs = [], []
    for h in range(NH):
        y, r = _rms(o[:, _hs(h)], g)
        outs.append(y * (z[:, _hs(h)] * sg[:, _hs(h)]))
        rs.append(r)
    return jnp.concatenate(outs, axis=1), rs, sg


def _merge_fwd(x, proj, ogdn, osb, omem, ggdn, wbg, wbs, wbm, wo, S):
    ts = _narrow_tile(S)

    def body(x_ref, z_ref, g0_ref, g1_ref, g2_ref, og_ref, os_ref, om_ref, gg_ref, wbg_ref, wbs_ref, wbm_ref,
             wo_ref, x1_o, mix_o):
        on, _, _ = _gated_gdn(og_ref[...], z_ref[...], gg_ref[...])
        mix = (_sigmoid(g0_ref[...]) * _dbf(on, wbg_ref[...]) + _sigmoid(g1_ref[...]) * _dbf(os_ref[...], wbs_ref[...])
               + _sigmoid(g2_ref[...]) * _dbf(om_ref[...], wbm_ref[...]))
        mix_o[...] = mix.astype(BF16)
        x1_o[...] = x_ref[...] + _dbf(mix, wo_ref[...])

    t512 = pl.BlockSpec((ts, HW), lambda i: (i, 0))
    t1k = pl.BlockSpec((ts, D), lambda i: (i, 0))
    gate = lambda j: pl.BlockSpec((ts, D), lambda i: (i, 4 + j))
    full = lambda r, c: pl.BlockSpec((r, c), lambda i: (0, 0))
    return pl.pallas_call(
        body, name="merge_fwd", grid=(S // ts,),
        in_specs=[t1k, pl.BlockSpec((ts, HW), lambda i: (i, CB_Z)), gate(0), gate(1), gate(2), t512, t512, t512,
                  full(1, DH), full(HW, D), full(HW, D), full(HW, D), full(D, D)],
        out_specs=[t1k, t1k],
        out_shape=[jax.ShapeDtypeStruct((S, D), F32), jax.ShapeDtypeStruct((S, D), BF16)],
        compiler_params=_cp(("parallel",)),
    )(x, proj, proj, proj, proj, ogdn, osb, omem, ggdn, wbg, wbs, wbm, wo)


def _merge_bwd(dmix, proj, ogdn, osb, omem, ggdn, wbg, wbs, wbm, S):
    ts = _narrow_tile(S)

    def body(dm_ref, z_ref, g0_ref, g1_ref, g2_ref, og_ref, os_ref, om_ref, gg_ref, wbg_ref, wbs_ref, wbm_ref,
             dgl0_o, dgl1_o, dgl2_o, dog_o, dz_o, dos_o, dom_o, dwbg_o, dwbs_o, dwbm_o, dgg_o):
        @pl.when(pl.program_id(0) == 0)
        def _():
            for ref in (dwbg_o, dwbs_o, dwbm_o, dgg_o):
                ref[...] = jnp.zeros_like(ref)

        dm = dm_ref[...]
        og = og_ref[...]
        z = z_ref[...]
        on, rs, sg = _gated_gdn(og, z, gg_ref[...])
        branch = ((on, g0_ref, wbg_ref, dgl0_o, dwbg_o), (os_ref[...], g1_ref, wbs_ref, dgl1_o, dwbs_o),
                  (om_ref[...], g2_ref, wbm_ref, dgl2_o, dwbm_o))
        dos = []
        for o, g_ref, w_ref, dgl_o, dw_o in branch:
            ob = o.astype(BF16)
            gate = _sigmoid(g_ref[...])
            dgl_o[...] = (dm * _dot(ob, w_ref[...]) * gate * (1.0 - gate)).astype(BF16)
            dy = (dm * gate).astype(BF16)
            dw_o[...] += _dot(ob, dy, TN)
            dos.append(_dot(dy, w_ref[...], NT))
        dos_o[...] = dos[1].astype(BF16)
        dom_o[...] = dos[2].astype(BF16)
        don = dos[0]
        dgg = jnp.zeros((1, DH), F32)
        for h in range(NH):
            oh, zh, sh = og[:, _hs(h)], z[:, _hs(h)], sg[:, _hs(h)]
            y = oh * rs[h] * gg_ref[...]
            dz_o[:, _hs(h)] = (don[:, _hs(h)] * y * (sh * (1.0 + zh * (1.0 - sh)))).astype(BF16)
            dx, dg = _rms_bwd(don[:, _hs(h)] * (zh * sh), oh, gg_ref[...], rs[h])
            dog_o[:, _hs(h)] = dx
            dgg = dgg + dg
        dgg_o[...] += dgg

    t512 = pl.BlockSpec((ts, HW), lambda i: (i, 0))
    t1k = pl.BlockSpec((ts, D), lambda i: (i, 0))
    gate = lambda j: pl.BlockSpec((ts, D), lambda i: (i, 4 + j))
    full = lambda r, c: pl.BlockSpec((r, c), lambda i: (0, 0))
    s1k = jax.ShapeDtypeStruct((S, D), BF16)
    s512 = jax.ShapeDtypeStruct((S, HW), BF16)
    wsh = jax.ShapeDtypeStruct((HW, D), F32)
    return pl.pallas_call(
        body, name="merge_bwd", grid=(S // ts,),
        in_specs=[t1k, pl.BlockSpec((ts, HW), lambda i: (i, CB_Z)), gate(0), gate(1), gate(2), t512, t512, t512,
                  full(1, DH), full(HW, D), full(HW, D), full(HW, D)],
        out_specs=[t1k, t1k, t1k, t512, t512, t512, t512, full(HW, D), full(HW, D), full(HW, D), full(1, DH)],
        out_shape=[s1k, s1k, s1k, jax.ShapeDtypeStruct((S, HW), F32), s512, s512, s512, wsh, wsh, wsh,
                   jax.ShapeDtypeStruct((1, DH), F32)],
        compiler_params=_cp(("arbitrary",)),
    )(dmix, proj, proj, proj, proj, ogdn, osb, omem, ggdn, wbg, wbs, wbm)


def _norm_cast(name, x, g):
    rows = x.shape[0]
    ts = min(_row_tile(rows), rows)

    def body(x_ref, g_ref, o_ref):
        o_ref[...] = _rms(x_ref[...], g_ref[...])[0].astype(BF16)

    t1k = pl.BlockSpec((ts, D), lambda i: (i, 0))
    return pl.pallas_call(
        body, name=name, grid=(rows // ts,), in_specs=[t1k, pl.BlockSpec((1, D), lambda i: (0, 0))], out_specs=t1k,
        out_shape=jax.ShapeDtypeStruct((rows, D), BF16), compiler_params=_cp(("parallel",)),
    )(x, g)


def _norm_bwd(name, dh, x, g, res):
    rows = x.shape[0]
    ts = min(_row_tile(rows), rows)

    def body(*refs):
        dh_ref, x_ref, g_ref = refs[:3]
        dx_o, dg_o = refs[-2:]

        @pl.when(pl.program_id(0) == 0)
        def _():
            dg_o[...] = jnp.zeros_like(dg_o)

        xv = x_ref[...]
        _, r = _rms(xv, g_ref[...])
        dx, dg = _rms_bwd(dh_ref[...], xv, g_ref[...], r)
        dx_o[...] = dx if res is None else dx + refs[3][...]
        dg_o[...] += dg

    t1k = pl.BlockSpec((ts, D), lambda i: (i, 0))
    gsp = pl.BlockSpec((1, D), lambda i: (0, 0))
    ops = [dh, x, g] + ([] if res is None else [res])
    return pl.pallas_call(
        body, name=name, grid=(rows // ts,), in_specs=[t1k, t1k, gsp] + ([] if res is None else [t1k]),
        out_specs=[t1k, gsp],
        out_shape=[jax.ShapeDtypeStruct((rows, D), F32), jax.ShapeDtypeStruct((1, D), F32)],
        compiler_params=_cp(("arbitrary",)),
    )(*ops)


def _adamw(name, gall, w, m, v):
    rows = w.shape[0]
    nsrc = gall.shape[0]
    tr = min(SLAB_TILE, rows)
    assert rows % tr == 0

    def body(g_ref, w_ref, m_ref, v_ref, g_o, d_o, m_o, v_o):
        g = g_ref[0].astype(F32)
        for j in range(1, nsrc):
            g = g + g_ref[j].astype(F32)
        m_new = ADAM_B1 * m_ref[...] + (1.0 - ADAM_B1) * g
        v_new = ADAM_B2 * v_ref[...] + (1.0 - ADAM_B2) * jnp.square(g)
        m_hat = m_new / (1.0 - ADAM_B1 ** ADAM_STEP)
        v_hat = v_new / (1.0 - ADAM_B2 ** ADAM_STEP)
        g_o[...] = g
        d_o[...] = -ADAM_LR * (m_hat / (jnp.sqrt(v_hat) + ADAM_EPS) + ADAM_WD * w_ref[...])
        m_o[...] = m_new
        v_o[...] = v_new

    t = pl.BlockSpec((tr, LANES), lambda i: (i, 0))
    o = jax.ShapeDtypeStruct((rows, LANES), F32)
    return pl.pallas_call(
        body, name=name, grid=(rows // tr,),
        in_specs=[pl.BlockSpec((nsrc, tr, LANES), lambda i: (0, i, 0)), t, t, t],
        out_specs=[t, t, t, t], out_shape=[o, o, o, o],
        compiler_params=_cp(("parallel",)),
    )(gall, w, m, v)


def _pair_sum(mine, theirs):
    rows = mine.shape[1]
    tr = min(SLAB_TILE, rows)
    assert rows % tr == 0
    core = lax.axis_index("c").astype(jnp.int32).reshape(1)

    def body(c_ref, a_ref, b_ref, o_ref):
        o_ref[...] = (a_ref[...].astype(F32) + b_ref[...].astype(F32)).astype(o_ref.dtype)

    blk = pl.BlockSpec((1, tr, LANES), lambda j, i, c_ref: (j, i, 0))
    return pl.pallas_call(
        body, name="pair_sum",
        grid_spec=pltpu.PrefetchScalarGridSpec(
            num_scalar_prefetch=1, grid=(NDEV // 2, rows // tr),
            in_specs=[pl.BlockSpec((1, tr, LANES), lambda j, i, c_ref: (2 * j + c_ref[0], i, 0)), blk],
            out_specs=blk),
        out_shape=jax.ShapeDtypeStruct((NDEV // 2, rows, LANES), mine.dtype),
        compiler_params=_cp(("parallel", "parallel")),
    )(core, mine, theirs)


HBM_SPEC = pl.BlockSpec(memory_space=pltpu.HBM)


def _remote(src, dst, send_sems, recv_sems, k, to):
    return pltpu.make_async_remote_copy(src_ref=src, dst_ref=dst, send_sem=send_sems.at[k], recv_sem=recv_sems.at[k],
                                        device_id=to, device_id_type=pl.DeviceIdType.MESH)


def _gather(name, x):
    rows, cols = x.shape

    def body(x_ref, o_ref, send_sems, recv_sems, local_sem):
        ix, iy, ic = lax.axis_index("x"), lax.axis_index("y"), lax.axis_index("c")
        me, sibling = (ix, iy, ic), (ix, iy, 1 - ic)
        chips = [(1 - ix, iy), (ix, 1 - iy), (1 - ix, 1 - iy)]

        def slab(px, py, pc):
            return o_ref.at[4 * px + 2 * py + pc]

        def copy(k, block, to, src=None):
            return _remote(slab(*block) if src is None else src, slab(*block), send_sems, recv_sems, k, to)

        mine = pltpu.make_async_copy(x_ref, slab(*me), local_sem)
        mine.start()
        first = [copy(0, me, sibling, src=x_ref)]
        first += [copy(1 + j, me, (*chip, ic), src=x_ref) for j, chip in enumerate(chips)]
        for cp in first:
            cp.start()
        passed = [copy(4 + j, (*chip, ic), sibling) for j, chip in enumerate(chips)]
        for j, chip in enumerate(chips):
            copy(1 + j, (*chip, ic), me).wait_recv()
            passed[j].start()
        copy(0, sibling, me).wait_recv()
        for j, chip in enumerate(chips):
            copy(4 + j, (*chip, 1 - ic), me).wait_recv()
        for cp in first + passed:
            cp.wait_send()
        mine.wait()

    return pl.pallas_call(
        body, name=name, in_specs=[HBM_SPEC], out_specs=HBM_SPEC,
        out_shape=jax.ShapeDtypeStruct((NDEV, rows, cols), x.dtype),
        scratch_shapes=[pltpu.SemaphoreType.DMA((NDEV - 1,)), pltpu.SemaphoreType.DMA((NDEV - 1,)),
                        pltpu.SemaphoreType.DMA],
    )(x)


def _sibling_exchange(name, x):
    rows, cols = x.shape[-2:]
    nchip = NDEV // 2

    def body(x_ref, o_ref, send_sems, recv_sems):
        ix, iy, ic = lax.axis_index("x"), lax.axis_index("y"), lax.axis_index("c")
        copies = [_remote(x_ref.at[2 * j + (1 - ic)], o_ref.at[j], send_sems, recv_sems, j, (ix, iy, 1 - ic))
                  for j in range(nchip)]
        for cp in copies:
            cp.start()
        for cp in copies:
            cp.wait()

    return pl.pallas_call(
        body, name=name, in_specs=[HBM_SPEC], out_specs=HBM_SPEC,
        out_shape=jax.ShapeDtypeStruct((nchip, rows, cols), x.dtype),
        scratch_shapes=[pltpu.SemaphoreType.DMA((nchip,)), pltpu.SemaphoreType.DMA((nchip,))],
    )(x)


def _chip_exchange(name, x):
    rows, cols = x.shape[-2:]
    nchip = NDEV // 2

    def body(x_ref, o_ref, send_sems, recv_sems, local_sem):
        ix, iy, ic = lax.axis_index("x"), lax.axis_index("y"), lax.axis_index("c")
        my_chip = 2 * ix + iy
        own = pltpu.make_async_copy(x_ref.at[my_chip], o_ref.at[my_chip], local_sem)
        own.start()
        copies = []
        for k in range(1, nchip):
            px, py = ix ^ (k >> 1), iy ^ (k & 1)
            copies.append(_remote(x_ref.at[2 * px + py], o_ref.at[my_chip], send_sems, recv_sems, k - 1, (px, py, ic)))
        for cp in copies:
            cp.start()
        for cp in copies:
            cp.wait()
        own.wait()

    return pl.pallas_call(
        body, name=name, in_specs=[HBM_SPEC], out_specs=HBM_SPEC,
        out_shape=jax.ShapeDtypeStruct((nchip, rows, cols), x.dtype),
        scratch_shapes=[pltpu.SemaphoreType.DMA((nchip - 1,)), pltpu.SemaphoreType.DMA((nchip - 1,)),
                        pltpu.SemaphoreType.DMA],
    )(x)


COL_SHARDED = {"w_in": (D, D_IN), "w_br_gdn": (HW, D), "w_br_sb": (HW, D), "w_br_mem": (HW, D), "w_up": (D, DFF),
               "conv_w": (4, 3 * HW)}
ROW_SHARDED = {"w_mem_kv": (D, 2 * HW), "w_o": (D, D), "w_down": (DFF, D)}


def _to_slab(p):
    return p.reshape(p.shape[:-2] + (-1, LANES))


def _from_slab(flat, r, c):
    return flat.reshape(flat.shape[:-2] + (r, c))


def _shard_dims(name):
    if name in COL_SHARDED:
        r, c = COL_SHARDED[name]
        return r, c // NDEV
    r, c = ROW_SHARDED[name]
    return r // NDEV, c


def _pack_rows(parts, total):
    flat = jnp.concatenate(parts, axis=-2)
    return jnp.pad(flat, [(0, 0)] * (flat.ndim - 2) + [(0, total - flat.shape[-2]), (0, 0)])


def _pack_shards(vals):
    return _pack_rows([_to_slab(vals[n][0]) for n in BIG], R_BIG)


def _pack_full_grads(grads):
    parts = []
    for name in BIG:
        g = grads[name]
        r, c = _shard_dims(name)
        if name in COL_SHARDED:
            g = g.reshape(r, NDEV, c).transpose(1, 0, 2)
        else:
            g = g.reshape(NDEV, r, c)
        parts.append(_to_slab(g))
    return _pack_rows(parts, R_BIG)


def _unpack_gathered(slabs):
    out, pos = {}, 0
    for name, rows in zip(BIG, BIG_ROWS):
        r, c = _shard_dims(name)
        g = _from_slab(slabs[:, pos:pos + rows], r, c)
        pos += rows
        if name in COL_SHARDED:
            out[name] = g.transpose(1, 0, 2).reshape(r, NDEV * c)
        else:
            out[name] = g.reshape(NDEV * r, c)
    return out


def _unpack_shard(flat, shapes):
    out, pos = {}, 0
    for name, rows in zip(BIG, BIG_ROWS):
        r, c = _shard_dims(name)
        out[name] = _from_slab(flat[pos:pos + rows], r, c).reshape(shapes[name])
        pos += rows
    return out


def _pack_vec(vals):
    row = jnp.concatenate([vals[n] for n in VEC], axis=1)
    return jnp.pad(row, ((0, 0), (0, VEC_WIDTH - row.shape[1])))


def _adamw_vec(gall, w, m, v):
    aligned = [(off, n) for off, n in zip(VEC_OFFSETS, VEC_SIZES) if n % LANES == 0]

    def body(g_ref, w_ref, m_ref, v_ref, *outs):
        g = g_ref[0]
        for j in range(1, NDEV):
            g = g + g_ref[j]
        m_new = ADAM_B1 * m_ref[...] + (1.0 - ADAM_B1) * g
        v_new = ADAM_B2 * v_ref[...] + (1.0 - ADAM_B2) * jnp.square(g)
        m_hat = m_new / (1.0 - ADAM_B1 ** ADAM_STEP)
        v_hat = v_new / (1.0 - ADAM_B2 ** ADAM_STEP)
        delta = -ADAM_LR * (m_hat / (jnp.sqrt(v_hat) + ADAM_EPS) + ADAM_WD * w_ref[...])
        for r, val in enumerate((g, delta, m_new, v_new)):
            outs[r][...] = val
            for i, (off, n) in enumerate(aligned):
                outs[4 + r * len(aligned) + i][...] = val[:, off:off + n]

    full = lambda *shape: pl.BlockSpec(shape, lambda: (0,) * len(shape))
    row = jax.ShapeDtypeStruct((1, VEC_WIDTH), F32)
    out_shape = [row] * 4 + [jax.ShapeDtypeStruct((1, n), F32) for _ in range(4) for _, n in aligned]
    out_specs = [full(1, VEC_WIDTH)] * 4 + [full(1, n) for _ in range(4) for _, n in aligned]
    return pl.pallas_call(
        body, name="adamw_replicated",
        in_specs=[full(NDEV, 1, VEC_WIDTH), full(1, VEC_WIDTH), full(1, VEC_WIDTH), full(1, VEC_WIDTH)],
        out_specs=out_specs, out_shape=out_shape,
    )(gall, w, m, v)


def _unpack_vec(outs, r):
    aligned = [name for name, n in zip(VEC, VEC_SIZES) if n % LANES == 0]
    vals = {name: outs[4 + r * len(aligned) + i] for i, name in enumerate(aligned)}
    for name, off, n in zip(VEC, VEC_OFFSETS, VEC_SIZES):
        if name not in vals:
            vals[name] = outs[r][:, off:off + n]
    return vals


def _pad_w_in(w):
    return jnp.concatenate([w[:, :2048], w[:, 2056:], w[:, 2048:2056], jnp.zeros((D, D_INP - D_IN), w.dtype)], axis=1)


def _unpad_w_in(w):
    return jnp.concatenate([w[:, :2048], w[:, 7168:7176], w[:, 2048:7168]], axis=1)


def _per_head(v):
    return jnp.repeat(v.reshape(NH), DH).reshape(1, HW)


def _local_step(x, mem, target, w, sm):
    S = x.shape[0]
    ts = _row_tile(S)
    alog_f, dtb_f = _per_head(sm["a_log"]), _per_head(sm["dt_bias"])

    h1 = _norm_cast("norm1", x, sm["norm1_g"])
    proj = _mm("in_proj", h1, w["w_in"], "nn", ts, 1536, D, n_outer=True)
    gq, gk, gv, gf, bf, sqn, skn, svb, qmn = _pre_fwd(proj, w["conv_w"], alog_f, dtb_f, sm["sb_q_norm_g"],
                                                      sm["sb_k_norm_g"], sm["mem_q_norm_g"], S)
    ogdn, states = _gdn_fwd(gq, gk, gv, gf, bf, S)
    osb, sb_tot, sb_walked = _sb_fwd(sqn, skn, svb, S)
    kv = _mm("mem_kv", mem, w["w_mem_kv"], "nn", NMEM, D, D, pro="rms", pro_g=sm["mem_norm_g"])
    omem = _mem_fwd(qmn, kv, sm["mem_k_norm_g"], S)
    x1, mix = _merge_fwd(x, proj, ogdn, osb, omem, sm["gdn_norm_g"], w["w_br_gdn"], w["w_br_sb"], w["w_br_mem"],
                         w["w_o"], S)
    h2 = _norm_cast("norm2", x1, sm["norm2_g"])
    up = _mm("mlp_up", h2, w["w_up"], "nn", ts, 2048, D, n_outer=True)
    dy, loss = _mm("mlp_down", up, w["w_down"], "nn", ts, D, 1024, pro="relu2", epi="loss", epi_x=(x1, target))

    g = {}
    dup = _mm("d_up", dy, w["w_down"], "nt", ts, 1024, D, epi="drelu2", epi_x=up, out_dtype=BF16)
    g["w_down"] = _mm("dw_down", up, dy, "tn", 1024, D, 512, pro="relu2")
    g["w_up"] = _mm("dw_up", h2, dup, "tn", D, 1024, 512)
    dx1, g["norm2_g"] = _mm("d_h2", dup, w["w_up"], "nt", ts, D, 1024, epi="rms_bwd", epi_x=(x1, sm["norm2_g"], dy))

    dmix = _mm("d_mix", dx1, w["w_o"], "nt", ts, D, D)
    g["w_o"] = _mm("dw_o", mix, dx1, "tn", D, D, 512)
    (dgl0, dgl1, dgl2, dogdn, dz, dosb, domem, g["w_br_gdn"], g["w_br_sb"], g["w_br_mem"],
     g["gdn_norm_g"]) = _merge_bwd(dmix, proj, ogdn, osb, omem, sm["gdn_norm_g"], w["w_br_gdn"], w["w_br_sb"],
                                   w["w_br_mem"], S)
    dmq, dkv, g["mem_q_norm_g"], g["mem_k_norm_g"] = _mem_bwd(proj, qmn, kv, sm["mem_q_norm_g"], sm["mem_k_norm_g"],
                                                             domem, S)
    g["w_mem_kv"] = _mm("dw_mem_kv", mem, dkv, "tn", D, D, NMEM, pro="rms", pro_g=sm["mem_norm_g"])
    dmn = _mm("d_mem_n", dkv, w["w_mem_kv"], "nt", NMEM, D, D)
    _, g["mem_norm_g"] = _norm_bwd("mem_norm_bwd", dmn, mem, sm["mem_norm_g"], None)
    dsqn, dskn, dsv = _sb_bwd(sqn, skn, svb, dosb, sb_tot, sb_walked, S)
    dgq, dgk, dgv, dgf, dbf = _gdn_bwd(gq, gk, gv, gf, bf, states, dogdn, S)
    dc, dab, dsq, dsk, g["conv_w"], dal_f, ddt_f, g["sb_q_norm_g"], g["sb_k_norm_g"] = _pre_bwd(
        proj, w["conv_w"], alog_f, dtb_f, sm["sb_q_norm_g"], sm["sb_k_norm_g"], dgq, dgk, dgv, dgf, dbf, dsqn, dskn, S)
    g["a_log"] = dal_f.reshape(NH, DH)[:, 0].reshape(1, NH)
    g["dt_bias"] = ddt_f.reshape(NH, DH)[:, 0].reshape(1, NH)
    dqkv = _conv_bwd(dc, w["conv_w"], S)

    dproj = jnp.concatenate([dqkv, dz, dsq, dsk, dsv, dmq, dgl0, dgl1, dgl2, dab], axis=1)
    g["w_in"] = _mm("dw_in", h1, dproj, "tn", D, 1536, 512)
    dx, g["norm1_g"] = _mm("d_h", dproj, w["w_in"], "nt", ts, D, 1536, epi="rms_bwd", epi_x=(x, sm["norm1_g"], dx1))
    return loss[0, 0], dx, g


def kernel(x, mem, norm1_g, w_in, conv_w, a_log, dt_bias, gdn_norm_g, sb_q_norm_g, sb_k_norm_g, mem_norm_g, w_mem_kv, mem_q_norm_g, mem_k_norm_g, w_br_gdn, w_br_sb, w_br_mem, w_o, norm2_g, w_up, w_down, loss_target, m_norm1_g, m_w_in, m_conv_w, m_a_log, m_dt_bias, m_gdn_norm_g, m_sb_q_norm_g, m_sb_k_norm_g, m_mem_norm_g, m_w_mem_kv, m_mem_q_norm_g, m_mem_k_norm_g, m_w_br_gdn, m_w_br_sb, m_w_br_mem, m_w_o, m_norm2_g, m_w_up, m_w_down, v_norm1_g, v_w_in, v_conv_w, v_a_log, v_dt_bias, v_gdn_norm_g, v_sb_q_norm_g, v_sb_k_norm_g, v_mem_norm_g, v_w_mem_kv, v_mem_q_norm_g, v_mem_k_norm_g, v_w_br_gdn, v_w_br_sb, v_w_br_mem, v_w_o, v_norm2_g, v_w_up, v_w_down):
    given = dict(norm1_g=norm1_g, w_in=w_in, conv_w=conv_w, a_log=a_log, dt_bias=dt_bias, gdn_norm_g=gdn_norm_g,
                 sb_q_norm_g=sb_q_norm_g, sb_k_norm_g=sb_k_norm_g, mem_norm_g=mem_norm_g, w_mem_kv=w_mem_kv,
                 mem_q_norm_g=mem_q_norm_g, mem_k_norm_g=mem_k_norm_g, w_br_gdn=w_br_gdn, w_br_sb=w_br_sb,
                 w_br_mem=w_br_mem, w_o=w_o, norm2_g=norm2_g, w_up=w_up, w_down=w_down)
    mom1 = dict(norm1_g=m_norm1_g, w_in=m_w_in, conv_w=m_conv_w, a_log=m_a_log, dt_bias=m_dt_bias,
                gdn_norm_g=m_gdn_norm_g, sb_q_norm_g=m_sb_q_norm_g, sb_k_norm_g=m_sb_k_norm_g,
                mem_norm_g=m_mem_norm_g, w_mem_kv=m_w_mem_kv, mem_q_norm_g=m_mem_q_norm_g,
                mem_k_norm_g=m_mem_k_norm_g, w_br_gdn=m_w_br_gdn, w_br_sb=m_w_br_sb, w_br_mem=m_w_br_mem, w_o=m_w_o,
                norm2_g=m_norm2_g, w_up=m_w_up, w_down=m_w_down)
    mom2 = dict(norm1_g=v_norm1_g, w_in=v_w_in, conv_w=v_conv_w, a_log=v_a_log, dt_bias=v_dt_bias,
                gdn_norm_g=v_gdn_norm_g, sb_q_norm_g=v_sb_q_norm_g, sb_k_norm_g=v_sb_k_norm_g,
                mem_norm_g=v_mem_norm_g, w_mem_kv=v_w_mem_kv, mem_q_norm_g=v_mem_q_norm_g,
                mem_k_norm_g=v_mem_k_norm_g, w_br_gdn=v_w_br_gdn, w_br_sb=v_w_br_sb, w_br_mem=v_w_br_mem, w_o=v_w_o,
                norm2_g=v_norm2_g, w_up=v_w_up, w_down=v_w_down)
    shapes = {n: given[n].shape for n in WEIGHTS}

    w_loc = _pack_shards(given)
    gathered = _gather("gather_weights", w_loc.astype(BF16))
    w = _unpack_gathered(gathered[:, :sum(BIG_ROWS)])
    w["w_in"] = _pad_w_in(w["w_in"])
    conv_loc = jnp.pad(given["conv_w"][0].reshape(-1, LANES), ((0, 2), (0, 0)))
    conv_all = _gather("gather_conv", conv_loc)
    w["conv_w"] = conv_all[:, :6].reshape(NDEV, 4, 3 * HW // NDEV).transpose(1, 0, 2).reshape(4, 3 * HW)
    sm = {n: given[n] for n in SMALL}

    loss, dx, g = _local_step(x[0], mem[0], loss_target[0], w, sm)
    g["w_in"] = _unpad_w_in(g["w_in"])

    g_mine = _pack_full_grads(g).astype(BF16)
    g_pair = _pair_sum(g_mine, _sibling_exchange("scatter_sibling", g_mine))
    g_all = _chip_exchange("scatter_chips", g_pair)
    gb, db, mb, vb = _adamw("adamw_sharded", g_all, w_loc, _pack_shards(mom1), _pack_shards(mom2))
    gs_all = _gather("gather_small_grads", _pack_vec(g))
    vec_outs = _adamw_vec(gs_all, _pack_vec(given), _pack_vec(mom1), _pack_vec(mom2))

    outs = {}
    for r, (prefix, big) in enumerate((("grad_", gb), ("delta_", db), ("new_m_", mb), ("new_v_", vb))):
        vals = _unpack_shard(big, shapes)
        vals.update(_unpack_vec(vec_outs, r))
        for n in WEIGHTS:
            outs[prefix + n] = vals[n]
    loss = lax.psum(loss, ("x", "y", "c"))
    return (loss, dx[None], *[outs[p + n] for p in ("grad_", "delta_", "new_m_", "new_v_") for n in WEIGHTS])
```

```python
import jax
import jax.numpy as jnp
from jax import lax
from jax.experimental import pallas as pl
from jax.experimental.pallas import tpu as pltpu

F32 = jnp.float32
BF16 = jnp.bfloat16

D = 1024
NH = 4
DH = 128
HW = NH * DH
DFF = 4 * D
NMEM = 256
EPS = 1e-6
NDEV = 8
LANES = 128
PAIR = 128
CHUNK = 64
D_IN = 7176
D_INP = 7680
VMEM_LIMIT = 56 * 1024 * 1024

ADAM_LR, ADAM_B1, ADAM_B2, ADAM_EPS, ADAM_WD, ADAM_STEP = 0.001, 0.9, 0.999, 1e-08, 0.01, 10

CB_Z, CB_SQ, CB_SK, CB_SV, CB_MQ, CB_AB = 3, 4, 5, 6, 7, 14

NN = (((1,), (0,)), ((), ()))
NT = (((1,), (1,)), ((), ()))
TN = (((0,), (0,)), ((), ()))

BIG = ("w_in", "w_mem_kv", "w_br_gdn", "w_br_sb", "w_br_mem", "w_o", "w_up", "w_down", "conv_w")
BIG_ROWS = (7176, 1024, 512, 512, 512, 1024, 4096, 4096, 6)
SLAB_ROWS = dict(zip(BIG, BIG_ROWS))
SLAB_TILE = 1216
FIRST = ("w_in", "conv_w")
REST = ("w_mem_kv", "w_br_gdn", "w_br_sb", "w_br_mem", "w_o", "w_up", "w_down")
R_FIRST = 6 * SLAB_TILE
R_REST = 10 * SLAB_TILE
SMALL = ("norm1_g", "a_log", "dt_bias", "gdn_norm_g", "sb_q_norm_g", "sb_k_norm_g", "mem_norm_g",
         "mem_q_norm_g", "mem_k_norm_g", "norm2_g")
VEC = ("norm1_g", "mem_norm_g", "norm2_g", "gdn_norm_g", "sb_q_norm_g", "sb_k_norm_g", "mem_q_norm_g", "mem_k_norm_g",
       "a_log", "dt_bias")
VEC_SIZES = (1024, 1024, 1024, 128, 128, 128, 128, 128, 4, 4)
VEC_OFFSETS = (0, 1024, 2048, 3072, 3200, 3328, 3456, 3584, 3712, 3716)
VEC_WIDTH = 3840
WEIGHTS = ("norm1_g", "w_in", "conv_w", "a_log", "dt_bias", "gdn_norm_g", "sb_q_norm_g", "sb_k_norm_g",
           "mem_norm_g", "w_mem_kv", "mem_q_norm_g", "mem_k_norm_g", "w_br_gdn", "w_br_sb", "w_br_mem",
           "w_o", "norm2_g", "w_up", "w_down")


def _cp(sem=None):
    return pltpu.CompilerParams(dimension_semantics=sem, vmem_limit_bytes=VMEM_LIMIT)


def _dot(a, b, dims=NN):
    return lax.dot_general(a, b, dims, preferred_element_type=F32)


def _dbf(a, b, dims=NN):
    return _dot(a.astype(BF16), b.astype(BF16), dims)


def _split(a, n):
    parts = []
    for _ in range(n):
        h = a.astype(BF16)
        parts.append(h)
        a = a - h.astype(F32)
    return parts


def _dg(a, b, dims=NN):
    return _dbf(a, b, dims)


def _dxr(a, e, dims=NN):
    eb = e.astype(BF16)
    a1, a2, a3 = _split(a, 3)
    return _dot(a1, eb, dims) + (_dot(a2, eb, dims) + _dot(a3, eb, dims))


def _dxl(e, a, dims=NN):
    eb = e.astype(BF16)
    a1, a2, a3 = _split(a, 3)
    return _dot(eb, a1, dims) + (_dot(eb, a2, dims) + _dot(eb, a3, dims))


def _sigmoid(x):
    return 1.0 / (1.0 + jnp.exp(-x))


def _softplus(x):
    return jnp.maximum(x, 0.0) + jnp.log(1.0 + jnp.exp(-jnp.abs(x)))


def _rms(x, g):
    r = lax.rsqrt(jnp.mean(x * x, axis=-1, keepdims=True) + EPS)
    return x * r * g, r


def _rms_bwd(dy, x, g, r):
    dyg = dy * g
    dx = r * (dyg - x * (r * r) * jnp.mean(dyg * x, axis=-1, keepdims=True))
    dg = jnp.sum(dy * (x * r), axis=0, keepdims=True)
    return dx, dg


def _hs(h):
    return slice(h * DH, (h + 1) * DH)


def _row_tile(s):
    return 512 if s >= 2048 else 256


def _narrow_tile(s):
    return min(256, s)


def _mm(name, a, b, mode, tm, tn, tk, pro=None, pro_g=None, epi=None, epi_x=None, out_dtype=F32, n_outer=False,
        comm=None):
    if mode == "tn":
        K, M = a.shape
    else:
        M, K = a.shape
    N = b.shape[0] if mode == "nt" else b.shape[1]
    tm, tn, tk = min(tm, M), min(tn, N), min(tk, K)
    nk = K // tk
    assert M % tm == 0 and N % tn == 0 and K % tk == 0, (name, M, N, K, tm, tn, tk)
    dims = {"nn": NN, "nt": NT, "tn": TN}[mode]
    reducing = epi in ("rms_bwd", "loss")
    assert not reducing or (tn == N and not n_outer), name
    epi_ops = () if epi is None else (epi_x if isinstance(epi_x, tuple) else (epi_x,))

    def body(*refs):
        a_ref, b_ref = refs[0], refs[1]
        pos = 2
        g_ref = None
        if pro == "rms":
            g_ref = refs[pos]
            pos += 1
        e_refs = refs[pos:pos + len(epi_ops)]
        pos += len(epi_ops)
        cx_ref = None
        if comm is not None:
            cx_ref = refs[pos]
            pos += 1
        o_ref = refs[pos]
        pos += 1
        r_ref = None
        if reducing:
            r_ref = refs[pos]
            pos += 1
        if comm is not None:
            steps_of = _gather_steps if comm[0] == "gather" else _chip_steps
            start, forward, finish_comm = steps_of(cx_ref, refs[pos], *refs[-3:])
            pos += 1
            step = (pl.program_id(0) * grid[1] + pl.program_id(1)) * nk + pl.program_id(2)
            total = grid[0] * grid[1] * nk
            pl.when(step == 0)(start)
            pl.when(step == (4 * total) // 5)(forward)
        av = a_ref[...]
        if pro == "rms":
            av, _ = _rms(av.astype(F32), g_ref[...])
        elif pro == "relu2":
            av = jnp.square(jnp.maximum(av, 0.0))
        part = _dbf(av, b_ref[...], dims)
        first = pl.program_id(0) == 0

        def finish(acc):
            red = None
            if epi == "add":
                acc = acc + e_refs[0][...]
            elif epi == "drelu2":
                acc = acc * (2.0 * jnp.maximum(e_refs[0][...], 0.0))
            elif epi == "rms_bwd":
                xv, gv = e_refs[0][...], e_refs[1][...]
                _, r = _rms(xv, gv)
                dx, red = _rms_bwd(acc, xv, gv, r)
                acc = dx + e_refs[2][...]
            elif epi == "loss":
                err = acc + e_refs[0][...] - e_refs[1][...]
                acc = err * (1.0 / N)
                per_tok = jnp.sum(err * err, axis=1, keepdims=True) * (1.0 / N)
                red = 0.5 * jnp.sum(per_tok, axis=0, keepdims=True)
            o_ref[...] = acc.astype(out_dtype)
            if reducing:

                @pl.when(first)
                def _():
                    r_ref[...] = red

                @pl.when(jnp.logical_not(first))
                def _():
                    r_ref[...] += red

        if nk == 1:
            finish(part)
        else:
            acc_ref = refs[pos]
            k = pl.program_id(2)

            @pl.when(k == 0)
            def _():
                acc_ref[...] = part

            @pl.when(k > 0)
            def _():
                acc_ref[...] += part

            @pl.when(k == nk - 1)
            def _():
                finish(acc_ref[...])

        if comm is not None:
            pl.when(step == total - 1)(finish_comm)

    def spec(shape, index):
        if n_outer:
            return pl.BlockSpec(shape, lambda j, i, k: index(i, j, k))
        return pl.BlockSpec(shape, index)

    if mode == "tn":
        a_spec = spec((tk, tm), lambda i, j, k: (k, i))
    else:
        a_spec = spec((tm, tk), lambda i, j, k: (i, k))
    if mode == "nt":
        b_spec = spec((tn, tk), lambda i, j, k: (j, k))
    else:
        b_spec = spec((tk, tn), lambda i, j, k: (k, j))
    in_specs, ops = [a_spec, b_spec], [a, b]
    if pro == "rms":
        w = pro_g.shape[1]
        assert (tm if mode == "tn" else tk) == w, name
        in_specs.append(spec((1, w), lambda i, j, k: (0, 0)))
        ops.append(pro_g)
    for op in epi_ops:
        if op.shape[0] == 1:
            in_specs.append(spec((1, tn), lambda i, j, k: (0, j)))
        else:
            in_specs.append(spec((tm, tn), lambda i, j, k: (i, j)))
        ops.append(op)
    out_specs = [spec((tm, tn), lambda i, j, k: (i, j))]
    out_shape = [jax.ShapeDtypeStruct((M, N), out_dtype)]
    if reducing:
        width = N if epi == "rms_bwd" else 1
        out_specs.append(spec((1, width), lambda i, j, k: (0, 0)))
        out_shape.append(jax.ShapeDtypeStruct((1, width), F32))
    scratch = [pltpu.VMEM((tm, tn), F32)] if nk > 1 else []
    if comm is not None:
        kind, cx = comm
        in_specs.append(HBM_SPEC)
        ops.append(cx)
        out_specs.append(HBM_SPEC)
        out_shape.append(jax.ShapeDtypeStruct((NDEV if kind == "gather" else NDEV // 2,) + cx.shape[-2:], cx.dtype))
        scratch += list(GATHER_SEMS if kind == "gather" else CHIP_SEMS)
    grid = (N // tn, M // tm, nk) if n_outer else (M // tm, N // tn, nk)
    ordered = reducing or comm is not None
    outs = pl.pallas_call(
        body, name=name, grid=grid,
        in_specs=in_specs, out_specs=out_specs, out_shape=out_shape, scratch_shapes=scratch,
        compiler_params=_cp(("arbitrary" if ordered else "parallel", "arbitrary" if comm is not None else "parallel",
                             "arbitrary")),
    )(*ops)
    return outs if len(out_shape) > 1 else outs[0]


def _head_select(first_lane):
    l = lax.broadcasted_iota(jnp.int32, (LANES, HW), 0)
    c = lax.broadcasted_iota(jnp.int32, (LANES, HW), 1)
    return (l == first_lane + c // DH).astype(F32)


def _conv_taps(buf, cw, ts):
    c = cw[3:4, :] * buf[8:8 + ts, :]
    for j in range(3):
        k = 3 - j
        c = c + cw[j:j + 1, :] * buf[8 - k:8 - k + ts, :]
    return c


def _pre_fwd(proj, conv_w, alog_f, dtb_f, gsq, gsk, gmq, S):
    ts = _narrow_tile(S)
    hb = ts // 8

    def body(qkv_ref, halo_ref, ab_ref, sq_ref, sk_ref, sv_ref, mq_ref, cw_ref, al_ref, dt_ref, gsq_ref, gsk_ref,
             gmq_ref, gq_o, gk_o, gv_o, gf_o, bf_o, sqn_o, skn_o, svb_o, qmn_o, buf):
        i = pl.program_id(0)
        buf[0:8, :] = jnp.where(i == 0, 0.0, halo_ref[...])
        buf[8:8 + ts, :] = qkv_ref[...]
        c = _conv_taps(buf, cw_ref[...], ts)
        a = c * _sigmoid(c)
        for h in range(NH):
            q = a[:, h * DH:(h + 1) * DH]
            k = a[:, HW + h * DH:HW + (h + 1) * DH]
            gq_o[:, _hs(h)] = q * (lax.rsqrt(jnp.sum(q * q, axis=-1, keepdims=True) + EPS) * DH ** -0.5)
            gk_o[:, _hs(h)] = k * lax.rsqrt(jnp.sum(k * k, axis=-1, keepdims=True) + EPS)
            sqn_o[:, _hs(h)] = _rms(sq_ref[:, _hs(h)], gsq_ref[...])[0].astype(BF16)
            skn_o[:, _hs(h)] = _rms(sk_ref[:, _hs(h)], gsk_ref[...])[0].astype(BF16)
            qmn_o[:, _hs(h)] = _rms(mq_ref[:, _hs(h)], gmq_ref[...])[0].astype(BF16)
        gv_o[...] = a[:, 2 * HW:3 * HW]
        svb_o[...] = sv_ref[...].astype(BF16)
        ab = ab_ref[:, 0:LANES]
        a_bc = _dxr(ab, _head_select(0))
        b_bc = _dxr(ab, _head_select(NH))
        gf_o[...] = -jnp.exp(al_ref[...]) * _softplus(a_bc + dt_ref[...])
        bf_o[...] = _sigmoid(b_bc)

    row = lambda cb: pl.BlockSpec((ts, HW), lambda i: (i, cb))
    full = lambda r, c: pl.BlockSpec((r, c), lambda i: (0, 0))
    f32o = jax.ShapeDtypeStruct((S, HW), F32)
    bfo = jax.ShapeDtypeStruct((S, HW), BF16)
    return pl.pallas_call(
        body, name="pre_fwd", grid=(S // ts,),
        in_specs=[pl.BlockSpec((ts, 3 * HW), lambda i: (i, 0)),
                  pl.BlockSpec((8, 3 * HW), lambda i: (jnp.maximum(i * hb - 1, 0), 0)),
                  row(CB_AB), row(CB_SQ), row(CB_SK), row(CB_SV), row(CB_MQ),
                  full(4, 3 * HW), full(1, HW), full(1, HW), full(1, DH), full(1, DH), full(1, DH)],
        out_specs=[pl.BlockSpec((ts, HW), lambda i: (i, 0))] * 9,
        out_shape=[f32o, f32o, f32o, f32o, f32o, bfo, bfo, bfo, bfo],
        scratch_shapes=[pltpu.VMEM((ts + 8, 3 * HW), F32)],
        compiler_params=_cp(("parallel",)),
    )(proj, proj, proj, proj, proj, proj, proj, conv_w, alog_f, dtb_f, gsq, gsk, gmq)


def _pre_bwd(proj, conv_w, alog_f, dtb_f, gsq, gsk, dgq, dgk, dgv, dgf, dbf, dsqn, dskn, S):
    ts = _narrow_tile(S)
    hb = ts // 8

    def body(qkv_ref, halo_ref, ab_ref, sq_ref, sk_ref, cw_ref, al_ref, dt_ref, gsq_ref, gsk_ref,
             dgq_ref, dgk_ref, dgv_ref, dgf_ref, dbf_ref, dsqn_ref, dskn_ref,
             dc_o, dab_o, dsq_o, dsk_o, dcw_o, dal_o, ddt_o, dgsq_o, dgsk_o, buf):
        i = pl.program_id(0)

        @pl.when(i == 0)
        def _():
            dcw_o[...] = jnp.zeros_like(dcw_o)
            dal_o[...] = jnp.zeros_like(dal_o)
            ddt_o[...] = jnp.zeros_like(ddt_o)
            dgsq_o[...] = jnp.zeros_like(dgsq_o)
            dgsk_o[...] = jnp.zeros_like(dgsk_o)

        buf[0:8, :] = jnp.where(i == 0, 0.0, halo_ref[...])
        buf[8:8 + ts, :] = qkv_ref[...]
        c = _conv_taps(buf, cw_ref[...], ts)
        sg = _sigmoid(c)
        a = c * sg
        dsilu = sg * (1.0 + c * (1.0 - sg))
        dgsq = jnp.zeros((1, DH), F32)
        dgsk = jnp.zeros((1, DH), F32)
        for h in range(NH):
            q = a[:, h * DH:(h + 1) * DH]
            k = a[:, HW + h * DH:HW + (h + 1) * DH]
            nq = lax.rsqrt(jnp.sum(q * q, axis=-1, keepdims=True) + EPS)
            nk = lax.rsqrt(jnp.sum(k * k, axis=-1, keepdims=True) + EPS)
            dyq = dgq_ref[:, _hs(h)]
            dyk = dgk_ref[:, _hs(h)]
            dq = (nq * dyq - q * (nq * nq * nq) * jnp.sum(dyq * q, axis=-1, keepdims=True)) * DH ** -0.5
            dk = nk * dyk - k * (nk * nk * nk) * jnp.sum(dyk * k, axis=-1, keepdims=True)
            dc_o[:, h * DH:(h + 1) * DH] = dq * dsilu[:, h * DH:(h + 1) * DH]
            dc_o[:, HW + h * DH:HW + (h + 1) * DH] = dk * dsilu[:, HW + h * DH:HW + (h + 1) * DH]
            x = sq_ref[:, _hs(h)]
            _, r = _rms(x, gsq_ref[...])
            dx, dg = _rms_bwd(dsqn_ref[:, _hs(h)], x, gsq_ref[...], r)
            dsq_o[:, _hs(h)] = dx.astype(BF16)
            dgsq = dgsq + dg
            x = sk_ref[:, _hs(h)]
            _, r = _rms(x, gsk_ref[...])
            dx, dg = _rms_bwd(dskn_ref[:, _hs(h)], x, gsk_ref[...], r)
            dsk_o[:, _hs(h)] = dx.astype(BF16)
            dgsk = dgsk + dg
        dc_o[:, 2 * HW:3 * HW] = dgv_ref[...] * dsilu[:, 2 * HW:3 * HW]
        dgsq_o[...] += dgsq
        dgsk_o[...] += dgsk
        dc = dc_o[...]
        for j in range(4):
            k = 3 - j
            dcw_o[j:j + 1, :] += jnp.sum(dc * buf[8 - k:8 - k + ts, :], axis=0, keepdims=True)
        ab = ab_ref[:, 0:LANES]
        a_bc = _dxr(ab, _head_select(0))
        b_bc = _dxr(ab, _head_select(NH))
        pre = a_bc + dt_ref[...]
        ea = jnp.exp(al_ref[...])
        dgf = dgf_ref[...]
        dal_o[...] += jnp.sum(dgf * (-ea * _softplus(pre)), axis=0, keepdims=True)
        da = dgf * (-ea * _sigmoid(pre))
        ddt_o[...] += jnp.sum(da, axis=0, keepdims=True)
        beta = _sigmoid(b_bc)
        db = dbf_ref[...] * beta * (1.0 - beta)
        lane = lax.broadcasted_iota(jnp.int32, (ts, LANES), 1)
        dab = jnp.zeros((ts, LANES), F32)
        for h in range(NH):
            dab = dab + jnp.where(lane == h, da[:, _hs(h)], 0.0) + jnp.where(lane == NH + h, db[:, _hs(h)], 0.0)
        dab_o[:, 0:LANES] = dab.astype(BF16)
        dab_o[:, LANES:HW] = jnp.zeros((ts, HW - LANES), BF16)

    row = lambda cb: pl.BlockSpec((ts, HW), lambda i: (i, cb))
    full = lambda r, c: pl.BlockSpec((r, c), lambda i: (0, 0))
    t512 = pl.BlockSpec((ts, HW), lambda i: (i, 0))
    return pl.pallas_call(
        body, name="pre_bwd", grid=(S // ts,),
        in_specs=[pl.BlockSpec((ts, 3 * HW), lambda i: (i, 0)),
                  pl.BlockSpec((8, 3 * HW), lambda i: (jnp.maximum(i * hb - 1, 0), 0)),
                  row(CB_AB), row(CB_SQ), row(CB_SK),
                  full(4, 3 * HW), full(1, HW), full(1, HW), full(1, DH), full(1, DH)] + [t512] * 7,
        out_specs=[pl.BlockSpec((ts, 3 * HW), lambda i: (i, 0)), t512, t512, t512,
                   full(4, 3 * HW), full(1, HW), full(1, HW), full(1, DH), full(1, DH)],
        out_shape=[jax.ShapeDtypeStruct((S, 3 * HW), F32)] + [jax.ShapeDtypeStruct((S, HW), BF16)] * 3
        + [jax.ShapeDtypeStruct((4, 3 * HW), F32), jax.ShapeDtypeStruct((1, HW), F32),
           jax.ShapeDtypeStruct((1, HW), F32), jax.ShapeDtypeStruct((1, DH), F32),
           jax.ShapeDtypeStruct((1, DH), F32)],
        scratch_shapes=[pltpu.VMEM((ts + 8, 3 * HW), F32)],
        compiler_params=_cp(("arbitrary",)),
    )(proj, proj, proj, proj, proj, conv_w, alog_f, dtb_f, gsq, gsk, dgq, dgk, dgv, dgf, dbf, dsqn, dskn)


def _conv_bwd(dc, conv_w, S):
    ts = _row_tile(S)
    hb = ts // 8
    n = S // ts

    def body(dc_ref, halo_ref, cw_ref, o_ref, buf):
        i = pl.program_id(0)
        buf[0:ts, :] = dc_ref[...]
        buf[ts:ts + 8, :] = jnp.where(i == n - 1, 0.0, halo_ref[...])
        cw = cw_ref[...]
        acc = cw[3:4, :] * buf[0:ts, :]
        for k in range(1, 4):
            acc = acc + cw[3 - k:4 - k, :] * buf[k:k + ts, :]
        o_ref[...] = acc.astype(BF16)

    return pl.pallas_call(
        body, name="conv_bwd", grid=(n,),
        in_specs=[pl.BlockSpec((ts, 3 * HW), lambda i: (i, 0)),
                  pl.BlockSpec((8, 3 * HW), lambda i: (jnp.minimum((i + 1) * hb, S // 8 - 1), 0)),
                  pl.BlockSpec((4, 3 * HW), lambda i: (0, 0))],
        out_specs=pl.BlockSpec((ts, 3 * HW), lambda i: (i, 0)),
        out_shape=jax.ShapeDtypeStruct((S, 3 * HW), BF16),
        scratch_shapes=[pltpu.VMEM((ts + 8, 3 * HW), F32)],
        compiler_params=_cp(("parallel",)),
    )(dc, dc, conv_w)


def _gdn_masks():
    r = lax.broadcasted_iota(jnp.int32, (PAIR, PAIR), 0)
    c = lax.broadcasted_iota(jnp.int32, (PAIR, PAIR), 1)
    same = ((r >= CHUNK) & (c >= CHUNK)) | ((r < CHUNK) & (c < CHUNK))
    return dict(r=r, same=same, tril=same & (r >= c), strict=same & (r > c), triu=same & (c >= r), eye=r == c,
                in_a=r < CHUNK, last_a=r == CHUNK - 1, last_b=r == PAIR - 1)


def _each(fn, *cols):
    return [fn(*xs) for xs in zip(*cols)]


def _mul(a, b):
    return a * b


def _top(x):
    return x[:CHUNK]


def _bot(x):
    return x[CHUNK:]


def _rows(a, b):
    return jnp.concatenate([a, b], axis=0)


def _tri_inv(lm, eye):
    eye_f = eye.astype(F32)
    p = _each(lambda l: eye_f - l, lm)
    lp = _each(lambda l: _dg(l, l), lm)
    for it in range(5):
        p = _each(lambda a, b: a + _dg(a, b), p, lp)
        if it < 4:
            lp = _each(lambda b: _dg(b, b), lp)
    return p


def _gdn_block(m, q, k, v, g, beta):
    tril_f = m["tril"].astype(F32)
    col_sum = lambda mask: (lambda x: jnp.sum(jnp.where(mask, x, 0.0), axis=0, keepdims=True))
    gc = _each(lambda x: _dxl(tril_f, x), g)
    gcr = _each(col_sum(m["eye"]), gc)
    gam = _each(lambda a, b: jnp.where(m["tril"], jnp.exp(jnp.minimum(a - b, 0.0)), 0.0), gc, gcr)
    kb = _each(_mul, k, beta)
    vb = _each(_mul, v, beta)
    lm = _each(lambda a, b, c: jnp.where(m["strict"], _dg(a, b, NT) * c, 0.0), kb, k, gam)
    t = _tri_inv(lm, m["eye"])
    eg = _each(jnp.exp, gc)
    kbe = _each(_mul, kb, eg)
    u = _each(_dg, t, vb)
    w = _each(_dg, t, kbe)
    aqk = _each(lambda a, b, c: jnp.where(m["tril"], _dg(a, b, NT) * c, 0.0), q, k, gam)
    qd = _each(_mul, q, eg)
    ga = _each(col_sum(m["last_a"]), gc)
    gb = _each(col_sum(m["last_b"]), gc)
    e2 = _each(lambda a, b, c: jnp.exp(jnp.where(m["in_a"], a, b) - c), ga, gb, gc)
    kd = _each(_mul, k, e2)
    return dict(u=u, w=w, aqk=aqk, qd=qd, kd=kd, gam=gam, kb=kb, vb=vb, lm=lm, t=t, eg=eg, kbe=kbe, e2=e2,
                gla=_each(jnp.exp, ga), glb=_each(jnp.exp, gb))


def _gdn_fwd(gq, gk, gv, gf, bf, S):
    nb = S // PAIR

    def body(q_ref, k_ref, v_ref, g_ref, b_ref, o_ref, st_ref, s_scr):
        @pl.when(pl.program_id(0) == 0)
        def _():
            s_scr[...] = jnp.zeros_like(s_scr)

        m = _gdn_masks()
        heads = lambda ref: [ref[:, _hs(h)] for h in range(NH)]
        f = _gdn_block(m, heads(q_ref), heads(k_ref), heads(v_ref), heads(g_ref), heads(b_ref))
        u, w, qd, kd = f["u"], f["w"], f["qd"], f["kd"]
        s0 = [s_scr[h * DH:(h + 1) * DH, :] for h in range(NH)]
        vna = _each(lambda a, b, s: _top(a) - _dg(_top(b), s), u, w, s0)
        oa = _each(lambda a, s: _dg(_top(a), s), qd, s0)
        s1 = _each(lambda s, gl, a, vn: s * gl + _dg(_top(a), vn, TN), s0, f["gla"], kd, vna)
        vnb = _each(lambda a, b, s: _bot(a) - _dg(_bot(b), s), u, w, s1)
        ob = _each(lambda a, s: _dg(_bot(a), s), qd, s1)
        s2 = _each(lambda s, gl, a, vn: s * gl + _dg(_bot(a), vn, TN), s1, f["glb"], kd, vnb)
        outs = _each(lambda a, b, c, va, vb: _rows(a, b) + _dg(c, _rows(va, vb)), oa, ob, f["aqk"], vna, vnb)
        o_ref[...] = jnp.concatenate(outs, axis=1)
        st_ref[...] = jnp.concatenate(s0 + s1, axis=0)
        s_scr[...] = jnp.concatenate(s2, axis=0)

    blk = pl.BlockSpec((PAIR, HW), lambda i: (i, 0))
    return pl.pallas_call(
        body, name="gdn_fwd", grid=(nb,),
        in_specs=[blk] * 5,
        out_specs=[blk, pl.BlockSpec((2 * NH * DH, DH), lambda i: (i, 0))],
        out_shape=[jax.ShapeDtypeStruct((S, HW), F32), jax.ShapeDtypeStruct((nb * 2 * NH * DH, DH), F32)],
        scratch_shapes=[pltpu.VMEM((NH * DH, DH), F32)],
        compiler_params=_cp(("arbitrary",)),
    )(gq, gk, gv, gf, bf)


def _gdn_bwd(gq, gk, gv, gf, bf, states, do, S):
    nb = S // PAIR

    def body(q_ref, k_ref, v_ref, g_ref, b_ref, st_ref, do_ref, dq_o, dk_o, dv_o, dg_o, db_o, ds_scr):
        @pl.when(pl.program_id(0) == 0)
        def _():
            ds_scr[...] = jnp.zeros_like(ds_scr)

        m = _gdn_masks()
        ones = jnp.ones((PAIR, PAIR), F32)
        heads = lambda ref: [ref[:, _hs(h)] for h in range(NH)]
        q, k, v, beta, do = heads(q_ref), heads(k_ref), heads(v_ref), heads(b_ref), heads(do_ref)
        f = _gdn_block(m, q, k, v, heads(g_ref), beta)
        u, w, aqk, qd, kd, t = f["u"], f["w"], f["aqk"], f["qd"], f["kd"], f["t"]
        s0 = [st_ref[h * DH:(h + 1) * DH, :] for h in range(NH)]
        s1 = [st_ref[(NH + h) * DH:(NH + h + 1) * DH, :] for h in range(NH)]
        ds2 = [ds_scr[h * DH:(h + 1) * DH, :] for h in range(NH)]
        total = lambda a, b: jnp.sum(jnp.sum(a * b, axis=1, keepdims=True), axis=0, keepdims=True)
        vna = _each(lambda a, b, s: _top(a) - _dg(_top(b), s), u, w, s0)
        vnb = _each(lambda a, b, s: _bot(a) - _dg(_bot(b), s), u, w, s1)
        dvn_i = _each(lambda a, b: _dg(a, b, TN), aqk, do)
        dvnb = _each(lambda a, b, s: _bot(a) + _dg(_bot(b), s), dvn_i, kd, ds2)
        dqdb = _each(lambda a, s: _dg(_bot(a), s, NT), do, s1)
        dkdb = _each(lambda a, s: _dg(a, s, NT), vnb, ds2)
        dglb = _each(total, ds2, s1)
        dwb = _each(lambda a, s: -_dg(a, s, NT), dvnb, s1)
        ds1 = _each(lambda s, gl, a, b, c, d: s * gl + _dg(_bot(a), _bot(b), TN) - _dg(_bot(c), d, TN),
                    ds2, f["glb"], qd, do, w, dvnb)
        dvna = _each(lambda a, b, s: _top(a) + _dg(_top(b), s), dvn_i, kd, ds1)
        dqda = _each(lambda a, s: _dg(_top(a), s, NT), do, s0)
        dkda = _each(lambda a, s: _dg(a, s, NT), vna, ds1)
        dgla = _each(total, ds1, s0)
        dwa = _each(lambda a, s: -_dg(a, s, NT), dvna, s0)
        ds0 = _each(lambda s, gl, a, b, c, d: s * gl + _dg(_top(a), _top(b), TN) - _dg(_top(c), d, TN),
                    ds1, f["gla"], qd, do, w, dvna)
        dvn, dqd, dkd, dw = (_each(_rows, a, b) for a, b in ((dvna, dvnb), (dqda, dqdb), (dkda, dkdb), (dwa, dwb)))
        daqk = _each(lambda a, va, vb: jnp.where(m["tril"], _dg(a, _rows(va, vb), NT), 0.0), do, vna, vnb)
        dt = _each(lambda a, b, c, d: _dg(a, b, NT) + _dg(c, d, NT), dvn, f["vb"], dw, f["kbe"])
        dvb = _each(lambda a, b: _dg(a, b, TN), t, dvn)
        dkbe = _each(lambda a, b: _dg(a, b, TN), t, dw)
        dtt = _each(lambda a, b: _dg(a, b, NT), dt, t)
        dl = _each(lambda a, b: -jnp.where(m["strict"], _dg(a, b, TN), 0.0), t, dtt)
        dm = _each(_mul, dl, f["gam"])
        dn = _each(_mul, daqk, f["gam"])
        dkb = _each(lambda a, b, c, d: _dg(a, b) + c * d, dm, k, dkbe, f["eg"])
        dks = _each(lambda a, b, c, d, e, g, h, i: _dg(a, b, TN) + _dg(c, d, TN) + e * g + h * i,
                    dm, f["kb"], dn, q, dkd, f["e2"], beta, dkb)
        dqs = _each(lambda a, b, c, d: _dg(a, b) + c * d, dn, k, dqd, f["eg"])
        gm = _each(lambda a, b, c, d: a * b + c * d, dl, f["lm"], daqk, aqk)
        dkdkd = _each(_mul, dkd, kd)
        dgc = _each(lambda a, b, c, d, e, g: _dxr(a + b * c + d * e - g, ones) - _dxr(a, ones, TN),
                    gm, dqd, qd, dkbe, f["kbe"], dkdkd)
        same_f = m["same"].astype(F32)
        chunk_tot = _each(lambda a: _dxl(same_f, _dxr(a, ones)), dkdkd)
        last = m["last_a"] | m["last_b"]
        dgc = _each(lambda a, b, ga, gla, gb, glb: a + jnp.where(last, b + jnp.where(m["in_a"], ga * gla, gb * glb), 0.0),
                    dgc, chunk_tot, dgla, f["gla"], dglb, f["glb"])
        dbs = _each(lambda a, b, c, d: _dxr(a * b + c * d, ones), dkb, k, dvb, v)
        dvs = _each(_mul, beta, dvb)
        triu_f = m["triu"].astype(F32)
        dgs = _each(lambda a: _dxl(triu_f, a), dgc)
        for ref, parts in ((dq_o, dqs), (dk_o, dks), (dv_o, dvs), (dg_o, dgs), (db_o, dbs)):
            ref[...] = jnp.concatenate(parts, axis=1)
        ds_scr[...] = jnp.concatenate(ds0, axis=0)

    blk = pl.BlockSpec((PAIR, HW), lambda i: (nb - 1 - i, 0))
    o = jax.ShapeDtypeStruct((S, HW), F32)
    return pl.pallas_call(
        body, name="gdn_bwd", grid=(nb,),
        in_specs=[blk] * 5 + [pl.BlockSpec((2 * NH * DH, DH), lambda i: (nb - 1 - i, 0)), blk],
        out_specs=[blk] * 5, out_shape=[o] * 5,
        scratch_shapes=[pltpu.VMEM((NH * DH, DH), F32)],
        compiler_params=_cp(("arbitrary",)),
    )(gq, gk, gv, gf, bf, states, do)


SB_T = 256
SB_GROUP = 4
SB_GROUP_BWD = 4
SB_SINGLES = 1
SB_DEAD = -110.0


def _group_sizes(g):
    sizes = []
    while g >= 1:
        sizes.append(g)
        g //= 2
    return sizes


def _sb_iotas(t):
    return lax.broadcasted_iota(jnp.int32, (t, t), 0), lax.broadcasted_iota(jnp.int32, (t, t), 1)


def _sb_scores(q, k, mask):
    z = _dot(q, k, NT) * DH ** -0.5
    ls = jnp.minimum(z, 0.0) - jnp.log(1.0 + jnp.exp(-jnp.abs(z)))
    lneg = ls - z
    if mask is not None:
        lneg = jnp.where(mask, lneg, 0.0)
    return ls, lneg


def _prefix(x, u):
    xh, xl = _split(x, 2)
    return _dot(xh, u) + _dot(xl, u)


def _sb_fwd(sqn, skn, svb, S):
    t = min(SB_T, S)

    def body(q_ref, k_ref, v_ref, o_ref, t_ref, cnt_ref):
        qb = pl.program_id(1)
        q = q_ref[...]
        r, c = _sb_iotas(t)
        diag = c < r
        u_after = (r > c).astype(BF16)

        def tiles(k0s, run, masks):
            sc = _each(lambda k0, m: _sb_scores(q, k_ref[pl.ds(k0, t), :], m), k0s, masks)
            ls, lneg = [s[0] for s in sc], [s[1] for s in sc]
            sums = _each(lambda x: jnp.sum(x, axis=1, keepdims=True), lneg)
            pre = _each(lambda x: _prefix(x, u_after), lneg)
            runs = [run]
            for s in sums:
                runs.append(runs[-1] + s)
            att = _each(lambda a, b, rn: jnp.exp(a + (rn + b)), ls, pre, runs[:-1])
            att = _each(lambda a, m: a if m is None else jnp.where(m, a, 0.0), att, masks)
            parts = _each(lambda a, k0: _dot(a.astype(BF16), v_ref[pl.ds(k0, t), :]), att, k0s)
            return sum(parts[1:], parts[0]), runs[-1]

        left = jnp.full((t, t), qb > 0)
        acc, run = tiles([pl.multiple_of(qb * t, t), pl.multiple_of(jnp.maximum(qb - 1, 0) * t, t)],
                         jnp.zeros((t, 1), F32), [diag, left])

        def alive(run):
            return jnp.max(run) >= SB_DEAD

        carry, done = (0, acc, run, alive(run)), jnp.minimum(qb, 1)
        for size, limit in [(1, SB_SINGLES)] + [(s, None) for s in _group_sizes(SB_GROUP)]:

            def more(c, size=size, done=done, limit=limit):
                i, _, _, go = c
                fits = done + (i + 1) * size <= qb
                return (fits if limit is None else fits & (i < limit)) & go

            def group(c, size=size, done=done):
                i, acc, run, _ = c
                first = qb - 1 - done - size * i
                part, run = tiles([pl.multiple_of((first - j) * t, t) for j in range(size)], run, [None] * size)
                return i + 1, acc + part, run, alive(run)

            n, acc, run, go = lax.while_loop(more, group, (0,) + carry[1:])
            carry, done = (0, acc, run, go), done + n * size
        o_ref[...] = acc.astype(BF16)
        t_ref[...] = jnp.broadcast_to(run, (t, DH))
        cnt_ref[pl.program_id(0), qb] = done

    qspec = pl.BlockSpec((t, DH), lambda h, i: (i, h))
    kspec = pl.BlockSpec((S, DH), lambda h, i: (0, h))
    return pl.pallas_call(
        body, name="sb_fwd", grid=(NH, S // t),
        in_specs=[qspec, kspec, kspec],
        out_specs=[qspec, qspec, pl.BlockSpec(memory_space=pltpu.SMEM)],
        out_shape=[jax.ShapeDtypeStruct((S, HW), BF16), jax.ShapeDtypeStruct((S, HW), F32),
                   jax.ShapeDtypeStruct((NH, S // t), jnp.int32)],
        compiler_params=_cp(("arbitrary", "arbitrary")),
    )(sqn, skn, svb)


def _sb_bwd(sqn, skn, svb, do, tot, walked, S):
    t = min(SB_T, S)

    def body(cnt_ref, q_ref, k_ref, v_ref, do_ref, t_ref, dq_o, dk_o, dv_o, dv_acc):
        qb = pl.program_id(1)

        @pl.when(qb == 0)
        def _():
            dk_o[...] = jnp.zeros_like(dk_o)
            dv_acc[...] = jnp.zeros_like(dv_acc)

        q = q_ref[...]
        do = do_ref[...].astype(BF16)
        tot_l = jnp.concatenate([t_ref[...]] * (t // DH), axis=1)
        r, c = _sb_iotas(t)
        diag = c < r
        u_upto = (r <= c).astype(BF16)
        u_before = (r < c).astype(BF16)

        def tiles(k0s, run_l, run_e, masks):
            rowsum = lambda x: jnp.sum(x, axis=1, keepdims=True)
            masked = lambda xs: _each(lambda a, m: a if m is None else jnp.where(m, a, 0.0), xs, masks)
            ks = [k_ref[pl.ds(k0, t), :] for k0 in k0s]
            vs = [v_ref[pl.ds(k0, t), :] for k0 in k0s]
            sc = _each(lambda k, m: _sb_scores(q, k, m), ks, masks)
            ls, lneg = [s[0] for s in sc], [s[1] for s in sc]
            sums_l = _each(rowsum, lneg)
            pre_l = _each(lambda x: _prefix(x, u_upto), lneg)
            runs_l = [run_l]
            for s in sums_l:
                runs_l.append(runs_l[-1] + s)
            att = masked(_each(lambda a, b, rn: jnp.exp(a + (tot_l - (rn + b))), ls, pre_l, runs_l[:-1]))
            e = _each(lambda v, a: _dot(do, v, NT) * a, vs, att)
            sums_e = _each(rowsum, e)
            pre_e = _each(lambda x: _prefix(x, u_before), e)
            runs_e = [run_e]
            for s in sums_e:
                runs_e.append(runs_e[-1] + s)
            sg = _each(jnp.exp, ls)
            dz = masked(_each(lambda a, b, rn, s: a * (1.0 - s) - (rn + b) * s, e, pre_e, runs_e[:-1], sg))
            dz = _each(lambda a: (a * DH ** -0.5).astype(BF16), dz)
            dvs = _each(lambda a: _dot(a.astype(BF16), do, TN), att)
            dks = _each(lambda a: _dot(a, q, TN), dz)
            dqs = _each(_dot, dz, ks)
            for k0, dv, dk in zip(k0s, dvs, dks):
                dv_acc[pl.ds(k0, t), :] += dv
                dk_o[pl.ds(k0, t), :] += dk
            return sum(dqs[1:], dqs[0]), runs_l[-1], runs_e[-1]

        walked = cnt_ref[pl.program_id(0), qb]
        early = jnp.maximum(walked - 1, 0)
        z1 = jnp.zeros((t, 1), F32)
        carry, done = (jnp.zeros((t, DH), F32), z1, z1), 0
        for size in _group_sizes(SB_GROUP_BWD):
            n = (early - done) // size

            def group(i, carry, size=size, done=done):
                dq, run_l, run_e = carry
                first = qb - walked + done + size * i
                part, run_l, run_e = tiles([pl.multiple_of((first + j) * t, t) for j in range(size)], run_l, run_e,
                                           [None] * size)
                return dq + part, run_l, run_e

            carry = lax.fori_loop(0, n, group, carry)
            done = done + n * size
        dq, run_l, run_e = carry
        left = jnp.full((t, t), qb > 0)
        part, _, _ = tiles([pl.multiple_of(jnp.maximum(qb - 1, 0) * t, t), pl.multiple_of(qb * t, t)], run_l, run_e,
                           [left, diag])
        dq_o[...] = dq + part

        @pl.when(qb == S // t - 1)
        def _():
            dv_o[...] = dv_acc[...].astype(BF16)

    qspec = pl.BlockSpec((t, DH), lambda h, i, cnt: (i, h))
    kspec = pl.BlockSpec((S, DH), lambda h, i, cnt: (0, h))
    o = jax.ShapeDtypeStruct((S, HW), F32)
    return pl.pallas_call(
        body, name="sb_bwd",
        grid_spec=pltpu.PrefetchScalarGridSpec(
            num_scalar_prefetch=1, grid=(NH, S // t),
            in_specs=[qspec, kspec, kspec, qspec, qspec], out_specs=[qspec, kspec, kspec],
            scratch_shapes=[pltpu.VMEM((S, DH), F32)]),
        out_shape=[o, o, jax.ShapeDtypeStruct((S, HW), BF16)],
        compiler_params=_cp(("parallel", "arbitrary")),
    )(walked, sqn, skn, svb, do, tot)


def _mem_probs(qn, kn):
    s = _dot(qn, kn.astype(BF16), NT) * DH ** -0.5
    p = jnp.exp(s - jnp.max(s, axis=-1, keepdims=True))
    return p / jnp.sum(p, axis=-1, keepdims=True)


def _mem_fwd(qmn, kv, gmk, S):
    ts = _row_tile(S)

    def body(q_ref, kv_ref, gk_ref, o_ref):
        for h in range(NH):
            kn, _ = _rms(kv_ref[:, _hs(h)], gk_ref[...])
            p = _mem_probs(q_ref[:, _hs(h)], kn)
            o_ref[:, _hs(h)] = _dbf(p, kv_ref[:, HW + h * DH:HW + (h + 1) * DH]).astype(BF16)

    return pl.pallas_call(
        body, name="mem_fwd", grid=(S // ts,),
        in_specs=[pl.BlockSpec((ts, HW), lambda i: (i, 0)), pl.BlockSpec((NMEM, 2 * HW), lambda i: (0, 0)),
                  pl.BlockSpec((1, DH), lambda i: (0, 0))],
        out_specs=pl.BlockSpec((ts, HW), lambda i: (i, 0)),
        out_shape=jax.ShapeDtypeStruct((S, HW), BF16),
        compiler_params=_cp(("parallel",)),
    )(qmn, kv, gmk)


def _mem_bwd(proj, qmn, kv, gmq, gmk, do, S):
    ts = _row_tile(S)
    n = S // ts

    def body(mq_ref, q_ref, kv_ref, gq_ref, gk_ref, do_ref, dmq_o, dkv_o, dgq_o, dgk_o, dkn_scr):
        i = pl.program_id(0)

        @pl.when(i == 0)
        def _():
            dkv_o[...] = jnp.zeros_like(dkv_o)
            dgq_o[...] = jnp.zeros_like(dgq_o)
            dkn_scr[...] = jnp.zeros_like(dkn_scr)

        dgq = jnp.zeros((1, DH), F32)
        for h in range(NH):
            km = kv_ref[:, _hs(h)]
            vm = kv_ref[:, HW + h * DH:HW + (h + 1) * DH].astype(BF16)
            kn, _ = _rms(km, gk_ref[...])
            qn = q_ref[:, _hs(h)]
            p = _mem_probs(qn, kn)
            dob = do_ref[:, _hs(h)].astype(BF16)
            dkv_o[:, HW + h * DH:HW + (h + 1) * DH] += _dot(p.astype(BF16), dob, TN)
            dp = _dot(dob, vm, NT)
            dsc = (p * (dp - jnp.sum(dp * p, axis=-1, keepdims=True)) * DH ** -0.5).astype(BF16)
            dkn_scr[:, _hs(h)] += _dot(dsc, qn, TN)
            x = mq_ref[:, _hs(h)]
            _, r = _rms(x, gq_ref[...])
            dx, dg = _rms_bwd(_dot(dsc, kn.astype(BF16)), x, gq_ref[...], r)
            dmq_o[:, _hs(h)] = dx.astype(BF16)
            dgq = dgq + dg
        dgq_o[...] += dgq

        @pl.when(i == n - 1)
        def _():
            dgk = jnp.zeros((1, DH), F32)
            for h in range(NH):
                km = kv_ref[:, _hs(h)]
                _, r = _rms(km, gk_ref[...])
                dx, dg = _rms_bwd(dkn_scr[:, _hs(h)], km, gk_ref[...], r)
                dkv_o[:, _hs(h)] = dx
                dgk = dgk + dg
            dgk_o[...] = dgk

    full = lambda r, c: pl.BlockSpec((r, c), lambda i: (0, 0))
    t512 = pl.BlockSpec((ts, HW), lambda i: (i, 0))
    return pl.pallas_call(
        body, name="mem_bwd", grid=(n,),
        in_specs=[pl.BlockSpec((ts, HW), lambda i: (i, CB_MQ)), t512, full(NMEM, 2 * HW), full(1, DH), full(1, DH),
                  t512],
        out_specs=[t512, full(NMEM, 2 * HW), full(1, DH), full(1, DH)],
        out_shape=[jax.ShapeDtypeStruct((S, HW), BF16), jax.ShapeDtypeStruct((NMEM, 2 * HW), F32),
                   jax.ShapeDtypeStruct((1, DH), F32), jax.ShapeDtypeStruct((1, DH), F32)],
        scratch_shapes=[pltpu.VMEM((NMEM, HW), F32)],
        compiler_params=_cp(("arbitrary",)),
    )(proj, qmn, kv, gmq, gmk, do)


def _gated_gdn(o, z, g):
    sg = _sigmoid(z)
    outs, rs = [], []
    for h in range(NH):
        y, r = _rms(o[:, _hs(h)], g)
        outs.append(y * (z[:, _hs(h)] * sg[:, _hs(h)]))
        rs.append(r)
    return jnp.concatenate(outs, axis=1), rs, sg


def _merge_fwd(x, proj, ogdn, osb, omem, ggdn, wbg, wbs, wbm, wo, S):
    ts = _narrow_tile(S)

    def body(x_ref, z_ref, g0_ref, g1_ref, g2_ref, og_ref, os_ref, om_ref, gg_ref, wbg_ref, wbs_ref, wbm_ref,
             wo_ref, x1_o, mix_o):
        on, _, _ = _gated_gdn(og_ref[...], z_ref[...], gg_ref[...])
        mix = (_sigmoid(g0_ref[...]) * _dbf(on, wbg_ref[...]) + _sigmoid(g1_ref[...]) * _dbf(os_ref[...], wbs_ref[...])
               + _sigmoid(g2_ref[...]) * _dbf(om_ref[...], wbm_ref[...]))
        mix_o[...] = mix.astype(BF16)
        x1_o[...] = x_ref[...] + _dbf(mix, wo_ref[...])

    t512 = pl.BlockSpec((ts, HW), lambda i: (i, 0))
    t1k = pl.BlockSpec((ts, D), lambda i: (i, 0))
    gate = lambda j: pl.BlockSpec((ts, D), lambda i: (i, 4 + j))
    full = lambda r, c: pl.BlockSpec((r, c), lambda i: (0, 0))
    return pl.pallas_call(
        body, name="merge_fwd", grid=(S // ts,),
        in_specs=[t1k, pl.BlockSpec((ts, HW), lambda i: (i, CB_Z)), gate(0), gate(1), gate(2), t512, t512, t512,
                  full(1, DH), full(HW, D), full(HW, D), full(HW, D), full(D, D)],
        out_specs=[t1k, t1k],
        out_shape=[jax.ShapeDtypeStruct((S, D), F32), jax.ShapeDtypeStruct((S, D), BF16)],
        compiler_params=_cp(("parallel",)),
    )(x, proj, proj, proj, proj, ogdn, osb, omem, ggdn, wbg, wbs, wbm, wo)


def _merge_bwd(dmix, proj, ogdn, osb, omem, ggdn, wbg, wbs, wbm, S):
    ts = _narrow_tile(S)

    def body(dm_ref, z_ref, g0_ref, g1_ref, g2_ref, og_ref, os_ref, om_ref, gg_ref, wbg_ref, wbs_ref, wbm_ref,
             dgl0_o, dgl1_o, dgl2_o, dog_o, dz_o, dos_o, dom_o, dwbg_o, dwbs_o, dwbm_o, dgg_o):
        @pl.when(pl.program_id(0) == 0)
        def _():
            for ref in (dwbg_o, dwbs_o, dwbm_o, dgg_o):
                ref[...] = jnp.zeros_like(ref)

        dm = dm_ref[...]
        og = og_ref[...]
        z = z_ref[...]
        on, rs, sg = _gated_gdn(og, z, gg_ref[...])
        branch = ((on, g0_ref, wbg_ref, dgl0_o, dwbg_o), (os_ref[...], g1_ref, wbs_ref, dgl1_o, dwbs_o),
                  (om_ref[...], g2_ref, wbm_ref, dgl2_o, dwbm_o))
        dos = []
        for o, g_ref, w_ref, dgl_o, dw_o in branch:
            ob = o.astype(BF16)
            gate = _sigmoid(g_ref[...])
            dgl_o[...] = (dm * _dot(ob, w_ref[...]) * gate * (1.0 - gate)).astype(BF16)
            dy = (dm * gate).astype(BF16)
            dw_o[...] += _dot(ob, dy, TN)
            dos.append(_dot(dy, w_ref[...], NT))
        dos_o[...] = dos[1].astype(BF16)
        dom_o[...] = dos[2].astype(BF16)
        don = dos[0]
        dgg = jnp.zeros((1, DH), F32)
        for h in range(NH):
            oh, zh, sh = og[:, _hs(h)], z[:, _hs(h)], sg[:, _hs(h)]
            y = oh * rs[h] * gg_ref[...]
            dz_o[:, _hs(h)] = (don[:, _hs(h)] * y * (sh * (1.0 + zh * (1.0 - sh)))).astype(BF16)
            dx, dg = _rms_bwd(don[:, _hs(h)] * (zh * sh), oh, gg_ref[...], rs[h])
            dog_o[:, _hs(h)] = dx
            dgg = dgg + dg
        dgg_o[...] += dgg

    t512 = pl.BlockSpec((ts, HW), lambda i: (i, 0))
    t1k = pl.BlockSpec((ts, D), lambda i: (i, 0))
    gate = lambda j: pl.BlockSpec((ts, D), lambda i: (i, 4 + j))
    full = lambda r, c: pl.BlockSpec((r, c), lambda i: (0, 0))
    s1k = jax.ShapeDtypeStruct((S, D), BF16)
    s512 = jax.ShapeDtypeStruct((S, HW), BF16)
    wsh = jax.ShapeDtypeStruct((HW, D), F32)
    return pl.pallas_call(
        body, name="merge_bwd", grid=(S // ts,),
        in_specs=[t1k, pl.BlockSpec((ts, HW), lambda i: (i, CB_Z)), gate(0), gate(1), gate(2), t512, t512, t512,
                  full(1, DH), full(HW, D), full(HW, D), full(HW, D)],
        out_specs=[t1k, t1k, t1k, t512, t512, t512, t512, full(HW, D), full(HW, D), full(HW, D), full(1, DH)],
        out_shape=[s1k, s1k, s1k, jax.ShapeDtypeStruct((S, HW), F32), s512, s512, s512, wsh, wsh, wsh,
                   jax.ShapeDtypeStruct((1, DH), F32)],
        compiler_params=_cp(("arbitrary",)),
    )(dmix, proj, proj, proj, proj, ogdn, osb, omem, ggdn, wbg, wbs, wbm)


def _norm_cast(name, x, g):
    rows = x.shape[0]
    ts = min(_row_tile(rows), rows)

    def body(x_ref, g_ref, o_ref):
        o_ref[...] = _rms(x_ref[...], g_ref[...])[0].astype(BF16)

    t1k = pl.BlockSpec((ts, D), lambda i: (i, 0))
    return pl.pallas_call(
        body, name=name, grid=(rows // ts,), in_specs=[t1k, pl.BlockSpec((1, D), lambda i: (0, 0))], out_specs=t1k,
        out_shape=jax.ShapeDtypeStruct((rows, D), BF16), compiler_params=_cp(("parallel",)),
    )(x, g)


def _norm_bwd(name, dh, x, g, res):
    rows = x.shape[0]
    ts = min(_row_tile(rows), rows)

    def body(*refs):
        dh_ref, x_ref, g_ref = refs[:3]
        dx_o, dg_o = refs[-2:]

        @pl.when(pl.program_id(0) == 0)
        def _():
            dg_o[...] = jnp.zeros_like(dg_o)

        xv = x_ref[...]
        _, r = _rms(xv, g_ref[...])
        dx, dg = _rms_bwd(dh_ref[...], xv, g_ref[...], r)
        dx_o[...] = dx if res is None else dx + refs[3][...]
        dg_o[...] += dg

    t1k = pl.BlockSpec((ts, D), lambda i: (i, 0))
    gsp = pl.BlockSpec((1, D), lambda i: (0, 0))
    ops = [dh, x, g] + ([] if res is None else [res])
    return pl.pallas_call(
        body, name=name, grid=(rows // ts,), in_specs=[t1k, t1k, gsp] + ([] if res is None else [t1k]),
        out_specs=[t1k, gsp],
        out_shape=[jax.ShapeDtypeStruct((rows, D), F32), jax.ShapeDtypeStruct((1, D), F32)],
        compiler_params=_cp(("arbitrary",)),
    )(*ops)


def _adamw(name, gall, w, m, v):
    rows = w.shape[0]
    nsrc = gall.shape[0]
    tr = min(SLAB_TILE, rows)
    assert rows % tr == 0

    def body(g_ref, w_ref, m_ref, v_ref, g_o, d_o, m_o, v_o):
        g = g_ref[0].astype(F32)
        for j in range(1, nsrc):
            g = g + g_ref[j].astype(F32)
        m_new = ADAM_B1 * m_ref[...] + (1.0 - ADAM_B1) * g
        v_new = ADAM_B2 * v_ref[...] + (1.0 - ADAM_B2) * jnp.square(g)
        m_hat = m_new / (1.0 - ADAM_B1 ** ADAM_STEP)
        v_hat = v_new / (1.0 - ADAM_B2 ** ADAM_STEP)
        g_o[...] = g
        d_o[...] = -ADAM_LR * (m_hat / (jnp.sqrt(v_hat) + ADAM_EPS) + ADAM_WD * w_ref[...])
        m_o[...] = m_new
        v_o[...] = v_new

    t = pl.BlockSpec((tr, LANES), lambda i: (i, 0))
    o = jax.ShapeDtypeStruct((rows, LANES), F32)
    return pl.pallas_call(
        body, name=name, grid=(rows // tr,),
        in_specs=[pl.BlockSpec((nsrc, tr, LANES), lambda i: (0, i, 0)), t, t, t],
        out_specs=[t, t, t, t], out_shape=[o, o, o, o],
        compiler_params=_cp(("parallel",)),
    )(gall, w, m, v)


def _pair_sum(name, mine, theirs):
    rows = mine.shape[1]
    tr = min(SLAB_TILE, rows)
    assert rows % tr == 0
    core = lax.axis_index("c").astype(jnp.int32).reshape(1)

    def body(c_ref, a_ref, b_ref, o_ref):
        o_ref[...] = (a_ref[...].astype(F32) + b_ref[...].astype(F32)).astype(o_ref.dtype)

    blk = pl.BlockSpec((1, tr, LANES), lambda j, i, c_ref: (j, i, 0))
    return pl.pallas_call(
        body, name=name,
        grid_spec=pltpu.PrefetchScalarGridSpec(
            num_scalar_prefetch=1, grid=(NDEV // 2, rows // tr),
            in_specs=[pl.BlockSpec((1, tr, LANES), lambda j, i, c_ref: (2 * j + c_ref[0], i, 0)), blk],
            out_specs=blk),
        out_shape=jax.ShapeDtypeStruct((NDEV // 2, rows, LANES), mine.dtype),
        compiler_params=_cp(("parallel", "parallel")),
    )(core, mine, theirs)


HBM_SPEC = pl.BlockSpec(memory_space=pltpu.HBM)


def _remote(src, dst, send_sems, recv_sems, k, to):
    return pltpu.make_async_remote_copy(src_ref=src, dst_ref=dst, send_sem=send_sems.at[k], recv_sem=recv_sems.at[k],
                                        device_id=to, device_id_type=pl.DeviceIdType.MESH)


def _gather_steps(x_ref, o_ref, send_sems, recv_sems, local_sem):
    ix, iy, ic = lax.axis_index("x"), lax.axis_index("y"), lax.axis_index("c")
    me, sibling = (ix, iy, ic), (ix, iy, 1 - ic)
    chips = [(1 - ix, iy), (ix, 1 - iy), (1 - ix, 1 - iy)]

    def slab(px, py, pc):
        return o_ref.at[4 * px + 2 * py + pc]

    def copy(k, block, to, src=None):
        return _remote(slab(*block) if src is None else src, slab(*block), send_sems, recv_sems, k, to)

    def mine():
        return pltpu.make_async_copy(x_ref, slab(*me), local_sem)

    def first():
        return [copy(0, me, sibling, src=x_ref)] + [copy(1 + j, me, (*chip, ic), src=x_ref)
                                                    for j, chip in enumerate(chips)]

    def passed():
        return [copy(4 + j, (*chip, ic), sibling) for j, chip in enumerate(chips)]

    def start():
        mine().start()
        for cp in first():
            cp.start()

    def forward():
        for j, (chip, cp) in enumerate(zip(chips, passed())):
            copy(1 + j, (*chip, ic), me).wait_recv()
            cp.start()

    def finish():
        copy(0, sibling, me).wait_recv()
        for j, chip in enumerate(chips):
            copy(4 + j, (*chip, 1 - ic), me).wait_recv()
        for cp in first() + passed():
            cp.wait_send()
        mine().wait()

    return start, forward, finish


GATHER_SEMS = [pltpu.SemaphoreType.DMA((NDEV - 1,)), pltpu.SemaphoreType.DMA((NDEV - 1,)), pltpu.SemaphoreType.DMA]


def _gather(name, x):
    rows, cols = x.shape

    def body(x_ref, o_ref, send_sems, recv_sems, local_sem):
        for step in _gather_steps(x_ref, o_ref, send_sems, recv_sems, local_sem):
            step()

    return pl.pallas_call(
        body, name=name, in_specs=[HBM_SPEC], out_specs=HBM_SPEC,
        out_shape=jax.ShapeDtypeStruct((NDEV, rows, cols), x.dtype), scratch_shapes=list(GATHER_SEMS),
    )(x)


def _sibling_exchange(name, x):
    rows, cols = x.shape[-2:]
    nchip = NDEV // 2

    def body(x_ref, o_ref, send_sems, recv_sems):
        ix, iy, ic = lax.axis_index("x"), lax.axis_index("y"), lax.axis_index("c")
        copies = [_remote(x_ref.at[2 * j + (1 - ic)], o_ref.at[j], send_sems, recv_sems, j, (ix, iy, 1 - ic))
                  for j in range(nchip)]
        for cp in copies:
            cp.start()
        for cp in copies:
            cp.wait()

    return pl.pallas_call(
        body, name=name, in_specs=[HBM_SPEC], out_specs=HBM_SPEC,
        out_shape=jax.ShapeDtypeStruct((nchip, rows, cols), x.dtype),
        scratch_shapes=[pltpu.SemaphoreType.DMA((nchip,)), pltpu.SemaphoreType.DMA((nchip,))],
    )(x)


def _chip_steps(x_ref, o_ref, send_sems, recv_sems, local_sem):
    ix, iy, ic = lax.axis_index("x"), lax.axis_index("y"), lax.axis_index("c")
    my_chip = 2 * ix + iy

    def own():
        return pltpu.make_async_copy(x_ref.at[my_chip], o_ref.at[my_chip], local_sem)

    def copies():
        out = []
        for k in range(1, NDEV // 2):
            px, py = ix ^ (k >> 1), iy ^ (k & 1)
            out.append(_remote(x_ref.at[2 * px + py], o_ref.at[my_chip], send_sems, recv_sems, k - 1, (px, py, ic)))
        return out

    def start():
        own().start()
        for cp in copies():
            cp.start()

    def finish():
        for cp in copies():
            cp.wait()
        own().wait()

    return start, (lambda: None), finish


CHIP_SEMS = [pltpu.SemaphoreType.DMA((NDEV // 2 - 1,)), pltpu.SemaphoreType.DMA((NDEV // 2 - 1,)),
             pltpu.SemaphoreType.DMA]


def _chip_exchange(name, x):
    rows, cols = x.shape[-2:]

    def body(x_ref, o_ref, send_sems, recv_sems, local_sem):
        for step in _chip_steps(x_ref, o_ref, send_sems, recv_sems, local_sem):
            step()

    return pl.pallas_call(
        body, name=name, in_specs=[HBM_SPEC], out_specs=HBM_SPEC,
        out_shape=jax.ShapeDtypeStruct((NDEV // 2, rows, cols), x.dtype), scratch_shapes=list(CHIP_SEMS),
    )(x)


COL_SHARDED = {"w_in": (D, D_IN), "w_br_gdn": (HW, D), "w_br_sb": (HW, D), "w_br_mem": (HW, D), "w_up": (D, DFF),
               "conv_w": (4, 3 * HW)}
ROW_SHARDED = {"w_mem_kv": (D, 2 * HW), "w_o": (D, D), "w_down": (DFF, D)}


def _to_slab(p):
    return p.reshape(p.shape[:-2] + (-1, LANES))


def _from_slab(flat, r, c):
    return flat.reshape(flat.shape[:-2] + (r, c))


def _shard_dims(name):
    if name in COL_SHARDED:
        r, c = COL_SHARDED[name]
        return r, c // NDEV
    r, c = ROW_SHARDED[name]
    return r // NDEV, c


def _pack_rows(parts, total):
    flat = jnp.concatenate(parts, axis=-2)
    return jnp.pad(flat, [(0, 0)] * (flat.ndim - 2) + [(0, total - flat.shape[-2]), (0, 0)])


def _pack_shards(vals, names, total):
    return _pack_rows([_to_slab(vals[n][0]) for n in names], total)


def _pack_full_grads(grads, names, total):
    parts = []
    for name in names:
        g = grads[name]
        r, c = _shard_dims(name)
        if name in COL_SHARDED:
            g = g.reshape(r, NDEV, c).transpose(1, 0, 2)
        else:
            g = g.reshape(NDEV, r, c)
        parts.append(_to_slab(g))
    return _pack_rows(parts, total)


def _unpack_gathered(slabs, names):
    out, pos = {}, 0
    for name in names:
        rows = SLAB_ROWS[name]
        r, c = _shard_dims(name)
        g = _from_slab(slabs[:, pos:pos + rows], r, c)
        pos += rows
        if name in COL_SHARDED:
            out[name] = g.transpose(1, 0, 2).reshape(r, NDEV * c)
        else:
            out[name] = g.reshape(NDEV * r, c)
    return out


def _unpack_shard(flat, names, shapes):
    out, pos = {}, 0
    for name in names:
        rows = SLAB_ROWS[name]
        r, c = _shard_dims(name)
        out[name] = _from_slab(flat[pos:pos + rows], r, c).reshape(shapes[name])
        pos += rows
    return out


def _pack_vec(vals):
    row = jnp.concatenate([vals[n] for n in VEC], axis=1)
    return jnp.pad(row, ((0, 0), (0, VEC_WIDTH - row.shape[1])))


def _adamw_vec(gall, w, m, v):
    aligned = [(off, n) for off, n in zip(VEC_OFFSETS, VEC_SIZES) if n % LANES == 0]

    def body(g_ref, w_ref, m_ref, v_ref, *outs):
        g = g_ref[0]
        for j in range(1, NDEV):
            g = g + g_ref[j]
        m_new = ADAM_B1 * m_ref[...] + (1.0 - ADAM_B1) * g
        v_new = ADAM_B2 * v_ref[...] + (1.0 - ADAM_B2) * jnp.square(g)
        m_hat = m_new / (1.0 - ADAM_B1 ** ADAM_STEP)
        v_hat = v_new / (1.0 - ADAM_B2 ** ADAM_STEP)
        delta = -ADAM_LR * (m_hat / (jnp.sqrt(v_hat) + ADAM_EPS) + ADAM_WD * w_ref[...])
        for r, val in enumerate((g, delta, m_new, v_new)):
            outs[r][...] = val
            for i, (off, n) in enumerate(aligned):
                outs[4 + r * len(aligned) + i][...] = val[:, off:off + n]

    full = lambda *shape: pl.BlockSpec(shape, lambda: (0,) * len(shape))
    row = jax.ShapeDtypeStruct((1, VEC_WIDTH), F32)
    out_shape = [row] * 4 + [jax.ShapeDtypeStruct((1, n), F32) for _ in range(4) for _, n in aligned]
    out_specs = [full(1, VEC_WIDTH)] * 4 + [full(1, n) for _ in range(4) for _, n in aligned]
    return pl.pallas_call(
        body, name="adamw_replicated",
        in_specs=[full(NDEV, 1, VEC_WIDTH), full(1, VEC_WIDTH), full(1, VEC_WIDTH), full(1, VEC_WIDTH)],
        out_specs=out_specs, out_shape=out_shape,
    )(gall, w, m, v)


def _unpack_vec(outs, r):
    aligned = [name for name, n in zip(VEC, VEC_SIZES) if n % LANES == 0]
    vals = {name: outs[4 + r * len(aligned) + i] for i, name in enumerate(aligned)}
    for name, off, n in zip(VEC, VEC_OFFSETS, VEC_SIZES):
        if name not in vals:
            vals[name] = outs[r][:, off:off + n]
    return vals


def _pad_w_in(w):
    return jnp.concatenate([w[:, :2048], w[:, 2056:], w[:, 2048:2056], jnp.zeros((D, D_INP - D_IN), w.dtype)], axis=1)


def _unpad_w_in(w):
    return jnp.concatenate([w[:, :2048], w[:, 7168:7176], w[:, 2048:7168]], axis=1)


def _per_head(v):
    return jnp.repeat(v.reshape(NH), DH).reshape(1, HW)


def _local_step(x, mem, target, w, sm, rest_shards):
    S = x.shape[0]
    ts = _row_tile(S)
    alog_f, dtb_f = _per_head(sm["a_log"]), _per_head(sm["dt_bias"])
    w = dict(w)

    h1 = _norm_cast("norm1", x, sm["norm1_g"])
    proj, rest = _mm("in_proj", h1, w["w_in"], "nn", ts, 1536, D, n_outer=True, comm=("gather", rest_shards))
    w.update(_unpack_gathered(rest[:, :sum(SLAB_ROWS[n] for n in REST)], REST))
    gq, gk, gv, gf, bf, sqn, skn, svb, qmn = _pre_fwd(proj, w["conv_w"], alog_f, dtb_f, sm["sb_q_norm_g"],
                                                      sm["sb_k_norm_g"], sm["mem_q_norm_g"], S)
    ogdn, states = _gdn_fwd(gq, gk, gv, gf, bf, S)
    osb, sb_tot, sb_walked = _sb_fwd(sqn, skn, svb, S)
    kv = _mm("mem_kv", mem, w["w_mem_kv"], "nn", NMEM, D, D, pro="rms", pro_g=sm["mem_norm_g"])
    omem = _mem_fwd(qmn, kv, sm["mem_k_norm_g"], S)
    x1, mix = _merge_fwd(x, proj, ogdn, osb, omem, sm["gdn_norm_g"], w["w_br_gdn"], w["w_br_sb"], w["w_br_mem"],
                         w["w_o"], S)
    h2 = _norm_cast("norm2", x1, sm["norm2_g"])
    up = _mm("mlp_up", h2, w["w_up"], "nn", ts, 2048, D, n_outer=True)
    dy, loss = _mm("mlp_down", up, w["w_down"], "nn", ts, D, 1024, pro="relu2", epi="loss", epi_x=(x1, target))

    g = {}
    dup = _mm("d_up", dy, w["w_down"], "nt", ts, 1024, D, epi="drelu2", epi_x=up, out_dtype=BF16)
    g["w_down"] = _mm("dw_down", up, dy, "tn", 1024, D, 512, pro="relu2")
    g["w_up"] = _mm("dw_up", h2, dup, "tn", D, 1024, 512)
    dx1, g["norm2_g"] = _mm("d_h2", dup, w["w_up"], "nt", ts, D, 1024, epi="rms_bwd", epi_x=(x1, sm["norm2_g"], dy))

    dmix = _mm("d_mix", dx1, w["w_o"], "nt", ts, D, D)
    g["w_o"] = _mm("dw_o", mix, dx1, "tn", D, D, 512)
    (dgl0, dgl1, dgl2, dogdn, dz, dosb, domem, g["w_br_gdn"], g["w_br_sb"], g["w_br_mem"],
     g["gdn_norm_g"]) = _merge_bwd(dmix, proj, ogdn, osb, omem, sm["gdn_norm_g"], w["w_br_gdn"], w["w_br_sb"],
                                   w["w_br_mem"], S)
    dmq, dkv, g["mem_q_norm_g"], g["mem_k_norm_g"] = _mem_bwd(proj, qmn, kv, sm["mem_q_norm_g"], sm["mem_k_norm_g"],
                                                             domem, S)
    g["w_mem_kv"] = _mm("dw_mem_kv", mem, dkv, "tn", D, D, NMEM, pro="rms", pro_g=sm["mem_norm_g"])
    dmn = _mm("d_mem_n", dkv, w["w_mem_kv"], "nt", NMEM, D, D)
    _, g["mem_norm_g"] = _norm_bwd("mem_norm_bwd", dmn, mem, sm["mem_norm_g"], None)
    dsqn, dskn, dsv = _sb_bwd(sqn, skn, svb, dosb, sb_tot, sb_walked, S)
    dgq, dgk, dgv, dgf, dbf = _gdn_bwd(gq, gk, gv, gf, bf, states, dogdn, S)
    dc, dab, dsq, dsk, g["conv_w"], dal_f, ddt_f, g["sb_q_norm_g"], g["sb_k_norm_g"] = _pre_bwd(
        proj, w["conv_w"], alog_f, dtb_f, sm["sb_q_norm_g"], sm["sb_k_norm_g"], dgq, dgk, dgv, dgf, dbf, dsqn, dskn, S)
    g["a_log"] = dal_f.reshape(NH, DH)[:, 0].reshape(1, NH)
    g["dt_bias"] = ddt_f.reshape(NH, DH)[:, 0].reshape(1, NH)
    dqkv = _conv_bwd(dc, w["conv_w"], S)

    dproj = jnp.concatenate([dqkv, dz, dsq, dsk, dsv, dmq, dgl0, dgl1, dgl2, dab], axis=1)
    rest_mine = _pack_full_grads(g, REST, R_REST).astype(BF16)
    rest_pair = _pair_sum("pair_sum_rest", rest_mine, _sibling_exchange("scatter_sibling_rest", rest_mine))
    g["w_in"], rest_all = _mm("dw_in", h1, dproj, "tn", D, 1536, 512, comm=("chips", rest_pair))
    dx, g["norm1_g"] = _mm("d_h", dproj, w["w_in"], "nt", ts, D, 1536, epi="rms_bwd", epi_x=(x, sm["norm1_g"], dx1))
    return loss[0, 0], dx, g, rest_all


def kernel(x, mem, norm1_g, w_in, conv_w, a_log, dt_bias, gdn_norm_g, sb_q_norm_g, sb_k_norm_g, mem_norm_g, w_mem_kv, mem_q_norm_g, mem_k_norm_g, w_br_gdn, w_br_sb, w_br_mem, w_o, norm2_g, w_up, w_down, loss_target, m_norm1_g, m_w_in, m_conv_w, m_a_log, m_dt_bias, m_gdn_norm_g, m_sb_q_norm_g, m_sb_k_norm_g, m_mem_norm_g, m_w_mem_kv, m_mem_q_norm_g, m_mem_k_norm_g, m_w_br_gdn, m_w_br_sb, m_w_br_mem, m_w_o, m_norm2_g, m_w_up, m_w_down, v_norm1_g, v_w_in, v_conv_w, v_a_log, v_dt_bias, v_gdn_norm_g, v_sb_q_norm_g, v_sb_k_norm_g, v_mem_norm_g, v_w_mem_kv, v_mem_q_norm_g, v_mem_k_norm_g, v_w_br_gdn, v_w_br_sb, v_w_br_mem, v_w_o, v_norm2_g, v_w_up, v_w_down):
    given = dict(norm1_g=norm1_g, w_in=w_in, conv_w=conv_w, a_log=a_log, dt_bias=dt_bias, gdn_norm_g=gdn_norm_g,
                 sb_q_norm_g=sb_q_norm_g, sb_k_norm_g=sb_k_norm_g, mem_norm_g=mem_norm_g, w_mem_kv=w_mem_kv,
                 mem_q_norm_g=mem_q_norm_g, mem_k_norm_g=mem_k_norm_g, w_br_gdn=w_br_gdn, w_br_sb=w_br_sb,
                 w_br_mem=w_br_mem, w_o=w_o, norm2_g=norm2_g, w_up=w_up, w_down=w_down)
    mom1 = dict(norm1_g=m_norm1_g, w_in=m_w_in, conv_w=m_conv_w, a_log=m_a_log, dt_bias=m_dt_bias,
                gdn_norm_g=m_gdn_norm_g, sb_q_norm_g=m_sb_q_norm_g, sb_k_norm_g=m_sb_k_norm_g,
                mem_norm_g=m_mem_norm_g, w_mem_kv=m_w_mem_kv, mem_q_norm_g=m_mem_q_norm_g,
                mem_k_norm_g=m_mem_k_norm_g, w_br_gdn=m_w_br_gdn, w_br_sb=m_w_br_sb, w_br_mem=m_w_br_mem, w_o=m_w_o,
                norm2_g=m_norm2_g, w_up=m_w_up, w_down=m_w_down)
    mom2 = dict(norm1_g=v_norm1_g, w_in=v_w_in, conv_w=v_conv_w, a_log=v_a_log, dt_bias=v_dt_bias,
                gdn_norm_g=v_gdn_norm_g, sb_q_norm_g=v_sb_q_norm_g, sb_k_norm_g=v_sb_k_norm_g,
                mem_norm_g=v_mem_norm_g, w_mem_kv=v_w_mem_kv, mem_q_norm_g=v_mem_q_norm_g,
                mem_k_norm_g=v_mem_k_norm_g, w_br_gdn=v_w_br_gdn, w_br_sb=v_w_br_sb, w_br_mem=v_w_br_mem, w_o=v_w_o,
                norm2_g=v_norm2_g, w_up=v_w_up, w_down=v_w_down)
    shapes = {n: given[n].shape for n in WEIGHTS}

    first_loc = _pack_shards(given, FIRST, R_FIRST)
    rest_loc = _pack_shards(given, REST, R_REST)
    gathered = _gather("gather_first", first_loc.astype(BF16))
    w = _unpack_gathered(gathered[:, :sum(SLAB_ROWS[n] for n in FIRST)], FIRST)
    w["w_in"] = _pad_w_in(w["w_in"])
    conv_loc = jnp.pad(given["conv_w"][0].reshape(-1, LANES), ((0, 2), (0, 0)))
    conv_all = _gather("gather_conv", conv_loc)
    w["conv_w"] = conv_all[:, :6].reshape(NDEV, 4, 3 * HW // NDEV).transpose(1, 0, 2).reshape(4, 3 * HW)
    sm = {n: given[n] for n in SMALL}

    loss, dx, g, rest_all = _local_step(x[0], mem[0], loss_target[0], w, sm, rest_loc.astype(BF16))
    g["w_in"] = _unpad_w_in(g["w_in"])

    first_mine = _pack_full_grads(g, FIRST, R_FIRST).astype(BF16)
    first_pair = _pair_sum("pair_sum_first", first_mine, _sibling_exchange("scatter_sibling_first", first_mine))
    first_all = _chip_exchange("scatter_chips_first", first_pair)
    res_first = _adamw("adamw_first", first_all, first_loc, _pack_shards(mom1, FIRST, R_FIRST),
                       _pack_shards(mom2, FIRST, R_FIRST))
    res_rest = _adamw("adamw_rest", rest_all, rest_loc, _pack_shards(mom1, REST, R_REST),
                      _pack_shards(mom2, REST, R_REST))
    gs_all = _gather("gather_small_grads", _pack_vec(g))
    vec_outs = _adamw_vec(gs_all, _pack_vec(given), _pack_vec(mom1), _pack_vec(mom2))

    outs = {}
    for r, prefix in enumerate(("grad_", "delta_", "new_m_", "new_v_")):
        vals = _unpack_shard(res_first[r], FIRST, shapes)
        vals.update(_unpack_shard(res_rest[r], REST, shapes))
        vals.update(_unpack_vec(vec_outs, r))
        for n in WEIGHTS:
            outs[prefix + n] = vals[n]
    loss = lax.psum(loss, ("x", "y", "c"))
    return (loss, dx[None], *[outs[p + n] for p in ("grad_", "delta_", "new_m_", "new_v_") for n in WEIGHTS])
```

```python
import jax
import jax.numpy as jnp
from jax import lax
from jax.experimental import pallas as pl
from jax.experimental.pallas import tpu as pltpu

F32 = jnp.float32
BF16 = jnp.bfloat16

D = 1024
NH = 4
DH = 128
HW = NH * DH
DFF = 4 * D
NMEM = 256
EPS = 1e-6
NDEV = 8
LANES = 128
PAIR = 128
CHUNK = 64
D_IN = 7176
D_INP = 7680
VMEM_LIMIT = 56 * 1024 * 1024

ADAM_LR, ADAM_B1, ADAM_B2, ADAM_EPS, ADAM_WD, ADAM_STEP = 0.001, 0.9, 0.999, 1e-08, 0.01, 10

CB_Z, CB_SQ, CB_SK, CB_SV, CB_MQ, CB_AB = 3, 4, 5, 6, 7, 14

NN = (((1,), (0,)), ((), ()))
NT = (((1,), (1,)), ((), ()))
TN = (((0,), (0,)), ((), ()))

BIG = ("w_in", "w_mem_kv", "w_br_gdn", "w_br_sb", "w_br_mem", "w_o", "w_up", "w_down", "conv_w")
BIG_ROWS = (7176, 1024, 512, 512, 512, 1024, 4096, 4096, 6)
SLAB_ROWS = dict(zip(BIG, BIG_ROWS))
SLAB_TILE = 1216
FIRST = ("w_in", "conv_w")
REST = ("w_mem_kv", "w_br_gdn", "w_br_sb", "w_br_mem", "w_o", "w_up", "w_down")
R_FIRST = 6 * SLAB_TILE
R_REST = 10 * SLAB_TILE
SMALL = ("norm1_g", "a_log", "dt_bias", "gdn_norm_g", "sb_q_norm_g", "sb_k_norm_g", "mem_norm_g",
         "mem_q_norm_g", "mem_k_norm_g", "norm2_g")
VEC = ("norm1_g", "mem_norm_g", "norm2_g", "gdn_norm_g", "sb_q_norm_g", "sb_k_norm_g", "mem_q_norm_g", "mem_k_norm_g",
       "a_log", "dt_bias")
VEC_SIZES = (1024, 1024, 1024, 128, 128, 128, 128, 128, 4, 4)
VEC_OFFSETS = (0, 1024, 2048, 3072, 3200, 3328, 3456, 3584, 3712, 3716)
VEC_WIDTH = 3840
WEIGHTS = ("norm1_g", "w_in", "conv_w", "a_log", "dt_bias", "gdn_norm_g", "sb_q_norm_g", "sb_k_norm_g",
           "mem_norm_g", "w_mem_kv", "mem_q_norm_g", "mem_k_norm_g", "w_br_gdn", "w_br_sb", "w_br_mem",
           "w_o", "norm2_g", "w_up", "w_down")


def _cp(sem=None):
    return pltpu.CompilerParams(dimension_semantics=sem, vmem_limit_bytes=VMEM_LIMIT)


def _dot(a, b, dims=NN):
    return lax.dot_general(a, b, dims, preferred_element_type=F32)


def _dbf(a, b, dims=NN):
    return _dot(a.astype(BF16), b.astype(BF16), dims)


def _split(a, n):
    parts = []
    for _ in range(n):
        h = a.astype(BF16)
        parts.append(h)
        a = a - h.astype(F32)
    return parts


def _dg(a, b, dims=NN):
    return _dbf(a, b, dims)


def _dxr(a, e, dims=NN):
    eb = e.astype(BF16)
    a1, a2, a3 = _split(a, 3)
    return _dot(a1, eb, dims) + (_dot(a2, eb, dims) + _dot(a3, eb, dims))


def _dxl(e, a, dims=NN):
    eb = e.astype(BF16)
    a1, a2, a3 = _split(a, 3)
    return _dot(eb, a1, dims) + (_dot(eb, a2, dims) + _dot(eb, a3, dims))


def _sigmoid(x):
    return 1.0 / (1.0 + jnp.exp(-x))


def _softplus(x):
    return jnp.maximum(x, 0.0) + jnp.log(1.0 + jnp.exp(-jnp.abs(x)))


def _rms(x, g):
    r = lax.rsqrt(jnp.mean(x * x, axis=-1, keepdims=True) + EPS)
    return x * r * g, r


def _rms_bwd(dy, x, g, r):
    dyg = dy * g
    dx = r * (dyg - x * (r * r) * jnp.mean(dyg * x, axis=-1, keepdims=True))
    dg = jnp.sum(dy * (x * r), axis=0, keepdims=True)
    return dx, dg


def _hs(h):
    return slice(h * DH, (h + 1) * DH)


def _row_tile(s):
    return 512 if s >= 2048 else 256


def _narrow_tile(s):
    return min(256, s)


def _mm(name, a, b, mode, tm, tn, tk, pro=None, pro_g=None, epi=None, epi_x=None, out_dtype=F32, n_outer=False,
        comm=None):
    if mode == "tn":
        K, M = a.shape
    else:
        M, K = a.shape
    N = b.shape[0] if mode == "nt" else b.shape[1]
    tm, tn, tk = min(tm, M), min(tn, N), min(tk, K)
    nk = K // tk
    assert M % tm == 0 and N % tn == 0 and K % tk == 0, (name, M, N, K, tm, tn, tk)
    dims = {"nn": NN, "nt": NT, "tn": TN}[mode]
    reducing = epi in ("rms_bwd", "loss")
    assert not reducing or (tn == N and not n_outer), name
    epi_ops = () if epi is None else (epi_x if isinstance(epi_x, tuple) else (epi_x,))

    def body(*refs):
        a_ref, b_ref = refs[0], refs[1]
        pos = 2
        g_ref = None
        if pro == "rms":
            g_ref = refs[pos]
            pos += 1
        e_refs = refs[pos:pos + len(epi_ops)]
        pos += len(epi_ops)
        cx_ref = None
        if comm is not None:
            cx_ref = refs[pos]
            pos += 1
        o_ref = refs[pos]
        pos += 1
        r_ref = None
        if reducing:
            r_ref = refs[pos]
            pos += 1
        if comm is not None:
            steps_of = _gather_steps if comm[0] == "gather" else _chip_steps
            start, forward, finish_comm = steps_of(cx_ref, refs[pos], *refs[-3:])
            pos += 1
            step = (pl.program_id(0) * grid[1] + pl.program_id(1)) * nk + pl.program_id(2)
            total = grid[0] * grid[1] * nk
            pl.when(step == 0)(start)
            pl.when(step == (4 * total) // 5)(forward)
        av = a_ref[...]
        if pro == "rms":
            av, _ = _rms(av.astype(F32), g_ref[...])
        elif pro == "relu2":
            av = jnp.square(jnp.maximum(av, 0.0))
        part = _dbf(av, b_ref[...], dims)
        first = pl.program_id(0) == 0

        def finish(acc):
            red = None
            if epi == "add":
                acc = acc + e_refs[0][...]
            elif epi == "drelu2":
                acc = acc * (2.0 * jnp.maximum(e_refs[0][...], 0.0))
            elif epi == "rms_bwd":
                xv, gv = e_refs[0][...], e_refs[1][...]
                _, r = _rms(xv, gv)
                dx, red = _rms_bwd(acc, xv, gv, r)
                acc = dx + e_refs[2][...]
            elif epi == "loss":
                err = acc + e_refs[0][...] - e_refs[1][...]
                acc = err * (1.0 / N)
                per_tok = jnp.sum(err * err, axis=1, keepdims=True) * (1.0 / N)
                red = 0.5 * jnp.sum(per_tok, axis=0, keepdims=True)
            o_ref[...] = acc.astype(out_dtype)
            if reducing:

                @pl.when(first)
                def _():
                    r_ref[...] = red

                @pl.when(jnp.logical_not(first))
                def _():
                    r_ref[...] += red

        if nk == 1:
            finish(part)
        else:
            acc_ref = refs[pos]
            k = pl.program_id(2)

            @pl.when(k == 0)
            def _():
                acc_ref[...] = part

            @pl.when(k > 0)
            def _():
                acc_ref[...] += part

            @pl.when(k == nk - 1)
            def _():
                finish(acc_ref[...])

        if comm is not None:
            pl.when(step == total - 1)(finish_comm)

    def spec(shape, index):
        if n_outer:
            return pl.BlockSpec(shape, lambda j, i, k: index(i, j, k))
        return pl.BlockSpec(shape, index)

    if mode == "tn":
        a_spec = spec((tk, tm), lambda i, j, k: (k, i))
    else:
        a_spec = spec((tm, tk), lambda i, j, k: (i, k))
    if mode == "nt":
        b_spec = spec((tn, tk), lambda i, j, k: (j, k))
    else:
        b_spec = spec((tk, tn), lambda i, j, k: (k, j))
    in_specs, ops = [a_spec, b_spec], [a, b]
    if pro == "rms":
        w = pro_g.shape[1]
        assert (tm if mode == "tn" else tk) == w, name
        in_specs.append(spec((1, w), lambda i, j, k: (0, 0)))
        ops.append(pro_g)
    for op in epi_ops:
        if op.shape[0] == 1:
            in_specs.append(spec((1, tn), lambda i, j, k: (0, j)))
        else:
            in_specs.append(spec((tm, tn), lambda i, j, k: (i, j)))
        ops.append(op)
    out_specs = [spec((tm, tn), lambda i, j, k: (i, j))]
    out_shape = [jax.ShapeDtypeStruct((M, N), out_dtype)]
    if reducing:
        width = N if epi == "rms_bwd" else 1
        out_specs.append(spec((1, width), lambda i, j, k: (0, 0)))
        out_shape.append(jax.ShapeDtypeStruct((1, width), F32))
    scratch = [pltpu.VMEM((tm, tn), F32)] if nk > 1 else []
    if comm is not None:
        kind, cx = comm
        in_specs.append(HBM_SPEC)
        ops.append(cx)
        out_specs.append(HBM_SPEC)
        out_shape.append(jax.ShapeDtypeStruct((NDEV if kind == "gather" else NDEV // 2,) + cx.shape[-2:], cx.dtype))
        scratch += list(GATHER_SEMS if kind == "gather" else CHIP_SEMS)
    grid = (N // tn, M // tm, nk) if n_outer else (M // tm, N // tn, nk)
    ordered = reducing or comm is not None
    outs = pl.pallas_call(
        body, name=name, grid=grid,
        in_specs=in_specs, out_specs=out_specs, out_shape=out_shape, scratch_shapes=scratch,
        compiler_params=_cp(("arbitrary" if ordered else "parallel", "arbitrary" if comm is not None else "parallel",
                             "arbitrary")),
    )(*ops)
    return outs if len(out_shape) > 1 else outs[0]


def _head_select(first_lane):
    l = lax.broadcasted_iota(jnp.int32, (LANES, HW), 0)
    c = lax.broadcasted_iota(jnp.int32, (LANES, HW), 1)
    return (l == first_lane + c // DH).astype(F32)


def _conv_taps(buf, cw, ts):
    c = cw[3:4, :] * buf[8:8 + ts, :]
    for j in range(3):
        k = 3 - j
        c = c + cw[j:j + 1, :] * buf[8 - k:8 - k + ts, :]
    return c


def _pre_fwd(proj, conv_w, alog_f, dtb_f, gsq, gsk, gmq, S):
    ts = _narrow_tile(S)
    hb = ts // 8

    def body(qkv_ref, halo_ref, ab_ref, sq_ref, sk_ref, sv_ref, mq_ref, cw_ref, al_ref, dt_ref, gsq_ref, gsk_ref,
             gmq_ref, gq_o, gk_o, gv_o, gf_o, bf_o, sqn_o, skn_o, svb_o, qmn_o, buf):
        i = pl.program_id(0)
        buf[0:8, :] = jnp.where(i == 0, 0.0, halo_ref[...])
        buf[8:8 + ts, :] = qkv_ref[...]
        c = _conv_taps(buf, cw_ref[...], ts)
        a = c * _sigmoid(c)
        for h in range(NH):
            q = a[:, h * DH:(h + 1) * DH]
            k = a[:, HW + h * DH:HW + (h + 1) * DH]
            gq_o[:, _hs(h)] = q * (lax.rsqrt(jnp.sum(q * q, axis=-1, keepdims=True) + EPS) * DH ** -0.5)
            gk_o[:, _hs(h)] = k * lax.rsqrt(jnp.sum(k * k, axis=-1, keepdims=True) + EPS)
            sqn_o[:, _hs(h)] = _rms(sq_ref[:, _hs(h)], gsq_ref[...])[0].astype(BF16)
            skn_o[:, _hs(h)] = _rms(sk_ref[:, _hs(h)], gsk_ref[...])[0].astype(BF16)
            qmn_o[:, _hs(h)] = _rms(mq_ref[:, _hs(h)], gmq_ref[...])[0].astype(BF16)
        gv_o[...] = a[:, 2 * HW:3 * HW]
        svb_o[...] = sv_ref[...].astype(BF16)
        ab = ab_ref[:, 0:LANES]
        a_bc = _dxr(ab, _head_select(0))
        b_bc = _dxr(ab, _head_select(NH))
        gf_o[...] = -jnp.exp(al_ref[...]) * _softplus(a_bc + dt_ref[...])
        bf_o[...] = _sigmoid(b_bc)

    row = lambda cb: pl.BlockSpec((ts, HW), lambda i: (i, cb))
    full = lambda r, c: pl.BlockSpec((r, c), lambda i: (0, 0))
    f32o = jax.ShapeDtypeStruct((S, HW), F32)
    bfo = jax.ShapeDtypeStruct((S, HW), BF16)
    return pl.pallas_call(
        body, name="pre_fwd", grid=(S // ts,),
        in_specs=[pl.BlockSpec((ts, 3 * HW), lambda i: (i, 0)),
                  pl.BlockSpec((8, 3 * HW), lambda i: (jnp.maximum(i * hb - 1, 0), 0)),
                  row(CB_AB), row(CB_SQ), row(CB_SK), row(CB_SV), row(CB_MQ),
                  full(4, 3 * HW), full(1, HW), full(1, HW), full(1, DH), full(1, DH), full(1, DH)],
        out_specs=[pl.BlockSpec((ts, HW), lambda i: (i, 0))] * 9,
        out_shape=[f32o, f32o, f32o, f32o, f32o, bfo, bfo, bfo, bfo],
        scratch_shapes=[pltpu.VMEM((ts + 8, 3 * HW), F32)],
        compiler_params=_cp(("parallel",)),
    )(proj, proj, proj, proj, proj, proj, proj, conv_w, alog_f, dtb_f, gsq, gsk, gmq)


def _pre_bwd(proj, conv_w, alog_f, dtb_f, gsq, gsk, dgq, dgk, dgv, dgf, dbf, dsqn, dskn, S):
    ts = _narrow_tile(S)
    hb = ts // 8

    def body(qkv_ref, halo_ref, ab_ref, sq_ref, sk_ref, cw_ref, al_ref, dt_ref, gsq_ref, gsk_ref,
             dgq_ref, dgk_ref, dgv_ref, dgf_ref, dbf_ref, dsqn_ref, dskn_ref,
             dc_o, dab_o, dsq_o, dsk_o, dcw_o, dal_o, ddt_o, dgsq_o, dgsk_o, buf):
        i = pl.program_id(0)

        @pl.when(i == 0)
        def _():
            dcw_o[...] = jnp.zeros_like(dcw_o)
            dal_o[...] = jnp.zeros_like(dal_o)
            ddt_o[...] = jnp.zeros_like(ddt_o)
            dgsq_o[...] = jnp.zeros_like(dgsq_o)
            dgsk_o[...] = jnp.zeros_like(dgsk_o)

        buf[0:8, :] = jnp.where(i == 0, 0.0, halo_ref[...])
        buf[8:8 + ts, :] = qkv_ref[...]
        c = _conv_taps(buf, cw_ref[...], ts)
        sg = _sigmoid(c)
        a = c * sg
        dsilu = sg * (1.0 + c * (1.0 - sg))
        dgsq = jnp.zeros((1, DH), F32)
        dgsk = jnp.zeros((1, DH), F32)
        for h in range(NH):
            q = a[:, h * DH:(h + 1) * DH]
            k = a[:, HW + h * DH:HW + (h + 1) * DH]
            nq = lax.rsqrt(jnp.sum(q * q, axis=-1, keepdims=True) + EPS)
            nk = lax.rsqrt(jnp.sum(k * k, axis=-1, keepdims=True) + EPS)
            dyq = dgq_ref[:, _hs(h)]
            dyk = dgk_ref[:, _hs(h)]
            dq = (nq * dyq - q * (nq * nq * nq) * jnp.sum(dyq * q, axis=-1, keepdims=True)) * DH ** -0.5
            dk = nk * dyk - k * (nk * nk * nk) * jnp.sum(dyk * k, axis=-1, keepdims=True)
            dc_o[:, h * DH:(h + 1) * DH] = dq * dsilu[:, h * DH:(h + 1) * DH]
            dc_o[:, HW + h * DH:HW + (h + 1) * DH] = dk * dsilu[:, HW + h * DH:HW + (h + 1) * DH]
            x = sq_ref[:, _hs(h)]
            _, r = _rms(x, gsq_ref[...])
            dx, dg = _rms_bwd(dsqn_ref[:, _hs(h)], x, gsq_ref[...], r)
            dsq_o[:, _hs(h)] = dx.astype(BF16)
            dgsq = dgsq + dg
            x = sk_ref[:, _hs(h)]
            _, r = _rms(x, gsk_ref[...])
            dx, dg = _rms_bwd(dskn_ref[:, _hs(h)], x, gsk_ref[...], r)
            dsk_o[:, _hs(h)] = dx.astype(BF16)
            dgsk = dgsk + dg
        dc_o[:, 2 * HW:3 * HW] = dgv_ref[...] * dsilu[:, 2 * HW:3 * HW]
        dgsq_o[...] += dgsq
        dgsk_o[...] += dgsk
        dc = dc_o[...]
        for j in range(4):
            k = 3 - j
            dcw_o[j:j + 1, :] += jnp.sum(dc * buf[8 - k:8 - k + ts, :], axis=0, keepdims=True)
        ab = ab_ref[:, 0:LANES]
        a_bc = _dxr(ab, _head_select(0))
        b_bc = _dxr(ab, _head_select(NH))
        pre = a_bc + dt_ref[...]
        ea = jnp.exp(al_ref[...])
        dgf = dgf_ref[...]
        dal_o[...] += jnp.sum(dgf * (-ea * _softplus(pre)), axis=0, keepdims=True)
        da = dgf * (-ea * _sigmoid(pre))
        ddt_o[...] += jnp.sum(da, axis=0, keepdims=True)
        beta = _sigmoid(b_bc)
        db = dbf_ref[...] * beta * (1.0 - beta)
        lane = lax.broadcasted_iota(jnp.int32, (ts, LANES), 1)
        dab = jnp.zeros((ts, LANES), F32)
        for h in range(NH):
            dab = dab + jnp.where(lane == h, da[:, _hs(h)], 0.0) + jnp.where(lane == NH + h, db[:, _hs(h)], 0.0)
        dab_o[:, 0:LANES] = dab.astype(BF16)
        dab_o[:, LANES:HW] = jnp.zeros((ts, HW - LANES), BF16)

    row = lambda cb: pl.BlockSpec((ts, HW), lambda i: (i, cb))
    full = lambda r, c: pl.BlockSpec((r, c), lambda i: (0, 0))
    t512 = pl.BlockSpec((ts, HW), lambda i: (i, 0))
    return pl.pallas_call(
        body, name="pre_bwd", grid=(S // ts,),
        in_specs=[pl.BlockSpec((ts, 3 * HW), lambda i: (i, 0)),
                  pl.BlockSpec((8, 3 * HW), lambda i: (jnp.maximum(i * hb - 1, 0), 0)),
                  row(CB_AB), row(CB_SQ), row(CB_SK),
                  full(4, 3 * HW), full(1, HW), full(1, HW), full(1, DH), full(1, DH)] + [t512] * 7,
        out_specs=[pl.BlockSpec((ts, 3 * HW), lambda i: (i, 0)), t512, t512, t512,
                   full(4, 3 * HW), full(1, HW), full(1, HW), full(1, DH), full(1, DH)],
        out_shape=[jax.ShapeDtypeStruct((S, 3 * HW), F32)] + [jax.ShapeDtypeStruct((S, HW), BF16)] * 3
        + [jax.ShapeDtypeStruct((4, 3 * HW), F32), jax.ShapeDtypeStruct((1, HW), F32),
           jax.ShapeDtypeStruct((1, HW), F32), jax.ShapeDtypeStruct((1, DH), F32),
           jax.ShapeDtypeStruct((1, DH), F32)],
        scratch_shapes=[pltpu.VMEM((ts + 8, 3 * HW), F32)],
        compiler_params=_cp(("arbitrary",)),
    )(proj, proj, proj, proj, proj, conv_w, alog_f, dtb_f, gsq, gsk, dgq, dgk, dgv, dgf, dbf, dsqn, dskn)


def _conv_bwd(dc, conv_w, S):
    ts = _row_tile(S)
    hb = ts // 8
    n = S // ts

    def body(dc_ref, halo_ref, cw_ref, o_ref, buf):
        i = pl.program_id(0)
        buf[0:ts, :] = dc_ref[...]
        buf[ts:ts + 8, :] = jnp.where(i == n - 1, 0.0, halo_ref[...])
        cw = cw_ref[...]
        acc = cw[3:4, :] * buf[0:ts, :]
        for k in range(1, 4):
            acc = acc + cw[3 - k:4 - k, :] * buf[k:k + ts, :]
        o_ref[...] = acc.astype(BF16)

    return pl.pallas_call(
        body, name="conv_bwd", grid=(n,),
        in_specs=[pl.BlockSpec((ts, 3 * HW), lambda i: (i, 0)),
                  pl.BlockSpec((8, 3 * HW), lambda i: (jnp.minimum((i + 1) * hb, S // 8 - 1), 0)),
                  pl.BlockSpec((4, 3 * HW), lambda i: (0, 0))],
        out_specs=pl.BlockSpec((ts, 3 * HW), lambda i: (i, 0)),
        out_shape=jax.ShapeDtypeStruct((S, 3 * HW), BF16),
        scratch_shapes=[pltpu.VMEM((ts + 8, 3 * HW), F32)],
        compiler_params=_cp(("parallel",)),
    )(dc, dc, conv_w)


def _gdn_masks():
    r = lax.broadcasted_iota(jnp.int32, (PAIR, PAIR), 0)
    c = lax.broadcasted_iota(jnp.int32, (PAIR, PAIR), 1)
    same = ((r >= CHUNK) & (c >= CHUNK)) | ((r < CHUNK) & (c < CHUNK))
    return dict(r=r, same=same, tril=same & (r >= c), strict=same & (r > c), triu=same & (c >= r), eye=r == c,
                in_a=r < CHUNK, last_a=r == CHUNK - 1, last_b=r == PAIR - 1)


def _each(fn, *cols):
    return [fn(*xs) for xs in zip(*cols)]


def _mul(a, b):
    return a * b


def _top(x):
    return x[:CHUNK]


def _bot(x):
    return x[CHUNK:]


def _rows(a, b):
    return jnp.concatenate([a, b], axis=0)


def _tri_inv(lm, eye):
    eye_f = eye.astype(F32)
    p = _each(lambda l: eye_f - l, lm)
    lp = _each(lambda l: _dg(l, l), lm)
    for it in range(5):
        p = _each(lambda a, b: a + _dg(a, b), p, lp)
        if it < 4:
            lp = _each(lambda b: _dg(b, b), lp)
    return p


def _gdn_block(m, q, k, v, g, beta):
    tril_f = m["tril"].astype(F32)
    col_sum = lambda mask: (lambda x: jnp.sum(jnp.where(mask, x, 0.0), axis=0, keepdims=True))
    gc = _each(lambda x: _dxl(tril_f, x), g)
    gcr = _each(col_sum(m["eye"]), gc)
    gam = _each(lambda a, b: jnp.where(m["tril"], jnp.exp(jnp.minimum(a - b, 0.0)), 0.0), gc, gcr)
    kb = _each(_mul, k, beta)
    vb = _each(_mul, v, beta)
    lm = _each(lambda a, b, c: jnp.where(m["strict"], _dg(a, b, NT) * c, 0.0), kb, k, gam)
    t = _tri_inv(lm, m["eye"])
    eg = _each(jnp.exp, gc)
    kbe = _each(_mul, kb, eg)
    u = _each(_dg, t, vb)
    w = _each(_dg, t, kbe)
    aqk = _each(lambda a, b, c: jnp.where(m["tril"], _dg(a, b, NT) * c, 0.0), q, k, gam)
    qd = _each(_mul, q, eg)
    ga = _each(col_sum(m["last_a"]), gc)
    gb = _each(col_sum(m["last_b"]), gc)
    e2 = _each(lambda a, b, c: jnp.exp(jnp.where(m["in_a"], a, b) - c), ga, gb, gc)
    kd = _each(_mul, k, e2)
    return dict(u=u, w=w, aqk=aqk, qd=qd, kd=kd, gam=gam, kb=kb, vb=vb, lm=lm, t=t, eg=eg, kbe=kbe, e2=e2,
                gla=_each(jnp.exp, ga), glb=_each(jnp.exp, gb))


def _gdn_fwd(gq, gk, gv, gf, bf, S):
    nb = S // PAIR

    def body(q_ref, k_ref, v_ref, g_ref, b_ref, o_ref, st_ref, s_scr):
        @pl.when(pl.program_id(0) == 0)
        def _():
            s_scr[...] = jnp.zeros_like(s_scr)

        m = _gdn_masks()
        heads = lambda ref: [ref[:, _hs(h)] for h in range(NH)]
        f = _gdn_block(m, heads(q_ref), heads(k_ref), heads(v_ref), heads(g_ref), heads(b_ref))
        u, w, qd, kd = f["u"], f["w"], f["qd"], f["kd"]
        s0 = [s_scr[h * DH:(h + 1) * DH, :] for h in range(NH)]
        vna = _each(lambda a, b, s: _top(a) - _dg(_top(b), s), u, w, s0)
        oa = _each(lambda a, s: _dg(_top(a), s), qd, s0)
        s1 = _each(lambda s, gl, a, vn: s * gl + _dg(_top(a), vn, TN), s0, f["gla"], kd, vna)
        vnb = _each(lambda a, b, s: _bot(a) - _dg(_bot(b), s), u, w, s1)
        ob = _each(lambda a, s: _dg(_bot(a), s), qd, s1)
        s2 = _each(lambda s, gl, a, vn: s * gl + _dg(_bot(a), vn, TN), s1, f["glb"], kd, vnb)
        outs = _each(lambda a, b, c, va, vb: _rows(a, b) + _dg(c, _rows(va, vb)), oa, ob, f["aqk"], vna, vnb)
        o_ref[...] = jnp.concatenate(outs, axis=1)
        st_ref[...] = jnp.concatenate(s0 + s1, axis=0)
        s_scr[...] = jnp.concatenate(s2, axis=0)

    blk = pl.BlockSpec((PAIR, HW), lambda i: (i, 0))
    return pl.pallas_call(
        body, name="gdn_fwd", grid=(nb,),
        in_specs=[blk] * 5,
        out_specs=[blk, pl.BlockSpec((2 * NH * DH, DH), lambda i: (i, 0))],
        out_shape=[jax.ShapeDtypeStruct((S, HW), F32), jax.ShapeDtypeStruct((nb * 2 * NH * DH, DH), F32)],
        scratch_shapes=[pltpu.VMEM((NH * DH, DH), F32)],
        compiler_params=_cp(("arbitrary",)),
    )(gq, gk, gv, gf, bf)


def _gdn_bwd(gq, gk, gv, gf, bf, states, do, S):
    nb = S // PAIR

    def body(q_ref, k_ref, v_ref, g_ref, b_ref, st_ref, do_ref, dq_o, dk_o, dv_o, dg_o, db_o, ds_scr):
        @pl.when(pl.program_id(0) == 0)
        def _():
            ds_scr[...] = jnp.zeros_like(ds_scr)

        m = _gdn_masks()
        ones = jnp.ones((PAIR, PAIR), F32)
        heads = lambda ref: [ref[:, _hs(h)] for h in range(NH)]
        q, k, v, beta, do = heads(q_ref), heads(k_ref), heads(v_ref), heads(b_ref), heads(do_ref)
        f = _gdn_block(m, q, k, v, heads(g_ref), beta)
        u, w, aqk, qd, kd, t = f["u"], f["w"], f["aqk"], f["qd"], f["kd"], f["t"]
        s0 = [st_ref[h * DH:(h + 1) * DH, :] for h in range(NH)]
        s1 = [st_ref[(NH + h) * DH:(NH + h + 1) * DH, :] for h in range(NH)]
        ds2 = [ds_scr[h * DH:(h + 1) * DH, :] for h in range(NH)]
        total = lambda a, b: jnp.sum(jnp.sum(a * b, axis=1, keepdims=True), axis=0, keepdims=True)
        vna = _each(lambda a, b, s: _top(a) - _dg(_top(b), s), u, w, s0)
        vnb = _each(lambda a, b, s: _bot(a) - _dg(_bot(b), s), u, w, s1)
        dvn_i = _each(lambda a, b: _dg(a, b, TN), aqk, do)
        dvnb = _each(lambda a, b, s: _bot(a) + _dg(_bot(b), s), dvn_i, kd, ds2)
        dqdb = _each(lambda a, s: _dg(_bot(a), s, NT), do, s1)
        dkdb = _each(lambda a, s: _dg(a, s, NT), vnb, ds2)
        dglb = _each(total, ds2, s1)
        dwb = _each(lambda a, s: -_dg(a, s, NT), dvnb, s1)
        ds1 = _each(lambda s, gl, a, b, c, d: s * gl + _dg(_bot(a), _bot(b), TN) - _dg(_bot(c), d, TN),
                    ds2, f["glb"], qd, do, w, dvnb)
        dvna = _each(lambda a, b, s: _top(a) + _dg(_top(b), s), dvn_i, kd, ds1)
        dqda = _each(lambda a, s: _dg(_top(a), s, NT), do, s0)
        dkda = _each(lambda a, s: _dg(a, s, NT), vna, ds1)
        dgla = _each(total, ds1, s0)
        dwa = _each(lambda a, s: -_dg(a, s, NT), dvna, s0)
        ds0 = _each(lambda s, gl, a, b, c, d: s * gl + _dg(_top(a), _top(b), TN) - _dg(_top(c), d, TN),
                    ds1, f["gla"], qd, do, w, dvna)
        dvn, dqd, dkd, dw = (_each(_rows, a, b) for a, b in ((dvna, dvnb), (dqda, dqdb), (dkda, dkdb), (dwa, dwb)))
        daqk = _each(lambda a, va, vb: jnp.where(m["tril"], _dg(a, _rows(va, vb), NT), 0.0), do, vna, vnb)
        dt = _each(lambda a, b, c, d: _dg(a, b, NT) + _dg(c, d, NT), dvn, f["vb"], dw, f["kbe"])
        dvb = _each(lambda a, b: _dg(a, b, TN), t, dvn)
        dkbe = _each(lambda a, b: _dg(a, b, TN), t, dw)
        dtt = _each(lambda a, b: _dg(a, b, NT), dt, t)
        dl = _each(lambda a, b: -jnp.where(m["strict"], _dg(a, b, TN), 0.0), t, dtt)
        dm = _each(_mul, dl, f["gam"])
        dn = _each(_mul, daqk, f["gam"])
        dkb = _each(lambda a, b, c, d: _dg(a, b) + c * d, dm, k, dkbe, f["eg"])
        dks = _each(lambda a, b, c, d, e, g, h, i: _dg(a, b, TN) + _dg(c, d, TN) + e * g + h * i,
                    dm, f["kb"], dn, q, dkd, f["e2"], beta, dkb)
        dqs = _each(lambda a, b, c, d: _dg(a, b) + c * d, dn, k, dqd, f["eg"])
        gm = _each(lambda a, b, c, d: a * b + c * d, dl, f["lm"], daqk, aqk)
        dkdkd = _each(_mul, dkd, kd)
        dgc = _each(lambda a, b, c, d, e, g: _dxr(a + b * c + d * e - g, ones) - _dxr(a, ones, TN),
                    gm, dqd, qd, dkbe, f["kbe"], dkdkd)
        same_f = m["same"].astype(F32)
        chunk_tot = _each(lambda a: _dxl(same_f, _dxr(a, ones)), dkdkd)
        last = m["last_a"] | m["last_b"]
        dgc = _each(lambda a, b, ga, gla, gb, glb: a + jnp.where(last, b + jnp.where(m["in_a"], ga * gla, gb * glb), 0.0),
                    dgc, chunk_tot, dgla, f["gla"], dglb, f["glb"])
        dbs = _each(lambda a, b, c, d: _dxr(a * b + c * d, ones), dkb, k, dvb, v)
        dvs = _each(_mul, beta, dvb)
        triu_f = m["triu"].astype(F32)
        dgs = _each(lambda a: _dxl(triu_f, a), dgc)
        for ref, parts in ((dq_o, dqs), (dk_o, dks), (dv_o, dvs), (dg_o, dgs), (db_o, dbs)):
            ref[...] = jnp.concatenate(parts, axis=1)
        ds_scr[...] = jnp.concatenate(ds0, axis=0)

    blk = pl.BlockSpec((PAIR, HW), lambda i: (nb - 1 - i, 0))
    o = jax.ShapeDtypeStruct((S, HW), F32)
    return pl.pallas_call(
        body, name="gdn_bwd", grid=(nb,),
        in_specs=[blk] * 5 + [pl.BlockSpec((2 * NH * DH, DH), lambda i: (nb - 1 - i, 0)), blk],
        out_specs=[blk] * 5, out_shape=[o] * 5,
        scratch_shapes=[pltpu.VMEM((NH * DH, DH), F32)],
        compiler_params=_cp(("arbitrary",)),
    )(gq, gk, gv, gf, bf, states, do)


SB_T = 256
SB_GROUP = 4
SB_GROUP_BWD = 4
SB_SINGLES = 1
SB_DEAD = -110.0


def _group_sizes(g):
    sizes = []
    while g >= 1:
        sizes.append(g)
        g //= 2
    return sizes


def _sb_iotas(t):
    return lax.broadcasted_iota(jnp.int32, (t, t), 0), lax.broadcasted_iota(jnp.int32, (t, t), 1)


def _sb_scores(q, k, mask):
    z = _dot(q, k, NT) * DH ** -0.5
    ls = jnp.minimum(z, 0.0) - jnp.log(1.0 + jnp.exp(-jnp.abs(z)))
    lneg = ls - z
    if mask is not None:
        lneg = jnp.where(mask, lneg, 0.0)
    return ls, lneg


def _prefix(x, u):
    xh, xl = _split(x, 2)
    return _dot(xh, u) + _dot(xl, u)


def _sb_fwd(sqn, skn, svb, S):
    t = min(SB_T, S)

    def body(q_ref, k_ref, v_ref, o_ref, t_ref, cnt_ref):
        qb = pl.program_id(1)
        q = q_ref[...]
        r, c = _sb_iotas(t)
        diag = c < r
        u_after = (r > c).astype(BF16)

        def tiles(k0s, run, masks):
            sc = _each(lambda k0, m: _sb_scores(q, k_ref[pl.ds(k0, t), :], m), k0s, masks)
            ls, lneg = [s[0] for s in sc], [s[1] for s in sc]
            sums = _each(lambda x: jnp.sum(x, axis=1, keepdims=True), lneg)
            pre = _each(lambda x: _prefix(x, u_after), lneg)
            runs = [run]
            for s in sums:
                runs.append(runs[-1] + s)
            att = _each(lambda a, b, rn: jnp.exp(a + (rn + b)), ls, pre, runs[:-1])
            att = _each(lambda a, m: a if m is None else jnp.where(m, a, 0.0), att, masks)
            parts = _each(lambda a, k0: _dot(a.astype(BF16), v_ref[pl.ds(k0, t), :]), att, k0s)
            return sum(parts[1:], parts[0]), runs[-1]

        left = jnp.full((t, t), qb > 0)
        acc, run = tiles([pl.multiple_of(qb * t, t), pl.multiple_of(jnp.maximum(qb - 1, 0) * t, t)],
                         jnp.zeros((t, 1), F32), [diag, left])

        def alive(run):
            return jnp.max(run) >= SB_DEAD

        carry, done = (0, acc, run, alive(run)), jnp.minimum(qb, 1)
        for size, limit in [(1, SB_SINGLES)] + [(s, None) for s in _group_sizes(SB_GROUP)]:

            def more(c, size=size, done=done, limit=limit):
                i, _, _, go = c
                fits = done + (i + 1) * size <= qb
                return (fits if limit is None else fits & (i < limit)) & go

            def group(c, size=size, done=done):
                i, acc, run, _ = c
                first = qb - 1 - done - size * i
                part, run = tiles([pl.multiple_of((first - j) * t, t) for j in range(size)], run, [None] * size)
                return i + 1, acc + part, run, alive(run)

            n, acc, run, go = lax.while_loop(more, group, (0,) + carry[1:])
            carry, done = (0, acc, run, go), done + n * size
        o_ref[...] = acc.astype(BF16)
        t_ref[...] = jnp.broadcast_to(run, (t, DH))
        cnt_ref[pl.program_id(0), qb] = done

    qspec = pl.BlockSpec((t, DH), lambda h, i: (i, h))
    kspec = pl.BlockSpec((S, DH), lambda h, i: (0, h))
    return pl.pallas_call(
        body, name="sb_fwd", grid=(NH, S // t),
        in_specs=[qspec, kspec, kspec],
        out_specs=[qspec, qspec, pl.BlockSpec(memory_space=pltpu.SMEM)],
        out_shape=[jax.ShapeDtypeStruct((S, HW), BF16), jax.ShapeDtypeStruct((S, HW), F32),
                   jax.ShapeDtypeStruct((NH, S // t), jnp.int32)],
        compiler_params=_cp(("arbitrary", "arbitrary")),
    )(sqn, skn, svb)


def _sb_bwd(sqn, skn, svb, do, tot, walked, S):
    t = min(SB_T, S)

    def body(cnt_ref, q_ref, k_ref, v_ref, do_ref, t_ref, dq_o, dk_o, dv_o, dv_acc):
        qb = pl.program_id(1)

        @pl.when(qb == 0)
        def _():
            dk_o[...] = jnp.zeros_like(dk_o)
            dv_acc[...] = jnp.zeros_like(dv_acc)

        q = q_ref[...]
        do = do_ref[...].astype(BF16)
        tot_l = jnp.concatenate([t_ref[...]] * (t // DH), axis=1)
        r, c = _sb_iotas(t)
        diag = c < r
        u_upto = (r <= c).astype(BF16)
        u_before = (r < c).astype(BF16)

        def tiles(k0s, run_l, run_e, masks):
            rowsum = lambda x: jnp.sum(x, axis=1, keepdims=True)
            masked = lambda xs: _each(lambda a, m: a if m is None else jnp.where(m, a, 0.0), xs, masks)
            ks = [k_ref[pl.ds(k0, t), :] for k0 in k0s]
            vs = [v_ref[pl.ds(k0, t), :] for k0 in k0s]
            sc = _each(lambda k, m: _sb_scores(q, k, m), ks, masks)
            ls, lneg = [s[0] for s in sc], [s[1] for s in sc]
            sums_l = _each(rowsum, lneg)
            pre_l = _each(lambda x: _prefix(x, u_upto), lneg)
            runs_l = [run_l]
            for s in sums_l:
                runs_l.append(runs_l[-1] + s)
            att = masked(_each(lambda a, b, rn: jnp.exp(a + (tot_l - (rn + b))), ls, pre_l, runs_l[:-1]))
            e = _each(lambda v, a: _dot(do, v, NT) * a, vs, att)
            sums_e = _each(rowsum, e)
            pre_e = _each(lambda x: _prefix(x, u_before), e)
            runs_e = [run_e]
            for s in sums_e:
                runs_e.append(runs_e[-1] + s)
            sg = _each(jnp.exp, ls)
            dz = masked(_each(lambda a, b, rn, s: a * (1.0 - s) - (rn + b) * s, e, pre_e, runs_e[:-1], sg))
            dz = _each(lambda a: (a * DH ** -0.5).astype(BF16), dz)
            dvs = _each(lambda a: _dot(a.astype(BF16), do, TN), att)
            dks = _each(lambda a: _dot(a, q, TN), dz)
            dqs = _each(_dot, dz, ks)
            for k0, dv, dk in zip(k0s, dvs, dks):
                dv_acc[pl.ds(k0, t), :] += dv
                dk_o[pl.ds(k0, t), :] += dk
            return sum(dqs[1:], dqs[0]), runs_l[-1], runs_e[-1]

        walked = cnt_ref[pl.program_id(0), qb]
        early = jnp.maximum(walked - 1, 0)
        z1 = jnp.zeros((t, 1), F32)
        carry, done = (jnp.zeros((t, DH), F32), z1, z1), 0
        for size in _group_sizes(SB_GROUP_BWD):
            n = (early - done) // size

            def group(i, carry, size=size, done=done):
                dq, run_l, run_e = carry
                first = qb - walked + done + size * i
                part, run_l, run_e = tiles([pl.multiple_of((first + j) * t, t) for j in range(size)], run_l, run_e,
                                           [None] * size)
                return dq + part, run_l, run_e

            carry = lax.fori_loop(0, n, group, carry)
            done = done + n * size
        dq, run_l, run_e = carry
        left = jnp.full((t, t), qb > 0)
        part, _, _ = tiles([pl.multiple_of(jnp.maximum(qb - 1, 0) * t, t), pl.multiple_of(qb * t, t)], run_l, run_e,
                           [left, diag])
        dq_o[...] = dq + part

        @pl.when(qb == S // t - 1)
        def _():
            dv_o[...] = dv_acc[...].astype(BF16)

    qspec = pl.BlockSpec((t, DH), lambda h, i, cnt: (i, h))
    kspec = pl.BlockSpec((S, DH), lambda h, i, cnt: (0, h))
    o = jax.ShapeDtypeStruct((S, HW), F32)
    return pl.pallas_call(
        body, name="sb_bwd",
        grid_spec=pltpu.PrefetchScalarGridSpec(
            num_scalar_prefetch=1, grid=(NH, S // t),
            in_specs=[qspec, kspec, kspec, qspec, qspec], out_specs=[qspec, kspec, kspec],
            scratch_shapes=[pltpu.VMEM((S, DH), F32)]),
        out_shape=[o, o, jax.ShapeDtypeStruct((S, HW), BF16)],
        compiler_params=_cp(("parallel", "arbitrary")),
    )(walked, sqn, skn, svb, do, tot)


def _mem_probs(qn, kn):
    s = _dot(qn, kn.astype(BF16), NT) * DH ** -0.5
    p = jnp.exp(s - jnp.max(s, axis=-1, keepdims=True))
    return p / jnp.sum(p, axis=-1, keepdims=True)


def _mem_fwd(qmn, kv, gmk, S):
    ts = _row_tile(S)

    def body(q_ref, kv_ref, gk_ref, o_ref):
        for h in range(NH):
            kn, _ = _rms(kv_ref[:, _hs(h)], gk_ref[...])
            p = _mem_probs(q_ref[:, _hs(h)], kn)
            o_ref[:, _hs(h)] = _dbf(p, kv_ref[:, HW + h * DH:HW + (h + 1) * DH]).astype(BF16)

    return pl.pallas_call(
        body, name="mem_fwd", grid=(S // ts,),
        in_specs=[pl.BlockSpec((ts, HW), lambda i: (i, 0)), pl.BlockSpec((NMEM, 2 * HW), lambda i: (0, 0)),
                  pl.BlockSpec((1, DH), lambda i: (0, 0))],
        out_specs=pl.BlockSpec((ts, HW), lambda i: (i, 0)),
        out_shape=jax.ShapeDtypeStruct((S, HW), BF16),
        compiler_params=_cp(("parallel",)),
    )(qmn, kv, gmk)


def _mem_bwd(proj, qmn, kv, gmq, gmk, do, S):
    ts = _row_tile(S)
    n = S // ts

    def body(mq_ref, q_ref, kv_ref, gq_ref, gk_ref, do_ref, dmq_o, dkv_o, dgq_o, dgk_o, dkn_scr):
        i = pl.program_id(0)

        @pl.when(i == 0)
        def _():
            dkv_o[...] = jnp.zeros_like(dkv_o)
            dgq_o[...] = jnp.zeros_like(dgq_o)
            dkn_scr[...] = jnp.zeros_like(dkn_scr)

        dgq = jnp.zeros((1, DH), F32)
        for h in range(NH):
            km = kv_ref[:, _hs(h)]
            vm = kv_ref[:, HW + h * DH:HW + (h + 1) * DH].astype(BF16)
            kn, _ = _rms(km, gk_ref[...])
            qn = q_ref[:, _hs(h)]
            p = _mem_probs(qn, kn)
            dob = do_ref[:, _hs(h)].astype(BF16)
            dkv_o[:, HW + h * DH:HW + (h + 1) * DH] += _dot(p.astype(BF16), dob, TN)
            dp = _dot(dob, vm, NT)
            dsc = (p * (dp - jnp.sum(dp * p, axis=-1, keepdims=True)) * DH ** -0.5).astype(BF16)
            dkn_scr[:, _hs(h)] += _dot(dsc, qn, TN)
            x = mq_ref[:, _hs(h)]
            _, r = _rms(x, gq_ref[...])
            dx, dg = _rms_bwd(_dot(dsc, kn.astype(BF16)), x, gq_ref[...], r)
            dmq_o[:, _hs(h)] = dx.astype(BF16)
            dgq = dgq + dg
        dgq_o[...] += dgq

        @pl.when(i == n - 1)
        def _():
            dgk = jnp.zeros((1, DH), F32)
            for h in range(NH):
                km = kv_ref[:, _hs(h)]
                _, r = _rms(km, gk_ref[...])
                dx, dg = _rms_bwd(dkn_scr[:, _hs(h)], km, gk_ref[...], r)
                dkv_o[:, _hs(h)] = dx
                dgk = dgk + dg
            dgk_o[...] = dgk

    full = lambda r, c: pl.BlockSpec((r, c), lambda i: (0, 0))
    t512 = pl.BlockSpec((ts, HW), lambda i: (i, 0))
    return pl.pallas_call(
        body, name="mem_bwd", grid=(n,),
        in_specs=[pl.BlockSpec((ts, HW), lambda i: (i, CB_MQ)), t512, full(NMEM, 2 * HW), full(1, DH), full(1, DH),
                  t512],
        out_specs=[t512, full(NMEM, 2 * HW), full(1, DH), full(1, DH)],
        out_shape=[jax.ShapeDtypeStruct((S, HW), BF16), jax.ShapeDtypeStruct((NMEM, 2 * HW), F32),
                   jax.ShapeDtypeStruct((1, DH), F32), jax.ShapeDtypeStruct((1, DH), F32)],
        scratch_shapes=[pltpu.VMEM((NMEM, HW), F32)],
        compiler_params=_cp(("arbitrary",)),
    )(proj, qmn, kv, gmq, gmk, do)


def _gated_gdn(o, z, g):
    sg = _sigmoid(z)
    outs, rs = [], []
    for h in range(NH):
        y, r = _rms(o[:, _hs(h)], g)
        outs.append(y * (z[:, _hs(h)] * sg[:, _hs(h)]))
        rs.append(r)
    return jnp.concatenate(outs, axis=1), rs, sg


def _merge_fwd(x, proj, ogdn, osb, omem, ggdn, wbg, wbs, wbm, wo, S):
    ts = _narrow_tile(S)

    def body(x_ref, z_ref, g0_ref, g1_ref, g2_ref, og_ref, os_ref, om_ref, gg_ref, wbg_ref, wbs_ref, wbm_ref,
             wo_ref, x1_o, mix_o):
        on, _, _ = _gated_gdn(og_ref[...], z_ref[...], gg_ref[...])
        mix = (_sigmoid(g0_ref[...]) * _dbf(on, wbg_ref[...]) + _sigmoid(g1_ref[...]) * _dbf(os_ref[...], wbs_ref[...])
               + _sigmoid(g2_ref[...]) * _dbf(om_ref[...], wbm_ref[...]))
        mix_o[...] = mix.astype(BF16)
        x1_o[...] = x_ref[...] + _dbf(mix, wo_ref[...])

    t512 = pl.BlockSpec((ts, HW), lambda i: (i, 0))
    t1k = pl.BlockSpec((ts, D), lambda i: (i, 0))
    gate = lambda j: pl.BlockSpec((ts, D), lambda i: (i, 4 + j))
    full = lambda r, c: pl.BlockSpec((r, c), lambda i: (0, 0))
    return pl.pallas_call(
        body, name="merge_fwd", grid=(S // ts,),
        in_specs=[t1k, pl.BlockSpec((ts, HW), lambda i: (i, CB_Z)), gate(0), gate(1), gate(2), t512, t512, t512,
                  full(1, DH), full(HW, D), full(HW, D), full(HW, D), full(D, D)],
        out_specs=[t1k, t1k],
        out_shape=[jax.ShapeDtypeStruct((S, D), F32), jax.ShapeDtypeStruct((S, D), BF16)],
        compiler_params=_cp(("parallel",)),
    )(x, proj, proj, proj, proj, ogdn, osb, omem, ggdn, wbg, wbs, wbm, wo)


def _merge_bwd(dmix, proj, ogdn, osb, omem, ggdn, wbg, wbs, wbm, S):
    ts = _narrow_tile(S)

    def body(dm_ref, z_ref, g0_ref, g1_ref, g2_ref, og_ref, os_ref, om_ref, gg_ref, wbg_ref, wbs_ref, wbm_ref,
             dgl0_o, dgl1_o, dgl2_o, dog_o, dz_o, dos_o, dom_o, dwbg_o, dwbs_o, dwbm_o, dgg_o):
        @pl.when(pl.program_id(0) == 0)
        def _():
            for ref in (dwbg_o, dwbs_o, dwbm_o, dgg_o):
                ref[...] = jnp.zeros_like(ref)

        dm = dm_ref[...]
        og = og_ref[...]
        z = z_ref[...]
        on, rs, sg = _gated_gdn(og, z, gg_ref[...])
        branch = ((on, g0_ref, wbg_ref, dgl0_o, dwbg_o), (os_ref[...], g1_ref, wbs_ref, dgl1_o, dwbs_o),
                  (om_ref[...], g2_ref, wbm_ref, dgl2_o, dwbm_o))
        dos = []
        for o, g_ref, w_ref, dgl_o, dw_o in branch:
            ob = o.astype(BF16)
            gate = _sigmoid(g_ref[...])
            dgl_o[...] = (dm * _dot(ob, w_ref[...]) * gate * (1.0 - gate)).astype(BF16)
            dy = (dm * gate).astype(BF16)
            dw_o[...] += _dot(ob, dy, TN)
            dos.append(_dot(dy, w_ref[...], NT))
        dos_o[...] = dos[1].astype(BF16)
        dom_o[...] = dos[2].astype(BF16)
        don = dos[0]
        dgg = jnp.zeros((1, DH), F32)
        for h in range(NH):
            oh, zh, sh = og[:, _hs(h)], z[:, _hs(h)], sg[:, _hs(h)]
            y = oh * rs[h] * gg_ref[...]
            dz_o[:, _hs(h)] = (don[:, _hs(h)] * y * (sh * (1.0 + zh * (1.0 - sh)))).astype(BF16)
            dx, dg = _rms_bwd(don[:, _hs(h)] * (zh * sh), oh, gg_ref[...], rs[h])
            dog_o[:, _hs(h)] = dx
            dgg = dgg + dg
        dgg_o[...] += dgg

    t512 = pl.BlockSpec((ts, HW), lambda i: (i, 0))
    t1k = pl.BlockSpec((ts, D), lambda i: (i, 0))
    gate = lambda j: pl.BlockSpec((ts, D), lambda i: (i, 4 + j))
    full = lambda r, c: pl.BlockSpec((r, c), lambda i: (0, 0))
    s1k = jax.ShapeDtypeStruct((S, D), BF16)
    s512 = jax.ShapeDtypeStruct((S, HW), BF16)
    wsh = jax.ShapeDtypeStruct((HW, D), F32)
    return pl.pallas_call(
        body, name="merge_bwd", grid=(S // ts,),
        in_specs=[t1k, pl.BlockSpec((ts, HW), lambda i: (i, CB_Z)), gate(0), gate(1), gate(2), t512, t512, t512,
                  full(1, DH), full(HW, D), full(HW, D), full(HW, D)],
        out_specs=[t1k, t1k, t1k, t512, t512, t512, t512, full(HW, D), full(HW, D), full(HW, D), full(1, DH)],
        out_shape=[s1k, s1k, s1k, jax.ShapeDtypeStruct((S, HW), F32), s512, s512, s512, wsh, wsh, wsh,
                   jax.ShapeDtypeStruct((1, DH), F32)],
        compiler_params=_cp(("arbitrary",)),
    )(dmix, proj, proj, proj, proj, ogdn, osb, omem, ggdn, wbg, wbs, wbm)


def _norm_cast(name, x, g):
    rows = x.shape[0]
    ts = min(_row_tile(rows), rows)

    def body(x_ref, g_ref, o_ref):
        o_ref[...] = _rms(x_ref[...], g_ref[...])[0].astype(BF16)

    t1k = pl.BlockSpec((ts, D), lambda i: (i, 0))
    return pl.pallas_call(
        body, name=name, grid=(rows // ts,), in_specs=[t1k, pl.BlockSpec((1, D), lambda i: (0, 0))], out_specs=t1k,
        out_shape=jax.ShapeDtypeStruct((rows, D), BF16), compiler_params=_cp(("parallel",)),
    )(x, g)


def _norm_bwd(name, dh, x, g, res):
    rows = x.shape[0]
    ts = min(_row_tile(rows), rows)

    def body(*refs):
        dh_ref, x_ref, g_ref = refs[:3]
        dx_o, dg_o = refs[-2:]

        @pl.when(pl.program_id(0) == 0)
        def _():
            dg_o[...] = jnp.zeros_like(dg_o)

        xv = x_ref[...]
        _, r = _rms(xv, g_ref[...])
        dx, dg = _rms_bwd(dh_ref[...], xv, g_ref[...], r)
        dx_o[...] = dx if res is None else dx + refs[3][...]
        dg_o[...] += dg

    t1k = pl.BlockSpec((ts, D), lambda i: (i, 0))
    gsp = pl.BlockSpec((1, D), lambda i: (0, 0))
    ops = [dh, x, g] + ([] if res is None else [res])
    return pl.pallas_call(
        body, name=name, grid=(rows // ts,), in_specs=[t1k, t1k, gsp] + ([] if res is None else [t1k]),
        out_specs=[t1k, gsp],
        out_shape=[jax.ShapeDtypeStruct((rows, D), F32), jax.ShapeDtypeStruct((1, D), F32)],
        compiler_params=_cp(("arbitrary",)),
    )(*ops)


def _adamw(name, gall, w, m, v):
    rows = w.shape[0]
    nsrc = gall.shape[0]
    tr = min(SLAB_TILE, rows)
    assert rows % tr == 0

    def body(g_ref, w_ref, m_ref, v_ref, g_o, d_o, m_o, v_o):
        g = g_ref[0].astype(F32)
        for j in range(1, nsrc):
            g = g + g_ref[j].astype(F32)
        m_new = ADAM_B1 * m_ref[...] + (1.0 - ADAM_B1) * g
        v_new = ADAM_B2 * v_ref[...] + (1.0 - ADAM_B2) * jnp.square(g)
        m_hat = m_new / (1.0 - ADAM_B1 ** ADAM_STEP)
        v_hat = v_new / (1.0 - ADAM_B2 ** ADAM_STEP)
        g_o[...] = g
        d_o[...] = -ADAM_LR * (m_hat / (jnp.sqrt(v_hat) + ADAM_EPS) + ADAM_WD * w_ref[...])
        m_o[...] = m_new
        v_o[...] = v_new

    t = pl.BlockSpec((tr, LANES), lambda i: (i, 0))
    o = jax.ShapeDtypeStruct((rows, LANES), F32)
    return pl.pallas_call(
        body, name=name, grid=(rows // tr,),
        in_specs=[pl.BlockSpec((nsrc, tr, LANES), lambda i: (0, i, 0)), t, t, t],
        out_specs=[t, t, t, t], out_shape=[o, o, o, o],
        compiler_params=_cp(("parallel",)),
    )(gall, w, m, v)


def _pair_sum(name, mine, theirs):
    rows = mine.shape[1]
    tr = min(SLAB_TILE, rows)
    assert rows % tr == 0
    core = lax.axis_index("c").astype(jnp.int32).reshape(1)

    def body(c_ref, a_ref, b_ref, o_ref):
        o_ref[...] = (a_ref[...].astype(F32) + b_ref[...].astype(F32)).astype(o_ref.dtype)

    blk = pl.BlockSpec((1, tr, LANES), lambda j, i, c_ref: (j, i, 0))
    return pl.pallas_call(
        body, name=name,
        grid_spec=pltpu.PrefetchScalarGridSpec(
            num_scalar_prefetch=1, grid=(NDEV // 2, rows // tr),
            in_specs=[pl.BlockSpec((1, tr, LANES), lambda j, i, c_ref: (2 * j + c_ref[0], i, 0)), blk],
            out_specs=blk),
        out_shape=jax.ShapeDtypeStruct((NDEV // 2, rows, LANES), mine.dtype),
        compiler_params=_cp(("parallel", "parallel")),
    )(core, mine, theirs)


HBM_SPEC = pl.BlockSpec(memory_space=pltpu.HBM)


def _remote(src, dst, send_sems, recv_sems, k, to):
    return pltpu.make_async_remote_copy(src_ref=src, dst_ref=dst, send_sem=send_sems.at[k], recv_sem=recv_sems.at[k],
                                        device_id=to, device_id_type=pl.DeviceIdType.MESH)


def _gather_steps(x_ref, o_ref, send_sems, recv_sems, local_sem):
    ix, iy, ic = lax.axis_index("x"), lax.axis_index("y"), lax.axis_index("c")
    me, sibling = (ix, iy, ic), (ix, iy, 1 - ic)
    chips = [(1 - ix, iy), (ix, 1 - iy), (1 - ix, 1 - iy)]

    def slab(px, py, pc):
        return o_ref.at[4 * px + 2 * py + pc]

    def copy(k, block, to, src=None):
        return _remote(slab(*block) if src is None else src, slab(*block), send_sems, recv_sems, k, to)

    def mine():
        return pltpu.make_async_copy(x_ref, slab(*me), local_sem)

    def first():
        return [copy(0, me, sibling, src=x_ref)] + [copy(1 + j, me, (*chip, ic), src=x_ref)
                                                    for j, chip in enumerate(chips)]

    def passed():
        return [copy(4 + j, (*chip, ic), sibling) for j, chip in enumerate(chips)]

    def start():
        mine().start()
        for cp in first():
            cp.start()

    def forward():
        for j, (chip, cp) in enumerate(zip(chips, passed())):
            copy(1 + j, (*chip, ic), me).wait_recv()
            cp.start()

    def finish():
        copy(0, sibling, me).wait_recv()
        for j, chip in enumerate(chips):
            copy(4 + j, (*chip, 1 - ic), me).wait_recv()
        for cp in first() + passed():
            cp.wait_send()
        mine().wait()

    return start, forward, finish


GATHER_SEMS = [pltpu.SemaphoreType.DMA((NDEV - 1,)), pltpu.SemaphoreType.DMA((NDEV - 1,)), pltpu.SemaphoreType.DMA]


def _gather(name, x):
    rows, cols = x.shape

    def body(x_ref, o_ref, send_sems, recv_sems, local_sem):
        for step in _gather_steps(x_ref, o_ref, send_sems, recv_sems, local_sem):
            step()

    return pl.pallas_call(
        body, name=name, in_specs=[HBM_SPEC], out_specs=HBM_SPEC,
        out_shape=jax.ShapeDtypeStruct((NDEV, rows, cols), x.dtype), scratch_shapes=list(GATHER_SEMS),
    )(x)


def _sibling_exchange(name, x):
    rows, cols = x.shape[-2:]
    nchip = NDEV // 2

    def body(x_ref, o_ref, send_sems, recv_sems):
        ix, iy, ic = lax.axis_index("x"), lax.axis_index("y"), lax.axis_index("c")
        copies = [_remote(x_ref.at[2 * j + (1 - ic)], o_ref.at[j], send_sems, recv_sems, j, (ix, iy, 1 - ic))
                  for j in range(nchip)]
        for cp in copies:
            cp.start()
        for cp in copies:
            cp.wait()

    return pl.pallas_call(
        body, name=name, in_specs=[HBM_SPEC], out_specs=HBM_SPEC,
        out_shape=jax.ShapeDtypeStruct((nchip, rows, cols), x.dtype),
        scratch_shapes=[pltpu.SemaphoreType.DMA((nchip,)), pltpu.SemaphoreType.DMA((nchip,))],
    )(x)


def _chip_steps(x_ref, o_ref, send_sems, recv_sems, local_sem):
    ix, iy, ic = lax.axis_index("x"), lax.axis_index("y"), lax.axis_index("c")
    my_chip = 2 * ix + iy

    def own():
        return pltpu.make_async_copy(x_ref.at[my_chip], o_ref.at[my_chip], local_sem)

    def copies():
        out = []
        for k in range(1, NDEV // 2):
            px, py = ix ^ (k >> 1), iy ^ (k & 1)
            out.append(_remote(x_ref.at[2 * px + py], o_ref.at[my_chip], send_sems, recv_sems, k - 1, (px, py, ic)))
        return out

    def start():
        own().start()
        for cp in copies():
            cp.start()

    def finish():
        for cp in copies():
            cp.wait()
        own().wait()

    return start, (lambda: None), finish


CHIP_SEMS = [pltpu.SemaphoreType.DMA((NDEV // 2 - 1,)), pltpu.SemaphoreType.DMA((NDEV // 2 - 1,)),
             pltpu.SemaphoreType.DMA]


COL_SHARDED = {"w_in": (D, D_IN), "w_br_gdn": (HW, D), "w_br_sb": (HW, D), "w_br_mem": (HW, D), "w_up": (D, DFF),
               "conv_w": (4, 3 * HW)}
ROW_SHARDED = {"w_mem_kv": (D, 2 * HW), "w_o": (D, D), "w_down": (DFF, D)}


def _to_slab(p):
    return p.reshape(p.shape[:-2] + (-1, LANES))


def _from_slab(flat, r, c):
    return flat.reshape(flat.shape[:-2] + (r, c))


def _shard_dims(name):
    if name in COL_SHARDED:
        r, c = COL_SHARDED[name]
        return r, c // NDEV
    r, c = ROW_SHARDED[name]
    return r // NDEV, c


def _pack_rows(parts, total):
    flat = jnp.concatenate(parts, axis=-2)
    return jnp.pad(flat, [(0, 0)] * (flat.ndim - 2) + [(0, total - flat.shape[-2]), (0, 0)])


def _pack_shards(vals, names, total):
    return _pack_rows([_to_slab(vals[n][0]) for n in names], total)


def _pack_full_grads(grads, names, total):
    parts = []
    for name in names:
        g = grads[name]
        r, c = _shard_dims(name)
        if name in COL_SHARDED:
            g = g.reshape(r, NDEV, c).transpose(1, 0, 2)
        else:
            g = g.reshape(NDEV, r, c)
        parts.append(_to_slab(g))
    return _pack_rows(parts, total)


def _unpack_gathered(slabs, names):
    out, pos = {}, 0
    for name in names:
        rows = SLAB_ROWS[name]
        r, c = _shard_dims(name)
        g = _from_slab(slabs[:, pos:pos + rows], r, c)
        pos += rows
        if name in COL_SHARDED:
            out[name] = g.transpose(1, 0, 2).reshape(r, NDEV * c)
        else:
            out[name] = g.reshape(NDEV * r, c)
    return out


def _unpack_shard(flat, names, shapes):
    out, pos = {}, 0
    for name in names:
        rows = SLAB_ROWS[name]
        r, c = _shard_dims(name)
        out[name] = _from_slab(flat[pos:pos + rows], r, c).reshape(shapes[name])
        pos += rows
    return out


def _pack_vec(vals):
    row = jnp.concatenate([vals[n] for n in VEC], axis=1)
    return jnp.pad(row, ((0, 0), (0, VEC_WIDTH - row.shape[1])))


def _adamw_vec(gall, w, m, v):
    aligned = [(off, n) for off, n in zip(VEC_OFFSETS, VEC_SIZES) if n % LANES == 0]

    def body(g_ref, w_ref, m_ref, v_ref, *outs):
        g = g_ref[0]
        for j in range(1, NDEV):
            g = g + g_ref[j]
        m_new = ADAM_B1 * m_ref[...] + (1.0 - ADAM_B1) * g
        v_new = ADAM_B2 * v_ref[...] + (1.0 - ADAM_B2) * jnp.square(g)
        m_hat = m_new / (1.0 - ADAM_B1 ** ADAM_STEP)
        v_hat = v_new / (1.0 - ADAM_B2 ** ADAM_STEP)
        delta = -ADAM_LR * (m_hat / (jnp.sqrt(v_hat) + ADAM_EPS) + ADAM_WD * w_ref[...])
        for r, val in enumerate((g, delta, m_new, v_new)):
            outs[r][...] = val
            for i, (off, n) in enumerate(aligned):
                outs[4 + r * len(aligned) + i][...] = val[:, off:off + n]

    full = lambda *shape: pl.BlockSpec(shape, lambda: (0,) * len(shape))
    row = jax.ShapeDtypeStruct((1, VEC_WIDTH), F32)
    out_shape = [row] * 4 + [jax.ShapeDtypeStruct((1, n), F32) for _ in range(4) for _, n in aligned]
    out_specs = [full(1, VEC_WIDTH)] * 4 + [full(1, n) for _ in range(4) for _, n in aligned]
    return pl.pallas_call(
        body, name="adamw_replicated",
        in_specs=[full(NDEV, 1, VEC_WIDTH), full(1, VEC_WIDTH), full(1, VEC_WIDTH), full(1, VEC_WIDTH)],
        out_specs=out_specs, out_shape=out_shape,
    )(gall, w, m, v)


def _unpack_vec(outs, r):
    aligned = [name for name, n in zip(VEC, VEC_SIZES) if n % LANES == 0]
    vals = {name: outs[4 + r * len(aligned) + i] for i, name in enumerate(aligned)}
    for name, off, n in zip(VEC, VEC_OFFSETS, VEC_SIZES):
        if name not in vals:
            vals[name] = outs[r][:, off:off + n]
    return vals


def _pad_w_in(w):
    return jnp.concatenate([w[:, :2048], w[:, 2056:], w[:, 2048:2056], jnp.zeros((D, D_INP - D_IN), w.dtype)], axis=1)


def _unpad_w_in(w):
    return jnp.concatenate([w[:, :2048], w[:, 7168:7176], w[:, 2048:7168]], axis=1)


def _per_head(v):
    return jnp.repeat(v.reshape(NH), DH).reshape(1, HW)


def _local_step(x, mem, target, w, sm, rest_shards):
    S = x.shape[0]
    ts = _row_tile(S)
    alog_f, dtb_f = _per_head(sm["a_log"]), _per_head(sm["dt_bias"])
    w = dict(w)

    h1 = _norm_cast("norm1", x, sm["norm1_g"])
    proj, rest = _mm("in_proj", h1, w["w_in"], "nn", ts, 1536, D, n_outer=True, comm=("gather", rest_shards))
    w.update(_unpack_gathered(rest[:, :sum(SLAB_ROWS[n] for n in REST)], REST))
    gq, gk, gv, gf, bf, sqn, skn, svb, qmn = _pre_fwd(proj, w["conv_w"], alog_f, dtb_f, sm["sb_q_norm_g"],
                                                      sm["sb_k_norm_g"], sm["mem_q_norm_g"], S)
    ogdn, states = _gdn_fwd(gq, gk, gv, gf, bf, S)
    osb, sb_tot, sb_walked = _sb_fwd(sqn, skn, svb, S)
    kv = _mm("mem_kv", mem, w["w_mem_kv"], "nn", NMEM, D, D, pro="rms", pro_g=sm["mem_norm_g"])
    omem = _mem_fwd(qmn, kv, sm["mem_k_norm_g"], S)
    x1, mix = _merge_fwd(x, proj, ogdn, osb, omem, sm["gdn_norm_g"], w["w_br_gdn"], w["w_br_sb"], w["w_br_mem"],
                         w["w_o"], S)
    h2 = _norm_cast("norm2", x1, sm["norm2_g"])
    up = _mm("mlp_up", h2, w["w_up"], "nn", ts, 2048, D, n_outer=True)
    dy, loss = _mm("mlp_down", up, w["w_down"], "nn", ts, D, 1024, pro="relu2", epi="loss", epi_x=(x1, target))

    g = {}
    dup = _mm("d_up", dy, w["w_down"], "nt", ts, 1024, D, epi="drelu2", epi_x=up, out_dtype=BF16)
    g["w_down"] = _mm("dw_down", up, dy, "tn", 1024, D, 512, pro="relu2")
    g["w_up"] = _mm("dw_up", h2, dup, "tn", D, 1024, 512)
    dx1, g["norm2_g"] = _mm("d_h2", dup, w["w_up"], "nt", ts, D, 1024, epi="rms_bwd", epi_x=(x1, sm["norm2_g"], dy))

    dmix = _mm("d_mix", dx1, w["w_o"], "nt", ts, D, D)
    g["w_o"] = _mm("dw_o", mix, dx1, "tn", D, D, 512)
    (dgl0, dgl1, dgl2, dogdn, dz, dosb, domem, g["w_br_gdn"], g["w_br_sb"], g["w_br_mem"],
     g["gdn_norm_g"]) = _merge_bwd(dmix, proj, ogdn, osb, omem, sm["gdn_norm_g"], w["w_br_gdn"], w["w_br_sb"],
                                   w["w_br_mem"], S)
    dmq, dkv, g["mem_q_norm_g"], g["mem_k_norm_g"] = _mem_bwd(proj, qmn, kv, sm["mem_q_norm_g"], sm["mem_k_norm_g"],
                                                             domem, S)
    g["w_mem_kv"] = _mm("dw_mem_kv", mem, dkv, "tn", D, D, NMEM, pro="rms", pro_g=sm["mem_norm_g"])
    dmn = _mm("d_mem_n", dkv, w["w_mem_kv"], "nt", NMEM, D, D)
    _, g["mem_norm_g"] = _norm_bwd("mem_norm_bwd", dmn, mem, sm["mem_norm_g"], None)
    dsqn, dskn, dsv = _sb_bwd(sqn, skn, svb, dosb, sb_tot, sb_walked, S)
    dgq, dgk, dgv, dgf, dbf = _gdn_bwd(gq, gk, gv, gf, bf, states, dogdn, S)
    dc, dab, dsq, dsk, g["conv_w"], dal_f, ddt_f, g["sb_q_norm_g"], g["sb_k_norm_g"] = _pre_bwd(
        proj, w["conv_w"], alog_f, dtb_f, sm["sb_q_norm_g"], sm["sb_k_norm_g"], dgq, dgk, dgv, dgf, dbf, dsqn, dskn, S)
    g["a_log"] = dal_f.reshape(NH, DH)[:, 0].reshape(1, NH)
    g["dt_bias"] = ddt_f.reshape(NH, DH)[:, 0].reshape(1, NH)
    dqkv = _conv_bwd(dc, w["conv_w"], S)

    dproj = jnp.concatenate([dqkv, dz, dsq, dsk, dsv, dmq, dgl0, dgl1, dgl2, dab], axis=1)
    rest_mine = _pack_full_grads(g, REST, R_REST).astype(BF16)
    rest_pair = _pair_sum("pair_sum_rest", rest_mine, _sibling_exchange("scatter_sibling_rest", rest_mine))
    g["w_in"], rest_all = _mm("dw_in", h1, dproj, "tn", D, 1536, 512, comm=("chips", rest_pair))
    g["w_in"] = _unpad_w_in(g["w_in"])
    first_mine = _pack_full_grads(g, FIRST, R_FIRST).astype(BF16)
    first_pair = _pair_sum("pair_sum_first", first_mine, _sibling_exchange("scatter_sibling_first", first_mine))
    dx, g["norm1_g"], first_all = _mm("d_h", dproj, w["w_in"], "nt", ts, D, 1536, epi="rms_bwd",
                                      epi_x=(x, sm["norm1_g"], dx1), comm=("chips", first_pair))
    return loss[0, 0], dx, g, rest_all, first_all


def kernel(x, mem, norm1_g, w_in, conv_w, a_log, dt_bias, gdn_norm_g, sb_q_norm_g, sb_k_norm_g, mem_norm_g, w_mem_kv, mem_q_norm_g, mem_k_norm_g, w_br_gdn, w_br_sb, w_br_mem, w_o, norm2_g, w_up, w_down, loss_target, m_norm1_g, m_w_in, m_conv_w, m_a_log, m_dt_bias, m_gdn_norm_g, m_sb_q_norm_g, m_sb_k_norm_g, m_mem_norm_g, m_w_mem_kv, m_mem_q_norm_g, m_mem_k_norm_g, m_w_br_gdn, m_w_br_sb, m_w_br_mem, m_w_o, m_norm2_g, m_w_up, m_w_down, v_norm1_g, v_w_in, v_conv_w, v_a_log, v_dt_bias, v_gdn_norm_g, v_sb_q_norm_g, v_sb_k_norm_g, v_mem_norm_g, v_w_mem_kv, v_mem_q_norm_g, v_mem_k_norm_g, v_w_br_gdn, v_w_br_sb, v_w_br_mem, v_w_o, v_norm2_g, v_w_up, v_w_down):
    given = dict(norm1_g=norm1_g, w_in=w_in, conv_w=conv_w, a_log=a_log, dt_bias=dt_bias, gdn_norm_g=gdn_norm_g,
                 sb_q_norm_g=sb_q_norm_g, sb_k_norm_g=sb_k_norm_g, mem_norm_g=mem_norm_g, w_mem_kv=w_mem_kv,
                 mem_q_norm_g=mem_q_norm_g, mem_k_norm_g=mem_k_norm_g, w_br_gdn=w_br_gdn, w_br_sb=w_br_sb,
                 w_br_mem=w_br_mem, w_o=w_o, norm2_g=norm2_g, w_up=w_up, w_down=w_down)
    mom1 = dict(norm1_g=m_norm1_g, w_in=m_w_in, conv_w=m_conv_w, a_log=m_a_log, dt_bias=m_dt_bias,
                gdn_norm_g=m_gdn_norm_g, sb_q_norm_g=m_sb_q_norm_g, sb_k_norm_g=m_sb_k_norm_g,
                mem_norm_g=m_mem_norm_g, w_mem_kv=m_w_mem_kv, mem_q_norm_g=m_mem_q_norm_g,
                mem_k_norm_g=m_mem_k_norm_g, w_br_gdn=m_w_br_gdn, w_br_sb=m_w_br_sb, w_br_mem=m_w_br_mem, w_o=m_w_o,
                norm2_g=m_norm2_g, w_up=m_w_up, w_down=m_w_down)
    mom2 = dict(norm1_g=v_norm1_g, w_in=v_w_in, conv_w=v_conv_w, a_log=v_a_log, dt_bias=v_dt_bias,
                gdn_norm_g=v_gdn_norm_g, sb_q_norm_g=v_sb_q_norm_g, sb_k_norm_g=v_sb_k_norm_g,
                mem_norm_g=v_mem_norm_g, w_mem_kv=v_w_mem_kv, mem_q_norm_g=v_mem_q_norm_g,
                mem_k_norm_g=v_mem_k_norm_g, w_br_gdn=v_w_br_gdn, w_br_sb=v_w_br_sb, w_br_mem=v_w_br_mem, w_o=v_w_o,
                norm2_g=v_norm2_g, w_up=v_w_up, w_down=v_w_down)
    shapes = {n: given[n].shape for n in WEIGHTS}

    first_loc = _pack_shards(given, FIRST, R_FIRST)
    rest_loc = _pack_shards(given, REST, R_REST)
    gathered = _gather("gather_first", first_loc.astype(BF16))
    w = _unpack_gathered(gathered[:, :sum(SLAB_ROWS[n] for n in FIRST)], FIRST)
    w["w_in"] = _pad_w_in(w["w_in"])
    conv_loc = jnp.pad(given["conv_w"][0].reshape(-1, LANES), ((0, 2), (0, 0)))
    conv_all = _gather("gather_conv", conv_loc)
    w["conv_w"] = conv_all[:, :6].reshape(NDEV, 4, 3 * HW // NDEV).transpose(1, 0, 2).reshape(4, 3 * HW)
    sm = {n: given[n] for n in SMALL}

    loss, dx, g, rest_all, first_all = _local_step(x[0], mem[0], loss_target[0], w, sm, rest_loc.astype(BF16))
    res_first = _adamw("adamw_first", first_all, first_loc, _pack_shards(mom1, FIRST, R_FIRST),
                       _pack_shards(mom2, FIRST, R_FIRST))
    res_rest = _adamw("adamw_rest", rest_all, rest_loc, _pack_shards(mom1, REST, R_REST),
                      _pack_shards(mom2, REST, R_REST))
    gs_all = _gather("gather_small_grads", _pack_vec(g))
    vec_outs = _adamw_vec(gs_all, _pack_vec(given), _pack_vec(mom1), _pack_vec(mom2))

    outs = {}
    for r, prefix in enumerate(("grad_", "delta_", "new_m_", "new_v_")):
        vals = _unpack_shard(res_first[r], FIRST, shapes)
        vals.update(_unpack_shard(res_rest[r], REST, shapes))
        vals.update(_unpack_vec(vec_outs, r))
        for n in WEIGHTS:
            outs[prefix + n] = vals[n]
    loss = lax.psum(loss, ("x", "y", "c"))
    return (loss, dx[None], *[outs[p + n] for p in ("grad_", "delta_", "new_m_", "new_v_") for n in WEIGHTS])
```

```python
import jax
import jax.numpy as jnp
from jax import lax
from jax.experimental import pallas as pl
from jax.experimental.pallas import tpu as pltpu

F32 = jnp.float32
BF16 = jnp.bfloat16

D = 1024
NH = 4
DH = 128
HW = NH * DH
DFF = 4 * D
NMEM = 256
EPS = 1e-6
NDEV = 8
LANES = 128
PAIR = 128
CHUNK = 64
D_IN = 7176
D_INP = 7680
VMEM_LIMIT = 56 * 1024 * 1024

ADAM_LR, ADAM_B1, ADAM_B2, ADAM_EPS, ADAM_WD, ADAM_STEP = 0.001, 0.9, 0.999, 1e-08, 0.01, 10

CB_Z, CB_SQ, CB_SK, CB_SV, CB_MQ, CB_AB = 3, 4, 5, 6, 7, 14

NN = (((1,), (0,)), ((), ()))
NT = (((1,), (1,)), ((), ()))
TN = (((0,), (0,)), ((), ()))

BIG = ("w_in", "w_mem_kv", "w_br_gdn", "w_br_sb", "w_br_mem", "w_o", "w_up", "w_down", "conv_w")
BIG_ROWS = (7176, 1024, 512, 512, 512, 1024, 4096, 4096, 6)
SLAB_ROWS = dict(zip(BIG, BIG_ROWS))
SLAB_TILE = 1216
FIRST = ("w_in", "conv_w")
REST = ("w_mem_kv", "w_br_gdn", "w_br_sb", "w_br_mem", "w_o", "w_up", "w_down")
R_FIRST = 6 * SLAB_TILE
R_REST = 10 * SLAB_TILE
SMALL = ("norm1_g", "a_log", "dt_bias", "gdn_norm_g", "sb_q_norm_g", "sb_k_norm_g", "mem_norm_g",
         "mem_q_norm_g", "mem_k_norm_g", "norm2_g")
VEC = ("norm1_g", "mem_norm_g", "norm2_g", "gdn_norm_g", "sb_q_norm_g", "sb_k_norm_g", "mem_q_norm_g", "mem_k_norm_g",
       "a_log", "dt_bias")
VEC_SIZES = (1024, 1024, 1024, 128, 128, 128, 128, 128, 4, 4)
VEC_OFFSETS = (0, 1024, 2048, 3072, 3200, 3328, 3456, 3584, 3712, 3716)
VEC_WIDTH = 3840
WEIGHTS = ("norm1_g", "w_in", "conv_w", "a_log", "dt_bias", "gdn_norm_g", "sb_q_norm_g", "sb_k_norm_g",
           "mem_norm_g", "w_mem_kv", "mem_q_norm_g", "mem_k_norm_g", "w_br_gdn", "w_br_sb", "w_br_mem",
           "w_o", "norm2_g", "w_up", "w_down")


def _cp(sem=None):
    return pltpu.CompilerParams(dimension_semantics=sem, vmem_limit_bytes=VMEM_LIMIT)


def _dot(a, b, dims=NN):
    return lax.dot_general(a, b, dims, preferred_element_type=F32)


def _dbf(a, b, dims=NN):
    return _dot(a.astype(BF16), b.astype(BF16), dims)


def _split(a, n):
    parts = []
    for _ in range(n):
        h = a.astype(BF16)
        parts.append(h)
        a = a - h.astype(F32)
    return parts


def _dg(a, b, dims=NN):
    return _dbf(a, b, dims)


def _dxr(a, e, dims=NN):
    eb = e.astype(BF16)
    a1, a2, a3 = _split(a, 3)
    return _dot(a1, eb, dims) + (_dot(a2, eb, dims) + _dot(a3, eb, dims))


def _dxl(e, a, dims=NN):
    eb = e.astype(BF16)
    a1, a2, a3 = _split(a, 3)
    return _dot(eb, a1, dims) + (_dot(eb, a2, dims) + _dot(eb, a3, dims))


def _sigmoid(x):
    return 1.0 / (1.0 + jnp.exp(-x))


def _softplus(x):
    return jnp.maximum(x, 0.0) + jnp.log(1.0 + jnp.exp(-jnp.abs(x)))


def _rms(x, g):
    r = lax.rsqrt(jnp.mean(x * x, axis=-1, keepdims=True) + EPS)
    return x * r * g, r


def _rms_bwd(dy, x, g, r):
    dyg = dy * g
    dx = r * (dyg - x * (r * r) * jnp.mean(dyg * x, axis=-1, keepdims=True))
    dg = jnp.sum(dy * (x * r), axis=0, keepdims=True)
    return dx, dg


def _hs(h):
    return slice(h * DH, (h + 1) * DH)


def _row_tile(s):
    return 512 if s >= 2048 else 256


def _narrow_tile(s):
    return min(256, s)


def _mm(name, a, b, mode, tm, tn, tk, pro=None, pro_g=None, epi=None, epi_x=None, out_dtype=F32, n_outer=False,
        comm=None):
    if mode == "tn":
        K, M = a.shape
    else:
        M, K = a.shape
    N = b.shape[0] if mode == "nt" else b.shape[1]
    tm, tn, tk = min(tm, M), min(tn, N), min(tk, K)
    nk = K // tk
    assert M % tm == 0 and N % tn == 0 and K % tk == 0, (name, M, N, K, tm, tn, tk)
    dims = {"nn": NN, "nt": NT, "tn": TN}[mode]
    reducing = epi in ("rms_bwd", "loss")
    assert not reducing or (tn == N and not n_outer), name
    epi_ops = () if epi is None else (epi_x if isinstance(epi_x, tuple) else (epi_x,))

    def body(*refs):
        a_ref, b_ref = refs[0], refs[1]
        pos = 2
        g_ref = None
        if pro == "rms":
            g_ref = refs[pos]
            pos += 1
        e_refs = refs[pos:pos + len(epi_ops)]
        pos += len(epi_ops)
        cx_ref = None
        if comm is not None:
            cx_ref = refs[pos]
            pos += 1
        o_ref = refs[pos]
        pos += 1
        r_ref = None
        if reducing:
            r_ref = refs[pos]
            pos += 1
        if comm is not None:
            steps_of = _gather_steps if comm[0] == "gather" else _chip_steps
            start, forward, finish_comm = steps_of(cx_ref, refs[pos], *refs[-3:])
            pos += 1
            step = (pl.program_id(0) * grid[1] + pl.program_id(1)) * nk + pl.program_id(2)
            total = grid[0] * grid[1] * nk
            pl.when(step == 0)(start)
            pl.when(step == (4 * total) // 5)(forward)
        av = a_ref[...]
        if pro == "rms":
            av, _ = _rms(av.astype(F32), g_ref[...])
        elif pro == "relu2":
            av = jnp.square(jnp.maximum(av, 0.0))
        part = _dbf(av, b_ref[...], dims)
        first = pl.program_id(0) == 0

        def finish(acc):
            red = None
            if epi == "add":
                acc = acc + e_refs[0][...]
            elif epi == "drelu2":
                acc = acc * (2.0 * jnp.maximum(e_refs[0][...], 0.0))
            elif epi == "rms_bwd":
                xv, gv = e_refs[0][...], e_refs[1][...]
                _, r = _rms(xv, gv)
                dx, red = _rms_bwd(acc, xv, gv, r)
                acc = dx + e_refs[2][...]
            elif epi == "loss":
                err = acc + e_refs[0][...] - e_refs[1][...]
                acc = err * (1.0 / N)
                per_tok = jnp.sum(err * err, axis=1, keepdims=True) * (1.0 / N)
                red = 0.5 * jnp.sum(per_tok, axis=0, keepdims=True)
            o_ref[...] = acc.astype(out_dtype)
            if reducing:

                @pl.when(first)
                def _():
                    r_ref[...] = red

                @pl.when(jnp.logical_not(first))
                def _():
                    r_ref[...] += red

        if nk == 1:
            finish(part)
        else:
            acc_ref = refs[pos]
            k = pl.program_id(2)

            @pl.when(k == 0)
            def _():
                acc_ref[...] = part

            @pl.when(k > 0)
            def _():
                acc_ref[...] += part

            @pl.when(k == nk - 1)
            def _():
                finish(acc_ref[...])

        if comm is not None:
            pl.when(step == total - 1)(finish_comm)

    def spec(shape, index):
        if n_outer:
            return pl.BlockSpec(shape, lambda j, i, k: index(i, j, k))
        return pl.BlockSpec(shape, index)

    if mode == "tn":
        a_spec = spec((tk, tm), lambda i, j, k: (k, i))
    else:
        a_spec = spec((tm, tk), lambda i, j, k: (i, k))
    if mode == "nt":
        b_spec = spec((tn, tk), lambda i, j, k: (j, k))
    else:
        b_spec = spec((tk, tn), lambda i, j, k: (k, j))
    in_specs, ops = [a_spec, b_spec], [a, b]
    if pro == "rms":
        w = pro_g.shape[1]
        assert (tm if mode == "tn" else tk) == w, name
        in_specs.append(spec((1, w), lambda i, j, k: (0, 0)))
        ops.append(pro_g)
    for op in epi_ops:
        if op.shape[0] == 1:
            in_specs.append(spec((1, tn), lambda i, j, k: (0, j)))
        else:
            in_specs.append(spec((tm, tn), lambda i, j, k: (i, j)))
        ops.append(op)
    out_specs = [spec((tm, tn), lambda i, j, k: (i, j))]
    out_shape = [jax.ShapeDtypeStruct((M, N), out_dtype)]
    if reducing:
        width = N if epi == "rms_bwd" else 1
        out_specs.append(spec((1, width), lambda i, j, k: (0, 0)))
        out_shape.append(jax.ShapeDtypeStruct((1, width), F32))
    scratch = [pltpu.VMEM((tm, tn), F32)] if nk > 1 else []
    if comm is not None:
        kind, cx = comm
        in_specs.append(HBM_SPEC)
        ops.append(cx)
        out_specs.append(HBM_SPEC)
        out_shape.append(jax.ShapeDtypeStruct((NDEV if kind == "gather" else NDEV // 2,) + cx.shape[-2:], cx.dtype))
        scratch += list(GATHER_SEMS if kind == "gather" else CHIP_SEMS)
    grid = (N // tn, M // tm, nk) if n_outer else (M // tm, N // tn, nk)
    ordered = reducing or comm is not None
    outs = pl.pallas_call(
        body, name=name, grid=grid,
        in_specs=in_specs, out_specs=out_specs, out_shape=out_shape, scratch_shapes=scratch,
        compiler_params=_cp(("arbitrary" if ordered else "parallel", "arbitrary" if comm is not None else "parallel",
                             "arbitrary")),
    )(*ops)
    return outs if len(out_shape) > 1 else outs[0]


def _head_select(first_lane):
    l = lax.broadcasted_iota(jnp.int32, (LANES, HW), 0)
    c = lax.broadcasted_iota(jnp.int32, (LANES, HW), 1)
    return (l == first_lane + c // DH).astype(F32)


def _conv_taps(buf, cw, ts):
    c = cw[3:4, :] * buf[8:8 + ts, :]
    for j in range(3):
        k = 3 - j
        c = c + cw[j:j + 1, :] * buf[8 - k:8 - k + ts, :]
    return c


def _pre_fwd(proj, conv_w, alog_f, dtb_f, gsq, gsk, gmq, S):
    ts = _narrow_tile(S)
    hb = ts // 8

    def body(qkv_ref, halo_ref, ab_ref, sq_ref, sk_ref, sv_ref, mq_ref, cw_ref, al_ref, dt_ref, gsq_ref, gsk_ref,
             gmq_ref, gq_o, gk_o, gv_o, gf_o, bf_o, sqn_o, skn_o, svb_o, qmn_o, buf):
        i = pl.program_id(0)
        buf[0:8, :] = jnp.where(i == 0, 0.0, halo_ref[...])
        buf[8:8 + ts, :] = qkv_ref[...]
        c = _conv_taps(buf, cw_ref[...], ts)
        a = c * _sigmoid(c)
        for h in range(NH):
            q = a[:, h * DH:(h + 1) * DH]
            k = a[:, HW + h * DH:HW + (h + 1) * DH]
            gq_o[:, _hs(h)] = q * (lax.rsqrt(jnp.sum(q * q, axis=-1, keepdims=True) + EPS) * DH ** -0.5)
            gk_o[:, _hs(h)] = k * lax.rsqrt(jnp.sum(k * k, axis=-1, keepdims=True) + EPS)
            sqn_o[:, _hs(h)] = _rms(sq_ref[:, _hs(h)], gsq_ref[...])[0].astype(BF16)
            skn_o[:, _hs(h)] = _rms(sk_ref[:, _hs(h)], gsk_ref[...])[0].astype(BF16)
            qmn_o[:, _hs(h)] = _rms(mq_ref[:, _hs(h)], gmq_ref[...])[0].astype(BF16)
        gv_o[...] = a[:, 2 * HW:3 * HW]
        svb_o[...] = sv_ref[...].astype(BF16)
        ab = ab_ref[:, 0:LANES]
        a_bc = _dxr(ab, _head_select(0))
        b_bc = _dxr(ab, _head_select(NH))
        gf_o[...] = -jnp.exp(al_ref[...]) * _softplus(a_bc + dt_ref[...])
        bf_o[...] = _sigmoid(b_bc)

    row = lambda cb: pl.BlockSpec((ts, HW), lambda i: (i, cb))
    full = lambda r, c: pl.BlockSpec((r, c), lambda i: (0, 0))
    f32o = jax.ShapeDtypeStruct((S, HW), F32)
    bfo = jax.ShapeDtypeStruct((S, HW), BF16)
    return pl.pallas_call(
        body, name="pre_fwd", grid=(S // ts,),
        in_specs=[pl.BlockSpec((ts, 3 * HW), lambda i: (i, 0)),
                  pl.BlockSpec((8, 3 * HW), lambda i: (jnp.maximum(i * hb - 1, 0), 0)),
                  row(CB_AB), row(CB_SQ), row(CB_SK), row(CB_SV), row(CB_MQ),
                  full(4, 3 * HW), full(1, HW), full(1, HW), full(1, DH), full(1, DH), full(1, DH)],
        out_specs=[pl.BlockSpec((ts, HW), lambda i: (i, 0))] * 9,
        out_shape=[f32o, f32o, f32o, f32o, f32o, bfo, bfo, bfo, bfo],
        scratch_shapes=[pltpu.VMEM((ts + 8, 3 * HW), F32)],
        compiler_params=_cp(("parallel",)),
    )(proj, proj, proj, proj, proj, proj, proj, conv_w, alog_f, dtb_f, gsq, gsk, gmq)


def _pre_bwd(proj, conv_w, alog_f, dtb_f, gsq, gsk, dgq, dgk, dgv, dgf, dbf, dsqn, dskn, S):
    ts = _narrow_tile(S)
    hb = ts // 8

    def body(qkv_ref, halo_ref, ab_ref, sq_ref, sk_ref, cw_ref, al_ref, dt_ref, gsq_ref, gsk_ref,
             dgq_ref, dgk_ref, dgv_ref, dgf_ref, dbf_ref, dsqn_ref, dskn_ref,
             dc_o, dab_o, dsq_o, dsk_o, dcw_o, dal_o, ddt_o, dgsq_o, dgsk_o, buf):
        i = pl.program_id(0)

        @pl.when(i == 0)
        def _():
            dcw_o[...] = jnp.zeros_like(dcw_o)
            dal_o[...] = jnp.zeros_like(dal_o)
            ddt_o[...] = jnp.zeros_like(ddt_o)
            dgsq_o[...] = jnp.zeros_like(dgsq_o)
            dgsk_o[...] = jnp.zeros_like(dgsk_o)

        buf[0:8, :] = jnp.where(i == 0, 0.0, halo_ref[...])
        buf[8:8 + ts, :] = qkv_ref[...]
        c = _conv_taps(buf, cw_ref[...], ts)
        sg = _sigmoid(c)
        a = c * sg
        dsilu = sg * (1.0 + c * (1.0 - sg))
        dgsq = jnp.zeros((1, DH), F32)
        dgsk = jnp.zeros((1, DH), F32)
        for h in range(NH):
            q = a[:, h * DH:(h + 1) * DH]
            k = a[:, HW + h * DH:HW + (h + 1) * DH]
            nq = lax.rsqrt(jnp.sum(q * q, axis=-1, keepdims=True) + EPS)
            nk = lax.rsqrt(jnp.sum(k * k, axis=-1, keepdims=True) + EPS)
            dyq = dgq_ref[:, _hs(h)]
            dyk = dgk_ref[:, _hs(h)]
            dq = (nq * dyq - q * (nq * nq * nq) * jnp.sum(dyq * q, axis=-1, keepdims=True)) * DH ** -0.5
            dk = nk * dyk - k * (nk * nk * nk) * jnp.sum(dyk * k, axis=-1, keepdims=True)
            dc_o[:, h * DH:(h + 1) * DH] = dq * dsilu[:, h * DH:(h + 1) * DH]
            dc_o[:, HW + h * DH:HW + (h + 1) * DH] = dk * dsilu[:, HW + h * DH:HW + (h + 1) * DH]
            x = sq_ref[:, _hs(h)]
            _, r = _rms(x, gsq_ref[...])
            dx, dg = _rms_bwd(dsqn_ref[:, _hs(h)], x, gsq_ref[...], r)
            dsq_o[:, _hs(h)] = dx.astype(BF16)
            dgsq = dgsq + dg
            x = sk_ref[:, _hs(h)]
            _, r = _rms(x, gsk_ref[...])
            dx, dg = _rms_bwd(dskn_ref[:, _hs(h)], x, gsk_ref[...], r)
            dsk_o[:, _hs(h)] = dx.astype(BF16)
            dgsk = dgsk + dg
        dc_o[:, 2 * HW:3 * HW] = dgv_ref[...] * dsilu[:, 2 * HW:3 * HW]
        dgsq_o[...] += dgsq
        dgsk_o[...] += dgsk
        dc = dc_o[...]
        for j in range(4):
            k = 3 - j
            dcw_o[j:j + 1, :] += jnp.sum(dc * buf[8 - k:8 - k + ts, :], axis=0, keepdims=True)
        ab = ab_ref[:, 0:LANES]
        a_bc = _dxr(ab, _head_select(0))
        b_bc = _dxr(ab, _head_select(NH))
        pre = a_bc + dt_ref[...]
        ea = jnp.exp(al_ref[...])
        dgf = dgf_ref[...]
        dal_o[...] += jnp.sum(dgf * (-ea * _softplus(pre)), axis=0, keepdims=True)
        da = dgf * (-ea * _sigmoid(pre))
        ddt_o[...] += jnp.sum(da, axis=0, keepdims=True)
        beta = _sigmoid(b_bc)
        db = dbf_ref[...] * beta * (1.0 - beta)
        lane = lax.broadcasted_iota(jnp.int32, (ts, LANES), 1)
        dab = jnp.zeros((ts, LANES), F32)
        for h in range(NH):
            dab = dab + jnp.where(lane == h, da[:, _hs(h)], 0.0) + jnp.where(lane == NH + h, db[:, _hs(h)], 0.0)
        dab_o[:, 0:LANES] = dab.astype(BF16)
        dab_o[:, LANES:HW] = jnp.zeros((ts, HW - LANES), BF16)

    row = lambda cb: pl.BlockSpec((ts, HW), lambda i: (i, cb))
    full = lambda r, c: pl.BlockSpec((r, c), lambda i: (0, 0))
    t512 = pl.BlockSpec((ts, HW), lambda i: (i, 0))
    return pl.pallas_call(
        body, name="pre_bwd", grid=(S // ts,),
        in_specs=[pl.BlockSpec((ts, 3 * HW), lambda i: (i, 0)),
                  pl.BlockSpec((8, 3 * HW), lambda i: (jnp.maximum(i * hb - 1, 0), 0)),
                  row(CB_AB), row(CB_SQ), row(CB_SK),
                  full(4, 3 * HW), full(1, HW), full(1, HW), full(1, DH), full(1, DH)] + [t512] * 7,
        out_specs=[pl.BlockSpec((ts, 3 * HW), lambda i: (i, 0)), t512, t512, t512,
                   full(4, 3 * HW), full(1, HW), full(1, HW), full(1, DH), full(1, DH)],
        out_shape=[jax.ShapeDtypeStruct((S, 3 * HW), F32)] + [jax.ShapeDtypeStruct((S, HW), BF16)] * 3
        + [jax.ShapeDtypeStruct((4, 3 * HW), F32), jax.ShapeDtypeStruct((1, HW), F32),
           jax.ShapeDtypeStruct((1, HW), F32), jax.ShapeDtypeStruct((1, DH), F32),
           jax.ShapeDtypeStruct((1, DH), F32)],
        scratch_shapes=[pltpu.VMEM((ts + 8, 3 * HW), F32)],
        compiler_params=_cp(("arbitrary",)),
    )(proj, proj, proj, proj, proj, conv_w, alog_f, dtb_f, gsq, gsk, dgq, dgk, dgv, dgf, dbf, dsqn, dskn)


def _conv_bwd(dc, conv_w, S):
    ts = _row_tile(S)
    hb = ts // 8
    n = S // ts

    def body(dc_ref, halo_ref, cw_ref, o_ref, buf):
        i = pl.program_id(0)
        buf[0:ts, :] = dc_ref[...]
        buf[ts:ts + 8, :] = jnp.where(i == n - 1, 0.0, halo_ref[...])
        cw = cw_ref[...]
        acc = cw[3:4, :] * buf[0:ts, :]
        for k in range(1, 4):
            acc = acc + cw[3 - k:4 - k, :] * buf[k:k + ts, :]
        o_ref[...] = acc.astype(BF16)

    return pl.pallas_call(
        body, name="conv_bwd", grid=(n,),
        in_specs=[pl.BlockSpec((ts, 3 * HW), lambda i: (i, 0)),
                  pl.BlockSpec((8, 3 * HW), lambda i: (jnp.minimum((i + 1) * hb, S // 8 - 1), 0)),
                  pl.BlockSpec((4, 3 * HW), lambda i: (0, 0))],
        out_specs=pl.BlockSpec((ts, 3 * HW), lambda i: (i, 0)),
        out_shape=jax.ShapeDtypeStruct((S, 3 * HW), BF16),
        scratch_shapes=[pltpu.VMEM((ts + 8, 3 * HW), F32)],
        compiler_params=_cp(("parallel",)),
    )(dc, dc, conv_w)


def _gdn_masks():
    r = lax.broadcasted_iota(jnp.int32, (PAIR, PAIR), 0)
    c = lax.broadcasted_iota(jnp.int32, (PAIR, PAIR), 1)
    same = ((r >= CHUNK) & (c >= CHUNK)) | ((r < CHUNK) & (c < CHUNK))
    return dict(r=r, same=same, tril=same & (r >= c), strict=same & (r > c), triu=same & (c >= r), eye=r == c,
                in_a=r < CHUNK, last_a=r == CHUNK - 1, last_b=r == PAIR - 1)


def _each(fn, *cols):
    return [fn(*xs) for xs in zip(*cols)]


def _mul(a, b):
    return a * b


def _top(x):
    return x[:CHUNK]


def _bot(x):
    return x[CHUNK:]


def _rows(a, b):
    return jnp.concatenate([a, b], axis=0)


def _tri_inv(lm, eye):
    eye_f = eye.astype(F32)
    p = _each(lambda l: eye_f - l, lm)
    lp = _each(lambda l: _dg(l, l), lm)
    for it in range(5):
        p = _each(lambda a, b: a + _dg(a, b), p, lp)
        if it < 4:
            lp = _each(lambda b: _dg(b, b), lp)
    return p


def _gdn_block(m, q, k, v, g, beta):
    tril_f = m["tril"].astype(F32)
    col_sum = lambda mask: (lambda x: jnp.sum(jnp.where(mask, x, 0.0), axis=0, keepdims=True))
    gc = _each(lambda x: _dxl(tril_f, x), g)
    gcr = _each(col_sum(m["eye"]), gc)
    gam = _each(lambda a, b: jnp.where(m["tril"], jnp.exp(jnp.minimum(a - b, 0.0)), 0.0), gc, gcr)
    kb = _each(_mul, k, beta)
    vb = _each(_mul, v, beta)
    lm = _each(lambda a, b, c: jnp.where(m["strict"], _dg(a, b, NT) * c, 0.0), kb, k, gam)
    t = _tri_inv(lm, m["eye"])
    eg = _each(jnp.exp, gc)
    kbe = _each(_mul, kb, eg)
    u = _each(_dg, t, vb)
    w = _each(_dg, t, kbe)
    aqk = _each(lambda a, b, c: jnp.where(m["tril"], _dg(a, b, NT) * c, 0.0), q, k, gam)
    qd = _each(_mul, q, eg)
    ga = _each(col_sum(m["last_a"]), gc)
    gb = _each(col_sum(m["last_b"]), gc)
    e2 = _each(lambda a, b, c: jnp.exp(jnp.where(m["in_a"], a, b) - c), ga, gb, gc)
    kd = _each(_mul, k, e2)
    return dict(u=u, w=w, aqk=aqk, qd=qd, kd=kd, gam=gam, kb=kb, vb=vb, lm=lm, t=t, eg=eg, kbe=kbe, e2=e2,
                gla=_each(jnp.exp, ga), glb=_each(jnp.exp, gb))


def _gdn_fwd(gq, gk, gv, gf, bf, S):
    nb = S // PAIR

    def body(q_ref, k_ref, v_ref, g_ref, b_ref, o_ref, st_ref, s_scr):
        @pl.when(pl.program_id(0) == 0)
        def _():
            s_scr[...] = jnp.zeros_like(s_scr)

        m = _gdn_masks()
        heads = lambda ref: [ref[:, _hs(h)] for h in range(NH)]
        f = _gdn_block(m, heads(q_ref), heads(k_ref), heads(v_ref), heads(g_ref), heads(b_ref))
        u, w, qd, kd = f["u"], f["w"], f["qd"], f["kd"]
        s0 = [s_scr[h * DH:(h + 1) * DH, :] for h in range(NH)]
        vna = _each(lambda a, b, s: _top(a) - _dg(_top(b), s), u, w, s0)
        oa = _each(lambda a, s: _dg(_top(a), s), qd, s0)
        s1 = _each(lambda s, gl, a, vn: s * gl + _dg(_top(a), vn, TN), s0, f["gla"], kd, vna)
        vnb = _each(lambda a, b, s: _bot(a) - _dg(_bot(b), s), u, w, s1)
        ob = _each(lambda a, s: _dg(_bot(a), s), qd, s1)
        s2 = _each(lambda s, gl, a, vn: s * gl + _dg(_bot(a), vn, TN), s1, f["glb"], kd, vnb)
        outs = _each(lambda a, b, c, va, vb: _rows(a, b) + _dg(c, _rows(va, vb)), oa, ob, f["aqk"], vna, vnb)
        o_ref[...] = jnp.concatenate(outs, axis=1)
        st_ref[...] = jnp.concatenate(s0 + s1, axis=0)
        s_scr[...] = jnp.concatenate(s2, axis=0)

    blk = pl.BlockSpec((PAIR, HW), lambda i: (i, 0))
    return pl.pallas_call(
        body, name="gdn_fwd", grid=(nb,),
        in_specs=[blk] * 5,
        out_specs=[blk, pl.BlockSpec((2 * NH * DH, DH), lambda i: (i, 0))],
        out_shape=[jax.ShapeDtypeStruct((S, HW), F32), jax.ShapeDtypeStruct((nb * 2 * NH * DH, DH), F32)],
        scratch_shapes=[pltpu.VMEM((NH * DH, DH), F32)],
        compiler_params=_cp(("arbitrary",)),
    )(gq, gk, gv, gf, bf)


def _gdn_bwd(gq, gk, gv, gf, bf, states, do, S):
    nb = S // PAIR

    def body(q_ref, k_ref, v_ref, g_ref, b_ref, st_ref, do_ref, dq_o, dk_o, dv_o, dg_o, db_o, ds_scr):
        @pl.when(pl.program_id(0) == 0)
        def _():
            ds_scr[...] = jnp.zeros_like(ds_scr)

        m = _gdn_masks()
        ones = jnp.ones((PAIR, PAIR), F32)
        heads = lambda ref: [ref[:, _hs(h)] for h in range(NH)]
        q, k, v, beta, do = heads(q_ref), heads(k_ref), heads(v_ref), heads(b_ref), heads(do_ref)
        f = _gdn_block(m, q, k, v, heads(g_ref), beta)
        u, w, aqk, qd, kd, t = f["u"], f["w"], f["aqk"], f["qd"], f["kd"], f["t"]
        s0 = [st_ref[h * DH:(h + 1) * DH, :] for h in range(NH)]
        s1 = [st_ref[(NH + h) * DH:(NH + h + 1) * DH, :] for h in range(NH)]
        ds2 = [ds_scr[h * DH:(h + 1) * DH, :] for h in range(NH)]
        total = lambda a, b: jnp.sum(jnp.sum(a * b, axis=1, keepdims=True), axis=0, keepdims=True)
        vna = _each(lambda a, b, s: _top(a) - _dg(_top(b), s), u, w, s0)
        vnb = _each(lambda a, b, s: _bot(a) - _dg(_bot(b), s), u, w, s1)
        dvn_i = _each(lambda a, b: _dg(a, b, TN), aqk, do)
        dvnb = _each(lambda a, b, s: _bot(a) + _dg(_bot(b), s), dvn_i, kd, ds2)
        dqdb = _each(lambda a, s: _dg(_bot(a), s, NT), do, s1)
        dkdb = _each(lambda a, s: _dg(a, s, NT), vnb, ds2)
        dglb = _each(total, ds2, s1)
        dwb = _each(lambda a, s: -_dg(a, s, NT), dvnb, s1)
        ds1 = _each(lambda s, gl, a, b, c, d: s * gl + _dg(_bot(a), _bot(b), TN) - _dg(_bot(c), d, TN),
                    ds2, f["glb"], qd, do, w, dvnb)
        dvna = _each(lambda a, b, s: _top(a) + _dg(_top(b), s), dvn_i, kd, ds1)
        dqda = _each(lambda a, s: _dg(_top(a), s, NT), do, s0)
        dkda = _each(lambda a, s: _dg(a, s, NT), vna, ds1)
        dgla = _each(total, ds1, s0)
        dwa = _each(lambda a, s: -_dg(a, s, NT), dvna, s0)
        ds0 = _each(lambda s, gl, a, b, c, d: s * gl + _dg(_top(a), _top(b), TN) - _dg(_top(c), d, TN),
                    ds1, f["gla"], qd, do, w, dvna)
        dvn, dqd, dkd, dw = (_each(_rows, a, b) for a, b in ((dvna, dvnb), (dqda, dqdb), (dkda, dkdb), (dwa, dwb)))
        daqk = _each(lambda a, va, vb: jnp.where(m["tril"], _dg(a, _rows(va, vb), NT), 0.0), do, vna, vnb)
        dt = _each(lambda a, b, c, d: _dg(a, b, NT) + _dg(c, d, NT), dvn, f["vb"], dw, f["kbe"])
        dvb = _each(lambda a, b: _dg(a, b, TN), t, dvn)
        dkbe = _each(lambda a, b: _dg(a, b, TN), t, dw)
        dtt = _each(lambda a, b: _dg(a, b, NT), dt, t)
        dl = _each(lambda a, b: -jnp.where(m["strict"], _dg(a, b, TN), 0.0), t, dtt)
        dm = _each(_mul, dl, f["gam"])
        dn = _each(_mul, daqk, f["gam"])
        dkb = _each(lambda a, b, c, d: _dg(a, b) + c * d, dm, k, dkbe, f["eg"])
        dks = _each(lambda a, b, c, d, e, g, h, i: _dg(a, b, TN) + _dg(c, d, TN) + e * g + h * i,
                    dm, f["kb"], dn, q, dkd, f["e2"], beta, dkb)
        dqs = _each(lambda a, b, c, d: _dg(a, b) + c * d, dn, k, dqd, f["eg"])
        gm = _each(lambda a, b, c, d: a * b + c * d, dl, f["lm"], daqk, aqk)
        dkdkd = _each(_mul, dkd, kd)
        dgc = _each(lambda a, b, c, d, e, g: _dxr(a + b * c + d * e - g, ones) - _dxr(a, ones, TN),
                    gm, dqd, qd, dkbe, f["kbe"], dkdkd)
        same_f = m["same"].astype(F32)
        chunk_tot = _each(lambda a: _dxl(same_f, _dxr(a, ones)), dkdkd)
        last = m["last_a"] | m["last_b"]
        dgc = _each(lambda a, b, ga, gla, gb, glb: a + jnp.where(last, b + jnp.where(m["in_a"], ga * gla, gb * glb), 0.0),
                    dgc, chunk_tot, dgla, f["gla"], dglb, f["glb"])
        dbs = _each(lambda a, b, c, d: _dxr(a * b + c * d, ones), dkb, k, dvb, v)
        dvs = _each(_mul, beta, dvb)
        triu_f = m["triu"].astype(F32)
        dgs = _each(lambda a: _dxl(triu_f, a), dgc)
        for ref, parts in ((dq_o, dqs), (dk_o, dks), (dv_o, dvs), (dg_o, dgs), (db_o, dbs)):
            ref[...] = jnp.concatenate(parts, axis=1)
        ds_scr[...] = jnp.concatenate(ds0, axis=0)

    blk = pl.BlockSpec((PAIR, HW), lambda i: (nb - 1 - i, 0))
    o = jax.ShapeDtypeStruct((S, HW), F32)
    return pl.pallas_call(
        body, name="gdn_bwd", grid=(nb,),
        in_specs=[blk] * 5 + [pl.BlockSpec((2 * NH * DH, DH), lambda i: (nb - 1 - i, 0)), blk],
        out_specs=[blk] * 5, out_shape=[o] * 5,
        scratch_shapes=[pltpu.VMEM((NH * DH, DH), F32)],
        compiler_params=_cp(("arbitrary",)),
    )(gq, gk, gv, gf, bf, states, do)


SB_T = 256
SB_GROUP = 4
SB_GROUP_BWD = 4
SB_SINGLES = 1
SB_DEAD = -110.0


def _group_sizes(g):
    sizes = []
    while g >= 1:
        sizes.append(g)
        g //= 2
    return sizes


def _sb_iotas(t):
    return lax.broadcasted_iota(jnp.int32, (t, t), 0), lax.broadcasted_iota(jnp.int32, (t, t), 1)


def _sb_scores(q, k, mask):
    z = _dot(q, k, NT) * DH ** -0.5
    ls = jnp.minimum(z, 0.0) - jnp.log(1.0 + jnp.exp(-jnp.abs(z)))
    lneg = ls - z
    if mask is not None:
        lneg = jnp.where(mask, lneg, 0.0)
    return ls, lneg


def _prefix(x, u):
    xh, xl = _split(x, 2)
    return _dot(xh, u) + _dot(xl, u)


def _sb_fwd(sqn, skn, svb, S):
    t = min(SB_T, S)

    def body(q_ref, k_ref, v_ref, o_ref, t_ref, cnt_ref):
        qb = pl.program_id(1)
        q = q_ref[...]
        r, c = _sb_iotas(t)
        diag = c < r
        u_after = (r > c).astype(BF16)

        def tiles(k0s, run, masks):
            sc = _each(lambda k0, m: _sb_scores(q, k_ref[pl.ds(k0, t), :], m), k0s, masks)
            ls, lneg = [s[0] for s in sc], [s[1] for s in sc]
            sums = _each(lambda x: jnp.sum(x, axis=1, keepdims=True), lneg)
            pre = _each(lambda x: _prefix(x, u_after), lneg)
            runs = [run]
            for s in sums:
                runs.append(runs[-1] + s)
            att = _each(lambda a, b, rn: jnp.exp(a + (rn + b)), ls, pre, runs[:-1])
            att = _each(lambda a, m: a if m is None else jnp.where(m, a, 0.0), att, masks)
            parts = _each(lambda a, k0: _dot(a.astype(BF16), v_ref[pl.ds(k0, t), :]), att, k0s)
            return sum(parts[1:], parts[0]), runs[-1]

        left = jnp.full((t, t), qb > 0)
        acc, run = tiles([pl.multiple_of(qb * t, t), pl.multiple_of(jnp.maximum(qb - 1, 0) * t, t)],
                         jnp.zeros((t, 1), F32), [diag, left])

        def alive(run):
            return jnp.max(run) >= SB_DEAD

        carry, done = (0, acc, run, alive(run)), jnp.minimum(qb, 1)
        for size, limit in [(1, SB_SINGLES)] + [(s, None) for s in _group_sizes(SB_GROUP)]:

            def more(c, size=size, done=done, limit=limit):
                i, _, _, go = c
                fits = done + (i + 1) * size <= qb
                return (fits if limit is None else fits & (i < limit)) & go

            def group(c, size=size, done=done):
                i, acc, run, _ = c
                first = qb - 1 - done - size * i
                part, run = tiles([pl.multiple_of((first - j) * t, t) for j in range(size)], run, [None] * size)
                return i + 1, acc + part, run, alive(run)

            n, acc, run, go = lax.while_loop(more, group, (0,) + carry[1:])
            carry, done = (0, acc, run, go), done + n * size
        o_ref[...] = acc.astype(BF16)
        t_ref[...] = jnp.broadcast_to(run, (t, DH))
        cnt_ref[pl.program_id(0), qb] = done

    qspec = pl.BlockSpec((t, DH), lambda h, i: (i, h))
    kspec = pl.BlockSpec((S, DH), lambda h, i: (0, h))
    return pl.pallas_call(
        body, name="sb_fwd", grid=(NH, S // t),
        in_specs=[qspec, kspec, kspec],
        out_specs=[qspec, qspec, pl.BlockSpec(memory_space=pltpu.SMEM)],
        out_shape=[jax.ShapeDtypeStruct((S, HW), BF16), jax.ShapeDtypeStruct((S, HW), F32),
                   jax.ShapeDtypeStruct((NH, S // t), jnp.int32)],
        compiler_params=_cp(("arbitrary", "arbitrary")),
    )(sqn, skn, svb)


def _sb_bwd(sqn, skn, svb, do, tot, walked, S):
    t = min(SB_T, S)

    def body(cnt_ref, q_ref, k_ref, v_ref, do_ref, t_ref, dq_o, dk_o, dv_o, dv_acc):
        qb = pl.program_id(1)

        @pl.when(qb == 0)
        def _():
            dk_o[...] = jnp.zeros_like(dk_o)
            dv_acc[...] = jnp.zeros_like(dv_acc)

        q = q_ref[...]
        do = do_ref[...].astype(BF16)
        tot_l = jnp.concatenate([t_ref[...]] * (t // DH), axis=1)
        r, c = _sb_iotas(t)
        diag = c < r
        u_upto = (r <= c).astype(BF16)
        u_before = (r < c).astype(BF16)

        def tiles(k0s, run_l, run_e, masks):
            rowsum = lambda x: jnp.sum(x, axis=1, keepdims=True)
            masked = lambda xs: _each(lambda a, m: a if m is None else jnp.where(m, a, 0.0), xs, masks)
            ks = [k_ref[pl.ds(k0, t), :] for k0 in k0s]
            vs = [v_ref[pl.ds(k0, t), :] for k0 in k0s]
            sc = _each(lambda k, m: _sb_scores(q, k, m), ks, masks)
            ls, lneg = [s[0] for s in sc], [s[1] for s in sc]
            sums_l = _each(rowsum, lneg)
            pre_l = _each(lambda x: _prefix(x, u_upto), lneg)
            runs_l = [run_l]
            for s in sums_l:
                runs_l.append(runs_l[-1] + s)
            att = masked(_each(lambda a, b, rn: jnp.exp(a + (tot_l - (rn + b))), ls, pre_l, runs_l[:-1]))
            e = _each(lambda v, a: _dot(do, v, NT) * a, vs, att)
            sums_e = _each(rowsum, e)
            pre_e = _each(lambda x: _prefix(x, u_before), e)
            runs_e = [run_e]
            for s in sums_e:
                runs_e.append(runs_e[-1] + s)
            sg = _each(jnp.exp, ls)
            dz = masked(_each(lambda a, b, rn, s: a * (1.0 - s) - (rn + b) * s, e, pre_e, runs_e[:-1], sg))
            dz = _each(lambda a: (a * DH ** -0.5).astype(BF16), dz)
            dvs = _each(lambda a: _dot(a.astype(BF16), do, TN), att)
            dks = _each(lambda a: _dot(a, q, TN), dz)
            dqs = _each(_dot, dz, ks)
            for k0, dv, dk in zip(k0s, dvs, dks):
                dv_acc[pl.ds(k0, t), :] += dv
                dk_o[pl.ds(k0, t), :] += dk
            return sum(dqs[1:], dqs[0]), runs_l[-1], runs_e[-1]

        walked = cnt_ref[pl.program_id(0), qb]
        early = jnp.maximum(walked - 1, 0)
        z1 = jnp.zeros((t, 1), F32)
        carry, done = (jnp.zeros((t, DH), F32), z1, z1), 0
        for size in _group_sizes(SB_GROUP_BWD):
            n = (early - done) // size

            def group(i, carry, size=size, done=done):
                dq, run_l, run_e = carry
                first = qb - walked + done + size * i
                part, run_l, run_e = tiles([pl.multiple_of((first + j) * t, t) for j in range(size)], run_l, run_e,
                                           [None] * size)
                return dq + part, run_l, run_e

            carry = lax.fori_loop(0, n, group, carry)
            done = done + n * size
        dq, run_l, run_e = carry
        left = jnp.full((t, t), qb > 0)
        part, _, _ = tiles([pl.multiple_of(jnp.maximum(qb - 1, 0) * t, t), pl.multiple_of(qb * t, t)], run_l, run_e,
                           [left, diag])
        dq_o[...] = dq + part

        @pl.when(qb == S // t - 1)
        def _():
            dv_o[...] = dv_acc[...].astype(BF16)

    qspec = pl.BlockSpec((t, DH), lambda h, i, cnt: (i, h))
    kspec = pl.BlockSpec((S, DH), lambda h, i, cnt: (0, h))
    o = jax.ShapeDtypeStruct((S, HW), F32)
    return pl.pallas_call(
        body, name="sb_bwd",
        grid_spec=pltpu.PrefetchScalarGridSpec(
            num_scalar_prefetch=1, grid=(NH, S // t),
            in_specs=[qspec, kspec, kspec, qspec, qspec], out_specs=[qspec, kspec, kspec],
            scratch_shapes=[pltpu.VMEM((S, DH), F32)]),
        out_shape=[o, o, jax.ShapeDtypeStruct((S, HW), BF16)],
        compiler_params=_cp(("parallel", "arbitrary")),
    )(walked, sqn, skn, svb, do, tot)


def _mem_probs(qn, kn):
    s = _dot(qn, kn.astype(BF16), NT) * DH ** -0.5
    p = jnp.exp(s - jnp.max(s, axis=-1, keepdims=True))
    return p / jnp.sum(p, axis=-1, keepdims=True)


def _mem_fwd(qmn, kv, gmk, S):
    ts = _row_tile(S)

    def body(q_ref, kv_ref, gk_ref, o_ref):
        for h in range(NH):
            kn, _ = _rms(kv_ref[:, _hs(h)], gk_ref[...])
            p = _mem_probs(q_ref[:, _hs(h)], kn)
            o_ref[:, _hs(h)] = _dbf(p, kv_ref[:, HW + h * DH:HW + (h + 1) * DH]).astype(BF16)

    return pl.pallas_call(
        body, name="mem_fwd", grid=(S // ts,),
        in_specs=[pl.BlockSpec((ts, HW), lambda i: (i, 0)), pl.BlockSpec((NMEM, 2 * HW), lambda i: (0, 0)),
                  pl.BlockSpec((1, DH), lambda i: (0, 0))],
        out_specs=pl.BlockSpec((ts, HW), lambda i: (i, 0)),
        out_shape=jax.ShapeDtypeStruct((S, HW), BF16),
        compiler_params=_cp(("parallel",)),
    )(qmn, kv, gmk)


def _mem_bwd(proj, qmn, kv, gmq, gmk, do, S):
    ts = _row_tile(S)
    n = S // ts

    def body(mq_ref, q_ref, kv_ref, gq_ref, gk_ref, do_ref, dmq_o, dkv_o, dgq_o, dgk_o, dkn_scr):
        i = pl.program_id(0)

        @pl.when(i == 0)
        def _():
            dkv_o[...] = jnp.zeros_like(dkv_o)
            dgq_o[...] = jnp.zeros_like(dgq_o)
            dkn_scr[...] = jnp.zeros_like(dkn_scr)

        dgq = jnp.zeros((1, DH), F32)
        for h in range(NH):
            km = kv_ref[:, _hs(h)]
            vm = kv_ref[:, HW + h * DH:HW + (h + 1) * DH].astype(BF16)
            kn, _ = _rms(km, gk_ref[...])
            qn = q_ref[:, _hs(h)]
            p = _mem_probs(qn, kn)
            dob = do_ref[:, _hs(h)].astype(BF16)
            dkv_o[:, HW + h * DH:HW + (h + 1) * DH] += _dot(p.astype(BF16), dob, TN)
            dp = _dot(dob, vm, NT)
            dsc = (p * (dp - jnp.sum(dp * p, axis=-1, keepdims=True)) * DH ** -0.5).astype(BF16)
            dkn_scr[:, _hs(h)] += _dot(dsc, qn, TN)
            x = mq_ref[:, _hs(h)]
            _, r = _rms(x, gq_ref[...])
            dx, dg = _rms_bwd(_dot(dsc, kn.astype(BF16)), x, gq_ref[...], r)
            dmq_o[:, _hs(h)] = dx.astype(BF16)
            dgq = dgq + dg
        dgq_o[...] += dgq

        @pl.when(i == n - 1)
        def _():
            dgk = jnp.zeros((1, DH), F32)
            for h in range(NH):
                km = kv_ref[:, _hs(h)]
                _, r = _rms(km, gk_ref[...])
                dx, dg = _rms_bwd(dkn_scr[:, _hs(h)], km, gk_ref[...], r)
                dkv_o[:, _hs(h)] = dx
                dgk = dgk + dg
            dgk_o[...] = dgk

    full = lambda r, c: pl.BlockSpec((r, c), lambda i: (0, 0))
    t512 = pl.BlockSpec((ts, HW), lambda i: (i, 0))
    return pl.pallas_call(
        body, name="mem_bwd", grid=(n,),
        in_specs=[pl.BlockSpec((ts, HW), lambda i: (i, CB_MQ)), t512, full(NMEM, 2 * HW), full(1, DH), full(1, DH),
                  t512],
        out_specs=[t512, full(NMEM, 2 * HW), full(1, DH), full(1, DH)],
        out_shape=[jax.ShapeDtypeStruct((S, HW), BF16), jax.ShapeDtypeStruct((NMEM, 2 * HW), F32),
                   jax.ShapeDtypeStruct((1, DH), F32), jax.ShapeDtypeStruct((1, DH), F32)],
        scratch_shapes=[pltpu.VMEM((NMEM, HW), F32)],
        compiler_params=_cp(("arbitrary",)),
    )(proj, qmn, kv, gmq, gmk, do)


def _gated_gdn(o, z, g):
    sg = _sigmoid(z)
    outs, rs = [], []
    for h in range(NH):
        y, r = _rms(o[:, _hs(h)], g)
        outs.append(y * (z[:, _hs(h)] * sg[:, _hs(h)]))
        rs.append(r)
    return jnp.concatenate(outs, axis=1), rs, sg


def _merge_fwd(x, proj, ogdn, osb, omem, ggdn, wbg, wbs, wbm, wo, S):
    ts = _narrow_tile(S)

    def body(x_ref, z_ref, g0_ref, g1_ref, g2_ref, og_ref, os_ref, om_ref, gg_ref, wbg_ref, wbs_ref, wbm_ref,
             wo_ref, x1_o, mix_o):
        on, _, _ = _gated_gdn(og_ref[...], z_ref[...], gg_ref[...])
        mix = (_sigmoid(g0_ref[...]) * _dbf(on, wbg_ref[...]) + _sigmoid(g1_ref[...]) * _dbf(os_ref[...], wbs_ref[...])
               + _sigmoid(g2_ref[...]) * _dbf(om_ref[...], wbm_ref[...]))
        mix_o[...] = mix.astype(BF16)
        x1_o[...] = x_ref[...] + _dbf(mix, wo_ref[...])

    t512 = pl.BlockSpec((ts, HW), lambda i: (i, 0))
    t1k = pl.BlockSpec((ts, D), lambda i: (i, 0))
    gate = lambda j: pl.BlockSpec((ts, D), lambda i: (i, 4 + j))
    full = lambda r, c: pl.BlockSpec((r, c), lambda i: (0, 0))
    return pl.pallas_call(
        body, name="merge_fwd", grid=(S // ts,),
        in_specs=[t1k, pl.BlockSpec((ts, HW), lambda i: (i, CB_Z)), gate(0), gate(1), gate(2), t512, t512, t512,
                  full(1, DH), full(HW, D), full(HW, D), full(HW, D), full(D, D)],
        out_specs=[t1k, t1k],
        out_shape=[jax.ShapeDtypeStruct((S, D), F32), jax.ShapeDtypeStruct((S, D), BF16)],
        compiler_params=_cp(("parallel",)),
    )(x, proj, proj, proj, proj, ogdn, osb, omem, ggdn, wbg, wbs, wbm, wo)


def _merge_bwd(dmix, proj, ogdn, osb, omem, ggdn, wbg, wbs, wbm, S):
    ts = _narrow_tile(S)

    def body(dm_ref, z_ref, g0_ref, g1_ref, g2_ref, og_ref, os_ref, om_ref, gg_ref, wbg_ref, wbs_ref, wbm_ref,
             dgl0_o, dgl1_o, dgl2_o, dog_o, dz_o, dos_o, dom_o, dwbg_o, dwbs_o, dwbm_o, dgg_o):
        @pl.when(pl.program_id(0) == 0)
        def _():
            for ref in (dwbg_o, dwbs_o, dwbm_o, dgg_o):
                ref[...] = jnp.zeros_like(ref)

        dm = dm_ref[...]
        og = og_ref[...]
        z = z_ref[...]
        on, rs, sg = _gated_gdn(og, z, gg_ref[...])
        branch = ((on, g0_ref, wbg_ref, dgl0_o, dwbg_o), (os_ref[...], g1_ref, wbs_ref, dgl1_o, dwbs_o),
                  (om_ref[...], g2_ref, wbm_ref, dgl2_o, dwbm_o))
        dos = []
        for o, g_ref, w_ref, dgl_o, dw_o in branch:
            ob = o.astype(BF16)
            gate = _sigmoid(g_ref[...])
            dgl_o[...] = (dm * _dot(ob, w_ref[...]) * gate * (1.0 - gate)).astype(BF16)
            dy = (dm * gate).astype(BF16)
            dw_o[...] += _dot(ob, dy, TN)
            dos.append(_dot(dy, w_ref[...], NT))
        dos_o[...] = dos[1].astype(BF16)
        dom_o[...] = dos[2].astype(BF16)
        don = dos[0]
        dgg = jnp.zeros((1, DH), F32)
        for h in range(NH):
            oh, zh, sh = og[:, _hs(h)], z[:, _hs(h)], sg[:, _hs(h)]
            y = oh * rs[h] * gg_ref[...]
            dz_o[:, _hs(h)] = (don[:, _hs(h)] * y * (sh * (1.0 + zh * (1.0 - sh)))).astype(BF16)
            dx, dg = _rms_bwd(don[:, _hs(h)] * (zh * sh), oh, gg_ref[...], rs[h])
            dog_o[:, _hs(h)] = dx
            dgg = dgg + dg
        dgg_o[...] += dgg

    t512 = pl.BlockSpec((ts, HW), lambda i: (i, 0))
    t1k = pl.BlockSpec((ts, D), lambda i: (i, 0))
    gate = lambda j: pl.BlockSpec((ts, D), lambda i: (i, 4 + j))
    full = lambda r, c: pl.BlockSpec((r, c), lambda i: (0, 0))
    s1k = jax.ShapeDtypeStruct((S, D), BF16)
    s512 = jax.ShapeDtypeStruct((S, HW), BF16)
    wsh = jax.ShapeDtypeStruct((HW, D), F32)
    return pl.pallas_call(
        body, name="merge_bwd", grid=(S // ts,),
        in_specs=[t1k, pl.BlockSpec((ts, HW), lambda i: (i, CB_Z)), gate(0), gate(1), gate(2), t512, t512, t512,
                  full(1, DH), full(HW, D), full(HW, D), full(HW, D)],
        out_specs=[t1k, t1k, t1k, t512, t512, t512, t512, full(HW, D), full(HW, D), full(HW, D), full(1, DH)],
        out_shape=[s1k, s1k, s1k, jax.ShapeDtypeStruct((S, HW), F32), s512, s512, s512, wsh, wsh, wsh,
                   jax.ShapeDtypeStruct((1, DH), F32)],
        compiler_params=_cp(("arbitrary",)),
    )(dmix, proj, proj, proj, proj, ogdn, osb, omem, ggdn, wbg, wbs, wbm)


def _norm_cast(name, x, g):
    rows = x.shape[0]
    ts = min(_row_tile(rows), rows)

    def body(x_ref, g_ref, o_ref):
        o_ref[...] = _rms(x_ref[...], g_ref[...])[0].astype(BF16)

    t1k = pl.BlockSpec((ts, D), lambda i: (i, 0))
    return pl.pallas_call(
        body, name=name, grid=(rows // ts,), in_specs=[t1k, pl.BlockSpec((1, D), lambda i: (0, 0))], out_specs=t1k,
        out_shape=jax.ShapeDtypeStruct((rows, D), BF16), compiler_params=_cp(("parallel",)),
    )(x, g)


def _norm_bwd(name, dh, x, g, res):
    rows = x.shape[0]
    ts = min(_row_tile(rows), rows)

    def body(*refs):
        dh_ref, x_ref, g_ref = refs[:3]
        dx_o, dg_o = refs[-2:]

        @pl.when(pl.program_id(0) == 0)
        def _():
            dg_o[...] = jnp.zeros_like(dg_o)

        xv = x_ref[...]
        _, r = _rms(xv, g_ref[...])
        dx, dg = _rms_bwd(dh_ref[...], xv, g_ref[...], r)
        dx_o[...] = dx if res is None else dx + refs[3][...]
        dg_o[...] += dg

    t1k = pl.BlockSpec((ts, D), lambda i: (i, 0))
    gsp = pl.BlockSpec((1, D), lambda i: (0, 0))
    ops = [dh, x, g] + ([] if res is None else [res])
    return pl.pallas_call(
        body, name=name, grid=(rows // ts,), in_specs=[t1k, t1k, gsp] + ([] if res is None else [t1k]),
        out_specs=[t1k, gsp],
        out_shape=[jax.ShapeDtypeStruct((rows, D), F32), jax.ShapeDtypeStruct((1, D), F32)],
        compiler_params=_cp(("arbitrary",)),
    )(*ops)


def _adamw(name, gall, w, m, v):
    rows = w.shape[0]
    nsrc = gall.shape[0]
    tr = min(SLAB_TILE, rows)
    assert rows % tr == 0

    def body(g_ref, w_ref, m_ref, v_ref, g_o, d_o, m_o, v_o):
        g = g_ref[0].astype(F32)
        for j in range(1, nsrc):
            g = g + g_ref[j].astype(F32)
        m_new = ADAM_B1 * m_ref[...] + (1.0 - ADAM_B1) * g
        v_new = ADAM_B2 * v_ref[...] + (1.0 - ADAM_B2) * jnp.square(g)
        m_hat = m_new / (1.0 - ADAM_B1 ** ADAM_STEP)
        v_hat = v_new / (1.0 - ADAM_B2 ** ADAM_STEP)
        g_o[...] = g
        d_o[...] = -ADAM_LR * (m_hat / (jnp.sqrt(v_hat) + ADAM_EPS) + ADAM_WD * w_ref[...])
        m_o[...] = m_new
        v_o[...] = v_new

    t = pl.BlockSpec((tr, LANES), lambda i: (i, 0))
    o = jax.ShapeDtypeStruct((rows, LANES), F32)
    return pl.pallas_call(
        body, name=name, grid=(rows // tr,),
        in_specs=[pl.BlockSpec((nsrc, tr, LANES), lambda i: (0, i, 0)), t, t, t],
        out_specs=[t, t, t, t], out_shape=[o, o, o, o],
        compiler_params=_cp(("parallel",)),
    )(gall, w, m, v)


def _pair_sum(name, mine, theirs):
    rows = mine.shape[1]
    tr = min(SLAB_TILE, rows)
    assert rows % tr == 0
    core = lax.axis_index("c").astype(jnp.int32).reshape(1)

    def body(c_ref, a_ref, b_ref, o_ref):
        o_ref[...] = (a_ref[...].astype(F32) + b_ref[...].astype(F32)).astype(o_ref.dtype)

    blk = pl.BlockSpec((1, tr, LANES), lambda j, i, c_ref: (j, i, 0))
    return pl.pallas_call(
        body, name=name,
        grid_spec=pltpu.PrefetchScalarGridSpec(
            num_scalar_prefetch=1, grid=(NDEV // 2, rows // tr),
            in_specs=[pl.BlockSpec((1, tr, LANES), lambda j, i, c_ref: (2 * j + c_ref[0], i, 0)), blk],
            out_specs=blk),
        out_shape=jax.ShapeDtypeStruct((NDEV // 2, rows, LANES), mine.dtype),
        compiler_params=_cp(("parallel", "parallel")),
    )(core, mine, theirs)


HBM_SPEC = pl.BlockSpec(memory_space=pltpu.HBM)


def _remote(src, dst, send_sems, recv_sems, k, to):
    return pltpu.make_async_remote_copy(src_ref=src, dst_ref=dst, send_sem=send_sems.at[k], recv_sem=recv_sems.at[k],
                                        device_id=to, device_id_type=pl.DeviceIdType.MESH)


def _gather_steps(x_ref, o_ref, send_sems, recv_sems, local_sem):
    ix, iy, ic = lax.axis_index("x"), lax.axis_index("y"), lax.axis_index("c")
    me, sibling = (ix, iy, ic), (ix, iy, 1 - ic)
    chips = [(1 - ix, iy), (ix, 1 - iy), (1 - ix, 1 - iy)]

    def slab(px, py, pc):
        return o_ref.at[4 * px + 2 * py + pc]

    def copy(k, block, to, src=None):
        return _remote(slab(*block) if src is None else src, slab(*block), send_sems, recv_sems, k, to)

    def mine():
        return pltpu.make_async_copy(x_ref, slab(*me), local_sem)

    def first():
        return [copy(0, me, sibling, src=x_ref)] + [copy(1 + j, me, (*chip, ic), src=x_ref)
                                                    for j, chip in enumerate(chips)]

    def passed():
        return [copy(4 + j, (*chip, ic), sibling) for j, chip in enumerate(chips)]

    def start():
        mine().start()
        for cp in first():
            cp.start()

    def forward():
        for j, (chip, cp) in enumerate(zip(chips, passed())):
            copy(1 + j, (*chip, ic), me).wait_recv()
            cp.start()

    def finish():
        copy(0, sibling, me).wait_recv()
        for j, chip in enumerate(chips):
            copy(4 + j, (*chip, 1 - ic), me).wait_recv()
        for cp in first() + passed():
            cp.wait_send()
        mine().wait()

    return start, forward, finish


GATHER_SEMS = [pltpu.SemaphoreType.DMA((NDEV - 1,)), pltpu.SemaphoreType.DMA((NDEV - 1,)), pltpu.SemaphoreType.DMA]


def _gather(name, x):
    rows, cols = x.shape

    def body(x_ref, o_ref, send_sems, recv_sems, local_sem):
        for step in _gather_steps(x_ref, o_ref, send_sems, recv_sems, local_sem):
            step()

    return pl.pallas_call(
        body, name=name, in_specs=[HBM_SPEC], out_specs=HBM_SPEC,
        out_shape=jax.ShapeDtypeStruct((NDEV, rows, cols), x.dtype), scratch_shapes=list(GATHER_SEMS),
    )(x)


def _sibling_exchange(name, x):
    rows, cols = x.shape[-2:]
    nchip = NDEV // 2

    def body(x_ref, o_ref, send_sems, recv_sems):
        ix, iy, ic = lax.axis_index("x"), lax.axis_index("y"), lax.axis_index("c")
        copies = [_remote(x_ref.at[2 * j + (1 - ic)], o_ref.at[j], send_sems, recv_sems, j, (ix, iy, 1 - ic))
                  for j in range(nchip)]
        for cp in copies:
            cp.start()
        for cp in copies:
            cp.wait()

    return pl.pallas_call(
        body, name=name, in_specs=[HBM_SPEC], out_specs=HBM_SPEC,
        out_shape=jax.ShapeDtypeStruct((nchip, rows, cols), x.dtype),
        scratch_shapes=[pltpu.SemaphoreType.DMA((nchip,)), pltpu.SemaphoreType.DMA((nchip,))],
    )(x)


def _chip_steps(x_ref, o_ref, send_sems, recv_sems, local_sem):
    ix, iy, ic = lax.axis_index("x"), lax.axis_index("y"), lax.axis_index("c")
    my_chip = 2 * ix + iy

    def own():
        return pltpu.make_async_copy(x_ref.at[my_chip], o_ref.at[my_chip], local_sem)

    def copies():
        out = []
        for k in range(1, NDEV // 2):
            px, py = ix ^ (k >> 1), iy ^ (k & 1)
            out.append(_remote(x_ref.at[2 * px + py], o_ref.at[my_chip], send_sems, recv_sems, k - 1, (px, py, ic)))
        return out

    def start():
        own().start()
        for cp in copies():
            cp.start()

    def finish():
        for cp in copies():
            cp.wait()
        own().wait()

    return start, (lambda: None), finish


CHIP_SEMS = [pltpu.SemaphoreType.DMA((NDEV // 2 - 1,)), pltpu.SemaphoreType.DMA((NDEV // 2 - 1,)),
             pltpu.SemaphoreType.DMA]


COL_SHARDED = {"w_in": (D, D_IN), "w_br_gdn": (HW, D), "w_br_sb": (HW, D), "w_br_mem": (HW, D), "w_up": (D, DFF),
               "conv_w": (4, 3 * HW)}
ROW_SHARDED = {"w_mem_kv": (D, 2 * HW), "w_o": (D, D), "w_down": (DFF, D)}


def _to_slab(p):
    return p.reshape(p.shape[:-2] + (-1, LANES))


def _from_slab(flat, r, c):
    return flat.reshape(flat.shape[:-2] + (r, c))


def _shard_dims(name):
    if name in COL_SHARDED:
        r, c = COL_SHARDED[name]
        return r, c // NDEV
    r, c = ROW_SHARDED[name]
    return r // NDEV, c


def _pack_rows(parts, total):
    flat = jnp.concatenate(parts, axis=-2)
    return jnp.pad(flat, [(0, 0)] * (flat.ndim - 2) + [(0, total - flat.shape[-2]), (0, 0)])


def _pack_shards(vals, names, total):
    return _pack_rows([_to_slab(vals[n][0]) for n in names], total)


def _pack_full_grads(grads, names, total):
    parts = []
    for name in names:
        g = grads[name]
        r, c = _shard_dims(name)
        if name in COL_SHARDED:
            g = g.reshape(r, NDEV, c).transpose(1, 0, 2)
        else:
            g = g.reshape(NDEV, r, c)
        parts.append(_to_slab(g))
    return _pack_rows(parts, total)


def _unpack_gathered(slabs, names):
    out, pos = {}, 0
    for name in names:
        rows = SLAB_ROWS[name]
        r, c = _shard_dims(name)
        g = _from_slab(slabs[:, pos:pos + rows], r, c)
        pos += rows
        if name in COL_SHARDED:
            out[name] = g.transpose(1, 0, 2).reshape(r, NDEV * c)
        else:
            out[name] = g.reshape(NDEV * r, c)
    return out


def _unpack_shard(flat, names, shapes):
    out, pos = {}, 0
    for name in names:
        rows = SLAB_ROWS[name]
        r, c = _shard_dims(name)
        out[name] = _from_slab(flat[pos:pos + rows], r, c).reshape(shapes[name])
        pos += rows
    return out


def _pack_vec(vals):
    row = jnp.concatenate([vals[n] for n in VEC], axis=1)
    return jnp.pad(row, ((0, 0), (0, VEC_WIDTH - row.shape[1])))


def _adamw_vec(gall, w, m, v):
    aligned = [(off, n) for off, n in zip(VEC_OFFSETS, VEC_SIZES) if n % LANES == 0]

    def body(g_ref, w_ref, m_ref, v_ref, *outs):
        g = g_ref[0]
        for j in range(1, NDEV):
            g = g + g_ref[j]
        m_new = ADAM_B1 * m_ref[...] + (1.0 - ADAM_B1) * g
        v_new = ADAM_B2 * v_ref[...] + (1.0 - ADAM_B2) * jnp.square(g)
        m_hat = m_new / (1.0 - ADAM_B1 ** ADAM_STEP)
        v_hat = v_new / (1.0 - ADAM_B2 ** ADAM_STEP)
        delta = -ADAM_LR * (m_hat / (jnp.sqrt(v_hat) + ADAM_EPS) + ADAM_WD * w_ref[...])
        for r, val in enumerate((g, delta, m_new, v_new)):
            outs[r][...] = val
            for i, (off, n) in enumerate(aligned):
                outs[4 + r * len(aligned) + i][...] = val[:, off:off + n]

    full = lambda *shape: pl.BlockSpec(shape, lambda: (0,) * len(shape))
    row = jax.ShapeDtypeStruct((1, VEC_WIDTH), F32)
    out_shape = [row] * 4 + [jax.ShapeDtypeStruct((1, n), F32) for _ in range(4) for _, n in aligned]
    out_specs = [full(1, VEC_WIDTH)] * 4 + [full(1, n) for _ in range(4) for _, n in aligned]
    return pl.pallas_call(
        body, name="adamw_replicated",
        in_specs=[full(NDEV, 1, VEC_WIDTH), full(1, VEC_WIDTH), full(1, VEC_WIDTH), full(1, VEC_WIDTH)],
        out_specs=out_specs, out_shape=out_shape,
    )(gall, w, m, v)


def _unpack_vec(outs, r):
    aligned = [name for name, n in zip(VEC, VEC_SIZES) if n % LANES == 0]
    vals = {name: outs[4 + r * len(aligned) + i] for i, name in enumerate(aligned)}
    for name, off, n in zip(VEC, VEC_OFFSETS, VEC_SIZES):
        if name not in vals:
            vals[name] = outs[r][:, off:off + n]
    return vals


def _pad_w_in(w):
    return jnp.concatenate([w[:, :2048], w[:, 2056:], w[:, 2048:2056], jnp.zeros((D, D_INP - D_IN), w.dtype)], axis=1)


def _unpad_w_in(w):
    return jnp.concatenate([w[:, :2048], w[:, 7168:7176], w[:, 2048:7168]], axis=1)


def _per_head(v):
    return jnp.repeat(v.reshape(NH), DH).reshape(1, HW)


def _local_step(x, mem, target, w, sm, rest_shards):
    S = x.shape[0]
    ts = _row_tile(S)
    tb = 2 * ts
    alog_f, dtb_f = _per_head(sm["a_log"]), _per_head(sm["dt_bias"])
    w = dict(w)

    h1 = _norm_cast("norm1", x, sm["norm1_g"])
    proj, rest = _mm("in_proj", h1, w["w_in"], "nn", tb, 1536, D, n_outer=True, comm=("gather", rest_shards))
    w.update(_unpack_gathered(rest[:, :sum(SLAB_ROWS[n] for n in REST)], REST))
    gq, gk, gv, gf, bf, sqn, skn, svb, qmn = _pre_fwd(proj, w["conv_w"], alog_f, dtb_f, sm["sb_q_norm_g"],
                                                      sm["sb_k_norm_g"], sm["mem_q_norm_g"], S)
    ogdn, states = _gdn_fwd(gq, gk, gv, gf, bf, S)
    osb, sb_tot, sb_walked = _sb_fwd(sqn, skn, svb, S)
    kv = _mm("mem_kv", mem, w["w_mem_kv"], "nn", NMEM, D, D, pro="rms", pro_g=sm["mem_norm_g"])
    omem = _mem_fwd(qmn, kv, sm["mem_k_norm_g"], S)
    x1, mix = _merge_fwd(x, proj, ogdn, osb, omem, sm["gdn_norm_g"], w["w_br_gdn"], w["w_br_sb"], w["w_br_mem"],
                         w["w_o"], S)
    h2 = _norm_cast("norm2", x1, sm["norm2_g"])
    up = _mm("mlp_up", h2, w["w_up"], "nn", tb, 2048, D, n_outer=True)
    dy, loss = _mm("mlp_down", up, w["w_down"], "nn", tb, D, 1024, pro="relu2", epi="loss", epi_x=(x1, target))

    g = {}
    dup = _mm("d_up", dy, w["w_down"], "nt", tb, 1024, D, epi="drelu2", epi_x=up, out_dtype=BF16)
    g["w_down"] = _mm("dw_down", up, dy, "tn", 1024, D, 1024, pro="relu2")
    g["w_up"] = _mm("dw_up", h2, dup, "tn", D, 1024, 1024)
    dx1, g["norm2_g"] = _mm("d_h2", dup, w["w_up"], "nt", tb, D, 1024, epi="rms_bwd", epi_x=(x1, sm["norm2_g"], dy))

    dmix = _mm("d_mix", dx1, w["w_o"], "nt", tb, D, D)
    g["w_o"] = _mm("dw_o", mix, dx1, "tn", D, D, 1024)
    (dgl0, dgl1, dgl2, dogdn, dz, dosb, domem, g["w_br_gdn"], g["w_br_sb"], g["w_br_mem"],
     g["gdn_norm_g"]) = _merge_bwd(dmix, proj, ogdn, osb, omem, sm["gdn_norm_g"], w["w_br_gdn"], w["w_br_sb"],
                                   w["w_br_mem"], S)
    dmq, dkv, g["mem_q_norm_g"], g["mem_k_norm_g"] = _mem_bwd(proj, qmn, kv, sm["mem_q_norm_g"], sm["mem_k_norm_g"],
                                                             domem, S)
    g["w_mem_kv"] = _mm("dw_mem_kv", mem, dkv, "tn", D, D, NMEM, pro="rms", pro_g=sm["mem_norm_g"])
    dmn = _mm("d_mem_n", dkv, w["w_mem_kv"], "nt", NMEM, D, D)
    _, g["mem_norm_g"] = _norm_bwd("mem_norm_bwd", dmn, mem, sm["mem_norm_g"], None)
    dsqn, dskn, dsv = _sb_bwd(sqn, skn, svb, dosb, sb_tot, sb_walked, S)
    dgq, dgk, dgv, dgf, dbf = _gdn_bwd(gq, gk, gv, gf, bf, states, dogdn, S)
    dc, dab, dsq, dsk, g["conv_w"], dal_f, ddt_f, g["sb_q_norm_g"], g["sb_k_norm_g"] = _pre_bwd(
        proj, w["conv_w"], alog_f, dtb_f, sm["sb_q_norm_g"], sm["sb_k_norm_g"], dgq, dgk, dgv, dgf, dbf, dsqn, dskn, S)
    g["a_log"] = dal_f.reshape(NH, DH)[:, 0].reshape(1, NH)
    g["dt_bias"] = ddt_f.reshape(NH, DH)[:, 0].reshape(1, NH)
    dqkv = _conv_bwd(dc, w["conv_w"], S)

    dproj = jnp.concatenate([dqkv, dz, dsq, dsk, dsv, dmq, dgl0, dgl1, dgl2, dab], axis=1)
    rest_mine = _pack_full_grads(g, REST, R_REST).astype(BF16)
    rest_pair = _pair_sum("pair_sum_rest", rest_mine, _sibling_exchange("scatter_sibling_rest", rest_mine))
    g["w_in"], rest_all = _mm("dw_in", h1, dproj, "tn", D, 1536, 1024, comm=("chips", rest_pair))
    g["w_in"] = _unpad_w_in(g["w_in"])
    first_mine = _pack_full_grads(g, FIRST, R_FIRST).astype(BF16)
    first_pair = _pair_sum("pair_sum_first", first_mine, _sibling_exchange("scatter_sibling_first", first_mine))
    dx, g["norm1_g"], first_all = _mm("d_h", dproj, w["w_in"], "nt", tb, D, 1536, epi="rms_bwd",
                                      epi_x=(x, sm["norm1_g"], dx1), comm=("chips", first_pair))
    return loss[0, 0], dx, g, rest_all, first_all


def kernel(x, mem, norm1_g, w_in, conv_w, a_log, dt_bias, gdn_norm_g, sb_q_norm_g, sb_k_norm_g, mem_norm_g, w_mem_kv, mem_q_norm_g, mem_k_norm_g, w_br_gdn, w_br_sb, w_br_mem, w_o, norm2_g, w_up, w_down, loss_target, m_norm1_g, m_w_in, m_conv_w, m_a_log, m_dt_bias, m_gdn_norm_g, m_sb_q_norm_g, m_sb_k_norm_g, m_mem_norm_g, m_w_mem_kv, m_mem_q_norm_g, m_mem_k_norm_g, m_w_br_gdn, m_w_br_sb, m_w_br_mem, m_w_o, m_norm2_g, m_w_up, m_w_down, v_norm1_g, v_w_in, v_conv_w, v_a_log, v_dt_bias, v_gdn_norm_g, v_sb_q_norm_g, v_sb_k_norm_g, v_mem_norm_g, v_w_mem_kv, v_mem_q_norm_g, v_mem_k_norm_g, v_w_br_gdn, v_w_br_sb, v_w_br_mem, v_w_o, v_norm2_g, v_w_up, v_w_down):
    given = dict(norm1_g=norm1_g, w_in=w_in, conv_w=conv_w, a_log=a_log, dt_bias=dt_bias, gdn_norm_g=gdn_norm_g,
                 sb_q_norm_g=sb_q_norm_g, sb_k_norm_g=sb_k_norm_g, mem_norm_g=mem_norm_g, w_mem_kv=w_mem_kv,
                 mem_q_norm_g=mem_q_norm_g, mem_k_norm_g=mem_k_norm_g, w_br_gdn=w_br_gdn, w_br_sb=w_br_sb,
                 w_br_mem=w_br_mem, w_o=w_o, norm2_g=norm2_g, w_up=w_up, w_down=w_down)
    mom1 = dict(norm1_g=m_norm1_g, w_in=m_w_in, conv_w=m_conv_w, a_log=m_a_log, dt_bias=m_dt_bias,
                gdn_norm_g=m_gdn_norm_g, sb_q_norm_g=m_sb_q_norm_g, sb_k_norm_g=m_sb_k_norm_g,
                mem_norm_g=m_mem_norm_g, w_mem_kv=m_w_mem_kv, mem_q_norm_g=m_mem_q_norm_g,
                mem_k_norm_g=m_mem_k_norm_g, w_br_gdn=m_w_br_gdn, w_br_sb=m_w_br_sb, w_br_mem=m_w_br_mem, w_o=m_w_o,
                norm2_g=m_norm2_g, w_up=m_w_up, w_down=m_w_down)
    mom2 = dict(norm1_g=v_norm1_g, w_in=v_w_in, conv_w=v_conv_w, a_log=v_a_log, dt_bias=v_dt_bias,
                gdn_norm_g=v_gdn_norm_g, sb_q_norm_g=v_sb_q_norm_g, sb_k_norm_g=v_sb_k_norm_g,
                mem_norm_g=v_mem_norm_g, w_mem_kv=v_w_mem_kv, mem_q_norm_g=v_mem_q_norm_g,
                mem_k_norm_g=v_mem_k_norm_g, w_br_gdn=v_w_br_gdn, w_br_sb=v_w_br_sb, w_br_mem=v_w_br_mem, w_o=v_w_o,
                norm2_g=v_norm2_g, w_up=v_w_up, w_down=v_w_down)
    shapes = {n: given[n].shape for n in WEIGHTS}

    first_loc = _pack_shards(given, FIRST, R_FIRST)
    rest_loc = _pack_shards(given, REST, R_REST)
    gathered = _gather("gather_first", first_loc.astype(BF16))
    w = _unpack_gathered(gathered[:, :sum(SLAB_ROWS[n] for n in FIRST)], FIRST)
    w["w_in"] = _pad_w_in(w["w_in"])
    conv_loc = jnp.pad(given["conv_w"][0].reshape(-1, LANES), ((0, 2), (0, 0)))
    conv_all = _gather("gather_conv", conv_loc)
    w["conv_w"] = conv_all[:, :6].reshape(NDEV, 4, 3 * HW // NDEV).transpose(1, 0, 2).reshape(4, 3 * HW)
    sm = {n: given[n] for n in SMALL}

    loss, dx, g, rest_all, first_all = _local_step(x[0], mem[0], loss_target[0], w, sm, rest_loc.astype(BF16))
    res_first = _adamw("adamw_first", first_all, first_loc, _pack_shards(mom1, FIRST, R_FIRST),
                       _pack_shards(mom2, FIRST, R_FIRST))
    res_rest = _adamw("adamw_rest", rest_all, rest_loc, _pack_shards(mom1, REST, R_REST),
                      _pack_shards(mom2, REST, R_REST))
    gs_all = _gather("gather_small_grads", _pack_vec(g))
    vec_outs = _adamw_vec(gs_all, _pack_vec(given), _pack_vec(mom1), _pack_vec(mom2))

    outs = {}
    for r, prefix in enumerate(("grad_", "delta_", "new_m_", "new_v_")):
        vals = _unpack_shard(res_first[r], FIRST, shapes)
        vals.update(_unpack_shard(res_rest[r], REST, shapes))
        vals.update(_unpack_vec(vec_outs, r))
        for n in WEIGHTS:
            outs[prefix + n] = vals[n]
    loss = lax.psum(loss, ("x", "y", "c"))
    return (loss, dx[None], *[outs[p + n] for p in ("grad_", "delta_", "new_m_", "new_v_") for n in WEIGHTS])
```

```python
import jax
import jax.numpy as jnp
from jax import lax
from jax.experimental import pallas as pl
from jax.experimental.pallas import tpu as pltpu

F32 = jnp.float32
BF16 = jnp.bfloat16

D = 1024
NH = 4
DH = 128
HW = NH * DH
DFF = 4 * D
NMEM = 256
EPS = 1e-6
NDEV = 8
LANES = 128
PAIR = 128
CHUNK = 64
D_IN = 7176
D_INP = 7680
VMEM_LIMIT = 56 * 1024 * 1024

ADAM_LR, ADAM_B1, ADAM_B2, ADAM_EPS, ADAM_WD, ADAM_STEP = 0.001, 0.9, 0.999, 1e-08, 0.01, 10

CB_Z, CB_SQ, CB_SK, CB_SV, CB_MQ, CB_AB = 3, 4, 5, 6, 7, 14

NN = (((1,), (0,)), ((), ()))
NT = (((1,), (1,)), ((), ()))
TN = (((0,), (0,)), ((), ()))

BIG = ("w_in", "w_mem_kv", "w_br_gdn", "w_br_sb", "w_br_mem", "w_o", "w_up", "w_down", "conv_w")
BIG_ROWS = (7176, 1024, 512, 512, 512, 1024, 4096, 4096, 6)
SLAB_ROWS = dict(zip(BIG, BIG_ROWS))
SLAB_TILE = 1216
FIRST = ("w_in", "conv_w")
REST = ("w_mem_kv", "w_br_gdn", "w_br_sb", "w_br_mem", "w_o", "w_up", "w_down")
R_FIRST = 6 * SLAB_TILE
R_REST = 10 * SLAB_TILE
SMALL = ("norm1_g", "a_log", "dt_bias", "gdn_norm_g", "sb_q_norm_g", "sb_k_norm_g", "mem_norm_g",
         "mem_q_norm_g", "mem_k_norm_g", "norm2_g")
VEC = ("norm1_g", "mem_norm_g", "norm2_g", "gdn_norm_g", "sb_q_norm_g", "sb_k_norm_g", "mem_q_norm_g", "mem_k_norm_g",
       "a_log", "dt_bias")
VEC_SIZES = (1024, 1024, 1024, 128, 128, 128, 128, 128, 4, 4)
VEC_OFFSETS = (0, 1024, 2048, 3072, 3200, 3328, 3456, 3584, 3712, 3716)
VEC_WIDTH = 3840
WEIGHTS = ("norm1_g", "w_in", "conv_w", "a_log", "dt_bias", "gdn_norm_g", "sb_q_norm_g", "sb_k_norm_g",
           "mem_norm_g", "w_mem_kv", "mem_q_norm_g", "mem_k_norm_g", "w_br_gdn", "w_br_sb", "w_br_mem",
           "w_o", "norm2_g", "w_up", "w_down")


def _cp(sem=None):
    return pltpu.CompilerParams(dimension_semantics=sem, vmem_limit_bytes=VMEM_LIMIT)


def _dot(a, b, dims=NN):
    return lax.dot_general(a, b, dims, preferred_element_type=F32)


def _dbf(a, b, dims=NN):
    return _dot(a.astype(BF16), b.astype(BF16), dims)


def _split(a, n):
    parts = []
    for _ in range(n):
        h = a.astype(BF16)
        parts.append(h)
        a = a - h.astype(F32)
    return parts


def _dg(a, b, dims=NN):
    return _dbf(a, b, dims)


def _dxr(a, e, dims=NN):
    eb = e.astype(BF16)
    a1, a2, a3 = _split(a, 3)
    return _dot(a1, eb, dims) + (_dot(a2, eb, dims) + _dot(a3, eb, dims))


def _dxl(e, a, dims=NN):
    eb = e.astype(BF16)
    a1, a2, a3 = _split(a, 3)
    return _dot(eb, a1, dims) + (_dot(eb, a2, dims) + _dot(eb, a3, dims))


def _sigmoid(x):
    return 1.0 / (1.0 + jnp.exp(-x))


def _softplus(x):
    return jnp.maximum(x, 0.0) + jnp.log(1.0 + jnp.exp(-jnp.abs(x)))


def _rms(x, g):
    r = lax.rsqrt(jnp.mean(x * x, axis=-1, keepdims=True) + EPS)
    return x * r * g, r


def _rms_bwd(dy, x, g, r):
    dyg = dy * g
    dx = r * (dyg - x * (r * r) * jnp.mean(dyg * x, axis=-1, keepdims=True))
    dg = jnp.sum(dy * (x * r), axis=0, keepdims=True)
    return dx, dg


def _hs(h):
    return slice(h * DH, (h + 1) * DH)


def _row_tile(s):
    return 512 if s >= 2048 else 256


def _narrow_tile(s):
    return min(256, s)


def _mm(name, a, b, mode, tm, tn, tk, pro=None, pro_g=None, epi=None, epi_x=None, out_dtype=F32, n_outer=False,
        comm=None):
    if mode == "tn":
        K, M = a.shape
    else:
        M, K = a.shape
    N = b.shape[0] if mode == "nt" else b.shape[1]
    tm, tn, tk = min(tm, M), min(tn, N), min(tk, K)
    nk = K // tk
    assert M % tm == 0 and N % tn == 0 and K % tk == 0, (name, M, N, K, tm, tn, tk)
    dims = {"nn": NN, "nt": NT, "tn": TN}[mode]
    reducing = epi in ("rms_bwd", "loss")
    assert not reducing or (tn == N and not n_outer), name
    epi_ops = () if epi is None else (epi_x if isinstance(epi_x, tuple) else (epi_x,))

    def body(*refs):
        a_ref, b_ref = refs[0], refs[1]
        pos = 2
        g_ref = None
        if pro == "rms":
            g_ref = refs[pos]
            pos += 1
        e_refs = refs[pos:pos + len(epi_ops)]
        pos += len(epi_ops)
        cx_ref = None
        if comm is not None:
            cx_ref = refs[pos]
            pos += 1
        o_ref = refs[pos]
        pos += 1
        r_ref = None
        if reducing:
            r_ref = refs[pos]
            pos += 1
        if comm is not None:
            steps_of = _gather_steps if comm[0] == "gather" else _chip_steps
            start, forward, finish_comm = steps_of(cx_ref, refs[pos], *refs[-3:])
            pos += 1
            step = (pl.program_id(0) * grid[1] + pl.program_id(1)) * nk + pl.program_id(2)
            total = grid[0] * grid[1] * nk
            pl.when(step == 0)(start)
            pl.when(step == (4 * total) // 5)(forward)
        av = a_ref[...]
        if pro == "rms":
            av, _ = _rms(av.astype(F32), g_ref[...])
        elif pro == "relu2":
            av = jnp.square(jnp.maximum(av, 0.0))
        part = _dbf(av, b_ref[...], dims)
        first = pl.program_id(0) == 0

        def finish(acc):
            red = None
            if epi == "add":
                acc = acc + e_refs[0][...]
            elif epi == "drelu2":
                acc = acc * (2.0 * jnp.maximum(e_refs[0][...], 0.0))
            elif epi == "rms_bwd":
                xv, gv = e_refs[0][...], e_refs[1][...]
                _, r = _rms(xv, gv)
                dx, red = _rms_bwd(acc, xv, gv, r)
                acc = dx + e_refs[2][...]
            elif epi == "loss":
                err = acc + e_refs[0][...] - e_refs[1][...]
                acc = err * (1.0 / N)
                per_tok = jnp.sum(err * err, axis=1, keepdims=True) * (1.0 / N)
                red = 0.5 * jnp.sum(per_tok, axis=0, keepdims=True)
            o_ref[...] = acc.astype(out_dtype)
            if reducing:

                @pl.when(first)
                def _():
                    r_ref[...] = red

                @pl.when(jnp.logical_not(first))
                def _():
                    r_ref[...] += red

        if nk == 1:
            finish(part)
        else:
            acc_ref = refs[pos]
            k = pl.program_id(2)

            @pl.when(k == 0)
            def _():
                acc_ref[...] = part

            @pl.when(k > 0)
            def _():
                acc_ref[...] += part

            @pl.when(k == nk - 1)
            def _():
                finish(acc_ref[...])

        if comm is not None:
            pl.when(step == total - 1)(finish_comm)

    def spec(shape, index):
        if n_outer:
            return pl.BlockSpec(shape, lambda j, i, k: index(i, j, k))
        return pl.BlockSpec(shape, index)

    if mode == "tn":
        a_spec = spec((tk, tm), lambda i, j, k: (k, i))
    else:
        a_spec = spec((tm, tk), lambda i, j, k: (i, k))
    if mode == "nt":
        b_spec = spec((tn, tk), lambda i, j, k: (j, k))
    else:
        b_spec = spec((tk, tn), lambda i, j, k: (k, j))
    in_specs, ops = [a_spec, b_spec], [a, b]
    if pro == "rms":
        w = pro_g.shape[1]
        assert (tm if mode == "tn" else tk) == w, name
        in_specs.append(spec((1, w), lambda i, j, k: (0, 0)))
        ops.append(pro_g)
    for op in epi_ops:
        if op.shape[0] == 1:
            in_specs.append(spec((1, tn), lambda i, j, k: (0, j)))
        else:
            in_specs.append(spec((tm, tn), lambda i, j, k: (i, j)))
        ops.append(op)
    out_specs = [spec((tm, tn), lambda i, j, k: (i, j))]
    out_shape = [jax.ShapeDtypeStruct((M, N), out_dtype)]
    if reducing:
        width = N if epi == "rms_bwd" else 1
        out_specs.append(spec((1, width), lambda i, j, k: (0, 0)))
        out_shape.append(jax.ShapeDtypeStruct((1, width), F32))
    scratch = [pltpu.VMEM((tm, tn), F32)] if nk > 1 else []
    if comm is not None:
        kind, cx = comm
        in_specs.append(HBM_SPEC)
        ops.append(cx)
        out_specs.append(HBM_SPEC)
        out_shape.append(jax.ShapeDtypeStruct((NDEV if kind == "gather" else NDEV // 2,) + cx.shape[-2:], cx.dtype))
        scratch += list(GATHER_SEMS if kind == "gather" else CHIP_SEMS)
    grid = (N // tn, M // tm, nk) if n_outer else (M // tm, N // tn, nk)
    ordered = reducing or comm is not None
    outs = pl.pallas_call(
        body, name=name, grid=grid,
        in_specs=in_specs, out_specs=out_specs, out_shape=out_shape, scratch_shapes=scratch,
        compiler_params=_cp(("arbitrary" if ordered else "parallel", "arbitrary" if comm is not None else "parallel",
                             "arbitrary")),
    )(*ops)
    return outs if len(out_shape) > 1 else outs[0]


def _head_select(first_lane):
    l = lax.broadcasted_iota(jnp.int32, (LANES, HW), 0)
    c = lax.broadcasted_iota(jnp.int32, (LANES, HW), 1)
    return (l == first_lane + c // DH).astype(F32)


def _conv_taps(buf, cw, ts):
    c = cw[3:4, :] * buf[8:8 + ts, :]
    for j in range(3):
        k = 3 - j
        c = c + cw[j:j + 1, :] * buf[8 - k:8 - k + ts, :]
    return c


def _pre_fwd(proj, conv_w, alog_f, dtb_f, gsq, gsk, gmq, S):
    ts = _row_tile(S)
    hb = ts // 8

    def body(qkv_ref, halo_ref, ab_ref, sq_ref, sk_ref, sv_ref, mq_ref, cw_ref, al_ref, dt_ref, gsq_ref, gsk_ref,
             gmq_ref, gq_o, gk_o, gv_o, gf_o, bf_o, sqn_o, skn_o, svb_o, qmn_o, buf):
        i = pl.program_id(0)
        buf[0:8, :] = jnp.where(i == 0, 0.0, halo_ref[...])
        buf[8:8 + ts, :] = qkv_ref[...]
        c = _conv_taps(buf, cw_ref[...], ts)
        a = c * _sigmoid(c)
        for h in range(NH):
            q = a[:, h * DH:(h + 1) * DH]
            k = a[:, HW + h * DH:HW + (h + 1) * DH]
            gq_o[:, _hs(h)] = q * (lax.rsqrt(jnp.sum(q * q, axis=-1, keepdims=True) + EPS) * DH ** -0.5)
            gk_o[:, _hs(h)] = k * lax.rsqrt(jnp.sum(k * k, axis=-1, keepdims=True) + EPS)
            sqn_o[:, _hs(h)] = _rms(sq_ref[:, _hs(h)], gsq_ref[...])[0].astype(BF16)
            skn_o[:, _hs(h)] = _rms(sk_ref[:, _hs(h)], gsk_ref[...])[0].astype(BF16)
            qmn_o[:, _hs(h)] = _rms(mq_ref[:, _hs(h)], gmq_ref[...])[0].astype(BF16)
        gv_o[...] = a[:, 2 * HW:3 * HW]
        svb_o[...] = sv_ref[...].astype(BF16)
        ab = ab_ref[:, 0:LANES]
        a_bc = _dxr(ab, _head_select(0))
        b_bc = _dxr(ab, _head_select(NH))
        gf_o[...] = -jnp.exp(al_ref[...]) * _softplus(a_bc + dt_ref[...])
        bf_o[...] = _sigmoid(b_bc)

    row = lambda cb: pl.BlockSpec((ts, HW), lambda i: (i, cb))
    full = lambda r, c: pl.BlockSpec((r, c), lambda i: (0, 0))
    f32o = jax.ShapeDtypeStruct((S, HW), F32)
    bfo = jax.ShapeDtypeStruct((S, HW), BF16)
    return pl.pallas_call(
        body, name="pre_fwd", grid=(S // ts,),
        in_specs=[pl.BlockSpec((ts, 3 * HW), lambda i: (i, 0)),
                  pl.BlockSpec((8, 3 * HW), lambda i: (jnp.maximum(i * hb - 1, 0), 0)),
                  row(CB_AB), row(CB_SQ), row(CB_SK), row(CB_SV), row(CB_MQ),
                  full(4, 3 * HW), full(1, HW), full(1, HW), full(1, DH), full(1, DH), full(1, DH)],
        out_specs=[pl.BlockSpec((ts, HW), lambda i: (i, 0))] * 9,
        out_shape=[f32o, f32o, f32o, f32o, f32o, bfo, bfo, bfo, bfo],
        scratch_shapes=[pltpu.VMEM((ts + 8, 3 * HW), F32)],
        compiler_params=_cp(("parallel",)),
    )(proj, proj, proj, proj, proj, proj, proj, conv_w, alog_f, dtb_f, gsq, gsk, gmq)


def _pre_bwd(proj, conv_w, alog_f, dtb_f, gsq, gsk, dgq, dgk, dgv, dgf, dbf, dsqn, dskn, S):
    ts = _narrow_tile(S)
    hb = ts // 8

    def body(qkv_ref, halo_ref, ab_ref, sq_ref, sk_ref, cw_ref, al_ref, dt_ref, gsq_ref, gsk_ref,
             dgq_ref, dgk_ref, dgv_ref, dgf_ref, dbf_ref, dsqn_ref, dskn_ref,
             dc_o, dab_o, dsq_o, dsk_o, dcw_o, dal_o, ddt_o, dgsq_o, dgsk_o, buf):
        i = pl.program_id(0)

        @pl.when(i == 0)
        def _():
            dcw_o[...] = jnp.zeros_like(dcw_o)
            dal_o[...] = jnp.zeros_like(dal_o)
            ddt_o[...] = jnp.zeros_like(ddt_o)
            dgsq_o[...] = jnp.zeros_like(dgsq_o)
            dgsk_o[...] = jnp.zeros_like(dgsk_o)

        buf[0:8, :] = jnp.where(i == 0, 0.0, halo_ref[...])
        buf[8:8 + ts, :] = qkv_ref[...]
        c = _conv_taps(buf, cw_ref[...], ts)
        sg = _sigmoid(c)
        a = c * sg
        dsilu = sg * (1.0 + c * (1.0 - sg))
        dgsq = jnp.zeros((1, DH), F32)
        dgsk = jnp.zeros((1, DH), F32)
        for h in range(NH):
            q = a[:, h * DH:(h + 1) * DH]
            k = a[:, HW + h * DH:HW + (h + 1) * DH]
            nq = lax.rsqrt(jnp.sum(q * q, axis=-1, keepdims=True) + EPS)
            nk = lax.rsqrt(jnp.sum(k * k, axis=-1, keepdims=True) + EPS)
            dyq = dgq_ref[:, _hs(h)]
            dyk = dgk_ref[:, _hs(h)]
            dq = (nq * dyq - q * (nq * nq * nq) * jnp.sum(dyq * q, axis=-1, keepdims=True)) * DH ** -0.5
            dk = nk * dyk - k * (nk * nk * nk) * jnp.sum(dyk * k, axis=-1, keepdims=True)
            dc_o[:, h * DH:(h + 1) * DH] = dq * dsilu[:, h * DH:(h + 1) * DH]
            dc_o[:, HW + h * DH:HW + (h + 1) * DH] = dk * dsilu[:, HW + h * DH:HW + (h + 1) * DH]
            x = sq_ref[:, _hs(h)]
            _, r = _rms(x, gsq_ref[...])
            dx, dg = _rms_bwd(dsqn_ref[:, _hs(h)], x, gsq_ref[...], r)
            dsq_o[:, _hs(h)] = dx.astype(BF16)
            dgsq = dgsq + dg
            x = sk_ref[:, _hs(h)]
            _, r = _rms(x, gsk_ref[...])
            dx, dg = _rms_bwd(dskn_ref[:, _hs(h)], x, gsk_ref[...], r)
            dsk_o[:, _hs(h)] = dx.astype(BF16)
            dgsk = dgsk + dg
        dc_o[:, 2 * HW:3 * HW] = dgv_ref[...] * dsilu[:, 2 * HW:3 * HW]
        dgsq_o[...] += dgsq
        dgsk_o[...] += dgsk
        dc = dc_o[...]
        for j in range(4):
            k = 3 - j
            dcw_o[j:j + 1, :] += jnp.sum(dc * buf[8 - k:8 - k + ts, :], axis=0, keepdims=True)
        ab = ab_ref[:, 0:LANES]
        a_bc = _dxr(ab, _head_select(0))
        b_bc = _dxr(ab, _head_select(NH))
        pre = a_bc + dt_ref[...]
        ea = jnp.exp(al_ref[...])
        dgf = dgf_ref[...]
        dal_o[...] += jnp.sum(dgf * (-ea * _softplus(pre)), axis=0, keepdims=True)
        da = dgf * (-ea * _sigmoid(pre))
        ddt_o[...] += jnp.sum(da, axis=0, keepdims=True)
        beta = _sigmoid(b_bc)
        db = dbf_ref[...] * beta * (1.0 - beta)
        lane = lax.broadcasted_iota(jnp.int32, (ts, LANES), 1)
        dab = jnp.zeros((ts, LANES), F32)
        for h in range(NH):
            dab = dab + jnp.where(lane == h, da[:, _hs(h)], 0.0) + jnp.where(lane == NH + h, db[:, _hs(h)], 0.0)
        dab_o[:, 0:LANES] = dab.astype(BF16)
        dab_o[:, LANES:HW] = jnp.zeros((ts, HW - LANES), BF16)

    row = lambda cb: pl.BlockSpec((ts, HW), lambda i: (i, cb))
    full = lambda r, c: pl.BlockSpec((r, c), lambda i: (0, 0))
    t512 = pl.BlockSpec((ts, HW), lambda i: (i, 0))
    return pl.pallas_call(
        body, name="pre_bwd", grid=(S // ts,),
        in_specs=[pl.BlockSpec((ts, 3 * HW), lambda i: (i, 0)),
                  pl.BlockSpec((8, 3 * HW), lambda i: (jnp.maximum(i * hb - 1, 0), 0)),
                  row(CB_AB), row(CB_SQ), row(CB_SK),
                  full(4, 3 * HW), full(1, HW), full(1, HW), full(1, DH), full(1, DH)] + [t512] * 7,
        out_specs=[pl.BlockSpec((ts, 3 * HW), lambda i: (i, 0)), t512, t512, t512,
                   full(4, 3 * HW), full(1, HW), full(1, HW), full(1, DH), full(1, DH)],
        out_shape=[jax.ShapeDtypeStruct((S, 3 * HW), F32)] + [jax.ShapeDtypeStruct((S, HW), BF16)] * 3
        + [jax.ShapeDtypeStruct((4, 3 * HW), F32), jax.ShapeDtypeStruct((1, HW), F32),
           jax.ShapeDtypeStruct((1, HW), F32), jax.ShapeDtypeStruct((1, DH), F32),
           jax.ShapeDtypeStruct((1, DH), F32)],
        scratch_shapes=[pltpu.VMEM((ts + 8, 3 * HW), F32)],
        compiler_params=_cp(("arbitrary",)),
    )(proj, proj, proj, proj, proj, conv_w, alog_f, dtb_f, gsq, gsk, dgq, dgk, dgv, dgf, dbf, dsqn, dskn)


def _conv_bwd(dc, conv_w, S):
    ts = _row_tile(S)
    hb = ts // 8
    n = S // ts

    def body(dc_ref, halo_ref, cw_ref, o_ref, buf):
        i = pl.program_id(0)
        buf[0:ts, :] = dc_ref[...]
        buf[ts:ts + 8, :] = jnp.where(i == n - 1, 0.0, halo_ref[...])
        cw = cw_ref[...]
        acc = cw[3:4, :] * buf[0:ts, :]
        for k in range(1, 4):
            acc = acc + cw[3 - k:4 - k, :] * buf[k:k + ts, :]
        o_ref[...] = acc.astype(BF16)

    return pl.pallas_call(
        body, name="conv_bwd", grid=(n,),
        in_specs=[pl.BlockSpec((ts, 3 * HW), lambda i: (i, 0)),
                  pl.BlockSpec((8, 3 * HW), lambda i: (jnp.minimum((i + 1) * hb, S // 8 - 1), 0)),
                  pl.BlockSpec((4, 3 * HW), lambda i: (0, 0))],
        out_specs=pl.BlockSpec((ts, 3 * HW), lambda i: (i, 0)),
        out_shape=jax.ShapeDtypeStruct((S, 3 * HW), BF16),
        scratch_shapes=[pltpu.VMEM((ts + 8, 3 * HW), F32)],
        compiler_params=_cp(("parallel",)),
    )(dc, dc, conv_w)


def _gdn_masks():
    r = lax.broadcasted_iota(jnp.int32, (PAIR, PAIR), 0)
    c = lax.broadcasted_iota(jnp.int32, (PAIR, PAIR), 1)
    same = ((r >= CHUNK) & (c >= CHUNK)) | ((r < CHUNK) & (c < CHUNK))
    return dict(r=r, same=same, tril=same & (r >= c), strict=same & (r > c), triu=same & (c >= r), eye=r == c,
                in_a=r < CHUNK, last_a=r == CHUNK - 1, last_b=r == PAIR - 1)


def _each(fn, *cols):
    return [fn(*xs) for xs in zip(*cols)]


def _mul(a, b):
    return a * b


def _top(x):
    return x[:CHUNK]


def _bot(x):
    return x[CHUNK:]


def _rows(a, b):
    return jnp.concatenate([a, b], axis=0)


def _tri_inv(lm, eye):
    eye_f = eye.astype(F32)
    p = _each(lambda l: eye_f - l, lm)
    lp = _each(lambda l: _dg(l, l), lm)
    for it in range(5):
        p = _each(lambda a, b: a + _dg(a, b), p, lp)
        if it < 4:
            lp = _each(lambda b: _dg(b, b), lp)
    return p


def _gdn_block(m, q, k, v, g, beta):
    tril_f = m["tril"].astype(F32)
    col_sum = lambda mask: (lambda x: jnp.sum(jnp.where(mask, x, 0.0), axis=0, keepdims=True))
    gc = _each(lambda x: _dxl(tril_f, x), g)
    gcr = _each(col_sum(m["eye"]), gc)
    gam = _each(lambda a, b: jnp.where(m["tril"], jnp.exp(jnp.minimum(a - b, 0.0)), 0.0), gc, gcr)
    kb = _each(_mul, k, beta)
    vb = _each(_mul, v, beta)
    lm = _each(lambda a, b, c: jnp.where(m["strict"], _dg(a, b, NT) * c, 0.0), kb, k, gam)
    t = _tri_inv(lm, m["eye"])
    eg = _each(jnp.exp, gc)
    kbe = _each(_mul, kb, eg)
    u = _each(_dg, t, vb)
    w = _each(_dg, t, kbe)
    aqk = _each(lambda a, b, c: jnp.where(m["tril"], _dg(a, b, NT) * c, 0.0), q, k, gam)
    qd = _each(_mul, q, eg)
    ga = _each(col_sum(m["last_a"]), gc)
    gb = _each(col_sum(m["last_b"]), gc)
    e2 = _each(lambda a, b, c: jnp.exp(jnp.where(m["in_a"], a, b) - c), ga, gb, gc)
    kd = _each(_mul, k, e2)
    return dict(u=u, w=w, aqk=aqk, qd=qd, kd=kd, gam=gam, kb=kb, vb=vb, lm=lm, t=t, eg=eg, kbe=kbe, e2=e2,
                gla=_each(jnp.exp, ga), glb=_each(jnp.exp, gb))


def _gdn_fwd(gq, gk, gv, gf, bf, S):
    nb = S // PAIR

    def body(q_ref, k_ref, v_ref, g_ref, b_ref, o_ref, st_ref, s_scr):
        @pl.when(pl.program_id(0) == 0)
        def _():
            s_scr[...] = jnp.zeros_like(s_scr)

        m = _gdn_masks()
        heads = lambda ref: [ref[:, _hs(h)] for h in range(NH)]
        f = _gdn_block(m, heads(q_ref), heads(k_ref), heads(v_ref), heads(g_ref), heads(b_ref))
        u, w, qd, kd = f["u"], f["w"], f["qd"], f["kd"]
        s0 = [s_scr[h * DH:(h + 1) * DH, :] for h in range(NH)]
        vna = _each(lambda a, b, s: _top(a) - _dg(_top(b), s), u, w, s0)
        oa = _each(lambda a, s: _dg(_top(a), s), qd, s0)
        s1 = _each(lambda s, gl, a, vn: s * gl + _dg(_top(a), vn, TN), s0, f["gla"], kd, vna)
        vnb = _each(lambda a, b, s: _bot(a) - _dg(_bot(b), s), u, w, s1)
        ob = _each(lambda a, s: _dg(_bot(a), s), qd, s1)
        s2 = _each(lambda s, gl, a, vn: s * gl + _dg(_bot(a), vn, TN), s1, f["glb"], kd, vnb)
        outs = _each(lambda a, b, c, va, vb: _rows(a, b) + _dg(c, _rows(va, vb)), oa, ob, f["aqk"], vna, vnb)
        o_ref[...] = jnp.concatenate(outs, axis=1)
        st_ref[...] = jnp.concatenate(s0 + s1, axis=0)
        s_scr[...] = jnp.concatenate(s2, axis=0)

    blk = pl.BlockSpec((PAIR, HW), lambda i: (i, 0))
    return pl.pallas_call(
        body, name="gdn_fwd", grid=(nb,),
        in_specs=[blk] * 5,
        out_specs=[blk, pl.BlockSpec((2 * NH * DH, DH), lambda i: (i, 0))],
        out_shape=[jax.ShapeDtypeStruct((S, HW), F32), jax.ShapeDtypeStruct((nb * 2 * NH * DH, DH), F32)],
        scratch_shapes=[pltpu.VMEM((NH * DH, DH), F32)],
        compiler_params=_cp(("arbitrary",)),
    )(gq, gk, gv, gf, bf)


def _gdn_bwd(gq, gk, gv, gf, bf, states, do, S):
    nb = S // PAIR

    def body(q_ref, k_ref, v_ref, g_ref, b_ref, st_ref, do_ref, dq_o, dk_o, dv_o, dg_o, db_o, ds_scr):
        @pl.when(pl.program_id(0) == 0)
        def _():
            ds_scr[...] = jnp.zeros_like(ds_scr)

        m = _gdn_masks()
        ones = jnp.ones((PAIR, PAIR), F32)
        heads = lambda ref: [ref[:, _hs(h)] for h in range(NH)]
        q, k, v, beta, do = heads(q_ref), heads(k_ref), heads(v_ref), heads(b_ref), heads(do_ref)
        f = _gdn_block(m, q, k, v, heads(g_ref), beta)
        u, w, aqk, qd, kd, t = f["u"], f["w"], f["aqk"], f["qd"], f["kd"], f["t"]
        s0 = [st_ref[h * DH:(h + 1) * DH, :] for h in range(NH)]
        s1 = [st_ref[(NH + h) * DH:(NH + h + 1) * DH, :] for h in range(NH)]
        ds2 = [ds_scr[h * DH:(h + 1) * DH, :] for h in range(NH)]
        total = lambda a, b: jnp.sum(jnp.sum(a * b, axis=1, keepdims=True), axis=0, keepdims=True)
        vna = _each(lambda a, b, s: _top(a) - _dg(_top(b), s), u, w, s0)
        vnb = _each(lambda a, b, s: _bot(a) - _dg(_bot(b), s), u, w, s1)
        dvn_i = _each(lambda a, b: _dg(a, b, TN), aqk, do)
        dvnb = _each(lambda a, b, s: _bot(a) + _dg(_bot(b), s), dvn_i, kd, ds2)
        dqdb = _each(lambda a, s: _dg(_bot(a), s, NT), do, s1)
        dkdb = _each(lambda a, s: _dg(a, s, NT), vnb, ds2)
        dglb = _each(total, ds2, s1)
        dwb = _each(lambda a, s: -_dg(a, s, NT), dvnb, s1)
        ds1 = _each(lambda s, gl, a, b, c, d: s * gl + _dg(_bot(a), _bot(b), TN) - _dg(_bot(c), d, TN),
                    ds2, f["glb"], qd, do, w, dvnb)
        dvna = _each(lambda a, b, s: _top(a) + _dg(_top(b), s), dvn_i, kd, ds1)
        dqda = _each(lambda a, s: _dg(_top(a), s, NT), do, s0)
        dkda = _each(lambda a, s: _dg(a, s, NT), vna, ds1)
        dgla = _each(total, ds1, s0)
        dwa = _each(lambda a, s: -_dg(a, s, NT), dvna, s0)
        ds0 = _each(lambda s, gl, a, b, c, d: s * gl + _dg(_top(a), _top(b), TN) - _dg(_top(c), d, TN),
                    ds1, f["gla"], qd, do, w, dvna)
        dvn, dqd, dkd, dw = (_each(_rows, a, b) for a, b in ((dvna, dvnb), (dqda, dqdb), (dkda, dkdb), (dwa, dwb)))
        daqk = _each(lambda a, va, vb: jnp.where(m["tril"], _dg(a, _rows(va, vb), NT), 0.0), do, vna, vnb)
        dt = _each(lambda a, b, c, d: _dg(a, b, NT) + _dg(c, d, NT), dvn, f["vb"], dw, f["kbe"])
        dvb = _each(lambda a, b: _dg(a, b, TN), t, dvn)
        dkbe = _each(lambda a, b: _dg(a, b, TN), t, dw)
        dtt = _each(lambda a, b: _dg(a, b, NT), dt, t)
        dl = _each(lambda a, b: -jnp.where(m["strict"], _dg(a, b, TN), 0.0), t, dtt)
        dm = _each(_mul, dl, f["gam"])
        dn = _each(_mul, daqk, f["gam"])
        dkb = _each(lambda a, b, c, d: _dg(a, b) + c * d, dm, k, dkbe, f["eg"])
        dks = _each(lambda a, b, c, d, e, g, h, i: _dg(a, b, TN) + _dg(c, d, TN) + e * g + h * i,
                    dm, f["kb"], dn, q, dkd, f["e2"], beta, dkb)
        dqs = _each(lambda a, b, c, d: _dg(a, b) + c * d, dn, k, dqd, f["eg"])
        gm = _each(lambda a, b, c, d: a * b + c * d, dl, f["lm"], daqk, aqk)
        dkdkd = _each(_mul, dkd, kd)
        dgc = _each(lambda a, b, c, d, e, g: _dxr(a + b * c + d * e - g, ones) - _dxr(a, ones, TN),
                    gm, dqd, qd, dkbe, f["kbe"], dkdkd)
        same_f = m["same"].astype(F32)
        chunk_tot = _each(lambda a: _dxl(same_f, _dxr(a, ones)), dkdkd)
        last = m["last_a"] | m["last_b"]
        dgc = _each(lambda a, b, ga, gla, gb, glb: a + jnp.where(last, b + jnp.where(m["in_a"], ga * gla, gb * glb), 0.0),
                    dgc, chunk_tot, dgla, f["gla"], dglb, f["glb"])
        dbs = _each(lambda a, b, c, d: _dxr(a * b + c * d, ones), dkb, k, dvb, v)
        dvs = _each(_mul, beta, dvb)
        triu_f = m["triu"].astype(F32)
        dgs = _each(lambda a: _dxl(triu_f, a), dgc)
        for ref, parts in ((dq_o, dqs), (dk_o, dks), (dv_o, dvs), (dg_o, dgs), (db_o, dbs)):
            ref[...] = jnp.concatenate(parts, axis=1)
        ds_scr[...] = jnp.concatenate(ds0, axis=0)

    blk = pl.BlockSpec((PAIR, HW), lambda i: (nb - 1 - i, 0))
    o = jax.ShapeDtypeStruct((S, HW), F32)
    return pl.pallas_call(
        body, name="gdn_bwd", grid=(nb,),
        in_specs=[blk] * 5 + [pl.BlockSpec((2 * NH * DH, DH), lambda i: (nb - 1 - i, 0)), blk],
        out_specs=[blk] * 5, out_shape=[o] * 5,
        scratch_shapes=[pltpu.VMEM((NH * DH, DH), F32)],
        compiler_params=_cp(("arbitrary",)),
    )(gq, gk, gv, gf, bf, states, do)


SB_T = 256
SB_GROUP = 4
SB_GROUP_BWD = 4
SB_SINGLES = 1
SB_DEAD = -110.0


def _group_sizes(g):
    sizes = []
    while g >= 1:
        sizes.append(g)
        g //= 2
    return sizes


def _sb_iotas(t):
    return lax.broadcasted_iota(jnp.int32, (t, t), 0), lax.broadcasted_iota(jnp.int32, (t, t), 1)


def _sb_scores(q, k, mask):
    z = _dot(q, k, NT) * DH ** -0.5
    ls = jnp.minimum(z, 0.0) - jnp.log(1.0 + jnp.exp(-jnp.abs(z)))
    lneg = ls - z
    if mask is not None:
        lneg = jnp.where(mask, lneg, 0.0)
    return ls, lneg


def _prefix(x, u):
    xh, xl = _split(x, 2)
    return _dot(xh, u) + _dot(xl, u)


def _sb_fwd(sqn, skn, svb, S):
    t = min(SB_T, S)

    def body(q_ref, k_ref, v_ref, o_ref, t_ref, cnt_ref):
        qb = pl.program_id(1)
        q = q_ref[...]
        r, c = _sb_iotas(t)
        diag = c < r
        u_after = (r > c).astype(BF16)

        def tiles(k0s, run, masks):
            sc = _each(lambda k0, m: _sb_scores(q, k_ref[pl.ds(k0, t), :], m), k0s, masks)
            ls, lneg = [s[0] for s in sc], [s[1] for s in sc]
            sums = _each(lambda x: jnp.sum(x, axis=1, keepdims=True), lneg)
            pre = _each(lambda x: _prefix(x, u_after), lneg)
            runs = [run]
            for s in sums:
                runs.append(runs[-1] + s)
            att = _each(lambda a, b, rn: jnp.exp(a + (rn + b)), ls, pre, runs[:-1])
            att = _each(lambda a, m: a if m is None else jnp.where(m, a, 0.0), att, masks)
            parts = _each(lambda a, k0: _dot(a.astype(BF16), v_ref[pl.ds(k0, t), :]), att, k0s)
            return sum(parts[1:], parts[0]), runs[-1]

        left = jnp.full((t, t), qb > 0)
        acc, run = tiles([pl.multiple_of(qb * t, t), pl.multiple_of(jnp.maximum(qb - 1, 0) * t, t)],
                         jnp.zeros((t, 1), F32), [diag, left])

        def alive(run):
            return jnp.max(run) >= SB_DEAD

        carry, done = (0, acc, run, alive(run)), jnp.minimum(qb, 1)
        for size, limit in [(1, SB_SINGLES)] + [(s, None) for s in _group_sizes(SB_GROUP)]:

            def more(c, size=size, done=done, limit=limit):
                i, _, _, go = c
                fits = done + (i + 1) * size <= qb
                return (fits if limit is None else fits & (i < limit)) & go

            def group(c, size=size, done=done):
                i, acc, run, _ = c
                first = qb - 1 - done - size * i
                part, run = tiles([pl.multiple_of((first - j) * t, t) for j in range(size)], run, [None] * size)
                return i + 1, acc + part, run, alive(run)

            n, acc, run, go = lax.while_loop(more, group, (0,) + carry[1:])
            carry, done = (0, acc, run, go), done + n * size
        o_ref[...] = acc.astype(BF16)
        t_ref[...] = jnp.broadcast_to(run, (t, DH))
        cnt_ref[pl.program_id(0), qb] = done

    qspec = pl.BlockSpec((t, DH), lambda h, i: (i, h))
    kspec = pl.BlockSpec((S, DH), lambda h, i: (0, h))
    return pl.pallas_call(
        body, name="sb_fwd", grid=(NH, S // t),
        in_specs=[qspec, kspec, kspec],
        out_specs=[qspec, qspec, pl.BlockSpec(memory_space=pltpu.SMEM)],
        out_shape=[jax.ShapeDtypeStruct((S, HW), BF16), jax.ShapeDtypeStruct((S, HW), F32),
                   jax.ShapeDtypeStruct((NH, S // t), jnp.int32)],
        compiler_params=_cp(("arbitrary", "arbitrary")),
    )(sqn, skn, svb)


def _sb_bwd(sqn, skn, svb, do, tot, walked, S):
    t = min(SB_T, S)

    def body(cnt_ref, q_ref, k_ref, v_ref, do_ref, t_ref, dq_o, dk_o, dv_o, dv_acc):
        qb = pl.program_id(1)

        @pl.when(qb == 0)
        def _():
            dk_o[...] = jnp.zeros_like(dk_o)
            dv_acc[...] = jnp.zeros_like(dv_acc)

        q = q_ref[...]
        do = do_ref[...].astype(BF16)
        tot_l = jnp.concatenate([t_ref[...]] * (t // DH), axis=1)
        r, c = _sb_iotas(t)
        diag = c < r
        u_upto = (r <= c).astype(BF16)
        u_before = (r < c).astype(BF16)

        def tiles(k0s, run_l, run_e, masks):
            rowsum = lambda x: jnp.sum(x, axis=1, keepdims=True)
            masked = lambda xs: _each(lambda a, m: a if m is None else jnp.where(m, a, 0.0), xs, masks)
            ks = [k_ref[pl.ds(k0, t), :] for k0 in k0s]
            vs = [v_ref[pl.ds(k0, t), :] for k0 in k0s]
            sc = _each(lambda k, m: _sb_scores(q, k, m), ks, masks)
            ls, lneg = [s[0] for s in sc], [s[1] for s in sc]
            sums_l = _each(rowsum, lneg)
            pre_l = _each(lambda x: _prefix(x, u_upto), lneg)
            runs_l = [run_l]
            for s in sums_l:
                runs_l.append(runs_l[-1] + s)
            att = masked(_each(lambda a, b, rn: jnp.exp(a + (tot_l - (rn + b))), ls, pre_l, runs_l[:-1]))
            e = _each(lambda v, a: _dot(do, v, NT) * a, vs, att)
            sums_e = _each(rowsum, e)
            pre_e = _each(lambda x: _prefix(x, u_before), e)
            runs_e = [run_e]
            for s in sums_e:
                runs_e.append(runs_e[-1] + s)
            sg = _each(jnp.exp, ls)
            dz = masked(_each(lambda a, b, rn, s: a * (1.0 - s) - (rn + b) * s, e, pre_e, runs_e[:-1], sg))
            dz = _each(lambda a: (a * DH ** -0.5).astype(BF16), dz)
            dvs = _each(lambda a: _dot(a.astype(BF16), do, TN), att)
            dks = _each(lambda a: _dot(a, q, TN), dz)
            dqs = _each(_dot, dz, ks)
            for k0, dv, dk in zip(k0s, dvs, dks):
                dv_acc[pl.ds(k0, t), :] += dv
                dk_o[pl.ds(k0, t), :] += dk
            return sum(dqs[1:], dqs[0]), runs_l[-1], runs_e[-1]

        walked = cnt_ref[pl.program_id(0), qb]
        early = jnp.maximum(walked - 1, 0)
        z1 = jnp.zeros((t, 1), F32)
        carry, done = (jnp.zeros((t, DH), F32), z1, z1), 0
        for size in _group_sizes(SB_GROUP_BWD):
            n = (early - done) // size

            def group(i, carry, size=size, done=done):
                dq, run_l, run_e = carry
                first = qb - walked + done + size * i
                part, run_l, run_e = tiles([pl.multiple_of((first + j) * t, t) for j in range(size)], run_l, run_e,
                                           [None] * size)
                return dq + part, run_l, run_e

            carry = lax.fori_loop(0, n, group, carry)
            done = done + n * size
        dq, run_l, run_e = carry
        left = jnp.full((t, t), qb > 0)
        part, _, _ = tiles([pl.multiple_of(jnp.maximum(qb - 1, 0) * t, t), pl.multiple_of(qb * t, t)], run_l, run_e,
                           [left, diag])
        dq_o[...] = dq + part

        @pl.when(qb == S // t - 1)
        def _():
            dv_o[...] = dv_acc[...].astype(BF16)

    qspec = pl.BlockSpec((t, DH), lambda h, i, cnt: (i, h))
    kspec = pl.BlockSpec((S, DH), lambda h, i, cnt: (0, h))
    o = jax.ShapeDtypeStruct((S, HW), F32)
    return pl.pallas_call(
        body, name="sb_bwd",
        grid_spec=pltpu.PrefetchScalarGridSpec(
            num_scalar_prefetch=1, grid=(NH, S // t),
            in_specs=[qspec, kspec, kspec, qspec, qspec], out_specs=[qspec, kspec, kspec],
            scratch_shapes=[pltpu.VMEM((S, DH), F32)]),
        out_shape=[o, o, jax.ShapeDtypeStruct((S, HW), BF16)],
        compiler_params=_cp(("parallel", "arbitrary")),
    )(walked, sqn, skn, svb, do, tot)


def _mem_probs(qn, kn):
    s = _dot(qn, kn.astype(BF16), NT) * DH ** -0.5
    p = jnp.exp(s - jnp.max(s, axis=-1, keepdims=True))
    return p / jnp.sum(p, axis=-1, keepdims=True)


def _mem_fwd(qmn, kv, gmk, S):
    ts = _row_tile(S)

    def body(q_ref, kv_ref, gk_ref, o_ref):
        for h in range(NH):
            kn, _ = _rms(kv_ref[:, _hs(h)], gk_ref[...])
            p = _mem_probs(q_ref[:, _hs(h)], kn)
            o_ref[:, _hs(h)] = _dbf(p, kv_ref[:, HW + h * DH:HW + (h + 1) * DH]).astype(BF16)

    return pl.pallas_call(
        body, name="mem_fwd", grid=(S // ts,),
        in_specs=[pl.BlockSpec((ts, HW), lambda i: (i, 0)), pl.BlockSpec((NMEM, 2 * HW), lambda i: (0, 0)),
                  pl.BlockSpec((1, DH), lambda i: (0, 0))],
        out_specs=pl.BlockSpec((ts, HW), lambda i: (i, 0)),
        out_shape=jax.ShapeDtypeStruct((S, HW), BF16),
        compiler_params=_cp(("parallel",)),
    )(qmn, kv, gmk)


def _mem_bwd(proj, qmn, kv, gmq, gmk, do, S):
    ts = _row_tile(S)
    n = S // ts

    def body(mq_ref, q_ref, kv_ref, gq_ref, gk_ref, do_ref, dmq_o, dkv_o, dgq_o, dgk_o, dkn_scr):
        i = pl.program_id(0)

        @pl.when(i == 0)
        def _():
            dkv_o[...] = jnp.zeros_like(dkv_o)
            dgq_o[...] = jnp.zeros_like(dgq_o)
            dkn_scr[...] = jnp.zeros_like(dkn_scr)

        dgq = jnp.zeros((1, DH), F32)
        for h in range(NH):
            km = kv_ref[:, _hs(h)]
            vm = kv_ref[:, HW + h * DH:HW + (h + 1) * DH].astype(BF16)
            kn, _ = _rms(km, gk_ref[...])
            qn = q_ref[:, _hs(h)]
            p = _mem_probs(qn, kn)
            dob = do_ref[:, _hs(h)].astype(BF16)
            dkv_o[:, HW + h * DH:HW + (h + 1) * DH] += _dot(p.astype(BF16), dob, TN)
            dp = _dot(dob, vm, NT)
            dsc = (p * (dp - jnp.sum(dp * p, axis=-1, keepdims=True)) * DH ** -0.5).astype(BF16)
            dkn_scr[:, _hs(h)] += _dot(dsc, qn, TN)
            x = mq_ref[:, _hs(h)]
            _, r = _rms(x, gq_ref[...])
            dx, dg = _rms_bwd(_dot(dsc, kn.astype(BF16)), x, gq_ref[...], r)
            dmq_o[:, _hs(h)] = dx.astype(BF16)
            dgq = dgq + dg
        dgq_o[...] += dgq

        @pl.when(i == n - 1)
        def _():
            dgk = jnp.zeros((1, DH), F32)
            for h in range(NH):
                km = kv_ref[:, _hs(h)]
                _, r = _rms(km, gk_ref[...])
                dx, dg = _rms_bwd(dkn_scr[:, _hs(h)], km, gk_ref[...], r)
                dkv_o[:, _hs(h)] = dx
                dgk = dgk + dg
            dgk_o[...] = dgk

    full = lambda r, c: pl.BlockSpec((r, c), lambda i: (0, 0))
    t512 = pl.BlockSpec((ts, HW), lambda i: (i, 0))
    return pl.pallas_call(
        body, name="mem_bwd", grid=(n,),
        in_specs=[pl.BlockSpec((ts, HW), lambda i: (i, CB_MQ)), t512, full(NMEM, 2 * HW), full(1, DH), full(1, DH),
                  t512],
        out_specs=[t512, full(NMEM, 2 * HW), full(1, DH), full(1, DH)],
        out_shape=[jax.ShapeDtypeStruct((S, HW), BF16), jax.ShapeDtypeStruct((NMEM, 2 * HW), F32),
                   jax.ShapeDtypeStruct((1, DH), F32), jax.ShapeDtypeStruct((1, DH), F32)],
        scratch_shapes=[pltpu.VMEM((NMEM, HW), F32)],
        compiler_params=_cp(("arbitrary",)),
    )(proj, qmn, kv, gmq, gmk, do)


def _gated_gdn(o, z, g):
    sg = _sigmoid(z)
    outs, rs = [], []
    for h in range(NH):
        y, r = _rms(o[:, _hs(h)], g)
        outs.append(y * (z[:, _hs(h)] * sg[:, _hs(h)]))
        rs.append(r)
    return jnp.concatenate(outs, axis=1), rs, sg


def _merge_fwd(x, proj, ogdn, osb, omem, ggdn, wbg, wbs, wbm, wo, S):
    ts = _row_tile(S)

    def body(x_ref, z_ref, g0_ref, g1_ref, g2_ref, og_ref, os_ref, om_ref, gg_ref, wbg_ref, wbs_ref, wbm_ref,
             wo_ref, x1_o, mix_o):
        on, _, _ = _gated_gdn(og_ref[...], z_ref[...], gg_ref[...])
        mix = (_sigmoid(g0_ref[...]) * _dbf(on, wbg_ref[...]) + _sigmoid(g1_ref[...]) * _dbf(os_ref[...], wbs_ref[...])
               + _sigmoid(g2_ref[...]) * _dbf(om_ref[...], wbm_ref[...]))
        mix_o[...] = mix.astype(BF16)
        x1_o[...] = x_ref[...] + _dbf(mix, wo_ref[...])

    t512 = pl.BlockSpec((ts, HW), lambda i: (i, 0))
    t1k = pl.BlockSpec((ts, D), lambda i: (i, 0))
    gate = lambda j: pl.BlockSpec((ts, D), lambda i: (i, 4 + j))
    full = lambda r, c: pl.BlockSpec((r, c), lambda i: (0, 0))
    return pl.pallas_call(
        body, name="merge_fwd", grid=(S // ts,),
        in_specs=[t1k, pl.BlockSpec((ts, HW), lambda i: (i, CB_Z)), gate(0), gate(1), gate(2), t512, t512, t512,
                  full(1, DH), full(HW, D), full(HW, D), full(HW, D), full(D, D)],
        out_specs=[t1k, t1k],
        out_shape=[jax.ShapeDtypeStruct((S, D), F32), jax.ShapeDtypeStruct((S, D), BF16)],
        compiler_params=_cp(("parallel",)),
    )(x, proj, proj, proj, proj, ogdn, osb, omem, ggdn, wbg, wbs, wbm, wo)


def _merge_bwd(dmix, proj, ogdn, osb, omem, ggdn, wbg, wbs, wbm, S):
    ts = _narrow_tile(S)

    def body(dm_ref, z_ref, g0_ref, g1_ref, g2_ref, og_ref, os_ref, om_ref, gg_ref, wbg_ref, wbs_ref, wbm_ref,
             dgl0_o, dgl1_o, dgl2_o, dog_o, dz_o, dos_o, dom_o, dwbg_o, dwbs_o, dwbm_o, dgg_o):
        @pl.when(pl.program_id(0) == 0)
        def _():
            for ref in (dwbg_o, dwbs_o, dwbm_o, dgg_o):
                ref[...] = jnp.zeros_like(ref)

        dm = dm_ref[...]
        og = og_ref[...]
        z = z_ref[...]
        on, rs, sg = _gated_gdn(og, z, gg_ref[...])
        branch = ((on, g0_ref, wbg_ref, dgl0_o, dwbg_o), (os_ref[...], g1_ref, wbs_ref, dgl1_o, dwbs_o),
                  (om_ref[...], g2_ref, wbm_ref, dgl2_o, dwbm_o))
        dos = []
        for o, g_ref, w_ref, dgl_o, dw_o in branch:
            ob = o.astype(BF16)
            gate = _sigmoid(g_ref[...])
            dgl_o[...] = (dm * _dot(ob, w_ref[...]) * gate * (1.0 - gate)).astype(BF16)
            dy = (dm * gate).astype(BF16)
            dw_o[...] += _dot(ob, dy, TN)
            dos.append(_dot(dy, w_ref[...], NT))
        dos_o[...] = dos[1].astype(BF16)
        dom_o[...] = dos[2].astype(BF16)
        don = dos[0]
        dgg = jnp.zeros((1, DH), F32)
        for h in range(NH):
            oh, zh, sh = og[:, _hs(h)], z[:, _hs(h)], sg[:, _hs(h)]
            y = oh * rs[h] * gg_ref[...]
            dz_o[:, _hs(h)] = (don[:, _hs(h)] * y * (sh * (1.0 + zh * (1.0 - sh)))).astype(BF16)
            dx, dg = _rms_bwd(don[:, _hs(h)] * (zh * sh), oh, gg_ref[...], rs[h])
            dog_o[:, _hs(h)] = dx
            dgg = dgg + dg
        dgg_o[...] += dgg

    t512 = pl.BlockSpec((ts, HW), lambda i: (i, 0))
    t1k = pl.BlockSpec((ts, D), lambda i: (i, 0))
    gate = lambda j: pl.BlockSpec((ts, D), lambda i: (i, 4 + j))
    full = lambda r, c: pl.BlockSpec((r, c), lambda i: (0, 0))
    s1k = jax.ShapeDtypeStruct((S, D), BF16)
    s512 = jax.ShapeDtypeStruct((S, HW), BF16)
    wsh = jax.ShapeDtypeStruct((HW, D), F32)
    return pl.pallas_call(
        body, name="merge_bwd", grid=(S // ts,),
        in_specs=[t1k, pl.BlockSpec((ts, HW), lambda i: (i, CB_Z)), gate(0), gate(1), gate(2), t512, t512, t512,
                  full(1, DH), full(HW, D), full(HW, D), full(HW, D)],
        out_specs=[t1k, t1k, t1k, t512, t512, t512, t512, full(HW, D), full(HW, D), full(HW, D), full(1, DH)],
        out_shape=[s1k, s1k, s1k, jax.ShapeDtypeStruct((S, HW), F32), s512, s512, s512, wsh, wsh, wsh,
                   jax.ShapeDtypeStruct((1, DH), F32)],
        compiler_params=_cp(("arbitrary",)),
    )(dmix, proj, proj, proj, proj, ogdn, osb, omem, ggdn, wbg, wbs, wbm)


def _norm_cast(name, x, g):
    rows = x.shape[0]
    ts = min(_row_tile(rows), rows)

    def body(x_ref, g_ref, o_ref):
        o_ref[...] = _rms(x_ref[...], g_ref[...])[0].astype(BF16)

    t1k = pl.BlockSpec((ts, D), lambda i: (i, 0))
    return pl.pallas_call(
        body, name=name, grid=(rows // ts,), in_specs=[t1k, pl.BlockSpec((1, D), lambda i: (0, 0))], out_specs=t1k,
        out_shape=jax.ShapeDtypeStruct((rows, D), BF16), compiler_params=_cp(("parallel",)),
    )(x, g)


def _norm_bwd(name, dh, x, g, res):
    rows = x.shape[0]
    ts = min(_row_tile(rows), rows)

    def body(*refs):
        dh_ref, x_ref, g_ref = refs[:3]
        dx_o, dg_o = refs[-2:]

        @pl.when(pl.program_id(0) == 0)
        def _():
            dg_o[...] = jnp.zeros_like(dg_o)

        xv = x_ref[...]
        _, r = _rms(xv, g_ref[...])
        dx, dg = _rms_bwd(dh_ref[...], xv, g_ref[...], r)
        dx_o[...] = dx if res is None else dx + refs[3][...]
        dg_o[...] += dg

    t1k = pl.BlockSpec((ts, D), lambda i: (i, 0))
    gsp = pl.BlockSpec((1, D), lambda i: (0, 0))
    ops = [dh, x, g] + ([] if res is None else [res])
    return pl.pallas_call(
        body, name=name, grid=(rows // ts,), in_specs=[t1k, t1k, gsp] + ([] if res is None else [t1k]),
        out_specs=[t1k, gsp],
        out_shape=[jax.ShapeDtypeStruct((rows, D), F32), jax.ShapeDtypeStruct((1, D), F32)],
        compiler_params=_cp(("arbitrary",)),
    )(*ops)


def _adamw(name, gall, w, m, v):
    rows = w.shape[0]
    nsrc = gall.shape[0]
    tr = min(SLAB_TILE, rows)
    assert rows % tr == 0

    def body(g_ref, w_ref, m_ref, v_ref, g_o, d_o, m_o, v_o):
        g = g_ref[0].astype(F32)
        for j in range(1, nsrc):
            g = g + g_ref[j].astype(F32)
        m_new = ADAM_B1 * m_ref[...] + (1.0 - ADAM_B1) * g
        v_new = ADAM_B2 * v_ref[...] + (1.0 - ADAM_B2) * jnp.square(g)
        m_hat = m_new / (1.0 - ADAM_B1 ** ADAM_STEP)
        v_hat = v_new / (1.0 - ADAM_B2 ** ADAM_STEP)
        g_o[...] = g
        d_o[...] = -ADAM_LR * (m_hat / (jnp.sqrt(v_hat) + ADAM_EPS) + ADAM_WD * w_ref[...])
        m_o[...] = m_new
        v_o[...] = v_new

    t = pl.BlockSpec((tr, LANES), lambda i: (i, 0))
    o = jax.ShapeDtypeStruct((rows, LANES), F32)
    return pl.pallas_call(
        body, name=name, grid=(rows // tr,),
        in_specs=[pl.BlockSpec((nsrc, tr, LANES), lambda i: (0, i, 0)), t, t, t],
        out_specs=[t, t, t, t], out_shape=[o, o, o, o],
        compiler_params=_cp(("parallel",)),
    )(gall, w, m, v)


def _pair_sum(name, mine, theirs):
    rows = mine.shape[1]
    tr = min(SLAB_TILE, rows)
    assert rows % tr == 0
    core = lax.axis_index("c").astype(jnp.int32).reshape(1)

    def body(c_ref, a_ref, b_ref, o_ref):
        o_ref[...] = (a_ref[...].astype(F32) + b_ref[...].astype(F32)).astype(o_ref.dtype)

    blk = pl.BlockSpec((1, tr, LANES), lambda j, i, c_ref: (j, i, 0))
    return pl.pallas_call(
        body, name=name,
        grid_spec=pltpu.PrefetchScalarGridSpec(
            num_scalar_prefetch=1, grid=(NDEV // 2, rows // tr),
            in_specs=[pl.BlockSpec((1, tr, LANES), lambda j, i, c_ref: (2 * j + c_ref[0], i, 0)), blk],
            out_specs=blk),
        out_shape=jax.ShapeDtypeStruct((NDEV // 2, rows, LANES), mine.dtype),
        compiler_params=_cp(("parallel", "parallel")),
    )(core, mine, theirs)


HBM_SPEC = pl.BlockSpec(memory_space=pltpu.HBM)


def _remote(src, dst, send_sems, recv_sems, k, to):
    return pltpu.make_async_remote_copy(src_ref=src, dst_ref=dst, send_sem=send_sems.at[k], recv_sem=recv_sems.at[k],
                                        device_id=to, device_id_type=pl.DeviceIdType.MESH)


def _gather_steps(x_ref, o_ref, send_sems, recv_sems, local_sem):
    ix, iy, ic = lax.axis_index("x"), lax.axis_index("y"), lax.axis_index("c")
    me, sibling = (ix, iy, ic), (ix, iy, 1 - ic)
    chips = [(1 - ix, iy), (ix, 1 - iy), (1 - ix, 1 - iy)]

    def slab(px, py, pc):
        return o_ref.at[4 * px + 2 * py + pc]

    def copy(k, block, to, src=None):
        return _remote(slab(*block) if src is None else src, slab(*block), send_sems, recv_sems, k, to)

    def mine():
        return pltpu.make_async_copy(x_ref, slab(*me), local_sem)

    def first():
        return [copy(0, me, sibling, src=x_ref)] + [copy(1 + j, me, (*chip, ic), src=x_ref)
                                                    for j, chip in enumerate(chips)]

    def passed():
        return [copy(4 + j, (*chip, ic), sibling) for j, chip in enumerate(chips)]

    def start():
        mine().start()
        for cp in first():
            cp.start()

    def forward():
        for j, (chip, cp) in enumerate(zip(chips, passed())):
            copy(1 + j, (*chip, ic), me).wait_recv()
            cp.start()

    def finish():
        copy(0, sibling, me).wait_recv()
        for j, chip in enumerate(chips):
            copy(4 + j, (*chip, 1 - ic), me).wait_recv()
        for cp in first() + passed():
            cp.wait_send()
        mine().wait()

    return start, forward, finish


GATHER_SEMS = [pltpu.SemaphoreType.DMA((NDEV - 1,)), pltpu.SemaphoreType.DMA((NDEV - 1,)), pltpu.SemaphoreType.DMA]


def _gather(name, x):
    rows, cols = x.shape

    def body(x_ref, o_ref, send_sems, recv_sems, local_sem):
        for step in _gather_steps(x_ref, o_ref, send_sems, recv_sems, local_sem):
            step()

    return pl.pallas_call(
        body, name=name, in_specs=[HBM_SPEC], out_specs=HBM_SPEC,
        out_shape=jax.ShapeDtypeStruct((NDEV, rows, cols), x.dtype), scratch_shapes=list(GATHER_SEMS),
    )(x)


def _sibling_exchange(name, x):
    rows, cols = x.shape[-2:]
    nchip = NDEV // 2

    def body(x_ref, o_ref, send_sems, recv_sems):
        ix, iy, ic = lax.axis_index("x"), lax.axis_index("y"), lax.axis_index("c")
        copies = [_remote(x_ref.at[2 * j + (1 - ic)], o_ref.at[j], send_sems, recv_sems, j, (ix, iy, 1 - ic))
                  for j in range(nchip)]
        for cp in copies:
            cp.start()
        for cp in copies:
            cp.wait()

    return pl.pallas_call(
        body, name=name, in_specs=[HBM_SPEC], out_specs=HBM_SPEC,
        out_shape=jax.ShapeDtypeStruct((nchip, rows, cols), x.dtype),
        scratch_shapes=[pltpu.SemaphoreType.DMA((nchip,)), pltpu.SemaphoreType.DMA((nchip,))],
    )(x)


def _chip_steps(x_ref, o_ref, send_sems, recv_sems, local_sem):
    ix, iy, ic = lax.axis_index("x"), lax.axis_index("y"), lax.axis_index("c")
    my_chip = 2 * ix + iy

    def own():
        return pltpu.make_async_copy(x_ref.at[my_chip], o_ref.at[my_chip], local_sem)

    def copies():
        out = []
        for k in range(1, NDEV // 2):
            px, py = ix ^ (k >> 1), iy ^ (k & 1)
            out.append(_remote(x_ref.at[2 * px + py], o_ref.at[my_chip], send_sems, recv_sems, k - 1, (px, py, ic)))
        return out

    def start():
        own().start()
        for cp in copies():
            cp.start()

    def finish():
        for cp in copies():
            cp.wait()
        own().wait()

    return start, (lambda: None), finish


CHIP_SEMS = [pltpu.SemaphoreType.DMA((NDEV // 2 - 1,)), pltpu.SemaphoreType.DMA((NDEV // 2 - 1,)),
             pltpu.SemaphoreType.DMA]


COL_SHARDED = {"w_in": (D, D_IN), "w_br_gdn": (HW, D), "w_br_sb": (HW, D), "w_br_mem": (HW, D), "w_up": (D, DFF),
               "conv_w": (4, 3 * HW)}
ROW_SHARDED = {"w_mem_kv": (D, 2 * HW), "w_o": (D, D), "w_down": (DFF, D)}


def _to_slab(p):
    return p.reshape(p.shape[:-2] + (-1, LANES))


def _from_slab(flat, r, c):
    return flat.reshape(flat.shape[:-2] + (r, c))


def _shard_dims(name):
    if name in COL_SHARDED:
        r, c = COL_SHARDED[name]
        return r, c // NDEV
    r, c = ROW_SHARDED[name]
    return r // NDEV, c


def _pack_rows(parts, total):
    flat = jnp.concatenate(parts, axis=-2)
    return jnp.pad(flat, [(0, 0)] * (flat.ndim - 2) + [(0, total - flat.shape[-2]), (0, 0)])


def _pack_shards(vals, names, total):
    return _pack_rows([_to_slab(vals[n][0]) for n in names], total)


def _pack_full_grads(grads, names, total):
    parts = []
    for name in names:
        g = grads[name]
        r, c = _shard_dims(name)
        if name in COL_SHARDED:
            g = g.reshape(r, NDEV, c).transpose(1, 0, 2)
        else:
            g = g.reshape(NDEV, r, c)
        parts.append(_to_slab(g))
    return _pack_rows(parts, total)


def _unpack_gathered(slabs, names):
    out, pos = {}, 0
    for name in names:
        rows = SLAB_ROWS[name]
        r, c = _shard_dims(name)
        g = _from_slab(slabs[:, pos:pos + rows], r, c)
        pos += rows
        if name in COL_SHARDED:
            out[name] = g.transpose(1, 0, 2).reshape(r, NDEV * c)
        else:
            out[name] = g.reshape(NDEV * r, c)
    return out


def _unpack_shard(flat, names, shapes):
    out, pos = {}, 0
    for name in names:
        rows = SLAB_ROWS[name]
        r, c = _shard_dims(name)
        out[name] = _from_slab(flat[pos:pos + rows], r, c).reshape(shapes[name])
        pos += rows
    return out


def _pack_vec(vals):
    row = jnp.concatenate([vals[n] for n in VEC], axis=1)
    return jnp.pad(row, ((0, 0), (0, VEC_WIDTH - row.shape[1])))


def _adamw_vec(gall, w, m, v):
    aligned = [(off, n) for off, n in zip(VEC_OFFSETS, VEC_SIZES) if n % LANES == 0]

    def body(g_ref, w_ref, m_ref, v_ref, *outs):
        g = g_ref[0]
        for j in range(1, NDEV):
            g = g + g_ref[j]
        m_new = ADAM_B1 * m_ref[...] + (1.0 - ADAM_B1) * g
        v_new = ADAM_B2 * v_ref[...] + (1.0 - ADAM_B2) * jnp.square(g)
        m_hat = m_new / (1.0 - ADAM_B1 ** ADAM_STEP)
        v_hat = v_new / (1.0 - ADAM_B2 ** ADAM_STEP)
        delta = -ADAM_LR * (m_hat / (jnp.sqrt(v_hat) + ADAM_EPS) + ADAM_WD * w_ref[...])
        for r, val in enumerate((g, delta, m_new, v_new)):
            outs[r][...] = val
            for i, (off, n) in enumerate(aligned):
                outs[4 + r * len(aligned) + i][...] = val[:, off:off + n]

    full = lambda *shape: pl.BlockSpec(shape, lambda: (0,) * len(shape))
    row = jax.ShapeDtypeStruct((1, VEC_WIDTH), F32)
    out_shape = [row] * 4 + [jax.ShapeDtypeStruct((1, n), F32) for _ in range(4) for _, n in aligned]
    out_specs = [full(1, VEC_WIDTH)] * 4 + [full(1, n) for _ in range(4) for _, n in aligned]
    return pl.pallas_call(
        body, name="adamw_replicated",
        in_specs=[full(NDEV, 1, VEC_WIDTH), full(1, VEC_WIDTH), full(1, VEC_WIDTH), full(1, VEC_WIDTH)],
        out_specs=out_specs, out_shape=out_shape,
    )(gall, w, m, v)


def _unpack_vec(outs, r):
    aligned = [name for name, n in zip(VEC, VEC_SIZES) if n % LANES == 0]
    vals = {name: outs[4 + r * len(aligned) + i] for i, name in enumerate(aligned)}
    for name, off, n in zip(VEC, VEC_OFFSETS, VEC_SIZES):
        if name not in vals:
            vals[name] = outs[r][:, off:off + n]
    return vals


def _pad_w_in(w):
    return jnp.concatenate([w[:, :2048], w[:, 2056:], w[:, 2048:2056], jnp.zeros((D, D_INP - D_IN), w.dtype)], axis=1)


def _unpad_w_in(w):
    return jnp.concatenate([w[:, :2048], w[:, 7168:7176], w[:, 2048:7168]], axis=1)


def _per_head(v):
    return jnp.repeat(v.reshape(NH), DH).reshape(1, HW)


def _local_step(x, mem, target, w, sm, rest_shards):
    S = x.shape[0]
    ts = _row_tile(S)
    tb = 2 * ts
    alog_f, dtb_f = _per_head(sm["a_log"]), _per_head(sm["dt_bias"])
    w = dict(w)

    h1 = _norm_cast("norm1", x, sm["norm1_g"])
    proj, rest = _mm("in_proj", h1, w["w_in"], "nn", 2 * tb, 1536, D, n_outer=True, comm=("gather", rest_shards))
    w.update(_unpack_gathered(rest[:, :sum(SLAB_ROWS[n] for n in REST)], REST))
    gq, gk, gv, gf, bf, sqn, skn, svb, qmn = _pre_fwd(proj, w["conv_w"], alog_f, dtb_f, sm["sb_q_norm_g"],
                                                      sm["sb_k_norm_g"], sm["mem_q_norm_g"], S)
    ogdn, states = _gdn_fwd(gq, gk, gv, gf, bf, S)
    osb, sb_tot, sb_walked = _sb_fwd(sqn, skn, svb, S)
    kv = _mm("mem_kv", mem, w["w_mem_kv"], "nn", NMEM, D, D, pro="rms", pro_g=sm["mem_norm_g"])
    omem = _mem_fwd(qmn, kv, sm["mem_k_norm_g"], S)
    x1, mix = _merge_fwd(x, proj, ogdn, osb, omem, sm["gdn_norm_g"], w["w_br_gdn"], w["w_br_sb"], w["w_br_mem"],
                         w["w_o"], S)
    h2 = _norm_cast("norm2", x1, sm["norm2_g"])
    up = _mm("mlp_up", h2, w["w_up"], "nn", tb, 2048, D, n_outer=True)
    dy, loss = _mm("mlp_down", up, w["w_down"], "nn", tb, D, 1024, pro="relu2", epi="loss", epi_x=(x1, target))

    g = {}
    dup = _mm("d_up", dy, w["w_down"], "nt", tb, 1024, D, epi="drelu2", epi_x=up, out_dtype=BF16)
    g["w_down"] = _mm("dw_down", up, dy, "tn", 1024, D, 1024, pro="relu2")
    g["w_up"] = _mm("dw_up", h2, dup, "tn", D, 1024, 2048)
    dx1, g["norm2_g"] = _mm("d_h2", dup, w["w_up"], "nt", tb, D, 1024, epi="rms_bwd", epi_x=(x1, sm["norm2_g"], dy))

    dmix = _mm("d_mix", dx1, w["w_o"], "nt", tb, D, D)
    g["w_o"] = _mm("dw_o", mix, dx1, "tn", D, D, 1024)
    (dgl0, dgl1, dgl2, dogdn, dz, dosb, domem, g["w_br_gdn"], g["w_br_sb"], g["w_br_mem"],
     g["gdn_norm_g"]) = _merge_bwd(dmix, proj, ogdn, osb, omem, sm["gdn_norm_g"], w["w_br_gdn"], w["w_br_sb"],
                                   w["w_br_mem"], S)
    dmq, dkv, g["mem_q_norm_g"], g["mem_k_norm_g"] = _mem_bwd(proj, qmn, kv, sm["mem_q_norm_g"], sm["mem_k_norm_g"],
                                                             domem, S)
    g["w_mem_kv"] = _mm("dw_mem_kv", mem, dkv, "tn", D, D, NMEM, pro="rms", pro_g=sm["mem_norm_g"])
    dmn = _mm("d_mem_n", dkv, w["w_mem_kv"], "nt", NMEM, D, D)
    _, g["mem_norm_g"] = _norm_bwd("mem_norm_bwd", dmn, mem, sm["mem_norm_g"], None)
    dsqn, dskn, dsv = _sb_bwd(sqn, skn, svb, dosb, sb_tot, sb_walked, S)
    dgq, dgk, dgv, dgf, dbf = _gdn_bwd(gq, gk, gv, gf, bf, states, dogdn, S)
    dc, dab, dsq, dsk, g["conv_w"], dal_f, ddt_f, g["sb_q_norm_g"], g["sb_k_norm_g"] = _pre_bwd(
        proj, w["conv_w"], alog_f, dtb_f, sm["sb_q_norm_g"], sm["sb_k_norm_g"], dgq, dgk, dgv, dgf, dbf, dsqn, dskn, S)
    g["a_log"] = dal_f.reshape(NH, DH)[:, 0].reshape(1, NH)
    g["dt_bias"] = ddt_f.reshape(NH, DH)[:, 0].reshape(1, NH)
    dqkv = _conv_bwd(dc, w["conv_w"], S)

    dproj = jnp.concatenate([dqkv, dz, dsq, dsk, dsv, dmq, dgl0, dgl1, dgl2, dab], axis=1)
    rest_mine = _pack_full_grads(g, REST, R_REST).astype(BF16)
    rest_pair = _pair_sum("pair_sum_rest", rest_mine, _sibling_exchange("scatter_sibling_rest", rest_mine))
    g["w_in"], rest_all = _mm("dw_in", h1, dproj, "tn", D, 1536, 2048, comm=("chips", rest_pair))
    g["w_in"] = _unpad_w_in(g["w_in"])
    first_mine = _pack_full_grads(g, FIRST, R_FIRST).astype(BF16)
    first_pair = _pair_sum("pair_sum_first", first_mine, _sibling_exchange("scatter_sibling_first", first_mine))
    dx, g["norm1_g"], first_all = _mm("d_h", dproj, w["w_in"], "nt", tb, D, 1536, epi="rms_bwd",
                                      epi_x=(x, sm["norm1_g"], dx1), comm=("chips", first_pair))
    return loss[0, 0], dx, g, rest_all, first_all


def kernel(x, mem, norm1_g, w_in, conv_w, a_log, dt_bias, gdn_norm_g, sb_q_norm_g, sb_k_norm_g, mem_norm_g, w_mem_kv, mem_q_norm_g, mem_k_norm_g, w_br_gdn, w_br_sb, w_br_mem, w_o, norm2_g, w_up, w_down, loss_target, m_norm1_g, m_w_in, m_conv_w, m_a_log, m_dt_bias, m_gdn_norm_g, m_sb_q_norm_g, m_sb_k_norm_g, m_mem_norm_g, m_w_mem_kv, m_mem_q_norm_g, m_mem_k_norm_g, m_w_br_gdn, m_w_br_sb, m_w_br_mem, m_w_o, m_norm2_g, m_w_up, m_w_down, v_norm1_g, v_w_in, v_conv_w, v_a_log, v_dt_bias, v_gdn_norm_g, v_sb_q_norm_g, v_sb_k_norm_g, v_mem_norm_g, v_w_mem_kv, v_mem_q_norm_g, v_mem_k_norm_g, v_w_br_gdn, v_w_br_sb, v_w_br_mem, v_w_o, v_norm2_g, v_w_up, v_w_down):
    given = dict(norm1_g=norm1_g, w_in=w_in, conv_w=conv_w, a_log=a_log, dt_bias=dt_bias, gdn_norm_g=gdn_norm_g,
                 sb_q_norm_g=sb_q_norm_g, sb_k_norm_g=sb_k_norm_g, mem_norm_g=mem_norm_g, w_mem_kv=w_mem_kv,
                 mem_q_norm_g=mem_q_norm_g, mem_k_norm_g=mem_k_norm_g, w_br_gdn=w_br_gdn, w_br_sb=w_br_sb,
                 w_br_mem=w_br_mem, w_o=w_o, norm2_g=norm2_g, w_up=w_up, w_down=w_down)
    mom1 = dict(norm1_g=m_norm1_g, w_in=m_w_in, conv_w=m_conv_w, a_log=m_a_log, dt_bias=m_dt_bias,
                gdn_norm_g=m_gdn_norm_g, sb_q_norm_g=m_sb_q_norm_g, sb_k_norm_g=m_sb_k_norm_g,
                mem_norm_g=m_mem_norm_g, w_mem_kv=m_w_mem_kv, mem_q_norm_g=m_mem_q_norm_g,
                mem_k_norm_g=m_mem_k_norm_g, w_br_gdn=m_w_br_gdn, w_br_sb=m_w_br_sb, w_br_mem=m_w_br_mem, w_o=m_w_o,
                norm2_g=m_norm2_g, w_up=m_w_up, w_down=m_w_down)
    mom2 = dict(norm1_g=v_norm1_g, w_in=v_w_in, conv_w=v_conv_w, a_log=v_a_log, dt_bias=v_dt_bias,
                gdn_norm_g=v_gdn_norm_g, sb_q_norm_g=v_sb_q_norm_g, sb_k_norm_g=v_sb_k_norm_g,
                mem_norm_g=v_mem_norm_g, w_mem_kv=v_w_mem_kv, mem_q_norm_g=v_mem_q_norm_g,
                mem_k_norm_g=v_mem_k_norm_g, w_br_gdn=v_w_br_gdn, w_br_sb=v_w_br_sb, w_br_mem=v_w_br_mem, w_o=v_w_o,
                norm2_g=v_norm2_g, w_up=v_w_up, w_down=v_w_down)
    shapes = {n: given[n].shape for n in WEIGHTS}

    first_loc = _pack_shards(given, FIRST, R_FIRST)
    rest_loc = _pack_shards(given, REST, R_REST)
    gathered = _gather("gather_first", first_loc.astype(BF16))
    w = _unpack_gathered(gathered[:, :sum(SLAB_ROWS[n] for n in FIRST)], FIRST)
    w["w_in"] = _pad_w_in(w["w_in"])
    conv_loc = jnp.pad(given["conv_w"][0].reshape(-1, LANES), ((0, 2), (0, 0)))
    conv_all = _gather("gather_conv", conv_loc)
    w["conv_w"] = conv_all[:, :6].reshape(NDEV, 4, 3 * HW // NDEV).transpose(1, 0, 2).reshape(4, 3 * HW)
    sm = {n: given[n] for n in SMALL}

    loss, dx, g, rest_all, first_all = _local_step(x[0], mem[0], loss_target[0], w, sm, rest_loc.astype(BF16))
    res_first = _adamw("adamw_first", first_all, first_loc, _pack_shards(mom1, FIRST, R_FIRST),
                       _pack_shards(mom2, FIRST, R_FIRST))
    res_rest = _adamw("adamw_rest", rest_all, rest_loc, _pack_shards(mom1, REST, R_REST),
                      _pack_shards(mom2, REST, R_REST))
    gs_all = _gather("gather_small_grads", _pack_vec(g))
    vec_outs = _adamw_vec(gs_all, _pack_vec(given), _pack_vec(mom1), _pack_vec(mom2))

    outs = {}
    for r, prefix in enumerate(("grad_", "delta_", "new_m_", "new_v_")):
        vals = _unpack_shard(res_first[r], FIRST, shapes)
        vals.update(_unpack_shard(res_rest[r], REST, shapes))
        vals.update(_unpack_vec(vec_outs, r))
        for n in WEIGHTS:
            outs[prefix + n] = vals[n]
    loss = lax.psum(loss, ("x", "y", "c"))
    return (loss, dx[None], *[outs[p + n] for p in ("grad_", "delta_", "new_m_", "new_v_") for n in WEIGHTS])
```

```python
import jax
import jax.numpy as jnp
from jax import lax
from jax.experimental import pallas as pl
from jax.experimental.pallas import tpu as pltpu

F32 = jnp.float32
BF16 = jnp.bfloat16

D = 1024
NH = 4
DH = 128
HW = NH * DH
DFF = 4 * D
NMEM = 256
EPS = 1e-6
NDEV = 8
LANES = 128
PAIR = 128
CHUNK = 64
D_IN = 7176
D_INP = 7680
VMEM_LIMIT = 56 * 1024 * 1024

ADAM_LR, ADAM_B1, ADAM_B2, ADAM_EPS, ADAM_WD, ADAM_STEP = 0.001, 0.9, 0.999, 1e-08, 0.01, 10

CB_Z, CB_SQ, CB_SK, CB_SV, CB_MQ, CB_AB = 3, 4, 5, 6, 7, 14

NN = (((1,), (0,)), ((), ()))
NT = (((1,), (1,)), ((), ()))
TN = (((0,), (0,)), ((), ()))

BIG = ("w_in", "w_mem_kv", "w_br_gdn", "w_br_sb", "w_br_mem", "w_o", "w_up", "w_down", "conv_w")
BIG_ROWS = (7176, 1024, 512, 512, 512, 1024, 4096, 4096, 6)
SLAB_ROWS = dict(zip(BIG, BIG_ROWS))
SLAB_TILE = 1216
FIRST = ("w_in", "conv_w")
REST = ("w_mem_kv", "w_br_gdn", "w_br_sb", "w_br_mem", "w_o", "w_up", "w_down")
R_REST = 10 * SLAB_TILE
FIRST_LANES = 1024
R_FIRST = 1040
SMALL = ("norm1_g", "a_log", "dt_bias", "gdn_norm_g", "sb_q_norm_g", "sb_k_norm_g", "mem_norm_g",
         "mem_q_norm_g", "mem_k_norm_g", "norm2_g")
VEC = ("norm1_g", "mem_norm_g", "norm2_g", "gdn_norm_g", "sb_q_norm_g", "sb_k_norm_g", "mem_q_norm_g", "mem_k_norm_g",
       "a_log", "dt_bias")
VEC_SIZES = (1024, 1024, 1024, 128, 128, 128, 128, 128, 4, 4)
VEC_OFFSETS = (0, 1024, 2048, 3072, 3200, 3328, 3456, 3584, 3712, 3716)
VEC_WIDTH = 3840
WEIGHTS = ("norm1_g", "w_in", "conv_w", "a_log", "dt_bias", "gdn_norm_g", "sb_q_norm_g", "sb_k_norm_g",
           "mem_norm_g", "w_mem_kv", "mem_q_norm_g", "mem_k_norm_g", "w_br_gdn", "w_br_sb", "w_br_mem",
           "w_o", "norm2_g", "w_up", "w_down")


def _cp(sem=None):
    return pltpu.CompilerParams(dimension_semantics=sem, vmem_limit_bytes=VMEM_LIMIT)


def _dot(a, b, dims=NN):
    return lax.dot_general(a, b, dims, preferred_element_type=F32)


def _dbf(a, b, dims=NN):
    return _dot(a.astype(BF16), b.astype(BF16), dims)


def _split(a, n):
    parts = []
    for _ in range(n):
        h = a.astype(BF16)
        parts.append(h)
        a = a - h.astype(F32)
    return parts


def _dg(a, b, dims=NN):
    return _dbf(a, b, dims)


def _dxr(a, e, dims=NN):
    eb = e.astype(BF16)
    a1, a2, a3 = _split(a, 3)
    return _dot(a1, eb, dims) + (_dot(a2, eb, dims) + _dot(a3, eb, dims))


def _dxl(e, a, dims=NN):
    eb = e.astype(BF16)
    a1, a2, a3 = _split(a, 3)
    return _dot(eb, a1, dims) + (_dot(eb, a2, dims) + _dot(eb, a3, dims))


def _sigmoid(x):
    return 1.0 / (1.0 + jnp.exp(-x))


def _softplus(x):
    return jnp.maximum(x, 0.0) + jnp.log(1.0 + jnp.exp(-jnp.abs(x)))


def _rms(x, g):
    r = lax.rsqrt(jnp.mean(x * x, axis=-1, keepdims=True) + EPS)
    return x * r * g, r


def _rms_bwd(dy, x, g, r):
    dyg = dy * g
    dx = r * (dyg - x * (r * r) * jnp.mean(dyg * x, axis=-1, keepdims=True))
    dg = jnp.sum(dy * (x * r), axis=0, keepdims=True)
    return dx, dg


def _hs(h):
    return slice(h * DH, (h + 1) * DH)


def _row_tile(s):
    return 512 if s >= 2048 else 256


def _narrow_tile(s):
    return min(256, s)


def _mm(name, a, b, mode, tm, tn, tk, pro=None, pro_g=None, epi=None, epi_x=None, out_dtype=F32, n_outer=False,
        comm=None):
    if mode == "tn":
        K, M = a.shape
    else:
        M, K = a.shape
    N = b.shape[0] if mode == "nt" else b.shape[1]
    tm, tn, tk = min(tm, M), min(tn, N), min(tk, K)
    nk = K // tk
    assert M % tm == 0 and N % tn == 0 and K % tk == 0, (name, M, N, K, tm, tn, tk)
    dims = {"nn": NN, "nt": NT, "tn": TN}[mode]
    reducing = epi in ("rms_bwd", "loss")
    assert not reducing or (tn == N and not n_outer), name
    epi_ops = () if epi is None else (epi_x if isinstance(epi_x, tuple) else (epi_x,))

    def body(*refs):
        a_ref, b_ref = refs[0], refs[1]
        pos = 2
        g_ref = None
        if pro == "rms":
            g_ref = refs[pos]
            pos += 1
        e_refs = refs[pos:pos + len(epi_ops)]
        pos += len(epi_ops)
        cx_ref = None
        if comm is not None:
            cx_ref = refs[pos]
            pos += 1
        o_ref = refs[pos]
        pos += 1
        r_ref = None
        if reducing:
            r_ref = refs[pos]
            pos += 1
        if comm is not None:
            steps_of = _gather_steps if comm[0] == "gather" else _chip_steps
            start, forward, finish_comm = steps_of(cx_ref, refs[pos], *refs[-3:])
            pos += 1
            step = (pl.program_id(0) * grid[1] + pl.program_id(1)) * nk + pl.program_id(2)
            total = grid[0] * grid[1] * nk
            pl.when(step == 0)(start)
            pl.when(step == (4 * total) // 5)(forward)
        av = a_ref[...]
        if pro == "rms":
            av, _ = _rms(av.astype(F32), g_ref[...])
        elif pro == "relu2":
            av = jnp.square(jnp.maximum(av, 0.0))
        part = _dbf(av, b_ref[...], dims)
        first = pl.program_id(0) == 0

        def finish(acc):
            red = None
            if epi == "add":
                acc = acc + e_refs[0][...]
            elif epi == "drelu2":
                acc = acc * (2.0 * jnp.maximum(e_refs[0][...], 0.0))
            elif epi == "rms_bwd":
                xv, gv = e_refs[0][...], e_refs[1][...]
                _, r = _rms(xv, gv)
                dx, red = _rms_bwd(acc, xv, gv, r)
                acc = dx + e_refs[2][...]
            elif epi == "loss":
                err = acc + e_refs[0][...] - e_refs[1][...]
                acc = err * (1.0 / N)
                per_tok = jnp.sum(err * err, axis=1, keepdims=True) * (1.0 / N)
                red = 0.5 * jnp.sum(per_tok, axis=0, keepdims=True)
            o_ref[...] = acc.astype(out_dtype)
            if reducing:

                @pl.when(first)
                def _():
                    r_ref[...] = red

                @pl.when(jnp.logical_not(first))
                def _():
                    r_ref[...] += red

        if nk == 1:
            finish(part)
        else:
            acc_ref = refs[pos]
            k = pl.program_id(2)

            @pl.when(k == 0)
            def _():
                acc_ref[...] = part

            @pl.when(k > 0)
            def _():
                acc_ref[...] += part

            @pl.when(k == nk - 1)
            def _():
                finish(acc_ref[...])

        if comm is not None:
            pl.when(step == total - 1)(finish_comm)

    def spec(shape, index):
        if n_outer:
            return pl.BlockSpec(shape, lambda j, i, k: index(i, j, k))
        return pl.BlockSpec(shape, index)

    if mode == "tn":
        a_spec = spec((tk, tm), lambda i, j, k: (k, i))
    else:
        a_spec = spec((tm, tk), lambda i, j, k: (i, k))
    if mode == "nt":
        b_spec = spec((tn, tk), lambda i, j, k: (j, k))
    else:
        b_spec = spec((tk, tn), lambda i, j, k: (k, j))
    in_specs, ops = [a_spec, b_spec], [a, b]
    if pro == "rms":
        w = pro_g.shape[1]
        assert (tm if mode == "tn" else tk) == w, name
        in_specs.append(spec((1, w), lambda i, j, k: (0, 0)))
        ops.append(pro_g)
    for op in epi_ops:
        if op.shape[0] == 1:
            in_specs.append(spec((1, tn), lambda i, j, k: (0, j)))
        else:
            in_specs.append(spec((tm, tn), lambda i, j, k: (i, j)))
        ops.append(op)
    out_specs = [spec((tm, tn), lambda i, j, k: (i, j))]
    out_shape = [jax.ShapeDtypeStruct((M, N), out_dtype)]
    if reducing:
        width = N if epi == "rms_bwd" else 1
        out_specs.append(spec((1, width), lambda i, j, k: (0, 0)))
        out_shape.append(jax.ShapeDtypeStruct((1, width), F32))
    scratch = [pltpu.VMEM((tm, tn), F32)] if nk > 1 else []
    if comm is not None:
        kind, cx = comm
        in_specs.append(HBM_SPEC)
        ops.append(cx)
        out_specs.append(HBM_SPEC)
        out_shape.append(jax.ShapeDtypeStruct((NDEV if kind == "gather" else NDEV // 2,) + cx.shape[-2:], cx.dtype))
        scratch += list(GATHER_SEMS if kind == "gather" else CHIP_SEMS)
    grid = (N // tn, M // tm, nk) if n_outer else (M // tm, N // tn, nk)
    ordered = reducing or comm is not None
    outs = pl.pallas_call(
        body, name=name, grid=grid,
        in_specs=in_specs, out_specs=out_specs, out_shape=out_shape, scratch_shapes=scratch,
        compiler_params=_cp(("arbitrary" if ordered else "parallel", "arbitrary" if comm is not None else "parallel",
                             "arbitrary")),
    )(*ops)
    return outs if len(out_shape) > 1 else outs[0]


def _head_select(first_lane):
    l = lax.broadcasted_iota(jnp.int32, (LANES, HW), 0)
    c = lax.broadcasted_iota(jnp.int32, (LANES, HW), 1)
    return (l == first_lane + c // DH).astype(F32)


def _conv_taps(buf, cw, ts):
    c = cw[3:4, :] * buf[8:8 + ts, :]
    for j in range(3):
        k = 3 - j
        c = c + cw[j:j + 1, :] * buf[8 - k:8 - k + ts, :]
    return c


def _pre_fwd(proj, conv_w, alog_f, dtb_f, gsq, gsk, gmq, S):
    ts = _row_tile(S)
    hb = ts // 8

    def body(qkv_ref, halo_ref, ab_ref, sq_ref, sk_ref, sv_ref, mq_ref, cw_ref, al_ref, dt_ref, gsq_ref, gsk_ref,
             gmq_ref, gq_o, gk_o, gv_o, gf_o, bf_o, sqn_o, skn_o, svb_o, qmn_o, buf):
        i = pl.program_id(0)
        buf[0:8, :] = jnp.where(i == 0, 0.0, halo_ref[...])
        buf[8:8 + ts, :] = qkv_ref[...]
        c = _conv_taps(buf, cw_ref[...], ts)
        a = c * _sigmoid(c)
        for h in range(NH):
            q = a[:, h * DH:(h + 1) * DH]
            k = a[:, HW + h * DH:HW + (h + 1) * DH]
            gq_o[:, _hs(h)] = q * (lax.rsqrt(jnp.sum(q * q, axis=-1, keepdims=True) + EPS) * DH ** -0.5)
            gk_o[:, _hs(h)] = k * lax.rsqrt(jnp.sum(k * k, axis=-1, keepdims=True) + EPS)
            sqn_o[:, _hs(h)] = _rms(sq_ref[:, _hs(h)], gsq_ref[...])[0].astype(BF16)
            skn_o[:, _hs(h)] = _rms(sk_ref[:, _hs(h)], gsk_ref[...])[0].astype(BF16)
            qmn_o[:, _hs(h)] = _rms(mq_ref[:, _hs(h)], gmq_ref[...])[0].astype(BF16)
        gv_o[...] = a[:, 2 * HW:3 * HW]
        svb_o[...] = sv_ref[...].astype(BF16)
        ab = ab_ref[:, 0:LANES]
        a_bc = _dxr(ab, _head_select(0))
        b_bc = _dxr(ab, _head_select(NH))
        gf_o[...] = -jnp.exp(al_ref[...]) * _softplus(a_bc + dt_ref[...])
        bf_o[...] = _sigmoid(b_bc)

    row = lambda cb: pl.BlockSpec((ts, HW), lambda i: (i, cb))
    full = lambda r, c: pl.BlockSpec((r, c), lambda i: (0, 0))
    f32o = jax.ShapeDtypeStruct((S, HW), F32)
    bfo = jax.ShapeDtypeStruct((S, HW), BF16)
    return pl.pallas_call(
        body, name="pre_fwd", grid=(S // ts,),
        in_specs=[pl.BlockSpec((ts, 3 * HW), lambda i: (i, 0)),
                  pl.BlockSpec((8, 3 * HW), lambda i: (jnp.maximum(i * hb - 1, 0), 0)),
                  row(CB_AB), row(CB_SQ), row(CB_SK), row(CB_SV), row(CB_MQ),
                  full(4, 3 * HW), full(1, HW), full(1, HW), full(1, DH), full(1, DH), full(1, DH)],
        out_specs=[pl.BlockSpec((ts, HW), lambda i: (i, 0))] * 9,
        out_shape=[f32o, f32o, f32o, f32o, f32o, bfo, bfo, bfo, bfo],
        scratch_shapes=[pltpu.VMEM((ts + 8, 3 * HW), F32)],
        compiler_params=_cp(("parallel",)),
    )(proj, proj, proj, proj, proj, proj, proj, conv_w, alog_f, dtb_f, gsq, gsk, gmq)


def _pre_bwd(proj, conv_w, alog_f, dtb_f, gsq, gsk, dgq, dgk, dgv, dgf, dbf, dsqn, dskn, S):
    ts = _narrow_tile(S)
    hb = ts // 8

    def body(qkv_ref, halo_ref, ab_ref, sq_ref, sk_ref, cw_ref, al_ref, dt_ref, gsq_ref, gsk_ref,
             dgq_ref, dgk_ref, dgv_ref, dgf_ref, dbf_ref, dsqn_ref, dskn_ref,
             dc_o, dab_o, dsq_o, dsk_o, dcw_o, dal_o, ddt_o, dgsq_o, dgsk_o, buf):
        i = pl.program_id(0)

        @pl.when(i == 0)
        def _():
            dcw_o[...] = jnp.zeros_like(dcw_o)
            dal_o[...] = jnp.zeros_like(dal_o)
            ddt_o[...] = jnp.zeros_like(ddt_o)
            dgsq_o[...] = jnp.zeros_like(dgsq_o)
            dgsk_o[...] = jnp.zeros_like(dgsk_o)

        buf[0:8, :] = jnp.where(i == 0, 0.0, halo_ref[...])
        buf[8:8 + ts, :] = qkv_ref[...]
        c = _conv_taps(buf, cw_ref[...], ts)
        sg = _sigmoid(c)
        a = c * sg
        dsilu = sg * (1.0 + c * (1.0 - sg))
        dgsq = jnp.zeros((1, DH), F32)
        dgsk = jnp.zeros((1, DH), F32)
        for h in range(NH):
            q = a[:, h * DH:(h + 1) * DH]
            k = a[:, HW + h * DH:HW + (h + 1) * DH]
            nq = lax.rsqrt(jnp.sum(q * q, axis=-1, keepdims=True) + EPS)
            nk = lax.rsqrt(jnp.sum(k * k, axis=-1, keepdims=True) + EPS)
            dyq = dgq_ref[:, _hs(h)]
            dyk = dgk_ref[:, _hs(h)]
            dq = (nq * dyq - q * (nq * nq * nq) * jnp.sum(dyq * q, axis=-1, keepdims=True)) * DH ** -0.5
            dk = nk * dyk - k * (nk * nk * nk) * jnp.sum(dyk * k, axis=-1, keepdims=True)
            dc_o[:, h * DH:(h + 1) * DH] = dq * dsilu[:, h * DH:(h + 1) * DH]
            dc_o[:, HW + h * DH:HW + (h + 1) * DH] = dk * dsilu[:, HW + h * DH:HW + (h + 1) * DH]
            x = sq_ref[:, _hs(h)]
            _, r = _rms(x, gsq_ref[...])
            dx, dg = _rms_bwd(dsqn_ref[:, _hs(h)], x, gsq_ref[...], r)
            dsq_o[:, _hs(h)] = dx.astype(BF16)
            dgsq = dgsq + dg
            x = sk_ref[:, _hs(h)]
            _, r = _rms(x, gsk_ref[...])
            dx, dg = _rms_bwd(dskn_ref[:, _hs(h)], x, gsk_ref[...], r)
            dsk_o[:, _hs(h)] = dx.astype(BF16)
            dgsk = dgsk + dg
        dc_o[:, 2 * HW:3 * HW] = dgv_ref[...] * dsilu[:, 2 * HW:3 * HW]
        dgsq_o[...] += dgsq
        dgsk_o[...] += dgsk
        dc = dc_o[...]
        for j in range(4):
            k = 3 - j
            dcw_o[j:j + 1, :] += jnp.sum(dc * buf[8 - k:8 - k + ts, :], axis=0, keepdims=True)
        ab = ab_ref[:, 0:LANES]
        a_bc = _dxr(ab, _head_select(0))
        b_bc = _dxr(ab, _head_select(NH))
        pre = a_bc + dt_ref[...]
        ea = jnp.exp(al_ref[...])
        dgf = dgf_ref[...]
        dal_o[...] += jnp.sum(dgf * (-ea * _softplus(pre)), axis=0, keepdims=True)
        da = dgf * (-ea * _sigmoid(pre))
        ddt_o[...] += jnp.sum(da, axis=0, keepdims=True)
        beta = _sigmoid(b_bc)
        db = dbf_ref[...] * beta * (1.0 - beta)
        lane = lax.broadcasted_iota(jnp.int32, (ts, LANES), 1)
        dab = jnp.zeros((ts, LANES), F32)
        for h in range(NH):
            dab = dab + jnp.where(lane == h, da[:, _hs(h)], 0.0) + jnp.where(lane == NH + h, db[:, _hs(h)], 0.0)
        dab_o[:, 0:LANES] = dab.astype(BF16)
        dab_o[:, LANES:HW] = jnp.zeros((ts, HW - LANES), BF16)

    row = lambda cb: pl.BlockSpec((ts, HW), lambda i: (i, cb))
    full = lambda r, c: pl.BlockSpec((r, c), lambda i: (0, 0))
    t512 = pl.BlockSpec((ts, HW), lambda i: (i, 0))
    return pl.pallas_call(
        body, name="pre_bwd", grid=(S // ts,),
        in_specs=[pl.BlockSpec((ts, 3 * HW), lambda i: (i, 0)),
                  pl.BlockSpec((8, 3 * HW), lambda i: (jnp.maximum(i * hb - 1, 0), 0)),
                  row(CB_AB), row(CB_SQ), row(CB_SK),
                  full(4, 3 * HW), full(1, HW), full(1, HW), full(1, DH), full(1, DH)] + [t512] * 7,
        out_specs=[pl.BlockSpec((ts, 3 * HW), lambda i: (i, 0)), t512, t512, t512,
                   full(4, 3 * HW), full(1, HW), full(1, HW), full(1, DH), full(1, DH)],
        out_shape=[jax.ShapeDtypeStruct((S, 3 * HW), F32)] + [jax.ShapeDtypeStruct((S, HW), BF16)] * 3
        + [jax.ShapeDtypeStruct((4, 3 * HW), F32), jax.ShapeDtypeStruct((1, HW), F32),
           jax.ShapeDtypeStruct((1, HW), F32), jax.ShapeDtypeStruct((1, DH), F32),
           jax.ShapeDtypeStruct((1, DH), F32)],
        scratch_shapes=[pltpu.VMEM((ts + 8, 3 * HW), F32)],
        compiler_params=_cp(("arbitrary",)),
    )(proj, proj, proj, proj, proj, conv_w, alog_f, dtb_f, gsq, gsk, dgq, dgk, dgv, dgf, dbf, dsqn, dskn)


def _conv_bwd(dc, conv_w, S):
    ts = _row_tile(S)
    hb = ts // 8
    n = S // ts

    def body(dc_ref, halo_ref, cw_ref, o_ref, buf):
        i = pl.program_id(0)
        buf[0:ts, :] = dc_ref[...]
        buf[ts:ts + 8, :] = jnp.where(i == n - 1, 0.0, halo_ref[...])
        cw = cw_ref[...]
        acc = cw[3:4, :] * buf[0:ts, :]
        for k in range(1, 4):
            acc = acc + cw[3 - k:4 - k, :] * buf[k:k + ts, :]
        o_ref[...] = acc.astype(BF16)

    return pl.pallas_call(
        body, name="conv_bwd", grid=(n,),
        in_specs=[pl.BlockSpec((ts, 3 * HW), lambda i: (i, 0)),
                  pl.BlockSpec((8, 3 * HW), lambda i: (jnp.minimum((i + 1) * hb, S // 8 - 1), 0)),
                  pl.BlockSpec((4, 3 * HW), lambda i: (0, 0))],
        out_specs=pl.BlockSpec((ts, 3 * HW), lambda i: (i, 0)),
        out_shape=jax.ShapeDtypeStruct((S, 3 * HW), BF16),
        scratch_shapes=[pltpu.VMEM((ts + 8, 3 * HW), F32)],
        compiler_params=_cp(("parallel",)),
    )(dc, dc, conv_w)


def _gdn_masks():
    r = lax.broadcasted_iota(jnp.int32, (PAIR, PAIR), 0)
    c = lax.broadcasted_iota(jnp.int32, (PAIR, PAIR), 1)
    same = ((r >= CHUNK) & (c >= CHUNK)) | ((r < CHUNK) & (c < CHUNK))
    return dict(r=r, same=same, tril=same & (r >= c), strict=same & (r > c), triu=same & (c >= r), eye=r == c,
                in_a=r < CHUNK, last_a=r == CHUNK - 1, last_b=r == PAIR - 1)


def _each(fn, *cols):
    return [fn(*xs) for xs in zip(*cols)]


def _mul(a, b):
    return a * b


def _top(x):
    return x[:CHUNK]


def _bot(x):
    return x[CHUNK:]


def _rows(a, b):
    return jnp.concatenate([a, b], axis=0)


def _tri_inv(lm, eye):
    eye_f = eye.astype(F32)
    p = _each(lambda l: eye_f - l, lm)
    lp = _each(lambda l: _dg(l, l), lm)
    for it in range(5):
        p = _each(lambda a, b: a + _dg(a, b), p, lp)
        if it < 4:
            lp = _each(lambda b: _dg(b, b), lp)
    return p


def _gdn_block(m, q, k, v, g, beta):
    tril_f = m["tril"].astype(F32)
    col_sum = lambda mask: (lambda x: jnp.sum(jnp.where(mask, x, 0.0), axis=0, keepdims=True))
    gc = _each(lambda x: _dxl(tril_f, x), g)
    gcr = _each(col_sum(m["eye"]), gc)
    gam = _each(lambda a, b: jnp.where(m["tril"], jnp.exp(jnp.minimum(a - b, 0.0)), 0.0), gc, gcr)
    kb = _each(_mul, k, beta)
    vb = _each(_mul, v, beta)
    lm = _each(lambda a, b, c: jnp.where(m["strict"], _dg(a, b, NT) * c, 0.0), kb, k, gam)
    t = _tri_inv(lm, m["eye"])
    eg = _each(jnp.exp, gc)
    kbe = _each(_mul, kb, eg)
    u = _each(_dg, t, vb)
    w = _each(_dg, t, kbe)
    aqk = _each(lambda a, b, c: jnp.where(m["tril"], _dg(a, b, NT) * c, 0.0), q, k, gam)
    qd = _each(_mul, q, eg)
    ga = _each(col_sum(m["last_a"]), gc)
    gb = _each(col_sum(m["last_b"]), gc)
    e2 = _each(lambda a, b, c: jnp.exp(jnp.where(m["in_a"], a, b) - c), ga, gb, gc)
    kd = _each(_mul, k, e2)
    return dict(u=u, w=w, aqk=aqk, qd=qd, kd=kd, gam=gam, kb=kb, vb=vb, lm=lm, t=t, eg=eg, kbe=kbe, e2=e2,
                gla=_each(jnp.exp, ga), glb=_each(jnp.exp, gb))


def _gdn_fwd(gq, gk, gv, gf, bf, S):
    nb = S // PAIR

    def body(q_ref, k_ref, v_ref, g_ref, b_ref, o_ref, st_ref, s_scr):
        @pl.when(pl.program_id(0) == 0)
        def _():
            s_scr[...] = jnp.zeros_like(s_scr)

        m = _gdn_masks()
        heads = lambda ref: [ref[:, _hs(h)] for h in range(NH)]
        f = _gdn_block(m, heads(q_ref), heads(k_ref), heads(v_ref), heads(g_ref), heads(b_ref))
        u, w, qd, kd = f["u"], f["w"], f["qd"], f["kd"]
        s0 = [s_scr[h * DH:(h + 1) * DH, :] for h in range(NH)]
        vna = _each(lambda a, b, s: _top(a) - _dg(_top(b), s), u, w, s0)
        oa = _each(lambda a, s: _dg(_top(a), s), qd, s0)
        s1 = _each(lambda s, gl, a, vn: s * gl + _dg(_top(a), vn, TN), s0, f["gla"], kd, vna)
        vnb = _each(lambda a, b, s: _bot(a) - _dg(_bot(b), s), u, w, s1)
        ob = _each(lambda a, s: _dg(_bot(a), s), qd, s1)
        s2 = _each(lambda s, gl, a, vn: s * gl + _dg(_bot(a), vn, TN), s1, f["glb"], kd, vnb)
        outs = _each(lambda a, b, c, va, vb: _rows(a, b) + _dg(c, _rows(va, vb)), oa, ob, f["aqk"], vna, vnb)
        o_ref[...] = jnp.concatenate(outs, axis=1)
        st_ref[...] = jnp.concatenate(s0 + s1, axis=0)
        s_scr[...] = jnp.concatenate(s2, axis=0)

    blk = pl.BlockSpec((PAIR, HW), lambda i: (i, 0))
    return pl.pallas_call(
        body, name="gdn_fwd", grid=(nb,),
        in_specs=[blk] * 5,
        out_specs=[blk, pl.BlockSpec((2 * NH * DH, DH), lambda i: (i, 0))],
        out_shape=[jax.ShapeDtypeStruct((S, HW), F32), jax.ShapeDtypeStruct((nb * 2 * NH * DH, DH), F32)],
        scratch_shapes=[pltpu.VMEM((NH * DH, DH), F32)],
        compiler_params=_cp(("arbitrary",)),
    )(gq, gk, gv, gf, bf)


def _gdn_bwd(gq, gk, gv, gf, bf, states, do, S):
    nb = S // PAIR

    def body(q_ref, k_ref, v_ref, g_ref, b_ref, st_ref, do_ref, dq_o, dk_o, dv_o, dg_o, db_o, ds_scr):
        @pl.when(pl.program_id(0) == 0)
        def _():
            ds_scr[...] = jnp.zeros_like(ds_scr)

        m = _gdn_masks()
        ones = jnp.ones((PAIR, PAIR), F32)
        heads = lambda ref: [ref[:, _hs(h)] for h in range(NH)]
        q, k, v, beta, do = heads(q_ref), heads(k_ref), heads(v_ref), heads(b_ref), heads(do_ref)
        f = _gdn_block(m, q, k, v, heads(g_ref), beta)
        u, w, aqk, qd, kd, t = f["u"], f["w"], f["aqk"], f["qd"], f["kd"], f["t"]
        s0 = [st_ref[h * DH:(h + 1) * DH, :] for h in range(NH)]
        s1 = [st_ref[(NH + h) * DH:(NH + h + 1) * DH, :] for h in range(NH)]
        ds2 = [ds_scr[h * DH:(h + 1) * DH, :] for h in range(NH)]
        total = lambda a, b: jnp.sum(jnp.sum(a * b, axis=1, keepdims=True), axis=0, keepdims=True)
        vna = _each(lambda a, b, s: _top(a) - _dg(_top(b), s), u, w, s0)
        vnb = _each(lambda a, b, s: _bot(a) - _dg(_bot(b), s), u, w, s1)
        dvn_i = _each(lambda a, b: _dg(a, b, TN), aqk, do)
        dvnb = _each(lambda a, b, s: _bot(a) + _dg(_bot(b), s), dvn_i, kd, ds2)
        dqdb = _each(lambda a, s: _dg(_bot(a), s, NT), do, s1)
        dkdb = _each(lambda a, s: _dg(a, s, NT), vnb, ds2)
        dglb = _each(total, ds2, s1)
        dwb = _each(lambda a, s: -_dg(a, s, NT), dvnb, s1)
        ds1 = _each(lambda s, gl, a, b, c, d: s * gl + _dg(_bot(a), _bot(b), TN) - _dg(_bot(c), d, TN),
                    ds2, f["glb"], qd, do, w, dvnb)
        dvna = _each(lambda a, b, s: _top(a) + _dg(_top(b), s), dvn_i, kd, ds1)
        dqda = _each(lambda a, s: _dg(_top(a), s, NT), do, s0)
        dkda = _each(lambda a, s: _dg(a, s, NT), vna, ds1)
        dgla = _each(total, ds1, s0)
        dwa = _each(lambda a, s: -_dg(a, s, NT), dvna, s0)
        ds0 = _each(lambda s, gl, a, b, c, d: s * gl + _dg(_top(a), _top(b), TN) - _dg(_top(c), d, TN),
                    ds1, f["gla"], qd, do, w, dvna)
        dvn, dqd, dkd, dw = (_each(_rows, a, b) for a, b in ((dvna, dvnb), (dqda, dqdb), (dkda, dkdb), (dwa, dwb)))
        daqk = _each(lambda a, va, vb: jnp.where(m["tril"], _dg(a, _rows(va, vb), NT), 0.0), do, vna, vnb)
        dt = _each(lambda a, b, c, d: _dg(a, b, NT) + _dg(c, d, NT), dvn, f["vb"], dw, f["kbe"])
        dvb = _each(lambda a, b: _dg(a, b, TN), t, dvn)
        dkbe = _each(lambda a, b: _dg(a, b, TN), t, dw)
        dtt = _each(lambda a, b: _dg(a, b, NT), dt, t)
        dl = _each(lambda a, b: -jnp.where(m["strict"], _dg(a, b, TN), 0.0), t, dtt)
        dm = _each(_mul, dl, f["gam"])
        dn = _each(_mul, daqk, f["gam"])
        dkb = _each(lambda a, b, c, d: _dg(a, b) + c * d, dm, k, dkbe, f["eg"])
        dks = _each(lambda a, b, c, d, e, g, h, i: _dg(a, b, TN) + _dg(c, d, TN) + e * g + h * i,
                    dm, f["kb"], dn, q, dkd, f["e2"], beta, dkb)
        dqs = _each(lambda a, b, c, d: _dg(a, b) + c * d, dn, k, dqd, f["eg"])
        gm = _each(lambda a, b, c, d: a * b + c * d, dl, f["lm"], daqk, aqk)
        dkdkd = _each(_mul, dkd, kd)
        dgc = _each(lambda a, b, c, d, e, g: _dxr(a + b * c + d * e - g, ones) - _dxr(a, ones, TN),
                    gm, dqd, qd, dkbe, f["kbe"], dkdkd)
        same_f = m["same"].astype(F32)
        chunk_tot = _each(lambda a: _dxl(same_f, _dxr(a, ones)), dkdkd)
        last = m["last_a"] | m["last_b"]
        dgc = _each(lambda a, b, ga, gla, gb, glb: a + jnp.where(last, b + jnp.where(m["in_a"], ga * gla, gb * glb), 0.0),
                    dgc, chunk_tot, dgla, f["gla"], dglb, f["glb"])
        dbs = _each(lambda a, b, c, d: _dxr(a * b + c * d, ones), dkb, k, dvb, v)
        dvs = _each(_mul, beta, dvb)
        triu_f = m["triu"].astype(F32)
        dgs = _each(lambda a: _dxl(triu_f, a), dgc)
        for ref, parts in ((dq_o, dqs), (dk_o, dks), (dv_o, dvs), (dg_o, dgs), (db_o, dbs)):
            ref[...] = jnp.concatenate(parts, axis=1)
        ds_scr[...] = jnp.concatenate(ds0, axis=0)

    blk = pl.BlockSpec((PAIR, HW), lambda i: (nb - 1 - i, 0))
    o = jax.ShapeDtypeStruct((S, HW), F32)
    return pl.pallas_call(
        body, name="gdn_bwd", grid=(nb,),
        in_specs=[blk] * 5 + [pl.BlockSpec((2 * NH * DH, DH), lambda i: (nb - 1 - i, 0)), blk],
        out_specs=[blk] * 5, out_shape=[o] * 5,
        scratch_shapes=[pltpu.VMEM((NH * DH, DH), F32)],
        compiler_params=_cp(("arbitrary",)),
    )(gq, gk, gv, gf, bf, states, do)


SB_T = 256
SB_GROUP = 4
SB_GROUP_BWD = 4
SB_SINGLES = 1
SB_DEAD = -110.0


def _group_sizes(g):
    sizes = []
    while g >= 1:
        sizes.append(g)
        g //= 2
    return sizes


def _sb_iotas(t):
    return lax.broadcasted_iota(jnp.int32, (t, t), 0), lax.broadcasted_iota(jnp.int32, (t, t), 1)


def _sb_scores(q, k, mask):
    z = _dot(q, k, NT) * DH ** -0.5
    ls = jnp.minimum(z, 0.0) - jnp.log(1.0 + jnp.exp(-jnp.abs(z)))
    lneg = ls - z
    if mask is not None:
        lneg = jnp.where(mask, lneg, 0.0)
    return ls, lneg


def _prefix(x, u):
    xh, xl = _split(x, 2)
    return _dot(xh, u) + _dot(xl, u)


def _sb_fwd(sqn, skn, svb, S):
    t = min(SB_T, S)

    def body(q_ref, k_ref, v_ref, o_ref, t_ref, cnt_ref):
        qb = pl.program_id(1)
        q = q_ref[...]
        r, c = _sb_iotas(t)
        diag = c < r
        u_after = (r > c).astype(BF16)

        def tiles(k0s, run, masks):
            sc = _each(lambda k0, m: _sb_scores(q, k_ref[pl.ds(k0, t), :], m), k0s, masks)
            ls, lneg = [s[0] for s in sc], [s[1] for s in sc]
            sums = _each(lambda x: jnp.sum(x, axis=1, keepdims=True), lneg)
            pre = _each(lambda x: _prefix(x, u_after), lneg)
            runs = [run]
            for s in sums:
                runs.append(runs[-1] + s)
            att = _each(lambda a, b, rn: jnp.exp(a + (rn + b)), ls, pre, runs[:-1])
            att = _each(lambda a, m: a if m is None else jnp.where(m, a, 0.0), att, masks)
            parts = _each(lambda a, k0: _dot(a.astype(BF16), v_ref[pl.ds(k0, t), :]), att, k0s)
            return sum(parts[1:], parts[0]), runs[-1]

        left = jnp.full((t, t), qb > 0)
        acc, run = tiles([pl.multiple_of(qb * t, t), pl.multiple_of(jnp.maximum(qb - 1, 0) * t, t)],
                         jnp.zeros((t, 1), F32), [diag, left])

        def alive(run):
            return jnp.max(run) >= SB_DEAD

        carry, done = (0, acc, run, alive(run)), jnp.minimum(qb, 1)
        for size, limit in [(1, SB_SINGLES)] + [(s, None) for s in _group_sizes(SB_GROUP)]:

            def more(c, size=size, done=done, limit=limit):
                i, _, _, go = c
                fits = done + (i + 1) * size <= qb
                return (fits if limit is None else fits & (i < limit)) & go

            def group(c, size=size, done=done):
                i, acc, run, _ = c
                first = qb - 1 - done - size * i
                part, run = tiles([pl.multiple_of((first - j) * t, t) for j in range(size)], run, [None] * size)
                return i + 1, acc + part, run, alive(run)

            n, acc, run, go = lax.while_loop(more, group, (0,) + carry[1:])
            carry, done = (0, acc, run, go), done + n * size
        o_ref[...] = acc.astype(BF16)
        t_ref[...] = jnp.broadcast_to(run, (t, DH))
        cnt_ref[pl.program_id(0), qb] = done

    qspec = pl.BlockSpec((t, DH), lambda h, i: (i, h))
    kspec = pl.BlockSpec((S, DH), lambda h, i: (0, h))
    return pl.pallas_call(
        body, name="sb_fwd", grid=(NH, S // t),
        in_specs=[qspec, kspec, kspec],
        out_specs=[qspec, qspec, pl.BlockSpec(memory_space=pltpu.SMEM)],
        out_shape=[jax.ShapeDtypeStruct((S, HW), BF16), jax.ShapeDtypeStruct((S, HW), F32),
                   jax.ShapeDtypeStruct((NH, S // t), jnp.int32)],
        compiler_params=_cp(("arbitrary", "arbitrary")),
    )(sqn, skn, svb)


def _sb_bwd(sqn, skn, svb, do, tot, walked, S):
    t = min(SB_T, S)

    def body(cnt_ref, q_ref, k_ref, v_ref, do_ref, t_ref, dq_o, dk_o, dv_o, dv_acc):
        qb = pl.program_id(1)

        @pl.when(qb == 0)
        def _():
            dk_o[...] = jnp.zeros_like(dk_o)
            dv_acc[...] = jnp.zeros_like(dv_acc)

        q = q_ref[...]
        do = do_ref[...].astype(BF16)
        tot_l = jnp.concatenate([t_ref[...]] * (t // DH), axis=1)
        r, c = _sb_iotas(t)
        diag = c < r
        u_upto = (r <= c).astype(BF16)
        u_before = (r < c).astype(BF16)

        def tiles(k0s, run_l, run_e, masks):
            rowsum = lambda x: jnp.sum(x, axis=1, keepdims=True)
            masked = lambda xs: _each(lambda a, m: a if m is None else jnp.where(m, a, 0.0), xs, masks)
            ks = [k_ref[pl.ds(k0, t), :] for k0 in k0s]
            vs = [v_ref[pl.ds(k0, t), :] for k0 in k0s]
            sc = _each(lambda k, m: _sb_scores(q, k, m), ks, masks)
            ls, lneg = [s[0] for s in sc], [s[1] for s in sc]
            sums_l = _each(rowsum, lneg)
            pre_l = _each(lambda x: _prefix(x, u_upto), lneg)
            runs_l = [run_l]
            for s in sums_l:
                runs_l.append(runs_l[-1] + s)
            att = masked(_each(lambda a, b, rn: jnp.exp(a + (tot_l - (rn + b))), ls, pre_l, runs_l[:-1]))
            e = _each(lambda v, a: _dot(do, v, NT) * a, vs, att)
            sums_e = _each(rowsum, e)
            pre_e = _each(lambda x: _prefix(x, u_before), e)
            runs_e = [run_e]
            for s in sums_e:
                runs_e.append(runs_e[-1] + s)
            sg = _each(jnp.exp, ls)
            dz = masked(_each(lambda a, b, rn, s: a * (1.0 - s) - (rn + b) * s, e, pre_e, runs_e[:-1], sg))
            dz = _each(lambda a: (a * DH ** -0.5).astype(BF16), dz)
            dvs = _each(lambda a: _dot(a.astype(BF16), do, TN), att)
            dks = _each(lambda a: _dot(a, q, TN), dz)
            dqs = _each(_dot, dz, ks)
            for k0, dv, dk in zip(k0s, dvs, dks):
                dv_acc[pl.ds(k0, t), :] += dv
                dk_o[pl.ds(k0, t), :] += dk
            return sum(dqs[1:], dqs[0]), runs_l[-1], runs_e[-1]

        walked = cnt_ref[pl.program_id(0), qb]
        early = jnp.maximum(walked - 1, 0)
        z1 = jnp.zeros((t, 1), F32)
        carry, done = (jnp.zeros((t, DH), F32), z1, z1), 0
        for size in _group_sizes(SB_GROUP_BWD):
            n = (early - done) // size

            def group(i, carry, size=size, done=done):
                dq, run_l, run_e = carry
                first = qb - walked + done + size * i
                part, run_l, run_e = tiles([pl.multiple_of((first + j) * t, t) for j in range(size)], run_l, run_e,
                                           [None] * size)
                return dq + part, run_l, run_e

            carry = lax.fori_loop(0, n, group, carry)
            done = done + n * size
        dq, run_l, run_e = carry
        left = jnp.full((t, t), qb > 0)
        part, _, _ = tiles([pl.multiple_of(jnp.maximum(qb - 1, 0) * t, t), pl.multiple_of(qb * t, t)], run_l, run_e,
                           [left, diag])
        dq_o[...] = dq + part

        @pl.when(qb == S // t - 1)
        def _():
            dv_o[...] = dv_acc[...].astype(BF16)

    qspec = pl.BlockSpec((t, DH), lambda h, i, cnt: (i, h))
    kspec = pl.BlockSpec((S, DH), lambda h, i, cnt: (0, h))
    o = jax.ShapeDtypeStruct((S, HW), F32)
    return pl.pallas_call(
        body, name="sb_bwd",
        grid_spec=pltpu.PrefetchScalarGridSpec(
            num_scalar_prefetch=1, grid=(NH, S // t),
            in_specs=[qspec, kspec, kspec, qspec, qspec], out_specs=[qspec, kspec, kspec],
            scratch_shapes=[pltpu.VMEM((S, DH), F32)]),
        out_shape=[o, o, jax.ShapeDtypeStruct((S, HW), BF16)],
        compiler_params=_cp(("parallel", "arbitrary")),
    )(walked, sqn, skn, svb, do, tot)


def _mem_probs(qn, kn):
    s = _dot(qn, kn.astype(BF16), NT) * DH ** -0.5
    p = jnp.exp(s - jnp.max(s, axis=-1, keepdims=True))
    return p / jnp.sum(p, axis=-1, keepdims=True)


def _mem_fwd(qmn, kv, gmk, S):
    ts = _row_tile(S)

    def body(q_ref, kv_ref, gk_ref, o_ref):
        for h in range(NH):
            kn, _ = _rms(kv_ref[:, _hs(h)], gk_ref[...])
            p = _mem_probs(q_ref[:, _hs(h)], kn)
            o_ref[:, _hs(h)] = _dbf(p, kv_ref[:, HW + h * DH:HW + (h + 1) * DH]).astype(BF16)

    return pl.pallas_call(
        body, name="mem_fwd", grid=(S // ts,),
        in_specs=[pl.BlockSpec((ts, HW), lambda i: (i, 0)), pl.BlockSpec((NMEM, 2 * HW), lambda i: (0, 0)),
                  pl.BlockSpec((1, DH), lambda i: (0, 0))],
        out_specs=pl.BlockSpec((ts, HW), lambda i: (i, 0)),
        out_shape=jax.ShapeDtypeStruct((S, HW), BF16),
        compiler_params=_cp(("parallel",)),
    )(qmn, kv, gmk)


def _mem_bwd(proj, qmn, kv, gmq, gmk, do, S):
    ts = _row_tile(S)
    n = S // ts

    def body(mq_ref, q_ref, kv_ref, gq_ref, gk_ref, do_ref, dmq_o, dkv_o, dgq_o, dgk_o, dkn_scr):
        i = pl.program_id(0)

        @pl.when(i == 0)
        def _():
            dkv_o[...] = jnp.zeros_like(dkv_o)
            dgq_o[...] = jnp.zeros_like(dgq_o)
            dkn_scr[...] = jnp.zeros_like(dkn_scr)

        dgq = jnp.zeros((1, DH), F32)
        for h in range(NH):
            km = kv_ref[:, _hs(h)]
            vm = kv_ref[:, HW + h * DH:HW + (h + 1) * DH].astype(BF16)
            kn, _ = _rms(km, gk_ref[...])
            qn = q_ref[:, _hs(h)]
            p = _mem_probs(qn, kn)
            dob = do_ref[:, _hs(h)].astype(BF16)
            dkv_o[:, HW + h * DH:HW + (h + 1) * DH] += _dot(p.astype(BF16), dob, TN)
            dp = _dot(dob, vm, NT)
            dsc = (p * (dp - jnp.sum(dp * p, axis=-1, keepdims=True)) * DH ** -0.5).astype(BF16)
            dkn_scr[:, _hs(h)] += _dot(dsc, qn, TN)
            x = mq_ref[:, _hs(h)]
            _, r = _rms(x, gq_ref[...])
            dx, dg = _rms_bwd(_dot(dsc, kn.astype(BF16)), x, gq_ref[...], r)
            dmq_o[:, _hs(h)] = dx.astype(BF16)
            dgq = dgq + dg
        dgq_o[...] += dgq

        @pl.when(i == n - 1)
        def _():
            dgk = jnp.zeros((1, DH), F32)
            for h in range(NH):
                km = kv_ref[:, _hs(h)]
                _, r = _rms(km, gk_ref[...])
                dx, dg = _rms_bwd(dkn_scr[:, _hs(h)], km, gk_ref[...], r)
                dkv_o[:, _hs(h)] = dx
                dgk = dgk + dg
            dgk_o[...] = dgk

    full = lambda r, c: pl.BlockSpec((r, c), lambda i: (0, 0))
    t512 = pl.BlockSpec((ts, HW), lambda i: (i, 0))
    return pl.pallas_call(
        body, name="mem_bwd", grid=(n,),
        in_specs=[pl.BlockSpec((ts, HW), lambda i: (i, CB_MQ)), t512, full(NMEM, 2 * HW), full(1, DH), full(1, DH),
                  t512],
        out_specs=[t512, full(NMEM, 2 * HW), full(1, DH), full(1, DH)],
        out_shape=[jax.ShapeDtypeStruct((S, HW), BF16), jax.ShapeDtypeStruct((NMEM, 2 * HW), F32),
                   jax.ShapeDtypeStruct((1, DH), F32), jax.ShapeDtypeStruct((1, DH), F32)],
        scratch_shapes=[pltpu.VMEM((NMEM, HW), F32)],
        compiler_params=_cp(("arbitrary",)),
    )(proj, qmn, kv, gmq, gmk, do)


def _gated_gdn(o, z, g):
    sg = _sigmoid(z)
    outs, rs = [], []
    for h in range(NH):
        y, r = _rms(o[:, _hs(h)], g)
        outs.append(y * (z[:, _hs(h)] * sg[:, _hs(h)]))
        rs.append(r)
    return jnp.concatenate(outs, axis=1), rs, sg


def _merge_fwd(x, proj, ogdn, osb, omem, ggdn, wbg, wbs, wbm, wo, S):
    ts = _row_tile(S)

    def body(x_ref, z_ref, g0_ref, g1_ref, g2_ref, og_ref, os_ref, om_ref, gg_ref, wbg_ref, wbs_ref, wbm_ref,
             wo_ref, x1_o, mix_o):
        on, _, _ = _gated_gdn(og_ref[...], z_ref[...], gg_ref[...])
        mix = (_sigmoid(g0_ref[...]) * _dbf(on, wbg_ref[...]) + _sigmoid(g1_ref[...]) * _dbf(os_ref[...], wbs_ref[...])
               + _sigmoid(g2_ref[...]) * _dbf(om_ref[...], wbm_ref[...]))
        mix_o[...] = mix.astype(BF16)
        x1_o[...] = x_ref[...] + _dbf(mix, wo_ref[...])

    t512 = pl.BlockSpec((ts, HW), lambda i: (i, 0))
    t1k = pl.BlockSpec((ts, D), lambda i: (i, 0))
    gate = lambda j: pl.BlockSpec((ts, D), lambda i: (i, 4 + j))
    full = lambda r, c: pl.BlockSpec((r, c), lambda i: (0, 0))
    return pl.pallas_call(
        body, name="merge_fwd", grid=(S // ts,),
        in_specs=[t1k, pl.BlockSpec((ts, HW), lambda i: (i, CB_Z)), gate(0), gate(1), gate(2), t512, t512, t512,
                  full(1, DH), full(HW, D), full(HW, D), full(HW, D), full(D, D)],
        out_specs=[t1k, t1k],
        out_shape=[jax.ShapeDtypeStruct((S, D), F32), jax.ShapeDtypeStruct((S, D), BF16)],
        compiler_params=_cp(("parallel",)),
    )(x, proj, proj, proj, proj, ogdn, osb, omem, ggdn, wbg, wbs, wbm, wo)


def _merge_bwd(dmix, proj, ogdn, osb, omem, ggdn, wbg, wbs, wbm, S):
    ts = _narrow_tile(S)

    def body(dm_ref, z_ref, g0_ref, g1_ref, g2_ref, og_ref, os_ref, om_ref, gg_ref, wbg_ref, wbs_ref, wbm_ref,
             dgl0_o, dgl1_o, dgl2_o, dog_o, dz_o, dos_o, dom_o, dwbg_o, dwbs_o, dwbm_o, dgg_o):
        @pl.when(pl.program_id(0) == 0)
        def _():
            for ref in (dwbg_o, dwbs_o, dwbm_o, dgg_o):
                ref[...] = jnp.zeros_like(ref)

        dm = dm_ref[...]
        og = og_ref[...]
        z = z_ref[...]
        on, rs, sg = _gated_gdn(og, z, gg_ref[...])
        branch = ((on, g0_ref, wbg_ref, dgl0_o, dwbg_o), (os_ref[...], g1_ref, wbs_ref, dgl1_o, dwbs_o),
                  (om_ref[...], g2_ref, wbm_ref, dgl2_o, dwbm_o))
        dos = []
        for o, g_ref, w_ref, dgl_o, dw_o in branch:
            ob = o.astype(BF16)
            gate = _sigmoid(g_ref[...])
            dgl_o[...] = (dm * _dot(ob, w_ref[...]) * gate * (1.0 - gate)).astype(BF16)
            dy = (dm * gate).astype(BF16)
            dw_o[...] += _dot(ob, dy, TN)
            dos.append(_dot(dy, w_ref[...], NT))
        dos_o[...] = dos[1].astype(BF16)
        dom_o[...] = dos[2].astype(BF16)
        don = dos[0]
        dgg = jnp.zeros((1, DH), F32)
        for h in range(NH):
            oh, zh, sh = og[:, _hs(h)], z[:, _hs(h)], sg[:, _hs(h)]
            y = oh * rs[h] * gg_ref[...]
            dz_o[:, _hs(h)] = (don[:, _hs(h)] * y * (sh * (1.0 + zh * (1.0 - sh)))).astype(BF16)
            dx, dg = _rms_bwd(don[:, _hs(h)] * (zh * sh), oh, gg_ref[...], rs[h])
            dog_o[:, _hs(h)] = dx
            dgg = dgg + dg
        dgg_o[...] += dgg

    t512 = pl.BlockSpec((ts, HW), lambda i: (i, 0))
    t1k = pl.BlockSpec((ts, D), lambda i: (i, 0))
    gate = lambda j: pl.BlockSpec((ts, D), lambda i: (i, 4 + j))
    full = lambda r, c: pl.BlockSpec((r, c), lambda i: (0, 0))
    s1k = jax.ShapeDtypeStruct((S, D), BF16)
    s512 = jax.ShapeDtypeStruct((S, HW), BF16)
    wsh = jax.ShapeDtypeStruct((HW, D), F32)
    return pl.pallas_call(
        body, name="merge_bwd", grid=(S // ts,),
        in_specs=[t1k, pl.BlockSpec((ts, HW), lambda i: (i, CB_Z)), gate(0), gate(1), gate(2), t512, t512, t512,
                  full(1, DH), full(HW, D), full(HW, D), full(HW, D)],
        out_specs=[t1k, t1k, t1k, t512, t512, t512, t512, full(HW, D), full(HW, D), full(HW, D), full(1, DH)],
        out_shape=[s1k, s1k, s1k, jax.ShapeDtypeStruct((S, HW), F32), s512, s512, s512, wsh, wsh, wsh,
                   jax.ShapeDtypeStruct((1, DH), F32)],
        compiler_params=_cp(("arbitrary",)),
    )(dmix, proj, proj, proj, proj, ogdn, osb, omem, ggdn, wbg, wbs, wbm)


def _norm_cast(name, x, g):
    rows = x.shape[0]
    ts = min(_row_tile(rows), rows)

    def body(x_ref, g_ref, o_ref):
        o_ref[...] = _rms(x_ref[...], g_ref[...])[0].astype(BF16)

    t1k = pl.BlockSpec((ts, D), lambda i: (i, 0))
    return pl.pallas_call(
        body, name=name, grid=(rows // ts,), in_specs=[t1k, pl.BlockSpec((1, D), lambda i: (0, 0))], out_specs=t1k,
        out_shape=jax.ShapeDtypeStruct((rows, D), BF16), compiler_params=_cp(("parallel",)),
    )(x, g)


def _norm_bwd(name, dh, x, g, res):
    rows = x.shape[0]
    ts = min(_row_tile(rows), rows)

    def body(*refs):
        dh_ref, x_ref, g_ref = refs[:3]
        dx_o, dg_o = refs[-2:]

        @pl.when(pl.program_id(0) == 0)
        def _():
            dg_o[...] = jnp.zeros_like(dg_o)

        xv = x_ref[...]
        _, r = _rms(xv, g_ref[...])
        dx, dg = _rms_bwd(dh_ref[...], xv, g_ref[...], r)
        dx_o[...] = dx if res is None else dx + refs[3][...]
        dg_o[...] += dg

    t1k = pl.BlockSpec((ts, D), lambda i: (i, 0))
    gsp = pl.BlockSpec((1, D), lambda i: (0, 0))
    ops = [dh, x, g] + ([] if res is None else [res])
    return pl.pallas_call(
        body, name=name, grid=(rows // ts,), in_specs=[t1k, t1k, gsp] + ([] if res is None else [t1k]),
        out_specs=[t1k, gsp],
        out_shape=[jax.ShapeDtypeStruct((rows, D), F32), jax.ShapeDtypeStruct((1, D), F32)],
        compiler_params=_cp(("arbitrary",)),
    )(*ops)


def _slab_tile(rows, lanes):
    cap = min(SLAB_TILE * LANES // lanes, rows)
    return max(d for d in range(16, cap + 1, 16) if rows % d == 0)


def _adamw(name, gall, w, m, v):
    rows, lanes = w.shape
    nsrc = gall.shape[0]
    tr = _slab_tile(rows, lanes)

    def body(g_ref, w_ref, m_ref, v_ref, g_o, d_o, m_o, v_o):
        g = g_ref[0].astype(F32)
        for j in range(1, nsrc):
            g = g + g_ref[j].astype(F32)
        m_new = ADAM_B1 * m_ref[...] + (1.0 - ADAM_B1) * g
        v_new = ADAM_B2 * v_ref[...] + (1.0 - ADAM_B2) * jnp.square(g)
        m_hat = m_new / (1.0 - ADAM_B1 ** ADAM_STEP)
        v_hat = v_new / (1.0 - ADAM_B2 ** ADAM_STEP)
        g_o[...] = g
        d_o[...] = -ADAM_LR * (m_hat / (jnp.sqrt(v_hat) + ADAM_EPS) + ADAM_WD * w_ref[...])
        m_o[...] = m_new
        v_o[...] = v_new

    t = pl.BlockSpec((tr, lanes), lambda i: (i, 0))
    o = jax.ShapeDtypeStruct((rows, lanes), F32)
    return pl.pallas_call(
        body, name=name, grid=(rows // tr,),
        in_specs=[pl.BlockSpec((nsrc, tr, lanes), lambda i: (0, i, 0)), t, t, t],
        out_specs=[t, t, t, t], out_shape=[o, o, o, o],
        compiler_params=_cp(("parallel",)),
    )(gall, w, m, v)


def _pair_sum(name, mine, theirs):
    rows, lanes = mine.shape[1:]
    tr = _slab_tile(rows, lanes)
    core = lax.axis_index("c").astype(jnp.int32).reshape(1)

    def body(c_ref, a_ref, b_ref, o_ref):
        o_ref[...] = (a_ref[...].astype(F32) + b_ref[...].astype(F32)).astype(o_ref.dtype)

    blk = pl.BlockSpec((1, tr, lanes), lambda j, i, c_ref: (j, i, 0))
    return pl.pallas_call(
        body, name=name,
        grid_spec=pltpu.PrefetchScalarGridSpec(
            num_scalar_prefetch=1, grid=(NDEV // 2, rows // tr),
            in_specs=[pl.BlockSpec((1, tr, lanes), lambda j, i, c_ref: (2 * j + c_ref[0], i, 0)), blk],
            out_specs=blk),
        out_shape=jax.ShapeDtypeStruct((NDEV // 2, rows, lanes), mine.dtype),
        compiler_params=_cp(("parallel", "parallel")),
    )(core, mine, theirs)


HBM_SPEC = pl.BlockSpec(memory_space=pltpu.HBM)


def _remote(src, dst, send_sems, recv_sems, k, to):
    return pltpu.make_async_remote_copy(src_ref=src, dst_ref=dst, send_sem=send_sems.at[k], recv_sem=recv_sems.at[k],
                                        device_id=to, device_id_type=pl.DeviceIdType.MESH)


def _gather_steps(x_ref, o_ref, send_sems, recv_sems, local_sem):
    ix, iy, ic = lax.axis_index("x"), lax.axis_index("y"), lax.axis_index("c")
    me, sibling = (ix, iy, ic), (ix, iy, 1 - ic)
    chips = [(1 - ix, iy), (ix, 1 - iy), (1 - ix, 1 - iy)]

    def slab(px, py, pc):
        return o_ref.at[4 * px + 2 * py + pc]

    def copy(k, block, to, src=None):
        return _remote(slab(*block) if src is None else src, slab(*block), send_sems, recv_sems, k, to)

    def mine():
        return pltpu.make_async_copy(x_ref, slab(*me), local_sem)

    def first():
        return [copy(0, me, sibling, src=x_ref)] + [copy(1 + j, me, (*chip, ic), src=x_ref)
                                                    for j, chip in enumerate(chips)]

    def passed():
        return [copy(4 + j, (*chip, ic), sibling) for j, chip in enumerate(chips)]

    def start():
        mine().start()
        for cp in first():
            cp.start()

    def forward():
        for j, (chip, cp) in enumerate(zip(chips, passed())):
            copy(1 + j, (*chip, ic), me).wait_recv()
            cp.start()

    def finish():
        copy(0, sibling, me).wait_recv()
        for j, chip in enumerate(chips):
            copy(4 + j, (*chip, 1 - ic), me).wait_recv()
        for cp in first() + passed():
            cp.wait_send()
        mine().wait()

    return start, forward, finish


GATHER_SEMS = [pltpu.SemaphoreType.DMA((NDEV - 1,)), pltpu.SemaphoreType.DMA((NDEV - 1,)), pltpu.SemaphoreType.DMA]


def _gather(name, x):
    rows, cols = x.shape

    def body(x_ref, o_ref, send_sems, recv_sems, local_sem):
        for step in _gather_steps(x_ref, o_ref, send_sems, recv_sems, local_sem):
            step()

    return pl.pallas_call(
        body, name=name, in_specs=[HBM_SPEC], out_specs=HBM_SPEC,
        out_shape=jax.ShapeDtypeStruct((NDEV, rows, cols), x.dtype), scratch_shapes=list(GATHER_SEMS),
    )(x)


def _sibling_exchange(name, x):
    rows, cols = x.shape[-2:]
    nchip = NDEV // 2

    def body(x_ref, o_ref, send_sems, recv_sems):
        ix, iy, ic = lax.axis_index("x"), lax.axis_index("y"), lax.axis_index("c")
        copies = [_remote(x_ref.at[2 * j + (1 - ic)], o_ref.at[j], send_sems, recv_sems, j, (ix, iy, 1 - ic))
                  for j in range(nchip)]
        for cp in copies:
            cp.start()
        for cp in copies:
            cp.wait()

    return pl.pallas_call(
        body, name=name, in_specs=[HBM_SPEC], out_specs=HBM_SPEC,
        out_shape=jax.ShapeDtypeStruct((nchip, rows, cols), x.dtype),
        scratch_shapes=[pltpu.SemaphoreType.DMA((nchip,)), pltpu.SemaphoreType.DMA((nchip,))],
    )(x)


def _chip_steps(x_ref, o_ref, send_sems, recv_sems, local_sem):
    ix, iy, ic = lax.axis_index("x"), lax.axis_index("y"), lax.axis_index("c")
    my_chip = 2 * ix + iy

    def own():
        return pltpu.make_async_copy(x_ref.at[my_chip], o_ref.at[my_chip], local_sem)

    def copies():
        out = []
        for k in range(1, NDEV // 2):
            px, py = ix ^ (k >> 1), iy ^ (k & 1)
            out.append(_remote(x_ref.at[2 * px + py], o_ref.at[my_chip], send_sems, recv_sems, k - 1, (px, py, ic)))
        return out

    def start():
        own().start()
        for cp in copies():
            cp.start()

    def finish():
        for cp in copies():
            cp.wait()
        own().wait()

    return start, (lambda: None), finish


CHIP_SEMS = [pltpu.SemaphoreType.DMA((NDEV // 2 - 1,)), pltpu.SemaphoreType.DMA((NDEV // 2 - 1,)),
             pltpu.SemaphoreType.DMA]


COL_SHARDED = {"w_in": (D, D_IN), "w_br_gdn": (HW, D), "w_br_sb": (HW, D), "w_br_mem": (HW, D), "w_up": (D, DFF),
               "conv_w": (4, 3 * HW)}
ROW_SHARDED = {"w_mem_kv": (D, 2 * HW), "w_o": (D, D), "w_down": (DFF, D)}


def _to_slab(p):
    return p.reshape(p.shape[:-2] + (-1, LANES))


def _from_slab(flat, r, c):
    return flat.reshape(flat.shape[:-2] + (r, c))


def _shard_dims(name):
    if name in COL_SHARDED:
        r, c = COL_SHARDED[name]
        return r, c // NDEV
    r, c = ROW_SHARDED[name]
    return r // NDEV, c


def _pack_rows(parts, total):
    flat = jnp.concatenate(parts, axis=-2)
    return jnp.pad(flat, [(0, 0)] * (flat.ndim - 2) + [(0, total - flat.shape[-2]), (0, 0)])


def _pack_shards(vals, names, total):
    return _pack_rows([_to_slab(vals[n][0]) for n in names], total)


def _pack_full_grads(grads, names, total):
    parts = []
    for name in names:
        g = grads[name]
        r, c = _shard_dims(name)
        if name in COL_SHARDED:
            g = g.reshape(r, NDEV, c).transpose(1, 0, 2)
        else:
            g = g.reshape(NDEV, r, c)
        parts.append(_to_slab(g))
    return _pack_rows(parts, total)


def _unpack_gathered(slabs, names):
    out, pos = {}, 0
    for name in names:
        rows = SLAB_ROWS[name]
        r, c = _shard_dims(name)
        g = _from_slab(slabs[:, pos:pos + rows], r, c)
        pos += rows
        if name in COL_SHARDED:
            out[name] = g.transpose(1, 0, 2).reshape(r, NDEV * c)
        else:
            out[name] = g.reshape(NDEV * r, c)
    return out


def _unpack_shard(flat, names, shapes):
    out, pos = {}, 0
    for name in names:
        rows = SLAB_ROWS[name]
        r, c = _shard_dims(name)
        out[name] = _from_slab(flat[pos:pos + rows], r, c).reshape(shapes[name])
        pos += rows
    return out


def _first_slab(w_in, conv):
    lead = [(0, 0)] * (w_in.ndim - 2)
    taps = conv.reshape(conv.shape[:-2] + (1, -1))
    parts = [jnp.pad(p, lead + [(0, 0), (0, FIRST_LANES - p.shape[-1])]) for p in (w_in, taps)]
    return _pack_rows(parts, R_FIRST)


def _first_unslab(flat):
    cols = 3 * HW // NDEV
    taps = flat[..., D, :4 * cols]
    return flat[..., :D, :D_IN // NDEV], taps.reshape(taps.shape[:-1] + (4, cols))


def _pack_vec(vals):
    row = jnp.concatenate([vals[n] for n in VEC], axis=1)
    return jnp.pad(row, ((0, 0), (0, VEC_WIDTH - row.shape[1])))


def _adamw_vec(gall, w, m, v):
    aligned = [(off, n) for off, n in zip(VEC_OFFSETS, VEC_SIZES) if n % LANES == 0]

    def body(g_ref, w_ref, m_ref, v_ref, *outs):
        g = g_ref[0]
        for j in range(1, NDEV):
            g = g + g_ref[j]
        m_new = ADAM_B1 * m_ref[...] + (1.0 - ADAM_B1) * g
        v_new = ADAM_B2 * v_ref[...] + (1.0 - ADAM_B2) * jnp.square(g)
        m_hat = m_new / (1.0 - ADAM_B1 ** ADAM_STEP)
        v_hat = v_new / (1.0 - ADAM_B2 ** ADAM_STEP)
        delta = -ADAM_LR * (m_hat / (jnp.sqrt(v_hat) + ADAM_EPS) + ADAM_WD * w_ref[...])
        for r, val in enumerate((g, delta, m_new, v_new)):
            outs[r][...] = val
            for i, (off, n) in enumerate(aligned):
                outs[4 + r * len(aligned) + i][...] = val[:, off:off + n]

    full = lambda *shape: pl.BlockSpec(shape, lambda: (0,) * len(shape))
    row = jax.ShapeDtypeStruct((1, VEC_WIDTH), F32)
    out_shape = [row] * 4 + [jax.ShapeDtypeStruct((1, n), F32) for _ in range(4) for _, n in aligned]
    out_specs = [full(1, VEC_WIDTH)] * 4 + [full(1, n) for _ in range(4) for _, n in aligned]
    return pl.pallas_call(
        body, name="adamw_replicated",
        in_specs=[full(NDEV, 1, VEC_WIDTH), full(1, VEC_WIDTH), full(1, VEC_WIDTH), full(1, VEC_WIDTH)],
        out_specs=out_specs, out_shape=out_shape,
    )(gall, w, m, v)


def _unpack_vec(outs, r):
    aligned = [name for name, n in zip(VEC, VEC_SIZES) if n % LANES == 0]
    vals = {name: outs[4 + r * len(aligned) + i] for i, name in enumerate(aligned)}
    for name, off, n in zip(VEC, VEC_OFFSETS, VEC_SIZES):
        if name not in vals:
            vals[name] = outs[r][:, off:off + n]
    return vals


def _pad_w_in(w):
    return jnp.concatenate([w[:, :2048], w[:, 2056:], w[:, 2048:2056], jnp.zeros((D, D_INP - D_IN), w.dtype)], axis=1)


def _unpad_w_in(w):
    return jnp.concatenate([w[:, :2048], w[:, 7168:7176], w[:, 2048:7168]], axis=1)


def _per_head(v):
    return jnp.repeat(v.reshape(NH), DH).reshape(1, HW)


def _local_step(x, mem, target, w, sm, rest_shards):
    S = x.shape[0]
    ts = _row_tile(S)
    tb = 2 * ts
    alog_f, dtb_f = _per_head(sm["a_log"]), _per_head(sm["dt_bias"])
    w = dict(w)

    h1 = _norm_cast("norm1", x, sm["norm1_g"])
    proj, rest = _mm("in_proj", h1, w["w_in"], "nn", 2 * tb, 1536, D, n_outer=True, comm=("gather", rest_shards))
    w.update(_unpack_gathered(rest[:, :sum(SLAB_ROWS[n] for n in REST)], REST))
    gq, gk, gv, gf, bf, sqn, skn, svb, qmn = _pre_fwd(proj, w["conv_w"], alog_f, dtb_f, sm["sb_q_norm_g"],
                                                      sm["sb_k_norm_g"], sm["mem_q_norm_g"], S)
    ogdn, states = _gdn_fwd(gq, gk, gv, gf, bf, S)
    osb, sb_tot, sb_walked = _sb_fwd(sqn, skn, svb, S)
    kv = _mm("mem_kv", mem, w["w_mem_kv"], "nn", NMEM, D, D, pro="rms", pro_g=sm["mem_norm_g"])
    omem = _mem_fwd(qmn, kv, sm["mem_k_norm_g"], S)
    x1, mix = _merge_fwd(x, proj, ogdn, osb, omem, sm["gdn_norm_g"], w["w_br_gdn"], w["w_br_sb"], w["w_br_mem"],
                         w["w_o"], S)
    h2 = _norm_cast("norm2", x1, sm["norm2_g"])
    up = _mm("mlp_up", h2, w["w_up"], "nn", tb, 2048, D, n_outer=True)
    dy, loss = _mm("mlp_down", up, w["w_down"], "nn", tb, D, 1024, pro="relu2", epi="loss", epi_x=(x1, target))

    g = {}
    dup = _mm("d_up", dy, w["w_down"], "nt", tb, 1024, D, epi="drelu2", epi_x=up, out_dtype=BF16)
    g["w_down"] = _mm("dw_down", up, dy, "tn", 1024, D, 1024, pro="relu2")
    g["w_up"] = _mm("dw_up", h2, dup, "tn", D, 1024, 2048)
    dx1, g["norm2_g"] = _mm("d_h2", dup, w["w_up"], "nt", tb, D, 1024, epi="rms_bwd", epi_x=(x1, sm["norm2_g"], dy))

    dmix = _mm("d_mix", dx1, w["w_o"], "nt", tb, D, D)
    g["w_o"] = _mm("dw_o", mix, dx1, "tn", D, D, 1024)
    (dgl0, dgl1, dgl2, dogdn, dz, dosb, domem, g["w_br_gdn"], g["w_br_sb"], g["w_br_mem"],
     g["gdn_norm_g"]) = _merge_bwd(dmix, proj, ogdn, osb, omem, sm["gdn_norm_g"], w["w_br_gdn"], w["w_br_sb"],
                                   w["w_br_mem"], S)
    dmq, dkv, g["mem_q_norm_g"], g["mem_k_norm_g"] = _mem_bwd(proj, qmn, kv, sm["mem_q_norm_g"], sm["mem_k_norm_g"],
                                                             domem, S)
    g["w_mem_kv"] = _mm("dw_mem_kv", mem, dkv, "tn", D, D, NMEM, pro="rms", pro_g=sm["mem_norm_g"])
    dmn = _mm("d_mem_n", dkv, w["w_mem_kv"], "nt", NMEM, D, D)
    _, g["mem_norm_g"] = _norm_bwd("mem_norm_bwd", dmn, mem, sm["mem_norm_g"], None)
    dsqn, dskn, dsv = _sb_bwd(sqn, skn, svb, dosb, sb_tot, sb_walked, S)
    dgq, dgk, dgv, dgf, dbf = _gdn_bwd(gq, gk, gv, gf, bf, states, dogdn, S)
    dc, dab, dsq, dsk, g["conv_w"], dal_f, ddt_f, g["sb_q_norm_g"], g["sb_k_norm_g"] = _pre_bwd(
        proj, w["conv_w"], alog_f, dtb_f, sm["sb_q_norm_g"], sm["sb_k_norm_g"], dgq, dgk, dgv, dgf, dbf, dsqn, dskn, S)
    g["a_log"] = dal_f.reshape(NH, DH)[:, 0].reshape(1, NH)
    g["dt_bias"] = ddt_f.reshape(NH, DH)[:, 0].reshape(1, NH)
    dqkv = _conv_bwd(dc, w["conv_w"], S)

    dproj = jnp.concatenate([dqkv, dz, dsq, dsk, dsv, dmq, dgl0, dgl1, dgl2, dab], axis=1)
    rest_mine = _pack_full_grads(g, REST, R_REST).astype(BF16)
    rest_pair = _pair_sum("pair_sum_rest", rest_mine, _sibling_exchange("scatter_sibling_rest", rest_mine))
    g["w_in"], rest_all = _mm("dw_in", h1, dproj, "tn", D, 1536, 2048, comm=("chips", rest_pair))
    g["w_in"] = _unpad_w_in(g["w_in"])
    by_owner = lambda grad, r, c: grad.reshape(r, NDEV, c // NDEV).transpose(1, 0, 2)
    first_mine = _first_slab(by_owner(g["w_in"], D, D_IN), by_owner(g["conv_w"], 4, 3 * HW)).astype(BF16)
    first_pair = _pair_sum("pair_sum_first", first_mine, _sibling_exchange("scatter_sibling_first", first_mine))
    dx, g["norm1_g"], first_all = _mm("d_h", dproj, w["w_in"], "nt", tb, D, 1536, epi="rms_bwd",
                                      epi_x=(x, sm["norm1_g"], dx1), comm=("chips", first_pair))
    return loss[0, 0], dx, g, rest_all, first_all


def kernel(x, mem, norm1_g, w_in, conv_w, a_log, dt_bias, gdn_norm_g, sb_q_norm_g, sb_k_norm_g, mem_norm_g, w_mem_kv, mem_q_norm_g, mem_k_norm_g, w_br_gdn, w_br_sb, w_br_mem, w_o, norm2_g, w_up, w_down, loss_target, m_norm1_g, m_w_in, m_conv_w, m_a_log, m_dt_bias, m_gdn_norm_g, m_sb_q_norm_g, m_sb_k_norm_g, m_mem_norm_g, m_w_mem_kv, m_mem_q_norm_g, m_mem_k_norm_g, m_w_br_gdn, m_w_br_sb, m_w_br_mem, m_w_o, m_norm2_g, m_w_up, m_w_down, v_norm1_g, v_w_in, v_conv_w, v_a_log, v_dt_bias, v_gdn_norm_g, v_sb_q_norm_g, v_sb_k_norm_g, v_mem_norm_g, v_w_mem_kv, v_mem_q_norm_g, v_mem_k_norm_g, v_w_br_gdn, v_w_br_sb, v_w_br_mem, v_w_o, v_norm2_g, v_w_up, v_w_down):
    given = dict(norm1_g=norm1_g, w_in=w_in, conv_w=conv_w, a_log=a_log, dt_bias=dt_bias, gdn_norm_g=gdn_norm_g,
                 sb_q_norm_g=sb_q_norm_g, sb_k_norm_g=sb_k_norm_g, mem_norm_g=mem_norm_g, w_mem_kv=w_mem_kv,
                 mem_q_norm_g=mem_q_norm_g, mem_k_norm_g=mem_k_norm_g, w_br_gdn=w_br_gdn, w_br_sb=w_br_sb,
                 w_br_mem=w_br_mem, w_o=w_o, norm2_g=norm2_g, w_up=w_up, w_down=w_down)
    mom1 = dict(norm1_g=m_norm1_g, w_in=m_w_in, conv_w=m_conv_w, a_log=m_a_log, dt_bias=m_dt_bias,
                gdn_norm_g=m_gdn_norm_g, sb_q_norm_g=m_sb_q_norm_g, sb_k_norm_g=m_sb_k_norm_g,
                mem_norm_g=m_mem_norm_g, w_mem_kv=m_w_mem_kv, mem_q_norm_g=m_mem_q_norm_g,
                mem_k_norm_g=m_mem_k_norm_g, w_br_gdn=m_w_br_gdn, w_br_sb=m_w_br_sb, w_br_mem=m_w_br_mem, w_o=m_w_o,
                norm2_g=m_norm2_g, w_up=m_w_up, w_down=m_w_down)
    mom2 = dict(norm1_g=v_norm1_g, w_in=v_w_in, conv_w=v_conv_w, a_log=v_a_log, dt_bias=v_dt_bias,
                gdn_norm_g=v_gdn_norm_g, sb_q_norm_g=v_sb_q_norm_g, sb_k_norm_g=v_sb_k_norm_g,
                mem_norm_g=v_mem_norm_g, w_mem_kv=v_w_mem_kv, mem_q_norm_g=v_mem_q_norm_g,
                mem_k_norm_g=v_mem_k_norm_g, w_br_gdn=v_w_br_gdn, w_br_sb=v_w_br_sb, w_br_mem=v_w_br_mem, w_o=v_w_o,
                norm2_g=v_norm2_g, w_up=v_w_up, w_down=v_w_down)
    shapes = {n: given[n].shape for n in WEIGHTS}

    first_loc = _first_slab(given["w_in"][0], given["conv_w"][0])
    rest_loc = _pack_shards(given, REST, R_REST)
    gathered = _gather("gather_first", first_loc.astype(BF16))
    w = {"w_in": _pad_w_in(_first_unslab(gathered)[0].transpose(1, 0, 2).reshape(D, D_IN))}
    conv_loc = jnp.pad(given["conv_w"][0].reshape(-1, LANES), ((0, 2), (0, 0)))
    conv_all = _gather("gather_conv", conv_loc)
    w["conv_w"] = conv_all[:, :6].reshape(NDEV, 4, 3 * HW // NDEV).transpose(1, 0, 2).reshape(4, 3 * HW)
    sm = {n: given[n] for n in SMALL}

    loss, dx, g, rest_all, first_all = _local_step(x[0], mem[0], loss_target[0], w, sm, rest_loc.astype(BF16))
    res_first = _adamw("adamw_first", first_all, first_loc, _first_slab(mom1["w_in"][0], mom1["conv_w"][0]),
                       _first_slab(mom2["w_in"][0], mom2["conv_w"][0]))
    res_rest = _adamw("adamw_rest", rest_all, rest_loc, _pack_shards(mom1, REST, R_REST),
                      _pack_shards(mom2, REST, R_REST))
    gs_all = _gather("gather_small_grads", _pack_vec(g))
    vec_outs = _adamw_vec(gs_all, _pack_vec(given), _pack_vec(mom1), _pack_vec(mom2))

    outs = {}
    for r, prefix in enumerate(("grad_", "delta_", "new_m_", "new_v_")):
        vals = {n: v.reshape(shapes[n]) for n, v in zip(FIRST, _first_unslab(res_first[r]))}
        vals.update(_unpack_shard(res_rest[r], REST, shapes))
        vals.update(_unpack_vec(vec_outs, r))
        for n in WEIGHTS:
            outs[prefix + n] = vals[n]
    loss = lax.psum(loss, ("x", "y", "c"))
    return (loss, dx[None], *[outs[p + n] for p in ("grad_", "delta_", "new_m_", "new_v_") for n in WEIGHTS])
```

```python
import jax
import jax.numpy as jnp
from jax import lax
from jax.experimental import pallas as pl
from jax.experimental.pallas import tpu as pltpu

F32 = jnp.float32
BF16 = jnp.bfloat16

D = 1024
NH = 4
DH = 128
HW = NH * DH
DFF = 4 * D
NMEM = 256
EPS = 1e-6
NDEV = 8
LANES = 128
PAIR = 128
CHUNK = 64
D_IN = 7176
D_INP = 7680
VMEM_LIMIT = 56 * 1024 * 1024

ADAM_LR, ADAM_B1, ADAM_B2, ADAM_EPS, ADAM_WD, ADAM_STEP = 0.001, 0.9, 0.999, 1e-08, 0.01, 10

CB_Z, CB_SQ, CB_SK, CB_SV, CB_MQ, CB_AB = 3, 4, 5, 6, 7, 14

NN = (((1,), (0,)), ((), ()))
NT = (((1,), (1,)), ((), ()))
TN = (((0,), (0,)), ((), ()))

BIG = ("w_in", "w_mem_kv", "w_br_gdn", "w_br_sb", "w_br_mem", "w_o", "w_up", "w_down", "conv_w")
BIG_ROWS = (7176, 1024, 512, 512, 512, 1024, 4096, 4096, 6)
SLAB_ROWS = dict(zip(BIG, BIG_ROWS))
SLAB_TILE = 1216
FIRST = ("w_in", "conv_w")
REST = ("w_mem_kv", "w_br_gdn", "w_br_sb", "w_br_mem", "w_o", "w_up", "w_down")
R_REST = 10 * SLAB_TILE
FIRST_LANES = 1024
R_FIRST = 1040
SMALL = ("norm1_g", "a_log", "dt_bias", "gdn_norm_g", "sb_q_norm_g", "sb_k_norm_g", "mem_norm_g",
         "mem_q_norm_g", "mem_k_norm_g", "norm2_g")
VEC = ("norm1_g", "mem_norm_g", "norm2_g", "gdn_norm_g", "sb_q_norm_g", "sb_k_norm_g", "mem_q_norm_g", "mem_k_norm_g",
       "a_log", "dt_bias")
VEC_SIZES = (1024, 1024, 1024, 128, 128, 128, 128, 128, 4, 4)
VEC_OFFSETS = (0, 1024, 2048, 3072, 3200, 3328, 3456, 3584, 3712, 3716)
VEC_WIDTH = 3840
WEIGHTS = ("norm1_g", "w_in", "conv_w", "a_log", "dt_bias", "gdn_norm_g", "sb_q_norm_g", "sb_k_norm_g",
           "mem_norm_g", "w_mem_kv", "mem_q_norm_g", "mem_k_norm_g", "w_br_gdn", "w_br_sb", "w_br_mem",
           "w_o", "norm2_g", "w_up", "w_down")


def _cp(sem=None):
    return pltpu.CompilerParams(dimension_semantics=sem, vmem_limit_bytes=VMEM_LIMIT)


def _dot(a, b, dims=NN):
    return lax.dot_general(a, b, dims, preferred_element_type=F32)


def _dbf(a, b, dims=NN):
    return _dot(a.astype(BF16), b.astype(BF16), dims)


def _split(a, n):
    parts = []
    for _ in range(n):
        h = a.astype(BF16)
        parts.append(h)
        a = a - h.astype(F32)
    return parts


def _dg(a, b, dims=NN):
    return _dbf(a, b, dims)


def _dxr(a, e, dims=NN):
    eb = e.astype(BF16)
    a1, a2, a3 = _split(a, 3)
    return _dot(a1, eb, dims) + (_dot(a2, eb, dims) + _dot(a3, eb, dims))


def _dxl(e, a, dims=NN):
    eb = e.astype(BF16)
    a1, a2, a3 = _split(a, 3)
    return _dot(eb, a1, dims) + (_dot(eb, a2, dims) + _dot(eb, a3, dims))


def _sigmoid(x):
    return 1.0 / (1.0 + jnp.exp(-x))


def _softplus(x):
    return jnp.maximum(x, 0.0) + jnp.log(1.0 + jnp.exp(-jnp.abs(x)))


def _rms(x, g):
    r = lax.rsqrt(jnp.mean(x * x, axis=-1, keepdims=True) + EPS)
    return x * r * g, r


def _rms_bwd(dy, x, g, r):
    dyg = dy * g
    dx = r * (dyg - x * (r * r) * jnp.mean(dyg * x, axis=-1, keepdims=True))
    dg = jnp.sum(dy * (x * r), axis=0, keepdims=True)
    return dx, dg


def _hs(h):
    return slice(h * DH, (h + 1) * DH)


def _row_tile(s):
    return 512 if s >= 2048 else 256


def _narrow_tile(s):
    return min(256, s)


def _mm(name, a, b, mode, tm, tn, tk, pro=None, pro_g=None, epi=None, epi_x=None, out_dtype=F32, n_outer=False,
        comm=None):
    if mode == "tn":
        K, M = a.shape
    else:
        M, K = a.shape
    N = b.shape[0] if mode == "nt" else b.shape[1]
    tm, tn, tk = min(tm, M), min(tn, N), min(tk, K)
    nk = K // tk
    assert M % tm == 0 and N % tn == 0 and K % tk == 0, (name, M, N, K, tm, tn, tk)
    dims = {"nn": NN, "nt": NT, "tn": TN}[mode]
    reducing = epi in ("rms_bwd", "loss")
    assert not reducing or (tn == N and not n_outer), name
    epi_ops = () if epi is None else (epi_x if isinstance(epi_x, tuple) else (epi_x,))

    def body(*refs):
        a_ref, b_ref = refs[0], refs[1]
        pos = 2
        g_ref = None
        if pro == "rms":
            g_ref = refs[pos]
            pos += 1
        e_refs = refs[pos:pos + len(epi_ops)]
        pos += len(epi_ops)
        cx_ref = None
        if comm is not None:
            cx_ref = refs[pos]
            pos += 1
        o_ref = refs[pos]
        pos += 1
        r_ref = None
        if reducing:
            r_ref = refs[pos]
            pos += 1
        if comm is not None:
            steps_of = _gather_steps if comm[0] == "gather" else _chip_steps
            start, forward, finish_comm = steps_of(cx_ref, refs[pos], *refs[-3:])
            pos += 1
            step = (pl.program_id(0) * grid[1] + pl.program_id(1)) * nk + pl.program_id(2)
            total = grid[0] * grid[1] * nk
            pl.when(step == 0)(start)
            pl.when(step == (4 * total) // 5)(forward)
        av = a_ref[...]
        if pro == "rms":
            av, _ = _rms(av.astype(F32), g_ref[...])
        elif pro == "relu2":
            av = jnp.square(jnp.maximum(av, 0.0))
        part = _dbf(av, b_ref[...], dims)
        first = pl.program_id(0) == 0

        def finish(acc):
            red = None
            if epi == "add":
                acc = acc + e_refs[0][...]
            elif epi == "drelu2":
                acc = acc * (2.0 * jnp.maximum(e_refs[0][...], 0.0))
            elif epi == "rms_bwd":
                xv, gv = e_refs[0][...], e_refs[1][...]
                _, r = _rms(xv, gv)
                dx, red = _rms_bwd(acc, xv, gv, r)
                acc = dx + e_refs[2][...]
            elif epi == "loss":
                err = acc + e_refs[0][...] - e_refs[1][...]
                acc = err * (1.0 / N)
                per_tok = jnp.sum(err * err, axis=1, keepdims=True) * (1.0 / N)
                red = 0.5 * jnp.sum(per_tok, axis=0, keepdims=True)
            o_ref[...] = acc.astype(out_dtype)
            if reducing:

                @pl.when(first)
                def _():
                    r_ref[...] = red

                @pl.when(jnp.logical_not(first))
                def _():
                    r_ref[...] += red

        if nk == 1:
            finish(part)
        else:
            acc_ref = refs[pos]
            k = pl.program_id(2)

            @pl.when(k == 0)
            def _():
                acc_ref[...] = part

            @pl.when(k > 0)
            def _():
                acc_ref[...] += part

            @pl.when(k == nk - 1)
            def _():
                finish(acc_ref[...])

        if comm is not None:
            pl.when(step == total - 1)(finish_comm)

    def spec(shape, index):
        if n_outer:
            return pl.BlockSpec(shape, lambda j, i, k: index(i, j, k))
        return pl.BlockSpec(shape, index)

    if mode == "tn":
        a_spec = spec((tk, tm), lambda i, j, k: (k, i))
    else:
        a_spec = spec((tm, tk), lambda i, j, k: (i, k))
    if mode == "nt":
        b_spec = spec((tn, tk), lambda i, j, k: (j, k))
    else:
        b_spec = spec((tk, tn), lambda i, j, k: (k, j))
    in_specs, ops = [a_spec, b_spec], [a, b]
    if pro == "rms":
        w = pro_g.shape[1]
        assert (tm if mode == "tn" else tk) == w, name
        in_specs.append(spec((1, w), lambda i, j, k: (0, 0)))
        ops.append(pro_g)
    for op in epi_ops:
        if op.shape[0] == 1:
            in_specs.append(spec((1, tn), lambda i, j, k: (0, j)))
        else:
            in_specs.append(spec((tm, tn), lambda i, j, k: (i, j)))
        ops.append(op)
    out_specs = [spec((tm, tn), lambda i, j, k: (i, j))]
    out_shape = [jax.ShapeDtypeStruct((M, N), out_dtype)]
    if reducing:
        width = N if epi == "rms_bwd" else 1
        out_specs.append(spec((1, width), lambda i, j, k: (0, 0)))
        out_shape.append(jax.ShapeDtypeStruct((1, width), F32))
    scratch = [pltpu.VMEM((tm, tn), F32)] if nk > 1 else []
    if comm is not None:
        kind, cx = comm
        in_specs.append(HBM_SPEC)
        ops.append(cx)
        out_specs.append(HBM_SPEC)
        out_shape.append(jax.ShapeDtypeStruct((NDEV if kind == "gather" else NDEV // 2,) + cx.shape[-2:], cx.dtype))
        scratch += list(GATHER_SEMS if kind == "gather" else CHIP_SEMS)
    grid = (N // tn, M // tm, nk) if n_outer else (M // tm, N // tn, nk)
    ordered = reducing or comm is not None
    outs = pl.pallas_call(
        body, name=name, grid=grid,
        in_specs=in_specs, out_specs=out_specs, out_shape=out_shape, scratch_shapes=scratch,
        compiler_params=_cp(("arbitrary" if ordered else "parallel", "arbitrary" if comm is not None else "parallel",
                             "arbitrary")),
    )(*ops)
    return outs if len(out_shape) > 1 else outs[0]


def _head_select(first_lane):
    l = lax.broadcasted_iota(jnp.int32, (LANES, HW), 0)
    c = lax.broadcasted_iota(jnp.int32, (LANES, HW), 1)
    return (l == first_lane + c // DH).astype(F32)


def _conv_taps(buf, cw, ts):
    c = cw[3:4, :] * buf[8:8 + ts, :]
    for j in range(3):
        k = 3 - j
        c = c + cw[j:j + 1, :] * buf[8 - k:8 - k + ts, :]
    return c


def _pre_fwd(proj, conv_w, alog_f, dtb_f, gsq, gsk, gmq, S):
    ts = _row_tile(S)
    hb = ts // 8

    def body(qkv_ref, halo_ref, ab_ref, sq_ref, sk_ref, sv_ref, mq_ref, cw_ref, al_ref, dt_ref, gsq_ref, gsk_ref,
             gmq_ref, gq_o, gk_o, gv_o, gf_o, bf_o, sqn_o, skn_o, svb_o, qmn_o, buf):
        i = pl.program_id(0)
        buf[0:8, :] = jnp.where(i == 0, 0.0, halo_ref[...])
        buf[8:8 + ts, :] = qkv_ref[...]
        c = _conv_taps(buf, cw_ref[...], ts)
        a = c * _sigmoid(c)
        for h in range(NH):
            q = a[:, h * DH:(h + 1) * DH]
            k = a[:, HW + h * DH:HW + (h + 1) * DH]
            gq_o[:, _hs(h)] = q * (lax.rsqrt(jnp.sum(q * q, axis=-1, keepdims=True) + EPS) * DH ** -0.5)
            gk_o[:, _hs(h)] = k * lax.rsqrt(jnp.sum(k * k, axis=-1, keepdims=True) + EPS)
            sqn_o[:, _hs(h)] = _rms(sq_ref[:, _hs(h)], gsq_ref[...])[0].astype(BF16)
            skn_o[:, _hs(h)] = _rms(sk_ref[:, _hs(h)], gsk_ref[...])[0].astype(BF16)
            qmn_o[:, _hs(h)] = _rms(mq_ref[:, _hs(h)], gmq_ref[...])[0].astype(BF16)
        gv_o[...] = a[:, 2 * HW:3 * HW]
        svb_o[...] = sv_ref[...].astype(BF16)
        ab = ab_ref[:, 0:LANES]
        a_bc = _dxr(ab, _head_select(0))
        b_bc = _dxr(ab, _head_select(NH))
        gf_o[...] = -jnp.exp(al_ref[...]) * _softplus(a_bc + dt_ref[...])
        bf_o[...] = _sigmoid(b_bc)

    row = lambda cb: pl.BlockSpec((ts, HW), lambda i: (i, cb))
    full = lambda r, c: pl.BlockSpec((r, c), lambda i: (0, 0))
    f32o = jax.ShapeDtypeStruct((S, HW), F32)
    bfo = jax.ShapeDtypeStruct((S, HW), BF16)
    return pl.pallas_call(
        body, name="pre_fwd", grid=(S // ts,),
        in_specs=[pl.BlockSpec((ts, 3 * HW), lambda i: (i, 0)),
                  pl.BlockSpec((8, 3 * HW), lambda i: (jnp.maximum(i * hb - 1, 0), 0)),
                  row(CB_AB), row(CB_SQ), row(CB_SK), row(CB_SV), row(CB_MQ),
                  full(4, 3 * HW), full(1, HW), full(1, HW), full(1, DH), full(1, DH), full(1, DH)],
        out_specs=[pl.BlockSpec((ts, HW), lambda i: (i, 0))] * 9,
        out_shape=[f32o, f32o, f32o, f32o, f32o, bfo, bfo, bfo, bfo],
        scratch_shapes=[pltpu.VMEM((ts + 8, 3 * HW), F32)],
        compiler_params=_cp(("parallel",)),
    )(proj, proj, proj, proj, proj, proj, proj, conv_w, alog_f, dtb_f, gsq, gsk, gmq)


def _pre_bwd(proj, conv_w, alog_f, dtb_f, gsq, gsk, dgq, dgk, dgv, dgf, dbf, dsqn, dskn, S):
    ts = _narrow_tile(S)
    hb = ts // 8

    def body(qkv_ref, halo_ref, ab_ref, sq_ref, sk_ref, cw_ref, al_ref, dt_ref, gsq_ref, gsk_ref,
             dgq_ref, dgk_ref, dgv_ref, dgf_ref, dbf_ref, dsqn_ref, dskn_ref,
             dc_o, dab_o, dsq_o, dsk_o, dcw_o, dal_o, ddt_o, dgsq_o, dgsk_o, buf):
        i = pl.program_id(0)

        @pl.when(i == 0)
        def _():
            dcw_o[...] = jnp.zeros_like(dcw_o)
            dal_o[...] = jnp.zeros_like(dal_o)
            ddt_o[...] = jnp.zeros_like(ddt_o)
            dgsq_o[...] = jnp.zeros_like(dgsq_o)
            dgsk_o[...] = jnp.zeros_like(dgsk_o)

        buf[0:8, :] = jnp.where(i == 0, 0.0, halo_ref[...])
        buf[8:8 + ts, :] = qkv_ref[...]
        c = _conv_taps(buf, cw_ref[...], ts)
        sg = _sigmoid(c)
        a = c * sg
        dsilu = sg * (1.0 + c * (1.0 - sg))
        dgsq = jnp.zeros((1, DH), F32)
        dgsk = jnp.zeros((1, DH), F32)
        for h in range(NH):
            q = a[:, h * DH:(h + 1) * DH]
            k = a[:, HW + h * DH:HW + (h + 1) * DH]
            nq = lax.rsqrt(jnp.sum(q * q, axis=-1, keepdims=True) + EPS)
            nk = lax.rsqrt(jnp.sum(k * k, axis=-1, keepdims=True) + EPS)
            dyq = dgq_ref[:, _hs(h)]
            dyk = dgk_ref[:, _hs(h)]
            dq = (nq * dyq - q * (nq * nq * nq) * jnp.sum(dyq * q, axis=-1, keepdims=True)) * DH ** -0.5
            dk = nk * dyk - k * (nk * nk * nk) * jnp.sum(dyk * k, axis=-1, keepdims=True)
            dc_o[:, h * DH:(h + 1) * DH] = dq * dsilu[:, h * DH:(h + 1) * DH]
            dc_o[:, HW + h * DH:HW + (h + 1) * DH] = dk * dsilu[:, HW + h * DH:HW + (h + 1) * DH]
            x = sq_ref[:, _hs(h)]
            _, r = _rms(x, gsq_ref[...])
            dx, dg = _rms_bwd(dsqn_ref[:, _hs(h)], x, gsq_ref[...], r)
            dsq_o[:, _hs(h)] = dx.astype(BF16)
            dgsq = dgsq + dg
            x = sk_ref[:, _hs(h)]
            _, r = _rms(x, gsk_ref[...])
            dx, dg = _rms_bwd(dskn_ref[:, _hs(h)], x, gsk_ref[...], r)
            dsk_o[:, _hs(h)] = dx.astype(BF16)
            dgsk = dgsk + dg
        dc_o[:, 2 * HW:3 * HW] = dgv_ref[...] * dsilu[:, 2 * HW:3 * HW]
        dgsq_o[...] += dgsq
        dgsk_o[...] += dgsk
        dc = dc_o[...]
        for j in range(4):
            k = 3 - j
            dcw_o[j:j + 1, :] += jnp.sum(dc * buf[8 - k:8 - k + ts, :], axis=0, keepdims=True)
        ab = ab_ref[:, 0:LANES]
        a_bc = _dxr(ab, _head_select(0))
        b_bc = _dxr(ab, _head_select(NH))
        pre = a_bc + dt_ref[...]
        ea = jnp.exp(al_ref[...])
        dgf = dgf_ref[...]
        dal_o[...] += jnp.sum(dgf * (-ea * _softplus(pre)), axis=0, keepdims=True)
        da = dgf * (-ea * _sigmoid(pre))
        ddt_o[...] += jnp.sum(da, axis=0, keepdims=True)
        beta = _sigmoid(b_bc)
        db = dbf_ref[...] * beta * (1.0 - beta)
        lane = lax.broadcasted_iota(jnp.int32, (ts, LANES), 1)
        dab = jnp.zeros((ts, LANES), F32)
        for h in range(NH):
            dab = dab + jnp.where(lane == h, da[:, _hs(h)], 0.0) + jnp.where(lane == NH + h, db[:, _hs(h)], 0.0)
        dab_o[:, 0:LANES] = dab.astype(BF16)
        dab_o[:, LANES:HW] = jnp.zeros((ts, HW - LANES), BF16)

    row = lambda cb: pl.BlockSpec((ts, HW), lambda i: (i, cb))
    full = lambda r, c: pl.BlockSpec((r, c), lambda i: (0, 0))
    t512 = pl.BlockSpec((ts, HW), lambda i: (i, 0))
    return pl.pallas_call(
        body, name="pre_bwd", grid=(S // ts,),
        in_specs=[pl.BlockSpec((ts, 3 * HW), lambda i: (i, 0)),
                  pl.BlockSpec((8, 3 * HW), lambda i: (jnp.maximum(i * hb - 1, 0), 0)),
                  row(CB_AB), row(CB_SQ), row(CB_SK),
                  full(4, 3 * HW), full(1, HW), full(1, HW), full(1, DH), full(1, DH)] + [t512] * 7,
        out_specs=[pl.BlockSpec((ts, 3 * HW), lambda i: (i, 0)), t512, t512, t512,
                   full(4, 3 * HW), full(1, HW), full(1, HW), full(1, DH), full(1, DH)],
        out_shape=[jax.ShapeDtypeStruct((S, 3 * HW), F32)] + [jax.ShapeDtypeStruct((S, HW), BF16)] * 3
        + [jax.ShapeDtypeStruct((4, 3 * HW), F32), jax.ShapeDtypeStruct((1, HW), F32),
           jax.ShapeDtypeStruct((1, HW), F32), jax.ShapeDtypeStruct((1, DH), F32),
           jax.ShapeDtypeStruct((1, DH), F32)],
        scratch_shapes=[pltpu.VMEM((ts + 8, 3 * HW), F32)],
        compiler_params=_cp(("arbitrary",)),
    )(proj, proj, proj, proj, proj, conv_w, alog_f, dtb_f, gsq, gsk, dgq, dgk, dgv, dgf, dbf, dsqn, dskn)


def _conv_bwd(dc, conv_w, S):
    ts = _row_tile(S)
    hb = ts // 8
    n = S // ts

    def body(dc_ref, halo_ref, cw_ref, o_ref, buf):
        i = pl.program_id(0)
        buf[0:ts, :] = dc_ref[...]
        buf[ts:ts + 8, :] = jnp.where(i == n - 1, 0.0, halo_ref[...])
        cw = cw_ref[...]
        acc = cw[3:4, :] * buf[0:ts, :]
        for k in range(1, 4):
            acc = acc + cw[3 - k:4 - k, :] * buf[k:k + ts, :]
        o_ref[...] = acc.astype(BF16)

    return pl.pallas_call(
        body, name="conv_bwd", grid=(n,),
        in_specs=[pl.BlockSpec((ts, 3 * HW), lambda i: (i, 0)),
                  pl.BlockSpec((8, 3 * HW), lambda i: (jnp.minimum((i + 1) * hb, S // 8 - 1), 0)),
                  pl.BlockSpec((4, 3 * HW), lambda i: (0, 0))],
        out_specs=pl.BlockSpec((ts, 3 * HW), lambda i: (i, 0)),
        out_shape=jax.ShapeDtypeStruct((S, 3 * HW), BF16),
        scratch_shapes=[pltpu.VMEM((ts + 8, 3 * HW), F32)],
        compiler_params=_cp(("parallel",)),
    )(dc, dc, conv_w)


def _gdn_masks():
    r = lax.broadcasted_iota(jnp.int32, (PAIR, PAIR), 0)
    c = lax.broadcasted_iota(jnp.int32, (PAIR, PAIR), 1)
    same = ((r >= CHUNK) & (c >= CHUNK)) | ((r < CHUNK) & (c < CHUNK))
    return dict(r=r, same=same, tril=same & (r >= c), strict=same & (r > c), triu=same & (c >= r), eye=r == c,
                in_a=r < CHUNK, last_a=r == CHUNK - 1, last_b=r == PAIR - 1)


def _each(fn, *cols):
    return [fn(*xs) for xs in zip(*cols)]


def _mul(a, b):
    return a * b


def _top(x):
    return x[:CHUNK]


def _bot(x):
    return x[CHUNK:]


def _rows(a, b):
    return jnp.concatenate([a, b], axis=0)


def _tri_inv(lm, eye):
    eye_f = eye.astype(F32)
    p = _each(lambda l: eye_f - l, lm)
    lp = _each(lambda l: _dg(l, l), lm)
    for it in range(5):
        p = _each(lambda a, b: a + _dg(a, b), p, lp)
        if it < 4:
            lp = _each(lambda b: _dg(b, b), lp)
    return p


def _gdn_block(m, q, k, v, g, beta):
    tril_f = m["tril"].astype(F32)
    col_sum = lambda mask: (lambda x: jnp.sum(jnp.where(mask, x, 0.0), axis=0, keepdims=True))
    gc = _each(lambda x: _dxl(tril_f, x), g)
    gcr = _each(col_sum(m["eye"]), gc)
    gam = _each(lambda a, b: jnp.where(m["tril"], jnp.exp(jnp.minimum(a - b, 0.0)), 0.0), gc, gcr)
    kb = _each(_mul, k, beta)
    vb = _each(_mul, v, beta)
    lm = _each(lambda a, b, c: jnp.where(m["strict"], _dg(a, b, NT) * c, 0.0), kb, k, gam)
    t = _tri_inv(lm, m["eye"])
    eg = _each(jnp.exp, gc)
    kbe = _each(_mul, kb, eg)
    u = _each(_dg, t, vb)
    w = _each(_dg, t, kbe)
    aqk = _each(lambda a, b, c: jnp.where(m["tril"], _dg(a, b, NT) * c, 0.0), q, k, gam)
    qd = _each(_mul, q, eg)
    ga = _each(col_sum(m["last_a"]), gc)
    gb = _each(col_sum(m["last_b"]), gc)
    e2 = _each(lambda a, b, c: jnp.exp(jnp.where(m["in_a"], a, b) - c), ga, gb, gc)
    kd = _each(_mul, k, e2)
    return dict(u=u, w=w, aqk=aqk, qd=qd, kd=kd, gam=gam, kb=kb, vb=vb, lm=lm, t=t, eg=eg, kbe=kbe, e2=e2,
                gla=_each(jnp.exp, ga), glb=_each(jnp.exp, gb))


def _gdn_fwd(gq, gk, gv, gf, bf, S):
    nb = S // PAIR

    def body(q_ref, k_ref, v_ref, g_ref, b_ref, o_ref, st_ref, s_scr):
        @pl.when(pl.program_id(0) == 0)
        def _():
            s_scr[...] = jnp.zeros_like(s_scr)

        m = _gdn_masks()
        heads = lambda ref: [ref[:, _hs(h)] for h in range(NH)]
        f = _gdn_block(m, heads(q_ref), heads(k_ref), heads(v_ref), heads(g_ref), heads(b_ref))
        u, w, qd, kd = f["u"], f["w"], f["qd"], f["kd"]
        s0 = [s_scr[h * DH:(h + 1) * DH, :] for h in range(NH)]
        vna = _each(lambda a, b, s: _top(a) - _dg(_top(b), s), u, w, s0)
        oa = _each(lambda a, s: _dg(_top(a), s), qd, s0)
        s1 = _each(lambda s, gl, a, vn: s * gl + _dg(_top(a), vn, TN), s0, f["gla"], kd, vna)
        vnb = _each(lambda a, b, s: _bot(a) - _dg(_bot(b), s), u, w, s1)
        ob = _each(lambda a, s: _dg(_bot(a), s), qd, s1)
        s2 = _each(lambda s, gl, a, vn: s * gl + _dg(_bot(a), vn, TN), s1, f["glb"], kd, vnb)
        outs = _each(lambda a, b, c, va, vb: _rows(a, b) + _dg(c, _rows(va, vb)), oa, ob, f["aqk"], vna, vnb)
        o_ref[...] = jnp.concatenate(outs, axis=1)
        st_ref[...] = jnp.concatenate(s0 + s1, axis=0)
        s_scr[...] = jnp.concatenate(s2, axis=0)

    blk = pl.BlockSpec((PAIR, HW), lambda i: (i, 0))
    return pl.pallas_call(
        body, name="gdn_fwd", grid=(nb,),
        in_specs=[blk] * 5,
        out_specs=[blk, pl.BlockSpec((2 * NH * DH, DH), lambda i: (i, 0))],
        out_shape=[jax.ShapeDtypeStruct((S, HW), F32), jax.ShapeDtypeStruct((nb * 2 * NH * DH, DH), F32)],
        scratch_shapes=[pltpu.VMEM((NH * DH, DH), F32)],
        compiler_params=_cp(("arbitrary",)),
    )(gq, gk, gv, gf, bf)


def _gdn_bwd(gq, gk, gv, gf, bf, states, do, S):
    nb = S // PAIR

    def body(q_ref, k_ref, v_ref, g_ref, b_ref, st_ref, do_ref, dq_o, dk_o, dv_o, dg_o, db_o, ds_scr):
        @pl.when(pl.program_id(0) == 0)
        def _():
            ds_scr[...] = jnp.zeros_like(ds_scr)

        m = _gdn_masks()
        ones = jnp.ones((PAIR, PAIR), F32)
        heads = lambda ref: [ref[:, _hs(h)] for h in range(NH)]
        q, k, v, beta, do = heads(q_ref), heads(k_ref), heads(v_ref), heads(b_ref), heads(do_ref)
        f = _gdn_block(m, q, k, v, heads(g_ref), beta)
        u, w, aqk, qd, kd, t = f["u"], f["w"], f["aqk"], f["qd"], f["kd"], f["t"]
        s0 = [st_ref[h * DH:(h + 1) * DH, :] for h in range(NH)]
        s1 = [st_ref[(NH + h) * DH:(NH + h + 1) * DH, :] for h in range(NH)]
        ds2 = [ds_scr[h * DH:(h + 1) * DH, :] for h in range(NH)]
        total = lambda a, b: jnp.sum(jnp.sum(a * b, axis=1, keepdims=True), axis=0, keepdims=True)
        vna = _each(lambda a, b, s: _top(a) - _dg(_top(b), s), u, w, s0)
        vnb = _each(lambda a, b, s: _bot(a) - _dg(_bot(b), s), u, w, s1)
        dvn_i = _each(lambda a, b: _dg(a, b, TN), aqk, do)
        dvnb = _each(lambda a, b, s: _bot(a) + _dg(_bot(b), s), dvn_i, kd, ds2)
        dqdb = _each(lambda a, s: _dg(_bot(a), s, NT), do, s1)
        dkdb = _each(lambda a, s: _dg(a, s, NT), vnb, ds2)
        dglb = _each(total, ds2, s1)
        dwb = _each(lambda a, s: -_dg(a, s, NT), dvnb, s1)
        ds1 = _each(lambda s, gl, a, b, c, d: s * gl + _dg(_bot(a), _bot(b), TN) - _dg(_bot(c), d, TN),
                    ds2, f["glb"], qd, do, w, dvnb)
        dvna = _each(lambda a, b, s: _top(a) + _dg(_top(b), s), dvn_i, kd, ds1)
        dqda = _each(lambda a, s: _dg(_top(a), s, NT), do, s0)
        dkda = _each(lambda a, s: _dg(a, s, NT), vna, ds1)
        dgla = _each(total, ds1, s0)
        dwa = _each(lambda a, s: -_dg(a, s, NT), dvna, s0)
        ds0 = _each(lambda s, gl, a, b, c, d: s * gl + _dg(_top(a), _top(b), TN) - _dg(_top(c), d, TN),
                    ds1, f["gla"], qd, do, w, dvna)
        dvn, dqd, dkd, dw = (_each(_rows, a, b) for a, b in ((dvna, dvnb), (dqda, dqdb), (dkda, dkdb), (dwa, dwb)))
        daqk = _each(lambda a, va, vb: jnp.where(m["tril"], _dg(a, _rows(va, vb), NT), 0.0), do, vna, vnb)
        dt = _each(lambda a, b, c, d: _dg(a, b, NT) + _dg(c, d, NT), dvn, f["vb"], dw, f["kbe"])
        dvb = _each(lambda a, b: _dg(a, b, TN), t, dvn)
        dkbe = _each(lambda a, b: _dg(a, b, TN), t, dw)
        dtt = _each(lambda a, b: _dg(a, b, NT), dt, t)
        dl = _each(lambda a, b: -jnp.where(m["strict"], _dg(a, b, TN), 0.0), t, dtt)
        dm = _each(_mul, dl, f["gam"])
        dn = _each(_mul, daqk, f["gam"])
        dkb = _each(lambda a, b, c, d: _dg(a, b) + c * d, dm, k, dkbe, f["eg"])
        dks = _each(lambda a, b, c, d, e, g, h, i: _dg(a, b, TN) + _dg(c, d, TN) + e * g + h * i,
                    dm, f["kb"], dn, q, dkd, f["e2"], beta, dkb)
        dqs = _each(lambda a, b, c, d: _dg(a, b) + c * d, dn, k, dqd, f["eg"])
        gm = _each(lambda a, b, c, d: a * b + c * d, dl, f["lm"], daqk, aqk)
        dkdkd = _each(_mul, dkd, kd)
        dgc = _each(lambda a, b, c, d, e, g: _dxr(a + b * c + d * e - g, ones) - _dxr(a, ones, TN),
                    gm, dqd, qd, dkbe, f["kbe"], dkdkd)
        same_f = m["same"].astype(F32)
        chunk_tot = _each(lambda a: _dxl(same_f, _dxr(a, ones)), dkdkd)
        last = m["last_a"] | m["last_b"]
        dgc = _each(lambda a, b, ga, gla, gb, glb: a + jnp.where(last, b + jnp.where(m["in_a"], ga * gla, gb * glb), 0.0),
                    dgc, chunk_tot, dgla, f["gla"], dglb, f["glb"])
        dbs = _each(lambda a, b, c, d: _dxr(a * b + c * d, ones), dkb, k, dvb, v)
        dvs = _each(_mul, beta, dvb)
        triu_f = m["triu"].astype(F32)
        dgs = _each(lambda a: _dxl(triu_f, a), dgc)
        for ref, parts in ((dq_o, dqs), (dk_o, dks), (dv_o, dvs), (dg_o, dgs), (db_o, dbs)):
            ref[...] = jnp.concatenate(parts, axis=1)
        ds_scr[...] = jnp.concatenate(ds0, axis=0)

    blk = pl.BlockSpec((PAIR, HW), lambda i: (nb - 1 - i, 0))
    o = jax.ShapeDtypeStruct((S, HW), F32)
    return pl.pallas_call(
        body, name="gdn_bwd", grid=(nb,),
        in_specs=[blk] * 5 + [pl.BlockSpec((2 * NH * DH, DH), lambda i: (nb - 1 - i, 0)), blk],
        out_specs=[blk] * 5, out_shape=[o] * 5,
        scratch_shapes=[pltpu.VMEM((NH * DH, DH), F32)],
        compiler_params=_cp(("arbitrary",)),
    )(gq, gk, gv, gf, bf, states, do)


SB_T = 256
SB_GROUP = 4
SB_GROUP_BWD = 4
SB_SINGLES = 1
SB_DEAD = -110.0


def _group_sizes(g):
    sizes = []
    while g >= 1:
        sizes.append(g)
        g //= 2
    return sizes


def _sb_iotas(t):
    return lax.broadcasted_iota(jnp.int32, (t, t), 0), lax.broadcasted_iota(jnp.int32, (t, t), 1)


def _sb_scores(q, k, mask):
    z = _dot(q, k, NT) * DH ** -0.5
    ls = jnp.minimum(z, 0.0) - jnp.log(1.0 + jnp.exp(-jnp.abs(z)))
    lneg = ls - z
    if mask is not None:
        lneg = jnp.where(mask, lneg, 0.0)
    return ls, lneg


def _prefix(x, u):
    xh, xl = _split(x, 2)
    return _dot(xh, u) + _dot(xl, u)


def _sb_fwd(sqn, skn, svb, S):
    t = min(SB_T, S)

    def body(q_ref, k_ref, v_ref, o_ref, t_ref, cnt_ref):
        qb = pl.program_id(1)
        q = q_ref[...]
        r, c = _sb_iotas(t)
        diag = c < r
        u_after = (r > c).astype(BF16)

        def tiles(k0s, run, masks):
            sc = _each(lambda k0, m: _sb_scores(q, k_ref[pl.ds(k0, t), :], m), k0s, masks)
            ls, lneg = [s[0] for s in sc], [s[1] for s in sc]
            sums = _each(lambda x: jnp.sum(x, axis=1, keepdims=True), lneg)
            pre = _each(lambda x: _prefix(x, u_after), lneg)
            runs = [run]
            for s in sums:
                runs.append(runs[-1] + s)
            att = _each(lambda a, b, rn: jnp.exp(a + (rn + b)), ls, pre, runs[:-1])
            att = _each(lambda a, m: a if m is None else jnp.where(m, a, 0.0), att, masks)
            parts = _each(lambda a, k0: _dot(a.astype(BF16), v_ref[pl.ds(k0, t), :]), att, k0s)
            return sum(parts[1:], parts[0]), runs[-1]

        left = jnp.full((t, t), qb > 0)
        acc, run = tiles([pl.multiple_of(qb * t, t), pl.multiple_of(jnp.maximum(qb - 1, 0) * t, t)],
                         jnp.zeros((t, 1), F32), [diag, left])

        def alive(run):
            return jnp.max(run) >= SB_DEAD

        carry, done = (0, acc, run, alive(run)), jnp.minimum(qb, 1)
        for size, limit in [(1, SB_SINGLES)] + [(s, None) for s in _group_sizes(SB_GROUP)]:

            def more(c, size=size, done=done, limit=limit):
                i, _, _, go = c
                fits = done + (i + 1) * size <= qb
                return (fits if limit is None else fits & (i < limit)) & go

            def group(c, size=size, done=done):
                i, acc, run, _ = c
                first = qb - 1 - done - size * i
                part, run = tiles([pl.multiple_of((first - j) * t, t) for j in range(size)], run, [None] * size)
                return i + 1, acc + part, run, alive(run)

            n, acc, run, go = lax.while_loop(more, group, (0,) + carry[1:])
            carry, done = (0, acc, run, go), done + n * size
        o_ref[...] = acc.astype(BF16)
        t_ref[...] = jnp.broadcast_to(run, (t, DH))
        cnt_ref[pl.program_id(0), qb] = done

    qspec = pl.BlockSpec((t, DH), lambda h, i: (i, h))
    kspec = pl.BlockSpec((S, DH), lambda h, i: (0, h))
    return pl.pallas_call(
        body, name="sb_fwd", grid=(NH, S // t),
        in_specs=[qspec, kspec, kspec],
        out_specs=[qspec, qspec, pl.BlockSpec(memory_space=pltpu.SMEM)],
        out_shape=[jax.ShapeDtypeStruct((S, HW), BF16), jax.ShapeDtypeStruct((S, HW), F32),
                   jax.ShapeDtypeStruct((NH, S // t), jnp.int32)],
        compiler_params=_cp(("arbitrary", "arbitrary")),
    )(sqn, skn, svb)


def _sb_bwd(sqn, skn, svb, do, tot, walked, S):
    t = min(SB_T, S)

    def body(cnt_ref, q_ref, k_ref, v_ref, do_ref, t_ref, dq_o, dk_o, dv_o, dv_acc):
        qb = pl.program_id(1)

        @pl.when(qb == 0)
        def _():
            dk_o[...] = jnp.zeros_like(dk_o)
            dv_acc[...] = jnp.zeros_like(dv_acc)

        q = q_ref[...]
        do = do_ref[...].astype(BF16)
        tot_l = jnp.concatenate([t_ref[...]] * (t // DH), axis=1)
        r, c = _sb_iotas(t)
        diag = c < r
        u_upto = (r <= c).astype(BF16)
        u_before = (r < c).astype(BF16)

        def tiles(k0s, run_l, run_e, masks):
            rowsum = lambda x: jnp.sum(x, axis=1, keepdims=True)
            masked = lambda xs: _each(lambda a, m: a if m is None else jnp.where(m, a, 0.0), xs, masks)
            ks = [k_ref[pl.ds(k0, t), :] for k0 in k0s]
            vs = [v_ref[pl.ds(k0, t), :] for k0 in k0s]
            sc = _each(lambda k, m: _sb_scores(q, k, m), ks, masks)
            ls, lneg = [s[0] for s in sc], [s[1] for s in sc]
            sums_l = _each(rowsum, lneg)
            pre_l = _each(lambda x: _prefix(x, u_upto), lneg)
            runs_l = [run_l]
            for s in sums_l:
                runs_l.append(runs_l[-1] + s)
            att = masked(_each(lambda a, b, rn: jnp.exp(a + (tot_l - (rn + b))), ls, pre_l, runs_l[:-1]))
            e = _each(lambda v, a: _dot(do, v, NT) * a, vs, att)
            sums_e = _each(rowsum, e)
            pre_e = _each(lambda x: _prefix(x, u_before), e)
            runs_e = [run_e]
            for s in sums_e:
                runs_e.append(runs_e[-1] + s)
            sg = _each(jnp.exp, ls)
            dz = masked(_each(lambda a, b, rn, s: a * (1.0 - s) - (rn + b) * s, e, pre_e, runs_e[:-1], sg))
            dz = _each(lambda a: (a * DH ** -0.5).astype(BF16), dz)
            dvs = _each(lambda a: _dot(a.astype(BF16), do, TN), att)
            dks = _each(lambda a: _dot(a, q, TN), dz)
            dqs = _each(_dot, dz, ks)
            for k0, dv, dk in zip(k0s, dvs, dks):
                dv_acc[pl.ds(k0, t), :] += dv
                dk_o[pl.ds(k0, t), :] += dk
            return sum(dqs[1:], dqs[0]), runs_l[-1], runs_e[-1]

        walked = cnt_ref[pl.program_id(0), qb]
        early = jnp.maximum(walked - 1, 0)
        z1 = jnp.zeros((t, 1), F32)
        carry, done = (jnp.zeros((t, DH), F32), z1, z1), 0
        for size in _group_sizes(SB_GROUP_BWD):
            n = (early - done) // size

            def group(i, carry, size=size, done=done):
                dq, run_l, run_e = carry
                first = qb - walked + done + size * i
                part, run_l, run_e = tiles([pl.multiple_of((first + j) * t, t) for j in range(size)], run_l, run_e,
                                           [None] * size)
                return dq + part, run_l, run_e

            carry = lax.fori_loop(0, n, group, carry)
            done = done + n * size
        dq, run_l, run_e = carry
        left = jnp.full((t, t), qb > 0)
        part, _, _ = tiles([pl.multiple_of(jnp.maximum(qb - 1, 0) * t, t), pl.multiple_of(qb * t, t)], run_l, run_e,
                           [left, diag])
        dq_o[...] = dq + part

        @pl.when(qb == S // t - 1)
        def _():
            dv_o[...] = dv_acc[...].astype(BF16)

    qspec = pl.BlockSpec((t, DH), lambda h, i, cnt: (i, h))
    kspec = pl.BlockSpec((S, DH), lambda h, i, cnt: (0, h))
    o = jax.ShapeDtypeStruct((S, HW), F32)
    return pl.pallas_call(
        body, name="sb_bwd",
        grid_spec=pltpu.PrefetchScalarGridSpec(
            num_scalar_prefetch=1, grid=(NH, S // t),
            in_specs=[qspec, kspec, kspec, qspec, qspec], out_specs=[qspec, kspec, kspec],
            scratch_shapes=[pltpu.VMEM((S, DH), F32)]),
        out_shape=[o, o, jax.ShapeDtypeStruct((S, HW), BF16)],
        compiler_params=_cp(("parallel", "arbitrary")),
    )(walked, sqn, skn, svb, do, tot)


def _mem_probs(qn, kn):
    s = _dot(qn, kn.astype(BF16), NT) * DH ** -0.5
    p = jnp.exp(s - jnp.max(s, axis=-1, keepdims=True))
    return p / jnp.sum(p, axis=-1, keepdims=True)


def _mem_fwd(qmn, kv, gmk, S):
    ts = _row_tile(S)

    def body(q_ref, kv_ref, gk_ref, o_ref):
        for h in range(NH):
            kn, _ = _rms(kv_ref[:, _hs(h)], gk_ref[...])
            p = _mem_probs(q_ref[:, _hs(h)], kn)
            o_ref[:, _hs(h)] = _dbf(p, kv_ref[:, HW + h * DH:HW + (h + 1) * DH]).astype(BF16)

    return pl.pallas_call(
        body, name="mem_fwd", grid=(S // ts,),
        in_specs=[pl.BlockSpec((ts, HW), lambda i: (i, 0)), pl.BlockSpec((NMEM, 2 * HW), lambda i: (0, 0)),
                  pl.BlockSpec((1, DH), lambda i: (0, 0))],
        out_specs=pl.BlockSpec((ts, HW), lambda i: (i, 0)),
        out_shape=jax.ShapeDtypeStruct((S, HW), BF16),
        compiler_params=_cp(("parallel",)),
    )(qmn, kv, gmk)


def _mem_bwd(proj, qmn, kv, gmq, gmk, do, S):
    ts = _row_tile(S)
    n = S // ts

    def body(mq_ref, q_ref, kv_ref, gq_ref, gk_ref, do_ref, dmq_o, dkv_o, dgq_o, dgk_o, dkn_scr):
        i = pl.program_id(0)

        @pl.when(i == 0)
        def _():
            dkv_o[...] = jnp.zeros_like(dkv_o)
            dgq_o[...] = jnp.zeros_like(dgq_o)
            dkn_scr[...] = jnp.zeros_like(dkn_scr)

        dgq = jnp.zeros((1, DH), F32)
        for h in range(NH):
            km = kv_ref[:, _hs(h)]
            vm = kv_ref[:, HW + h * DH:HW + (h + 1) * DH].astype(BF16)
            kn, _ = _rms(km, gk_ref[...])
            qn = q_ref[:, _hs(h)]
            p = _mem_probs(qn, kn)
            dob = do_ref[:, _hs(h)].astype(BF16)
            dkv_o[:, HW + h * DH:HW + (h + 1) * DH] += _dot(p.astype(BF16), dob, TN)
            dp = _dot(dob, vm, NT)
            dsc = (p * (dp - jnp.sum(dp * p, axis=-1, keepdims=True)) * DH ** -0.5).astype(BF16)
            dkn_scr[:, _hs(h)] += _dot(dsc, qn, TN)
            x = mq_ref[:, _hs(h)]
            _, r = _rms(x, gq_ref[...])
            dx, dg = _rms_bwd(_dot(dsc, kn.astype(BF16)), x, gq_ref[...], r)
            dmq_o[:, _hs(h)] = dx.astype(BF16)
            dgq = dgq + dg
        dgq_o[...] += dgq

        @pl.when(i == n - 1)
        def _():
            dgk = jnp.zeros((1, DH), F32)
            for h in range(NH):
                km = kv_ref[:, _hs(h)]
                _, r = _rms(km, gk_ref[...])
                dx, dg = _rms_bwd(dkn_scr[:, _hs(h)], km, gk_ref[...], r)
                dkv_o[:, _hs(h)] = dx
                dgk = dgk + dg
            dgk_o[...] = dgk

    full = lambda r, c: pl.BlockSpec((r, c), lambda i: (0, 0))
    t512 = pl.BlockSpec((ts, HW), lambda i: (i, 0))
    return pl.pallas_call(
        body, name="mem_bwd", grid=(n,),
        in_specs=[pl.BlockSpec((ts, HW), lambda i: (i, CB_MQ)), t512, full(NMEM, 2 * HW), full(1, DH), full(1, DH),
                  t512],
        out_specs=[t512, full(NMEM, 2 * HW), full(1, DH), full(1, DH)],
        out_shape=[jax.ShapeDtypeStruct((S, HW), BF16), jax.ShapeDtypeStruct((NMEM, 2 * HW), F32),
                   jax.ShapeDtypeStruct((1, DH), F32), jax.ShapeDtypeStruct((1, DH), F32)],
        scratch_shapes=[pltpu.VMEM((NMEM, HW), F32)],
        compiler_params=_cp(("arbitrary",)),
    )(proj, qmn, kv, gmq, gmk, do)


def _gated_gdn(o, z, g):
    sg = _sigmoid(z)
    outs, rs = [], []
    for h in range(NH):
        y, r = _rms(o[:, _hs(h)], g)
        outs.append(y * (z[:, _hs(h)] * sg[:, _hs(h)]))
        rs.append(r)
    return jnp.concatenate(outs, axis=1), rs, sg


def _merge_fwd(x, proj, ogdn, osb, omem, ggdn, wbg, wbs, wbm, wo, S):
    ts = _row_tile(S)

    def body(x_ref, z_ref, g0_ref, g1_ref, g2_ref, og_ref, os_ref, om_ref, gg_ref, wbg_ref, wbs_ref, wbm_ref,
             wo_ref, x1_o, mix_o):
        on, _, _ = _gated_gdn(og_ref[...], z_ref[...], gg_ref[...])
        mix = (_sigmoid(g0_ref[...]) * _dbf(on, wbg_ref[...]) + _sigmoid(g1_ref[...]) * _dbf(os_ref[...], wbs_ref[...])
               + _sigmoid(g2_ref[...]) * _dbf(om_ref[...], wbm_ref[...]))
        mix_o[...] = mix.astype(BF16)
        x1_o[...] = x_ref[...] + _dbf(mix, wo_ref[...])

    t512 = pl.BlockSpec((ts, HW), lambda i: (i, 0))
    t1k = pl.BlockSpec((ts, D), lambda i: (i, 0))
    gate = lambda j: pl.BlockSpec((ts, D), lambda i: (i, 4 + j))
    full = lambda r, c: pl.BlockSpec((r, c), lambda i: (0, 0))
    return pl.pallas_call(
        body, name="merge_fwd", grid=(S // ts,),
        in_specs=[t1k, pl.BlockSpec((ts, HW), lambda i: (i, CB_Z)), gate(0), gate(1), gate(2), t512, t512, t512,
                  full(1, DH), full(HW, D), full(HW, D), full(HW, D), full(D, D)],
        out_specs=[t1k, t1k],
        out_shape=[jax.ShapeDtypeStruct((S, D), F32), jax.ShapeDtypeStruct((S, D), BF16)],
        compiler_params=_cp(("parallel",)),
    )(x, proj, proj, proj, proj, ogdn, osb, omem, ggdn, wbg, wbs, wbm, wo)


def _merge_bwd(dmix, proj, ogdn, osb, omem, ggdn, wbg, wbs, wbm, S):
    ts = _narrow_tile(S)

    def body(dm_ref, z_ref, g0_ref, g1_ref, g2_ref, og_ref, os_ref, om_ref, gg_ref, wbg_ref, wbs_ref, wbm_ref,
             dgl0_o, dgl1_o, dgl2_o, dog_o, dz_o, dos_o, dom_o, dwbg_o, dwbs_o, dwbm_o, dgg_o):
        @pl.when(pl.program_id(0) == 0)
        def _():
            for ref in (dwbg_o, dwbs_o, dwbm_o, dgg_o):
                ref[...] = jnp.zeros_like(ref)

        dm = dm_ref[...]
        og = og_ref[...]
        z = z_ref[...]
        on, rs, sg = _gated_gdn(og, z, gg_ref[...])
        branch = ((on, g0_ref, wbg_ref, dgl0_o, dwbg_o), (os_ref[...], g1_ref, wbs_ref, dgl1_o, dwbs_o),
                  (om_ref[...], g2_ref, wbm_ref, dgl2_o, dwbm_o))
        dos = []
        for o, g_ref, w_ref, dgl_o, dw_o in branch:
            ob = o.astype(BF16)
            gate = _sigmoid(g_ref[...])
            dgl_o[...] = (dm * _dot(ob, w_ref[...]) * gate * (1.0 - gate)).astype(BF16)
            dy = (dm * gate).astype(BF16)
            dw_o[...] += _dot(ob, dy, TN)
            dos.append(_dot(dy, w_ref[...], NT))
        dos_o[...] = dos[1].astype(BF16)
        dom_o[...] = dos[2].astype(BF16)
        don = dos[0]
        dgg = jnp.zeros((1, DH), F32)
        for h in range(NH):
            oh, zh, sh = og[:, _hs(h)], z[:, _hs(h)], sg[:, _hs(h)]
            y = oh * rs[h] * gg_ref[...]
            dz_o[:, _hs(h)] = (don[:, _hs(h)] * y * (sh * (1.0 + zh * (1.0 - sh)))).astype(BF16)
            dx, dg = _rms_bwd(don[:, _hs(h)] * (zh * sh), oh, gg_ref[...], rs[h])
            dog_o[:, _hs(h)] = dx
            dgg = dgg + dg
        dgg_o[...] += dgg

    t512 = pl.BlockSpec((ts, HW), lambda i: (i, 0))
    t1k = pl.BlockSpec((ts, D), lambda i: (i, 0))
    gate = lambda j: pl.BlockSpec((ts, D), lambda i: (i, 4 + j))
    full = lambda r, c: pl.BlockSpec((r, c), lambda i: (0, 0))
    s1k = jax.ShapeDtypeStruct((S, D), BF16)
    s512 = jax.ShapeDtypeStruct((S, HW), BF16)
    wsh = jax.ShapeDtypeStruct((HW, D), F32)
    return pl.pallas_call(
        body, name="merge_bwd", grid=(S // ts,),
        in_specs=[t1k, pl.BlockSpec((ts, HW), lambda i: (i, CB_Z)), gate(0), gate(1), gate(2), t512, t512, t512,
                  full(1, DH), full(HW, D), full(HW, D), full(HW, D)],
        out_specs=[t1k, t1k, t1k, t512, t512, t512, t512, full(HW, D), full(HW, D), full(HW, D), full(1, DH)],
        out_shape=[s1k, s1k, s1k, jax.ShapeDtypeStruct((S, HW), F32), s512, s512, s512, wsh, wsh, wsh,
                   jax.ShapeDtypeStruct((1, DH), F32)],
        compiler_params=_cp(("arbitrary",)),
    )(dmix, proj, proj, proj, proj, ogdn, osb, omem, ggdn, wbg, wbs, wbm)


def _norm_cast(name, x, g):
    rows = x.shape[0]
    ts = min(_row_tile(rows), rows)

    def body(x_ref, g_ref, o_ref):
        o_ref[...] = _rms(x_ref[...], g_ref[...])[0].astype(BF16)

    t1k = pl.BlockSpec((ts, D), lambda i: (i, 0))
    return pl.pallas_call(
        body, name=name, grid=(rows // ts,), in_specs=[t1k, pl.BlockSpec((1, D), lambda i: (0, 0))], out_specs=t1k,
        out_shape=jax.ShapeDtypeStruct((rows, D), BF16), compiler_params=_cp(("parallel",)),
    )(x, g)


def _norm_bwd(name, dh, x, g, res):
    rows = x.shape[0]
    ts = min(_row_tile(rows), rows)

    def body(*refs):
        dh_ref, x_ref, g_ref = refs[:3]
        dx_o, dg_o = refs[-2:]

        @pl.when(pl.program_id(0) == 0)
        def _():
            dg_o[...] = jnp.zeros_like(dg_o)

        xv = x_ref[...]
        _, r = _rms(xv, g_ref[...])
        dx, dg = _rms_bwd(dh_ref[...], xv, g_ref[...], r)
        dx_o[...] = dx if res is None else dx + refs[3][...]
        dg_o[...] += dg

    t1k = pl.BlockSpec((ts, D), lambda i: (i, 0))
    gsp = pl.BlockSpec((1, D), lambda i: (0, 0))
    ops = [dh, x, g] + ([] if res is None else [res])
    return pl.pallas_call(
        body, name=name, grid=(rows // ts,), in_specs=[t1k, t1k, gsp] + ([] if res is None else [t1k]),
        out_specs=[t1k, gsp],
        out_shape=[jax.ShapeDtypeStruct((rows, D), F32), jax.ShapeDtypeStruct((1, D), F32)],
        compiler_params=_cp(("arbitrary",)),
    )(*ops)


def _slab_tile(rows, lanes):
    cap = min(SLAB_TILE * LANES // lanes, rows)
    return max(d for d in range(16, cap + 1, 16) if rows % d == 0)


def _adamw(name, gall, w, m, v):
    rows, lanes = w.shape
    nsrc = gall.shape[0]
    tr = _slab_tile(rows, lanes)

    def body(g_ref, w_ref, m_ref, v_ref, g_o, d_o, m_o, v_o):
        g = g_ref[0].astype(F32)
        for j in range(1, nsrc):
            g = g + g_ref[j].astype(F32)
        m_new = ADAM_B1 * m_ref[...] + (1.0 - ADAM_B1) * g
        v_new = ADAM_B2 * v_ref[...] + (1.0 - ADAM_B2) * jnp.square(g)
        m_hat = m_new / (1.0 - ADAM_B1 ** ADAM_STEP)
        v_hat = v_new / (1.0 - ADAM_B2 ** ADAM_STEP)
        g_o[...] = g
        d_o[...] = -ADAM_LR * (m_hat / (jnp.sqrt(v_hat) + ADAM_EPS) + ADAM_WD * w_ref[...])
        m_o[...] = m_new
        v_o[...] = v_new

    t = pl.BlockSpec((tr, lanes), lambda i: (i, 0))
    o = jax.ShapeDtypeStruct((rows, lanes), F32)
    return pl.pallas_call(
        body, name=name, grid=(rows // tr,),
        in_specs=[pl.BlockSpec((nsrc, tr, lanes), lambda i: (0, i, 0)), t, t, t],
        out_specs=[t, t, t, t], out_shape=[o, o, o, o],
        compiler_params=_cp(("parallel",)),
    )(gall, w, m, v)


def _pair_sum(name, mine, theirs):
    rows, lanes = mine.shape[1:]
    tr = _slab_tile(rows, lanes)
    core = lax.axis_index("c").astype(jnp.int32).reshape(1)

    def body(c_ref, a_ref, b_ref, o_ref):
        o_ref[...] = (a_ref[...].astype(F32) + b_ref[...].astype(F32)).astype(o_ref.dtype)

    blk = pl.BlockSpec((1, tr, lanes), lambda j, i, c_ref: (j, i, 0))
    return pl.pallas_call(
        body, name=name,
        grid_spec=pltpu.PrefetchScalarGridSpec(
            num_scalar_prefetch=1, grid=(NDEV // 2, rows // tr),
            in_specs=[pl.BlockSpec((1, tr, lanes), lambda j, i, c_ref: (2 * j + c_ref[0], i, 0)), blk],
            out_specs=blk),
        out_shape=jax.ShapeDtypeStruct((NDEV // 2, rows, lanes), mine.dtype),
        compiler_params=_cp(("parallel", "parallel")),
    )(core, mine, theirs)


HBM_SPEC = pl.BlockSpec(memory_space=pltpu.HBM)


def _remote(src, dst, send_sems, recv_sems, k, to):
    return pltpu.make_async_remote_copy(src_ref=src, dst_ref=dst, send_sem=send_sems.at[k], recv_sem=recv_sems.at[k],
                                        device_id=to, device_id_type=pl.DeviceIdType.MESH)


def _gather_steps(x_ref, o_ref, send_sems, recv_sems, local_sem):
    ix, iy, ic = lax.axis_index("x"), lax.axis_index("y"), lax.axis_index("c")
    me, sibling = (ix, iy, ic), (ix, iy, 1 - ic)
    chips = [(1 - ix, iy), (ix, 1 - iy), (1 - ix, 1 - iy)]

    def slab(px, py, pc):
        return o_ref.at[4 * px + 2 * py + pc]

    def copy(k, block, to, src=None):
        return _remote(slab(*block) if src is None else src, slab(*block), send_sems, recv_sems, k, to)

    def mine():
        return pltpu.make_async_copy(x_ref, slab(*me), local_sem)

    def first():
        return [copy(0, me, sibling, src=x_ref)] + [copy(1 + j, me, (*chip, ic), src=x_ref)
                                                    for j, chip in enumerate(chips)]

    def passed():
        return [copy(4 + j, (*chip, ic), sibling) for j, chip in enumerate(chips)]

    def start():
        mine().start()
        for cp in first():
            cp.start()

    def forward():
        for j, (chip, cp) in enumerate(zip(chips, passed())):
            copy(1 + j, (*chip, ic), me).wait_recv()
            cp.start()

    def finish():
        copy(0, sibling, me).wait_recv()
        for j, chip in enumerate(chips):
            copy(4 + j, (*chip, 1 - ic), me).wait_recv()
        for cp in first() + passed():
            cp.wait_send()
        mine().wait()

    return start, forward, finish


GATHER_SEMS = [pltpu.SemaphoreType.DMA((NDEV - 1,)), pltpu.SemaphoreType.DMA((NDEV - 1,)), pltpu.SemaphoreType.DMA]


def _gather(name, x):
    rows, cols = x.shape

    def body(x_ref, o_ref, send_sems, recv_sems, local_sem):
        for step in _gather_steps(x_ref, o_ref, send_sems, recv_sems, local_sem):
            step()

    return pl.pallas_call(
        body, name=name, in_specs=[HBM_SPEC], out_specs=HBM_SPEC,
        out_shape=jax.ShapeDtypeStruct((NDEV, rows, cols), x.dtype), scratch_shapes=list(GATHER_SEMS),
    )(x)


def _sibling_exchange(name, x):
    rows, cols = x.shape[-2:]
    nchip = NDEV // 2

    def body(x_ref, o_ref, send_sems, recv_sems):
        ix, iy, ic = lax.axis_index("x"), lax.axis_index("y"), lax.axis_index("c")
        copies = [_remote(x_ref.at[2 * j + (1 - ic)], o_ref.at[j], send_sems, recv_sems, j, (ix, iy, 1 - ic))
                  for j in range(nchip)]
        for cp in copies:
            cp.start()
        for cp in copies:
            cp.wait()

    return pl.pallas_call(
        body, name=name, in_specs=[HBM_SPEC], out_specs=HBM_SPEC,
        out_shape=jax.ShapeDtypeStruct((nchip, rows, cols), x.dtype),
        scratch_shapes=[pltpu.SemaphoreType.DMA((nchip,)), pltpu.SemaphoreType.DMA((nchip,))],
    )(x)


def _chip_steps(x_ref, o_ref, send_sems, recv_sems, local_sem):
    ix, iy, ic = lax.axis_index("x"), lax.axis_index("y"), lax.axis_index("c")
    my_chip = 2 * ix + iy

    def own():
        return pltpu.make_async_copy(x_ref.at[my_chip], o_ref.at[my_chip], local_sem)

    def copies():
        out = []
        for k in range(1, NDEV // 2):
            px, py = ix ^ (k >> 1), iy ^ (k & 1)
            out.append(_remote(x_ref.at[2 * px + py], o_ref.at[my_chip], send_sems, recv_sems, k - 1, (px, py, ic)))
        return out

    def start():
        own().start()
        for cp in copies():
            cp.start()

    def finish():
        for cp in copies():
            cp.wait()
        own().wait()

    return start, (lambda: None), finish


CHIP_SEMS = [pltpu.SemaphoreType.DMA((NDEV // 2 - 1,)), pltpu.SemaphoreType.DMA((NDEV // 2 - 1,)),
             pltpu.SemaphoreType.DMA]


COL_SHARDED = {"w_in": (D, D_IN), "w_br_gdn": (HW, D), "w_br_sb": (HW, D), "w_br_mem": (HW, D), "w_up": (D, DFF),
               "conv_w": (4, 3 * HW)}
ROW_SHARDED = {"w_mem_kv": (D, 2 * HW), "w_o": (D, D), "w_down": (DFF, D)}


def _to_slab(p):
    return p.reshape(p.shape[:-2] + (-1, LANES))


def _from_slab(flat, r, c):
    return flat.reshape(flat.shape[:-2] + (r, c))


def _shard_dims(name):
    if name in COL_SHARDED:
        r, c = COL_SHARDED[name]
        return r, c // NDEV
    r, c = ROW_SHARDED[name]
    return r // NDEV, c


def _pack_rows(parts, total):
    flat = jnp.concatenate(parts, axis=-2)
    return jnp.pad(flat, [(0, 0)] * (flat.ndim - 2) + [(0, total - flat.shape[-2]), (0, 0)])


def _pack_shards(vals, names, total):
    return _pack_rows([_to_slab(vals[n][0]) for n in names], total)


def _pack_full_grads(grads, names, total):
    parts = []
    for name in names:
        g = grads[name]
        r, c = _shard_dims(name)
        if name in COL_SHARDED:
            g = g.reshape(r, NDEV, c).transpose(1, 0, 2)
        else:
            g = g.reshape(NDEV, r, c)
        parts.append(_to_slab(g))
    return _pack_rows(parts, total)


def _unpack_gathered(slabs, names):
    out, pos = {}, 0
    for name in names:
        rows = SLAB_ROWS[name]
        r, c = _shard_dims(name)
        g = _from_slab(slabs[:, pos:pos + rows], r, c)
        pos += rows
        if name in COL_SHARDED:
            out[name] = g.transpose(1, 0, 2).reshape(r, NDEV * c)
        else:
            out[name] = g.reshape(NDEV * r, c)
    return out


def _unpack_shard(flat, names, shapes):
    out, pos = {}, 0
    for name in names:
        rows = SLAB_ROWS[name]
        r, c = _shard_dims(name)
        out[name] = _from_slab(flat[pos:pos + rows], r, c).reshape(shapes[name])
        pos += rows
    return out


def _first_slab(w_in, conv):
    lead = [(0, 0)] * (w_in.ndim - 2)
    taps = conv.reshape(conv.shape[:-2] + (1, -1))
    parts = [jnp.pad(p, lead + [(0, 0), (0, FIRST_LANES - p.shape[-1])]) for p in (w_in, taps)]
    return _pack_rows(parts, R_FIRST)


def _first_unslab(flat):
    cols = 3 * HW // NDEV
    taps = flat[..., D, :4 * cols]
    return flat[..., :D, :D_IN // NDEV], taps.reshape(taps.shape[:-1] + (4, cols))


def _pack_vec(vals):
    row = jnp.concatenate([vals[n] for n in VEC], axis=1)
    return jnp.pad(row, ((0, 0), (0, VEC_WIDTH - row.shape[1])))


def _adamw_vec(gall, w, m, v):
    aligned = [(off, n) for off, n in zip(VEC_OFFSETS, VEC_SIZES) if n % LANES == 0]

    def body(g_ref, w_ref, m_ref, v_ref, *outs):
        g = g_ref[0]
        for j in range(1, NDEV):
            g = g + g_ref[j]
        m_new = ADAM_B1 * m_ref[...] + (1.0 - ADAM_B1) * g
        v_new = ADAM_B2 * v_ref[...] + (1.0 - ADAM_B2) * jnp.square(g)
        m_hat = m_new / (1.0 - ADAM_B1 ** ADAM_STEP)
        v_hat = v_new / (1.0 - ADAM_B2 ** ADAM_STEP)
        delta = -ADAM_LR * (m_hat / (jnp.sqrt(v_hat) + ADAM_EPS) + ADAM_WD * w_ref[...])
        for r, val in enumerate((g, delta, m_new, v_new)):
            outs[r][...] = val
            for i, (off, n) in enumerate(aligned):
                outs[4 + r * len(aligned) + i][...] = val[:, off:off + n]

    full = lambda *shape: pl.BlockSpec(shape, lambda: (0,) * len(shape))
    row = jax.ShapeDtypeStruct((1, VEC_WIDTH), F32)
    out_shape = [row] * 4 + [jax.ShapeDtypeStruct((1, n), F32) for _ in range(4) for _, n in aligned]
    out_specs = [full(1, VEC_WIDTH)] * 4 + [full(1, n) for _ in range(4) for _, n in aligned]
    return pl.pallas_call(
        body, name="adamw_replicated",
        in_specs=[full(NDEV, 1, VEC_WIDTH), full(1, VEC_WIDTH), full(1, VEC_WIDTH), full(1, VEC_WIDTH)],
        out_specs=out_specs, out_shape=out_shape,
    )(gall, w, m, v)


def _unpack_vec(outs, r):
    aligned = [name for name, n in zip(VEC, VEC_SIZES) if n % LANES == 0]
    vals = {name: outs[4 + r * len(aligned) + i] for i, name in enumerate(aligned)}
    for name, off, n in zip(VEC, VEC_OFFSETS, VEC_SIZES):
        if name not in vals:
            vals[name] = outs[r][:, off:off + n]
    return vals


def _pad_w_in(w):
    return jnp.concatenate([w[:, :2048], w[:, 2056:], w[:, 2048:2056], jnp.zeros((D, D_INP - D_IN), w.dtype)], axis=1)


def _unpad_w_in(w):
    return jnp.concatenate([w[:, :2048], w[:, 7168:7176], w[:, 2048:7168]], axis=1)


def _per_head(v):
    return jnp.repeat(v.reshape(NH), DH).reshape(1, HW)


def _local_step(x, mem, target, w, sm, rest_shards):
    S = x.shape[0]
    ts = _row_tile(S)
    tb = 2 * ts
    alog_f, dtb_f = _per_head(sm["a_log"]), _per_head(sm["dt_bias"])
    w = dict(w)

    h1 = _norm_cast("norm1", x, sm["norm1_g"])
    proj, rest = _mm("in_proj", h1, w["w_in"], "nn", 2 * tb, 1536, D, n_outer=True, comm=("gather", rest_shards))
    w.update(_unpack_gathered(rest[:, :sum(SLAB_ROWS[n] for n in REST)], REST))
    gq, gk, gv, gf, bf, sqn, skn, svb, qmn = _pre_fwd(proj, w["conv_w"], alog_f, dtb_f, sm["sb_q_norm_g"],
                                                      sm["sb_k_norm_g"], sm["mem_q_norm_g"], S)
    ogdn, states = _gdn_fwd(gq, gk, gv, gf, bf, S)
    osb, sb_tot, sb_walked = _sb_fwd(sqn, skn, svb, S)
    kv = _mm("mem_kv", mem, w["w_mem_kv"], "nn", NMEM, D, D, pro="rms", pro_g=sm["mem_norm_g"])
    omem = _mem_fwd(qmn, kv, sm["mem_k_norm_g"], S)
    x1, mix = _merge_fwd(x, proj, ogdn, osb, omem, sm["gdn_norm_g"], w["w_br_gdn"], w["w_br_sb"], w["w_br_mem"],
                         w["w_o"], S)
    h2 = _norm_cast("norm2", x1, sm["norm2_g"])
    up = _mm("mlp_up", h2, w["w_up"], "nn", tb, 2048, D, n_outer=True)
    dy, loss = _mm("mlp_down", up, w["w_down"], "nn", tb, D, 1024, pro="relu2", epi="loss", epi_x=(x1, target))

    g = {}
    dup = _mm("d_up", dy, w["w_down"], "nt", tb, 2048, D, epi="drelu2", epi_x=up, out_dtype=BF16)
    g["w_down"] = _mm("dw_down", up, dy, "tn", 1024, D, 2048, pro="relu2")
    g["w_up"] = _mm("dw_up", h2, dup, "tn", D, 1024, 2048)
    dx1, g["norm2_g"] = _mm("d_h2", dup, w["w_up"], "nt", tb, D, 2048, epi="rms_bwd", epi_x=(x1, sm["norm2_g"], dy))

    dmix = _mm("d_mix", dx1, w["w_o"], "nt", tb, D, D)
    g["w_o"] = _mm("dw_o", mix, dx1, "tn", D, D, 1024)
    (dgl0, dgl1, dgl2, dogdn, dz, dosb, domem, g["w_br_gdn"], g["w_br_sb"], g["w_br_mem"],
     g["gdn_norm_g"]) = _merge_bwd(dmix, proj, ogdn, osb, omem, sm["gdn_norm_g"], w["w_br_gdn"], w["w_br_sb"],
                                   w["w_br_mem"], S)
    dmq, dkv, g["mem_q_norm_g"], g["mem_k_norm_g"] = _mem_bwd(proj, qmn, kv, sm["mem_q_norm_g"], sm["mem_k_norm_g"],
                                                             domem, S)
    g["w_mem_kv"] = _mm("dw_mem_kv", mem, dkv, "tn", D, D, NMEM, pro="rms", pro_g=sm["mem_norm_g"])
    dmn = _mm("d_mem_n", dkv, w["w_mem_kv"], "nt", NMEM, D, D)
    _, g["mem_norm_g"] = _norm_bwd("mem_norm_bwd", dmn, mem, sm["mem_norm_g"], None)
    dsqn, dskn, dsv = _sb_bwd(sqn, skn, svb, dosb, sb_tot, sb_walked, S)
    dgq, dgk, dgv, dgf, dbf = _gdn_bwd(gq, gk, gv, gf, bf, states, dogdn, S)
    dc, dab, dsq, dsk, g["conv_w"], dal_f, ddt_f, g["sb_q_norm_g"], g["sb_k_norm_g"] = _pre_bwd(
        proj, w["conv_w"], alog_f, dtb_f, sm["sb_q_norm_g"], sm["sb_k_norm_g"], dgq, dgk, dgv, dgf, dbf, dsqn, dskn, S)
    g["a_log"] = dal_f.reshape(NH, DH)[:, 0].reshape(1, NH)
    g["dt_bias"] = ddt_f.reshape(NH, DH)[:, 0].reshape(1, NH)
    dqkv = _conv_bwd(dc, w["conv_w"], S)

    dproj = jnp.concatenate([dqkv, dz, dsq, dsk, dsv, dmq, dgl0, dgl1, dgl2, dab], axis=1)
    rest_mine = _pack_full_grads(g, REST, R_REST).astype(BF16)
    rest_pair = _pair_sum("pair_sum_rest", rest_mine, _sibling_exchange("scatter_sibling_rest", rest_mine))
    g["w_in"], rest_all = _mm("dw_in", h1, dproj, "tn", D, 1536, 2048, comm=("chips", rest_pair))
    g["w_in"] = _unpad_w_in(g["w_in"])
    by_owner = lambda grad, r, c: grad.reshape(r, NDEV, c // NDEV).transpose(1, 0, 2)
    first_mine = _first_slab(by_owner(g["w_in"], D, D_IN), by_owner(g["conv_w"], 4, 3 * HW)).astype(BF16)
    first_pair = _pair_sum("pair_sum_first", first_mine, _sibling_exchange("scatter_sibling_first", first_mine))
    dx, g["norm1_g"], first_all = _mm("d_h", dproj, w["w_in"], "nt", tb, D, 2560, epi="rms_bwd",
                                      epi_x=(x, sm["norm1_g"], dx1), comm=("chips", first_pair))
    return loss[0, 0], dx, g, rest_all, first_all


def kernel(x, mem, norm1_g, w_in, conv_w, a_log, dt_bias, gdn_norm_g, sb_q_norm_g, sb_k_norm_g, mem_norm_g, w_mem_kv, mem_q_norm_g, mem_k_norm_g, w_br_gdn, w_br_sb, w_br_mem, w_o, norm2_g, w_up, w_down, loss_target, m_norm1_g, m_w_in, m_conv_w, m_a_log, m_dt_bias, m_gdn_norm_g, m_sb_q_norm_g, m_sb_k_norm_g, m_mem_norm_g, m_w_mem_kv, m_mem_q_norm_g, m_mem_k_norm_g, m_w_br_gdn, m_w_br_sb, m_w_br_mem, m_w_o, m_norm2_g, m_w_up, m_w_down, v_norm1_g, v_w_in, v_conv_w, v_a_log, v_dt_bias, v_gdn_norm_g, v_sb_q_norm_g, v_sb_k_norm_g, v_mem_norm_g, v_w_mem_kv, v_mem_q_norm_g, v_mem_k_norm_g, v_w_br_gdn, v_w_br_sb, v_w_br_mem, v_w_o, v_norm2_g, v_w_up, v_w_down):
    given = dict(norm1_g=norm1_g, w_in=w_in, conv_w=conv_w, a_log=a_log, dt_bias=dt_bias, gdn_norm_g=gdn_norm_g,
                 sb_q_norm_g=sb_q_norm_g, sb_k_norm_g=sb_k_norm_g, mem_norm_g=mem_norm_g, w_mem_kv=w_mem_kv,
                 mem_q_norm_g=mem_q_norm_g, mem_k_norm_g=mem_k_norm_g, w_br_gdn=w_br_gdn, w_br_sb=w_br_sb,
                 w_br_mem=w_br_mem, w_o=w_o, norm2_g=norm2_g, w_up=w_up, w_down=w_down)
    mom1 = dict(norm1_g=m_norm1_g, w_in=m_w_in, conv_w=m_conv_w, a_log=m_a_log, dt_bias=m_dt_bias,
                gdn_norm_g=m_gdn_norm_g, sb_q_norm_g=m_sb_q_norm_g, sb_k_norm_g=m_sb_k_norm_g,
                mem_norm_g=m_mem_norm_g, w_mem_kv=m_w_mem_kv, mem_q_norm_g=m_mem_q_norm_g,
                mem_k_norm_g=m_mem_k_norm_g, w_br_gdn=m_w_br_gdn, w_br_sb=m_w_br_sb, w_br_mem=m_w_br_mem, w_o=m_w_o,
                norm2_g=m_norm2_g, w_up=m_w_up, w_down=m_w_down)
    mom2 = dict(norm1_g=v_norm1_g, w_in=v_w_in, conv_w=v_conv_w, a_log=v_a_log, dt_bias=v_dt_bias,
                gdn_norm_g=v_gdn_norm_g, sb_q_norm_g=v_sb_q_norm_g, sb_k_norm_g=v_sb_k_norm_g,
                mem_norm_g=v_mem_norm_g, w_mem_kv=v_w_mem_kv, mem_q_norm_g=v_mem_q_norm_g,
                mem_k_norm_g=v_mem_k_norm_g, w_br_gdn=v_w_br_gdn, w_br_sb=v_w_br_sb, w_br_mem=v_w_br_mem, w_o=v_w_o,
                norm2_g=v_norm2_g, w_up=v_w_up, w_down=v_w_down)
    shapes = {n: given[n].shape for n in WEIGHTS}

    first_loc = _first_slab(given["w_in"][0], given["conv_w"][0])
    rest_loc = _pack_shards(given, REST, R_REST)
    gathered = _gather("gather_first", first_loc.astype(BF16))
    w = {"w_in": _pad_w_in(_first_unslab(gathered)[0].transpose(1, 0, 2).reshape(D, D_IN))}
    conv_loc = jnp.pad(given["conv_w"][0].reshape(-1, LANES), ((0, 2), (0, 0)))
    conv_all = _gather("gather_conv", conv_loc)
    w["conv_w"] = conv_all[:, :6].reshape(NDEV, 4, 3 * HW // NDEV).transpose(1, 0, 2).reshape(4, 3 * HW)
    sm = {n: given[n] for n in SMALL}

    loss, dx, g, rest_all, first_all = _local_step(x[0], mem[0], loss_target[0], w, sm, rest_loc.astype(BF16))
    res_first = _adamw("adamw_first", first_all, first_loc, _first_slab(mom1["w_in"][0], mom1["conv_w"][0]),
                       _first_slab(mom2["w_in"][0], mom2["conv_w"][0]))
    res_rest = _adamw("adamw_rest", rest_all, rest_loc, _pack_shards(mom1, REST, R_REST),
                      _pack_shards(mom2, REST, R_REST))
    gs_all = _gather("gather_small_grads", _pack_vec(g))
    vec_outs = _adamw_vec(gs_all, _pack_vec(given), _pack_vec(mom1), _pack_vec(mom2))

    outs = {}
    for r, prefix in enumerate(("grad_", "delta_", "new_m_", "new_v_")):
        vals = {n: v.reshape(shapes[n]) for n, v in zip(FIRST, _first_unslab(res_first[r]))}
        vals.update(_unpack_shard(res_rest[r], REST, shapes))
        vals.update(_unpack_vec(vec_outs, r))
        for n in WEIGHTS:
            outs[prefix + n] = vals[n]
    loss = lax.psum(loss, ("x", "y", "c"))
    return (loss, dx[None], *[outs[p + n] for p in ("grad_", "delta_", "new_m_", "new_v_") for n in WEIGHTS])
```

```python
import jax
import jax.numpy as jnp
from jax import lax
from jax.experimental import pallas as pl
from jax.experimental.pallas import tpu as pltpu

F32 = jnp.float32
BF16 = jnp.bfloat16

D = 1024
NH = 4
DH = 128
HW = NH * DH
DFF = 4 * D
NMEM = 256
EPS = 1e-6
NDEV = 8
LANES = 128
PAIR = 128
CHUNK = 64
D_IN = 7176
D_INP = 7680
VMEM_LIMIT = 56 * 1024 * 1024

ADAM_LR, ADAM_B1, ADAM_B2, ADAM_EPS, ADAM_WD, ADAM_STEP = 0.001, 0.9, 0.999, 1e-08, 0.01, 10

CB_Z, CB_SQ, CB_SK, CB_SV, CB_MQ, CB_AB = 3, 4, 5, 6, 7, 14

NN = (((1,), (0,)), ((), ()))
NT = (((1,), (1,)), ((), ()))
TN = (((0,), (0,)), ((), ()))

BIG = ("w_in", "w_mem_kv", "w_br_gdn", "w_br_sb", "w_br_mem", "w_o", "w_up", "w_down", "conv_w")
BIG_ROWS = (7176, 1024, 512, 512, 512, 1024, 4096, 4096, 6)
SLAB_ROWS = dict(zip(BIG, BIG_ROWS))
SLAB_TILE = 1216
FIRST = ("w_in", "conv_w")
REST = ("w_mem_kv", "w_br_gdn", "w_br_sb", "w_br_mem", "w_o", "w_up", "w_down")
R_REST = 10 * SLAB_TILE
FIRST_LANES = 1024
R_FIRST = 1040
SMALL = ("norm1_g", "a_log", "dt_bias", "gdn_norm_g", "sb_q_norm_g", "sb_k_norm_g", "mem_norm_g",
         "mem_q_norm_g", "mem_k_norm_g", "norm2_g")
VEC = ("norm1_g", "mem_norm_g", "norm2_g", "gdn_norm_g", "sb_q_norm_g", "sb_k_norm_g", "mem_q_norm_g", "mem_k_norm_g",
       "a_log", "dt_bias")
VEC_SIZES = (1024, 1024, 1024, 128, 128, 128, 128, 128, 4, 4)
VEC_OFFSETS = (0, 1024, 2048, 3072, 3200, 3328, 3456, 3584, 3712, 3716)
VEC_WIDTH = 3840
WEIGHTS = ("norm1_g", "w_in", "conv_w", "a_log", "dt_bias", "gdn_norm_g", "sb_q_norm_g", "sb_k_norm_g",
           "mem_norm_g", "w_mem_kv", "mem_q_norm_g", "mem_k_norm_g", "w_br_gdn", "w_br_sb", "w_br_mem",
           "w_o", "norm2_g", "w_up", "w_down")


def _cp(sem=None):
    return pltpu.CompilerParams(dimension_semantics=sem, vmem_limit_bytes=VMEM_LIMIT)


def _dot(a, b, dims=NN):
    return lax.dot_general(a, b, dims, preferred_element_type=F32)


def _dbf(a, b, dims=NN):
    return _dot(a.astype(BF16), b.astype(BF16), dims)


def _split(a, n):
    parts = []
    for _ in range(n):
        h = a.astype(BF16)
        parts.append(h)
        a = a - h.astype(F32)
    return parts


def _dg(a, b, dims=NN):
    return _dbf(a, b, dims)


def _dxr(a, e, dims=NN):
    eb = e.astype(BF16)
    a1, a2, a3 = _split(a, 3)
    return _dot(a1, eb, dims) + (_dot(a2, eb, dims) + _dot(a3, eb, dims))


def _dxl(e, a, dims=NN):
    eb = e.astype(BF16)
    a1, a2, a3 = _split(a, 3)
    return _dot(eb, a1, dims) + (_dot(eb, a2, dims) + _dot(eb, a3, dims))


def _sigmoid(x):
    return 1.0 / (1.0 + jnp.exp(-x))


def _softplus(x):
    return jnp.maximum(x, 0.0) + jnp.log(1.0 + jnp.exp(-jnp.abs(x)))


def _rms(x, g):
    r = lax.rsqrt(jnp.mean(x * x, axis=-1, keepdims=True) + EPS)
    return x * r * g, r


def _rms_bwd(dy, x, g, r):
    dyg = dy * g
    dx = r * (dyg - x * (r * r) * jnp.mean(dyg * x, axis=-1, keepdims=True))
    dg = jnp.sum(dy * (x * r), axis=0, keepdims=True)
    return dx, dg


def _hs(h):
    return slice(h * DH, (h + 1) * DH)


def _row_tile(s):
    return 512 if s >= 2048 else 256


def _narrow_tile(s):
    return min(256, s)


def _mm(name, a, b, mode, tm, tn, tk, pro=None, pro_g=None, epi=None, epi_x=None, out_dtype=F32, n_outer=False,
        comm=None):
    if mode == "tn":
        K, M = a.shape
    else:
        M, K = a.shape
    N = b.shape[0] if mode == "nt" else b.shape[1]
    tm, tn, tk = min(tm, M), min(tn, N), min(tk, K)
    nk = K // tk
    assert M % tm == 0 and N % tn == 0 and K % tk == 0, (name, M, N, K, tm, tn, tk)
    dims = {"nn": NN, "nt": NT, "tn": TN}[mode]
    reducing = epi in ("rms_bwd", "loss")
    assert not reducing or (tn == N and not n_outer), name
    epi_ops = () if epi is None else (epi_x if isinstance(epi_x, tuple) else (epi_x,))

    def body(*refs):
        a_ref, b_ref = refs[0], refs[1]
        pos = 2
        g_ref = None
        if pro == "rms":
            g_ref = refs[pos]
            pos += 1
        e_refs = refs[pos:pos + len(epi_ops)]
        pos += len(epi_ops)
        cx_ref = None
        if comm is not None:
            cx_ref = refs[pos]
            pos += 1
        o_ref = refs[pos]
        pos += 1
        r_ref = None
        if reducing:
            r_ref = refs[pos]
            pos += 1
        if comm is not None:
            steps_of = _gather_steps if comm[0] == "gather" else _chip_steps
            start, forward, finish_comm = steps_of(cx_ref, refs[pos], *refs[-3:])
            pos += 1
            step = (pl.program_id(0) * grid[1] + pl.program_id(1)) * nk + pl.program_id(2)
            total = grid[0] * grid[1] * nk
            pl.when(step == 0)(start)
            pl.when(step == (4 * total) // 5)(forward)
        av = a_ref[...]
        if pro == "rms":
            av, _ = _rms(av.astype(F32), g_ref[...])
        elif pro == "relu2":
            av = jnp.square(jnp.maximum(av, 0.0))
        part = _dbf(av, b_ref[...], dims)
        first = pl.program_id(0) == 0

        def finish(acc):
            red = None
            if epi == "add":
                acc = acc + e_refs[0][...]
            elif epi == "drelu2":
                acc = acc * (2.0 * jnp.maximum(e_refs[0][...], 0.0))
            elif epi == "rms_bwd":
                xv, gv = e_refs[0][...], e_refs[1][...]
                _, r = _rms(xv, gv)
                dx, red = _rms_bwd(acc, xv, gv, r)
                acc = dx + e_refs[2][...]
            elif epi == "loss":
                err = acc + e_refs[0][...] - e_refs[1][...]
                acc = err * (1.0 / N)
                per_tok = jnp.sum(err * err, axis=1, keepdims=True) * (1.0 / N)
                red = 0.5 * jnp.sum(per_tok, axis=0, keepdims=True)
            o_ref[...] = acc.astype(out_dtype)
            if reducing:

                @pl.when(first)
                def _():
                    r_ref[...] = red

                @pl.when(jnp.logical_not(first))
                def _():
                    r_ref[...] += red

        if nk == 1:
            finish(part)
        else:
            acc_ref = refs[pos]
            k = pl.program_id(2)

            @pl.when(k == 0)
            def _():
                acc_ref[...] = part

            @pl.when(k > 0)
            def _():
                acc_ref[...] += part

            @pl.when(k == nk - 1)
            def _():
                finish(acc_ref[...])

        if comm is not None:
            pl.when(step == total - 1)(finish_comm)

    def spec(shape, index):
        if n_outer:
            return pl.BlockSpec(shape, lambda j, i, k: index(i, j, k))
        return pl.BlockSpec(shape, index)

    if mode == "tn":
        a_spec = spec((tk, tm), lambda i, j, k: (k, i))
    else:
        a_spec = spec((tm, tk), lambda i, j, k: (i, k))
    if mode == "nt":
        b_spec = spec((tn, tk), lambda i, j, k: (j, k))
    else:
        b_spec = spec((tk, tn), lambda i, j, k: (k, j))
    in_specs, ops = [a_spec, b_spec], [a, b]
    if pro == "rms":
        w = pro_g.shape[1]
        assert (tm if mode == "tn" else tk) == w, name
        in_specs.append(spec((1, w), lambda i, j, k: (0, 0)))
        ops.append(pro_g)
    for op in epi_ops:
        if op.shape[0] == 1:
            in_specs.append(spec((1, tn), lambda i, j, k: (0, j)))
        else:
            in_specs.append(spec((tm, tn), lambda i, j, k: (i, j)))
        ops.append(op)
    out_specs = [spec((tm, tn), lambda i, j, k: (i, j))]
    out_shape = [jax.ShapeDtypeStruct((M, N), out_dtype)]
    if reducing:
        width = N if epi == "rms_bwd" else 1
        out_specs.append(spec((1, width), lambda i, j, k: (0, 0)))
        out_shape.append(jax.ShapeDtypeStruct((1, width), F32))
    scratch = [pltpu.VMEM((tm, tn), F32)] if nk > 1 else []
    if comm is not None:
        kind, cx = comm
        in_specs.append(HBM_SPEC)
        ops.append(cx)
        out_specs.append(HBM_SPEC)
        out_shape.append(jax.ShapeDtypeStruct((NDEV if kind == "gather" else NDEV // 2,) + cx.shape[-2:], cx.dtype))
        scratch += list(GATHER_SEMS if kind == "gather" else CHIP_SEMS)
    grid = (N // tn, M // tm, nk) if n_outer else (M // tm, N // tn, nk)
    ordered = reducing or comm is not None
    outs = pl.pallas_call(
        body, name=name, grid=grid,
        in_specs=in_specs, out_specs=out_specs, out_shape=out_shape, scratch_shapes=scratch,
        compiler_params=_cp(("arbitrary" if ordered else "parallel", "arbitrary" if comm is not None else "parallel",
                             "arbitrary")),
    )(*ops)
    return outs if len(out_shape) > 1 else outs[0]


def _head_select(first_lane):
    l = lax.broadcasted_iota(jnp.int32, (LANES, HW), 0)
    c = lax.broadcasted_iota(jnp.int32, (LANES, HW), 1)
    return (l == first_lane + c // DH).astype(F32)


def _conv_taps(buf, cw, ts):
    c = cw[3:4, :] * buf[8:8 + ts, :]
    for j in range(3):
        k = 3 - j
        c = c + cw[j:j + 1, :] * buf[8 - k:8 - k + ts, :]
    return c


def _pre_fwd(proj, conv_w, alog_f, dtb_f, gsq, gsk, gmq, S):
    ts = _row_tile(S)
    hb = ts // 8

    def body(qkv_ref, halo_ref, ab_ref, sq_ref, sk_ref, sv_ref, mq_ref, cw_ref, al_ref, dt_ref, gsq_ref, gsk_ref,
             gmq_ref, gq_o, gk_o, gv_o, gf_o, bf_o, sqn_o, skn_o, svb_o, qmn_o, buf):
        i = pl.program_id(0)
        buf[0:8, :] = jnp.where(i == 0, 0.0, halo_ref[...])
        buf[8:8 + ts, :] = qkv_ref[...]
        c = _conv_taps(buf, cw_ref[...], ts)
        a = c * _sigmoid(c)
        for h in range(NH):
            q = a[:, h * DH:(h + 1) * DH]
            k = a[:, HW + h * DH:HW + (h + 1) * DH]
            gq_o[:, _hs(h)] = q * (lax.rsqrt(jnp.sum(q * q, axis=-1, keepdims=True) + EPS) * DH ** -0.5)
            gk_o[:, _hs(h)] = k * lax.rsqrt(jnp.sum(k * k, axis=-1, keepdims=True) + EPS)
            sqn_o[:, _hs(h)] = _rms(sq_ref[:, _hs(h)], gsq_ref[...])[0].astype(BF16)
            skn_o[:, _hs(h)] = _rms(sk_ref[:, _hs(h)], gsk_ref[...])[0].astype(BF16)
            qmn_o[:, _hs(h)] = _rms(mq_ref[:, _hs(h)], gmq_ref[...])[0].astype(BF16)
        gv_o[...] = a[:, 2 * HW:3 * HW]
        svb_o[...] = sv_ref[...].astype(BF16)
        ab = ab_ref[:, 0:LANES]
        a_bc = _dxr(ab, _head_select(0))
        b_bc = _dxr(ab, _head_select(NH))
        gf_o[...] = -jnp.exp(al_ref[...]) * _softplus(a_bc + dt_ref[...])
        bf_o[...] = _sigmoid(b_bc)

    row = lambda cb: pl.BlockSpec((ts, HW), lambda i: (i, cb))
    full = lambda r, c: pl.BlockSpec((r, c), lambda i: (0, 0))
    f32o = jax.ShapeDtypeStruct((S, HW), F32)
    bfo = jax.ShapeDtypeStruct((S, HW), BF16)
    return pl.pallas_call(
        body, name="pre_fwd", grid=(S // ts,),
        in_specs=[pl.BlockSpec((ts, 3 * HW), lambda i: (i, 0)),
                  pl.BlockSpec((8, 3 * HW), lambda i: (jnp.maximum(i * hb - 1, 0), 0)),
                  row(CB_AB), row(CB_SQ), row(CB_SK), row(CB_SV), row(CB_MQ),
                  full(4, 3 * HW), full(1, HW), full(1, HW), full(1, DH), full(1, DH), full(1, DH)],
        out_specs=[pl.BlockSpec((ts, HW), lambda i: (i, 0))] * 9,
        out_shape=[f32o, f32o, f32o, f32o, f32o, bfo, bfo, bfo, bfo],
        scratch_shapes=[pltpu.VMEM((ts + 8, 3 * HW), F32)],
        compiler_params=_cp(("parallel",)),
    )(proj, proj, proj, proj, proj, proj, proj, conv_w, alog_f, dtb_f, gsq, gsk, gmq)


def _pre_bwd(proj, conv_w, alog_f, dtb_f, gsq, gsk, dgq, dgk, dgv, dgf, dbf, dsqn, dskn, S):
    ts = _narrow_tile(S)
    hb = ts // 8

    def body(qkv_ref, halo_ref, ab_ref, sq_ref, sk_ref, cw_ref, al_ref, dt_ref, gsq_ref, gsk_ref,
             dgq_ref, dgk_ref, dgv_ref, dgf_ref, dbf_ref, dsqn_ref, dskn_ref,
             dc_o, dab_o, dsq_o, dsk_o, dcw_o, dal_o, ddt_o, dgsq_o, dgsk_o, buf):
        i = pl.program_id(0)

        @pl.when(i == 0)
        def _():
            dcw_o[...] = jnp.zeros_like(dcw_o)
            dal_o[...] = jnp.zeros_like(dal_o)
            ddt_o[...] = jnp.zeros_like(ddt_o)
            dgsq_o[...] = jnp.zeros_like(dgsq_o)
            dgsk_o[...] = jnp.zeros_like(dgsk_o)

        buf[0:8, :] = jnp.where(i == 0, 0.0, halo_ref[...])
        buf[8:8 + ts, :] = qkv_ref[...]
        c = _conv_taps(buf, cw_ref[...], ts)
        sg = _sigmoid(c)
        a = c * sg
        dsilu = sg * (1.0 + c * (1.0 - sg))
        dgsq = jnp.zeros((1, DH), F32)
        dgsk = jnp.zeros((1, DH), F32)
        for h in range(NH):
            q = a[:, h * DH:(h + 1) * DH]
            k = a[:, HW + h * DH:HW + (h + 1) * DH]
            nq = lax.rsqrt(jnp.sum(q * q, axis=-1, keepdims=True) + EPS)
            nk = lax.rsqrt(jnp.sum(k * k, axis=-1, keepdims=True) + EPS)
            dyq = dgq_ref[:, _hs(h)]
            dyk = dgk_ref[:, _hs(h)]
            dq = (nq * dyq - q * (nq * nq * nq) * jnp.sum(dyq * q, axis=-1, keepdims=True)) * DH ** -0.5
            dk = nk * dyk - k * (nk * nk * nk) * jnp.sum(dyk * k, axis=-1, keepdims=True)
            dc_o[:, h * DH:(h + 1) * DH] = dq * dsilu[:, h * DH:(h + 1) * DH]
            dc_o[:, HW + h * DH:HW + (h + 1) * DH] = dk * dsilu[:, HW + h * DH:HW + (h + 1) * DH]
            x = sq_ref[:, _hs(h)]
            _, r = _rms(x, gsq_ref[...])
            dx, dg = _rms_bwd(dsqn_ref[:, _hs(h)], x, gsq_ref[...], r)
            dsq_o[:, _hs(h)] = dx.astype(BF16)
            dgsq = dgsq + dg
            x = sk_ref[:, _hs(h)]
            _, r = _rms(x, gsk_ref[...])
            dx, dg = _rms_bwd(dskn_ref[:, _hs(h)], x, gsk_ref[...], r)
            dsk_o[:, _hs(h)] = dx.astype(BF16)
            dgsk = dgsk + dg
        dc_o[:, 2 * HW:3 * HW] = dgv_ref[...] * dsilu[:, 2 * HW:3 * HW]
        dgsq_o[...] += dgsq
        dgsk_o[...] += dgsk
        dc = dc_o[...]
        for j in range(4):
            k = 3 - j
            dcw_o[j:j + 1, :] += jnp.sum(dc * buf[8 - k:8 - k + ts, :], axis=0, keepdims=True)
        ab = ab_ref[:, 0:LANES]
        a_bc = _dxr(ab, _head_select(0))
        b_bc = _dxr(ab, _head_select(NH))
        pre = a_bc + dt_ref[...]
        ea = jnp.exp(al_ref[...])
        dgf = dgf_ref[...]
        dal_o[...] += jnp.sum(dgf * (-ea * _softplus(pre)), axis=0, keepdims=True)
        da = dgf * (-ea * _sigmoid(pre))
        ddt_o[...] += jnp.sum(da, axis=0, keepdims=True)
        beta = _sigmoid(b_bc)
        db = dbf_ref[...] * beta * (1.0 - beta)
        lane = lax.broadcasted_iota(jnp.int32, (ts, LANES), 1)
        dab = jnp.zeros((ts, LANES), F32)
        for h in range(NH):
            dab = dab + jnp.where(lane == h, da[:, _hs(h)], 0.0) + jnp.where(lane == NH + h, db[:, _hs(h)], 0.0)
        dab_o[:, 0:LANES] = dab.astype(BF16)
        dab_o[:, LANES:HW] = jnp.zeros((ts, HW - LANES), BF16)

    row = lambda cb: pl.BlockSpec((ts, HW), lambda i: (i, cb))
    full = lambda r, c: pl.BlockSpec((r, c), lambda i: (0, 0))
    t512 = pl.BlockSpec((ts, HW), lambda i: (i, 0))
    return pl.pallas_call(
        body, name="pre_bwd", grid=(S // ts,),
        in_specs=[pl.BlockSpec((ts, 3 * HW), lambda i: (i, 0)),
                  pl.BlockSpec((8, 3 * HW), lambda i: (jnp.maximum(i * hb - 1, 0), 0)),
                  row(CB_AB), row(CB_SQ), row(CB_SK),
                  full(4, 3 * HW), full(1, HW), full(1, HW), full(1, DH), full(1, DH)] + [t512] * 7,
        out_specs=[pl.BlockSpec((ts, 3 * HW), lambda i: (i, 0)), t512, t512, t512,
                   full(4, 3 * HW), full(1, HW), full(1, HW), full(1, DH), full(1, DH)],
        out_shape=[jax.ShapeDtypeStruct((S, 3 * HW), F32)] + [jax.ShapeDtypeStruct((S, HW), BF16)] * 3
        + [jax.ShapeDtypeStruct((4, 3 * HW), F32), jax.ShapeDtypeStruct((1, HW), F32),
           jax.ShapeDtypeStruct((1, HW), F32), jax.ShapeDtypeStruct((1, DH), F32),
           jax.ShapeDtypeStruct((1, DH), F32)],
        scratch_shapes=[pltpu.VMEM((ts + 8, 3 * HW), F32)],
        compiler_params=_cp(("arbitrary",)),
    )(proj, proj, proj, proj, proj, conv_w, alog_f, dtb_f, gsq, gsk, dgq, dgk, dgv, dgf, dbf, dsqn, dskn)


def _conv_bwd(dc, conv_w, S):
    ts = _row_tile(S)
    hb = ts // 8
    n = S // ts

    def body(dc_ref, halo_ref, cw_ref, o_ref, buf):
        i = pl.program_id(0)
        buf[0:ts, :] = dc_ref[...]
        buf[ts:ts + 8, :] = jnp.where(i == n - 1, 0.0, halo_ref[...])
        cw = cw_ref[...]
        acc = cw[3:4, :] * buf[0:ts, :]
        for k in range(1, 4):
            acc = acc + cw[3 - k:4 - k, :] * buf[k:k + ts, :]
        o_ref[...] = acc.astype(BF16)

    return pl.pallas_call(
        body, name="conv_bwd", grid=(n,),
        in_specs=[pl.BlockSpec((ts, 3 * HW), lambda i: (i, 0)),
                  pl.BlockSpec((8, 3 * HW), lambda i: (jnp.minimum((i + 1) * hb, S // 8 - 1), 0)),
                  pl.BlockSpec((4, 3 * HW), lambda i: (0, 0))],
        out_specs=pl.BlockSpec((ts, 3 * HW), lambda i: (i, 0)),
        out_shape=jax.ShapeDtypeStruct((S, 3 * HW), BF16),
        scratch_shapes=[pltpu.VMEM((ts + 8, 3 * HW), F32)],
        compiler_params=_cp(("parallel",)),
    )(dc, dc, conv_w)


def _gdn_masks():
    r = lax.broadcasted_iota(jnp.int32, (PAIR, PAIR), 0)
    c = lax.broadcasted_iota(jnp.int32, (PAIR, PAIR), 1)
    same = ((r >= CHUNK) & (c >= CHUNK)) | ((r < CHUNK) & (c < CHUNK))
    return dict(r=r, same=same, tril=same & (r >= c), strict=same & (r > c), triu=same & (c >= r), eye=r == c,
                in_a=r < CHUNK, last_a=r == CHUNK - 1, last_b=r == PAIR - 1)


def _each(fn, *cols):
    return [fn(*xs) for xs in zip(*cols)]


def _mul(a, b):
    return a * b


def _top(x):
    return x[:CHUNK]


def _bot(x):
    return x[CHUNK:]


def _rows(a, b):
    return jnp.concatenate([a, b], axis=0)


def _tri_inv(lm, eye):
    eye_f = eye.astype(F32)
    p = _each(lambda l: eye_f - l, lm)
    lp = _each(lambda l: _dg(l, l), lm)
    for it in range(5):
        p = _each(lambda a, b: a + _dg(a, b), p, lp)
        if it < 4:
            lp = _each(lambda b: _dg(b, b), lp)
    return p


def _gdn_block(m, q, k, v, g, beta):
    tril_f = m["tril"].astype(F32)
    col_sum = lambda mask: (lambda x: jnp.sum(jnp.where(mask, x, 0.0), axis=0, keepdims=True))
    gc = _each(lambda x: _dxl(tril_f, x), g)
    gcr = _each(col_sum(m["eye"]), gc)
    gam = _each(lambda a, b: jnp.where(m["tril"], jnp.exp(jnp.minimum(a - b, 0.0)), 0.0), gc, gcr)
    kb = _each(_mul, k, beta)
    vb = _each(_mul, v, beta)
    lm = _each(lambda a, b, c: jnp.where(m["strict"], _dg(a, b, NT) * c, 0.0), kb, k, gam)
    t = _tri_inv(lm, m["eye"])
    eg = _each(jnp.exp, gc)
    kbe = _each(_mul, kb, eg)
    u = _each(_dg, t, vb)
    w = _each(_dg, t, kbe)
    aqk = _each(lambda a, b, c: jnp.where(m["tril"], _dg(a, b, NT) * c, 0.0), q, k, gam)
    qd = _each(_mul, q, eg)
    ga = _each(col_sum(m["last_a"]), gc)
    gb = _each(col_sum(m["last_b"]), gc)
    e2 = _each(lambda a, b, c: jnp.exp(jnp.where(m["in_a"], a, b) - c), ga, gb, gc)
    kd = _each(_mul, k, e2)
    return dict(u=u, w=w, aqk=aqk, qd=qd, kd=kd, gam=gam, kb=kb, vb=vb, lm=lm, t=t, eg=eg, kbe=kbe, e2=e2,
                gla=_each(jnp.exp, ga), glb=_each(jnp.exp, gb))


def _gdn_fwd(gq, gk, gv, gf, bf, S):
    nb = S // PAIR

    def body(q_ref, k_ref, v_ref, g_ref, b_ref, o_ref, st_ref, s_scr):
        @pl.when(pl.program_id(0) == 0)
        def _():
            s_scr[...] = jnp.zeros_like(s_scr)

        m = _gdn_masks()
        heads = lambda ref: [ref[:, _hs(h)] for h in range(NH)]
        f = _gdn_block(m, heads(q_ref), heads(k_ref), heads(v_ref), heads(g_ref), heads(b_ref))
        u, w, qd, kd = f["u"], f["w"], f["qd"], f["kd"]
        s0 = [s_scr[h * DH:(h + 1) * DH, :] for h in range(NH)]
        vna = _each(lambda a, b, s: _top(a) - _dg(_top(b), s), u, w, s0)
        oa = _each(lambda a, s: _dg(_top(a), s), qd, s0)
        s1 = _each(lambda s, gl, a, vn: s * gl + _dg(_top(a), vn, TN), s0, f["gla"], kd, vna)
        vnb = _each(lambda a, b, s: _bot(a) - _dg(_bot(b), s), u, w, s1)
        ob = _each(lambda a, s: _dg(_bot(a), s), qd, s1)
        s2 = _each(lambda s, gl, a, vn: s * gl + _dg(_bot(a), vn, TN), s1, f["glb"], kd, vnb)
        outs = _each(lambda a, b, c, va, vb: _rows(a, b) + _dg(c, _rows(va, vb)), oa, ob, f["aqk"], vna, vnb)
        o_ref[...] = jnp.concatenate(outs, axis=1)
        st_ref[...] = jnp.concatenate(s0 + s1, axis=0)
        s_scr[...] = jnp.concatenate(s2, axis=0)

    blk = pl.BlockSpec((PAIR, HW), lambda i: (i, 0))
    return pl.pallas_call(
        body, name="gdn_fwd", grid=(nb,),
        in_specs=[blk] * 5,
        out_specs=[blk, pl.BlockSpec((2 * NH * DH, DH), lambda i: (i, 0))],
        out_shape=[jax.ShapeDtypeStruct((S, HW), F32), jax.ShapeDtypeStruct((nb * 2 * NH * DH, DH), F32)],
        scratch_shapes=[pltpu.VMEM((NH * DH, DH), F32)],
        compiler_params=_cp(("arbitrary",)),
    )(gq, gk, gv, gf, bf)


def _gdn_bwd(gq, gk, gv, gf, bf, states, do, S):
    nb = S // PAIR

    def body(q_ref, k_ref, v_ref, g_ref, b_ref, st_ref, do_ref, dq_o, dk_o, dv_o, dg_o, db_o, ds_scr):
        @pl.when(pl.program_id(0) == 0)
        def _():
            ds_scr[...] = jnp.zeros_like(ds_scr)

        m = _gdn_masks()
        ones = jnp.ones((PAIR, PAIR), F32)
        heads = lambda ref: [ref[:, _hs(h)] for h in range(NH)]
        q, k, v, beta, do = heads(q_ref), heads(k_ref), heads(v_ref), heads(b_ref), heads(do_ref)
        f = _gdn_block(m, q, k, v, heads(g_ref), beta)
        u, w, aqk, qd, kd, t = f["u"], f["w"], f["aqk"], f["qd"], f["kd"], f["t"]
        s0 = [st_ref[h * DH:(h + 1) * DH, :] for h in range(NH)]
        s1 = [st_ref[(NH + h) * DH:(NH + h + 1) * DH, :] for h in range(NH)]
        ds2 = [ds_scr[h * DH:(h + 1) * DH, :] for h in range(NH)]
        total = lambda a, b: jnp.sum(jnp.sum(a * b, axis=1, keepdims=True), axis=0, keepdims=True)
        vna = _each(lambda a, b, s: _top(a) - _dg(_top(b), s), u, w, s0)
        vnb = _each(lambda a, b, s: _bot(a) - _dg(_bot(b), s), u, w, s1)
        dvn_i = _each(lambda a, b: _dg(a, b, TN), aqk, do)
        dvnb = _each(lambda a, b, s: _bot(a) + _dg(_bot(b), s), dvn_i, kd, ds2)
        dqdb = _each(lambda a, s: _dg(_bot(a), s, NT), do, s1)
        dkdb = _each(lambda a, s: _dg(a, s, NT), vnb, ds2)
        dglb = _each(total, ds2, s1)
        dwb = _each(lambda a, s: -_dg(a, s, NT), dvnb, s1)
        ds1 = _each(lambda s, gl, a, b, c, d: s * gl + _dg(_bot(a), _bot(b), TN) - _dg(_bot(c), d, TN),
                    ds2, f["glb"], qd, do, w, dvnb)
        dvna = _each(lambda a, b, s: _top(a) + _dg(_top(b), s), dvn_i, kd, ds1)
        dqda = _each(lambda a, s: _dg(_top(a), s, NT), do, s0)
        dkda = _each(lambda a, s: _dg(a, s, NT), vna, ds1)
        dgla = _each(total, ds1, s0)
        dwa = _each(lambda a, s: -_dg(a, s, NT), dvna, s0)
        ds0 = _each(lambda s, gl, a, b, c, d: s * gl + _dg(_top(a), _top(b), TN) - _dg(_top(c), d, TN),
                    ds1, f["gla"], qd, do, w, dvna)
        dvn, dqd, dkd, dw = (_each(_rows, a, b) for a, b in ((dvna, dvnb), (dqda, dqdb), (dkda, dkdb), (dwa, dwb)))
        daqk = _each(lambda a, va, vb: jnp.where(m["tril"], _dg(a, _rows(va, vb), NT), 0.0), do, vna, vnb)
        dt = _each(lambda a, b, c, d: _dg(a, b, NT) + _dg(c, d, NT), dvn, f["vb"], dw, f["kbe"])
        dvb = _each(lambda a, b: _dg(a, b, TN), t, dvn)
        dkbe = _each(lambda a, b: _dg(a, b, TN), t, dw)
        dtt = _each(lambda a, b: _dg(a, b, NT), dt, t)
        dl = _each(lambda a, b: -jnp.where(m["strict"], _dg(a, b, TN), 0.0), t, dtt)
        dm = _each(_mul, dl, f["gam"])
        dn = _each(_mul, daqk, f["gam"])
        dkb = _each(lambda a, b, c, d: _dg(a, b) + c * d, dm, k, dkbe, f["eg"])
        dks = _each(lambda a, b, c, d, e, g, h, i: _dg(a, b, TN) + _dg(c, d, TN) + e * g + h * i,
                    dm, f["kb"], dn, q, dkd, f["e2"], beta, dkb)
        dqs = _each(lambda a, b, c, d: _dg(a, b) + c * d, dn, k, dqd, f["eg"])
        gm = _each(lambda a, b, c, d: a * b + c * d, dl, f["lm"], daqk, aqk)
        dkdkd = _each(_mul, dkd, kd)
        dgc = _each(lambda a, b, c, d, e, g: _dxr(a + b * c + d * e - g, ones) - _dxr(a, ones, TN),
                    gm, dqd, qd, dkbe, f["kbe"], dkdkd)
        same_f = m["same"].astype(F32)
        chunk_tot = _each(lambda a: _dxl(same_f, _dxr(a, ones)), dkdkd)
        last = m["last_a"] | m["last_b"]
        dgc = _each(lambda a, b, ga, gla, gb, glb: a + jnp.where(last, b + jnp.where(m["in_a"], ga * gla, gb * glb), 0.0),
                    dgc, chunk_tot, dgla, f["gla"], dglb, f["glb"])
        dbs = _each(lambda a, b, c, d: _dxr(a * b + c * d, ones), dkb, k, dvb, v)
        dvs = _each(_mul, beta, dvb)
        triu_f = m["triu"].astype(F32)
        dgs = _each(lambda a: _dxl(triu_f, a), dgc)
        for ref, parts in ((dq_o, dqs), (dk_o, dks), (dv_o, dvs), (dg_o, dgs), (db_o, dbs)):
            ref[...] = jnp.concatenate(parts, axis=1)
        ds_scr[...] = jnp.concatenate(ds0, axis=0)

    blk = pl.BlockSpec((PAIR, HW), lambda i: (nb - 1 - i, 0))
    o = jax.ShapeDtypeStruct((S, HW), F32)
    return pl.pallas_call(
        body, name="gdn_bwd", grid=(nb,),
        in_specs=[blk] * 5 + [pl.BlockSpec((2 * NH * DH, DH), lambda i: (nb - 1 - i, 0)), blk],
        out_specs=[blk] * 5, out_shape=[o] * 5,
        scratch_shapes=[pltpu.VMEM((NH * DH, DH), F32)],
        compiler_params=_cp(("arbitrary",)),
    )(gq, gk, gv, gf, bf, states, do)


SB_T = 256
SB_GROUP = 4
SB_GROUP_BWD = 4
SB_SINGLES = 1
SB_DEAD = -110.0


def _group_sizes(g):
    sizes = []
    while g >= 1:
        sizes.append(g)
        g //= 2
    return sizes


def _sb_iotas(t):
    return lax.broadcasted_iota(jnp.int32, (t, t), 0), lax.broadcasted_iota(jnp.int32, (t, t), 1)


def _sb_scores(q, k, mask):
    z = _dot(q, k, NT) * DH ** -0.5
    ls = jnp.minimum(z, 0.0) - jnp.log(1.0 + jnp.exp(-jnp.abs(z)))
    lneg = ls - z
    if mask is not None:
        lneg = jnp.where(mask, lneg, 0.0)
    return ls, lneg


def _prefix(x, u):
    xh, xl = _split(x, 2)
    return _dot(xh, u) + _dot(xl, u)


def _sb_fwd(sqn, skn, svb, S):
    t = min(SB_T, S)

    def body(q_ref, k_ref, v_ref, o_ref, t_ref, cnt_ref):
        qb = pl.program_id(1)
        q = q_ref[...]
        r, c = _sb_iotas(t)
        diag = c < r
        u_after = (r > c).astype(BF16)

        def tiles(k0s, run, masks):
            sc = _each(lambda k0, m: _sb_scores(q, k_ref[pl.ds(k0, t), :], m), k0s, masks)
            ls, lneg = [s[0] for s in sc], [s[1] for s in sc]
            sums = _each(lambda x: jnp.sum(x, axis=1, keepdims=True), lneg)
            pre = _each(lambda x: _prefix(x, u_after), lneg)
            runs = [run]
            for s in sums:
                runs.append(runs[-1] + s)
            att = _each(lambda a, b, rn: jnp.exp(a + (rn + b)), ls, pre, runs[:-1])
            att = _each(lambda a, m: a if m is None else jnp.where(m, a, 0.0), att, masks)
            parts = _each(lambda a, k0: _dot(a.astype(BF16), v_ref[pl.ds(k0, t), :]), att, k0s)
            return sum(parts[1:], parts[0]), runs[-1]

        left = jnp.full((t, t), qb > 0)
        acc, run = tiles([pl.multiple_of(qb * t, t), pl.multiple_of(jnp.maximum(qb - 1, 0) * t, t)],
                         jnp.zeros((t, 1), F32), [diag, left])

        def alive(run):
            return jnp.max(run) >= SB_DEAD

        carry, done = (0, acc, run, alive(run)), jnp.minimum(qb, 1)
        for size, limit in [(1, SB_SINGLES)] + [(s, None) for s in _group_sizes(SB_GROUP)]:

            def more(c, size=size, done=done, limit=limit):
                i, _, _, go = c
                fits = done + (i + 1) * size <= qb
                return (fits if limit is None else fits & (i < limit)) & go

            def group(c, size=size, done=done):
                i, acc, run, _ = c
                first = qb - 1 - done - size * i
                part, run = tiles([pl.multiple_of((first - j) * t, t) for j in range(size)], run, [None] * size)
                return i + 1, acc + part, run, alive(run)

            n, acc, run, go = lax.while_loop(more, group, (0,) + carry[1:])
            carry, done = (0, acc, run, go), done + n * size
        o_ref[...] = acc.astype(BF16)
        t_ref[...] = jnp.broadcast_to(run, (t, DH))
        cnt_ref[pl.program_id(0), qb] = done

    qspec = pl.BlockSpec((t, DH), lambda h, i: (i, h))
    kspec = pl.BlockSpec((S, DH), lambda h, i: (0, h))
    return pl.pallas_call(
        body, name="sb_fwd", grid=(NH, S // t),
        in_specs=[qspec, kspec, kspec],
        out_specs=[qspec, qspec, pl.BlockSpec(memory_space=pltpu.SMEM)],
        out_shape=[jax.ShapeDtypeStruct((S, HW), BF16), jax.ShapeDtypeStruct((S, HW), F32),
                   jax.ShapeDtypeStruct((NH, S // t), jnp.int32)],
        compiler_params=_cp(("arbitrary", "arbitrary")),
    )(sqn, skn, svb)


def _sb_bwd(sqn, skn, svb, do, tot, walked, S):
    t = min(SB_T, S)

    def body(cnt_ref, q_ref, k_ref, v_ref, do_ref, t_ref, dq_o, dk_o, dv_o, dv_acc):
        qb = pl.program_id(1)

        @pl.when(qb == 0)
        def _():
            dk_o[...] = jnp.zeros_like(dk_o)
            dv_acc[...] = jnp.zeros_like(dv_acc)

        q = q_ref[...]
        do = do_ref[...].astype(BF16)
        tot_l = jnp.concatenate([t_ref[...]] * (t // DH), axis=1)
        r, c = _sb_iotas(t)
        diag = c < r
        u_upto = (r <= c).astype(BF16)
        u_before = (r < c).astype(BF16)

        def tiles(k0s, run_l, run_e, masks):
            rowsum = lambda x: jnp.sum(x, axis=1, keepdims=True)
            masked = lambda xs: _each(lambda a, m: a if m is None else jnp.where(m, a, 0.0), xs, masks)
            ks = [k_ref[pl.ds(k0, t), :] for k0 in k0s]
            vs = [v_ref[pl.ds(k0, t), :] for k0 in k0s]
            sc = _each(lambda k, m: _sb_scores(q, k, m), ks, masks)
            ls, lneg = [s[0] for s in sc], [s[1] for s in sc]
            sums_l = _each(rowsum, lneg)
            pre_l = _each(lambda x: _prefix(x, u_upto), lneg)
            runs_l = [run_l]
            for s in sums_l:
                runs_l.append(runs_l[-1] + s)
            att = masked(_each(lambda a, b, rn: jnp.exp(a + (tot_l - (rn + b))), ls, pre_l, runs_l[:-1]))
            e = _each(lambda v, a: _dot(do, v, NT) * a, vs, att)
            sums_e = _each(rowsum, e)
            pre_e = _each(lambda x: _prefix(x, u_before), e)
            runs_e = [run_e]
            for s in sums_e:
                runs_e.append(runs_e[-1] + s)
            sg = _each(jnp.exp, ls)
            dz = masked(_each(lambda a, b, rn, s: a * (1.0 - s) - (rn + b) * s, e, pre_e, runs_e[:-1], sg))
            dz = _each(lambda a: (a * DH ** -0.5).astype(BF16), dz)
            dvs = _each(lambda a: _dot(a.astype(BF16), do, TN), att)
            dks = _each(lambda a: _dot(a, q, TN), dz)
            dqs = _each(_dot, dz, ks)
            for k0, dv, dk in zip(k0s, dvs, dks):
                dv_acc[pl.ds(k0, t), :] += dv
                dk_o[pl.ds(k0, t), :] += dk
            return sum(dqs[1:], dqs[0]), runs_l[-1], runs_e[-1]

        walked = cnt_ref[pl.program_id(0), qb]
        early = jnp.maximum(walked - 1, 0)
        z1 = jnp.zeros((t, 1), F32)
        carry, done = (jnp.zeros((t, DH), F32), z1, z1), 0
        for size in _group_sizes(SB_GROUP_BWD):
            n = (early - done) // size

            def group(i, carry, size=size, done=done):
                dq, run_l, run_e = carry
                first = qb - walked + done + size * i
                part, run_l, run_e = tiles([pl.multiple_of((first + j) * t, t) for j in range(size)], run_l, run_e,
                                           [None] * size)
                return dq + part, run_l, run_e

            carry = lax.fori_loop(0, n, group, carry)
            done = done + n * size
        dq, run_l, run_e = carry
        left = jnp.full((t, t), qb > 0)
        part, _, _ = tiles([pl.multiple_of(jnp.maximum(qb - 1, 0) * t, t), pl.multiple_of(qb * t, t)], run_l, run_e,
                           [left, diag])
        dq_o[...] = dq + part

        @pl.when(qb == S // t - 1)
        def _():
            dv_o[...] = dv_acc[...].astype(BF16)

    qspec = pl.BlockSpec((t, DH), lambda h, i, cnt: (i, h))
    kspec = pl.BlockSpec((S, DH), lambda h, i, cnt: (0, h))
    o = jax.ShapeDtypeStruct((S, HW), F32)
    return pl.pallas_call(
        body, name="sb_bwd",
        grid_spec=pltpu.PrefetchScalarGridSpec(
            num_scalar_prefetch=1, grid=(NH, S // t),
            in_specs=[qspec, kspec, kspec, qspec, qspec], out_specs=[qspec, kspec, kspec],
            scratch_shapes=[pltpu.VMEM((S, DH), F32)]),
        out_shape=[o, o, jax.ShapeDtypeStruct((S, HW), BF16)],
        compiler_params=_cp(("parallel", "arbitrary")),
    )(walked, sqn, skn, svb, do, tot)


def _mem_probs(qn, kn):
    s = _dot(qn, kn.astype(BF16), NT) * DH ** -0.5
    p = jnp.exp(s - jnp.max(s, axis=-1, keepdims=True))
    return p / jnp.sum(p, axis=-1, keepdims=True)


def _mem_fwd(qmn, kv, gmk, S):
    ts = _row_tile(S)

    def body(q_ref, kv_ref, gk_ref, o_ref):
        for h in range(NH):
            kn, _ = _rms(kv_ref[:, _hs(h)], gk_ref[...])
            p = _mem_probs(q_ref[:, _hs(h)], kn)
            o_ref[:, _hs(h)] = _dbf(p, kv_ref[:, HW + h * DH:HW + (h + 1) * DH]).astype(BF16)

    return pl.pallas_call(
        body, name="mem_fwd", grid=(S // ts,),
        in_specs=[pl.BlockSpec((ts, HW), lambda i: (i, 0)), pl.BlockSpec((NMEM, 2 * HW), lambda i: (0, 0)),
                  pl.BlockSpec((1, DH), lambda i: (0, 0))],
        out_specs=pl.BlockSpec((ts, HW), lambda i: (i, 0)),
        out_shape=jax.ShapeDtypeStruct((S, HW), BF16),
        compiler_params=_cp(("parallel",)),
    )(qmn, kv, gmk)


def _mem_bwd(proj, qmn, kv, gmq, gmk, do, S):
    ts = _row_tile(S)
    n = S // ts

    def body(mq_ref, q_ref, kv_ref, gq_ref, gk_ref, do_ref, dmq_o, dkv_o, dgq_o, dgk_o, dkn_scr):
        i = pl.program_id(0)

        @pl.when(i == 0)
        def _():
            dkv_o[...] = jnp.zeros_like(dkv_o)
            dgq_o[...] = jnp.zeros_like(dgq_o)
            dkn_scr[...] = jnp.zeros_like(dkn_scr)

        dgq = jnp.zeros((1, DH), F32)
        for h in range(NH):
            km = kv_ref[:, _hs(h)]
            vm = kv_ref[:, HW + h * DH:HW + (h + 1) * DH].astype(BF16)
            kn, _ = _rms(km, gk_ref[...])
            qn = q_ref[:, _hs(h)]
            p = _mem_probs(qn, kn)
            dob = do_ref[:, _hs(h)].astype(BF16)
            dkv_o[:, HW + h * DH:HW + (h + 1) * DH] += _dot(p.astype(BF16), dob, TN)
            dp = _dot(dob, vm, NT)
            dsc = (p * (dp - jnp.sum(dp * p, axis=-1, keepdims=True)) * DH ** -0.5).astype(BF16)
            dkn_scr[:, _hs(h)] += _dot(dsc, qn, TN)
            x = mq_ref[:, _hs(h)]
            _, r = _rms(x, gq_ref[...])
            dx, dg = _rms_bwd(_dot(dsc, kn.astype(BF16)), x, gq_ref[...], r)
            dmq_o[:, _hs(h)] = dx.astype(BF16)
            dgq = dgq + dg
        dgq_o[...] += dgq

        @pl.when(i == n - 1)
        def _():
            dgk = jnp.zeros((1, DH), F32)
            for h in range(NH):
                km = kv_ref[:, _hs(h)]
                _, r = _rms(km, gk_ref[...])
                dx, dg = _rms_bwd(dkn_scr[:, _hs(h)], km, gk_ref[...], r)
                dkv_o[:, _hs(h)] = dx
                dgk = dgk + dg
            dgk_o[...] = dgk

    full = lambda r, c: pl.BlockSpec((r, c), lambda i: (0, 0))
    t512 = pl.BlockSpec((ts, HW), lambda i: (i, 0))
    return pl.pallas_call(
        body, name="mem_bwd", grid=(n,),
        in_specs=[pl.BlockSpec((ts, HW), lambda i: (i, CB_MQ)), t512, full(NMEM, 2 * HW), full(1, DH), full(1, DH),
                  t512],
        out_specs=[t512, full(NMEM, 2 * HW), full(1, DH), full(1, DH)],
        out_shape=[jax.ShapeDtypeStruct((S, HW), BF16), jax.ShapeDtypeStruct((NMEM, 2 * HW), F32),
                   jax.ShapeDtypeStruct((1, DH), F32), jax.ShapeDtypeStruct((1, DH), F32)],
        scratch_shapes=[pltpu.VMEM((NMEM, HW), F32)],
        compiler_params=_cp(("arbitrary",)),
    )(proj, qmn, kv, gmq, gmk, do)


def _gated_gdn(o, z, g):
    sg = _sigmoid(z)
    outs, rs = [], []
    for h in range(NH):
        y, r = _rms(o[:, _hs(h)], g)
        outs.append(y * (z[:, _hs(h)] * sg[:, _hs(h)]))
        rs.append(r)
    return jnp.concatenate(outs, axis=1), rs, sg


def _merge_fwd(x, proj, ogdn, osb, omem, ggdn, wbg, wbs, wbm, wo, S):
    ts = _row_tile(S)

    def body(x_ref, z_ref, g0_ref, g1_ref, g2_ref, og_ref, os_ref, om_ref, gg_ref, wbg_ref, wbs_ref, wbm_ref,
             wo_ref, x1_o, mix_o):
        on, _, _ = _gated_gdn(og_ref[...], z_ref[...], gg_ref[...])
        mix = (_sigmoid(g0_ref[...]) * _dbf(on, wbg_ref[...]) + _sigmoid(g1_ref[...]) * _dbf(os_ref[...], wbs_ref[...])
               + _sigmoid(g2_ref[...]) * _dbf(om_ref[...], wbm_ref[...]))
        mix_o[...] = mix.astype(BF16)
        x1_o[...] = x_ref[...] + _dbf(mix, wo_ref[...])

    t512 = pl.BlockSpec((ts, HW), lambda i: (i, 0))
    t1k = pl.BlockSpec((ts, D), lambda i: (i, 0))
    gate = lambda j: pl.BlockSpec((ts, D), lambda i: (i, 4 + j))
    full = lambda r, c: pl.BlockSpec((r, c), lambda i: (0, 0))
    return pl.pallas_call(
        body, name="merge_fwd", grid=(S // ts,),
        in_specs=[t1k, pl.BlockSpec((ts, HW), lambda i: (i, CB_Z)), gate(0), gate(1), gate(2), t512, t512, t512,
                  full(1, DH), full(HW, D), full(HW, D), full(HW, D), full(D, D)],
        out_specs=[t1k, t1k],
        out_shape=[jax.ShapeDtypeStruct((S, D), F32), jax.ShapeDtypeStruct((S, D), BF16)],
        compiler_params=_cp(("parallel",)),
    )(x, proj, proj, proj, proj, ogdn, osb, omem, ggdn, wbg, wbs, wbm, wo)


def _merge_bwd(dmix, proj, ogdn, osb, omem, ggdn, wbg, wbs, wbm, S):
    ts = _narrow_tile(S)

    def body(dm_ref, z_ref, g0_ref, g1_ref, g2_ref, og_ref, os_ref, om_ref, gg_ref, wbg_ref, wbs_ref, wbm_ref,
             dgl0_o, dgl1_o, dgl2_o, dog_o, dz_o, dos_o, dom_o, dwbg_o, dwbs_o, dwbm_o, dgg_o):
        @pl.when(pl.program_id(0) == 0)
        def _():
            for ref in (dwbg_o, dwbs_o, dwbm_o, dgg_o):
                ref[...] = jnp.zeros_like(ref)

        dm = dm_ref[...]
        og = og_ref[...]
        z = z_ref[...]
        on, rs, sg = _gated_gdn(og, z, gg_ref[...])
        branch = ((on, g0_ref, wbg_ref, dgl0_o, dwbg_o), (os_ref[...], g1_ref, wbs_ref, dgl1_o, dwbs_o),
                  (om_ref[...], g2_ref, wbm_ref, dgl2_o, dwbm_o))
        dos = []
        for o, g_ref, w_ref, dgl_o, dw_o in branch:
            ob = o.astype(BF16)
            gate = _sigmoid(g_ref[...])
            dgl_o[...] = (dm * _dot(ob, w_ref[...]) * gate * (1.0 - gate)).astype(BF16)
            dy = (dm * gate).astype(BF16)
            dw_o[...] += _dot(ob, dy, TN)
            dos.append(_dot(dy, w_ref[...], NT))
        dos_o[...] = dos[1].astype(BF16)
        dom_o[...] = dos[2].astype(BF16)
        don = dos[0]
        dgg = jnp.zeros((1, DH), F32)
        for h in range(NH):
            oh, zh, sh = og[:, _hs(h)], z[:, _hs(h)], sg[:, _hs(h)]
            y = oh * rs[h] * gg_ref[...]
            dz_o[:, _hs(h)] = (don[:, _hs(h)] * y * (sh * (1.0 + zh * (1.0 - sh)))).astype(BF16)
            dx, dg = _rms_bwd(don[:, _hs(h)] * (zh * sh), oh, gg_ref[...], rs[h])
            dog_o[:, _hs(h)] = dx
            dgg = dgg + dg
        dgg_o[...] += dgg

    t512 = pl.BlockSpec((ts, HW), lambda i: (i, 0))
    t1k = pl.BlockSpec((ts, D), lambda i: (i, 0))
    gate = lambda j: pl.BlockSpec((ts, D), lambda i: (i, 4 + j))
    full = lambda r, c: pl.BlockSpec((r, c), lambda i: (0, 0))
    s1k = jax.ShapeDtypeStruct((S, D), BF16)
    s512 = jax.ShapeDtypeStruct((S, HW), BF16)
    wsh = jax.ShapeDtypeStruct((HW, D), F32)
    return pl.pallas_call(
        body, name="merge_bwd", grid=(S // ts,),
        in_specs=[t1k, pl.BlockSpec((ts, HW), lambda i: (i, CB_Z)), gate(0), gate(1), gate(2), t512, t512, t512,
                  full(1, DH), full(HW, D), full(HW, D), full(HW, D)],
        out_specs=[t1k, t1k, t1k, t512, t512, t512, t512, full(HW, D), full(HW, D), full(HW, D), full(1, DH)],
        out_shape=[s1k, s1k, s1k, jax.ShapeDtypeStruct((S, HW), F32), s512, s512, s512, wsh, wsh, wsh,
                   jax.ShapeDtypeStruct((1, DH), F32)],
        compiler_params=_cp(("arbitrary",)),
    )(dmix, proj, proj, proj, proj, ogdn, osb, omem, ggdn, wbg, wbs, wbm)


def _norm_cast(name, x, g):
    rows = x.shape[0]
    ts = min(_row_tile(rows), rows)

    def body(x_ref, g_ref, o_ref):
        o_ref[...] = _rms(x_ref[...], g_ref[...])[0].astype(BF16)

    t1k = pl.BlockSpec((ts, D), lambda i: (i, 0))
    return pl.pallas_call(
        body, name=name, grid=(rows // ts,), in_specs=[t1k, pl.BlockSpec((1, D), lambda i: (0, 0))], out_specs=t1k,
        out_shape=jax.ShapeDtypeStruct((rows, D), BF16), compiler_params=_cp(("parallel",)),
    )(x, g)


def _norm_bwd(name, dh, x, g, res):
    rows = x.shape[0]
    ts = min(_row_tile(rows), rows)

    def body(*refs):
        dh_ref, x_ref, g_ref = refs[:3]
        dx_o, dg_o = refs[-2:]

        @pl.when(pl.program_id(0) == 0)
        def _():
            dg_o[...] = jnp.zeros_like(dg_o)

        xv = x_ref[...]
        _, r = _rms(xv, g_ref[...])
        dx, dg = _rms_bwd(dh_ref[...], xv, g_ref[...], r)
        dx_o[...] = dx if res is None else dx + refs[3][...]
        dg_o[...] += dg

    t1k = pl.BlockSpec((ts, D), lambda i: (i, 0))
    gsp = pl.BlockSpec((1, D), lambda i: (0, 0))
    ops = [dh, x, g] + ([] if res is None else [res])
    return pl.pallas_call(
        body, name=name, grid=(rows // ts,), in_specs=[t1k, t1k, gsp] + ([] if res is None else [t1k]),
        out_specs=[t1k, gsp],
        out_shape=[jax.ShapeDtypeStruct((rows, D), F32), jax.ShapeDtypeStruct((1, D), F32)],
        compiler_params=_cp(("arbitrary",)),
    )(*ops)


def _slab_tile(rows, lanes):
    cap = min(SLAB_TILE * LANES // lanes, rows)
    return max(d for d in range(16, cap + 1, 16) if rows % d == 0)


def _adamw(name, gall, w, m, v):
    rows, lanes = w.shape
    nsrc = gall.shape[0]
    tr = _slab_tile(rows, lanes)

    def body(g_ref, w_ref, m_ref, v_ref, g_o, d_o, m_o, v_o):
        g = g_ref[0].astype(F32)
        for j in range(1, nsrc):
            g = g + g_ref[j].astype(F32)
        m_new = ADAM_B1 * m_ref[...] + (1.0 - ADAM_B1) * g
        v_new = ADAM_B2 * v_ref[...] + (1.0 - ADAM_B2) * jnp.square(g)
        m_hat = m_new / (1.0 - ADAM_B1 ** ADAM_STEP)
        v_hat = v_new / (1.0 - ADAM_B2 ** ADAM_STEP)
        g_o[...] = g
        d_o[...] = -ADAM_LR * (m_hat / (jnp.sqrt(v_hat) + ADAM_EPS) + ADAM_WD * w_ref[...])
        m_o[...] = m_new
        v_o[...] = v_new

    t = pl.BlockSpec((tr, lanes), lambda i: (i, 0))
    o = jax.ShapeDtypeStruct((rows, lanes), F32)
    return pl.pallas_call(
        body, name=name, grid=(rows // tr,),
        in_specs=[pl.BlockSpec((nsrc, tr, lanes), lambda i: (0, i, 0)), t, t, t],
        out_specs=[t, t, t, t], out_shape=[o, o, o, o],
        compiler_params=_cp(("parallel",)),
    )(gall, w, m, v)


def _pair_sum(name, mine, theirs):
    rows, lanes = mine.shape[1:]
    tr = _slab_tile(rows, lanes)
    core = lax.axis_index("c").astype(jnp.int32).reshape(1)

    def body(c_ref, a_ref, b_ref, o_ref):
        o_ref[...] = (a_ref[...].astype(F32) + b_ref[...].astype(F32)).astype(o_ref.dtype)

    blk = pl.BlockSpec((1, tr, lanes), lambda j, i, c_ref: (j, i, 0))
    return pl.pallas_call(
        body, name=name,
        grid_spec=pltpu.PrefetchScalarGridSpec(
            num_scalar_prefetch=1, grid=(NDEV // 2, rows // tr),
            in_specs=[pl.BlockSpec((1, tr, lanes), lambda j, i, c_ref: (2 * j + c_ref[0], i, 0)), blk],
            out_specs=blk),
        out_shape=jax.ShapeDtypeStruct((NDEV // 2, rows, lanes), mine.dtype),
        compiler_params=_cp(("parallel", "parallel")),
    )(core, mine, theirs)


HBM_SPEC = pl.BlockSpec(memory_space=pltpu.HBM)


def _remote(src, dst, send_sems, recv_sems, k, to):
    return pltpu.make_async_remote_copy(src_ref=src, dst_ref=dst, send_sem=send_sems.at[k], recv_sem=recv_sems.at[k],
                                        device_id=to, device_id_type=pl.DeviceIdType.MESH)


def _gather_steps(x_ref, o_ref, send_sems, recv_sems, local_sem):
    ix, iy, ic = lax.axis_index("x"), lax.axis_index("y"), lax.axis_index("c")
    me, sibling = (ix, iy, ic), (ix, iy, 1 - ic)
    chips = [(1 - ix, iy), (ix, 1 - iy), (1 - ix, 1 - iy)]

    def slab(px, py, pc):
        return o_ref.at[4 * px + 2 * py + pc]

    def copy(k, block, to, src=None):
        return _remote(slab(*block) if src is None else src, slab(*block), send_sems, recv_sems, k, to)

    def mine():
        return pltpu.make_async_copy(x_ref, slab(*me), local_sem)

    def first():
        return [copy(0, me, sibling, src=x_ref)] + [copy(1 + j, me, (*chip, ic), src=x_ref)
                                                    for j, chip in enumerate(chips)]

    def passed():
        return [copy(4 + j, (*chip, ic), sibling) for j, chip in enumerate(chips)]

    def start():
        mine().start()
        for cp in first():
            cp.start()

    def forward():
        for j, (chip, cp) in enumerate(zip(chips, passed())):
            copy(1 + j, (*chip, ic), me).wait_recv()
            cp.start()

    def finish():
        copy(0, sibling, me).wait_recv()
        for j, chip in enumerate(chips):
            copy(4 + j, (*chip, 1 - ic), me).wait_recv()
        for cp in first() + passed():
            cp.wait_send()
        mine().wait()

    return start, forward, finish


GATHER_SEMS = [pltpu.SemaphoreType.DMA((NDEV - 1,)), pltpu.SemaphoreType.DMA((NDEV - 1,)), pltpu.SemaphoreType.DMA]


def _gather(name, x):
    rows, cols = x.shape

    def body(x_ref, o_ref, send_sems, recv_sems, local_sem):
        for step in _gather_steps(x_ref, o_ref, send_sems, recv_sems, local_sem):
            step()

    return pl.pallas_call(
        body, name=name, in_specs=[HBM_SPEC], out_specs=HBM_SPEC,
        out_shape=jax.ShapeDtypeStruct((NDEV, rows, cols), x.dtype), scratch_shapes=list(GATHER_SEMS),
    )(x)


def _sibling_exchange(name, x):
    rows, cols = x.shape[-2:]
    nchip = NDEV // 2

    def body(x_ref, o_ref, send_sems, recv_sems):
        ix, iy, ic = lax.axis_index("x"), lax.axis_index("y"), lax.axis_index("c")
        copies = [_remote(x_ref.at[2 * j + (1 - ic)], o_ref.at[j], send_sems, recv_sems, j, (ix, iy, 1 - ic))
                  for j in range(nchip)]
        for cp in copies:
            cp.start()
        for cp in copies:
            cp.wait()

    return pl.pallas_call(
        body, name=name, in_specs=[HBM_SPEC], out_specs=HBM_SPEC,
        out_shape=jax.ShapeDtypeStruct((nchip, rows, cols), x.dtype),
        scratch_shapes=[pltpu.SemaphoreType.DMA((nchip,)), pltpu.SemaphoreType.DMA((nchip,))],
    )(x)


def _chip_steps(x_ref, o_ref, send_sems, recv_sems, local_sem):
    ix, iy, ic = lax.axis_index("x"), lax.axis_index("y"), lax.axis_index("c")
    my_chip = 2 * ix + iy

    def own():
        return pltpu.make_async_copy(x_ref.at[my_chip], o_ref.at[my_chip], local_sem)

    def copies():
        out = []
        for k in range(1, NDEV // 2):
            px, py = ix ^ (k >> 1), iy ^ (k & 1)
            out.append(_remote(x_ref.at[2 * px + py], o_ref.at[my_chip], send_sems, recv_sems, k - 1, (px, py, ic)))
        return out

    def start():
        own().start()
        for cp in copies():
            cp.start()

    def finish():
        for cp in copies():
            cp.wait()
        own().wait()

    return start, (lambda: None), finish


CHIP_SEMS = [pltpu.SemaphoreType.DMA((NDEV // 2 - 1,)), pltpu.SemaphoreType.DMA((NDEV // 2 - 1,)),
             pltpu.SemaphoreType.DMA]


COL_SHARDED = {"w_in": (D, D_IN), "w_br_gdn": (HW, D), "w_br_sb": (HW, D), "w_br_mem": (HW, D), "w_up": (D, DFF),
               "conv_w": (4, 3 * HW)}
ROW_SHARDED = {"w_mem_kv": (D, 2 * HW), "w_o": (D, D), "w_down": (DFF, D)}


def _to_slab(p):
    return p.reshape(p.shape[:-2] + (-1, LANES))


def _from_slab(flat, r, c):
    return flat.reshape(flat.shape[:-2] + (r, c))


def _shard_dims(name):
    if name in COL_SHARDED:
        r, c = COL_SHARDED[name]
        return r, c // NDEV
    r, c = ROW_SHARDED[name]
    return r // NDEV, c


def _pack_rows(parts, total):
    flat = jnp.concatenate(parts, axis=-2)
    return jnp.pad(flat, [(0, 0)] * (flat.ndim - 2) + [(0, total - flat.shape[-2]), (0, 0)])


def _pack_shards(vals, names, total):
    return _pack_rows([_to_slab(vals[n][0]) for n in names], total)


def _pack_full_grads(grads, names, total):
    parts = []
    for name in names:
        g = grads[name]
        r, c = _shard_dims(name)
        if name in COL_SHARDED:
            g = g.reshape(r, NDEV, c).transpose(1, 0, 2)
        else:
            g = g.reshape(NDEV, r, c)
        parts.append(_to_slab(g))
    return _pack_rows(parts, total)


def _unpack_gathered(slabs, names):
    out, pos = {}, 0
    for name in names:
        rows = SLAB_ROWS[name]
        r, c = _shard_dims(name)
        g = _from_slab(slabs[:, pos:pos + rows], r, c)
        pos += rows
        if name in COL_SHARDED:
            out[name] = g.transpose(1, 0, 2).reshape(r, NDEV * c)
        else:
            out[name] = g.reshape(NDEV * r, c)
    return out


def _unpack_shard(flat, names, shapes):
    out, pos = {}, 0
    for name in names:
        rows = SLAB_ROWS[name]
        r, c = _shard_dims(name)
        out[name] = _from_slab(flat[pos:pos + rows], r, c).reshape(shapes[name])
        pos += rows
    return out


def _first_slab(w_in, conv):
    lead = [(0, 0)] * (w_in.ndim - 2)
    taps = conv.reshape(conv.shape[:-2] + (1, -1))
    parts = [jnp.pad(p, lead + [(0, 0), (0, FIRST_LANES - p.shape[-1])]) for p in (w_in, taps)]
    return _pack_rows(parts, R_FIRST)


def _first_unslab(flat):
    cols = 3 * HW // NDEV
    taps = flat[..., D, :4 * cols]
    return flat[..., :D, :D_IN // NDEV], taps.reshape(taps.shape[:-1] + (4, cols))


def _pack_vec(vals):
    row = jnp.concatenate([vals[n] for n in VEC], axis=1)
    return jnp.pad(row, ((0, 0), (0, VEC_WIDTH - row.shape[1])))


def _adamw_vec(gall, w, m, v):
    aligned = [(off, n) for off, n in zip(VEC_OFFSETS, VEC_SIZES) if n % LANES == 0]

    def body(g_ref, w_ref, m_ref, v_ref, *outs):
        g = g_ref[0]
        for j in range(1, NDEV):
            g = g + g_ref[j]
        m_new = ADAM_B1 * m_ref[...] + (1.0 - ADAM_B1) * g
        v_new = ADAM_B2 * v_ref[...] + (1.0 - ADAM_B2) * jnp.square(g)
        m_hat = m_new / (1.0 - ADAM_B1 ** ADAM_STEP)
        v_hat = v_new / (1.0 - ADAM_B2 ** ADAM_STEP)
        delta = -ADAM_LR * (m_hat / (jnp.sqrt(v_hat) + ADAM_EPS) + ADAM_WD * w_ref[...])
        for r, val in enumerate((g, delta, m_new, v_new)):
            outs[r][...] = val
            for i, (off, n) in enumerate(aligned):
                outs[4 + r * len(aligned) + i][...] = val[:, off:off + n]

    full = lambda *shape: pl.BlockSpec(shape, lambda: (0,) * len(shape))
    row = jax.ShapeDtypeStruct((1, VEC_WIDTH), F32)
    out_shape = [row] * 4 + [jax.ShapeDtypeStruct((1, n), F32) for _ in range(4) for _, n in aligned]
    out_specs = [full(1, VEC_WIDTH)] * 4 + [full(1, n) for _ in range(4) for _, n in aligned]
    return pl.pallas_call(
        body, name="adamw_replicated",
        in_specs=[full(NDEV, 1, VEC_WIDTH), full(1, VEC_WIDTH), full(1, VEC_WIDTH), full(1, VEC_WIDTH)],
        out_specs=out_specs, out_shape=out_shape,
    )(gall, w, m, v)


def _unpack_vec(outs, r):
    aligned = [name for name, n in zip(VEC, VEC_SIZES) if n % LANES == 0]
    vals = {name: outs[4 + r * len(aligned) + i] for i, name in enumerate(aligned)}
    for name, off, n in zip(VEC, VEC_OFFSETS, VEC_SIZES):
        if name not in vals:
            vals[name] = outs[r][:, off:off + n]
    return vals


def _pad_w_in(w):
    return jnp.concatenate([w[:, :2048], w[:, 2056:], w[:, 2048:2056], jnp.zeros((D, D_INP - D_IN), w.dtype)], axis=1)


def _unpad_w_in(w):
    return jnp.concatenate([w[:, :2048], w[:, 7168:7176], w[:, 2048:7168]], axis=1)


def _per_head(v):
    return jnp.repeat(v.reshape(NH), DH).reshape(1, HW)


def _local_step(x, mem, target, w, sm, rest_shards):
    S = x.shape[0]
    ts = _row_tile(S)
    tb = 2 * ts
    alog_f, dtb_f = _per_head(sm["a_log"]), _per_head(sm["dt_bias"])
    w = dict(w)

    h1 = _norm_cast("norm1", x, sm["norm1_g"])
    proj, rest = _mm("in_proj", h1, w["w_in"], "nn", 2 * tb, 1536, D, n_outer=True, comm=("gather", rest_shards))
    w.update(_unpack_gathered(rest[:, :sum(SLAB_ROWS[n] for n in REST)], REST))
    gq, gk, gv, gf, bf, sqn, skn, svb, qmn = _pre_fwd(proj, w["conv_w"], alog_f, dtb_f, sm["sb_q_norm_g"],
                                                      sm["sb_k_norm_g"], sm["mem_q_norm_g"], S)
    ogdn, states = _gdn_fwd(gq, gk, gv, gf, bf, S)
    osb, sb_tot, sb_walked = _sb_fwd(sqn, skn, svb, S)
    kv = _mm("mem_kv", mem, w["w_mem_kv"], "nn", NMEM, D, D, pro="rms", pro_g=sm["mem_norm_g"])
    omem = _mem_fwd(qmn, kv, sm["mem_k_norm_g"], S)
    x1, mix = _merge_fwd(x, proj, ogdn, osb, omem, sm["gdn_norm_g"], w["w_br_gdn"], w["w_br_sb"], w["w_br_mem"],
                         w["w_o"], S)
    h2 = _norm_cast("norm2", x1, sm["norm2_g"])
    up = _mm("mlp_up", h2, w["w_up"], "nn", 2 * tb, 2048, D, n_outer=True)
    dy, loss = _mm("mlp_down", up, w["w_down"], "nn", tb, D, 1024, pro="relu2", epi="loss", epi_x=(x1, target))

    g = {}
    dup = _mm("d_up", dy, w["w_down"], "nt", tb, 2048, D, epi="drelu2", epi_x=up, out_dtype=BF16)
    g["w_down"] = _mm("dw_down", up, dy, "tn", 1024, D, 2048, pro="relu2")
    g["w_up"] = _mm("dw_up", h2, dup, "tn", D, 1024, 2048)
    dx1, g["norm2_g"] = _mm("d_h2", dup, w["w_up"], "nt", tb, D, 2048, epi="rms_bwd", epi_x=(x1, sm["norm2_g"], dy))

    dmix = _mm("d_mix", dx1, w["w_o"], "nt", 2 * tb, D, D)
    g["w_o"] = _mm("dw_o", mix, dx1, "tn", D, D, 2048)
    (dgl0, dgl1, dgl2, dogdn, dz, dosb, domem, g["w_br_gdn"], g["w_br_sb"], g["w_br_mem"],
     g["gdn_norm_g"]) = _merge_bwd(dmix, proj, ogdn, osb, omem, sm["gdn_norm_g"], w["w_br_gdn"], w["w_br_sb"],
                                   w["w_br_mem"], S)
    dmq, dkv, g["mem_q_norm_g"], g["mem_k_norm_g"] = _mem_bwd(proj, qmn, kv, sm["mem_q_norm_g"], sm["mem_k_norm_g"],
                                                             domem, S)
    g["w_mem_kv"] = _mm("dw_mem_kv", mem, dkv, "tn", D, D, NMEM, pro="rms", pro_g=sm["mem_norm_g"])
    dmn = _mm("d_mem_n", dkv, w["w_mem_kv"], "nt", NMEM, D, D)
    _, g["mem_norm_g"] = _norm_bwd("mem_norm_bwd", dmn, mem, sm["mem_norm_g"], None)
    dsqn, dskn, dsv = _sb_bwd(sqn, skn, svb, dosb, sb_tot, sb_walked, S)
    dgq, dgk, dgv, dgf, dbf = _gdn_bwd(gq, gk, gv, gf, bf, states, dogdn, S)
    dc, dab, dsq, dsk, g["conv_w"], dal_f, ddt_f, g["sb_q_norm_g"], g["sb_k_norm_g"] = _pre_bwd(
        proj, w["conv_w"], alog_f, dtb_f, sm["sb_q_norm_g"], sm["sb_k_norm_g"], dgq, dgk, dgv, dgf, dbf, dsqn, dskn, S)
    g["a_log"] = dal_f.reshape(NH, DH)[:, 0].reshape(1, NH)
    g["dt_bias"] = ddt_f.reshape(NH, DH)[:, 0].reshape(1, NH)
    dqkv = _conv_bwd(dc, w["conv_w"], S)

    dproj = jnp.concatenate([dqkv, dz, dsq, dsk, dsv, dmq, dgl0, dgl1, dgl2, dab], axis=1)
    rest_mine = _pack_full_grads(g, REST, R_REST).astype(BF16)
    rest_pair = _pair_sum("pair_sum_rest", rest_mine, _sibling_exchange("scatter_sibling_rest", rest_mine))
    g["w_in"], rest_all = _mm("dw_in", h1, dproj, "tn", D, 1536, 2048, comm=("chips", rest_pair))
    g["w_in"] = _unpad_w_in(g["w_in"])
    by_owner = lambda grad, r, c: grad.reshape(r, NDEV, c // NDEV).transpose(1, 0, 2)
    first_mine = _first_slab(by_owner(g["w_in"], D, D_IN), by_owner(g["conv_w"], 4, 3 * HW)).astype(BF16)
    first_pair = _pair_sum("pair_sum_first", first_mine, _sibling_exchange("scatter_sibling_first", first_mine))
    dx, g["norm1_g"], first_all = _mm("d_h", dproj, w["w_in"], "nt", tb, D, 2560, epi="rms_bwd",
                                      epi_x=(x, sm["norm1_g"], dx1), comm=("chips", first_pair))
    return loss[0, 0], dx, g, rest_all, first_all


def kernel(x, mem, norm1_g, w_in, conv_w, a_log, dt_bias, gdn_norm_g, sb_q_norm_g, sb_k_norm_g, mem_norm_g, w_mem_kv, mem_q_norm_g, mem_k_norm_g, w_br_gdn, w_br_sb, w_br_mem, w_o, norm2_g, w_up, w_down, loss_target, m_norm1_g, m_w_in, m_conv_w, m_a_log, m_dt_bias, m_gdn_norm_g, m_sb_q_norm_g, m_sb_k_norm_g, m_mem_norm_g, m_w_mem_kv, m_mem_q_norm_g, m_mem_k_norm_g, m_w_br_gdn, m_w_br_sb, m_w_br_mem, m_w_o, m_norm2_g, m_w_up, m_w_down, v_norm1_g, v_w_in, v_conv_w, v_a_log, v_dt_bias, v_gdn_norm_g, v_sb_q_norm_g, v_sb_k_norm_g, v_mem_norm_g, v_w_mem_kv, v_mem_q_norm_g, v_mem_k_norm_g, v_w_br_gdn, v_w_br_sb, v_w_br_mem, v_w_o, v_norm2_g, v_w_up, v_w_down):
    given = dict(norm1_g=norm1_g, w_in=w_in, conv_w=conv_w, a_log=a_log, dt_bias=dt_bias, gdn_norm_g=gdn_norm_g,
                 sb_q_norm_g=sb_q_norm_g, sb_k_norm_g=sb_k_norm_g, mem_norm_g=mem_norm_g, w_mem_kv=w_mem_kv,
                 mem_q_norm_g=mem_q_norm_g, mem_k_norm_g=mem_k_norm_g, w_br_gdn=w_br_gdn, w_br_sb=w_br_sb,
                 w_br_mem=w_br_mem, w_o=w_o, norm2_g=norm2_g, w_up=w_up, w_down=w_down)
    mom1 = dict(norm1_g=m_norm1_g, w_in=m_w_in, conv_w=m_conv_w, a_log=m_a_log, dt_bias=m_dt_bias,
                gdn_norm_g=m_gdn_norm_g, sb_q_norm_g=m_sb_q_norm_g, sb_k_norm_g=m_sb_k_norm_g,
                mem_norm_g=m_mem_norm_g, w_mem_kv=m_w_mem_kv, mem_q_norm_g=m_mem_q_norm_g,
                mem_k_norm_g=m_mem_k_norm_g, w_br_gdn=m_w_br_gdn, w_br_sb=m_w_br_sb, w_br_mem=m_w_br_mem, w_o=m_w_o,
                norm2_g=m_norm2_g, w_up=m_w_up, w_down=m_w_down)
    mom2 = dict(norm1_g=v_norm1_g, w_in=v_w_in, conv_w=v_conv_w, a_log=v_a_log, dt_bias=v_dt_bias,
                gdn_norm_g=v_gdn_norm_g, sb_q_norm_g=v_sb_q_norm_g, sb_k_norm_g=v_sb_k_norm_g,
                mem_norm_g=v_mem_norm_g, w_mem_kv=v_w_mem_kv, mem_q_norm_g=v_mem_q_norm_g,
                mem_k_norm_g=v_mem_k_norm_g, w_br_gdn=v_w_br_gdn, w_br_sb=v_w_br_sb, w_br_mem=v_w_br_mem, w_o=v_w_o,
                norm2_g=v_norm2_g, w_up=v_w_up, w_down=v_w_down)
    shapes = {n: given[n].shape for n in WEIGHTS}

    first_loc = _first_slab(given["w_in"][0], given["conv_w"][0])
    rest_loc = _pack_shards(given, REST, R_REST)
    gathered = _gather("gather_first", first_loc.astype(BF16))
    w = {"w_in": _pad_w_in(_first_unslab(gathered)[0].transpose(1, 0, 2).reshape(D, D_IN))}
    conv_loc = jnp.pad(given["conv_w"][0].reshape(-1, LANES), ((0, 2), (0, 0)))
    conv_all = _gather("gather_conv", conv_loc)
    w["conv_w"] = conv_all[:, :6].reshape(NDEV, 4, 3 * HW // NDEV).transpose(1, 0, 2).reshape(4, 3 * HW)
    sm = {n: given[n] for n in SMALL}

    loss, dx, g, rest_all, first_all = _local_step(x[0], mem[0], loss_target[0], w, sm, rest_loc.astype(BF16))
    res_first = _adamw("adamw_first", first_all, first_loc, _first_slab(mom1["w_in"][0], mom1["conv_w"][0]),
                       _first_slab(mom2["w_in"][0], mom2["conv_w"][0]))
    res_rest = _adamw("adamw_rest", rest_all, rest_loc, _pack_shards(mom1, REST, R_REST),
                      _pack_shards(mom2, REST, R_REST))
    gs_all = _gather("gather_small_grads", _pack_vec(g))
    vec_outs = _adamw_vec(gs_all, _pack_vec(given), _pack_vec(mom1), _pack_vec(mom2))

    outs = {}
    for r, prefix in enumerate(("grad_", "delta_", "new_m_", "new_v_")):
        vals = {n: v.reshape(shapes[n]) for n, v in zip(FIRST, _first_unslab(res_first[r]))}
        vals.update(_unpack_shard(res_rest[r], REST, shapes))
        vals.update(_unpack_vec(vec_outs, r))
        for n in WEIGHTS:
            outs[prefix + n] = vals[n]
    loss = lax.psum(loss, ("x", "y", "c"))
    return (loss, dx[None], *[outs[p + n] for p in ("grad_", "delta_", "new_m_", "new_v_") for n in WEIGHTS])
```
